```python
import math
import jax, jax.numpy as jnp
from jax import lax
import numpy as np

D_MODEL = 2048
BATCH = 8
SEQ = 4096
DEPTH = 1

N_MEM = 256
NORM_EPS = 1e-6

SSD_WIDTH = D_MODEL // 2
SSD_HEAD_DIM = 64
SSD_HEADS = SSD_WIDTH // SSD_HEAD_DIM
SSD_GROUPS = 2
SSD_HEADS_PER_GROUP = SSD_HEADS // SSD_GROUPS
SSD_STATE = 128
SSD_CONV = 4
SSD_CHUNK = 128
SSD_CONV_DIM = SSD_WIDTH + 2 * SSD_GROUPS * SSD_STATE
SSD_IN = SSD_WIDTH + SSD_CONV_DIM + SSD_HEADS

RWKV_WIDTH = D_MODEL - SSD_WIDTH
RWKV_HEAD_DIM = 64
RWKV_HEADS = RWKV_WIDTH // RWKV_HEAD_DIM
RWKV_DECAY_RANK = 96
RWKV_AAA_RANK = 96
RWKV_GATE_RANK = 256
RWKV_IN = 3 * RWKV_WIDTH + RWKV_DECAY_RANK + RWKV_AAA_RANK + RWKV_GATE_RANK
RWKV_LN_EPS = 64e-5

MIX_WIDTH = SSD_WIDTH + RWKV_WIDTH
D_IN = SSD_IN + RWKV_IN

XATTN_HEADS = 4
XATTN_HEAD_DIM = D_MODEL // XATTN_HEADS

D_FF = 4 * D_MODEL

kernel_name = "hymba_ssd_rwkv7_memxattn_block"


def rms_norm(x, g, eps=NORM_EPS):
    xf = x.astype(jnp.float32)
    y = xf * lax.rsqrt(jnp.mean(xf * xf, axis=-1, keepdims=True) + eps)
    return (y * g).astype(x.dtype)


def causal_depthwise_conv(u, w, b):
    y = lax.conv_general_dilated(
        u, w, window_strides=(1,), padding=[(w.shape[0] - 1, 0)],
        dimension_numbers=("NWC", "WIO", "NWC"), feature_group_count=u.shape[-1])
    return y + b


def ssd_mixer(u, conv_w, conv_b, dt_bias, a_log, d_skip, norm_g):
    f32 = jnp.float32
    bsz, seq, _ = u.shape
    G, E, P, N, Q = SSD_GROUPS, SSD_HEADS_PER_GROUP, SSD_HEAD_DIM, SSD_STATE, SSD_CHUNK
    nc = seq // Q
    z, xbc, dt = jnp.split(u, [SSD_WIDTH, SSD_WIDTH + SSD_CONV_DIM], axis=-1)
    xbc = jax.nn.silu(causal_depthwise_conv(xbc, conv_w, conv_b))
    xs, bm, cm = jnp.split(xbc, [SSD_WIDTH, SSD_WIDTH + G * N], axis=-1)
    xs = xs.astype(f32).reshape(bsz, nc, Q, G, E, P)
    bm = bm.astype(f32).reshape(bsz, nc, Q, G, N)
    cm = cm.astype(f32).reshape(bsz, nc, Q, G, N)
    dt = jax.nn.softplus(dt.astype(f32) + dt_bias.astype(f32))
    a = -jnp.exp(a_log.astype(f32))
    dt_c = dt.reshape(bsz, nc, Q, G, E)
    xdt = xs * dt_c[..., None]
    da = jnp.transpose(dt_c * a.reshape(G, E), (0, 1, 3, 4, 2))
    cs = jnp.cumsum(da, axis=-1)
    causal = jnp.tril(jnp.ones((Q, Q), dtype=bool))
    seg = cs[..., :, None] - cs[..., None, :]
    lmat = jnp.where(causal, jnp.exp(jnp.where(causal, seg, 0.0)), 0.0)
    cb = jnp.einsum('bclgn,bcsgn->bcgls', cm, bm)
    y_diag = jnp.einsum('bcgls,bcgels,bcsgep->bclgep', cb, lmat, xdt)
    decay_to_end = jnp.exp(cs[..., -1:] - cs)
    chunk_states = jnp.einsum('bcsgn,bcges,bcsgep->bcgepn', bm, decay_to_end, xdt)
    chunk_decay = jnp.exp(cs[..., -1])

    def carry_state(h, inp):
        st, dec = inp
        return h * dec[..., None, None] + st, h

    h0 = jnp.zeros((bsz, G, E, P, N), f32)
    _, start_states = lax.scan(carry_state, h0,
                               (jnp.moveaxis(chunk_states, 1, 0), jnp.moveaxis(chunk_decay, 1, 0)))
    start_states = jnp.moveaxis(start_states, 0, 1)
    y_off = jnp.einsum('bclgn,bcgepn,bcgel->bclgep', cm, start_states, jnp.exp(cs))
    y = y_diag + y_off + xs * d_skip.astype(f32).reshape(G, E, 1)
    y = y.reshape(bsz, seq, SSD_WIDTH) * jax.nn.silu(z.astype(f32))
    yg = y.reshape(bsz, seq, G, SSD_WIDTH // G)
    yg = yg * lax.rsqrt(jnp.mean(yg * yg, axis=-1, keepdims=True) + NORM_EPS)
    y = yg.reshape(bsz, seq, SSD_WIDTH) * norm_g
    return y.astype(u.dtype)


def rwkv7_mixer(u, mu, w0, w2, a0, a2, g2, k_k, k_a, r_k, ln_w, ln_b):
    f32 = jnp.float32
    bsz, seq, _ = u.shape
    H, N, W = RWKV_HEADS, RWKV_HEAD_DIM, RWKV_WIDTH
    uf = u.astype(f32)
    u_prev = jnp.pad(uf, ((0, 0), (1, 0), (0, 0)))[:, :-1]
    uf = uf + (u_prev - uf) * mu
    r, k, v, pw, pa, pg = jnp.split(
        uf, [W, 2 * W, 3 * W, 3 * W + RWKV_DECAY_RANK,
             3 * W + RWKV_DECAY_RANK + RWKV_AAA_RANK], axis=-1)
    w_log = -jax.nn.softplus(-(w0 + jnp.tanh(pw) @ w2)) - 0.5
    decay = jnp.exp(-jnp.exp(w_log))
    iclr = jax.nn.sigmoid(a0 + pa @ a2)
    gate = jax.nn.sigmoid(pg) @ g2
    heads = lambda t: t.reshape(bsz, seq, H, N)
    kk = heads(k * k_k)
    kk = kk / jnp.maximum(jnp.sqrt(jnp.sum(kk * kk, axis=-1, keepdims=True)), 1e-12)
    k = k * (1.0 + (iclr - 1.0) * k_a)
    r, k, v, decay, iclr = heads(r), heads(k), heads(v), heads(decay), heads(iclr)

    def step(state, inp):
        r_t, w_t, k_t, v_t, kk_t, a_t = inp
        sa = jnp.einsum('bhij,bhj->bhi', state, -kk_t)
        state = (state * w_t[:, :, None, :]
                 + sa[..., None] * (kk_t * a_t)[:, :, None, :]
                 + v_t[..., None] * k_t[:, :, None, :])
        return state, jnp.einsum('bhij,bhj->bhi', state, r_t)

    seq_first = lambda t: jnp.moveaxis(t, 1, 0)
    s0 = jnp.zeros((bsz, H, N, N), f32)
    _, y = lax.scan(step, s0, (seq_first(r), seq_first(decay), seq_first(k),
                               seq_first(v), seq_first(kk), seq_first(iclr)))
    y = jnp.moveaxis(y, 0, 1)
    mean = jnp.mean(y, axis=-1, keepdims=True)
    var = jnp.mean(jnp.square(y - mean), axis=-1, keepdims=True)
    y = ((y - mean) * lax.rsqrt(var + RWKV_LN_EPS)).reshape(bsz, seq, W) * ln_w + ln_b
    bonus = jnp.sum(r * k * r_k, axis=-1, keepdims=True) * v
    y = (y + bonus.reshape(bsz, seq, W)) * gate
    return y.astype(u.dtype)


def memory_cross_attention(h, m, wq, wk, wv, wo):
    bsz, seq, _ = h.shape
    q = (h @ wq).reshape(bsz, seq, XATTN_HEADS, XATTN_HEAD_DIM)
    k = (m @ wk).reshape(bsz, m.shape[1], XATTN_HEADS, XATTN_HEAD_DIM)
    v = (m @ wv).reshape(bsz, m.shape[1], XATTN_HEADS, XATTN_HEAD_DIM)
    scores = jnp.einsum('bshd,bmhd->bhsm', q, k).astype(jnp.float32) * (XATTN_HEAD_DIM ** -0.5)
    p = jax.nn.softmax(scores, axis=-1).astype(v.dtype)
    o = jnp.einsum('bhsm,bmhd->bshd', p, v).reshape(bsz, seq, D_MODEL)
    return o @ wo


def _fwd_setup_inputs(seed: int = 0) -> dict:
    key = jax.random.key(seed)
    ks = iter(jax.random.split(key, 40))
    nrm = lambda shape, scale: jax.random.normal(next(ks), shape, jnp.float32) * scale
    uni = lambda shape, lo, hi: jax.random.uniform(next(ks), shape, jnp.float32, lo, hi)
    L = DEPTH
    dt0 = jnp.exp(uni((L, SSD_HEADS), math.log(1e-3), math.log(1e-1)))
    return {
        "x": nrm((BATCH, SEQ, D_MODEL), 1.0),
        "mem": nrm((BATCH, N_MEM, D_MODEL), 1.0),
        "norm_mix_g": 1.0 + nrm((L, D_MODEL), 0.02),
        "w_in": nrm((L, D_MODEL, D_IN), D_MODEL ** -0.5),
        "ssd_conv_w": nrm((L, SSD_CONV, 1, SSD_CONV_DIM), SSD_CONV ** -0.5),
        "ssd_conv_b": nrm((L, SSD_CONV_DIM), 0.01),
        "ssd_dt_bias": dt0 + jnp.log(-jnp.expm1(-dt0)),
        "ssd_a_log": jnp.log(uni((L, SSD_HEADS), 1.0, 16.0)),
        "ssd_d": 1.0 + nrm((L, SSD_HEADS), 0.1),
        "ssd_norm_g": 1.0 + nrm((L, SSD_WIDTH), 0.02),
        "rwkv_mu": uni((L, RWKV_IN), 0.0, 1.0),
        "rwkv_w0": uni((L, RWKV_WIDTH), -6.0, -1.0),
        "rwkv_w2": nrm((L, RWKV_DECAY_RANK, RWKV_WIDTH), 0.5 * RWKV_DECAY_RANK ** -0.5),
        "rwkv_a0": nrm((L, RWKV_WIDTH), 0.1),
        "rwkv_a2": nrm((L, RWKV_AAA_RANK, RWKV_WIDTH), RWKV_AAA_RANK ** -0.5),
        "rwkv_g2": nrm((L, RWKV_GATE_RANK, RWKV_WIDTH), RWKV_GATE_RANK ** -0.5),
        "rwkv_k_k": 0.85 + nrm((L, RWKV_WIDTH), 0.05),
        "rwkv_k_a": 1.0 + nrm((L, RWKV_WIDTH), 0.05),
        "rwkv_r_k": nrm((L, RWKV_HEADS, RWKV_HEAD_DIM), 0.1),
        "rwkv_ln_w": 1.0 + nrm((L, RWKV_WIDTH), 0.02),
        "rwkv_ln_b": nrm((L, RWKV_WIDTH), 0.01),
        "w_out": nrm((L, MIX_WIDTH, D_MODEL), MIX_WIDTH ** -0.5),
        "norm_x_g": 1.0 + nrm((L, D_MODEL), 0.02),
        "norm_mem_g": 1.0 + nrm((L, D_MODEL), 0.02),
        "xattn_wq": nrm((L, D_MODEL, D_MODEL), D_MODEL ** -0.5),
        "xattn_wk": nrm((L, D_MODEL, D_MODEL), D_MODEL ** -0.5),
        "xattn_wv": nrm((L, D_MODEL, D_MODEL), D_MODEL ** -0.5),
        "xattn_wo": nrm((L, D_MODEL, D_MODEL), D_MODEL ** -0.5),
        "norm_ffn_g": 1.0 + nrm((L, D_MODEL), 0.02),
        "ffn_w1": nrm((L, D_MODEL, D_FF), D_MODEL ** -0.5),
        "ffn_w2": nrm((L, D_FF, D_MODEL), D_FF ** -0.5),
        "final_norm_g": 1.0 + nrm((D_MODEL,), 0.02),
    }


def _fwd_reference(x, mem, norm_mix_g, w_in, ssd_conv_w, ssd_conv_b, ssd_dt_bias, ssd_a_log,
              ssd_d, ssd_norm_g, rwkv_mu, rwkv_w0, rwkv_w2, rwkv_a0, rwkv_a2, rwkv_g2,
              rwkv_k_k, rwkv_k_a, rwkv_r_k, rwkv_ln_w, rwkv_ln_b, w_out, norm_x_g,
              norm_mem_g, xattn_wq, xattn_wk, xattn_wv, xattn_wo, norm_ffn_g, ffn_w1,
              ffn_w2, final_norm_g):
    for l in range(DEPTH):
        h = rms_norm(x, norm_mix_g[l])
        u = h @ w_in[l]
        y_ssd = ssd_mixer(u[..., :SSD_IN], ssd_conv_w[l], ssd_conv_b[l], ssd_dt_bias[l],
                          ssd_a_log[l], ssd_d[l], ssd_norm_g[l])
        y_rwkv = rwkv7_mixer(u[..., SSD_IN:], rwkv_mu[l], rwkv_w0[l], rwkv_w2[l], rwkv_a0[l],
                             rwkv_a2[l], rwkv_g2[l], rwkv_k_k[l], rwkv_k_a[l], rwkv_r_k[l],
                             rwkv_ln_w[l], rwkv_ln_b[l])
        x = x + jnp.concatenate([y_ssd, y_rwkv], axis=-1) @ w_out[l]
        h = rms_norm(x, norm_x_g[l])
        m = rms_norm(mem, norm_mem_g[l])
        x = x + memory_cross_attention(h, m, xattn_wq[l], xattn_wk[l], xattn_wv[l], xattn_wo[l])
        h = rms_norm(x, norm_ffn_g[l])
        x = x + jnp.square(jax.nn.relu(h @ ffn_w1[l])) @ ffn_w2[l]
    return rms_norm(x, final_norm_g)


import jax as _jax
import jax.numpy as _jnp

TWIN_FORMAT = 'train_step'
FWD_PARAMS = ['x', 'mem', 'norm_mix_g', 'w_in', 'ssd_conv_w', 'ssd_conv_b', 'ssd_dt_bias', 'ssd_a_log', 'ssd_d', 'ssd_norm_g', 'rwkv_mu', 'rwkv_w0', 'rwkv_w2', 'rwkv_a0', 'rwkv_a2', 'rwkv_g2', 'rwkv_k_k', 'rwkv_k_a', 'rwkv_r_k', 'rwkv_ln_w', 'rwkv_ln_b', 'w_out', 'norm_x_g', 'norm_mem_g', 'xattn_wq', 'xattn_wk', 'xattn_wv', 'xattn_wo', 'norm_ffn_g', 'ffn_w1', 'ffn_w2', 'final_norm_g']
TWIN_WEIGHTS = ['norm_mix_g', 'w_in', 'ssd_conv_w', 'ssd_conv_b', 'ssd_dt_bias', 'ssd_a_log', 'ssd_d', 'ssd_norm_g', 'rwkv_mu', 'rwkv_w0', 'rwkv_w2', 'rwkv_a0', 'rwkv_a2', 'rwkv_g2', 'rwkv_k_k', 'rwkv_k_a', 'rwkv_r_k', 'rwkv_ln_w', 'rwkv_ln_b', 'w_out', 'norm_x_g', 'norm_mem_g', 'xattn_wq', 'xattn_wk', 'xattn_wv', 'xattn_wo', 'norm_ffn_g', 'ffn_w1', 'ffn_w2', 'final_norm_g']
TWIN_DIFF_INPUT = 'x'
TWIN_INPUTS = ['x', 'mem', 'norm_mix_g', 'w_in', 'ssd_conv_w', 'ssd_conv_b', 'ssd_dt_bias', 'ssd_a_log', 'ssd_d', 'ssd_norm_g', 'rwkv_mu', 'rwkv_w0', 'rwkv_w2', 'rwkv_a0', 'rwkv_a2', 'rwkv_g2', 'rwkv_k_k', 'rwkv_k_a', 'rwkv_r_k', 'rwkv_ln_w', 'rwkv_ln_b', 'w_out', 'norm_x_g', 'norm_mem_g', 'xattn_wq', 'xattn_wk', 'xattn_wv', 'xattn_wo', 'norm_ffn_g', 'ffn_w1', 'ffn_w2', 'final_norm_g', 'loss_target', 'm_norm_mix_g', 'm_w_in', 'm_ssd_conv_w', 'm_ssd_conv_b', 'm_ssd_dt_bias', 'm_ssd_a_log', 'm_ssd_d', 'm_ssd_norm_g', 'm_rwkv_mu', 'm_rwkv_w0', 'm_rwkv_w2', 'm_rwkv_a0', 'm_rwkv_a2', 'm_rwkv_g2', 'm_rwkv_k_k', 'm_rwkv_k_a', 'm_rwkv_r_k', 'm_rwkv_ln_w', 'm_rwkv_ln_b', 'm_w_out', 'm_norm_x_g', 'm_norm_mem_g', 'm_xattn_wq', 'm_xattn_wk', 'm_xattn_wv', 'm_xattn_wo', 'm_norm_ffn_g', 'm_ffn_w1', 'm_ffn_w2', 'm_final_norm_g', 'v_norm_mix_g', 'v_w_in', 'v_ssd_conv_w', 'v_ssd_conv_b', 'v_ssd_dt_bias', 'v_ssd_a_log', 'v_ssd_d', 'v_ssd_norm_g', 'v_rwkv_mu', 'v_rwkv_w0', 'v_rwkv_w2', 'v_rwkv_a0', 'v_rwkv_a2', 'v_rwkv_g2', 'v_rwkv_k_k', 'v_rwkv_k_a', 'v_rwkv_r_k', 'v_rwkv_ln_w', 'v_rwkv_ln_b', 'v_w_out', 'v_norm_x_g', 'v_norm_mem_g', 'v_xattn_wq', 'v_xattn_wk', 'v_xattn_wv', 'v_xattn_wo', 'v_norm_ffn_g', 'v_ffn_w1', 'v_ffn_w2', 'v_final_norm_g']
TWIN_OUTPUTS = ['loss', 'grad_x', 'grad_norm_mix_g', 'grad_w_in', 'grad_ssd_conv_w', 'grad_ssd_conv_b', 'grad_ssd_dt_bias', 'grad_ssd_a_log', 'grad_ssd_d', 'grad_ssd_norm_g', 'grad_rwkv_mu', 'grad_rwkv_w0', 'grad_rwkv_w2', 'grad_rwkv_a0', 'grad_rwkv_a2', 'grad_rwkv_g2', 'grad_rwkv_k_k', 'grad_rwkv_k_a', 'grad_rwkv_r_k', 'grad_rwkv_ln_w', 'grad_rwkv_ln_b', 'grad_w_out', 'grad_norm_x_g', 'grad_norm_mem_g', 'grad_xattn_wq', 'grad_xattn_wk', 'grad_xattn_wv', 'grad_xattn_wo', 'grad_norm_ffn_g', 'grad_ffn_w1', 'grad_ffn_w2', 'grad_final_norm_g', 'delta_norm_mix_g', 'delta_w_in', 'delta_ssd_conv_w', 'delta_ssd_conv_b', 'delta_ssd_dt_bias', 'delta_ssd_a_log', 'delta_ssd_d', 'delta_ssd_norm_g', 'delta_rwkv_mu', 'delta_rwkv_w0', 'delta_rwkv_w2', 'delta_rwkv_a0', 'delta_rwkv_a2', 'delta_rwkv_g2', 'delta_rwkv_k_k', 'delta_rwkv_k_a', 'delta_rwkv_r_k', 'delta_rwkv_ln_w', 'delta_rwkv_ln_b', 'delta_w_out', 'delta_norm_x_g', 'delta_norm_mem_g', 'delta_xattn_wq', 'delta_xattn_wk', 'delta_xattn_wv', 'delta_xattn_wo', 'delta_norm_ffn_g', 'delta_ffn_w1', 'delta_ffn_w2', 'delta_final_norm_g', 'new_m_norm_mix_g', 'new_m_w_in', 'new_m_ssd_conv_w', 'new_m_ssd_conv_b', 'new_m_ssd_dt_bias', 'new_m_ssd_a_log', 'new_m_ssd_d', 'new_m_ssd_norm_g', 'new_m_rwkv_mu', 'new_m_rwkv_w0', 'new_m_rwkv_w2', 'new_m_rwkv_a0', 'new_m_rwkv_a2', 'new_m_rwkv_g2', 'new_m_rwkv_k_k', 'new_m_rwkv_k_a', 'new_m_rwkv_r_k', 'new_m_rwkv_ln_w', 'new_m_rwkv_ln_b', 'new_m_w_out', 'new_m_norm_x_g', 'new_m_norm_mem_g', 'new_m_xattn_wq', 'new_m_xattn_wk', 'new_m_xattn_wv', 'new_m_xattn_wo', 'new_m_norm_ffn_g', 'new_m_ffn_w1', 'new_m_ffn_w2', 'new_m_final_norm_g', 'new_v_norm_mix_g', 'new_v_w_in', 'new_v_ssd_conv_w', 'new_v_ssd_conv_b', 'new_v_ssd_dt_bias', 'new_v_ssd_a_log', 'new_v_ssd_d', 'new_v_ssd_norm_g', 'new_v_rwkv_mu', 'new_v_rwkv_w0', 'new_v_rwkv_w2', 'new_v_rwkv_a0', 'new_v_rwkv_a2', 'new_v_rwkv_g2', 'new_v_rwkv_k_k', 'new_v_rwkv_k_a', 'new_v_rwkv_r_k', 'new_v_rwkv_ln_w', 'new_v_rwkv_ln_b', 'new_v_w_out', 'new_v_norm_x_g', 'new_v_norm_mem_g', 'new_v_xattn_wq', 'new_v_xattn_wk', 'new_v_xattn_wv', 'new_v_xattn_wo', 'new_v_norm_ffn_g', 'new_v_ffn_w1', 'new_v_ffn_w2', 'new_v_final_norm_g']
TWIN_LEAF_KINDS = {'loss': 'loss', 'grad_x': 'grad_x', 'grad_norm_mix_g': 'grad_w', 'grad_w_in': 'grad_w', 'grad_ssd_conv_w': 'grad_w', 'grad_ssd_conv_b': 'grad_w', 'grad_ssd_dt_bias': 'grad_w', 'grad_ssd_a_log': 'grad_w', 'grad_ssd_d': 'grad_w', 'grad_ssd_norm_g': 'grad_w', 'grad_rwkv_mu': 'grad_w', 'grad_rwkv_w0': 'grad_w', 'grad_rwkv_w2': 'grad_w', 'grad_rwkv_a0': 'grad_w', 'grad_rwkv_a2': 'grad_w', 'grad_rwkv_g2': 'grad_w', 'grad_rwkv_k_k': 'grad_w', 'grad_rwkv_k_a': 'grad_w', 'grad_rwkv_r_k': 'grad_w', 'grad_rwkv_ln_w': 'grad_w', 'grad_rwkv_ln_b': 'grad_w', 'grad_w_out': 'grad_w', 'grad_norm_x_g': 'grad_w', 'grad_norm_mem_g': 'grad_w', 'grad_xattn_wq': 'grad_w', 'grad_xattn_wk': 'grad_w', 'grad_xattn_wv': 'grad_w', 'grad_xattn_wo': 'grad_w', 'grad_norm_ffn_g': 'grad_w', 'grad_ffn_w1': 'grad_w', 'grad_ffn_w2': 'grad_w', 'grad_final_norm_g': 'grad_w', 'delta_norm_mix_g': 'delta_w', 'delta_w_in': 'delta_w', 'delta_ssd_conv_w': 'delta_w', 'delta_ssd_conv_b': 'delta_w', 'delta_ssd_dt_bias': 'delta_w', 'delta_ssd_a_log': 'delta_w', 'delta_ssd_d': 'delta_w', 'delta_ssd_norm_g': 'delta_w', 'delta_rwkv_mu': 'delta_w', 'delta_rwkv_w0': 'delta_w', 'delta_rwkv_w2': 'delta_w', 'delta_rwkv_a0': 'delta_w', 'delta_rwkv_a2': 'delta_w', 'delta_rwkv_g2': 'delta_w', 'delta_rwkv_k_k': 'delta_w', 'delta_rwkv_k_a': 'delta_w', 'delta_rwkv_r_k': 'delta_w', 'delta_rwkv_ln_w': 'delta_w', 'delta_rwkv_ln_b': 'delta_w', 'delta_w_out': 'delta_w', 'delta_norm_x_g': 'delta_w', 'delta_norm_mem_g': 'delta_w', 'delta_xattn_wq': 'delta_w', 'delta_xattn_wk': 'delta_w', 'delta_xattn_wv': 'delta_w', 'delta_xattn_wo': 'delta_w', 'delta_norm_ffn_g': 'delta_w', 'delta_ffn_w1': 'delta_w', 'delta_ffn_w2': 'delta_w', 'delta_final_norm_g': 'delta_w', 'new_m_norm_mix_g': 'new_m', 'new_m_w_in': 'new_m', 'new_m_ssd_conv_w': 'new_m', 'new_m_ssd_conv_b': 'new_m', 'new_m_ssd_dt_bias': 'new_m', 'new_m_ssd_a_log': 'new_m', 'new_m_ssd_d': 'new_m', 'new_m_ssd_norm_g': 'new_m', 'new_m_rwkv_mu': 'new_m', 'new_m_rwkv_w0': 'new_m', 'new_m_rwkv_w2': 'new_m', 'new_m_rwkv_a0': 'new_m', 'new_m_rwkv_a2': 'new_m', 'new_m_rwkv_g2': 'new_m', 'new_m_rwkv_k_k': 'new_m', 'new_m_rwkv_k_a': 'new_m', 'new_m_rwkv_r_k': 'new_m', 'new_m_rwkv_ln_w': 'new_m', 'new_m_rwkv_ln_b': 'new_m', 'new_m_w_out': 'new_m', 'new_m_norm_x_g': 'new_m', 'new_m_norm_mem_g': 'new_m', 'new_m_xattn_wq': 'new_m', 'new_m_xattn_wk': 'new_m', 'new_m_xattn_wv': 'new_m', 'new_m_xattn_wo': 'new_m', 'new_m_norm_ffn_g': 'new_m', 'new_m_ffn_w1': 'new_m', 'new_m_ffn_w2': 'new_m', 'new_m_final_norm_g': 'new_m', 'new_v_norm_mix_g': 'new_v', 'new_v_w_in': 'new_v', 'new_v_ssd_conv_w': 'new_v', 'new_v_ssd_conv_b': 'new_v', 'new_v_ssd_dt_bias': 'new_v', 'new_v_ssd_a_log': 'new_v', 'new_v_ssd_d': 'new_v', 'new_v_ssd_norm_g': 'new_v', 'new_v_rwkv_mu': 'new_v', 'new_v_rwkv_w0': 'new_v', 'new_v_rwkv_w2': 'new_v', 'new_v_rwkv_a0': 'new_v', 'new_v_rwkv_a2': 'new_v', 'new_v_rwkv_g2': 'new_v', 'new_v_rwkv_k_k': 'new_v', 'new_v_rwkv_k_a': 'new_v', 'new_v_rwkv_r_k': 'new_v', 'new_v_rwkv_ln_w': 'new_v', 'new_v_rwkv_ln_b': 'new_v', 'new_v_w_out': 'new_v', 'new_v_norm_x_g': 'new_v', 'new_v_norm_mem_g': 'new_v', 'new_v_xattn_wq': 'new_v', 'new_v_xattn_wk': 'new_v', 'new_v_xattn_wv': 'new_v', 'new_v_xattn_wo': 'new_v', 'new_v_norm_ffn_g': 'new_v', 'new_v_ffn_w1': 'new_v', 'new_v_ffn_w2': 'new_v', 'new_v_final_norm_g': 'new_v'}


def _forward(args):
    return _fwd_reference(*[args[k] for k in FWD_PARAMS])


def _output_shape():
    def fwd():
        inp = _fwd_setup_inputs(0)
        return _fwd_reference(*[inp[k] for k in FWD_PARAMS])
    out = _jax.eval_shape(fwd)
    return out.shape, out.dtype

N_MICROBATCH = 1
ADAM_LR = 0.001
ADAM_B1 = 0.9
ADAM_B2 = 0.999
ADAM_EPS = 1e-08
ADAM_WD = 0.01
ADAM_STEP = 10
PER_EXAMPLE_BATCH_AXIS = {'x': 0, 'mem': 0, 'loss_target': 0}
SHARED_INPUTS = []
_WEIGHT_DTYPES = {'norm_mix_g': _jnp.float32, 'w_in': _jnp.float32, 'ssd_conv_w': _jnp.float32, 'ssd_conv_b': _jnp.float32, 'ssd_dt_bias': _jnp.float32, 'ssd_a_log': _jnp.float32, 'ssd_d': _jnp.float32, 'ssd_norm_g': _jnp.float32, 'rwkv_mu': _jnp.float32, 'rwkv_w0': _jnp.float32, 'rwkv_w2': _jnp.float32, 'rwkv_a0': _jnp.float32, 'rwkv_a2': _jnp.float32, 'rwkv_g2': _jnp.float32, 'rwkv_k_k': _jnp.float32, 'rwkv_k_a': _jnp.float32, 'rwkv_r_k': _jnp.float32, 'rwkv_ln_w': _jnp.float32, 'rwkv_ln_b': _jnp.float32, 'w_out': _jnp.float32, 'norm_x_g': _jnp.float32, 'norm_mem_g': _jnp.float32, 'xattn_wq': _jnp.float32, 'xattn_wk': _jnp.float32, 'xattn_wv': _jnp.float32, 'xattn_wo': _jnp.float32, 'norm_ffn_g': _jnp.float32, 'ffn_w1': _jnp.float32, 'ffn_w2': _jnp.float32, 'final_norm_g': _jnp.float32}
MOMENT_SCALE = {'norm_mix_g': 9.637612e-02, 'w_in': 5.694770e-02, 'ssd_conv_w': 6.492432e-02, 'ssd_conv_b': 9.203472e-02, 'ssd_dt_bias': 1.948267e-01, 'ssd_a_log': 2.045979e-01, 'ssd_d': 5.376066e-01, 'ssd_norm_g': 7.825994e-02, 'rwkv_mu': 7.105764e-02, 'rwkv_w0': 1.685688e-02, 'rwkv_w2': 1.985362e-03, 'rwkv_a0': 1.768519e-02, 'rwkv_a2': 1.538601e-02, 'rwkv_g2': 3.980057e-02, 'rwkv_k_k': 4.241477e-02, 'rwkv_k_a': 4.970380e-02, 'rwkv_r_k': 9.390583e-02, 'rwkv_ln_w': 4.339390e-02, 'rwkv_ln_b': 6.315919e-02, 'w_out': 6.097313e-02, 'norm_x_g': 7.922768e-03, 'norm_mem_g': 1.142359e-02, 'xattn_wq': 7.566994e-03, 'xattn_wk': 7.566733e-03, 'xattn_wv': 7.873047e-03, 'xattn_wo': 7.842228e-03, 'norm_ffn_g': 7.016085e-02, 'ffn_w1': 3.532124e-02, 'ffn_w2': 7.056102e-02, 'final_norm_g': 1.611898e+01}


def _to_microbatches(a, axis):
    t = _jnp.moveaxis(a, axis, 0)
    t = t.reshape((N_MICROBATCH, t.shape[0] // N_MICROBATCH) + t.shape[1:])
    return _jnp.moveaxis(t, 1, axis + 1)


def setup_inputs(seed: int = 0) -> dict:
    inp = _fwd_setup_inputs(seed)
    key = _jax.random.fold_in(_jax.random.key(seed), 7919)
    shape, _ = _output_shape()
    out = dict(inp)
    out["loss_target"] = _jax.random.normal(_jax.random.fold_in(key, 0), shape, _jnp.float32)
    for i, name in enumerate(TWIN_WEIGHTS):
        w = inp[name].astype(_jnp.float32)
        if MOMENT_SCALE is None:
            s = _jnp.sqrt(_jnp.mean(_jnp.square(w)) + 1e-30)
        else:
            s = MOMENT_SCALE[name]
        km, kv = _jax.random.split(_jax.random.fold_in(key, i + 1))
        out[name] = w
        out["m_" + name] = s * _jax.random.normal(km, w.shape, _jnp.float32)
        out["v_" + name] = (s * s) * _jax.random.uniform(kv, w.shape, _jnp.float32, 0.5, 1.5)
    if N_MICROBATCH > 1:
        for name, axis in PER_EXAMPLE_BATCH_AXIS.items():
            out[name] = _to_microbatches(out[name], axis)
    return {'x': out['x'], 'mem': out['mem'], 'norm_mix_g': out['norm_mix_g'], 'w_in': out['w_in'], 'ssd_conv_w': out['ssd_conv_w'], 'ssd_conv_b': out['ssd_conv_b'], 'ssd_dt_bias': out['ssd_dt_bias'], 'ssd_a_log': out['ssd_a_log'], 'ssd_d': out['ssd_d'], 'ssd_norm_g': out['ssd_norm_g'], 'rwkv_mu': out['rwkv_mu'], 'rwkv_w0': out['rwkv_w0'], 'rwkv_w2': out['rwkv_w2'], 'rwkv_a0': out['rwkv_a0'], 'rwkv_a2': out['rwkv_a2'], 'rwkv_g2': out['rwkv_g2'], 'rwkv_k_k': out['rwkv_k_k'], 'rwkv_k_a': out['rwkv_k_a'], 'rwkv_r_k': out['rwkv_r_k'], 'rwkv_ln_w': out['rwkv_ln_w'], 'rwkv_ln_b': out['rwkv_ln_b'], 'w_out': out['w_out'], 'norm_x_g': out['norm_x_g'], 'norm_mem_g': out['norm_mem_g'], 'xattn_wq': out['xattn_wq'], 'xattn_wk': out['xattn_wk'], 'xattn_wv': out['xattn_wv'], 'xattn_wo': out['xattn_wo'], 'norm_ffn_g': out['norm_ffn_g'], 'ffn_w1': out['ffn_w1'], 'ffn_w2': out['ffn_w2'], 'final_norm_g': out['final_norm_g'], 'loss_target': out['loss_target'], 'm_norm_mix_g': out['m_norm_mix_g'], 'm_w_in': out['m_w_in'], 'm_ssd_conv_w': out['m_ssd_conv_w'], 'm_ssd_conv_b': out['m_ssd_conv_b'], 'm_ssd_dt_bias': out['m_ssd_dt_bias'], 'm_ssd_a_log': out['m_ssd_a_log'], 'm_ssd_d': out['m_ssd_d'], 'm_ssd_norm_g': out['m_ssd_norm_g'], 'm_rwkv_mu': out['m_rwkv_mu'], 'm_rwkv_w0': out['m_rwkv_w0'], 'm_rwkv_w2': out['m_rwkv_w2'], 'm_rwkv_a0': out['m_rwkv_a0'], 'm_rwkv_a2': out['m_rwkv_a2'], 'm_rwkv_g2': out['m_rwkv_g2'], 'm_rwkv_k_k': out['m_rwkv_k_k'], 'm_rwkv_k_a': out['m_rwkv_k_a'], 'm_rwkv_r_k': out['m_rwkv_r_k'], 'm_rwkv_ln_w': out['m_rwkv_ln_w'], 'm_rwkv_ln_b': out['m_rwkv_ln_b'], 'm_w_out': out['m_w_out'], 'm_norm_x_g': out['m_norm_x_g'], 'm_norm_mem_g': out['m_norm_mem_g'], 'm_xattn_wq': out['m_xattn_wq'], 'm_xattn_wk': out['m_xattn_wk'], 'm_xattn_wv': out['m_xattn_wv'], 'm_xattn_wo': out['m_xattn_wo'], 'm_norm_ffn_g': out['m_norm_ffn_g'], 'm_ffn_w1': out['m_ffn_w1'], 'm_ffn_w2': out['m_ffn_w2'], 'm_final_norm_g': out['m_final_norm_g'], 'v_norm_mix_g': out['v_norm_mix_g'], 'v_w_in': out['v_w_in'], 'v_ssd_conv_w': out['v_ssd_conv_w'], 'v_ssd_conv_b': out['v_ssd_conv_b'], 'v_ssd_dt_bias': out['v_ssd_dt_bias'], 'v_ssd_a_log': out['v_ssd_a_log'], 'v_ssd_d': out['v_ssd_d'], 'v_ssd_norm_g': out['v_ssd_norm_g'], 'v_rwkv_mu': out['v_rwkv_mu'], 'v_rwkv_w0': out['v_rwkv_w0'], 'v_rwkv_w2': out['v_rwkv_w2'], 'v_rwkv_a0': out['v_rwkv_a0'], 'v_rwkv_a2': out['v_rwkv_a2'], 'v_rwkv_g2': out['v_rwkv_g2'], 'v_rwkv_k_k': out['v_rwkv_k_k'], 'v_rwkv_k_a': out['v_rwkv_k_a'], 'v_rwkv_r_k': out['v_rwkv_r_k'], 'v_rwkv_ln_w': out['v_rwkv_ln_w'], 'v_rwkv_ln_b': out['v_rwkv_ln_b'], 'v_w_out': out['v_w_out'], 'v_norm_x_g': out['v_norm_x_g'], 'v_norm_mem_g': out['v_norm_mem_g'], 'v_xattn_wq': out['v_xattn_wq'], 'v_xattn_wk': out['v_xattn_wk'], 'v_xattn_wv': out['v_xattn_wv'], 'v_xattn_wo': out['v_xattn_wo'], 'v_norm_ffn_g': out['v_norm_ffn_g'], 'v_ffn_w1': out['v_ffn_w1'], 'v_ffn_w2': out['v_ffn_w2'], 'v_final_norm_g': out['v_final_norm_g']}


def _loss(weights, diff, rest, loss_target):
    with _jax.named_scope("forward"):
        args = {**rest, TWIN_DIFF_INPUT: diff, **{k: w.astype(_WEIGHT_DTYPES[k]) for k, w in weights.items()}}
        y = _forward(args)
    with _jax.named_scope("loss_head"):
        err = _jnp.square(y.astype(_jnp.float32) - loss_target)
        return 0.5 * _jnp.sum(_jnp.mean(err, axis=-1)) if err.ndim else 0.5 * err


def _adamw(w, g, m, v):
    m = ADAM_B1 * m + (1.0 - ADAM_B1) * g
    v = ADAM_B2 * v + (1.0 - ADAM_B2) * _jnp.square(g)
    m_hat = m / (1.0 - ADAM_B1 ** ADAM_STEP)
    v_hat = v / (1.0 - ADAM_B2 ** ADAM_STEP)
    delta = -ADAM_LR * (m_hat / (_jnp.sqrt(v_hat) + ADAM_EPS) + ADAM_WD * w)
    return delta, m, v


def reference(x, mem, norm_mix_g, w_in, ssd_conv_w, ssd_conv_b, ssd_dt_bias, ssd_a_log, ssd_d, ssd_norm_g, rwkv_mu, rwkv_w0, rwkv_w2, rwkv_a0, rwkv_a2, rwkv_g2, rwkv_k_k, rwkv_k_a, rwkv_r_k, rwkv_ln_w, rwkv_ln_b, w_out, norm_x_g, norm_mem_g, xattn_wq, xattn_wk, xattn_wv, xattn_wo, norm_ffn_g, ffn_w1, ffn_w2, final_norm_g, loss_target, m_norm_mix_g, m_w_in, m_ssd_conv_w, m_ssd_conv_b, m_ssd_dt_bias, m_ssd_a_log, m_ssd_d, m_ssd_norm_g, m_rwkv_mu, m_rwkv_w0, m_rwkv_w2, m_rwkv_a0, m_rwkv_a2, m_rwkv_g2, m_rwkv_k_k, m_rwkv_k_a, m_rwkv_r_k, m_rwkv_ln_w, m_rwkv_ln_b, m_w_out, m_norm_x_g, m_norm_mem_g, m_xattn_wq, m_xattn_wk, m_xattn_wv, m_xattn_wo, m_norm_ffn_g, m_ffn_w1, m_ffn_w2, m_final_norm_g, v_norm_mix_g, v_w_in, v_ssd_conv_w, v_ssd_conv_b, v_ssd_dt_bias, v_ssd_a_log, v_ssd_d, v_ssd_norm_g, v_rwkv_mu, v_rwkv_w0, v_rwkv_w2, v_rwkv_a0, v_rwkv_a2, v_rwkv_g2, v_rwkv_k_k, v_rwkv_k_a, v_rwkv_r_k, v_rwkv_ln_w, v_rwkv_ln_b, v_w_out, v_norm_x_g, v_norm_mem_g, v_xattn_wq, v_xattn_wk, v_xattn_wv, v_xattn_wo, v_norm_ffn_g, v_ffn_w1, v_ffn_w2, v_final_norm_g):
    given = dict(x=x, mem=mem, norm_mix_g=norm_mix_g, w_in=w_in, ssd_conv_w=ssd_conv_w, ssd_conv_b=ssd_conv_b, ssd_dt_bias=ssd_dt_bias, ssd_a_log=ssd_a_log, ssd_d=ssd_d, ssd_norm_g=ssd_norm_g, rwkv_mu=rwkv_mu, rwkv_w0=rwkv_w0, rwkv_w2=rwkv_w2, rwkv_a0=rwkv_a0, rwkv_a2=rwkv_a2, rwkv_g2=rwkv_g2, rwkv_k_k=rwkv_k_k, rwkv_k_a=rwkv_k_a, rwkv_r_k=rwkv_r_k, rwkv_ln_w=rwkv_ln_w, rwkv_ln_b=rwkv_ln_b, w_out=w_out, norm_x_g=norm_x_g, norm_mem_g=norm_mem_g, xattn_wq=xattn_wq, xattn_wk=xattn_wk, xattn_wv=xattn_wv, xattn_wo=xattn_wo, norm_ffn_g=norm_ffn_g, ffn_w1=ffn_w1, ffn_w2=ffn_w2, final_norm_g=final_norm_g, loss_target=loss_target, m_norm_mix_g=m_norm_mix_g, m_w_in=m_w_in, m_ssd_conv_w=m_ssd_conv_w, m_ssd_conv_b=m_ssd_conv_b, m_ssd_dt_bias=m_ssd_dt_bias, m_ssd_a_log=m_ssd_a_log, m_ssd_d=m_ssd_d, m_ssd_norm_g=m_ssd_norm_g, m_rwkv_mu=m_rwkv_mu, m_rwkv_w0=m_rwkv_w0, m_rwkv_w2=m_rwkv_w2, m_rwkv_a0=m_rwkv_a0, m_rwkv_a2=m_rwkv_a2, m_rwkv_g2=m_rwkv_g2, m_rwkv_k_k=m_rwkv_k_k, m_rwkv_k_a=m_rwkv_k_a, m_rwkv_r_k=m_rwkv_r_k, m_rwkv_ln_w=m_rwkv_ln_w, m_rwkv_ln_b=m_rwkv_ln_b, m_w_out=m_w_out, m_norm_x_g=m_norm_x_g, m_norm_mem_g=m_norm_mem_g, m_xattn_wq=m_xattn_wq, m_xattn_wk=m_xattn_wk, m_xattn_wv=m_xattn_wv, m_xattn_wo=m_xattn_wo, m_norm_ffn_g=m_norm_ffn_g, m_ffn_w1=m_ffn_w1, m_ffn_w2=m_ffn_w2, m_final_norm_g=m_final_norm_g, v_norm_mix_g=v_norm_mix_g, v_w_in=v_w_in, v_ssd_conv_w=v_ssd_conv_w, v_ssd_conv_b=v_ssd_conv_b, v_ssd_dt_bias=v_ssd_dt_bias, v_ssd_a_log=v_ssd_a_log, v_ssd_d=v_ssd_d, v_ssd_norm_g=v_ssd_norm_g, v_rwkv_mu=v_rwkv_mu, v_rwkv_w0=v_rwkv_w0, v_rwkv_w2=v_rwkv_w2, v_rwkv_a0=v_rwkv_a0, v_rwkv_a2=v_rwkv_a2, v_rwkv_g2=v_rwkv_g2, v_rwkv_k_k=v_rwkv_k_k, v_rwkv_k_a=v_rwkv_k_a, v_rwkv_r_k=v_rwkv_r_k, v_rwkv_ln_w=v_rwkv_ln_w, v_rwkv_ln_b=v_rwkv_ln_b, v_w_out=v_w_out, v_norm_x_g=v_norm_x_g, v_norm_mem_g=v_norm_mem_g, v_xattn_wq=v_xattn_wq, v_xattn_wk=v_xattn_wk, v_xattn_wv=v_xattn_wv, v_xattn_wo=v_xattn_wo, v_norm_ffn_g=v_norm_ffn_g, v_ffn_w1=v_ffn_w1, v_ffn_w2=v_ffn_w2, v_final_norm_g=v_final_norm_g)
    weights = {n: given[n] for n in TWIN_WEIGHTS}
    shared = {n: given[n] for n in SHARED_INPUTS}
    per_example = {n: given[n] for n in ['x', 'mem']}
    grad_fn = _jax.value_and_grad(_loss, argnums=(0, 1))

    def one_microbatch(ex, loss_target):
        ex = dict(ex)
        diff = ex.pop(TWIN_DIFF_INPUT)
        return grad_fn(weights, diff, {**shared, **ex}, loss_target)

    if N_MICROBATCH == 1:
        loss, (grad_w, grad_x) = one_microbatch(per_example, given["loss_target"])
    else:
        def body(carry, xs):
            loss_sum, grad_sum = carry
            l_k, (gw_k, gx_k) = one_microbatch(xs[0], xs[1])
            with _jax.named_scope("update"):
                return (loss_sum + l_k, _jax.tree.map(_jnp.add, grad_sum, gw_k)), gx_k

        init = (_jnp.zeros((), _jnp.float32), _jax.tree.map(_jnp.zeros_like, weights))
        (loss, grad_w), grad_x = _jax.lax.scan(body, init, (per_example, given["loss_target"]))
    with _jax.named_scope("update"):
        delta_w, new_m, new_v = {}, {}, {}
        for n in TWIN_WEIGHTS:
            delta_w[n], new_m[n], new_v[n] = _adamw(weights[n], grad_w[n], given["m_" + n], given["v_" + n])
    return (loss, grad_x, *[grad_w[n] for n in TWIN_WEIGHTS], *[delta_w[n] for n in TWIN_WEIGHTS],
            *[new_m[n] for n in TWIN_WEIGHTS], *[new_v[n] for n in TWIN_WEIGHTS])
```

```python
import functools
import math

import jax
import jax.numpy as jnp
from jax import lax
from jax.experimental import pallas as pl
from jax.experimental.pallas import tpu as pltpu

F32 = jnp.float32
BF16 = jnp.bfloat16
HIGHEST = lax.Precision.HIGHEST
MESH_ID = pl.DeviceIdType.MESH

NORM_EPS = 1e-6
RWKV_LN_EPS = 64e-5
HEAD_DIM = 64
PAIR = 2 * HEAD_DIM
LANES = 128
SSD_STATE = 128
SSD_CHUNK = 128
SSD_GROUPS = 2
SSD_CONV = 4
RWKV_CHUNK = 64
HALO = 8
ROW_TILE = 128
XATTN_HEADS = 4
RWKV_PASSES = 1
VMEM_LIMIT = 56 * 1024 * 1024

ADAM_LR = 0.001
ADAM_B1 = 0.9
ADAM_B2 = 0.999
ADAM_EPS = 1e-08
ADAM_WD = 0.01
ADAM_STEP = 10


def _dims(ca, cb):
    return (((ca,), (cb,)), ((), ()))


def _split_bf16(a):
    hi = a.astype(BF16)
    lo = (a - hi.astype(F32)).astype(BF16)
    return hi, lo


def _mm_impl(a, b, ca, cb, passes):
    dn = _dims(ca, cb)
    if passes == 1:
        return lax.dot_general(a.astype(BF16), b.astype(BF16), dn, preferred_element_type=F32)
    ah, al = _split_bf16(a)
    bh, bl = _split_bf16(b)
    out = lax.dot_general(ah, bh, dn, preferred_element_type=F32)
    out = out + lax.dot_general(ah, bl, dn, preferred_element_type=F32)
    return out + lax.dot_general(al, bh, dn, preferred_element_type=F32)


@functools.partial(jax.custom_vjp, nondiff_argnums=(2, 3, 4))
def mm(a, b, ca, cb, passes):
    return _mm_impl(a, b, ca, cb, passes)


def _mm_fwd(a, b, ca, cb, passes):
    return _mm_impl(a, b, ca, cb, passes), (a, b)


def _mm_bwd(ca, cb, passes, res, g):
    a, b = res
    da = mm(g, b, 1, 1 - cb, passes) if ca == 1 else mm(b, g, 1 - cb, 1, passes)
    db = mm(a, g, 1 - ca, 0, passes) if cb == 0 else mm(g, a, 0, 1 - ca, passes)
    return da, db


mm.defvjp(_mm_fwd, _mm_bwd)


def _dot_exact(a, b):
    return lax.dot_general(a, b, _dims(1, 0), precision=HIGHEST, preferred_element_type=F32)


def _iota(shape, dim):
    return lax.broadcasted_iota(jnp.int32, shape, dim)


def _sigmoid(x):
    return 1.0 / (1.0 + jnp.exp(-x))


def _silu(x):
    return x * _sigmoid(x)


def _softplus(x):
    return jnp.maximum(x, 0.0) + jnp.log(1.0 + jnp.exp(-jnp.abs(x)))


def _rms(x, g):
    return x * lax.rsqrt(jnp.mean(x * x, axis=-1, keepdims=True) + NORM_EPS) * g


def _head_sum(x):
    n = x.shape[1]
    sel = (_iota((n, LANES), 0) // HEAD_DIM == _iota((n, LANES), 1)).astype(F32)
    return _dot_exact(x, sel)


def _head_expand(s, n):
    sel = (_iota((LANES, n), 1) // HEAD_DIM == _iota((LANES, n), 0)).astype(F32)
    return _dot_exact(s, sel)


def _row_vector_expand(v, n):
    v8 = jnp.broadcast_to(v, (8, LANES))
    return jnp.sum(_head_expand(v8, n), axis=0, keepdims=True) * 0.125


def _shift_rows(u, halo, s):
    tr = u.shape[0]
    sm = (_iota((tr, tr), 1) == _iota((tr, tr), 0) - s).astype(F32)
    out = _dot_exact(sm, u)
    row = _iota((tr, 1), 0)
    hrow = _iota((HALO, 1), 0)
    for r in range(s):
        src = jnp.sum(jnp.where(hrow == HALO - s + r, halo, 0.0), axis=0, keepdims=True)
        out = out + jnp.where(row == r, src, 0.0)
    return out


def _params(sem):
    return pltpu.CompilerParams(dimension_semantics=sem, vmem_limit_bytes=VMEM_LIMIT)


def row_call(name, body, n_tiles, tiled, full, out_tiled, out_acc):
    nt, nf, no, na = len(tiled), len(full), len(out_tiled), len(out_acc)

    def kern(*refs):
        tv = [r[...] for r in refs[:nt]]
        fv = [r[...] for r in refs[nt:nt + nf]]
        outs, accs = body(tv, fv)
        for r, v in zip(refs[nt + nf:nt + nf + no], outs):
            r[...] = v.astype(r.dtype)
        if na:
            a_refs = refs[nt + nf + no:]
            first = pl.program_id(0) == 0

            @pl.when(first)
            def _():
                for r, v in zip(a_refs, accs):
                    r[...] = v

            @pl.when(jnp.logical_not(first))
            def _():
                for r, v in zip(a_refs, accs):
                    r[...] += v

    in_specs = [pl.BlockSpec((rt, w), functools.partial(lambda i, cb: (i, cb), cb=cb)) for (_, rt, w, cb) in tiled]
    in_specs += [pl.BlockSpec(a.shape, lambda i: (0, 0)) for a in full]
    out_specs = [pl.BlockSpec((rt, w), lambda i: (i, 0)) for (_, rt, w, _) in out_tiled]
    out_specs += [pl.BlockSpec(s, lambda i: (0, 0)) for s in out_acc]
    out_shape = [jax.ShapeDtypeStruct((rows, w), dt) for (rows, _, w, dt) in out_tiled]
    out_shape += [jax.ShapeDtypeStruct(s, F32) for s in out_acc]
    res = pl.pallas_call(
        kern, name=name, grid=(n_tiles,), in_specs=in_specs, out_specs=out_specs, out_shape=out_shape,
        compiler_params=_params(("arbitrary",)),
    )(*[t[0] for t in tiled], *full)
    return list(res[:no]), list(res[no:])


def _pick(dim, cands):
    for c in cands:
        if dim % c == 0:
            return c
    return dim


def matmul(name, a, b, tb=False, resid=None, out_dtype=F32):
    m, k = a.shape
    n = b.shape[0] if tb else b.shape[1]
    tm = _pick(m, (512, 256, 128))
    tn = _pick(n, (1024, 896, 768, 512, 384, 256, 128))
    tk = _pick(k, (1024, 896, 768, 512, 384, 256, 128))
    nk = k // tk
    has_resid = resid is not None

    def kern(*refs):
        a_ref, b_ref = refs[0], refs[1]
        o_ref, acc = refs[-2], refs[-1]
        kk = pl.program_id(2)

        @pl.when(kk == 0)
        def _():
            acc[...] = jnp.zeros_like(acc)

        acc[...] += lax.dot_general(a_ref[...], b_ref[...], _dims(1, 1 if tb else 0), preferred_element_type=F32)

        @pl.when(kk == nk - 1)
        def _():
            out = acc[...]
            if has_resid:
                out = out + refs[2][...]
            o_ref[...] = out.astype(o_ref.dtype)

    in_specs = [pl.BlockSpec((tm, tk), lambda i, j, kk: (i, kk))]
    if tb:
        in_specs.append(pl.BlockSpec((tn, tk), lambda i, j, kk: (j, kk)))
    else:
        in_specs.append(pl.BlockSpec((tk, tn), lambda i, j, kk: (kk, j)))
    args = [a, b]
    if has_resid:
        in_specs.append(pl.BlockSpec((tm, tn), lambda i, j, kk: (i, j)))
        args.append(resid)
    return pl.pallas_call(
        kern, name=name, grid=(m // tm, n // tn, nk), in_specs=in_specs,
        out_specs=pl.BlockSpec((tm, tn), lambda i, j, kk: (i, j)),
        out_shape=jax.ShapeDtypeStruct((m, n), out_dtype),
        scratch_shapes=[pltpu.VMEM((tm, tn), F32)],
        compiler_params=_params(("parallel", "parallel", "arbitrary")),
    )(*args)


def norm_fwd(name, x, g, tr):
    def body(tv, fv):
        return [_rms(tv[0], fv[0])], []
    rows, d = x.shape
    (h,), _ = row_call(name, body, rows // tr, [(x, tr, d, 0)], [g], [(rows, tr, d, BF16)], [])
    return h


def norm_bwd(name, x, g, dh, extra, tr):
    def body(tv, fv):
        _, vjp = jax.vjp(_rms, tv[0], fv[0])
        dx, dg = vjp(tv[1])
        if extra is not None:
            dx = dx + tv[2]
        return [dx], [dg]
    rows, d = x.shape
    tiled = [(x, tr, d, 0), (dh, tr, d, 0)] + ([(extra, tr, d, 0)] if extra is not None else [])
    (dx,), (dg,) = row_call(name, body, rows // tr, tiled, [g], [(rows, tr, d, F32)], [g.shape])
    return dx, dg


def _ssd_pre(xbc, halo, dtraw, w0, w1, w2, w3, cb, dtb):
    y = w3 * xbc + w2 * _shift_rows(xbc, halo, 1) + w1 * _shift_rows(xbc, halo, 2) + w0 * _shift_rows(xbc, halo, 3) + cb
    return _silu(y), _softplus(dtraw + dtb)


def _ssd_post(ys, xs, z, dskip, ng):
    w = ys.shape[1]
    y = (ys + xs * _row_vector_expand(dskip, w)) * _silu(z)
    gw = w // SSD_GROUPS
    parts = []
    for gi in range(SSD_GROUPS):
        yg = y[:, gi * gw:(gi + 1) * gw]
        parts.append(yg * lax.rsqrt(jnp.mean(yg * yg, axis=-1, keepdims=True) + NORM_EPS))
    return jnp.concatenate(parts, axis=1) * ng


def _rwkv_pre(urkv, ulora, hrkv, hlora, mu_rkv, mu_lora, w0, a0, kkw, kaw, w2p, a2p, g2):
    w = w0.shape[1]
    urkv = urkv + (_shift_rows(urkv, hrkv, 1) - urkv) * mu_rkv
    ulora = ulora + (_shift_rows(ulora, hlora, 1) - ulora) * mu_lora
    r, k, v = urkv[:, :w], urkv[:, w:2 * w], urkv[:, 2 * w:]
    pw, pa, pg = ulora[:, :LANES], ulora[:, LANES:2 * LANES], ulora[:, 2 * LANES:]
    w_log = -_softplus(-(w0 + mm(jnp.tanh(pw), w2p, 1, 0, 1))) - 0.5
    lw = -jnp.exp(w_log)
    iclr = _sigmoid(a0 + mm(pa, a2p, 1, 0, 1))
    gate = mm(_sigmoid(pg), g2, 1, 0, 1)
    kk = k * kkw
    kk = kk / jnp.maximum(jnp.sqrt(_head_expand(_head_sum(kk * kk), w)), 1e-12)
    k2 = k * (1.0 + (iclr - 1.0) * kaw)
    return r, lw, k2, v, -kk, kk * iclr, gate


def _rwkv_post(ys, r, k2, v, gate, rk, lnw, lnb):
    w = ys.shape[1]
    inv = 1.0 / HEAD_DIM
    mean = _head_expand(_head_sum(ys), w) * inv
    d = ys - mean
    var = _head_expand(_head_sum(d * d), w) * inv
    yn = d * lax.rsqrt(var + RWKV_LN_EPS) * lnw + lnb
    bonus = _head_expand(_head_sum(r * k2 * rk), w) * v
    return (yn + bonus) * gate


def _attn(q, k, v):
    d = q.shape[1]
    hd = d // XATTN_HEADS
    outs = []
    for h in range(XATTN_HEADS):
        sl = slice(h * hd, (h + 1) * hd)
        s = mm(q[:, sl], k[:, sl], 1, 1, 1) * (hd ** -0.5)
        s = s - jnp.max(s, axis=-1, keepdims=True)
        p = jnp.exp(s)
        p = p / jnp.sum(p, axis=-1, keepdims=True)
        outs.append(mm(p, v[:, sl], 1, 0, 1))
    return jnp.concatenate(outs, axis=1)


def _relu2(a):
    return jnp.square(jnp.maximum(a, 0.0))


def fn_fwd(name, fn, n_tiles, tiled, full, out_tiled):
    def body(tv, fv):
        outs = fn(*tv, *fv)
        return (list(outs) if isinstance(outs, (tuple, list)) else [outs]), []
    outs, _ = row_call(name, body, n_tiles, tiled, full, out_tiled, [])
    return outs


def fn_bwd(name, fn, n_tiles, tiled, full, cts, ct_fn, out_tiled):
    nt = len(tiled)

    def body(tv, fv):
        outs, vjp = jax.vjp(fn, *tv[:nt], *fv)
        ct = ct_fn(tv[nt:])
        grads = vjp(tuple(ct) if isinstance(outs, (tuple, list)) else ct[0])
        return list(grads[:nt]), list(grads[nt:])
    return row_call(name, body, n_tiles, tiled + cts, full, out_tiled, [f.shape for f in full])


def _ssd_chunk(xs, bm, cm, dt_all, a_log, ht, p):
    q = xs.shape[0]
    lane = _iota((1, LANES), 1)
    row = _iota((q, 1), 0)
    tril = _iota((q, q), 0) >= _iota((q, q), 1)
    half = lane < HEAD_DIM
    da = dt_all * (-jnp.exp(a_log))
    cs = _dot_exact(tril.astype(F32), da)

    def col(mat, h):
        return jnp.sum(jnp.where(lane == h, mat, 0.0), axis=1, keepdims=True)

    cs0, cs1 = col(cs, 2 * p), col(cs, 2 * p + 1)
    xdt = xs * jnp.where(half, col(dt_all, 2 * p), col(dt_all, 2 * p + 1))
    csx = jnp.where(half, cs0, cs1)
    last = jnp.sum(jnp.where(row == q - 1, csx, 0.0), axis=0, keepdims=True)
    cb = mm(cm, bm, 1, 1, 1)
    y = mm(cm, ht, 1, 0, 1) * jnp.exp(csx)
    for csh, hm in ((cs0, half), (cs1, jnp.logical_not(half))):
        csl = jnp.broadcast_to(csh, (q, q))
        seg = csl - csl.T
        lmat = jnp.where(tril, jnp.exp(jnp.where(tril, seg, 0.0)), 0.0)
        y = y + jnp.where(hm, mm(cb * lmat, xdt, 1, 0, 1), 0.0)
    st = mm(bm, xdt * jnp.exp(last - csx), 0, 0, 1)
    return y, ht * jnp.exp(last) + st


def _rwkv_chunk(r, lw, k, v, a, b, ht):
    c = r.shape[0]
    ps = RWKV_PASSES
    lane = _iota((1, LANES), 1)
    row = _iota((c, 1), 0)
    ri, ci = _iota((c, c), 0), _iota((c, c), 1)
    tril_i, tril_s = ri >= ci, ri > ci
    eye = (ri == ci).astype(F32)
    half = lane < HEAD_DIM
    halves = (half, jnp.logical_not(half))
    cum = _dot_exact(tril_i.astype(F32), lw)
    at = a * jnp.exp(cum - lw)
    en = jnp.exp(-cum)
    bt, kt = b * en, k * en
    rt = r * jnp.exp(cum)
    ah = mm(at, ht, 1, 1, ps)
    u = jnp.zeros_like(r)
    for hm in halves:
        atm = jnp.where(hm, at, 0.0)
        aab = jnp.where(tril_s, mm(atm, bt, 1, 1, ps), 0.0)
        aak = jnp.where(tril_s, mm(atm, kt, 1, 1, ps), 0.0)
        tm, pm = eye + aab, aab
        for _ in range(int(math.log2(c)) - 1):
            pm = mm(pm, pm, 1, 0, ps)
            tm = tm + mm(tm, pm, 1, 0, ps)
        u = u + jnp.where(hm, mm(tm, ah + mm(aak, v, 1, 0, ps), 1, 0, ps), 0.0)
    y = mm(rt, ht, 1, 1, ps)
    for hm in halves:
        rtm = jnp.where(hm, rt, 0.0)
        arb = jnp.where(tril_i, mm(rtm, bt, 1, 1, ps), 0.0)
        ark = jnp.where(tril_i, mm(rtm, kt, 1, 1, ps), 0.0)
        y = y + jnp.where(hm, mm(arb, u, 1, 0, ps) + mm(ark, v, 1, 0, ps), 0.0)
    plast = jnp.sum(jnp.where(row == c - 1, cum, 0.0), axis=0, keepdims=True)
    bd = (_iota((LANES, LANES), 0) < HEAD_DIM) == (_iota((LANES, LANES), 1) < HEAD_DIM)
    htn = jnp.where(bd, (ht + mm(u, bt, 0, 0, ps) + mm(v, kt, 0, 0, ps)) * jnp.exp(plast), 0.0)
    return y, htn


def scan_fwd(name, chunk_fn, chunk, seq_in, const_in, n_pairs, out_width):
    t = seq_in[0][0].shape[0]
    nc = t // chunk
    ns, ncst = len(seq_in), len(const_in)

    def kern(*refs):
        y_ref, st_ref, ht = refs[ns + ncst], refs[ns + ncst + 1], refs[ns + ncst + 2]

        @pl.when(pl.program_id(1) == 0)
        def _():
            ht[...] = jnp.zeros_like(ht)

        h0 = ht[...]
        st_ref[...] = h0
        y, hn = chunk_fn([r[...] for r in refs[:ns]], [r[...] for r in refs[ns:ns + ncst]], h0, pl.program_id(0))
        y_ref[...] = y
        ht[...] = hn

    in_specs = [pl.BlockSpec((chunk, LANES), functools.partial(lambda p, c, f: (c, f(p)), f=f)) for (_, f) in seq_in]
    in_specs += [pl.BlockSpec(a.shape, lambda p, c: (0, 0)) for a in const_in]
    return pl.pallas_call(
        kern, name=name, grid=(n_pairs, nc), in_specs=in_specs,
        out_specs=[pl.BlockSpec((chunk, LANES), lambda p, c: (c, p)),
                   pl.BlockSpec((None, None, LANES, LANES), lambda p, c: (p, c, 0, 0))],
        out_shape=[jax.ShapeDtypeStruct((t, out_width), F32), jax.ShapeDtypeStruct((n_pairs, nc, LANES, LANES), F32)],
        scratch_shapes=[pltpu.VMEM((LANES, LANES), F32)],
        compiler_params=_params(("arbitrary", "arbitrary")),
    )(*[s[0] for s in seq_in], *const_in)


def scan_bwd(name, chunk_fn, chunk, seq_in, const_in, states, dy, n_pairs):
    t = dy.shape[0]
    nc = t // chunk
    ns, ncst = len(seq_in), len(const_in)

    def kern(*refs):
        seq_refs, cst_refs = refs[:ns], refs[ns:ns + ncst]
        st_ref, dy_ref = refs[ns + ncst], refs[ns + ncst + 1]
        o = ns + ncst + 2
        dseq_refs, dcst_refs, dht = refs[o:o + ns], refs[o + ns:o + ns + ncst], refs[o + ns + ncst]
        p, i = pl.program_id(0), pl.program_id(1)

        @pl.when(i == 0)
        def _():
            dht[...] = jnp.zeros_like(dht)

        def fn(*vals):
            return chunk_fn(list(vals[:ns]), list(vals[ns:ns + ncst]), vals[ns + ncst], p)

        _, vjp = jax.vjp(fn, *[r[...] for r in seq_refs], *[r[...] for r in cst_refs], st_ref[...])
        grads = vjp((dy_ref[...], dht[...]))
        for r, g in zip(dseq_refs, grads[:ns]):
            r[...] = g
        dht[...] = grads[ns + ncst]
        if ncst:
            first = jnp.logical_and(p == 0, i == 0)

            @pl.when(first)
            def _():
                for r, g in zip(dcst_refs, grads[ns:ns + ncst]):
                    r[...] = g

            @pl.when(jnp.logical_not(first))
            def _():
                for r, g in zip(dcst_refs, grads[ns:ns + ncst]):
                    r[...] += g

    rev = lambda i: nc - 1 - i
    in_specs = [pl.BlockSpec((chunk, LANES), functools.partial(lambda p, i, f: (rev(i), f(p)), f=f)) for (_, f) in seq_in]
    in_specs += [pl.BlockSpec(a.shape, lambda p, i: (0, 0)) for a in const_in]
    in_specs += [pl.BlockSpec((None, None, LANES, LANES), lambda p, i: (p, rev(i), 0, 0)),
                 pl.BlockSpec((chunk, LANES), lambda p, i: (rev(i), p))]
    out_specs = [pl.BlockSpec((chunk, LANES), lambda p, i: (rev(i), p)) for _ in seq_in]
    out_specs += [pl.BlockSpec(a.shape, lambda p, i: (0, 0)) for a in const_in]
    out_shape = [jax.ShapeDtypeStruct((t, n_pairs * LANES), F32) for _ in seq_in]
    out_shape += [jax.ShapeDtypeStruct(a.shape, F32) for a in const_in]
    res = pl.pallas_call(
        kern, name=name, grid=(n_pairs, nc), in_specs=in_specs, out_specs=out_specs, out_shape=out_shape,
        scratch_shapes=[pltpu.VMEM((LANES, LANES), F32)],
        compiler_params=_params(("arbitrary", "arbitrary")),
    )(*[s[0] for s in seq_in], *const_in, states, dy)
    return list(res[:ns]), list(res[ns:])


def loss_head(x3, tgt, g, tr):
    rows, d = x3.shape

    def body(tv, fv):
        def f(x, gg):
            e = jnp.square(_rms(x, gg) - tv[1])
            return 0.5 * jnp.sum(jnp.mean(e, axis=-1, keepdims=True), axis=0, keepdims=True)
        l, vjp = jax.vjp(f, tv[0], fv[0])
        dx, dg = vjp(jnp.ones((1, 1), F32))
        return [dx], [dg, jnp.broadcast_to(l, (8, LANES))]
    (dx,), (dg, l) = row_call("loss_head", body, rows // tr, [(x3, tr, d, 0), (tgt, tr, d, 0)], [g],
                              [(rows, tr, d, F32)], [g.shape, (8, LANES)])
    return dx, dg, l


def _adam_math(w, g, m, v):
    m = ADAM_B1 * m + (1.0 - ADAM_B1) * g
    v = ADAM_B2 * v + (1.0 - ADAM_B2) * jnp.square(g)
    m_hat = m / (1.0 - ADAM_B1 ** ADAM_STEP)
    v_hat = v / (1.0 - ADAM_B2 ** ADAM_STEP)
    delta = -ADAM_LR * (m_hat / (jnp.sqrt(v_hat) + ADAM_EPS) + ADAM_WD * w)
    return delta, m, v


def adamw(name, w, m, v, g_parts):
    rows, cols = w.shape
    tr = _pick(rows, (256, 128, 64, 32, 16, 8))
    n_g = len(g_parts)

    def body(tv, fv):
        g = tv[3]
        for extra in tv[4:4 + n_g - 1]:
            g = g + extra
        delta, mn, vn = _adam_math(tv[0], g, tv[1], tv[2])
        return [g, delta, mn, vn], []
    tiled = [(a, tr, cols, 0) for a in (w, m, v, *g_parts)]
    outs, _ = row_call(name, body, rows // tr, tiled, [], [(rows, tr, cols, F32)] * 4, [])
    return outs


def sum_slots(name, r):
    _, rows, cols = r.shape
    tr = _pick(rows, (256, 128, 64, 32, 16, 8))

    def kern(r0, r1, r2, r3, o):
        o[...] = ((r0[...] + r1[...]) + r2[...]) + r3[...]

    in_specs = [pl.BlockSpec((None, tr, cols), functools.partial(lambda i, s: (s, i, 0), s=s)) for s in range(4)]
    return pl.pallas_call(
        kern, name=name, grid=(rows // tr,), in_specs=in_specs, out_specs=pl.BlockSpec((tr, cols), lambda i: (i, 0)),
        out_shape=jax.ShapeDtypeStruct((rows, cols), F32), compiler_params=_params(("arbitrary",)),
    )(r, r, r, r)


def _my_place():
    return lax.axis_index("x"), lax.axis_index("y"), lax.axis_index("c")


def chip_exchange(name, arrays, gather):
    nw = len(arrays)
    ANY = pl.BlockSpec(memory_space=pl.ANY)

    def body(*refs):
        ins, outs = refs[:nw], refs[nw:2 * nw]
        send, recv, loc = refs[2 * nw:]
        x, y, c = _my_place()
        q = 2 * x + y
        peers = [(1 - x, y), (x, 1 - y), (1 - x, 1 - y)]

        def src(w, dest_chip):
            return ins[w] if gather else ins[w].at[dest_chip]

        def remote(w, j):
            px, py = peers[j]
            return pltpu.make_async_remote_copy(
                src_ref=src(w, 2 * px + py), dst_ref=outs[w].at[q], send_sem=send.at[w, j], recv_sem=recv.at[w, j],
                device_id=(px, py, c), device_id_type=MESH_ID)

        def arrival(w, j):
            px, py = peers[j]
            return pltpu.make_async_remote_copy(
                src_ref=src(w, q), dst_ref=outs[w].at[2 * px + py], send_sem=send.at[w, j], recv_sem=recv.at[w, j],
                device_id=(px, py, c), device_id_type=MESH_ID)

        local = [pltpu.make_async_copy(src(w, q), outs[w].at[q], loc.at[w]) for w in range(nw)]
        sends = [[remote(w, j) for j in range(3)] for w in range(nw)]
        for w in range(nw):
            local[w].start()
            for j in range(3):
                sends[w][j].start()
        for w in range(nw):
            local[w].wait()
            for j in range(3):
                sends[w][j].wait_send()
                arrival(w, j).wait_recv()

    out_shape = [jax.ShapeDtypeStruct((4,) + (a.shape if gather else a.shape[1:]), a.dtype) for a in arrays]
    return pl.pallas_call(
        body, name=name, in_specs=[ANY] * nw, out_specs=[ANY] * nw, out_shape=out_shape,
        scratch_shapes=[pltpu.SemaphoreType.DMA((nw, 3)), pltpu.SemaphoreType.DMA((nw, 3)), pltpu.SemaphoreType.DMA((nw,))],
        compiler_params=pltpu.CompilerParams(has_side_effects=True),
    )(*arrays)


def core_swap(name, arrays):
    nw = len(arrays)
    ANY = pl.BlockSpec(memory_space=pl.ANY)

    def body(*refs):
        ins, outs = refs[:nw], refs[nw:2 * nw]
        send, recv = refs[2 * nw:]
        x, y, c = _my_place()
        copies = [pltpu.make_async_remote_copy(
            src_ref=ins[w], dst_ref=outs[w], send_sem=send.at[w], recv_sem=recv.at[w],
            device_id=(x, y, 1 - c), device_id_type=MESH_ID) for w in range(nw)]
        for cp in copies:
            cp.start()
        for cp in copies:
            cp.wait_send()
            cp.wait_recv()

    return pl.pallas_call(
        body, name=name, in_specs=[ANY] * nw, out_specs=[ANY] * nw,
        out_shape=[jax.ShapeDtypeStruct(a.shape, a.dtype) for a in arrays],
        scratch_shapes=[pltpu.SemaphoreType.DMA((nw,)), pltpu.SemaphoreType.DMA((nw,))],
        compiler_params=pltpu.CompilerParams(has_side_effects=True),
    )(*arrays)


def all_reduce_small(name, v):
    rows = v.shape[0]
    VM = pl.BlockSpec(memory_space=pltpu.VMEM)

    def body(v_ref, o_ref, buf, send, recv):
        x, y, c = _my_place()
        me = 4 * x + 2 * y + c

        def peer(kx):
            return (x ^ ((kx >> 2) & 1), y ^ ((kx >> 1) & 1), c ^ (kx & 1))

        def copy(kx, slot):
            return pltpu.make_async_remote_copy(
                src_ref=v_ref, dst_ref=buf.at[slot], send_sem=send.at[kx - 1], recv_sem=recv.at[kx - 1],
                device_id=peer(kx), device_id_type=MESH_ID)

        sends = [copy(kx, me) for kx in range(1, 8)]
        for cp in sends:
            cp.start()
        buf[me] = v_ref[...]
        for kx in range(1, 8):
            copy(kx, me ^ kx).wait_recv()
        for cp in sends:
            cp.wait_send()
        acc = buf[0]
        for d in range(1, 8):
            acc = acc + buf[d]
        o_ref[...] = acc

    return pl.pallas_call(
        body, name=name, in_specs=[VM], out_specs=VM, out_shape=jax.ShapeDtypeStruct(v.shape, F32),
        scratch_shapes=[pltpu.VMEM((8, rows, LANES), F32), pltpu.SemaphoreType.DMA((7,)), pltpu.SemaphoreType.DMA((7,))],
        compiler_params=pltpu.CompilerParams(has_side_effects=True, vmem_limit_bytes=VMEM_LIMIT),
    )(v)


def _pad_cols(a, n):
    return jnp.pad(a, ((0, 0), (0, n - a.shape[1])))


def _pad_rows(a, n):
    return jnp.pad(a, ((0, n - a.shape[0]), (0, 0)))


def _halo(u, tr):
    t, cdim = u.shape
    tails = u.reshape(t // tr, tr, cdim)[:, tr - HALO:, :]
    tails = jnp.concatenate([jnp.zeros((1, HALO, cdim), u.dtype), tails[:-1]], axis=0)
    return tails.reshape(-1, cdim)


def _unhalo(du, dhalo, tr):
    t, cdim = du.shape
    n = t // tr
    dh = dhalo.reshape(n, HALO, cdim)
    dh = jnp.concatenate([dh[1:], jnp.zeros((1, HALO, cdim), du.dtype)], axis=0)
    d3 = du.reshape(n, tr, cdim)
    d3 = jnp.concatenate([d3[:, :tr - HALO, :], d3[:, tr - HALO:, :] + dh], axis=1)
    return d3.reshape(t, cdim)


def _to_slots(g, axis):
    r, cdim = g.shape
    if axis == 0:
        return g.reshape(4, r // 4, cdim)
    return g.reshape(r, 4, cdim // 4).transpose(1, 0, 2)


def _from_slots(s, axis):
    if axis == 0:
        return s.reshape(s.shape[0] * s.shape[1], s.shape[2])
    return s.transpose(1, 0, 2).reshape(s.shape[1], 4 * s.shape[2])


BIG = ("w_in", "w_out", "xattn_wq", "xattn_wk", "xattn_wv", "xattn_wo", "ffn_w1", "ffn_w2")
BIG_AXIS = {"w_in": 1, "w_out": 0, "xattn_wq": 0, "xattn_wk": 0, "xattn_wv": 0, "xattn_wo": 0, "ffn_w1": 1, "ffn_w2": 0}
SMALL_SHARDED = ("ssd_conv_w", "rwkv_w2", "rwkv_a2", "rwkv_g2")
WEIGHTS = ("norm_mix_g", "w_in", "ssd_conv_w", "ssd_conv_b", "ssd_dt_bias", "ssd_a_log", "ssd_d", "ssd_norm_g",
           "rwkv_mu", "rwkv_w0", "rwkv_w2", "rwkv_a0", "rwkv_a2", "rwkv_g2", "rwkv_k_k", "rwkv_k_a", "rwkv_r_k",
           "rwkv_ln_w", "rwkv_ln_b", "w_out", "norm_x_g", "norm_mem_g", "xattn_wq", "xattn_wk", "xattn_wv", "xattn_wo",
           "norm_ffn_g", "ffn_w1", "ffn_w2", "final_norm_g")


def _local_grads(x, mem, tgt, wt, full):
    t, d = x.shape
    w = d // 2
    nh = w // HEAD_DIM
    n_pairs = nh // 2
    ppg = n_pairs // SSD_GROUPS
    bc = SSD_GROUPS * SSD_STATE
    conv_dim = w + 2 * bc
    tr = ROW_TILE
    nt = t // tr
    dr = wt["rwkv_w2"].shape[0]
    ar = wt["rwkv_a2"].shape[0]
    gr = wt["rwkv_g2"].shape[0]

    w_in = full["w_in"]
    o = 0
    segs = {}
    for nm, width in (("z", w), ("xbc", conv_dim), ("dt", nh), ("rkv", 3 * w), ("pw", dr), ("pa", ar), ("pg", gr)):
        segs[nm] = (o, width)
        o += width
    padded = {"z": w, "xbc": conv_dim, "dt": LANES, "rkv": 3 * w, "pw": LANES, "pa": LANES, "pg": gr}
    order = ("z", "xbc", "dt", "rkv", "pw", "pa", "pg")
    w_perm = jnp.concatenate([_pad_cols(w_in[:, segs[nm][0]:segs[nm][0] + segs[nm][1]], padded[nm]) for nm in order], axis=1)
    offs = {}
    o = 0
    for nm in order:
        offs[nm] = o
        o += padded[nm]
    n_perm = o
    lora_w = 2 * LANES + gr

    def seg_cols(a, nm, width=None):
        return a[:, offs[nm]:offs[nm] + (padded[nm] if width is None else width)]

    mu = wt["rwkv_mu"]
    mo = 3 * w
    mu_rkv = mu[:, :mo]
    mu_lora = jnp.concatenate([_pad_cols(mu[:, mo:mo + dr], LANES), _pad_cols(mu[:, mo + dr:mo + dr + ar], LANES),
                               mu[:, mo + dr + ar:]], axis=1)
    w2p = _pad_rows(full["rwkv_w2"], LANES)
    a2p = _pad_rows(full["rwkv_a2"], LANES)
    g2 = full["rwkv_g2"]
    conv_w = full["ssd_conv_w"]
    cw = [conv_w[i:i + 1] for i in range(SSD_CONV)]
    dt_bias = _pad_cols(wt["ssd_dt_bias"], LANES)
    a_log = _pad_cols(wt["ssd_a_log"], LANES)
    d_skip = _pad_cols(wt["ssd_d"], LANES)
    r_k = wt["rwkv_r_k"].reshape(1, w)

    h1 = norm_fwd("norm_mix", x, wt["norm_mix_g"], tr)
    u = matmul("in_proj", h1, w_perm)
    z, xbc, dtraw = seg_cols(u, "z"), seg_cols(u, "xbc"), seg_cols(u, "dt")
    urkv = seg_cols(u, "rkv")
    ulora = u[:, offs["pw"]:offs["pw"] + lora_w]

    halo_xbc = _halo(xbc, tr)
    ssd_pre_t = [(xbc, tr, conv_dim, 0), (halo_xbc, HALO, conv_dim, 0), (dtraw, tr, LANES, 0)]
    ssd_pre_f = cw + [wt["ssd_conv_b"], dt_bias]
    act, dt = fn_fwd("ssd_pre", _ssd_pre, nt, ssd_pre_t, ssd_pre_f, [(t, tr, conv_dim, F32), (t, tr, LANES, F32)])

    nb = w // LANES
    ssd_seq = [(act, lambda p: p), (act, lambda p: nb + p // ppg), (act, lambda p: nb + SSD_GROUPS + p // ppg),
               (dt, lambda p: 0)]

    def ssd_fn(sv, cv, ht, p):
        return _ssd_chunk(sv[0], sv[1], sv[2], sv[3], cv[0], ht, p)

    y_scan, ssd_states = scan_fwd("ssd_scan", ssd_fn, SSD_CHUNK, ssd_seq, [a_log], n_pairs, w)
    ssd_post_t = [(y_scan, tr, w, 0), (act, tr, w, 0), (z, tr, w, 0)]
    ssd_post_f = [d_skip, wt["ssd_norm_g"]]
    (y_ssd,) = fn_fwd("ssd_post", _ssd_post, nt, ssd_post_t, ssd_post_f, [(t, tr, w, F32)])

    halo_rkv, halo_lora = _halo(urkv, tr), _halo(ulora, tr)
    rw_pre_t = [(urkv, tr, 3 * w, 0), (ulora, tr, lora_w, 0), (halo_rkv, HALO, 3 * w, 0), (halo_lora, HALO, lora_w, 0)]
    rw_pre_f = [mu_rkv, mu_lora, wt["rwkv_w0"], wt["rwkv_a0"], wt["rwkv_k_k"], wt["rwkv_k_a"], w2p, a2p, g2]
    rw = fn_fwd("rwkv_pre", _rwkv_pre, nt, rw_pre_t, rw_pre_f, [(t, tr, w, F32)] * 7)
    r_, lw_, k2_, v_, nkk_, b_, gate_ = rw
    rw_seq = [(a, lambda p: p) for a in (r_, lw_, k2_, v_, nkk_, b_)]

    def rw_fn(sv, cv, ht, p):
        return _rwkv_chunk(*sv, ht)

    yr_scan, rw_states = scan_fwd("rwkv_scan", rw_fn, RWKV_CHUNK, rw_seq, [], n_pairs, w)
    rw_post_t = [(a, tr, w, 0) for a in (yr_scan, r_, k2_, v_, gate_)]
    rw_post_f = [r_k, wt["rwkv_ln_w"], wt["rwkv_ln_b"]]
    (y_rwkv,) = fn_fwd("rwkv_post", _rwkv_post, nt, rw_post_t, rw_post_f, [(t, tr, w, F32)])

    ymix = jnp.concatenate([y_ssd, y_rwkv], axis=1).astype(BF16)
    x1 = matmul("out_proj", ymix, full["w_out"], resid=x)

    h2 = norm_fwd("norm_x", x1, wt["norm_x_g"], tr)
    mrows = mem.shape[0]
    mn = norm_fwd("norm_mem", mem, wt["norm_mem_g"], mrows)
    q = matmul("xattn_q", h2, full["xattn_wq"])
    kx = matmul("xattn_k", mn, full["xattn_wk"])
    vx = matmul("xattn_v", mn, full["xattn_wv"])
    (ao,) = fn_fwd("xattn_core", _attn, nt, [(q, tr, d, 0)], [kx, vx], [(t, tr, d, BF16)])
    x2 = matmul("xattn_o", ao, full["xattn_wo"], resid=x1)

    h3 = norm_fwd("norm_ffn", x2, wt["norm_ffn_g"], tr)
    a1 = matmul("ffn_up", h3, full["ffn_w1"])
    dff = a1.shape[1]
    (f1,) = fn_fwd("ffn_act", _relu2, nt, [(a1, tr, dff, 0)], [], [(t, tr, dff, BF16)])
    x3 = matmul("ffn_down", f1, full["ffn_w2"], resid=x2)

    dx3, g_final, loss_tile = loss_head(x3, tgt, wt["final_norm_g"].reshape(1, d), tr)

    grads = {"final_norm_g": g_final.reshape(d)}
    dx3b = dx3.astype(BF16)
    grads["ffn_w2"] = matmul("ffn_down_dw", f1.T, dx3b)
    df1 = matmul("ffn_down_dx", dx3b, full["ffn_w2"], tb=True)
    (da1,), _ = fn_bwd("ffn_act_bwd", _relu2, nt, [(a1, tr, dff, 0)], [], [(df1, tr, dff, 0)], lambda c: c,
                       [(t, tr, dff, BF16)])
    grads["ffn_w1"] = matmul("ffn_up_dw", h3.T, da1)
    dh3 = matmul("ffn_up_dx", da1, full["ffn_w1"], tb=True)
    dx2, grads["norm_ffn_g"] = norm_bwd("norm_ffn_bwd", x2, wt["norm_ffn_g"], dh3, dx3, tr)

    dx2b = dx2.astype(BF16)
    grads["xattn_wo"] = matmul("xattn_o_dw", ao.T, dx2b)
    dao = matmul("xattn_o_dx", dx2b, full["xattn_wo"], tb=True)
    (dq,), (dkx, dvx) = fn_bwd("xattn_core_bwd", _attn, nt, [(q, tr, d, 0)], [kx, vx], [(dao, tr, d, 0)], lambda c: c,
                               [(t, tr, d, BF16)])
    grads["xattn_wq"] = matmul("xattn_q_dw", h2.T, dq)
    dh2 = matmul("xattn_q_dx", dq, full["xattn_wq"], tb=True)
    dkb, dvb = dkx.astype(BF16), dvx.astype(BF16)
    grads["xattn_wk"] = matmul("xattn_k_dw", mn.T, dkb)
    grads["xattn_wv"] = matmul("xattn_v_dw", mn.T, dvb)
    dmn = matmul("xattn_k_dx", dkb, full["xattn_wk"], tb=True)
    dmn = matmul("xattn_v_dx", dvb, full["xattn_wv"], tb=True, resid=dmn)
    _, grads["norm_mem_g"] = norm_bwd("norm_mem_bwd", mem, wt["norm_mem_g"], dmn, None, mrows)
    dx1, grads["norm_x_g"] = norm_bwd("norm_x_bwd", x1, wt["norm_x_g"], dh2, dx2, tr)

    dx1b = dx1.astype(BF16)
    grads["w_out"] = matmul("out_proj_dw", ymix.T, dx1b)
    dymix = matmul("out_proj_dx", dx1b, full["w_out"], tb=True)

    (dyr, dr1, dk1, dv1, dgate), (g_rk, grads["rwkv_ln_w"], grads["rwkv_ln_b"]) = fn_bwd(
        "rwkv_post_bwd", _rwkv_post, nt, rw_post_t, rw_post_f, [(dymix, tr, w, 1)], lambda c: c, [(t, tr, w, F32)] * 5)
    grads["rwkv_r_k"] = g_rk.reshape(wt["rwkv_r_k"].shape)
    (dr2, dlw, dk2, dv2, dnkk, db), _ = scan_bwd("rwkv_scan_bwd", rw_fn, RWKV_CHUNK, rw_seq, [], rw_states, dyr, n_pairs)
    rw_ct = [(a, tr, w, 0) for a in (dr1, dr2, dlw, dk1, dk2, dv1, dv2, dnkk, db, dgate)]

    def rw_ct_fn(c):
        return (c[0] + c[1], c[2], c[3] + c[4], c[5] + c[6], c[7], c[8], c[9])

    (durkv, dulora, dhrkv, dhlora), rw_pg = fn_bwd(
        "rwkv_pre_bwd", _rwkv_pre, nt, rw_pre_t, rw_pre_f, rw_ct, rw_ct_fn,
        [(t, tr, 3 * w, F32), (t, tr, lora_w, F32), (nt * HALO, HALO, 3 * w, F32), (nt * HALO, HALO, lora_w, F32)])
    durkv = _unhalo(durkv, dhrkv, tr)
    dulora = _unhalo(dulora, dhlora, tr)
    g_mu_rkv, g_mu_lora, grads["rwkv_w0"], grads["rwkv_a0"], grads["rwkv_k_k"], grads["rwkv_k_a"], g_w2p, g_a2p, grads["rwkv_g2"] = rw_pg
    grads["rwkv_mu"] = jnp.concatenate([g_mu_rkv, g_mu_lora[:, :dr], g_mu_lora[:, LANES:LANES + ar], g_mu_lora[:, 2 * LANES:]], axis=1)
    grads["rwkv_w2"] = g_w2p[:dr]
    grads["rwkv_a2"] = g_a2p[:ar]

    (dys, dxs1, dz), (g_d, grads["ssd_norm_g"]) = fn_bwd(
        "ssd_post_bwd", _ssd_post, nt, ssd_post_t, ssd_post_f, [(dymix, tr, w, 0)], lambda c: c, [(t, tr, w, F32)] * 3)
    grads["ssd_d"] = g_d[:, :nh]
    (dxs2, dbp, dcp, ddtp), (g_alog,) = scan_bwd("ssd_scan_bwd", ssd_fn, SSD_CHUNK, ssd_seq, [a_log], ssd_states, dys, n_pairs)
    grads["ssd_a_log"] = g_alog[:, :nh]
    ssd_ct = [(dxs1, tr, w, 0), (dxs2, tr, w, 0), (dbp, tr, w, 0), (dcp, tr, w, 0), (ddtp, tr, w, 0)]

    def ssd_ct_fn(c):
        def group_sum(a):
            parts = []
            for gi in range(SSD_GROUPS):
                s = a[:, gi * ppg * LANES:(gi * ppg + 1) * LANES]
                for j in range(1, ppg):
                    s = s + a[:, (gi * ppg + j) * LANES:(gi * ppg + j + 1) * LANES]
                parts.append(s)
            return parts
        ddt = c[4][:, :LANES]
        for j in range(1, n_pairs):
            ddt = ddt + c[4][:, j * LANES:(j + 1) * LANES]
        return (jnp.concatenate([c[0] + c[1]] + group_sum(c[2]) + group_sum(c[3]), axis=1), ddt)

    (dxbc, dhxbc, ddtraw), ssd_pg = fn_bwd(
        "ssd_pre_bwd", _ssd_pre, nt, ssd_pre_t, ssd_pre_f, ssd_ct, ssd_ct_fn,
        [(t, tr, conv_dim, F32), (nt * HALO, HALO, conv_dim, F32), (t, tr, LANES, F32)])
    dxbc = _unhalo(dxbc, dhxbc, tr)
    grads["ssd_conv_w"] = jnp.concatenate(ssd_pg[:SSD_CONV], axis=0)
    grads["ssd_conv_b"] = ssd_pg[SSD_CONV]
    grads["ssd_dt_bias"] = ssd_pg[SSD_CONV + 1][:, :nh]

    du = jnp.concatenate([dz, dxbc, ddtraw, durkv, dulora], axis=1).astype(BF16)
    g_perm = matmul("in_proj_dw", h1.T, du)
    grads["w_in"] = jnp.concatenate([seg_cols(g_perm, nm, segs[nm][1]) for nm in order], axis=1)
    dh1 = matmul("in_proj_dx", du, w_perm, tb=True)
    grad_x, grads["norm_mix_g"] = norm_bwd("norm_mix_bwd", x, wt["norm_mix_g"], dh1, dx1, tr)
    return loss_tile, grad_x, grads


def _pack(arrs):
    flat = jnp.concatenate([a.reshape(-1) for a in arrs])
    n = flat.shape[0]
    rows = -(-n // (8 * LANES)) * 8
    return jnp.pad(flat, (0, rows * LANES - n)).reshape(rows, LANES)


def _unpack(packed, shapes):
    flat = packed.reshape(-1)
    out, o = [], 0
    for s in shapes:
        n = math.prod(s)
        out.append(flat[o:o + n].reshape(s))
        o += n
    return out


def _as2d(a):
    return a.reshape(-1, a.shape[-1])


def _step(a):
    x, mem, tgt = a["x"][0], a["mem"][0], a["loss_target"][0]
    q = 2 * lax.axis_index("x") + lax.axis_index("y")

    shard2d = {n: _as2d(a[n][0]) for n in BIG}
    small_sh = {n: _as2d(a[n][0]) for n in SMALL_SHARDED}
    gathered = chip_exchange("gather_weights", [shard2d[n].astype(BF16) for n in BIG] + [small_sh[n] for n in SMALL_SHARDED], True)
    full = {n: _from_slots(g, BIG_AXIS[n]) for n, g in zip(BIG, gathered[:len(BIG)])}
    for n, g in zip(SMALL_SHARDED, gathered[len(BIG):]):
        full[n] = _from_slots(g, 1)

    wt = {n: (a[n] if a[n].ndim <= 2 else a[n][0]) for n in WEIGHTS if n not in BIG and n not in SMALL_SHARDED}
    for n in SMALL_SHARDED:
        wt[n] = small_sh[n]
    loss_tile, grad_x, grads = _local_grads(x, mem, tgt, wt, full)

    slots = chip_exchange("scatter_grads", [_to_slots(grads[n], BIG_AXIS[n]) for n in BIG], False)
    core_sums = [sum_slots("sum_" + n, s) for n, s in zip(BIG, slots)]
    other_sums = core_swap("swap_cores", core_sums)

    out = {}
    for n, mine, other in zip(BIG, core_sums, other_sums):
        g, dlt, mn, vn = adamw("adamw_" + n, shard2d[n], _as2d(a["m_" + n][0]), _as2d(a["v_" + n][0]), [mine, other])
        for key, val in (("grad_", g), ("delta_", dlt), ("new_m_", mn), ("new_v_", vn)):
            out[key + n] = val.reshape(a[n].shape)

    small = [n for n in WEIGHTS if n not in BIG]
    red = _unpack(all_reduce_small("all_reduce_small", _pack([grads[n] for n in small])), [grads[n].shape for n in small])
    g_loc = {}
    for n, g in zip(small, red):
        if n in SMALL_SHARDED:
            cols = g.shape[1] // 4
            g = lax.dynamic_slice_in_dim(g, q * cols, cols, axis=1)
        g_loc[n] = g.reshape(a[n].shape)
    res = adamw("adamw_small", *[_pack([src[n] for n in small]) for src in
                                 ({n: a[n] for n in small}, {n: a["m_" + n] for n in small}, {n: a["v_" + n] for n in small})],
                [_pack([g_loc[n] for n in small])])
    shapes = [a[n].shape for n in small]
    for key, packed in zip(("grad_", "delta_", "new_m_", "new_v_"), res):
        for n, val in zip(small, _unpack(packed, shapes)):
            out[key + n] = val

    loss = lax.psum(loss_tile[0, 0], ("x", "y", "c"))
    ordered = [loss, grad_x.reshape(a["x"].shape)]
    for key in ("grad_", "delta_", "new_m_", "new_v_"):
        ordered += [out[key + n] for n in WEIGHTS]
    return tuple(ordered)


def kernel(x, mem, norm_mix_g, w_in, ssd_conv_w, ssd_conv_b, ssd_dt_bias, ssd_a_log, ssd_d, ssd_norm_g, rwkv_mu, rwkv_w0, rwkv_w2, rwkv_a0, rwkv_a2, rwkv_g2, rwkv_k_k, rwkv_k_a, rwkv_r_k, rwkv_ln_w, rwkv_ln_b, w_out, norm_x_g, norm_mem_g, xattn_wq, xattn_wk, xattn_wv, xattn_wo, norm_ffn_g, ffn_w1, ffn_w2, final_norm_g, loss_target, m_norm_mix_g, m_w_in, m_ssd_conv_w, m_ssd_conv_b, m_ssd_dt_bias, m_ssd_a_log, m_ssd_d, m_ssd_norm_g, m_rwkv_mu, m_rwkv_w0, m_rwkv_w2, m_rwkv_a0, m_rwkv_a2, m_rwkv_g2, m_rwkv_k_k, m_rwkv_k_a, m_rwkv_r_k, m_rwkv_ln_w, m_rwkv_ln_b, m_w_out, m_norm_x_g, m_norm_mem_g, m_xattn_wq, m_xattn_wk, m_xattn_wv, m_xattn_wo, m_norm_ffn_g, m_ffn_w1, m_ffn_w2, m_final_norm_g, v_norm_mix_g, v_w_in, v_ssd_conv_w, v_ssd_conv_b, v_ssd_dt_bias, v_ssd_a_log, v_ssd_d, v_ssd_norm_g, v_rwkv_mu, v_rwkv_w0, v_rwkv_w2, v_rwkv_a0, v_rwkv_a2, v_rwkv_g2, v_rwkv_k_k, v_rwkv_k_a, v_rwkv_r_k, v_rwkv_ln_w, v_rwkv_ln_b, v_w_out, v_norm_x_g, v_norm_mem_g, v_xattn_wq, v_xattn_wk, v_xattn_wv, v_xattn_wo, v_norm_ffn_g, v_ffn_w1, v_ffn_w2, v_final_norm_g):
    return _step(dict(locals()))
```

```python
import functools
import math

import jax
import jax.numpy as jnp
from jax import lax
from jax.experimental import pallas as pl
from jax.experimental.pallas import tpu as pltpu

F32 = jnp.float32
BF16 = jnp.bfloat16
HIGHEST = lax.Precision.HIGHEST
MESH_ID = pl.DeviceIdType.MESH

NORM_EPS = 1e-6
RWKV_LN_EPS = 64e-5
HEAD_DIM = 64
PAIR = 2 * HEAD_DIM
LANES = 128
SSD_STATE = 128
SSD_CHUNK = 128
SSD_GROUPS = 2
SSD_CONV = 4
RWKV_CHUNK = 64
HALO = 8
ROW_TILE = 128
PAIRS_PER_STEP = 4
XATTN_HEADS = 4
RWKV_PASSES = 1
VMEM_LIMIT = 56 * 1024 * 1024

ADAM_LR = 0.001
ADAM_B1 = 0.9
ADAM_B2 = 0.999
ADAM_EPS = 1e-08
ADAM_WD = 0.01
ADAM_STEP = 10


def _dims(ca, cb):
    return (((ca,), (cb,)), ((), ()))


def _split_bf16(a):
    hi = a.astype(BF16)
    lo = (a - hi.astype(F32)).astype(BF16)
    return hi, lo


def _mm_impl(a, b, ca, cb, passes):
    dn = _dims(ca, cb)
    if passes == 1:
        return lax.dot_general(a.astype(BF16), b.astype(BF16), dn, preferred_element_type=F32)
    ah, al = _split_bf16(a)
    bh, bl = _split_bf16(b)
    out = lax.dot_general(ah, bh, dn, preferred_element_type=F32)
    out = out + lax.dot_general(ah, bl, dn, preferred_element_type=F32)
    return out + lax.dot_general(al, bh, dn, preferred_element_type=F32)


@functools.partial(jax.custom_vjp, nondiff_argnums=(2, 3, 4))
def mm(a, b, ca, cb, passes):
    return _mm_impl(a, b, ca, cb, passes)


def _mm_fwd(a, b, ca, cb, passes):
    return _mm_impl(a, b, ca, cb, passes), (a, b)


def _mm_bwd(ca, cb, passes, res, g):
    a, b = res
    da = mm(g, b, 1, 1 - cb, passes) if ca == 1 else mm(b, g, 1 - cb, 1, passes)
    db = mm(a, g, 1 - ca, 0, passes) if cb == 0 else mm(g, a, 0, 1 - ca, passes)
    return da, db


mm.defvjp(_mm_fwd, _mm_bwd)


def _dot_exact(a, b):
    return lax.dot_general(a, b, _dims(1, 0), precision=HIGHEST, preferred_element_type=F32)


def _iota(shape, dim):
    return lax.broadcasted_iota(jnp.int32, shape, dim)


def _sigmoid(x):
    return 1.0 / (1.0 + jnp.exp(-x))


def _silu(x):
    return x * _sigmoid(x)


def _softplus(x):
    return jnp.maximum(x, 0.0) + jnp.log(1.0 + jnp.exp(-jnp.abs(x)))


def _rms(x, g):
    return x * lax.rsqrt(jnp.mean(x * x, axis=-1, keepdims=True) + NORM_EPS) * g


def _head_sum(x):
    n = x.shape[1]
    sel = (_iota((n, LANES), 0) // HEAD_DIM == _iota((n, LANES), 1)).astype(F32)
    return _dot_exact(x, sel)


def _head_expand(s, n):
    sel = (_iota((LANES, n), 1) // HEAD_DIM == _iota((LANES, n), 0)).astype(F32)
    return _dot_exact(s, sel)


def _row_vector_expand(v, n):
    v8 = jnp.broadcast_to(v, (8, LANES))
    return jnp.sum(_head_expand(v8, n), axis=0, keepdims=True) * 0.125


def _shift_rows(u, halo, s):
    tr = u.shape[0]
    sm = (_iota((tr, tr), 1) == _iota((tr, tr), 0) - s).astype(F32)
    out = _dot_exact(sm, u)
    row = _iota((tr, 1), 0)
    hrow = _iota((HALO, 1), 0)
    for r in range(s):
        src = jnp.sum(jnp.where(hrow == HALO - s + r, halo, 0.0), axis=0, keepdims=True)
        out = out + jnp.where(row == r, src, 0.0)
    return out


def _params(sem):
    return pltpu.CompilerParams(dimension_semantics=sem, vmem_limit_bytes=VMEM_LIMIT)


def row_call(name, body, n_tiles, tiled, full, out_tiled, out_acc):
    nt, nf, no, na = len(tiled), len(full), len(out_tiled), len(out_acc)

    def kern(*refs):
        tv = [r[...] for r in refs[:nt]]
        fv = [r[...] for r in refs[nt:nt + nf]]
        outs, accs = body(tv, fv)
        for r, v in zip(refs[nt + nf:nt + nf + no], outs):
            r[...] = v.astype(r.dtype)
        if na:
            a_refs = refs[nt + nf + no:]
            first = pl.program_id(0) == 0

            @pl.when(first)
            def _():
                for r, v in zip(a_refs, accs):
                    r[...] = v

            @pl.when(jnp.logical_not(first))
            def _():
                for r, v in zip(a_refs, accs):
                    r[...] += v

    in_specs = [pl.BlockSpec((rt, w), functools.partial(lambda i, cb: (i, cb), cb=cb)) for (_, rt, w, cb) in tiled]
    in_specs += [pl.BlockSpec(a.shape, lambda i: (0, 0)) for a in full]
    out_specs = [pl.BlockSpec((rt, w), lambda i: (i, 0)) for (_, rt, w, _) in out_tiled]
    out_specs += [pl.BlockSpec(s, lambda i: (0, 0)) for s in out_acc]
    out_shape = [jax.ShapeDtypeStruct((rows, w), dt) for (rows, _, w, dt) in out_tiled]
    out_shape += [jax.ShapeDtypeStruct(s, F32) for s in out_acc]
    res = pl.pallas_call(
        kern, name=name, grid=(n_tiles,), in_specs=in_specs, out_specs=out_specs, out_shape=out_shape,
        compiler_params=_params(("arbitrary",)),
    )(*[t[0] for t in tiled], *full)
    return list(res[:no]), list(res[no:])


def _pick(dim, cands):
    for c in cands:
        if dim % c == 0:
            return c
    return dim


def matmul(name, a, b, tb=False, resid=None, out_dtype=F32):
    m, k = a.shape
    n = b.shape[0] if tb else b.shape[1]
    tm = _pick(m, (512, 256, 128))
    tn = _pick(n, (1024, 896, 768, 512, 384, 256, 128))
    tk = _pick(k, (1024, 896, 768, 512, 384, 256, 128))
    nk = k // tk
    has_resid = resid is not None

    def kern(*refs):
        a_ref, b_ref = refs[0], refs[1]
        o_ref, acc = refs[-2], refs[-1]
        kk = pl.program_id(2)

        @pl.when(kk == 0)
        def _():
            acc[...] = jnp.zeros_like(acc)

        acc[...] += lax.dot_general(a_ref[...], b_ref[...], _dims(1, 1 if tb else 0), preferred_element_type=F32)

        @pl.when(kk == nk - 1)
        def _():
            out = acc[...]
            if has_resid:
                out = out + refs[2][...]
            o_ref[...] = out.astype(o_ref.dtype)

    in_specs = [pl.BlockSpec((tm, tk), lambda i, j, kk: (i, kk))]
    if tb:
        in_specs.append(pl.BlockSpec((tn, tk), lambda i, j, kk: (j, kk)))
    else:
        in_specs.append(pl.BlockSpec((tk, tn), lambda i, j, kk: (kk, j)))
    args = [a, b]
    if has_resid:
        in_specs.append(pl.BlockSpec((tm, tn), lambda i, j, kk: (i, j)))
        args.append(resid)
    return pl.pallas_call(
        kern, name=name, grid=(m // tm, n // tn, nk), in_specs=in_specs,
        out_specs=pl.BlockSpec((tm, tn), lambda i, j, kk: (i, j)),
        out_shape=jax.ShapeDtypeStruct((m, n), out_dtype),
        scratch_shapes=[pltpu.VMEM((tm, tn), F32)],
        compiler_params=_params(("parallel", "parallel", "arbitrary")),
    )(*args)


def norm_fwd(name, x, g, tr):
    def body(tv, fv):
        return [_rms(tv[0], fv[0])], []
    rows, d = x.shape
    (h,), _ = row_call(name, body, rows // tr, [(x, tr, d, 0)], [g], [(rows, tr, d, BF16)], [])
    return h


def norm_bwd(name, x, g, dh, extra, tr):
    def body(tv, fv):
        _, vjp = jax.vjp(_rms, tv[0], fv[0])
        dx, dg = vjp(tv[1])
        if extra is not None:
            dx = dx + tv[2]
        return [dx], [dg]
    rows, d = x.shape
    tiled = [(x, tr, d, 0), (dh, tr, d, 0)] + ([(extra, tr, d, 0)] if extra is not None else [])
    (dx,), (dg,) = row_call(name, body, rows // tr, tiled, [g], [(rows, tr, d, F32)], [g.shape])
    return dx, dg


def _ssd_pre(xbc, halo, dtraw, w0, w1, w2, w3, cb, dtb):
    y = w3 * xbc + w2 * _shift_rows(xbc, halo, 1) + w1 * _shift_rows(xbc, halo, 2) + w0 * _shift_rows(xbc, halo, 3) + cb
    return _silu(y), _softplus(dtraw + dtb)


def _ssd_post(ys, xs, z, dskip, ng):
    w = ys.shape[1]
    y = (ys + xs * _row_vector_expand(dskip, w)) * _silu(z)
    gw = w // SSD_GROUPS
    parts = []
    for gi in range(SSD_GROUPS):
        yg = y[:, gi * gw:(gi + 1) * gw]
        parts.append(yg * lax.rsqrt(jnp.mean(yg * yg, axis=-1, keepdims=True) + NORM_EPS))
    return jnp.concatenate(parts, axis=1) * ng


def _rwkv_pre(urkv, ulora, hrkv, hlora, mu_rkv, mu_lora, w0, a0, kkw, kaw, w2p, a2p, g2):
    w = w0.shape[1]
    urkv = urkv + (_shift_rows(urkv, hrkv, 1) - urkv) * mu_rkv
    ulora = ulora + (_shift_rows(ulora, hlora, 1) - ulora) * mu_lora
    r, k, v = urkv[:, :w], urkv[:, w:2 * w], urkv[:, 2 * w:]
    pw, pa, pg = ulora[:, :LANES], ulora[:, LANES:2 * LANES], ulora[:, 2 * LANES:]
    w_log = -_softplus(-(w0 + mm(jnp.tanh(pw), w2p, 1, 0, 1))) - 0.5
    lw = -jnp.exp(w_log)
    iclr = _sigmoid(a0 + mm(pa, a2p, 1, 0, 1))
    gate = mm(_sigmoid(pg), g2, 1, 0, 1)
    kk = k * kkw
    kk = kk / jnp.maximum(jnp.sqrt(_head_expand(_head_sum(kk * kk), w)), 1e-12)
    k2 = k * (1.0 + (iclr - 1.0) * kaw)
    return r, lw, k2, v, -kk, kk * iclr, gate


def _rwkv_post(ys, r, k2, v, gate, rk, lnw, lnb):
    w = ys.shape[1]
    inv = 1.0 / HEAD_DIM
    mean = _head_expand(_head_sum(ys), w) * inv
    d = ys - mean
    var = _head_expand(_head_sum(d * d), w) * inv
    yn = d * lax.rsqrt(var + RWKV_LN_EPS) * lnw + lnb
    bonus = _head_expand(_head_sum(r * k2 * rk), w) * v
    return (yn + bonus) * gate


def _attn(q, k, v):
    d = q.shape[1]
    hd = d // XATTN_HEADS
    outs = []
    for h in range(XATTN_HEADS):
        sl = slice(h * hd, (h + 1) * hd)
        s = mm(q[:, sl], k[:, sl], 1, 1, 1) * (hd ** -0.5)
        s = s - jnp.max(s, axis=-1, keepdims=True)
        p = jnp.exp(s)
        p = p / jnp.sum(p, axis=-1, keepdims=True)
        outs.append(mm(p, v[:, sl], 1, 0, 1))
    return jnp.concatenate(outs, axis=1)


def _relu2(a):
    return jnp.square(jnp.maximum(a, 0.0))


def fn_fwd(name, fn, n_tiles, tiled, full, out_tiled):
    def body(tv, fv):
        outs = fn(*tv, *fv)
        return (list(outs) if isinstance(outs, (tuple, list)) else [outs]), []
    outs, _ = row_call(name, body, n_tiles, tiled, full, out_tiled, [])
    return outs


def fn_bwd(name, fn, n_tiles, tiled, full, cts, ct_fn, out_tiled):
    nt = len(tiled)

    def body(tv, fv):
        outs, vjp = jax.vjp(fn, *tv[:nt], *fv)
        ct = ct_fn(tv[nt:])
        grads = vjp(tuple(ct) if isinstance(outs, (tuple, list)) else ct[0])
        return list(grads[:nt]), list(grads[nt:])
    return row_call(name, body, n_tiles, tiled + cts, full, out_tiled, [f.shape for f in full])


def _ssd_chunk(xs, bm, cm, dt_all, a_log, ht, p):
    q = xs.shape[0]
    lane = _iota((1, LANES), 1)
    row = _iota((q, 1), 0)
    tril = _iota((q, q), 0) >= _iota((q, q), 1)
    half = lane < HEAD_DIM
    da = dt_all * (-jnp.exp(a_log))
    cs = _dot_exact(tril.astype(F32), da)

    def col(mat, h):
        return jnp.sum(jnp.where(lane == h, mat, 0.0), axis=1, keepdims=True)

    cs0, cs1 = col(cs, 2 * p), col(cs, 2 * p + 1)
    xdt = xs * jnp.where(half, col(dt_all, 2 * p), col(dt_all, 2 * p + 1))
    csx = jnp.where(half, cs0, cs1)
    last = jnp.sum(jnp.where(row == q - 1, csx, 0.0), axis=0, keepdims=True)
    cb = mm(cm, bm, 1, 1, 1)
    y = mm(cm, ht, 1, 0, 1) * jnp.exp(csx)
    for csh, hm in ((cs0, half), (cs1, jnp.logical_not(half))):
        csl = jnp.broadcast_to(csh, (q, q))
        seg = csl - csl.T
        lmat = jnp.where(tril, jnp.exp(jnp.where(tril, seg, 0.0)), 0.0)
        y = y + jnp.where(hm, mm(cb * lmat, xdt, 1, 0, 1), 0.0)
    st = mm(bm, xdt * jnp.exp(last - csx), 0, 0, 1)
    return y, ht * jnp.exp(last) + st


def _rwkv_chunks(pairs):
    c = pairs[0][0].shape[0]
    ps = RWKV_PASSES
    lane = _iota((1, LANES), 1)
    row = _iota((c, 1), 0)
    ri, ci = _iota((c, c), 0), _iota((c, c), 1)
    tril_i, tril_s = ri >= ci, ri > ci
    eye = (ri == ci).astype(F32)
    half = lane < HEAD_DIM
    halves = (half, jnp.logical_not(half))
    bd = (_iota((LANES, LANES), 0) < HEAD_DIM) == (_iota((LANES, LANES), 1) < HEAD_DIM)
    tri = tril_i.astype(F32)
    n = len(pairs)
    heads = [(j, hm) for j in range(n) for hm in halves]

    cum = [_dot_exact(tri, p[1]) for p in pairs]
    at = [p[4] * jnp.exp(cm - p[1]) for p, cm in zip(pairs, cum)]
    en = [jnp.exp(-cm) for cm in cum]
    bt = [p[5] * e for p, e in zip(pairs, en)]
    kt = [p[2] * e for p, e in zip(pairs, en)]
    rt = [p[0] * jnp.exp(cm) for p, cm in zip(pairs, cum)]
    ah = [mm(at[j], pairs[j][6], 1, 1, ps) for j in range(n)]
    y = [mm(rt[j], pairs[j][6], 1, 1, ps) for j in range(n)]
    atm = [jnp.where(hm, at[j], 0.0) for j, hm in heads]
    rtm = [jnp.where(hm, rt[j], 0.0) for j, hm in heads]
    aab = [jnp.where(tril_s, mm(atm[i], bt[j], 1, 1, ps), 0.0) for i, (j, _) in enumerate(heads)]
    aak = [jnp.where(tril_s, mm(atm[i], kt[j], 1, 1, ps), 0.0) for i, (j, _) in enumerate(heads)]
    arb = [jnp.where(tril_i, mm(rtm[i], bt[j], 1, 1, ps), 0.0) for i, (j, _) in enumerate(heads)]
    ark = [jnp.where(tril_i, mm(rtm[i], kt[j], 1, 1, ps), 0.0) for i, (j, _) in enumerate(heads)]
    rhs = [ah[j] + mm(aak[i], pairs[j][3], 1, 0, ps) for i, (j, _) in enumerate(heads)]
    yv = [mm(ark[i], pairs[j][3], 1, 0, ps) for i, (j, _) in enumerate(heads)]
    tm = [eye + a_ for a_ in aab]
    pm = aab
    for _ in range(int(math.log2(c)) - 1):
        pm = [mm(p_, p_, 1, 0, ps) for p_ in pm]
        tm = [t_ + mm(t_, p_, 1, 0, ps) for t_, p_ in zip(tm, pm)]
    uh = [mm(tm[i], rhs[i], 1, 0, ps) for i in range(len(heads))]
    u = [jnp.where(half, uh[2 * j], uh[2 * j + 1]) for j in range(n)]
    yu = [mm(arb[i], u[j], 1, 0, ps) for i, (j, _) in enumerate(heads)]
    out = []
    for j in range(n):
        yj = y[j] + jnp.where(half, yu[2 * j] + yv[2 * j], yu[2 * j + 1] + yv[2 * j + 1])
        plast = jnp.sum(jnp.where(row == c - 1, cum[j], 0.0), axis=0, keepdims=True)
        upd = pairs[j][6] + mm(u[j], bt[j], 0, 0, ps) + mm(pairs[j][3], kt[j], 0, 0, ps)
        out.append((yj, jnp.where(bd, upd * jnp.exp(plast), 0.0)))
    return out


def _seq_spec(chunk, ppb, col, row_of):
    if col is None:
        return pl.BlockSpec((chunk, ppb * LANES), lambda pb, i: (row_of(i), pb))
    return pl.BlockSpec((chunk, LANES), lambda pb, i: (row_of(i), col(pb * ppb)))


def _pair_vals(refs, seq_in, j):
    return [r[...] if col is not None else r[:, j * LANES:(j + 1) * LANES] for r, (_, col) in zip(refs, seq_in)]


def scan_fwd(name, chunk_fn, chunk, seq_in, const_in, n_pairs, ppb):
    t = seq_in[0][0].shape[0]
    nc = t // chunk
    ns, ncst = len(seq_in), len(const_in)

    def kern(*refs):
        y_ref, st_ref, ht = refs[ns + ncst], refs[ns + ncst + 1], refs[ns + ncst + 2]

        @pl.when(pl.program_id(1) == 0)
        def _():
            ht[...] = jnp.zeros_like(ht)

        cv = [r[...] for r in refs[ns:ns + ncst]]
        h0 = [ht[j] for j in range(ppb)]
        for j in range(ppb):
            st_ref[j] = h0[j]
        sv = [_pair_vals(refs[:ns], seq_in, j) for j in range(ppb)]
        outs = chunk_fn(sv, cv, h0, [pl.program_id(0) * ppb + j for j in range(ppb)])
        for j, (y, hn) in enumerate(outs):
            y_ref[:, j * LANES:(j + 1) * LANES] = y
            ht[j] = hn

    in_specs = [_seq_spec(chunk, ppb, col, lambda i: i) for (_, col) in seq_in]
    in_specs += [pl.BlockSpec(a.shape, lambda pb, i: (0, 0)) for a in const_in]
    return pl.pallas_call(
        kern, name=name, grid=(n_pairs // ppb, nc), in_specs=in_specs,
        out_specs=[pl.BlockSpec((chunk, ppb * LANES), lambda pb, i: (i, pb)),
                   pl.BlockSpec((ppb, None, LANES, LANES), lambda pb, i: (pb, i, 0, 0))],
        out_shape=[jax.ShapeDtypeStruct((t, n_pairs * LANES), F32), jax.ShapeDtypeStruct((n_pairs, nc, LANES, LANES), F32)],
        scratch_shapes=[pltpu.VMEM((ppb, LANES, LANES), F32)],
        compiler_params=_params(("arbitrary", "arbitrary")),
    )(*[s[0] for s in seq_in], *const_in)


def scan_bwd(name, chunk_fn, chunk, seq_in, const_in, states, dy, n_pairs, ppb):
    t = dy.shape[0]
    nc = t // chunk
    ns, ncst = len(seq_in), len(const_in)

    def kern(*refs):
        seq_refs, cst_refs = refs[:ns], refs[ns:ns + ncst]
        st_ref, dy_ref = refs[ns + ncst], refs[ns + ncst + 1]
        o = ns + ncst + 2
        dseq_refs, dcst_refs, dht = refs[o:o + ns], refs[o + ns:o + ns + ncst], refs[o + ns + ncst]
        pb, i = pl.program_id(0), pl.program_id(1)

        @pl.when(i == 0)
        def _():
            dht[...] = jnp.zeros_like(dht)

        ids = [pb * ppb + j for j in range(ppb)]
        lanes = [slice(j * LANES, (j + 1) * LANES) for j in range(ppb)]

        def fn(*flat):
            sv = [list(flat[j * ns:(j + 1) * ns]) for j in range(ppb)]
            outs = chunk_fn(sv, list(flat[ppb * ns:ppb * ns + ncst]), list(flat[ppb * ns + ncst:]), ids)
            return tuple(y for y, _ in outs), tuple(h for _, h in outs)

        flat_in = [v for j in range(ppb) for v in _pair_vals(seq_refs, seq_in, j)]
        flat_in += [r[...] for r in cst_refs] + [st_ref[j] for j in range(ppb)]
        _, vjp = jax.vjp(fn, *flat_in)
        grads = vjp((tuple(dy_ref[:, ln] for ln in lanes), tuple(dht[j] for j in range(ppb))))
        for j in range(ppb):
            for r, g in zip(dseq_refs, grads[j * ns:(j + 1) * ns]):
                r[:, lanes[j]] = g
            dht[j] = grads[ppb * ns + ncst + j]
        dcv = grads[ppb * ns:ppb * ns + ncst]
        if ncst:
            first = jnp.logical_and(pb == 0, i == 0)

            @pl.when(first)
            def _():
                for r, g in zip(dcst_refs, dcv):
                    r[...] = g

            @pl.when(jnp.logical_not(first))
            def _():
                for r, g in zip(dcst_refs, dcv):
                    r[...] += g

    rev = lambda i: nc - 1 - i
    wide = pl.BlockSpec((chunk, ppb * LANES), lambda pb, i: (rev(i), pb))
    in_specs = [_seq_spec(chunk, ppb, col, rev) for (_, col) in seq_in]
    in_specs += [pl.BlockSpec(a.shape, lambda pb, i: (0, 0)) for a in const_in]
    in_specs += [pl.BlockSpec((ppb, None, LANES, LANES), lambda pb, i: (pb, rev(i), 0, 0)), wide]
    out_specs = [wide for _ in seq_in]
    out_specs += [pl.BlockSpec(a.shape, lambda pb, i: (0, 0)) for a in const_in]
    out_shape = [jax.ShapeDtypeStruct((t, n_pairs * LANES), F32) for _ in seq_in]
    out_shape += [jax.ShapeDtypeStruct(a.shape, F32) for a in const_in]
    res = pl.pallas_call(
        kern, name=name, grid=(n_pairs // ppb, nc), in_specs=in_specs, out_specs=out_specs, out_shape=out_shape,
        scratch_shapes=[pltpu.VMEM((ppb, LANES, LANES), F32)],
        compiler_params=_params(("arbitrary", "arbitrary")),
    )(*[s[0] for s in seq_in], *const_in, states, dy)
    return list(res[:ns]), list(res[ns:])


def loss_head(x3, tgt, g, tr):
    rows, d = x3.shape

    def body(tv, fv):
        def f(x, gg):
            e = jnp.square(_rms(x, gg) - tv[1])
            return 0.5 * jnp.sum(jnp.mean(e, axis=-1, keepdims=True), axis=0, keepdims=True)
        l, vjp = jax.vjp(f, tv[0], fv[0])
        dx, dg = vjp(jnp.ones((1, 1), F32))
        return [dx], [dg, jnp.broadcast_to(l, (8, LANES))]
    (dx,), (dg, l) = row_call("loss_head", body, rows // tr, [(x3, tr, d, 0), (tgt, tr, d, 0)], [g],
                              [(rows, tr, d, F32)], [g.shape, (8, LANES)])
    return dx, dg, l


def _adam_math(w, g, m, v):
    m = ADAM_B1 * m + (1.0 - ADAM_B1) * g
    v = ADAM_B2 * v + (1.0 - ADAM_B2) * jnp.square(g)
    m_hat = m / (1.0 - ADAM_B1 ** ADAM_STEP)
    v_hat = v / (1.0 - ADAM_B2 ** ADAM_STEP)
    delta = -ADAM_LR * (m_hat / (jnp.sqrt(v_hat) + ADAM_EPS) + ADAM_WD * w)
    return delta, m, v


def adamw(name, w, m, v, g_parts):
    rows, cols = w.shape
    tr = _pick(rows, (256, 128, 64, 32, 16, 8))
    n_g = len(g_parts)

    def body(tv, fv):
        g = tv[3]
        for extra in tv[4:4 + n_g - 1]:
            g = g + extra
        delta, mn, vn = _adam_math(tv[0], g, tv[1], tv[2])
        return [g, delta, mn, vn], []
    tiled = [(a, tr, cols, 0) for a in (w, m, v, *g_parts)]
    outs, _ = row_call(name, body, rows // tr, tiled, [], [(rows, tr, cols, F32)] * 4, [])
    return outs


def sum_slots(name, r):
    _, rows, cols = r.shape
    tr = _pick(rows, (256, 128, 64, 32, 16, 8))

    def kern(r0, r1, r2, r3, o):
        o[...] = ((r0[...].astype(F32) + r1[...].astype(F32)) + r2[...].astype(F32)) + r3[...].astype(F32)

    in_specs = [pl.BlockSpec((None, tr, cols), functools.partial(lambda i, s: (s, i, 0), s=s)) for s in range(4)]
    return pl.pallas_call(
        kern, name=name, grid=(rows // tr,), in_specs=in_specs, out_specs=pl.BlockSpec((tr, cols), lambda i: (i, 0)),
        out_shape=jax.ShapeDtypeStruct((rows, cols), F32), compiler_params=_params(("arbitrary",)),
    )(r, r, r, r)


def _my_place():
    return lax.axis_index("x"), lax.axis_index("y"), lax.axis_index("c")


def chip_exchange(name, arrays, gather):
    nw = len(arrays)
    ANY = pl.BlockSpec(memory_space=pl.ANY)

    def body(*refs):
        ins, outs = refs[:nw], refs[nw:2 * nw]
        send, recv, loc = refs[2 * nw:]
        x, y, c = _my_place()
        q = 2 * x + y
        peers = [(1 - x, y), (x, 1 - y), (1 - x, 1 - y)]

        def src(w, dest_chip):
            return ins[w] if gather else ins[w].at[dest_chip]

        def remote(w, j):
            px, py = peers[j]
            return pltpu.make_async_remote_copy(
                src_ref=src(w, 2 * px + py), dst_ref=outs[w].at[q], send_sem=send.at[w, j], recv_sem=recv.at[w, j],
                device_id=(px, py, c), device_id_type=MESH_ID)

        def arrival(w, j):
            px, py = peers[j]
            return pltpu.make_async_remote_copy(
                src_ref=src(w, q), dst_ref=outs[w].at[2 * px + py], send_sem=send.at[w, j], recv_sem=recv.at[w, j],
                device_id=(px, py, c), device_id_type=MESH_ID)

        local = [pltpu.make_async_copy(src(w, q), outs[w].at[q], loc.at[w]) for w in range(nw)]
        sends = [[remote(w, j) for j in range(3)] for w in range(nw)]
        for w in range(nw):
            local[w].start()
            for j in range(3):
                sends[w][j].start()
        for w in range(nw):
            local[w].wait()
            for j in range(3):
                sends[w][j].wait_send()
                arrival(w, j).wait_recv()

    out_shape = [jax.ShapeDtypeStruct((4,) + (a.shape if gather else a.shape[1:]), a.dtype) for a in arrays]
    return pl.pallas_call(
        body, name=name, in_specs=[ANY] * nw, out_specs=[ANY] * nw, out_shape=out_shape,
        scratch_shapes=[pltpu.SemaphoreType.DMA((nw, 3)), pltpu.SemaphoreType.DMA((nw, 3)), pltpu.SemaphoreType.DMA((nw,))],
        compiler_params=pltpu.CompilerParams(has_side_effects=True),
    )(*arrays)


def gather_two_level(name, arrays):
    nw = len(arrays)
    ANY = pl.BlockSpec(memory_space=pl.ANY)

    def body(*refs):
        ins, outs = refs[:nw], refs[nw:2 * nw]
        send, recv, loc = refs[2 * nw:]
        x, y, c = _my_place()
        q = 2 * x + y
        me, sibling = (x, y, c), (x, y, 1 - c)
        peers = [(1 - x, y), (x, 1 - y), (1 - x, 1 - y)]
        chips = [2 * px + py for px, py in peers]

        def mine(w):
            hr = ins[w].shape[0] // 2
            return ins[w].at[pl.ds(c * hr, hr)]

        def copy(w, k, src, chip, half, to):
            return pltpu.make_async_remote_copy(
                src_ref=src, dst_ref=outs[w].at[chip, half], send_sem=send.at[w, k], recv_sem=recv.at[w, k],
                device_id=to, device_id_type=MESH_ID)

        local = [pltpu.make_async_copy(mine(w), outs[w].at[q, c], loc.at[w]) for w in range(nw)]
        first = [[copy(w, 0, mine(w), q, c, sibling)] + [copy(w, 1 + j, mine(w), q, c, (*peers[j], c)) for j in range(3)]
                 for w in range(nw)]
        for w in range(nw):
            local[w].start()
            for cp in first[w]:
                cp.start()
        passed = []
        for w in range(nw):
            for j in range(3):
                copy(w, 1 + j, mine(w), chips[j], c, me).wait_recv()
                fwd = copy(w, 4 + j, outs[w].at[chips[j], c], chips[j], c, sibling)
                fwd.start()
                passed.append(fwd)
        for w in range(nw):
            copy(w, 0, mine(w), q, 1 - c, me).wait_recv()
            for j in range(3):
                copy(w, 4 + j, mine(w), chips[j], 1 - c, me).wait_recv()
        for w in range(nw):
            local[w].wait()
            for cp in first[w]:
                cp.wait_send()
        for cp in passed:
            cp.wait_send()

    out_shape = [jax.ShapeDtypeStruct((4, 2, a.shape[0] // 2, a.shape[1]), a.dtype) for a in arrays]
    return pl.pallas_call(
        body, name=name, in_specs=[ANY] * nw, out_specs=[ANY] * nw, out_shape=out_shape,
        scratch_shapes=[pltpu.SemaphoreType.DMA((nw, 7)), pltpu.SemaphoreType.DMA((nw, 7)), pltpu.SemaphoreType.DMA((nw,))],
        compiler_params=pltpu.CompilerParams(has_side_effects=True),
    )(*arrays)


def swap_halves(name, arrays):
    nw = len(arrays)
    ANY = pl.BlockSpec(memory_space=pl.ANY)

    def body(*refs):
        ins, kept, got = refs[:nw], refs[nw:2 * nw], refs[2 * nw:3 * nw]
        send, recv, loc = refs[3 * nw:]
        x, y, c = _my_place()
        local = [pltpu.make_async_copy(ins[w].at[:, c], kept[w], loc.at[w]) for w in range(nw)]
        copies = [pltpu.make_async_remote_copy(
            src_ref=ins[w].at[:, 1 - c], dst_ref=got[w], send_sem=send.at[w], recv_sem=recv.at[w],
            device_id=(x, y, 1 - c), device_id_type=MESH_ID) for w in range(nw)]
        for lc, cp in zip(local, copies):
            cp.start()
            lc.start()
        for lc, cp in zip(local, copies):
            lc.wait()
            cp.wait_send()
            cp.wait_recv()

    half = [jax.ShapeDtypeStruct((4,) + a.shape[2:], a.dtype) for a in arrays]
    res = pl.pallas_call(
        body, name=name, in_specs=[ANY] * nw, out_specs=[ANY] * (2 * nw), out_shape=half + half,
        scratch_shapes=[pltpu.SemaphoreType.DMA((nw,)), pltpu.SemaphoreType.DMA((nw,)), pltpu.SemaphoreType.DMA((nw,))],
        compiler_params=pltpu.CompilerParams(has_side_effects=True),
    )(*arrays)
    return res[:nw], res[nw:]


def core_gather(name, arrays):
    nw = len(arrays)
    ANY = pl.BlockSpec(memory_space=pl.ANY)

    def body(*refs):
        ins, outs = refs[:nw], refs[nw:2 * nw]
        send, recv, loc = refs[2 * nw:]
        x, y, c = _my_place()
        local = [pltpu.make_async_copy(ins[w], outs[w].at[c], loc.at[w]) for w in range(nw)]
        copies = [pltpu.make_async_remote_copy(
            src_ref=ins[w], dst_ref=outs[w].at[c], send_sem=send.at[w], recv_sem=recv.at[w],
            device_id=(x, y, 1 - c), device_id_type=MESH_ID) for w in range(nw)]
        for lc, cp in zip(local, copies):
            cp.start()
            lc.start()
        for w in range(nw):
            local[w].wait()
            copies[w].wait_send()
            pltpu.make_async_remote_copy(
                src_ref=ins[w], dst_ref=outs[w].at[1 - c], send_sem=send.at[w], recv_sem=recv.at[w],
                device_id=(x, y, 1 - c), device_id_type=MESH_ID).wait_recv()

    return pl.pallas_call(
        body, name=name, in_specs=[ANY] * nw, out_specs=[ANY] * nw,
        out_shape=[jax.ShapeDtypeStruct((2,) + a.shape, a.dtype) for a in arrays],
        scratch_shapes=[pltpu.SemaphoreType.DMA((nw,)), pltpu.SemaphoreType.DMA((nw,)), pltpu.SemaphoreType.DMA((nw,))],
        compiler_params=pltpu.CompilerParams(has_side_effects=True),
    )(*arrays)


def all_reduce_small(name, v):
    rows = v.shape[0]
    VM = pl.BlockSpec(memory_space=pltpu.VMEM)

    def body(v_ref, o_ref, buf, send, recv):
        x, y, c = _my_place()
        me = 4 * x + 2 * y + c

        def peer(kx):
            return (x ^ ((kx >> 2) & 1), y ^ ((kx >> 1) & 1), c ^ (kx & 1))

        def copy(kx, slot):
            return pltpu.make_async_remote_copy(
                src_ref=v_ref, dst_ref=buf.at[slot], send_sem=send.at[kx - 1], recv_sem=recv.at[kx - 1],
                device_id=peer(kx), device_id_type=MESH_ID)

        sends = [copy(kx, me) for kx in range(1, 8)]
        for cp in sends:
            cp.start()
        buf[me] = v_ref[...]
        for kx in range(1, 8):
            copy(kx, me ^ kx).wait_recv()
        for cp in sends:
            cp.wait_send()
        acc = buf[0]
        for d in range(1, 8):
            acc = acc + buf[d]
        o_ref[...] = acc

    return pl.pallas_call(
        body, name=name, in_specs=[VM], out_specs=VM, out_shape=jax.ShapeDtypeStruct(v.shape, F32),
        scratch_shapes=[pltpu.VMEM((8, rows, LANES), F32), pltpu.SemaphoreType.DMA((7,)), pltpu.SemaphoreType.DMA((7,))],
        compiler_params=pltpu.CompilerParams(has_side_effects=True, vmem_limit_bytes=VMEM_LIMIT),
    )(v)


def _pad_cols(a, n):
    return jnp.pad(a, ((0, 0), (0, n - a.shape[1])))


def _pad_rows(a, n):
    return jnp.pad(a, ((0, n - a.shape[0]), (0, 0)))


def _halo(u, tr):
    t, cdim = u.shape
    tails = u.reshape(t // tr, tr, cdim)[:, tr - HALO:, :]
    tails = jnp.concatenate([jnp.zeros((1, HALO, cdim), u.dtype), tails[:-1]], axis=0)
    return tails.reshape(-1, cdim)


def _unhalo(du, dhalo, tr):
    t, cdim = du.shape
    n = t // tr
    dh = dhalo.reshape(n, HALO, cdim)
    dh = jnp.concatenate([dh[1:], jnp.zeros((1, HALO, cdim), du.dtype)], axis=0)
    d3 = du.reshape(n, tr, cdim)
    d3 = jnp.concatenate([d3[:, :tr - HALO, :], d3[:, tr - HALO:, :] + dh], axis=1)
    return d3.reshape(t, cdim)


def _to_slots(g, axis):
    r, cdim = g.shape
    if axis == 0:
        return g.reshape(4, r // 4, cdim)
    return g.reshape(r, 4, cdim // 4).transpose(1, 0, 2)


def _from_slots(s, axis):
    if axis == 0:
        return s.reshape(s.shape[0] * s.shape[1], s.shape[2])
    return s.transpose(1, 0, 2).reshape(s.shape[1], 4 * s.shape[2])


BIG = ("w_in", "w_out", "xattn_wq", "xattn_wk", "xattn_wv", "xattn_wo", "ffn_w1", "ffn_w2")
BIG_AXIS = {"w_in": 1, "w_out": 0, "xattn_wq": 0, "xattn_wk": 0, "xattn_wv": 0, "xattn_wo": 0, "ffn_w1": 1, "ffn_w2": 0}
SMALL_SHARDED = ("ssd_conv_w", "rwkv_w2", "rwkv_a2", "rwkv_g2")
WEIGHTS = ("norm_mix_g", "w_in", "ssd_conv_w", "ssd_conv_b", "ssd_dt_bias", "ssd_a_log", "ssd_d", "ssd_norm_g",
           "rwkv_mu", "rwkv_w0", "rwkv_w2", "rwkv_a0", "rwkv_a2", "rwkv_g2", "rwkv_k_k", "rwkv_k_a", "rwkv_r_k",
           "rwkv_ln_w", "rwkv_ln_b", "w_out", "norm_x_g", "norm_mem_g", "xattn_wq", "xattn_wk", "xattn_wv", "xattn_wo",
           "norm_ffn_g", "ffn_w1", "ffn_w2", "final_norm_g")


def _local_grads(x, mem, tgt, wt, full):
    t, d = x.shape
    w = d // 2
    nh = w // HEAD_DIM
    n_pairs = nh // 2
    ppg = n_pairs // SSD_GROUPS
    bc = SSD_GROUPS * SSD_STATE
    conv_dim = w + 2 * bc
    tr = ROW_TILE
    nt = t // tr
    dr = wt["rwkv_w2"].shape[0]
    ar = wt["rwkv_a2"].shape[0]
    gr = wt["rwkv_g2"].shape[0]

    w_in = full["w_in"]
    o = 0
    segs = {}
    for nm, width in (("z", w), ("xbc", conv_dim), ("dt", nh), ("rkv", 3 * w), ("pw", dr), ("pa", ar), ("pg", gr)):
        segs[nm] = (o, width)
        o += width
    padded = {"z": w, "xbc": conv_dim, "dt": LANES, "rkv": 3 * w, "pw": LANES, "pa": LANES, "pg": gr}
    order = ("z", "xbc", "dt", "rkv", "pw", "pa", "pg")
    w_perm = jnp.concatenate([_pad_cols(w_in[:, segs[nm][0]:segs[nm][0] + segs[nm][1]], padded[nm]) for nm in order], axis=1)
    offs = {}
    o = 0
    for nm in order:
        offs[nm] = o
        o += padded[nm]
    n_perm = o
    lora_w = 2 * LANES + gr

    def seg_cols(a, nm, width=None):
        return a[:, offs[nm]:offs[nm] + (padded[nm] if width is None else width)]

    mu = wt["rwkv_mu"]
    mo = 3 * w
    mu_rkv = mu[:, :mo]
    mu_lora = jnp.concatenate([_pad_cols(mu[:, mo:mo + dr], LANES), _pad_cols(mu[:, mo + dr:mo + dr + ar], LANES),
                               mu[:, mo + dr + ar:]], axis=1)
    w2p = _pad_rows(full["rwkv_w2"], LANES)
    a2p = _pad_rows(full["rwkv_a2"], LANES)
    g2 = full["rwkv_g2"]
    conv_w = full["ssd_conv_w"]
    cw = [conv_w[i:i + 1] for i in range(SSD_CONV)]
    dt_bias = _pad_cols(wt["ssd_dt_bias"], LANES)
    a_log = _pad_cols(wt["ssd_a_log"], LANES)
    d_skip = _pad_cols(wt["ssd_d"], LANES)
    r_k = wt["rwkv_r_k"].reshape(1, w)

    h1 = norm_fwd("norm_mix", x, wt["norm_mix_g"], tr)
    u = matmul("in_proj", h1, w_perm)
    z, xbc, dtraw = seg_cols(u, "z"), seg_cols(u, "xbc"), seg_cols(u, "dt")
    urkv = seg_cols(u, "rkv")
    ulora = u[:, offs["pw"]:offs["pw"] + lora_w]

    halo_xbc = _halo(xbc, tr)
    ssd_pre_t = [(xbc, tr, conv_dim, 0), (halo_xbc, HALO, conv_dim, 0), (dtraw, tr, LANES, 0)]
    ssd_pre_f = cw + [wt["ssd_conv_b"], dt_bias]
    act, dt = fn_fwd("ssd_pre", _ssd_pre, nt, ssd_pre_t, ssd_pre_f, [(t, tr, conv_dim, F32), (t, tr, LANES, F32)])

    nb = w // LANES
    ssd_seq = [(act, None), (act, lambda p: nb + p // ppg), (act, lambda p: nb + SSD_GROUPS + p // ppg), (dt, lambda p: 0)]
    ssd_ppb = min(ppg, PAIRS_PER_STEP)
    rw_ppb = min(n_pairs, PAIRS_PER_STEP)

    def ssd_fn(sv, cv, hts, ids):
        return [_ssd_chunk(*s, cv[0], ht, p) for s, ht, p in zip(sv, hts, ids)]

    y_scan, ssd_states = scan_fwd("ssd_scan", ssd_fn, SSD_CHUNK, ssd_seq, [a_log], n_pairs, ssd_ppb)
    ssd_post_t = [(y_scan, tr, w, 0), (act, tr, w, 0), (z, tr, w, 0)]
    ssd_post_f = [d_skip, wt["ssd_norm_g"]]
    (y_ssd,) = fn_fwd("ssd_post", _ssd_post, nt, ssd_post_t, ssd_post_f, [(t, tr, w, F32)])

    halo_rkv, halo_lora = _halo(urkv, tr), _halo(ulora, tr)
    rw_pre_t = [(urkv, tr, 3 * w, 0), (ulora, tr, lora_w, 0), (halo_rkv, HALO, 3 * w, 0), (halo_lora, HALO, lora_w, 0)]
    rw_pre_f = [mu_rkv, mu_lora, wt["rwkv_w0"], wt["rwkv_a0"], wt["rwkv_k_k"], wt["rwkv_k_a"], w2p, a2p, g2]
    rw = fn_fwd("rwkv_pre", _rwkv_pre, nt, rw_pre_t, rw_pre_f, [(t, tr, w, F32)] * 7)
    r_, lw_, k2_, v_, nkk_, b_, gate_ = rw
    rw_seq = [(a, None) for a in (r_, lw_, k2_, v_, nkk_, b_)]

    def rw_fn(sv, cv, hts, ids):
        return _rwkv_chunks([(*s, ht) for s, ht in zip(sv, hts)])

    yr_scan, rw_states = scan_fwd("rwkv_scan", rw_fn, RWKV_CHUNK, rw_seq, [], n_pairs, rw_ppb)
    rw_post_t = [(a, tr, w, 0) for a in (yr_scan, r_, k2_, v_, gate_)]
    rw_post_f = [r_k, wt["rwkv_ln_w"], wt["rwkv_ln_b"]]
    (y_rwkv,) = fn_fwd("rwkv_post", _rwkv_post, nt, rw_post_t, rw_post_f, [(t, tr, w, F32)])

    ymix = jnp.concatenate([y_ssd, y_rwkv], axis=1).astype(BF16)
    x1 = matmul("out_proj", ymix, full["w_out"], resid=x)

    h2 = norm_fwd("norm_x", x1, wt["norm_x_g"], tr)
    mrows = mem.shape[0]
    mn = norm_fwd("norm_mem", mem, wt["norm_mem_g"], mrows)
    q = matmul("xattn_q", h2, full["xattn_wq"])
    kx = matmul("xattn_k", mn, full["xattn_wk"])
    vx = matmul("xattn_v", mn, full["xattn_wv"])
    (ao,) = fn_fwd("xattn_core", _attn, nt, [(q, tr, d, 0)], [kx, vx], [(t, tr, d, BF16)])
    x2 = matmul("xattn_o", ao, full["xattn_wo"], resid=x1)

    h3 = norm_fwd("norm_ffn", x2, wt["norm_ffn_g"], tr)
    a1 = matmul("ffn_up", h3, full["ffn_w1"])
    dff = a1.shape[1]
    (f1,) = fn_fwd("ffn_act", _relu2, nt, [(a1, tr, dff, 0)], [], [(t, tr, dff, BF16)])
    x3 = matmul("ffn_down", f1, full["ffn_w2"], resid=x2)

    dx3, g_final, loss_tile = loss_head(x3, tgt, wt["final_norm_g"].reshape(1, d), tr)

    grads = {"final_norm_g": g_final.reshape(d)}
    dx3b = dx3.astype(BF16)
    grads["ffn_w2"] = matmul("ffn_down_dw", f1.T, dx3b)
    df1 = matmul("ffn_down_dx", dx3b, full["ffn_w2"], tb=True)
    (da1,), _ = fn_bwd("ffn_act_bwd", _relu2, nt, [(a1, tr, dff, 0)], [], [(df1, tr, dff, 0)], lambda c: c,
                       [(t, tr, dff, BF16)])
    grads["ffn_w1"] = matmul("ffn_up_dw", h3.T, da1)
    dh3 = matmul("ffn_up_dx", da1, full["ffn_w1"], tb=True)
    dx2, grads["norm_ffn_g"] = norm_bwd("norm_ffn_bwd", x2, wt["norm_ffn_g"], dh3, dx3, tr)

    dx2b = dx2.astype(BF16)
    grads["xattn_wo"] = matmul("xattn_o_dw", ao.T, dx2b)
    dao = matmul("xattn_o_dx", dx2b, full["xattn_wo"], tb=True)
    (dq,), (dkx, dvx) = fn_bwd("xattn_core_bwd", _attn, nt, [(q, tr, d, 0)], [kx, vx], [(dao, tr, d, 0)], lambda c: c,
                               [(t, tr, d, BF16)])
    grads["xattn_wq"] = matmul("xattn_q_dw", h2.T, dq)
    dh2 = matmul("xattn_q_dx", dq, full["xattn_wq"], tb=True)
    dkb, dvb = dkx.astype(BF16), dvx.astype(BF16)
    grads["xattn_wk"] = matmul("xattn_k_dw", mn.T, dkb)
    grads["xattn_wv"] = matmul("xattn_v_dw", mn.T, dvb)
    dmn = matmul("xattn_k_dx", dkb, full["xattn_wk"], tb=True)
    dmn = matmul("xattn_v_dx", dvb, full["xattn_wv"], tb=True, resid=dmn)
    _, grads["norm_mem_g"] = norm_bwd("norm_mem_bwd", mem, wt["norm_mem_g"], dmn, None, mrows)
    dx1, grads["norm_x_g"] = norm_bwd("norm_x_bwd", x1, wt["norm_x_g"], dh2, dx2, tr)

    dx1b = dx1.astype(BF16)
    grads["w_out"] = matmul("out_proj_dw", ymix.T, dx1b)
    dymix = matmul("out_proj_dx", dx1b, full["w_out"], tb=True)

    (dyr, dr1, dk1, dv1, dgate), (g_rk, grads["rwkv_ln_w"], grads["rwkv_ln_b"]) = fn_bwd(
        "rwkv_post_bwd", _rwkv_post, nt, rw_post_t, rw_post_f, [(dymix, tr, w, 1)], lambda c: c, [(t, tr, w, F32)] * 5)
    grads["rwkv_r_k"] = g_rk.reshape(wt["rwkv_r_k"].shape)
    (dr2, dlw, dk2, dv2, dnkk, db), _ = scan_bwd("rwkv_scan_bwd", rw_fn, RWKV_CHUNK, rw_seq, [], rw_states, dyr, n_pairs, rw_ppb)
    rw_ct = [(a, tr, w, 0) for a in (dr1, dr2, dlw, dk1, dk2, dv1, dv2, dnkk, db, dgate)]

    def rw_ct_fn(c):
        return (c[0] + c[1], c[2], c[3] + c[4], c[5] + c[6], c[7], c[8], c[9])

    (durkv, dulora, dhrkv, dhlora), rw_pg = fn_bwd(
        "rwkv_pre_bwd", _rwkv_pre, nt, rw_pre_t, rw_pre_f, rw_ct, rw_ct_fn,
        [(t, tr, 3 * w, F32), (t, tr, lora_w, F32), (nt * HALO, HALO, 3 * w, F32), (nt * HALO, HALO, lora_w, F32)])
    durkv = _unhalo(durkv, dhrkv, tr)
    dulora = _unhalo(dulora, dhlora, tr)
    g_mu_rkv, g_mu_lora, grads["rwkv_w0"], grads["rwkv_a0"], grads["rwkv_k_k"], grads["rwkv_k_a"], g_w2p, g_a2p, grads["rwkv_g2"] = rw_pg
    grads["rwkv_mu"] = jnp.concatenate([g_mu_rkv, g_mu_lora[:, :dr], g_mu_lora[:, LANES:LANES + ar], g_mu_lora[:, 2 * LANES:]], axis=1)
    grads["rwkv_w2"] = g_w2p[:dr]
    grads["rwkv_a2"] = g_a2p[:ar]

    (dys, dxs1, dz), (g_d, grads["ssd_norm_g"]) = fn_bwd(
        "ssd_post_bwd", _ssd_post, nt, ssd_post_t, ssd_post_f, [(dymix, tr, w, 0)], lambda c: c, [(t, tr, w, F32)] * 3)
    grads["ssd_d"] = g_d[:, :nh]
    (dxs2, dbp, dcp, ddtp), (g_alog,) = scan_bwd("ssd_scan_bwd", ssd_fn, SSD_CHUNK, ssd_seq, [a_log], ssd_states, dys, n_pairs, ssd_ppb)
    grads["ssd_a_log"] = g_alog[:, :nh]
    ssd_ct = [(dxs1, tr, w, 0), (dxs2, tr, w, 0), (dbp, tr, w, 0), (dcp, tr, w, 0), (ddtp, tr, w, 0)]

    def ssd_ct_fn(c):
        def group_sum(a):
            parts = []
            for gi in range(SSD_GROUPS):
                s = a[:, gi * ppg * LANES:(gi * ppg + 1) * LANES]
                for j in range(1, ppg):
                    s = s + a[:, (gi * ppg + j) * LANES:(gi * ppg + j + 1) * LANES]
                parts.append(s)
            return parts
        ddt = c[4][:, :LANES]
        for j in range(1, n_pairs):
            ddt = ddt + c[4][:, j * LANES:(j + 1) * LANES]
        return (jnp.concatenate([c[0] + c[1]] + group_sum(c[2]) + group_sum(c[3]), axis=1), ddt)

    (dxbc, dhxbc, ddtraw), ssd_pg = fn_bwd(
        "ssd_pre_bwd", _ssd_pre, nt, ssd_pre_t, ssd_pre_f, ssd_ct, ssd_ct_fn,
        [(t, tr, conv_dim, F32), (nt * HALO, HALO, conv_dim, F32), (t, tr, LANES, F32)])
    dxbc = _unhalo(dxbc, dhxbc, tr)
    grads["ssd_conv_w"] = jnp.concatenate(ssd_pg[:SSD_CONV], axis=0)
    grads["ssd_conv_b"] = ssd_pg[SSD_CONV]
    grads["ssd_dt_bias"] = ssd_pg[SSD_CONV + 1][:, :nh]

    du = jnp.concatenate([dz, dxbc, ddtraw, durkv, dulora], axis=1).astype(BF16)
    g_perm = matmul("in_proj_dw", h1.T, du)
    grads["w_in"] = jnp.concatenate([seg_cols(g_perm, nm, segs[nm][1]) for nm in order], axis=1)
    dh1 = matmul("in_proj_dx", du, w_perm, tb=True)
    grad_x, grads["norm_mix_g"] = norm_bwd("norm_mix_bwd", x, wt["norm_mix_g"], dh1, dx1, tr)
    return loss_tile, grad_x, grads


def _pack(arrs):
    flat = jnp.concatenate([a.reshape(-1) for a in arrs])
    n = flat.shape[0]
    rows = -(-n // (8 * LANES)) * 8
    return jnp.pad(flat, (0, rows * LANES - n)).reshape(rows, LANES)


def _unpack(packed, shapes):
    flat = packed.reshape(-1)
    out, o = [], 0
    for s in shapes:
        n = math.prod(s)
        out.append(flat[o:o + n].reshape(s))
        o += n
    return out


def _as2d(a):
    return a.reshape(-1, a.shape[-1])


def _step(a):
    x, mem, tgt = a["x"][0], a["mem"][0], a["loss_target"][0]
    q = 2 * lax.axis_index("x") + lax.axis_index("y")

    shard2d = {n: _as2d(a[n][0]) for n in BIG}
    small_sh = {n: _as2d(a[n][0]) for n in SMALL_SHARDED}
    gathered = gather_two_level("gather_weights", [shard2d[n].astype(BF16) for n in BIG])
    full = {n: _from_slots(g.reshape(4, 2 * g.shape[2], g.shape[3]), BIG_AXIS[n]) for n, g in zip(BIG, gathered)}
    gathered = chip_exchange("gather_small", [small_sh[n] for n in SMALL_SHARDED], True)
    for n, g in zip(SMALL_SHARDED, gathered):
        full[n] = _from_slots(g, 1)

    wt = {n: (a[n] if a[n].ndim <= 2 else a[n][0]) for n in WEIGHTS if n not in BIG and n not in SMALL_SHARDED}
    for n in SMALL_SHARDED:
        wt[n] = small_sh[n]
    loss_tile, grad_x, grads = _local_grads(x, mem, tgt, wt, full)

    g4 = []
    for n in BIG:
        s = _to_slots(grads[n], BIG_AXIS[n])
        g4.append(s.reshape(4, 2, s.shape[1] // 2, s.shape[2]))
    kept, got = swap_halves("swap_halves", g4)
    chip_parts = []
    for n, k, g in zip(BIG, kept, got):
        _, hr, cols = k.shape
        tr = _pick(4 * hr, (256, 128, 64, 32, 16))
        (part,), _ = row_call("chip_sum_" + n, lambda tv, fv: ([tv[0] + tv[1]], []), 4 * hr // tr,
                              [(k.reshape(4 * hr, cols), tr, cols, 0), (g.reshape(4 * hr, cols), tr, cols, 0)], [],
                              [(4 * hr, tr, cols, BF16)], [])
        chip_parts.append(part.reshape(4, hr, cols))
    slots = chip_exchange("scatter_grads", chip_parts, False)
    halves = [sum_slots("sum_" + n, s) for n, s in zip(BIG, slots)]
    reduced = core_gather("gather_cores", halves)

    out = {}
    for n, r in zip(BIG, reduced):
        gsum = r.reshape(2 * r.shape[1], r.shape[2])
        g, dlt, mn, vn = adamw("adamw_" + n, shard2d[n], _as2d(a["m_" + n][0]), _as2d(a["v_" + n][0]), [gsum])
        for key, val in (("grad_", g), ("delta_", dlt), ("new_m_", mn), ("new_v_", vn)):
            out[key + n] = val.reshape(a[n].shape)

    small = [n for n in WEIGHTS if n not in BIG]
    red = _unpack(all_reduce_small("all_reduce_small", _pack([grads[n] for n in small])), [grads[n].shape for n in small])
    g_loc = {}
    for n, g in zip(small, red):
        if n in SMALL_SHARDED:
            cols = g.shape[1] // 4
            g = lax.dynamic_slice_in_dim(g, q * cols, cols, axis=1)
        g_loc[n] = g.reshape(a[n].shape)
    res = adamw("adamw_small", *[_pack([src[n] for n in small]) for src in
                                 ({n: a[n] for n in small}, {n: a["m_" + n] for n in small}, {n: a["v_" + n] for n in small})],
                [_pack([g_loc[n] for n in small])])
    shapes = [a[n].shape for n in small]
    for key, packed in zip(("grad_", "delta_", "new_m_", "new_v_"), res):
        for n, val in zip(small, _unpack(packed, shapes)):
            out[key + n] = val

    loss = lax.psum(loss_tile[0, 0], ("x", "y", "c"))
    ordered = [loss, grad_x.reshape(a["x"].shape)]
    for key in ("grad_", "delta_", "new_m_", "new_v_"):
        ordered += [out[key + n] for n in WEIGHTS]
    return tuple(ordered)


def kernel(x, mem, norm_mix_g, w_in, ssd_conv_w, ssd_conv_b, ssd_dt_bias, ssd_a_log, ssd_d, ssd_norm_g, rwkv_mu, rwkv_w0, rwkv_w2, rwkv_a0, rwkv_a2, rwkv_g2, rwkv_k_k, rwkv_k_a, rwkv_r_k, rwkv_ln_w, rwkv_ln_b, w_out, norm_x_g, norm_mem_g, xattn_wq, xattn_wk, xattn_wv, xattn_wo, norm_ffn_g, ffn_w1, ffn_w2, final_norm_g, loss_target, m_norm_mix_g, m_w_in, m_ssd_conv_w, m_ssd_conv_b, m_ssd_dt_bias, m_ssd_a_log, m_ssd_d, m_ssd_norm_g, m_rwkv_mu, m_rwkv_w0, m_rwkv_w2, m_rwkv_a0, m_rwkv_a2, m_rwkv_g2, m_rwkv_k_k, m_rwkv_k_a, m_rwkv_r_k, m_rwkv_ln_w, m_rwkv_ln_b, m_w_out, m_norm_x_g, m_norm_mem_g, m_xattn_wq, m_xattn_wk, m_xattn_wv, m_xattn_wo, m_norm_ffn_g, m_ffn_w1, m_ffn_w2, m_final_norm_g, v_norm_mix_g, v_w_in, v_ssd_conv_w, v_ssd_conv_b, v_ssd_dt_bias, v_ssd_a_log, v_ssd_d, v_ssd_norm_g, v_rwkv_mu, v_rwkv_w0, v_rwkv_w2, v_rwkv_a0, v_rwkv_a2, v_rwkv_g2, v_rwkv_k_k, v_rwkv_k_a, v_rwkv_r_k, v_rwkv_ln_w, v_rwkv_ln_b, v_w_out, v_norm_x_g, v_norm_mem_g, v_xattn_wq, v_xattn_wk, v_xattn_wv, v_xattn_wo, v_norm_ffn_g, v_ffn_w1, v_ffn_w2, v_final_norm_g):
    return _step(dict(locals()))
```

```python
import functools
import math

import jax
import jax.numpy as jnp
from jax import lax
from jax.experimental import pallas as pl
from jax.experimental.pallas import tpu as pltpu

F32 = jnp.float32
BF16 = jnp.bfloat16
HIGHEST = lax.Precision.HIGHEST
MESH_ID = pl.DeviceIdType.MESH

NORM_EPS = 1e-6
RWKV_LN_EPS = 64e-5
HEAD_DIM = 64
PAIR = 2 * HEAD_DIM
LANES = 128
SSD_STATE = 128
SSD_CHUNK = 128
SSD_GROUPS = 2
SSD_CONV = 4
RWKV_CHUNK = 64
HALO = 8
ROW_TILE = 128
PAIRS_PER_STEP = 4
XATTN_HEADS = 4
RWKV_PASSES = 1
VMEM_LIMIT = 56 * 1024 * 1024

ADAM_LR = 0.001
ADAM_B1 = 0.9
ADAM_B2 = 0.999
ADAM_EPS = 1e-08
ADAM_WD = 0.01
ADAM_STEP = 10


def _dims(ca, cb):
    return (((ca,), (cb,)), ((), ()))


def _split_bf16(a):
    hi = a.astype(BF16)
    lo = (a - hi.astype(F32)).astype(BF16)
    return hi, lo


def _mm_impl(a, b, ca, cb, passes):
    dn = _dims(ca, cb)
    if passes == 1:
        return lax.dot_general(a.astype(BF16), b.astype(BF16), dn, preferred_element_type=F32)
    ah, al = _split_bf16(a)
    bh, bl = _split_bf16(b)
    out = lax.dot_general(ah, bh, dn, preferred_element_type=F32)
    out = out + lax.dot_general(ah, bl, dn, preferred_element_type=F32)
    return out + lax.dot_general(al, bh, dn, preferred_element_type=F32)


@functools.partial(jax.custom_vjp, nondiff_argnums=(2, 3, 4))
def mm(a, b, ca, cb, passes):
    return _mm_impl(a, b, ca, cb, passes)


def _mm_fwd(a, b, ca, cb, passes):
    return _mm_impl(a, b, ca, cb, passes), (a, b)


def _mm_bwd(ca, cb, passes, res, g):
    a, b = res
    da = mm(g, b, 1, 1 - cb, passes) if ca == 1 else mm(b, g, 1 - cb, 1, passes)
    db = mm(a, g, 1 - ca, 0, passes) if cb == 0 else mm(g, a, 0, 1 - ca, passes)
    return da, db


mm.defvjp(_mm_fwd, _mm_bwd)


def _dot_exact(a, b):
    return lax.dot_general(a, b, _dims(1, 0), precision=HIGHEST, preferred_element_type=F32)


def _iota(shape, dim):
    return lax.broadcasted_iota(jnp.int32, shape, dim)


def _sigmoid(x):
    return 1.0 / (1.0 + jnp.exp(-x))


def _silu(x):
    return x * _sigmoid(x)


def _softplus(x):
    return jnp.maximum(x, 0.0) + jnp.log(1.0 + jnp.exp(-jnp.abs(x)))


def _rms(x, g):
    return x * lax.rsqrt(jnp.mean(x * x, axis=-1, keepdims=True) + NORM_EPS) * g


def _head_sum(x):
    n = x.shape[1]
    sel = (_iota((n, LANES), 0) // HEAD_DIM == _iota((n, LANES), 1)).astype(F32)
    return _dot_exact(x, sel)


def _head_expand(s, n):
    sel = (_iota((LANES, n), 1) // HEAD_DIM == _iota((LANES, n), 0)).astype(F32)
    return _dot_exact(s, sel)


def _row_vector_expand(v, n):
    v8 = jnp.broadcast_to(v, (8, LANES))
    return jnp.sum(_head_expand(v8, n), axis=0, keepdims=True) * 0.125


def _shift_rows(u, halo, s):
    tr = u.shape[0]
    sm = (_iota((tr, tr), 1) == _iota((tr, tr), 0) - s).astype(F32)
    out = _dot_exact(sm, u)
    row = _iota((tr, 1), 0)
    hrow = _iota((HALO, 1), 0)
    for r in range(s):
        src = jnp.sum(jnp.where(hrow == HALO - s + r, halo, 0.0), axis=0, keepdims=True)
        out = out + jnp.where(row == r, src, 0.0)
    return out


def _params(sem):
    return pltpu.CompilerParams(dimension_semantics=sem, vmem_limit_bytes=VMEM_LIMIT)


def row_call(name, body, n_tiles, tiled, full, out_tiled, out_acc):
    nt, nf, no, na = len(tiled), len(full), len(out_tiled), len(out_acc)

    def kern(*refs):
        tv = [r[...] for r in refs[:nt]]
        fv = [r[...] for r in refs[nt:nt + nf]]
        outs, accs = body(tv, fv)
        for r, v in zip(refs[nt + nf:nt + nf + no], outs):
            r[...] = v.astype(r.dtype)
        if na:
            a_refs = refs[nt + nf + no:]
            first = pl.program_id(0) == 0

            @pl.when(first)
            def _():
                for r, v in zip(a_refs, accs):
                    r[...] = v

            @pl.when(jnp.logical_not(first))
            def _():
                for r, v in zip(a_refs, accs):
                    r[...] += v

    in_specs = [pl.BlockSpec((rt, w), functools.partial(lambda i, cb: (i, cb), cb=cb)) for (_, rt, w, cb) in tiled]
    in_specs += [pl.BlockSpec(a.shape, lambda i: (0, 0)) for a in full]
    out_specs = [pl.BlockSpec((rt, w), lambda i: (i, 0)) for (_, rt, w, _) in out_tiled]
    out_specs += [pl.BlockSpec(s, lambda i: (0, 0)) for s in out_acc]
    out_shape = [jax.ShapeDtypeStruct((rows, w), dt) for (rows, _, w, dt) in out_tiled]
    out_shape += [jax.ShapeDtypeStruct(s, F32) for s in out_acc]
    res = pl.pallas_call(
        kern, name=name, grid=(n_tiles,), in_specs=in_specs, out_specs=out_specs, out_shape=out_shape,
        compiler_params=_params(("arbitrary",)),
    )(*[t[0] for t in tiled], *full)
    return list(res[:no]), list(res[no:])


def _pick(dim, cands):
    for c in cands:
        if dim % c == 0:
            return c
    return dim


def matmul(name, a, b, tb=False, resid=None, out_dtype=F32):
    m, k = a.shape
    n = b.shape[0] if tb else b.shape[1]
    tm = _pick(m, (512, 256, 128))
    tn = _pick(n, (1024, 896, 768, 512, 384, 256, 128))
    tk = _pick(k, (1024, 896, 768, 512, 384, 256, 128))
    nk = k // tk
    has_resid = resid is not None

    def kern(*refs):
        a_ref, b_ref = refs[0], refs[1]
        o_ref, acc = refs[-2], refs[-1]
        kk = pl.program_id(2)

        @pl.when(kk == 0)
        def _():
            acc[...] = jnp.zeros_like(acc)

        acc[...] += lax.dot_general(a_ref[...], b_ref[...], _dims(1, 1 if tb else 0), preferred_element_type=F32)

        @pl.when(kk == nk - 1)
        def _():
            out = acc[...]
            if has_resid:
                out = out + refs[2][...]
            o_ref[...] = out.astype(o_ref.dtype)

    in_specs = [pl.BlockSpec((tm, tk), lambda i, j, kk: (i, kk))]
    if tb:
        in_specs.append(pl.BlockSpec((tn, tk), lambda i, j, kk: (j, kk)))
    else:
        in_specs.append(pl.BlockSpec((tk, tn), lambda i, j, kk: (kk, j)))
    args = [a, b]
    if has_resid:
        in_specs.append(pl.BlockSpec((tm, tn), lambda i, j, kk: (i, j)))
        args.append(resid)
    return pl.pallas_call(
        kern, name=name, grid=(m // tm, n // tn, nk), in_specs=in_specs,
        out_specs=pl.BlockSpec((tm, tn), lambda i, j, kk: (i, j)),
        out_shape=jax.ShapeDtypeStruct((m, n), out_dtype),
        scratch_shapes=[pltpu.VMEM((tm, tn), F32)],
        compiler_params=_params(("parallel", "parallel", "arbitrary")),
    )(*args)


def norm_fwd(name, x, g, tr):
    def body(tv, fv):
        return [_rms(tv[0], fv[0])], []
    rows, d = x.shape
    (h,), _ = row_call(name, body, rows // tr, [(x, tr, d, 0)], [g], [(rows, tr, d, BF16)], [])
    return h


def norm_bwd(name, x, g, dh, extra, tr):
    def body(tv, fv):
        _, vjp = jax.vjp(_rms, tv[0], fv[0])
        dx, dg = vjp(tv[1])
        if extra is not None:
            dx = dx + tv[2]
        return [dx], [dg]
    rows, d = x.shape
    tiled = [(x, tr, d, 0), (dh, tr, d, 0)] + ([(extra, tr, d, 0)] if extra is not None else [])
    (dx,), (dg,) = row_call(name, body, rows // tr, tiled, [g], [(rows, tr, d, F32)], [g.shape])
    return dx, dg


def _ssd_pre(xbc, halo, dtraw, w0, w1, w2, w3, cb, dtb):
    y = w3 * xbc + w2 * _shift_rows(xbc, halo, 1) + w1 * _shift_rows(xbc, halo, 2) + w0 * _shift_rows(xbc, halo, 3) + cb
    return _silu(y), _softplus(dtraw + dtb)


def _ssd_post(ys, xs, z, dskip, ng):
    w = ys.shape[1]
    y = (ys + xs * _row_vector_expand(dskip, w)) * _silu(z)
    gw = w // SSD_GROUPS
    parts = []
    for gi in range(SSD_GROUPS):
        yg = y[:, gi * gw:(gi + 1) * gw]
        parts.append(yg * lax.rsqrt(jnp.mean(yg * yg, axis=-1, keepdims=True) + NORM_EPS))
    return jnp.concatenate(parts, axis=1) * ng


def _rwkv_pre(urkv, ulora, hrkv, hlora, mu_rkv, mu_lora, w0, a0, kkw, kaw, w2p, a2p, g2):
    w = w0.shape[1]
    urkv = urkv + (_shift_rows(urkv, hrkv, 1) - urkv) * mu_rkv
    ulora = ulora + (_shift_rows(ulora, hlora, 1) - ulora) * mu_lora
    r, k, v = urkv[:, :w], urkv[:, w:2 * w], urkv[:, 2 * w:]
    pw, pa, pg = ulora[:, :LANES], ulora[:, LANES:2 * LANES], ulora[:, 2 * LANES:]
    w_log = -_softplus(-(w0 + mm(jnp.tanh(pw), w2p, 1, 0, 1))) - 0.5
    lw = -jnp.exp(w_log)
    iclr = _sigmoid(a0 + mm(pa, a2p, 1, 0, 1))
    gate = mm(_sigmoid(pg), g2, 1, 0, 1)
    kk = k * kkw
    kk = kk / jnp.maximum(jnp.sqrt(_head_expand(_head_sum(kk * kk), w)), 1e-12)
    k2 = k * (1.0 + (iclr - 1.0) * kaw)
    return r, lw, k2, v, -kk, kk * iclr, gate


def _rwkv_post(ys, r, k2, v, gate, rk, lnw, lnb):
    w = ys.shape[1]
    inv = 1.0 / HEAD_DIM
    mean = _head_expand(_head_sum(ys), w) * inv
    d = ys - mean
    var = _head_expand(_head_sum(d * d), w) * inv
    yn = d * lax.rsqrt(var + RWKV_LN_EPS) * lnw + lnb
    bonus = _head_expand(_head_sum(r * k2 * rk), w) * v
    return (yn + bonus) * gate


def _attn(q, k, v):
    d = q.shape[1]
    hd = d // XATTN_HEADS
    outs = []
    for h in range(XATTN_HEADS):
        sl = slice(h * hd, (h + 1) * hd)
        s = mm(q[:, sl], k[:, sl], 1, 1, 1) * (hd ** -0.5)
        s = s - jnp.max(s, axis=-1, keepdims=True)
        p = jnp.exp(s)
        p = p / jnp.sum(p, axis=-1, keepdims=True)
        outs.append(mm(p, v[:, sl], 1, 0, 1))
    return jnp.concatenate(outs, axis=1)


def _relu2(a):
    return jnp.square(jnp.maximum(a, 0.0))


def fn_fwd(name, fn, n_tiles, tiled, full, out_tiled):
    def body(tv, fv):
        outs = fn(*tv, *fv)
        return (list(outs) if isinstance(outs, (tuple, list)) else [outs]), []
    outs, _ = row_call(name, body, n_tiles, tiled, full, out_tiled, [])
    return outs


def fn_bwd(name, fn, n_tiles, tiled, full, cts, ct_fn, out_tiled):
    nt = len(tiled)

    def body(tv, fv):
        outs, vjp = jax.vjp(fn, *tv[:nt], *fv)
        ct = ct_fn(tv[nt:])
        grads = vjp(tuple(ct) if isinstance(outs, (tuple, list)) else ct[0])
        return list(grads[:nt]), list(grads[nt:])
    return row_call(name, body, n_tiles, tiled + cts, full, out_tiled, [f.shape for f in full])


def _ssd_chunk(xs, bm, cm, dt_all, a_log, ht, p):
    q = xs.shape[0]
    lane = _iota((1, LANES), 1)
    row = _iota((q, 1), 0)
    tril = _iota((q, q), 0) >= _iota((q, q), 1)
    half = lane < HEAD_DIM
    da = dt_all * (-jnp.exp(a_log))
    cs = _dot_exact(tril.astype(F32), da)

    def col(mat, h):
        return jnp.sum(jnp.where(lane == h, mat, 0.0), axis=1, keepdims=True)

    cs0, cs1 = col(cs, 2 * p), col(cs, 2 * p + 1)
    xdt = xs * jnp.where(half, col(dt_all, 2 * p), col(dt_all, 2 * p + 1))
    csx = jnp.where(half, cs0, cs1)
    last = jnp.sum(jnp.where(row == q - 1, csx, 0.0), axis=0, keepdims=True)
    cb = mm(cm, bm, 1, 1, 1)
    y = mm(cm, ht, 1, 0, 1) * jnp.exp(csx)
    for csh, hm in ((cs0, half), (cs1, jnp.logical_not(half))):
        csl = jnp.broadcast_to(csh, (q, q))
        seg = csl - csl.T
        lmat = jnp.where(tril, jnp.exp(jnp.where(tril, seg, 0.0)), 0.0)
        y = y + jnp.where(hm, mm(cb * lmat, xdt, 1, 0, 1), 0.0)
    st = mm(bm, xdt * jnp.exp(last - csx), 0, 0, 1)
    return y, ht * jnp.exp(last) + st


def _rwkv_chunks(pairs):
    c = pairs[0][0].shape[0]
    ps = RWKV_PASSES
    lane = _iota((1, LANES), 1)
    row = _iota((c, 1), 0)
    ri, ci = _iota((c, c), 0), _iota((c, c), 1)
    tril_i, tril_s = ri >= ci, ri > ci
    eye = (ri == ci).astype(F32)
    half = lane < HEAD_DIM
    halves = (half, jnp.logical_not(half))
    bd = (_iota((LANES, LANES), 0) < HEAD_DIM) == (_iota((LANES, LANES), 1) < HEAD_DIM)
    tri = tril_i.astype(F32)
    n = len(pairs)
    heads = [(j, hm) for j in range(n) for hm in halves]

    cum = [_dot_exact(tri, p[1]) for p in pairs]
    at = [p[4] * jnp.exp(cm - p[1]) for p, cm in zip(pairs, cum)]
    en = [jnp.exp(-cm) for cm in cum]
    bt = [p[5] * e for p, e in zip(pairs, en)]
    kt = [p[2] * e for p, e in zip(pairs, en)]
    rt = [p[0] * jnp.exp(cm) for p, cm in zip(pairs, cum)]
    ah = [mm(at[j], pairs[j][6], 1, 1, ps) for j in range(n)]
    y = [mm(rt[j], pairs[j][6], 1, 1, ps) for j in range(n)]
    atm = [jnp.where(hm, at[j], 0.0) for j, hm in heads]
    rtm = [jnp.where(hm, rt[j], 0.0) for j, hm in heads]
    aab = [jnp.where(tril_s, mm(atm[i], bt[j], 1, 1, ps), 0.0) for i, (j, _) in enumerate(heads)]
    aak = [jnp.where(tril_s, mm(atm[i], kt[j], 1, 1, ps), 0.0) for i, (j, _) in enumerate(heads)]
    arb = [jnp.where(tril_i, mm(rtm[i], bt[j], 1, 1, ps), 0.0) for i, (j, _) in enumerate(heads)]
    ark = [jnp.where(tril_i, mm(rtm[i], kt[j], 1, 1, ps), 0.0) for i, (j, _) in enumerate(heads)]
    rhs = [ah[j] + mm(aak[i], pairs[j][3], 1, 0, ps) for i, (j, _) in enumerate(heads)]
    yv = [mm(ark[i], pairs[j][3], 1, 0, ps) for i, (j, _) in enumerate(heads)]
    tm = [eye + a_ for a_ in aab]
    pm = aab
    for _ in range(int(math.log2(c)) - 1):
        pm = [mm(p_, p_, 1, 0, ps) for p_ in pm]
        tm = [t_ + mm(t_, p_, 1, 0, ps) for t_, p_ in zip(tm, pm)]
    uh = [mm(tm[i], rhs[i], 1, 0, ps) for i in range(len(heads))]
    u = [jnp.where(half, uh[2 * j], uh[2 * j + 1]) for j in range(n)]
    yu = [mm(arb[i], u[j], 1, 0, ps) for i, (j, _) in enumerate(heads)]
    out = []
    for j in range(n):
        yj = y[j] + jnp.where(half, yu[2 * j] + yv[2 * j], yu[2 * j + 1] + yv[2 * j + 1])
        plast = jnp.sum(jnp.where(row == c - 1, cum[j], 0.0), axis=0, keepdims=True)
        upd = pairs[j][6] + mm(u[j], bt[j], 0, 0, ps) + mm(pairs[j][3], kt[j], 0, 0, ps)
        out.append((yj, jnp.where(bd, upd * jnp.exp(plast), 0.0)))
    return out


def _seq_spec(chunk, ppb, col, row_of):
    if col is None:
        return pl.BlockSpec((chunk, ppb * LANES), lambda pb, i: (row_of(i), pb))
    return pl.BlockSpec((chunk, LANES), lambda pb, i: (row_of(i), col(pb * ppb)))


def _pair_vals(refs, seq_in, j):
    return [r[...] if col is not None else r[:, j * LANES:(j + 1) * LANES] for r, (_, col) in zip(refs, seq_in)]


def scan_fwd(name, chunk_fn, chunk, seq_in, const_in, n_pairs, ppb):
    t = seq_in[0][0].shape[0]
    nc = t // chunk
    ns, ncst = len(seq_in), len(const_in)

    def kern(*refs):
        y_ref, st_ref, ht = refs[ns + ncst], refs[ns + ncst + 1], refs[ns + ncst + 2]

        @pl.when(pl.program_id(1) == 0)
        def _():
            ht[...] = jnp.zeros_like(ht)

        cv = [r[...] for r in refs[ns:ns + ncst]]
        h0 = [ht[j] for j in range(ppb)]
        for j in range(ppb):
            st_ref[j] = h0[j]
        sv = [_pair_vals(refs[:ns], seq_in, j) for j in range(ppb)]
        outs = chunk_fn(sv, cv, h0, [pl.program_id(0) * ppb + j for j in range(ppb)])
        for j, (y, hn) in enumerate(outs):
            y_ref[:, j * LANES:(j + 1) * LANES] = y
            ht[j] = hn

    in_specs = [_seq_spec(chunk, ppb, col, lambda i: i) for (_, col) in seq_in]
    in_specs += [pl.BlockSpec(a.shape, lambda pb, i: (0, 0)) for a in const_in]
    return pl.pallas_call(
        kern, name=name, grid=(n_pairs // ppb, nc), in_specs=in_specs,
        out_specs=[pl.BlockSpec((chunk, ppb * LANES), lambda pb, i: (i, pb)),
                   pl.BlockSpec((ppb, None, LANES, LANES), lambda pb, i: (pb, i, 0, 0))],
        out_shape=[jax.ShapeDtypeStruct((t, n_pairs * LANES), F32), jax.ShapeDtypeStruct((n_pairs, nc, LANES, LANES), F32)],
        scratch_shapes=[pltpu.VMEM((ppb, LANES, LANES), F32)],
        compiler_params=_params(("arbitrary", "arbitrary")),
    )(*[s[0] for s in seq_in], *const_in)


def scan_bwd(name, chunk_fn, chunk, seq_in, const_in, states, dy, n_pairs, ppb):
    t = dy.shape[0]
    nc = t // chunk
    ns, ncst = len(seq_in), len(const_in)

    def kern(*refs):
        seq_refs, cst_refs = refs[:ns], refs[ns:ns + ncst]
        st_ref, dy_ref = refs[ns + ncst], refs[ns + ncst + 1]
        o = ns + ncst + 2
        dseq_refs, dcst_refs, dht = refs[o:o + ns], refs[o + ns:o + ns + ncst], refs[o + ns + ncst]
        pb, i = pl.program_id(0), pl.program_id(1)

        @pl.when(i == 0)
        def _():
            dht[...] = jnp.zeros_like(dht)

        ids = [pb * ppb + j for j in range(ppb)]
        lanes = [slice(j * LANES, (j + 1) * LANES) for j in range(ppb)]

        def fn(*flat):
            sv = [list(flat[j * ns:(j + 1) * ns]) for j in range(ppb)]
            outs = chunk_fn(sv, list(flat[ppb * ns:ppb * ns + ncst]), list(flat[ppb * ns + ncst:]), ids)
            return tuple(y for y, _ in outs), tuple(h for _, h in outs)

        flat_in = [v for j in range(ppb) for v in _pair_vals(seq_refs, seq_in, j)]
        flat_in += [r[...] for r in cst_refs] + [st_ref[j] for j in range(ppb)]
        _, vjp = jax.vjp(fn, *flat_in)
        grads = vjp((tuple(dy_ref[:, ln] for ln in lanes), tuple(dht[j] for j in range(ppb))))
        for j in range(ppb):
            for r, g in zip(dseq_refs, grads[j * ns:(j + 1) * ns]):
                r[:, lanes[j]] = g
            dht[j] = grads[ppb * ns + ncst + j]
        dcv = grads[ppb * ns:ppb * ns + ncst]
        if ncst:
            first = jnp.logical_and(pb == 0, i == 0)

            @pl.when(first)
            def _():
                for r, g in zip(dcst_refs, dcv):
                    r[...] = g

            @pl.when(jnp.logical_not(first))
            def _():
                for r, g in zip(dcst_refs, dcv):
                    r[...] += g

    rev = lambda i: nc - 1 - i
    wide = pl.BlockSpec((chunk, ppb * LANES), lambda pb, i: (rev(i), pb))
    in_specs = [_seq_spec(chunk, ppb, col, rev) for (_, col) in seq_in]
    in_specs += [pl.BlockSpec(a.shape, lambda pb, i: (0, 0)) for a in const_in]
    in_specs += [pl.BlockSpec((ppb, None, LANES, LANES), lambda pb, i: (pb, rev(i), 0, 0)), wide]
    out_specs = [wide for _ in seq_in]
    out_specs += [pl.BlockSpec(a.shape, lambda pb, i: (0, 0)) for a in const_in]
    out_shape = [jax.ShapeDtypeStruct((t, n_pairs * LANES), F32) for _ in seq_in]
    out_shape += [jax.ShapeDtypeStruct(a.shape, F32) for a in const_in]
    res = pl.pallas_call(
        kern, name=name, grid=(n_pairs // ppb, nc), in_specs=in_specs, out_specs=out_specs, out_shape=out_shape,
        scratch_shapes=[pltpu.VMEM((ppb, LANES, LANES), F32)],
        compiler_params=_params(("arbitrary", "arbitrary")),
    )(*[s[0] for s in seq_in], *const_in, states, dy)
    return list(res[:ns]), list(res[ns:])


def loss_head(x3, tgt, g, tr):
    rows, d = x3.shape

    def body(tv, fv):
        def f(x, gg):
            e = jnp.square(_rms(x, gg) - tv[1])
            return 0.5 * jnp.sum(jnp.mean(e, axis=-1, keepdims=True), axis=0, keepdims=True)
        l, vjp = jax.vjp(f, tv[0], fv[0])
        dx, dg = vjp(jnp.ones((1, 1), F32))
        return [dx], [dg, jnp.broadcast_to(l, (8, LANES))]
    (dx,), (dg, l) = row_call("loss_head", body, rows // tr, [(x3, tr, d, 0), (tgt, tr, d, 0)], [g],
                              [(rows, tr, d, F32)], [g.shape, (8, LANES)])
    return dx, dg, l


def _adam_math(w, g, m, v):
    m = ADAM_B1 * m + (1.0 - ADAM_B1) * g
    v = ADAM_B2 * v + (1.0 - ADAM_B2) * jnp.square(g)
    m_hat = m / (1.0 - ADAM_B1 ** ADAM_STEP)
    v_hat = v / (1.0 - ADAM_B2 ** ADAM_STEP)
    delta = -ADAM_LR * (m_hat / (jnp.sqrt(v_hat) + ADAM_EPS) + ADAM_WD * w)
    return delta, m, v


def adamw(name, w, m, v, g_parts):
    rows, cols = w.shape
    tr = _pick(rows, (256, 128, 64, 32, 16, 8))
    n_g = len(g_parts)

    def body(tv, fv):
        g = tv[3]
        for extra in tv[4:4 + n_g - 1]:
            g = g + extra
        delta, mn, vn = _adam_math(tv[0], g, tv[1], tv[2])
        return [g, delta, mn, vn], []
    tiled = [(a, tr, cols, 0) for a in (w, m, v, *g_parts)]
    outs, _ = row_call(name, body, rows // tr, tiled, [], [(rows, tr, cols, F32)] * 4, [])
    return outs


def sum_slots(name, r):
    _, rows, cols = r.shape
    tr = _pick(rows, (256, 128, 64, 32, 16, 8))

    def kern(r0, r1, r2, r3, o):
        o[...] = ((r0[...].astype(F32) + r1[...].astype(F32)) + r2[...].astype(F32)) + r3[...].astype(F32)

    in_specs = [pl.BlockSpec((None, tr, cols), functools.partial(lambda i, s: (s, i, 0), s=s)) for s in range(4)]
    return pl.pallas_call(
        kern, name=name, grid=(rows // tr,), in_specs=in_specs, out_specs=pl.BlockSpec((tr, cols), lambda i: (i, 0)),
        out_shape=jax.ShapeDtypeStruct((rows, cols), F32), compiler_params=_params(("arbitrary",)),
    )(r, r, r, r)


def _my_place():
    return lax.axis_index("x"), lax.axis_index("y"), lax.axis_index("c")


def chip_exchange(name, arrays, gather):
    nw = len(arrays)
    ANY = pl.BlockSpec(memory_space=pl.ANY)

    def body(*refs):
        ins, outs = refs[:nw], refs[nw:2 * nw]
        send, recv, loc = refs[2 * nw:]
        x, y, c = _my_place()
        q = 2 * x + y
        peers = [(1 - x, y), (x, 1 - y), (1 - x, 1 - y)]

        def src(w, dest_chip):
            return ins[w] if gather else ins[w].at[dest_chip]

        def remote(w, j):
            px, py = peers[j]
            return pltpu.make_async_remote_copy(
                src_ref=src(w, 2 * px + py), dst_ref=outs[w].at[q], send_sem=send.at[w, j], recv_sem=recv.at[w, j],
                device_id=(px, py, c), device_id_type=MESH_ID)

        def arrival(w, j):
            px, py = peers[j]
            return pltpu.make_async_remote_copy(
                src_ref=src(w, q), dst_ref=outs[w].at[2 * px + py], send_sem=send.at[w, j], recv_sem=recv.at[w, j],
                device_id=(px, py, c), device_id_type=MESH_ID)

        local = [pltpu.make_async_copy(src(w, q), outs[w].at[q], loc.at[w]) for w in range(nw)]
        sends = [[remote(w, j) for j in range(3)] for w in range(nw)]
        for w in range(nw):
            local[w].start()
            for j in range(3):
                sends[w][j].start()
        for w in range(nw):
            local[w].wait()
            for j in range(3):
                sends[w][j].wait_send()
                arrival(w, j).wait_recv()

    out_shape = [jax.ShapeDtypeStruct((4,) + (a.shape if gather else a.shape[1:]), a.dtype) for a in arrays]
    return pl.pallas_call(
        body, name=name, in_specs=[ANY] * nw, out_specs=[ANY] * nw, out_shape=out_shape,
        scratch_shapes=[pltpu.SemaphoreType.DMA((nw, 3)), pltpu.SemaphoreType.DMA((nw, 3)), pltpu.SemaphoreType.DMA((nw,))],
        compiler_params=pltpu.CompilerParams(has_side_effects=True),
    )(*arrays)


def gather_two_level(name, arrays):
    nw = len(arrays)
    ANY = pl.BlockSpec(memory_space=pl.ANY)

    def body(*refs):
        ins, outs = refs[:nw], refs[nw:2 * nw]
        send, recv, loc = refs[2 * nw:]
        x, y, c = _my_place()
        q = 2 * x + y
        me, sibling = (x, y, c), (x, y, 1 - c)
        peers = [(1 - x, y), (x, 1 - y), (1 - x, 1 - y)]
        chips = [2 * px + py for px, py in peers]

        def mine(w):
            hr = ins[w].shape[0] // 2
            return ins[w].at[pl.ds(c * hr, hr)]

        def copy(w, k, src, chip, half, to):
            return pltpu.make_async_remote_copy(
                src_ref=src, dst_ref=outs[w].at[chip, half], send_sem=send.at[w, k], recv_sem=recv.at[w, k],
                device_id=to, device_id_type=MESH_ID)

        local = [pltpu.make_async_copy(mine(w), outs[w].at[q, c], loc.at[w]) for w in range(nw)]
        first = [[copy(w, 0, mine(w), q, c, sibling)] + [copy(w, 1 + j, mine(w), q, c, (*peers[j], c)) for j in range(3)]
                 for w in range(nw)]
        for w in range(nw):
            local[w].start()
            for cp in first[w]:
                cp.start()
        passed = []
        for w in range(nw):
            for j in range(3):
                copy(w, 1 + j, mine(w), chips[j], c, me).wait_recv()
                fwd = copy(w, 4 + j, outs[w].at[chips[j], c], chips[j], c, sibling)
                fwd.start()
                passed.append(fwd)
        for w in range(nw):
            copy(w, 0, mine(w), q, 1 - c, me).wait_recv()
            for j in range(3):
                copy(w, 4 + j, mine(w), chips[j], 1 - c, me).wait_recv()
        for w in range(nw):
            local[w].wait()
            for cp in first[w]:
                cp.wait_send()
        for cp in passed:
            cp.wait_send()

    out_shape = [jax.ShapeDtypeStruct((4, 2, a.shape[0] // 2, a.shape[1]), a.dtype) for a in arrays]
    return pl.pallas_call(
        body, name=name, in_specs=[ANY] * nw, out_specs=[ANY] * nw, out_shape=out_shape,
        scratch_shapes=[pltpu.SemaphoreType.DMA((nw, 7)), pltpu.SemaphoreType.DMA((nw, 7)), pltpu.SemaphoreType.DMA((nw,))],
        compiler_params=pltpu.CompilerParams(has_side_effects=True),
    )(*arrays)


def core_swap(name, arrays):
    nw = len(arrays)
    ANY = pl.BlockSpec(memory_space=pl.ANY)

    def body(*refs):
        ins, outs = refs[:nw], refs[nw:2 * nw]
        send, recv = refs[2 * nw:]
        x, y, c = _my_place()
        copies = [pltpu.make_async_remote_copy(
            src_ref=ins[w], dst_ref=outs[w], send_sem=send.at[w], recv_sem=recv.at[w],
            device_id=(x, y, 1 - c), device_id_type=MESH_ID) for w in range(nw)]
        for cp in copies:
            cp.start()
        for cp in copies:
            cp.wait_send()
            cp.wait_recv()

    return pl.pallas_call(
        body, name=name, in_specs=[ANY] * nw, out_specs=[ANY] * nw,
        out_shape=[jax.ShapeDtypeStruct(a.shape, a.dtype) for a in arrays],
        scratch_shapes=[pltpu.SemaphoreType.DMA((nw,)), pltpu.SemaphoreType.DMA((nw,))],
        compiler_params=pltpu.CompilerParams(has_side_effects=True),
    )(*arrays)


def all_reduce_small(name, v):
    rows = v.shape[0]
    VM = pl.BlockSpec(memory_space=pltpu.VMEM)

    def body(v_ref, o_ref, buf, send, recv):
        x, y, c = _my_place()
        me = 4 * x + 2 * y + c

        def peer(kx):
            return (x ^ ((kx >> 2) & 1), y ^ ((kx >> 1) & 1), c ^ (kx & 1))

        def copy(kx, slot):
            return pltpu.make_async_remote_copy(
                src_ref=v_ref, dst_ref=buf.at[slot], send_sem=send.at[kx - 1], recv_sem=recv.at[kx - 1],
                device_id=peer(kx), device_id_type=MESH_ID)

        sends = [copy(kx, me) for kx in range(1, 8)]
        for cp in sends:
            cp.start()
        buf[me] = v_ref[...]
        for kx in range(1, 8):
            copy(kx, me ^ kx).wait_recv()
        for cp in sends:
            cp.wait_send()
        acc = buf[0]
        for d in range(1, 8):
            acc = acc + buf[d]
        o_ref[...] = acc

    return pl.pallas_call(
        body, name=name, in_specs=[VM], out_specs=VM, out_shape=jax.ShapeDtypeStruct(v.shape, F32),
        scratch_shapes=[pltpu.VMEM((8, rows, LANES), F32), pltpu.SemaphoreType.DMA((7,)), pltpu.SemaphoreType.DMA((7,))],
        compiler_params=pltpu.CompilerParams(has_side_effects=True, vmem_limit_bytes=VMEM_LIMIT),
    )(v)


def _pad_cols(a, n):
    return jnp.pad(a, ((0, 0), (0, n - a.shape[1])))


def _pad_rows(a, n):
    return jnp.pad(a, ((0, n - a.shape[0]), (0, 0)))


def _halo(u, tr):
    t, cdim = u.shape
    tails = u.reshape(t // tr, tr, cdim)[:, tr - HALO:, :]
    tails = jnp.concatenate([jnp.zeros((1, HALO, cdim), u.dtype), tails[:-1]], axis=0)
    return tails.reshape(-1, cdim)


def _unhalo(du, dhalo, tr):
    t, cdim = du.shape
    n = t // tr
    dh = dhalo.reshape(n, HALO, cdim)
    dh = jnp.concatenate([dh[1:], jnp.zeros((1, HALO, cdim), du.dtype)], axis=0)
    d3 = du.reshape(n, tr, cdim)
    d3 = jnp.concatenate([d3[:, :tr - HALO, :], d3[:, tr - HALO:, :] + dh], axis=1)
    return d3.reshape(t, cdim)


def _to_slots(g, axis):
    r, cdim = g.shape
    if axis == 0:
        return g.reshape(4, r // 4, cdim)
    return g.reshape(r, 4, cdim // 4).transpose(1, 0, 2)


def _from_slots(s, axis):
    if axis == 0:
        return s.reshape(s.shape[0] * s.shape[1], s.shape[2])
    return s.transpose(1, 0, 2).reshape(s.shape[1], 4 * s.shape[2])


BIG = ("w_in", "w_out", "xattn_wq", "xattn_wk", "xattn_wv", "xattn_wo", "ffn_w1", "ffn_w2")
BIG_AXIS = {"w_in": 1, "w_out": 0, "xattn_wq": 0, "xattn_wk": 0, "xattn_wv": 0, "xattn_wo": 0, "ffn_w1": 1, "ffn_w2": 0}
SMALL_SHARDED = ("ssd_conv_w", "rwkv_w2", "rwkv_a2", "rwkv_g2")
WEIGHTS = ("norm_mix_g", "w_in", "ssd_conv_w", "ssd_conv_b", "ssd_dt_bias", "ssd_a_log", "ssd_d", "ssd_norm_g",
           "rwkv_mu", "rwkv_w0", "rwkv_w2", "rwkv_a0", "rwkv_a2", "rwkv_g2", "rwkv_k_k", "rwkv_k_a", "rwkv_r_k",
           "rwkv_ln_w", "rwkv_ln_b", "w_out", "norm_x_g", "norm_mem_g", "xattn_wq", "xattn_wk", "xattn_wv", "xattn_wo",
           "norm_ffn_g", "ffn_w1", "ffn_w2", "final_norm_g")


def _local_grads(x, mem, tgt, wt, full):
    t, d = x.shape
    w = d // 2
    nh = w // HEAD_DIM
    n_pairs = nh // 2
    ppg = n_pairs // SSD_GROUPS
    bc = SSD_GROUPS * SSD_STATE
    conv_dim = w + 2 * bc
    tr = ROW_TILE
    nt = t // tr
    dr = wt["rwkv_w2"].shape[0]
    ar = wt["rwkv_a2"].shape[0]
    gr = wt["rwkv_g2"].shape[0]

    w_in = full["w_in"]
    o = 0
    segs = {}
    for nm, width in (("z", w), ("xbc", conv_dim), ("dt", nh), ("rkv", 3 * w), ("pw", dr), ("pa", ar), ("pg", gr)):
        segs[nm] = (o, width)
        o += width
    padded = {"z": w, "xbc": conv_dim, "dt": LANES, "rkv": 3 * w, "pw": LANES, "pa": LANES, "pg": gr}
    order = ("z", "xbc", "dt", "rkv", "pw", "pa", "pg")
    w_perm = jnp.concatenate([_pad_cols(w_in[:, segs[nm][0]:segs[nm][0] + segs[nm][1]], padded[nm]) for nm in order], axis=1)
    offs = {}
    o = 0
    for nm in order:
        offs[nm] = o
        o += padded[nm]
    n_perm = o
    lora_w = 2 * LANES + gr

    def seg_cols(a, nm, width=None):
        return a[:, offs[nm]:offs[nm] + (padded[nm] if width is None else width)]

    mu = wt["rwkv_mu"]
    mo = 3 * w
    mu_rkv = mu[:, :mo]
    mu_lora = jnp.concatenate([_pad_cols(mu[:, mo:mo + dr], LANES), _pad_cols(mu[:, mo + dr:mo + dr + ar], LANES),
                               mu[:, mo + dr + ar:]], axis=1)
    w2p = _pad_rows(full["rwkv_w2"], LANES)
    a2p = _pad_rows(full["rwkv_a2"], LANES)
    g2 = full["rwkv_g2"]
    conv_w = full["ssd_conv_w"]
    cw = [conv_w[i:i + 1] for i in range(SSD_CONV)]
    dt_bias = _pad_cols(wt["ssd_dt_bias"], LANES)
    a_log = _pad_cols(wt["ssd_a_log"], LANES)
    d_skip = _pad_cols(wt["ssd_d"], LANES)
    r_k = wt["rwkv_r_k"].reshape(1, w)

    h1 = norm_fwd("norm_mix", x, wt["norm_mix_g"], tr)
    u = matmul("in_proj", h1, w_perm)
    z, xbc, dtraw = seg_cols(u, "z"), seg_cols(u, "xbc"), seg_cols(u, "dt")
    urkv = seg_cols(u, "rkv")
    ulora = u[:, offs["pw"]:offs["pw"] + lora_w]

    halo_xbc = _halo(xbc, tr)
    ssd_pre_t = [(xbc, tr, conv_dim, 0), (halo_xbc, HALO, conv_dim, 0), (dtraw, tr, LANES, 0)]
    ssd_pre_f = cw + [wt["ssd_conv_b"], dt_bias]
    act, dt = fn_fwd("ssd_pre", _ssd_pre, nt, ssd_pre_t, ssd_pre_f, [(t, tr, conv_dim, F32), (t, tr, LANES, F32)])

    nb = w // LANES
    ssd_seq = [(act, None), (act, lambda p: nb + p // ppg), (act, lambda p: nb + SSD_GROUPS + p // ppg), (dt, lambda p: 0)]
    ssd_ppb = min(ppg, PAIRS_PER_STEP)
    rw_ppb = min(n_pairs, PAIRS_PER_STEP)

    def ssd_fn(sv, cv, hts, ids):
        return [_ssd_chunk(*s, cv[0], ht, p) for s, ht, p in zip(sv, hts, ids)]

    y_scan, ssd_states = scan_fwd("ssd_scan", ssd_fn, SSD_CHUNK, ssd_seq, [a_log], n_pairs, ssd_ppb)
    ssd_post_t = [(y_scan, tr, w, 0), (act, tr, w, 0), (z, tr, w, 0)]
    ssd_post_f = [d_skip, wt["ssd_norm_g"]]
    (y_ssd,) = fn_fwd("ssd_post", _ssd_post, nt, ssd_post_t, ssd_post_f, [(t, tr, w, F32)])

    halo_rkv, halo_lora = _halo(urkv, tr), _halo(ulora, tr)
    rw_pre_t = [(urkv, tr, 3 * w, 0), (ulora, tr, lora_w, 0), (halo_rkv, HALO, 3 * w, 0), (halo_lora, HALO, lora_w, 0)]
    rw_pre_f = [mu_rkv, mu_lora, wt["rwkv_w0"], wt["rwkv_a0"], wt["rwkv_k_k"], wt["rwkv_k_a"], w2p, a2p, g2]
    rw = fn_fwd("rwkv_pre", _rwkv_pre, nt, rw_pre_t, rw_pre_f, [(t, tr, w, F32)] * 7)
    r_, lw_, k2_, v_, nkk_, b_, gate_ = rw
    rw_seq = [(a, None) for a in (r_, lw_, k2_, v_, nkk_, b_)]

    def rw_fn(sv, cv, hts, ids):
        return _rwkv_chunks([(*s, ht) for s, ht in zip(sv, hts)])

    yr_scan, rw_states = scan_fwd("rwkv_scan", rw_fn, RWKV_CHUNK, rw_seq, [], n_pairs, rw_ppb)
    rw_post_t = [(a, tr, w, 0) for a in (yr_scan, r_, k2_, v_, gate_)]
    rw_post_f = [r_k, wt["rwkv_ln_w"], wt["rwkv_ln_b"]]
    (y_rwkv,) = fn_fwd("rwkv_post", _rwkv_post, nt, rw_post_t, rw_post_f, [(t, tr, w, F32)])

    ymix = jnp.concatenate([y_ssd, y_rwkv], axis=1).astype(BF16)
    x1 = matmul("out_proj", ymix, full["w_out"], resid=x)

    h2 = norm_fwd("norm_x", x1, wt["norm_x_g"], tr)
    mrows = mem.shape[0]
    mn = norm_fwd("norm_mem", mem, wt["norm_mem_g"], mrows)
    q = matmul("xattn_q", h2, full["xattn_wq"])
    kx = matmul("xattn_k", mn, full["xattn_wk"])
    vx = matmul("xattn_v", mn, full["xattn_wv"])
    (ao,) = fn_fwd("xattn_core", _attn, nt, [(q, tr, d, 0)], [kx, vx], [(t, tr, d, BF16)])
    x2 = matmul("xattn_o", ao, full["xattn_wo"], resid=x1)

    h3 = norm_fwd("norm_ffn", x2, wt["norm_ffn_g"], tr)
    a1 = matmul("ffn_up", h3, full["ffn_w1"])
    dff = a1.shape[1]
    (f1,) = fn_fwd("ffn_act", _relu2, nt, [(a1, tr, dff, 0)], [], [(t, tr, dff, BF16)])
    x3 = matmul("ffn_down", f1, full["ffn_w2"], resid=x2)

    dx3, g_final, loss_tile = loss_head(x3, tgt, wt["final_norm_g"].reshape(1, d), tr)

    grads = {"final_norm_g": g_final.reshape(d)}
    dx3b = dx3.astype(BF16)
    grads["ffn_w2"] = matmul("ffn_down_dw", f1.T, dx3b)
    df1 = matmul("ffn_down_dx", dx3b, full["ffn_w2"], tb=True)
    (da1,), _ = fn_bwd("ffn_act_bwd", _relu2, nt, [(a1, tr, dff, 0)], [], [(df1, tr, dff, 0)], lambda c: c,
                       [(t, tr, dff, BF16)])
    grads["ffn_w1"] = matmul("ffn_up_dw", h3.T, da1)
    dh3 = matmul("ffn_up_dx", da1, full["ffn_w1"], tb=True)
    dx2, grads["norm_ffn_g"] = norm_bwd("norm_ffn_bwd", x2, wt["norm_ffn_g"], dh3, dx3, tr)

    dx2b = dx2.astype(BF16)
    grads["xattn_wo"] = matmul("xattn_o_dw", ao.T, dx2b)
    dao = matmul("xattn_o_dx", dx2b, full["xattn_wo"], tb=True)
    (dq,), (dkx, dvx) = fn_bwd("xattn_core_bwd", _attn, nt, [(q, tr, d, 0)], [kx, vx], [(dao, tr, d, 0)], lambda c: c,
                               [(t, tr, d, BF16)])
    grads["xattn_wq"] = matmul("xattn_q_dw", h2.T, dq)
    dh2 = matmul("xattn_q_dx", dq, full["xattn_wq"], tb=True)
    dkb, dvb = dkx.astype(BF16), dvx.astype(BF16)
    grads["xattn_wk"] = matmul("xattn_k_dw", mn.T, dkb)
    grads["xattn_wv"] = matmul("xattn_v_dw", mn.T, dvb)
    dmn = matmul("xattn_k_dx", dkb, full["xattn_wk"], tb=True)
    dmn = matmul("xattn_v_dx", dvb, full["xattn_wv"], tb=True, resid=dmn)
    _, grads["norm_mem_g"] = norm_bwd("norm_mem_bwd", mem, wt["norm_mem_g"], dmn, None, mrows)
    dx1, grads["norm_x_g"] = norm_bwd("norm_x_bwd", x1, wt["norm_x_g"], dh2, dx2, tr)

    dx1b = dx1.astype(BF16)
    grads["w_out"] = matmul("out_proj_dw", ymix.T, dx1b)
    dymix = matmul("out_proj_dx", dx1b, full["w_out"], tb=True)

    (dyr, dr1, dk1, dv1, dgate), (g_rk, grads["rwkv_ln_w"], grads["rwkv_ln_b"]) = fn_bwd(
        "rwkv_post_bwd", _rwkv_post, nt, rw_post_t, rw_post_f, [(dymix, tr, w, 1)], lambda c: c, [(t, tr, w, F32)] * 5)
    grads["rwkv_r_k"] = g_rk.reshape(wt["rwkv_r_k"].shape)
    (dr2, dlw, dk2, dv2, dnkk, db), _ = scan_bwd("rwkv_scan_bwd", rw_fn, RWKV_CHUNK, rw_seq, [], rw_states, dyr, n_pairs, rw_ppb)
    rw_ct = [(a, tr, w, 0) for a in (dr1, dr2, dlw, dk1, dk2, dv1, dv2, dnkk, db, dgate)]

    def rw_ct_fn(c):
        return (c[0] + c[1], c[2], c[3] + c[4], c[5] + c[6], c[7], c[8], c[9])

    (durkv, dulora, dhrkv, dhlora), rw_pg = fn_bwd(
        "rwkv_pre_bwd", _rwkv_pre, nt, rw_pre_t, rw_pre_f, rw_ct, rw_ct_fn,
        [(t, tr, 3 * w, F32), (t, tr, lora_w, F32), (nt * HALO, HALO, 3 * w, F32), (nt * HALO, HALO, lora_w, F32)])
    durkv = _unhalo(durkv, dhrkv, tr)
    dulora = _unhalo(dulora, dhlora, tr)
    g_mu_rkv, g_mu_lora, grads["rwkv_w0"], grads["rwkv_a0"], grads["rwkv_k_k"], grads["rwkv_k_a"], g_w2p, g_a2p, grads["rwkv_g2"] = rw_pg
    grads["rwkv_mu"] = jnp.concatenate([g_mu_rkv, g_mu_lora[:, :dr], g_mu_lora[:, LANES:LANES + ar], g_mu_lora[:, 2 * LANES:]], axis=1)
    grads["rwkv_w2"] = g_w2p[:dr]
    grads["rwkv_a2"] = g_a2p[:ar]

    (dys, dxs1, dz), (g_d, grads["ssd_norm_g"]) = fn_bwd(
        "ssd_post_bwd", _ssd_post, nt, ssd_post_t, ssd_post_f, [(dymix, tr, w, 0)], lambda c: c, [(t, tr, w, F32)] * 3)
    grads["ssd_d"] = g_d[:, :nh]
    (dxs2, dbp, dcp, ddtp), (g_alog,) = scan_bwd("ssd_scan_bwd", ssd_fn, SSD_CHUNK, ssd_seq, [a_log], ssd_states, dys, n_pairs, ssd_ppb)
    grads["ssd_a_log"] = g_alog[:, :nh]
    ssd_ct = [(dxs1, tr, w, 0), (dxs2, tr, w, 0), (dbp, tr, w, 0), (dcp, tr, w, 0), (ddtp, tr, w, 0)]

    def ssd_ct_fn(c):
        def group_sum(a):
            parts = []
            for gi in range(SSD_GROUPS):
                s = a[:, gi * ppg * LANES:(gi * ppg + 1) * LANES]
                for j in range(1, ppg):
                    s = s + a[:, (gi * ppg + j) * LANES:(gi * ppg + j + 1) * LANES]
                parts.append(s)
            return parts
        ddt = c[4][:, :LANES]
        for j in range(1, n_pairs):
            ddt = ddt + c[4][:, j * LANES:(j + 1) * LANES]
        return (jnp.concatenate([c[0] + c[1]] + group_sum(c[2]) + group_sum(c[3]), axis=1), ddt)

    (dxbc, dhxbc, ddtraw), ssd_pg = fn_bwd(
        "ssd_pre_bwd", _ssd_pre, nt, ssd_pre_t, ssd_pre_f, ssd_ct, ssd_ct_fn,
        [(t, tr, conv_dim, F32), (nt * HALO, HALO, conv_dim, F32), (t, tr, LANES, F32)])
    dxbc = _unhalo(dxbc, dhxbc, tr)
    grads["ssd_conv_w"] = jnp.concatenate(ssd_pg[:SSD_CONV], axis=0)
    grads["ssd_conv_b"] = ssd_pg[SSD_CONV]
    grads["ssd_dt_bias"] = ssd_pg[SSD_CONV + 1][:, :nh]

    du = jnp.concatenate([dz, dxbc, ddtraw, durkv, dulora], axis=1).astype(BF16)
    g_perm = matmul("in_proj_dw", h1.T, du)
    grads["w_in"] = jnp.concatenate([seg_cols(g_perm, nm, segs[nm][1]) for nm in order], axis=1)
    dh1 = matmul("in_proj_dx", du, w_perm, tb=True)
    grad_x, grads["norm_mix_g"] = norm_bwd("norm_mix_bwd", x, wt["norm_mix_g"], dh1, dx1, tr)
    return loss_tile, grad_x, grads


def _pack(arrs):
    flat = jnp.concatenate([a.reshape(-1) for a in arrs])
    n = flat.shape[0]
    rows = -(-n // (8 * LANES)) * 8
    return jnp.pad(flat, (0, rows * LANES - n)).reshape(rows, LANES)


def _unpack(packed, shapes):
    flat = packed.reshape(-1)
    out, o = [], 0
    for s in shapes:
        n = math.prod(s)
        out.append(flat[o:o + n].reshape(s))
        o += n
    return out


def _as2d(a):
    return a.reshape(-1, a.shape[-1])


def _step(a):
    x, mem, tgt = a["x"][0], a["mem"][0], a["loss_target"][0]
    q = 2 * lax.axis_index("x") + lax.axis_index("y")

    shard2d = {n: _as2d(a[n][0]) for n in BIG}
    small_sh = {n: _as2d(a[n][0]) for n in SMALL_SHARDED}
    gathered = gather_two_level("gather_weights", [shard2d[n].astype(BF16) for n in BIG])
    full = {n: _from_slots(g.reshape(4, 2 * g.shape[2], g.shape[3]), BIG_AXIS[n]) for n, g in zip(BIG, gathered)}
    gathered = chip_exchange("gather_small", [small_sh[n] for n in SMALL_SHARDED], True)
    for n, g in zip(SMALL_SHARDED, gathered):
        full[n] = _from_slots(g, 1)

    wt = {n: (a[n] if a[n].ndim <= 2 else a[n][0]) for n in WEIGHTS if n not in BIG and n not in SMALL_SHARDED}
    for n in SMALL_SHARDED:
        wt[n] = small_sh[n]
    loss_tile, grad_x, grads = _local_grads(x, mem, tgt, wt, full)

    c = lax.axis_index("c")
    kept, sent = [], []
    for n in BIG:
        s = _to_slots(grads[n], BIG_AXIS[n])
        s = s.reshape(4, 2, s.shape[1] // 2, s.shape[2])
        kept.append(lax.dynamic_index_in_dim(s, c, axis=1, keepdims=False))
        sent.append(lax.dynamic_index_in_dim(s, 1 - c, axis=1, keepdims=False))
    got = core_swap("swap_halves", sent)
    chip_parts = []
    for n, k, g in zip(BIG, kept, got):
        _, hr, cols = k.shape
        tr = _pick(4 * hr, (256, 128, 64, 32, 16))
        (part,), _ = row_call("chip_sum_" + n, lambda tv, fv: ([tv[0] + tv[1]], []), 4 * hr // tr,
                              [(k.reshape(4 * hr, cols), tr, cols, 0), (g.reshape(4 * hr, cols), tr, cols, 0)], [],
                              [(4 * hr, tr, cols, BF16)], [])
        chip_parts.append(part.reshape(4, hr, cols))
    slots = chip_exchange("scatter_grads", chip_parts, False)
    halves = [sum_slots("sum_" + n, s) for n, s in zip(BIG, slots)]
    others = core_swap("swap_reduced", halves)

    out = {}
    for n, mine, other in zip(BIG, halves, others):
        gsum = jnp.concatenate([jnp.where(c == 0, mine, other), jnp.where(c == 0, other, mine)], axis=0)
        g, dlt, mn, vn = adamw("adamw_" + n, shard2d[n], _as2d(a["m_" + n][0]), _as2d(a["v_" + n][0]), [gsum])
        for key, val in (("grad_", g), ("delta_", dlt), ("new_m_", mn), ("new_v_", vn)):
            out[key + n] = val.reshape(a[n].shape)

    small = [n for n in WEIGHTS if n not in BIG]
    red = _unpack(all_reduce_small("all_reduce_small", _pack([grads[n] for n in small])), [grads[n].shape for n in small])
    g_loc = {}
    for n, g in zip(small, red):
        if n in SMALL_SHARDED:
            cols = g.shape[1] // 4
            g = lax.dynamic_slice_in_dim(g, q * cols, cols, axis=1)
        g_loc[n] = g.reshape(a[n].shape)
    res = adamw("adamw_small", *[_pack([src[n] for n in small]) for src in
                                 ({n: a[n] for n in small}, {n: a["m_" + n] for n in small}, {n: a["v_" + n] for n in small})],
                [_pack([g_loc[n] for n in small])])
    shapes = [a[n].shape for n in small]
    for key, packed in zip(("grad_", "delta_", "new_m_", "new_v_"), res):
        for n, val in zip(small, _unpack(packed, shapes)):
            out[key + n] = val

    loss = lax.psum(loss_tile[0, 0], ("x", "y", "c"))
    ordered = [loss, grad_x.reshape(a["x"].shape)]
    for key in ("grad_", "delta_", "new_m_", "new_v_"):
        ordered += [out[key + n] for n in WEIGHTS]
    return tuple(ordered)


def kernel(x, mem, norm_mix_g, w_in, ssd_conv_w, ssd_conv_b, ssd_dt_bias, ssd_a_log, ssd_d, ssd_norm_g, rwkv_mu, rwkv_w0, rwkv_w2, rwkv_a0, rwkv_a2, rwkv_g2, rwkv_k_k, rwkv_k_a, rwkv_r_k, rwkv_ln_w, rwkv_ln_b, w_out, norm_x_g, norm_mem_g, xattn_wq, xattn_wk, xattn_wv, xattn_wo, norm_ffn_g, ffn_w1, ffn_w2, final_norm_g, loss_target, m_norm_mix_g, m_w_in, m_ssd_conv_w, m_ssd_conv_b, m_ssd_dt_bias, m_ssd_a_log, m_ssd_d, m_ssd_norm_g, m_rwkv_mu, m_rwkv_w0, m_rwkv_w2, m_rwkv_a0, m_rwkv_a2, m_rwkv_g2, m_rwkv_k_k, m_rwkv_k_a, m_rwkv_r_k, m_rwkv_ln_w, m_rwkv_ln_b, m_w_out, m_norm_x_g, m_norm_mem_g, m_xattn_wq, m_xattn_wk, m_xattn_wv, m_xattn_wo, m_norm_ffn_g, m_ffn_w1, m_ffn_w2, m_final_norm_g, v_norm_mix_g, v_w_in, v_ssd_conv_w, v_ssd_conv_b, v_ssd_dt_bias, v_ssd_a_log, v_ssd_d, v_ssd_norm_g, v_rwkv_mu, v_rwkv_w0, v_rwkv_w2, v_rwkv_a0, v_rwkv_a2, v_rwkv_g2, v_rwkv_k_k, v_rwkv_k_a, v_rwkv_r_k, v_rwkv_ln_w, v_rwkv_ln_b, v_w_out, v_norm_x_g, v_norm_mem_g, v_xattn_wq, v_xattn_wk, v_xattn_wv, v_xattn_wo, v_norm_ffn_g, v_ffn_w1, v_ffn_w2, v_final_norm_g):
    return _step(dict(locals()))
```

```python
import functools
import math

import jax
import jax.numpy as jnp
from jax import lax
from jax.experimental import pallas as pl
from jax.experimental.pallas import tpu as pltpu

F32 = jnp.float32
BF16 = jnp.bfloat16
HIGHEST = lax.Precision.HIGHEST
MESH_ID = pl.DeviceIdType.MESH

NORM_EPS = 1e-6
RWKV_LN_EPS = 64e-5
HEAD_DIM = 64
PAIR = 2 * HEAD_DIM
LANES = 128
SSD_STATE = 128
SSD_CHUNK = 128
SSD_GROUPS = 2
SSD_CONV = 4
RWKV_CHUNK = 64
HALO = 8
ROW_TILE = 128
PAIRS_PER_STEP = 4
XATTN_HEADS = 4
RWKV_PASSES = 1
VMEM_LIMIT = 56 * 1024 * 1024
MATMUL_VMEM = 40 * 1024 * 1024

ADAM_LR = 0.001
ADAM_B1 = 0.9
ADAM_B2 = 0.999
ADAM_EPS = 1e-08
ADAM_WD = 0.01
ADAM_STEP = 10


def _dims(ca, cb):
    return (((ca,), (cb,)), ((), ()))


def _split_bf16(a):
    hi = a.astype(BF16)
    lo = (a - hi.astype(F32)).astype(BF16)
    return hi, lo


def _mm_impl(a, b, ca, cb, passes):
    dn = _dims(ca, cb)
    if passes == 1:
        return lax.dot_general(a.astype(BF16), b.astype(BF16), dn, preferred_element_type=F32)
    ah, al = _split_bf16(a)
    bh, bl = _split_bf16(b)
    out = lax.dot_general(ah, bh, dn, preferred_element_type=F32)
    out = out + lax.dot_general(ah, bl, dn, preferred_element_type=F32)
    return out + lax.dot_general(al, bh, dn, preferred_element_type=F32)


@functools.partial(jax.custom_vjp, nondiff_argnums=(2, 3, 4))
def mm(a, b, ca, cb, passes):
    return _mm_impl(a, b, ca, cb, passes)


def _mm_fwd(a, b, ca, cb, passes):
    return _mm_impl(a, b, ca, cb, passes), (a, b)


def _mm_bwd(ca, cb, passes, res, g):
    a, b = res
    da = mm(g, b, 1, 1 - cb, passes) if ca == 1 else mm(b, g, 1 - cb, 1, passes)
    db = mm(a, g, 1 - ca, 0, passes) if cb == 0 else mm(g, a, 0, 1 - ca, passes)
    return da, db


mm.defvjp(_mm_fwd, _mm_bwd)


def _dot_exact(a, b):
    return lax.dot_general(a, b, _dims(1, 0), precision=HIGHEST, preferred_element_type=F32)


def _iota(shape, dim):
    return lax.broadcasted_iota(jnp.int32, shape, dim)


def _sigmoid(x):
    return 1.0 / (1.0 + jnp.exp(-x))


def _silu(x):
    return x * _sigmoid(x)


def _softplus(x):
    return jnp.maximum(x, 0.0) + jnp.log(1.0 + jnp.exp(-jnp.abs(x)))


def _rms(x, g):
    return x * lax.rsqrt(jnp.mean(x * x, axis=-1, keepdims=True) + NORM_EPS) * g


def _select_mm(x, sel):
    hi = x.astype(BF16)
    r1 = x - hi.astype(F32)
    mid = r1.astype(BF16)
    lo = (r1 - mid.astype(F32)).astype(BF16)
    dn = _dims(1, 0)
    out = lax.dot_general(hi, sel, dn, preferred_element_type=F32)
    out = out + lax.dot_general(mid, sel, dn, preferred_element_type=F32)
    return out + lax.dot_general(lo, sel, dn, preferred_element_type=F32)


def _head_sum_impl(x, n):
    sel = (_iota((n, LANES), 0) // HEAD_DIM == _iota((n, LANES), 1)).astype(BF16)
    return _select_mm(x, sel)


def _head_expand_impl(s, n):
    sel = (_iota((LANES, n), 1) // HEAD_DIM == _iota((LANES, n), 0)).astype(BF16)
    return _select_mm(s, sel)


@functools.partial(jax.custom_vjp, nondiff_argnums=(1,))
def _head_sum_n(x, n):
    return _head_sum_impl(x, n)


@functools.partial(jax.custom_vjp, nondiff_argnums=(1,))
def _head_expand(s, n):
    return _head_expand_impl(s, n)


_head_sum_n.defvjp(lambda x, n: (_head_sum_impl(x, n), None), lambda n, _, g: (_head_expand(g, n),))
_head_expand.defvjp(lambda s, n: (_head_expand_impl(s, n), None), lambda n, _, g: (_head_sum_n(g, n),))


def _head_sum(x):
    return _head_sum_n(x, x.shape[1])


def _row_vector_expand(v, n):
    v8 = jnp.broadcast_to(v, (8, LANES))
    return jnp.sum(_head_expand(v8, n), axis=0, keepdims=True) * 0.125


def _shift_rows_impl(u, halo, s):
    rolled = pltpu.roll(u, s, 0)
    top = jnp.where(_iota((HALO, 1), 0) < s, pltpu.roll(halo, s, 0), rolled[:HALO])
    return jnp.concatenate([top, rolled[HALO:]], axis=0)


@functools.partial(jax.custom_vjp, nondiff_argnums=(2,))
def _shift_rows(u, halo, s):
    return _shift_rows_impl(u, halo, s)


def _shift_rows_bwd(s, _, g):
    tr = g.shape[0]
    rolled = pltpu.roll(g, tr - s, 0)
    hrow = _iota((HALO, 1), 0)
    bottom = jnp.where(hrow < HALO - s, rolled[tr - HALO:], 0.0)
    dhalo = jnp.where(hrow >= HALO - s, pltpu.roll(g[:HALO], HALO - s, 0), 0.0)
    return jnp.concatenate([rolled[:tr - HALO], bottom], axis=0), dhalo


_shift_rows.defvjp(lambda u, halo, s: (_shift_rows_impl(u, halo, s), None), _shift_rows_bwd)


def _params(sem):
    return pltpu.CompilerParams(dimension_semantics=sem, vmem_limit_bytes=VMEM_LIMIT)


def row_call(name, body, n_tiles, tiled, full, out_tiled, out_acc):
    nt, nf, no, na = len(tiled), len(full), len(out_tiled), len(out_acc)

    def kern(*refs):
        tv = [r[...] for r in refs[:nt]]
        fv = [r[...] for r in refs[nt:nt + nf]]
        outs, accs = body(tv, fv)
        for r, v in zip(refs[nt + nf:nt + nf + no], outs):
            r[...] = v.astype(r.dtype)
        if na:
            a_refs = refs[nt + nf + no:]
            first = pl.program_id(0) == 0

            @pl.when(first)
            def _():
                for r, v in zip(a_refs, accs):
                    r[...] = v

            @pl.when(jnp.logical_not(first))
            def _():
                for r, v in zip(a_refs, accs):
                    r[...] += v

    in_specs = [pl.BlockSpec((rt, w), functools.partial(lambda i, cb: (i, cb), cb=cb)) for (_, rt, w, cb) in tiled]
    in_specs += [pl.BlockSpec(a.shape, lambda i: (0, 0)) for a in full]
    out_specs = [pl.BlockSpec((rt, w), lambda i: (i, 0)) for (_, rt, w, _) in out_tiled]
    out_specs += [pl.BlockSpec(s, lambda i: (0, 0)) for s in out_acc]
    out_shape = [jax.ShapeDtypeStruct((rows, w), dt) for (rows, _, w, dt) in out_tiled]
    out_shape += [jax.ShapeDtypeStruct(s, F32) for s in out_acc]
    res = pl.pallas_call(
        kern, name=name, grid=(n_tiles,), in_specs=in_specs, out_specs=out_specs, out_shape=out_shape,
        compiler_params=_params(("arbitrary",)),
    )(*[t[0] for t in tiled], *full)
    return list(res[:no]), list(res[no:])


def _pick(dim, cands):
    for c in cands:
        if dim % c == 0:
            return c
    return dim


def matmul(name, a, b, tb=False, resid=None, out_dtype=F32):
    m, k = a.shape
    n = b.shape[0] if tb else b.shape[1]
    has_resid = resid is not None
    out_bytes = jnp.dtype(out_dtype).itemsize
    sizes = (2048, 1024, 896, 768, 512, 384, 256, 128)
    tm = _pick(m, sizes[1:])
    tn = _pick(n, sizes[1:])

    def vmem_bytes(tk):
        return 2 * 2 * tk * (tm + tn) + tm * tn * (2 * out_bytes + 4 + (8 if has_resid else 0))

    tk = next((c for c in sizes if k % c == 0 and vmem_bytes(c) <= MATMUL_VMEM), LANES)
    nk = k // tk

    def kern(*refs):
        a_ref, b_ref = refs[0], refs[1]
        o_ref, acc = refs[-2], refs[-1]
        kk = pl.program_id(2)
        part = lax.dot_general(a_ref[...], b_ref[...], _dims(1, 1 if tb else 0), preferred_element_type=F32)

        def finish(out):
            if has_resid:
                out = out + refs[2][...]
            o_ref[...] = out.astype(o_ref.dtype)

        if nk == 1:
            finish(part)
            return

        @pl.when(kk == 0)
        def _():
            acc[...] = part

        @pl.when(jnp.logical_and(kk > 0, kk < nk - 1))
        def _():
            acc[...] += part

        @pl.when(kk == nk - 1)
        def _():
            finish(acc[...] + part)

    in_specs = [pl.BlockSpec((tm, tk), lambda i, j, kk: (i, kk))]
    if tb:
        in_specs.append(pl.BlockSpec((tn, tk), lambda i, j, kk: (j, kk)))
    else:
        in_specs.append(pl.BlockSpec((tk, tn), lambda i, j, kk: (kk, j)))
    args = [a, b]
    if has_resid:
        in_specs.append(pl.BlockSpec((tm, tn), lambda i, j, kk: (i, j)))
        args.append(resid)
    return pl.pallas_call(
        kern, name=name, grid=(m // tm, n // tn, nk), in_specs=in_specs,
        out_specs=pl.BlockSpec((tm, tn), lambda i, j, kk: (i, j)),
        out_shape=jax.ShapeDtypeStruct((m, n), out_dtype),
        scratch_shapes=[pltpu.VMEM((tm, tn), F32)],
        compiler_params=_params(("parallel", "parallel", "arbitrary")),
    )(*args)


def norm_fwd(name, x, g, tr):
    def body(tv, fv):
        return [_rms(tv[0], fv[0])], []
    rows, d = x.shape
    (h,), _ = row_call(name, body, rows // tr, [(x, tr, d, 0)], [g], [(rows, tr, d, BF16)], [])
    return h


def norm_bwd(name, x, g, dh, extra, tr):
    def body(tv, fv):
        _, vjp = jax.vjp(_rms, tv[0], fv[0])
        dx, dg = vjp(tv[1])
        if extra is not None:
            dx = dx + tv[2]
        return [dx], [dg]
    rows, d = x.shape
    tiled = [(x, tr, d, 0), (dh, tr, d, 0)] + ([(extra, tr, d, 0)] if extra is not None else [])
    (dx,), (dg,) = row_call(name, body, rows // tr, tiled, [g], [(rows, tr, d, F32)], [g.shape])
    return dx, dg


def _ssd_pre(xbc, halo, dtraw, w0, w1, w2, w3, cb, dtb):
    y = w3 * xbc + w2 * _shift_rows(xbc, halo, 1) + w1 * _shift_rows(xbc, halo, 2) + w0 * _shift_rows(xbc, halo, 3) + cb
    return _silu(y), _softplus(dtraw + dtb)


def _ssd_post(ys, xs, z, dskip, ng):
    w = ys.shape[1]
    y = (ys + xs * _row_vector_expand(dskip, w)) * _silu(z)
    gw = w // SSD_GROUPS
    parts = []
    for gi in range(SSD_GROUPS):
        yg = y[:, gi * gw:(gi + 1) * gw]
        parts.append(yg * lax.rsqrt(jnp.mean(yg * yg, axis=-1, keepdims=True) + NORM_EPS))
    return jnp.concatenate(parts, axis=1) * ng


def _rwkv_pre(urkv, ulora, hrkv, hlora, mu_rkv, mu_lora, w0, a0, kkw, kaw, w2p, a2p, g2):
    w = w0.shape[1]
    urkv = urkv + (_shift_rows(urkv, hrkv, 1) - urkv) * mu_rkv
    ulora = ulora + (_shift_rows(ulora, hlora, 1) - ulora) * mu_lora
    r, k, v = urkv[:, :w], urkv[:, w:2 * w], urkv[:, 2 * w:]
    pw, pa, pg = ulora[:, :LANES], ulora[:, LANES:2 * LANES], ulora[:, 2 * LANES:]
    w_log = -_softplus(-(w0 + mm(jnp.tanh(pw), w2p, 1, 0, 1))) - 0.5
    lw = -jnp.exp(w_log)
    iclr = _sigmoid(a0 + mm(pa, a2p, 1, 0, 1))
    gate = mm(_sigmoid(pg), g2, 1, 0, 1)
    kk = k * kkw
    kk = kk / jnp.maximum(jnp.sqrt(_head_expand(_head_sum(kk * kk), w)), 1e-12)
    k2 = k * (1.0 + (iclr - 1.0) * kaw)
    return r, lw, k2, v, -kk, kk * iclr, gate


def _rwkv_post(ys, r, k2, v, gate, rk, lnw, lnb):
    w = ys.shape[1]
    inv = 1.0 / HEAD_DIM
    mean = _head_expand(_head_sum(ys), w) * inv
    d = ys - mean
    var = _head_expand(_head_sum(d * d), w) * inv
    yn = d * lax.rsqrt(var + RWKV_LN_EPS) * lnw + lnb
    bonus = _head_expand(_head_sum(r * k2 * rk), w) * v
    return (yn + bonus) * gate


def _attn(q, k, v):
    d = q.shape[1]
    hd = d // XATTN_HEADS
    outs = []
    for h in range(XATTN_HEADS):
        sl = slice(h * hd, (h + 1) * hd)
        s = mm(q[:, sl], k[:, sl], 1, 1, 1) * (hd ** -0.5)
        s = s - jnp.max(s, axis=-1, keepdims=True)
        p = jnp.exp(s)
        p = p / jnp.sum(p, axis=-1, keepdims=True)
        outs.append(mm(p, v[:, sl], 1, 0, 1))
    return jnp.concatenate(outs, axis=1)


def _relu2(a):
    return jnp.square(jnp.maximum(a, 0.0))


def fn_fwd(name, fn, n_tiles, tiled, full, out_tiled):
    def body(tv, fv):
        outs = fn(*tv, *fv)
        return (list(outs) if isinstance(outs, (tuple, list)) else [outs]), []
    outs, _ = row_call(name, body, n_tiles, tiled, full, out_tiled, [])
    return outs


def fn_bwd(name, fn, n_tiles, tiled, full, cts, ct_fn, out_tiled):
    nt = len(tiled)

    def body(tv, fv):
        outs, vjp = jax.vjp(fn, *tv[:nt], *fv)
        ct = ct_fn(tv[nt:])
        grads = vjp(tuple(ct) if isinstance(outs, (tuple, list)) else ct[0])
        return list(grads[:nt]), list(grads[nt:])
    return row_call(name, body, n_tiles, tiled + cts, full, out_tiled, [f.shape for f in full])


def _ssd_chunk(xs, bm, cm, dt_all, a_log, ht, p):
    q = xs.shape[0]
    lane = _iota((1, LANES), 1)
    row = _iota((q, 1), 0)
    tril = _iota((q, q), 0) >= _iota((q, q), 1)
    half = lane < HEAD_DIM
    da = dt_all * (-jnp.exp(a_log))
    cs = _dot_exact(tril.astype(F32), da)

    def col(mat, h):
        return jnp.sum(jnp.where(lane == h, mat, 0.0), axis=1, keepdims=True)

    cs0, cs1 = col(cs, 2 * p), col(cs, 2 * p + 1)
    xdt = xs * jnp.where(half, col(dt_all, 2 * p), col(dt_all, 2 * p + 1))
    csx = jnp.where(half, cs0, cs1)
    last = jnp.sum(jnp.where(row == q - 1, csx, 0.0), axis=0, keepdims=True)
    cb = mm(cm, bm, 1, 1, 1)
    y = mm(cm, ht, 1, 0, 1) * jnp.exp(csx)
    for csh, hm in ((cs0, half), (cs1, jnp.logical_not(half))):
        csl = jnp.broadcast_to(csh, (q, q))
        seg = csl - csl.T
        lmat = jnp.where(tril, jnp.exp(jnp.where(tril, seg, 0.0)), 0.0)
        y = y + jnp.where(hm, mm(cb * lmat, xdt, 1, 0, 1), 0.0)
    st = mm(bm, xdt * jnp.exp(last - csx), 0, 0, 1)
    return y, ht * jnp.exp(last) + st


def _rwkv_chunks(pairs):
    c = pairs[0][0].shape[0]
    ps = RWKV_PASSES
    lane = _iota((1, LANES), 1)
    row = _iota((c, 1), 0)
    ri, ci = _iota((c, c), 0), _iota((c, c), 1)
    tril_i, tril_s = ri >= ci, ri > ci
    eye = (ri == ci).astype(F32)
    half = lane < HEAD_DIM
    halves = (half, jnp.logical_not(half))
    bd = (_iota((LANES, LANES), 0) < HEAD_DIM) == (_iota((LANES, LANES), 1) < HEAD_DIM)
    tri = tril_i.astype(F32)
    n = len(pairs)
    heads = [(j, hm) for j in range(n) for hm in halves]

    cum = [_dot_exact(tri, p[1]) for p in pairs]
    at = [p[4] * jnp.exp(cm - p[1]) for p, cm in zip(pairs, cum)]
    en = [jnp.exp(-cm) for cm in cum]
    bt = [p[5] * e for p, e in zip(pairs, en)]
    kt = [p[2] * e for p, e in zip(pairs, en)]
    rt = [p[0] * jnp.exp(cm) for p, cm in zip(pairs, cum)]
    ah = [mm(at[j], pairs[j][6], 1, 1, ps) for j in range(n)]
    y = [mm(rt[j], pairs[j][6], 1, 1, ps) for j in range(n)]
    atm = [jnp.where(hm, at[j], 0.0) for j, hm in heads]
    rtm = [jnp.where(hm, rt[j], 0.0) for j, hm in heads]
    aab = [jnp.where(tril_s, mm(atm[i], bt[j], 1, 1, ps), 0.0) for i, (j, _) in enumerate(heads)]
    aak = [jnp.where(tril_s, mm(atm[i], kt[j], 1, 1, ps), 0.0) for i, (j, _) in enumerate(heads)]
    arb = [jnp.where(tril_i, mm(rtm[i], bt[j], 1, 1, ps), 0.0) for i, (j, _) in enumerate(heads)]
    ark = [jnp.where(tril_i, mm(rtm[i], kt[j], 1, 1, ps), 0.0) for i, (j, _) in enumerate(heads)]
    rhs = [ah[j] + mm(aak[i], pairs[j][3], 1, 0, ps) for i, (j, _) in enumerate(heads)]
    yv = [mm(ark[i], pairs[j][3], 1, 0, ps) for i, (j, _) in enumerate(heads)]
    tm = [eye + a_ for a_ in aab]
    pm = aab
    for _ in range(int(math.log2(c)) - 1):
        pm = [mm(p_, p_, 1, 0, ps) for p_ in pm]
        tm = [t_ + mm(t_, p_, 1, 0, ps) for t_, p_ in zip(tm, pm)]
    uh = [mm(tm[i], rhs[i], 1, 0, ps) for i in range(len(heads))]
    u = [jnp.where(half, uh[2 * j], uh[2 * j + 1]) for j in range(n)]
    yu = [mm(arb[i], u[j], 1, 0, ps) for i, (j, _) in enumerate(heads)]
    out = []
    for j in range(n):
        yj = y[j] + jnp.where(half, yu[2 * j] + yv[2 * j], yu[2 * j + 1] + yv[2 * j + 1])
        plast = jnp.sum(jnp.where(row == c - 1, cum[j], 0.0), axis=0, keepdims=True)
        upd = pairs[j][6] + mm(u[j], bt[j], 0, 0, ps) + mm(pairs[j][3], kt[j], 0, 0, ps)
        out.append((yj, jnp.where(bd, upd * jnp.exp(plast), 0.0)))
    return out


def _seq_spec(chunk, ppb, col, row_of):
    if col is None:
        return pl.BlockSpec((chunk, ppb * LANES), lambda pb, i: (row_of(i), pb))
    return pl.BlockSpec((chunk, LANES), lambda pb, i: (row_of(i), col(pb * ppb)))


def _pair_vals(refs, seq_in, j):
    return [r[...] if col is not None else r[:, j * LANES:(j + 1) * LANES] for r, (_, col) in zip(refs, seq_in)]


def scan_fwd(name, chunk_fn, chunk, seq_in, const_in, n_pairs, ppb):
    t = seq_in[0][0].shape[0]
    nc = t // chunk
    ns, ncst = len(seq_in), len(const_in)

    def kern(*refs):
        y_ref, st_ref, ht = refs[ns + ncst], refs[ns + ncst + 1], refs[ns + ncst + 2]

        @pl.when(pl.program_id(1) == 0)
        def _():
            ht[...] = jnp.zeros_like(ht)

        cv = [r[...] for r in refs[ns:ns + ncst]]
        h0 = [ht[j] for j in range(ppb)]
        for j in range(ppb):
            st_ref[j] = h0[j]
        sv = [_pair_vals(refs[:ns], seq_in, j) for j in range(ppb)]
        outs = chunk_fn(sv, cv, h0, [pl.program_id(0) * ppb + j for j in range(ppb)])
        for j, (y, hn) in enumerate(outs):
            y_ref[:, j * LANES:(j + 1) * LANES] = y
            ht[j] = hn

    in_specs = [_seq_spec(chunk, ppb, col, lambda i: i) for (_, col) in seq_in]
    in_specs += [pl.BlockSpec(a.shape, lambda pb, i: (0, 0)) for a in const_in]
    return pl.pallas_call(
        kern, name=name, grid=(n_pairs // ppb, nc), in_specs=in_specs,
        out_specs=[pl.BlockSpec((chunk, ppb * LANES), lambda pb, i: (i, pb)),
                   pl.BlockSpec((ppb, None, LANES, LANES), lambda pb, i: (pb, i, 0, 0))],
        out_shape=[jax.ShapeDtypeStruct((t, n_pairs * LANES), F32), jax.ShapeDtypeStruct((n_pairs, nc, LANES, LANES), F32)],
        scratch_shapes=[pltpu.VMEM((ppb, LANES, LANES), F32)],
        compiler_params=_params(("arbitrary", "arbitrary")),
    )(*[s[0] for s in seq_in], *const_in)


def scan_bwd(name, chunk_fn, chunk, seq_in, const_in, states, dy, n_pairs, ppb):
    t = dy.shape[0]
    nc = t // chunk
    ns, ncst = len(seq_in), len(const_in)

    def kern(*refs):
        seq_refs, cst_refs = refs[:ns], refs[ns:ns + ncst]
        st_ref, dy_ref = refs[ns + ncst], refs[ns + ncst + 1]
        o = ns + ncst + 2
        dseq_refs, dcst_refs, dht = refs[o:o + ns], refs[o + ns:o + ns + ncst], refs[o + ns + ncst]
        pb, i = pl.program_id(0), pl.program_id(1)

        @pl.when(i == 0)
        def _():
            dht[...] = jnp.zeros_like(dht)

        ids = [pb * ppb + j for j in range(ppb)]
        lanes = [slice(j * LANES, (j + 1) * LANES) for j in range(ppb)]

        def fn(*flat):
            sv = [list(flat[j * ns:(j + 1) * ns]) for j in range(ppb)]
            outs = chunk_fn(sv, list(flat[ppb * ns:ppb * ns + ncst]), list(flat[ppb * ns + ncst:]), ids)
            return tuple(y for y, _ in outs), tuple(h for _, h in outs)

        flat_in = [v for j in range(ppb) for v in _pair_vals(seq_refs, seq_in, j)]
        flat_in += [r[...] for r in cst_refs] + [st_ref[j] for j in range(ppb)]
        _, vjp = jax.vjp(fn, *flat_in)
        grads = vjp((tuple(dy_ref[:, ln] for ln in lanes), tuple(dht[j] for j in range(ppb))))
        for j in range(ppb):
            for r, g in zip(dseq_refs, grads[j * ns:(j + 1) * ns]):
                r[:, lanes[j]] = g
            dht[j] = grads[ppb * ns + ncst + j]
        dcv = grads[ppb * ns:ppb * ns + ncst]
        if ncst:
            first = jnp.logical_and(pb == 0, i == 0)

            @pl.when(first)
            def _():
                for r, g in zip(dcst_refs, dcv):
                    r[...] = g

            @pl.when(jnp.logical_not(first))
            def _():
                for r, g in zip(dcst_refs, dcv):
                    r[...] += g

    rev = lambda i: nc - 1 - i
    wide = pl.BlockSpec((chunk, ppb * LANES), lambda pb, i: (rev(i), pb))
    in_specs = [_seq_spec(chunk, ppb, col, rev) for (_, col) in seq_in]
    in_specs += [pl.BlockSpec(a.shape, lambda pb, i: (0, 0)) for a in const_in]
    in_specs += [pl.BlockSpec((ppb, None, LANES, LANES), lambda pb, i: (pb, rev(i), 0, 0)), wide]
    out_specs = [wide for _ in seq_in]
    out_specs += [pl.BlockSpec(a.shape, lambda pb, i: (0, 0)) for a in const_in]
    out_shape = [jax.ShapeDtypeStruct((t, n_pairs * LANES), F32) for _ in seq_in]
    out_shape += [jax.ShapeDtypeStruct(a.shape, F32) for a in const_in]
    res = pl.pallas_call(
        kern, name=name, grid=(n_pairs // ppb, nc), in_specs=in_specs, out_specs=out_specs, out_shape=out_shape,
        scratch_shapes=[pltpu.VMEM((ppb, LANES, LANES), F32)],
        compiler_params=_params(("arbitrary", "arbitrary")),
    )(*[s[0] for s in seq_in], *const_in, states, dy)
    return list(res[:ns]), list(res[ns:])


def loss_head(x3, tgt, g, tr):
    rows, d = x3.shape

    def body(tv, fv):
        def f(x, gg):
            e = jnp.square(_rms(x, gg) - tv[1])
            return 0.5 * jnp.sum(jnp.mean(e, axis=-1, keepdims=True), axis=0, keepdims=True)
        l, vjp = jax.vjp(f, tv[0], fv[0])
        dx, dg = vjp(jnp.ones((1, 1), F32))
        return [dx], [dg, jnp.broadcast_to(l, (8, LANES))]
    (dx,), (dg, l) = row_call("loss_head", body, rows // tr, [(x3, tr, d, 0), (tgt, tr, d, 0)], [g],
                              [(rows, tr, d, F32)], [g.shape, (8, LANES)])
    return dx, dg, l


def _adam_math(w, g, m, v):
    m = ADAM_B1 * m + (1.0 - ADAM_B1) * g
    v = ADAM_B2 * v + (1.0 - ADAM_B2) * jnp.square(g)
    m_hat = m / (1.0 - ADAM_B1 ** ADAM_STEP)
    v_hat = v / (1.0 - ADAM_B2 ** ADAM_STEP)
    delta = -ADAM_LR * (m_hat / (jnp.sqrt(v_hat) + ADAM_EPS) + ADAM_WD * w)
    return delta, m, v


def adamw(name, w, m, v, g_parts):
    rows, cols = w.shape
    tr = _pick(rows, (256, 128, 64, 32, 16, 8))
    n_g = len(g_parts)

    def body(tv, fv):
        g = tv[3]
        for extra in tv[4:4 + n_g - 1]:
            g = g + extra
        delta, mn, vn = _adam_math(tv[0], g, tv[1], tv[2])
        return [g, delta, mn, vn], []
    tiled = [(a, tr, cols, 0) for a in (w, m, v, *g_parts)]
    outs, _ = row_call(name, body, rows // tr, tiled, [], [(rows, tr, cols, F32)] * 4, [])
    return outs


def sum_slots(name, r):
    _, rows, cols = r.shape
    tr = _pick(rows, (256, 128, 64, 32, 16, 8))

    def kern(r0, r1, r2, r3, o):
        o[...] = ((r0[...].astype(F32) + r1[...].astype(F32)) + r2[...].astype(F32)) + r3[...].astype(F32)

    in_specs = [pl.BlockSpec((None, tr, cols), functools.partial(lambda i, s: (s, i, 0), s=s)) for s in range(4)]
    return pl.pallas_call(
        kern, name=name, grid=(rows // tr,), in_specs=in_specs, out_specs=pl.BlockSpec((tr, cols), lambda i: (i, 0)),
        out_shape=jax.ShapeDtypeStruct((rows, cols), F32), compiler_params=_params(("arbitrary",)),
    )(r, r, r, r)


def _my_place():
    return lax.axis_index("x"), lax.axis_index("y"), lax.axis_index("c")


def chip_exchange(name, arrays, gather):
    nw = len(arrays)
    ANY = pl.BlockSpec(memory_space=pl.ANY)

    def body(*refs):
        ins, outs = refs[:nw], refs[nw:2 * nw]
        send, recv, loc = refs[2 * nw:]
        x, y, c = _my_place()
        q = 2 * x + y
        peers = [(1 - x, y), (x, 1 - y), (1 - x, 1 - y)]

        def src(w, dest_chip):
            return ins[w] if gather else ins[w].at[dest_chip]

        def remote(w, j):
            px, py = peers[j]
            return pltpu.make_async_remote_copy(
                src_ref=src(w, 2 * px + py), dst_ref=outs[w].at[q], send_sem=send.at[w, j], recv_sem=recv.at[w, j],
                device_id=(px, py, c), device_id_type=MESH_ID)

        def arrival(w, j):
            px, py = peers[j]
            return pltpu.make_async_remote_copy(
                src_ref=src(w, q), dst_ref=outs[w].at[2 * px + py], send_sem=send.at[w, j], recv_sem=recv.at[w, j],
                device_id=(px, py, c), device_id_type=MESH_ID)

        local = [pltpu.make_async_copy(src(w, q), outs[w].at[q], loc.at[w]) for w in range(nw)]
        sends = [[remote(w, j) for j in range(3)] for w in range(nw)]
        for w in range(nw):
            local[w].start()
            for j in range(3):
                sends[w][j].start()
        for w in range(nw):
            local[w].wait()
            for j in range(3):
                sends[w][j].wait_send()
                arrival(w, j).wait_recv()

    out_shape = [jax.ShapeDtypeStruct((4,) + (a.shape if gather else a.shape[1:]), a.dtype) for a in arrays]
    return pl.pallas_call(
        body, name=name, in_specs=[ANY] * nw, out_specs=[ANY] * nw, out_shape=out_shape,
        scratch_shapes=[pltpu.SemaphoreType.DMA((nw, 3)), pltpu.SemaphoreType.DMA((nw, 3)), pltpu.SemaphoreType.DMA((nw,))],
        compiler_params=pltpu.CompilerParams(has_side_effects=True),
    )(*arrays)


def gather_two_level(name, arrays):
    nw = len(arrays)
    ANY = pl.BlockSpec(memory_space=pl.ANY)

    def body(*refs):
        ins, outs = refs[:nw], refs[nw:2 * nw]
        send, recv, loc = refs[2 * nw:]
        x, y, c = _my_place()
        q = 2 * x + y
        me, sibling = (x, y, c), (x, y, 1 - c)
        peers = [(1 - x, y), (x, 1 - y), (1 - x, 1 - y)]
        chips = [2 * px + py for px, py in peers]

        def mine(w):
            hr = ins[w].shape[0] // 2
            return ins[w].at[pl.ds(c * hr, hr)]

        def copy(w, k, src, chip, half, to):
            return pltpu.make_async_remote_copy(
                src_ref=src, dst_ref=outs[w].at[chip, half], send_sem=send.at[w, k], recv_sem=recv.at[w, k],
                device_id=to, device_id_type=MESH_ID)

        local = [pltpu.make_async_copy(mine(w), outs[w].at[q, c], loc.at[w]) for w in range(nw)]
        first = [[copy(w, 0, mine(w), q, c, sibling)] + [copy(w, 1 + j, mine(w), q, c, (*peers[j], c)) for j in range(3)]
                 for w in range(nw)]
        for w in range(nw):
            local[w].start()
            for cp in first[w]:
                cp.start()
        passed = []
        for w in range(nw):
            for j in range(3):
                copy(w, 1 + j, mine(w), chips[j], c, me).wait_recv()
                fwd = copy(w, 4 + j, outs[w].at[chips[j], c], chips[j], c, sibling)
                fwd.start()
                passed.append(fwd)
        for w in range(nw):
            copy(w, 0, mine(w), q, 1 - c, me).wait_recv()
            for j in range(3):
                copy(w, 4 + j, mine(w), chips[j], 1 - c, me).wait_recv()
        for w in range(nw):
            local[w].wait()
            for cp in first[w]:
                cp.wait_send()
        for cp in passed:
            cp.wait_send()

    out_shape = [jax.ShapeDtypeStruct((4, 2, a.shape[0] // 2, a.shape[1]), a.dtype) for a in arrays]
    return pl.pallas_call(
        body, name=name, in_specs=[ANY] * nw, out_specs=[ANY] * nw, out_shape=out_shape,
        scratch_shapes=[pltpu.SemaphoreType.DMA((nw, 7)), pltpu.SemaphoreType.DMA((nw, 7)), pltpu.SemaphoreType.DMA((nw,))],
        compiler_params=pltpu.CompilerParams(has_side_effects=True),
    )(*arrays)


def core_swap(name, arrays):
    nw = len(arrays)
    ANY = pl.BlockSpec(memory_space=pl.ANY)

    def body(*refs):
        ins, outs = refs[:nw], refs[nw:2 * nw]
        send, recv = refs[2 * nw:]
        x, y, c = _my_place()
        copies = [pltpu.make_async_remote_copy(
            src_ref=ins[w], dst_ref=outs[w], send_sem=send.at[w], recv_sem=recv.at[w],
            device_id=(x, y, 1 - c), device_id_type=MESH_ID) for w in range(nw)]
        for cp in copies:
            cp.start()
        for cp in copies:
            cp.wait_send()
            cp.wait_recv()

    return pl.pallas_call(
        body, name=name, in_specs=[ANY] * nw, out_specs=[ANY] * nw,
        out_shape=[jax.ShapeDtypeStruct(a.shape, a.dtype) for a in arrays],
        scratch_shapes=[pltpu.SemaphoreType.DMA((nw,)), pltpu.SemaphoreType.DMA((nw,))],
        compiler_params=pltpu.CompilerParams(has_side_effects=True),
    )(*arrays)


def all_reduce_small(name, v):
    rows = v.shape[0]
    VM = pl.BlockSpec(memory_space=pltpu.VMEM)

    def body(v_ref, o_ref, buf, send, recv):
        x, y, c = _my_place()
        me = 4 * x + 2 * y + c

        def peer(kx):
            return (x ^ ((kx >> 2) & 1), y ^ ((kx >> 1) & 1), c ^ (kx & 1))

        def copy(kx, slot):
            return pltpu.make_async_remote_copy(
                src_ref=v_ref, dst_ref=buf.at[slot], send_sem=send.at[kx - 1], recv_sem=recv.at[kx - 1],
                device_id=peer(kx), device_id_type=MESH_ID)

        sends = [copy(kx, me) for kx in range(1, 8)]
        for cp in sends:
            cp.start()
        buf[me] = v_ref[...]
        for kx in range(1, 8):
            copy(kx, me ^ kx).wait_recv()
        for cp in sends:
            cp.wait_send()
        acc = buf[0]
        for d in range(1, 8):
            acc = acc + buf[d]
        o_ref[...] = acc

    return pl.pallas_call(
        body, name=name, in_specs=[VM], out_specs=VM, out_shape=jax.ShapeDtypeStruct(v.shape, F32),
        scratch_shapes=[pltpu.VMEM((8, rows, LANES), F32), pltpu.SemaphoreType.DMA((7,)), pltpu.SemaphoreType.DMA((7,))],
        compiler_params=pltpu.CompilerParams(has_side_effects=True, vmem_limit_bytes=VMEM_LIMIT),
    )(v)


def _pad_cols(a, n):
    return jnp.pad(a, ((0, 0), (0, n - a.shape[1])))


def _pad_rows(a, n):
    return jnp.pad(a, ((0, n - a.shape[0]), (0, 0)))


def _halo(u, tr):
    t, cdim = u.shape
    tails = u.reshape(t // tr, tr, cdim)[:, tr - HALO:, :]
    tails = jnp.concatenate([jnp.zeros((1, HALO, cdim), u.dtype), tails[:-1]], axis=0)
    return tails.reshape(-1, cdim)


def _unhalo(du, dhalo, tr):
    t, cdim = du.shape
    n = t // tr
    dh = dhalo.reshape(n, HALO, cdim)
    dh = jnp.concatenate([dh[1:], jnp.zeros((1, HALO, cdim), du.dtype)], axis=0)
    d3 = du.reshape(n, tr, cdim)
    d3 = jnp.concatenate([d3[:, :tr - HALO, :], d3[:, tr - HALO:, :] + dh], axis=1)
    return d3.reshape(t, cdim)


def _to_slots(g, axis):
    r, cdim = g.shape
    if axis == 0:
        return g.reshape(4, r // 4, cdim)
    return g.reshape(r, 4, cdim // 4).transpose(1, 0, 2)


def _from_slots(s, axis):
    if axis == 0:
        return s.reshape(s.shape[0] * s.shape[1], s.shape[2])
    return s.transpose(1, 0, 2).reshape(s.shape[1], 4 * s.shape[2])


BIG = ("w_in", "w_out", "xattn_wq", "xattn_wk", "xattn_wv", "xattn_wo", "ffn_w1", "ffn_w2")
BIG_AXIS = {"w_in": 1, "w_out": 0, "xattn_wq": 0, "xattn_wk": 0, "xattn_wv": 0, "xattn_wo": 0, "ffn_w1": 1, "ffn_w2": 0}
SMALL_SHARDED = ("ssd_conv_w", "rwkv_w2", "rwkv_a2", "rwkv_g2")
WEIGHTS = ("norm_mix_g", "w_in", "ssd_conv_w", "ssd_conv_b", "ssd_dt_bias", "ssd_a_log", "ssd_d", "ssd_norm_g",
           "rwkv_mu", "rwkv_w0", "rwkv_w2", "rwkv_a0", "rwkv_a2", "rwkv_g2", "rwkv_k_k", "rwkv_k_a", "rwkv_r_k",
           "rwkv_ln_w", "rwkv_ln_b", "w_out", "norm_x_g", "norm_mem_g", "xattn_wq", "xattn_wk", "xattn_wv", "xattn_wo",
           "norm_ffn_g", "ffn_w1", "ffn_w2", "final_norm_g")


def _local_grads(x, mem, tgt, wt, full):
    t, d = x.shape
    w = d // 2
    nh = w // HEAD_DIM
    n_pairs = nh // 2
    ppg = n_pairs // SSD_GROUPS
    bc = SSD_GROUPS * SSD_STATE
    conv_dim = w + 2 * bc
    tr = ROW_TILE
    nt = t // tr
    dr = wt["rwkv_w2"].shape[0]
    ar = wt["rwkv_a2"].shape[0]
    gr = wt["rwkv_g2"].shape[0]

    w_in = full["w_in"]
    o = 0
    segs = {}
    for nm, width in (("z", w), ("xbc", conv_dim), ("dt", nh), ("rkv", 3 * w), ("pw", dr), ("pa", ar), ("pg", gr)):
        segs[nm] = (o, width)
        o += width
    padded = {"z": w, "xbc": conv_dim, "dt": LANES, "rkv": 3 * w, "pw": LANES, "pa": LANES, "pg": gr}
    order = ("z", "xbc", "dt", "rkv", "pw", "pa", "pg")
    w_perm = jnp.concatenate([_pad_cols(w_in[:, segs[nm][0]:segs[nm][0] + segs[nm][1]], padded[nm]) for nm in order], axis=1)
    offs = {}
    o = 0
    for nm in order:
        offs[nm] = o
        o += padded[nm]
    n_perm = o
    lora_w = 2 * LANES + gr

    def seg_cols(a, nm, width=None):
        return a[:, offs[nm]:offs[nm] + (padded[nm] if width is None else width)]

    mu = wt["rwkv_mu"]
    mo = 3 * w
    mu_rkv = mu[:, :mo]
    mu_lora = jnp.concatenate([_pad_cols(mu[:, mo:mo + dr], LANES), _pad_cols(mu[:, mo + dr:mo + dr + ar], LANES),
                               mu[:, mo + dr + ar:]], axis=1)
    w2p = _pad_rows(full["rwkv_w2"], LANES)
    a2p = _pad_rows(full["rwkv_a2"], LANES)
    g2 = full["rwkv_g2"]
    conv_w = full["ssd_conv_w"]
    cw = [conv_w[i:i + 1] for i in range(SSD_CONV)]
    dt_bias = _pad_cols(wt["ssd_dt_bias"], LANES)
    a_log = _pad_cols(wt["ssd_a_log"], LANES)
    d_skip = _pad_cols(wt["ssd_d"], LANES)
    r_k = wt["rwkv_r_k"].reshape(1, w)

    h1 = norm_fwd("norm_mix", x, wt["norm_mix_g"], tr)
    u = matmul("in_proj", h1, w_perm)
    z, xbc, dtraw = seg_cols(u, "z"), seg_cols(u, "xbc"), seg_cols(u, "dt")
    urkv = seg_cols(u, "rkv")
    ulora = u[:, offs["pw"]:offs["pw"] + lora_w]

    halo_xbc = _halo(xbc, tr)
    ssd_pre_t = [(xbc, tr, conv_dim, 0), (halo_xbc, HALO, conv_dim, 0), (dtraw, tr, LANES, 0)]
    ssd_pre_f = cw + [wt["ssd_conv_b"], dt_bias]
    act, dt = fn_fwd("ssd_pre", _ssd_pre, nt, ssd_pre_t, ssd_pre_f, [(t, tr, conv_dim, F32), (t, tr, LANES, F32)])

    nb = w // LANES
    ssd_seq = [(act, None), (act, lambda p: nb + p // ppg), (act, lambda p: nb + SSD_GROUPS + p // ppg), (dt, lambda p: 0)]
    ssd_ppb = min(ppg, PAIRS_PER_STEP)
    rw_ppb = min(n_pairs, PAIRS_PER_STEP)

    def ssd_fn(sv, cv, hts, ids):
        return [_ssd_chunk(*s, cv[0], ht, p) for s, ht, p in zip(sv, hts, ids)]

    y_scan, ssd_states = scan_fwd("ssd_scan", ssd_fn, SSD_CHUNK, ssd_seq, [a_log], n_pairs, ssd_ppb)
    ssd_post_t = [(y_scan, tr, w, 0), (act, tr, w, 0), (z, tr, w, 0)]
    ssd_post_f = [d_skip, wt["ssd_norm_g"]]
    (y_ssd,) = fn_fwd("ssd_post", _ssd_post, nt, ssd_post_t, ssd_post_f, [(t, tr, w, F32)])

    halo_rkv, halo_lora = _halo(urkv, tr), _halo(ulora, tr)
    rw_pre_t = [(urkv, tr, 3 * w, 0), (ulora, tr, lora_w, 0), (halo_rkv, HALO, 3 * w, 0), (halo_lora, HALO, lora_w, 0)]
    rw_pre_f = [mu_rkv, mu_lora, wt["rwkv_w0"], wt["rwkv_a0"], wt["rwkv_k_k"], wt["rwkv_k_a"], w2p, a2p, g2]
    rw = fn_fwd("rwkv_pre", _rwkv_pre, nt, rw_pre_t, rw_pre_f, [(t, tr, w, F32)] * 7)
    r_, lw_, k2_, v_, nkk_, b_, gate_ = rw
    rw_seq = [(a, None) for a in (r_, lw_, k2_, v_, nkk_, b_)]

    def rw_fn(sv, cv, hts, ids):
        return _rwkv_chunks([(*s, ht) for s, ht in zip(sv, hts)])

    yr_scan, rw_states = scan_fwd("rwkv_scan", rw_fn, RWKV_CHUNK, rw_seq, [], n_pairs, rw_ppb)
    rw_post_t = [(a, tr, w, 0) for a in (yr_scan, r_, k2_, v_, gate_)]
    rw_post_f = [r_k, wt["rwkv_ln_w"], wt["rwkv_ln_b"]]
    (y_rwkv,) = fn_fwd("rwkv_post", _rwkv_post, nt, rw_post_t, rw_post_f, [(t, tr, w, F32)])

    ymix = jnp.concatenate([y_ssd, y_rwkv], axis=1).astype(BF16)
    x1 = matmul("out_proj", ymix, full["w_out"], resid=x)

    h2 = norm_fwd("norm_x", x1, wt["norm_x_g"], tr)
    mrows = mem.shape[0]
    mn = norm_fwd("norm_mem", mem, wt["norm_mem_g"], mrows)
    q = matmul("xattn_q", h2, full["xattn_wq"])
    kx = matmul("xattn_k", mn, full["xattn_wk"])
    vx = matmul("xattn_v", mn, full["xattn_wv"])
    (ao,) = fn_fwd("xattn_core", _attn, nt, [(q, tr, d, 0)], [kx, vx], [(t, tr, d, BF16)])
    x2 = matmul("xattn_o", ao, full["xattn_wo"], resid=x1)

    h3 = norm_fwd("norm_ffn", x2, wt["norm_ffn_g"], tr)
    a1 = matmul("ffn_up", h3, full["ffn_w1"])
    dff = a1.shape[1]
    (f1,) = fn_fwd("ffn_act", _relu2, nt, [(a1, tr, dff, 0)], [], [(t, tr, dff, BF16)])
    x3 = matmul("ffn_down", f1, full["ffn_w2"], resid=x2)

    dx3, g_final, loss_tile = loss_head(x3, tgt, wt["final_norm_g"].reshape(1, d), tr)

    grads = {"final_norm_g": g_final.reshape(d)}
    dx3b = dx3.astype(BF16)
    grads["ffn_w2"] = matmul("ffn_down_dw", f1.T, dx3b)
    df1 = matmul("ffn_down_dx", dx3b, full["ffn_w2"], tb=True)
    (da1,), _ = fn_bwd("ffn_act_bwd", _relu2, nt, [(a1, tr, dff, 0)], [], [(df1, tr, dff, 0)], lambda c: c,
                       [(t, tr, dff, BF16)])
    grads["ffn_w1"] = matmul("ffn_up_dw", h3.T, da1)
    dh3 = matmul("ffn_up_dx", da1, full["ffn_w1"], tb=True)
    dx2, grads["norm_ffn_g"] = norm_bwd("norm_ffn_bwd", x2, wt["norm_ffn_g"], dh3, dx3, tr)

    dx2b = dx2.astype(BF16)
    grads["xattn_wo"] = matmul("xattn_o_dw", ao.T, dx2b)
    dao = matmul("xattn_o_dx", dx2b, full["xattn_wo"], tb=True)
    (dq,), (dkx, dvx) = fn_bwd("xattn_core_bwd", _attn, nt, [(q, tr, d, 0)], [kx, vx], [(dao, tr, d, 0)], lambda c: c,
                               [(t, tr, d, BF16)])
    grads["xattn_wq"] = matmul("xattn_q_dw", h2.T, dq)
    dh2 = matmul("xattn_q_dx", dq, full["xattn_wq"], tb=True)
    dkb, dvb = dkx.astype(BF16), dvx.astype(BF16)
    grads["xattn_wk"] = matmul("xattn_k_dw", mn.T, dkb)
    grads["xattn_wv"] = matmul("xattn_v_dw", mn.T, dvb)
    dmn = matmul("xattn_k_dx", dkb, full["xattn_wk"], tb=True)
    dmn = matmul("xattn_v_dx", dvb, full["xattn_wv"], tb=True, resid=dmn)
    _, grads["norm_mem_g"] = norm_bwd("norm_mem_bwd", mem, wt["norm_mem_g"], dmn, None, mrows)
    dx1, grads["norm_x_g"] = norm_bwd("norm_x_bwd", x1, wt["norm_x_g"], dh2, dx2, tr)

    dx1b = dx1.astype(BF16)
    grads["w_out"] = matmul("out_proj_dw", ymix.T, dx1b)
    dymix = matmul("out_proj_dx", dx1b, full["w_out"], tb=True)

    (dyr, dr1, dk1, dv1, dgate), (g_rk, grads["rwkv_ln_w"], grads["rwkv_ln_b"]) = fn_bwd(
        "rwkv_post_bwd", _rwkv_post, nt, rw_post_t, rw_post_f, [(dymix, tr, w, 1)], lambda c: c, [(t, tr, w, F32)] * 5)
    grads["rwkv_r_k"] = g_rk.reshape(wt["rwkv_r_k"].shape)
    (dr2, dlw, dk2, dv2, dnkk, db), _ = scan_bwd("rwkv_scan_bwd", rw_fn, RWKV_CHUNK, rw_seq, [], rw_states, dyr, n_pairs, rw_ppb)
    rw_ct = [(a, tr, w, 0) for a in (dr1, dr2, dlw, dk1, dk2, dv1, dv2, dnkk, db, dgate)]

    def rw_ct_fn(c):
        return (c[0] + c[1], c[2], c[3] + c[4], c[5] + c[6], c[7], c[8], c[9])

    (durkv, dulora, dhrkv, dhlora), rw_pg = fn_bwd(
        "rwkv_pre_bwd", _rwkv_pre, nt, rw_pre_t, rw_pre_f, rw_ct, rw_ct_fn,
        [(t, tr, 3 * w, F32), (t, tr, lora_w, F32), (nt * HALO, HALO, 3 * w, F32), (nt * HALO, HALO, lora_w, F32)])
    durkv = _unhalo(durkv, dhrkv, tr)
    dulora = _unhalo(dulora, dhlora, tr)
    g_mu_rkv, g_mu_lora, grads["rwkv_w0"], grads["rwkv_a0"], grads["rwkv_k_k"], grads["rwkv_k_a"], g_w2p, g_a2p, grads["rwkv_g2"] = rw_pg
    grads["rwkv_mu"] = jnp.concatenate([g_mu_rkv, g_mu_lora[:, :dr], g_mu_lora[:, LANES:LANES + ar], g_mu_lora[:, 2 * LANES:]], axis=1)
    grads["rwkv_w2"] = g_w2p[:dr]
    grads["rwkv_a2"] = g_a2p[:ar]

    (dys, dxs1, dz), (g_d, grads["ssd_norm_g"]) = fn_bwd(
        "ssd_post_bwd", _ssd_post, nt, ssd_post_t, ssd_post_f, [(dymix, tr, w, 0)], lambda c: c, [(t, tr, w, F32)] * 3)
    grads["ssd_d"] = g_d[:, :nh]
    (dxs2, dbp, dcp, ddtp), (g_alog,) = scan_bwd("ssd_scan_bwd", ssd_fn, SSD_CHUNK, ssd_seq, [a_log], ssd_states, dys, n_pairs, ssd_ppb)
    grads["ssd_a_log"] = g_alog[:, :nh]
    ssd_ct = [(dxs1, tr, w, 0), (dxs2, tr, w, 0), (dbp, tr, w, 0), (dcp, tr, w, 0), (ddtp, tr, w, 0)]

    def ssd_ct_fn(c):
        def group_sum(a):
            parts = []
            for gi in range(SSD_GROUPS):
                s = a[:, gi * ppg * LANES:(gi * ppg + 1) * LANES]
                for j in range(1, ppg):
                    s = s + a[:, (gi * ppg + j) * LANES:(gi * ppg + j + 1) * LANES]
                parts.append(s)
            return parts
        ddt = c[4][:, :LANES]
        for j in range(1, n_pairs):
            ddt = ddt + c[4][:, j * LANES:(j + 1) * LANES]
        return (jnp.concatenate([c[0] + c[1]] + group_sum(c[2]) + group_sum(c[3]), axis=1), ddt)

    (dxbc, dhxbc, ddtraw), ssd_pg = fn_bwd(
        "ssd_pre_bwd", _ssd_pre, nt, ssd_pre_t, ssd_pre_f, ssd_ct, ssd_ct_fn,
        [(t, tr, conv_dim, F32), (nt * HALO, HALO, conv_dim, F32), (t, tr, LANES, F32)])
    dxbc = _unhalo(dxbc, dhxbc, tr)
    grads["ssd_conv_w"] = jnp.concatenate(ssd_pg[:SSD_CONV], axis=0)
    grads["ssd_conv_b"] = ssd_pg[SSD_CONV]
    grads["ssd_dt_bias"] = ssd_pg[SSD_CONV + 1][:, :nh]

    du = jnp.concatenate([dz, dxbc, ddtraw, durkv, dulora], axis=1).astype(BF16)
    g_perm = matmul("in_proj_dw", h1.T, du)
    grads["w_in"] = jnp.concatenate([seg_cols(g_perm, nm, segs[nm][1]) for nm in order], axis=1)
    dh1 = matmul("in_proj_dx", du, w_perm, tb=True)
    grad_x, grads["norm_mix_g"] = norm_bwd("norm_mix_bwd", x, wt["norm_mix_g"], dh1, dx1, tr)
    return loss_tile, grad_x, grads


def _pack(arrs):
    flat = jnp.concatenate([a.reshape(-1) for a in arrs])
    n = flat.shape[0]
    rows = -(-n // (8 * LANES)) * 8
    return jnp.pad(flat, (0, rows * LANES - n)).reshape(rows, LANES)


def _unpack(packed, shapes):
    flat = packed.reshape(-1)
    out, o = [], 0
    for s in shapes:
        n = math.prod(s)
        out.append(flat[o:o + n].reshape(s))
        o += n
    return out


def _as2d(a):
    return a.reshape(-1, a.shape[-1])


def _step(a):
    x, mem, tgt = a["x"][0], a["mem"][0], a["loss_target"][0]
    q = 2 * lax.axis_index("x") + lax.axis_index("y")

    shard2d = {n: _as2d(a[n][0]) for n in BIG}
    small_sh = {n: _as2d(a[n][0]) for n in SMALL_SHARDED}
    gathered = gather_two_level("gather_weights", [shard2d[n].astype(BF16) for n in BIG])
    full = {n: _from_slots(g.reshape(4, 2 * g.shape[2], g.shape[3]), BIG_AXIS[n]) for n, g in zip(BIG, gathered)}
    gathered = chip_exchange("gather_small", [small_sh[n] for n in SMALL_SHARDED], True)
    for n, g in zip(SMALL_SHARDED, gathered):
        full[n] = _from_slots(g, 1)

    wt = {n: (a[n] if a[n].ndim <= 2 else a[n][0]) for n in WEIGHTS if n not in BIG and n not in SMALL_SHARDED}
    for n in SMALL_SHARDED:
        wt[n] = small_sh[n]
    loss_tile, grad_x, grads = _local_grads(x, mem, tgt, wt, full)

    c = lax.axis_index("c")
    kept, sent = [], []
    for n in BIG:
        s = _to_slots(grads[n], BIG_AXIS[n])
        s = s.reshape(4, 2, s.shape[1] // 2, s.shape[2])
        kept.append(lax.dynamic_index_in_dim(s, c, axis=1, keepdims=False))
        sent.append(lax.dynamic_index_in_dim(s, 1 - c, axis=1, keepdims=False))
    got = core_swap("swap_halves", sent)
    chip_parts = []
    for n, k, g in zip(BIG, kept, got):
        _, hr, cols = k.shape
        tr = _pick(4 * hr, (256, 128, 64, 32, 16))
        (part,), _ = row_call("chip_sum_" + n, lambda tv, fv: ([tv[0] + tv[1]], []), 4 * hr // tr,
                              [(k.reshape(4 * hr, cols), tr, cols, 0), (g.reshape(4 * hr, cols), tr, cols, 0)], [],
                              [(4 * hr, tr, cols, BF16)], [])
        chip_parts.append(part.reshape(4, hr, cols))
    slots = chip_exchange("scatter_grads", chip_parts, False)
    halves = [sum_slots("sum_" + n, s) for n, s in zip(BIG, slots)]
    others = core_swap("swap_reduced", halves)

    out = {}
    for n, mine, other in zip(BIG, halves, others):
        gsum = jnp.concatenate([jnp.where(c == 0, mine, other), jnp.where(c == 0, other, mine)], axis=0)
        g, dlt, mn, vn = adamw("adamw_" + n, shard2d[n], _as2d(a["m_" + n][0]), _as2d(a["v_" + n][0]), [gsum])
        for key, val in (("grad_", g), ("delta_", dlt), ("new_m_", mn), ("new_v_", vn)):
            out[key + n] = val.reshape(a[n].shape)

    small = [n for n in WEIGHTS if n not in BIG]
    red = _unpack(all_reduce_small("all_reduce_small", _pack([grads[n] for n in small])), [grads[n].shape for n in small])
    g_loc = {}
    for n, g in zip(small, red):
        if n in SMALL_SHARDED:
            cols = g.shape[1] // 4
            g = lax.dynamic_slice_in_dim(g, q * cols, cols, axis=1)
        g_loc[n] = g.reshape(a[n].shape)
    res = adamw("adamw_small", *[_pack([src[n] for n in small]) for src in
                                 ({n: a[n] for n in small}, {n: a["m_" + n] for n in small}, {n: a["v_" + n] for n in small})],
                [_pack([g_loc[n] for n in small])])
    shapes = [a[n].shape for n in small]
    for key, packed in zip(("grad_", "delta_", "new_m_", "new_v_"), res):
        for n, val in zip(small, _unpack(packed, shapes)):
            out[key + n] = val

    loss = lax.psum(loss_tile[0, 0], ("x", "y", "c"))
    ordered = [loss, grad_x.reshape(a["x"].shape)]
    for key in ("grad_", "delta_", "new_m_", "new_v_"):
        ordered += [out[key + n] for n in WEIGHTS]
    return tuple(ordered)


def kernel(x, mem, norm_mix_g, w_in, ssd_conv_w, ssd_conv_b, ssd_dt_bias, ssd_a_log, ssd_d, ssd_norm_g, rwkv_mu, rwkv_w0, rwkv_w2, rwkv_a0, rwkv_a2, rwkv_g2, rwkv_k_k, rwkv_k_a, rwkv_r_k, rwkv_ln_w, rwkv_ln_b, w_out, norm_x_g, norm_mem_g, xattn_wq, xattn_wk, xattn_wv, xattn_wo, norm_ffn_g, ffn_w1, ffn_w2, final_norm_g, loss_target, m_norm_mix_g, m_w_in, m_ssd_conv_w, m_ssd_conv_b, m_ssd_dt_bias, m_ssd_a_log, m_ssd_d, m_ssd_norm_g, m_rwkv_mu, m_rwkv_w0, m_rwkv_w2, m_rwkv_a0, m_rwkv_a2, m_rwkv_g2, m_rwkv_k_k, m_rwkv_k_a, m_rwkv_r_k, m_rwkv_ln_w, m_rwkv_ln_b, m_w_out, m_norm_x_g, m_norm_mem_g, m_xattn_wq, m_xattn_wk, m_xattn_wv, m_xattn_wo, m_norm_ffn_g, m_ffn_w1, m_ffn_w2, m_final_norm_g, v_norm_mix_g, v_w_in, v_ssd_conv_w, v_ssd_conv_b, v_ssd_dt_bias, v_ssd_a_log, v_ssd_d, v_ssd_norm_g, v_rwkv_mu, v_rwkv_w0, v_rwkv_w2, v_rwkv_a0, v_rwkv_a2, v_rwkv_g2, v_rwkv_k_k, v_rwkv_k_a, v_rwkv_r_k, v_rwkv_ln_w, v_rwkv_ln_b, v_w_out, v_norm_x_g, v_norm_mem_g, v_xattn_wq, v_xattn_wk, v_xattn_wv, v_xattn_wo, v_norm_ffn_g, v_ffn_w1, v_ffn_w2, v_final_norm_g):
    return _step(dict(locals()))
```

```python
import functools
import math

import jax
import jax.numpy as jnp
from jax import lax
from jax.experimental import pallas as pl
from jax.experimental.pallas import tpu as pltpu

F32 = jnp.float32
BF16 = jnp.bfloat16
HIGHEST = lax.Precision.HIGHEST
MESH_ID = pl.DeviceIdType.MESH

NORM_EPS = 1e-6
RWKV_LN_EPS = 64e-5
HEAD_DIM = 64
PAIR = 2 * HEAD_DIM
LANES = 128
SSD_STATE = 128
SSD_CHUNK = 128
SSD_GROUPS = 2
SSD_CONV = 4
RWKV_CHUNK = 64
HALO = 8
ROW_TILE = 128
PAIRS_PER_STEP = 4
XATTN_HEADS = 4
RWKV_PASSES = 1
VMEM_LIMIT = 56 * 1024 * 1024
MATMUL_VMEM = 40 * 1024 * 1024

ADAM_LR = 0.001
ADAM_B1 = 0.9
ADAM_B2 = 0.999
ADAM_EPS = 1e-08
ADAM_WD = 0.01
ADAM_STEP = 10


def _dims(ca, cb):
    return (((ca,), (cb,)), ((), ()))


def _split_bf16(a):
    hi = a.astype(BF16)
    lo = (a - hi.astype(F32)).astype(BF16)
    return hi, lo


def _mm_impl(a, b, ca, cb, passes):
    dn = _dims(ca, cb)
    if passes == 1:
        return lax.dot_general(a.astype(BF16), b.astype(BF16), dn, preferred_element_type=F32)
    ah, al = _split_bf16(a)
    bh, bl = _split_bf16(b)
    out = lax.dot_general(ah, bh, dn, preferred_element_type=F32)
    out = out + lax.dot_general(ah, bl, dn, preferred_element_type=F32)
    return out + lax.dot_general(al, bh, dn, preferred_element_type=F32)


@functools.partial(jax.custom_vjp, nondiff_argnums=(2, 3, 4))
def mm(a, b, ca, cb, passes):
    return _mm_impl(a, b, ca, cb, passes)


def _mm_fwd(a, b, ca, cb, passes):
    return _mm_impl(a, b, ca, cb, passes), (a, b)


def _mm_bwd(ca, cb, passes, res, g):
    a, b = res
    da = mm(g, b, 1, 1 - cb, passes) if ca == 1 else mm(b, g, 1 - cb, 1, passes)
    db = mm(a, g, 1 - ca, 0, passes) if cb == 0 else mm(g, a, 0, 1 - ca, passes)
    return da, db


mm.defvjp(_mm_fwd, _mm_bwd)


def _dot_exact(a, b):
    return lax.dot_general(a, b, _dims(1, 0), precision=HIGHEST, preferred_element_type=F32)


def _iota(shape, dim):
    return lax.broadcasted_iota(jnp.int32, shape, dim)


def _sigmoid(x):
    return 1.0 / (1.0 + jnp.exp(-x))


def _silu(x):
    return x * _sigmoid(x)


def _softplus(x):
    return jnp.maximum(x, 0.0) + jnp.log(1.0 + jnp.exp(-jnp.abs(x)))


def _rms(x, g):
    return x * lax.rsqrt(jnp.mean(x * x, axis=-1, keepdims=True) + NORM_EPS) * g


def _select_mm(x, sel):
    hi = x.astype(BF16)
    r1 = x - hi.astype(F32)
    mid = r1.astype(BF16)
    lo = (r1 - mid.astype(F32)).astype(BF16)
    dn = _dims(1, 0)
    out = lax.dot_general(hi, sel, dn, preferred_element_type=F32)
    out = out + lax.dot_general(mid, sel, dn, preferred_element_type=F32)
    return out + lax.dot_general(lo, sel, dn, preferred_element_type=F32)


def _head_sum_impl(x, n):
    sel = (_iota((n, LANES), 0) // HEAD_DIM == _iota((n, LANES), 1)).astype(BF16)
    return _select_mm(x, sel)


def _head_expand_impl(s, n):
    sel = (_iota((LANES, n), 1) // HEAD_DIM == _iota((LANES, n), 0)).astype(BF16)
    return _select_mm(s, sel)


@functools.partial(jax.custom_vjp, nondiff_argnums=(1,))
def _head_sum_n(x, n):
    return _head_sum_impl(x, n)


@functools.partial(jax.custom_vjp, nondiff_argnums=(1,))
def _head_expand(s, n):
    return _head_expand_impl(s, n)


_head_sum_n.defvjp(lambda x, n: (_head_sum_impl(x, n), None), lambda n, _, g: (_head_expand(g, n),))
_head_expand.defvjp(lambda s, n: (_head_expand_impl(s, n), None), lambda n, _, g: (_head_sum_n(g, n),))


def _head_sum(x):
    return _head_sum_n(x, x.shape[1])


def _row_vector_expand(v, n):
    v8 = jnp.broadcast_to(v, (8, LANES))
    return jnp.sum(_head_expand(v8, n), axis=0, keepdims=True) * 0.125


def _shift_rows_impl(u, halo, s):
    rolled = pltpu.roll(u, s, 0)
    top = jnp.where(_iota((HALO, 1), 0) < s, pltpu.roll(halo, s, 0), rolled[:HALO])
    return jnp.concatenate([top, rolled[HALO:]], axis=0)


@functools.partial(jax.custom_vjp, nondiff_argnums=(2,))
def _shift_rows(u, halo, s):
    return _shift_rows_impl(u, halo, s)


def _shift_rows_bwd(s, _, g):
    tr = g.shape[0]
    rolled = pltpu.roll(g, tr - s, 0)
    hrow = _iota((HALO, 1), 0)
    bottom = jnp.where(hrow < HALO - s, rolled[tr - HALO:], 0.0)
    dhalo = jnp.where(hrow >= HALO - s, pltpu.roll(g[:HALO], HALO - s, 0), 0.0)
    return jnp.concatenate([rolled[:tr - HALO], bottom], axis=0), dhalo


_shift_rows.defvjp(lambda u, halo, s: (_shift_rows_impl(u, halo, s), None), _shift_rows_bwd)


def _params(sem):
    return pltpu.CompilerParams(dimension_semantics=sem, vmem_limit_bytes=VMEM_LIMIT)


def row_call(name, body, n_tiles, tiled, full, out_tiled, out_acc, transposed=()):
    nt, nf, na = len(tiled), len(full), len(out_acc)
    n_plain = len(out_tiled)
    no = n_plain + len(transposed)

    def kern(*refs):
        tv = [r[...] for r in refs[:nt]]
        fv = [r[...] for r in refs[nt:nt + nf]]
        outs, accs = body(tv, fv)
        for r, v in zip(refs[nt + nf:nt + nf + n_plain], outs):
            r[...] = v.astype(r.dtype)
        for r, idx in zip(refs[nt + nf + n_plain:nt + nf + no], transposed):
            r[...] = outs[idx].astype(F32).T.astype(r.dtype)
        if na:
            a_refs = refs[nt + nf + no:]
            first = pl.program_id(0) == 0

            @pl.when(first)
            def _():
                for r, v in zip(a_refs, accs):
                    r[...] = v

            @pl.when(jnp.logical_not(first))
            def _():
                for r, v in zip(a_refs, accs):
                    r[...] += v

    in_specs = [pl.BlockSpec((rt, w), functools.partial(lambda i, cb: (i, cb), cb=cb)) for (_, rt, w, cb) in tiled]
    in_specs += [pl.BlockSpec(a.shape, lambda i: (0, 0)) for a in full]
    out_specs = [pl.BlockSpec((rt, w), lambda i: (i, 0)) for (_, rt, w, _) in out_tiled]
    out_specs += [pl.BlockSpec((out_tiled[idx][2], out_tiled[idx][1]), lambda i: (0, i)) for idx in transposed]
    out_specs += [pl.BlockSpec(s, lambda i: (0, 0)) for s in out_acc]
    out_shape = [jax.ShapeDtypeStruct((rows, w), dt) for (rows, _, w, dt) in out_tiled]
    out_shape += [jax.ShapeDtypeStruct((out_tiled[idx][2], out_tiled[idx][0]), BF16) for idx in transposed]
    out_shape += [jax.ShapeDtypeStruct(s, F32) for s in out_acc]
    res = pl.pallas_call(
        kern, name=name, grid=(n_tiles,), in_specs=in_specs, out_specs=out_specs, out_shape=out_shape,
        compiler_params=_params(("arbitrary",)),
    )(*[t[0] for t in tiled], *full)
    return list(res[:no]), list(res[no:])


def _pick(dim, cands):
    for c in cands:
        if dim % c == 0:
            return c
    return dim


def matmul(name, a, b, tb=False, resid=None, out_dtype=F32):
    m, k = a.shape
    n = b.shape[0] if tb else b.shape[1]
    has_resid = resid is not None
    out_bytes = jnp.dtype(out_dtype).itemsize
    sizes = (2048, 1024, 896, 768, 512, 384, 256, 128)
    tm = _pick(m, sizes[1:])
    tn = _pick(n, sizes[1:])

    def vmem_bytes(tk):
        return 2 * 2 * tk * (tm + tn) + tm * tn * (2 * out_bytes + 4 + (8 if has_resid else 0))

    tk = next((c for c in sizes if k % c == 0 and vmem_bytes(c) <= MATMUL_VMEM), LANES)
    nk = k // tk

    def kern(*refs):
        a_ref, b_ref = refs[0], refs[1]
        o_ref, acc = refs[-2], refs[-1]
        kk = pl.program_id(2)
        part = lax.dot_general(a_ref[...], b_ref[...], _dims(1, 1 if tb else 0), preferred_element_type=F32)

        def finish(out):
            if has_resid:
                out = out + refs[2][...]
            o_ref[...] = out.astype(o_ref.dtype)

        if nk == 1:
            finish(part)
            return

        @pl.when(kk == 0)
        def _():
            acc[...] = part

        @pl.when(jnp.logical_and(kk > 0, kk < nk - 1))
        def _():
            acc[...] += part

        @pl.when(kk == nk - 1)
        def _():
            finish(acc[...] + part)

    in_specs = [pl.BlockSpec((tm, tk), lambda i, j, kk: (i, kk))]
    if tb:
        in_specs.append(pl.BlockSpec((tn, tk), lambda i, j, kk: (j, kk)))
    else:
        in_specs.append(pl.BlockSpec((tk, tn), lambda i, j, kk: (kk, j)))
    args = [a, b]
    if has_resid:
        in_specs.append(pl.BlockSpec((tm, tn), lambda i, j, kk: (i, j)))
        args.append(resid)
    return pl.pallas_call(
        kern, name=name, grid=(m // tm, n // tn, nk), in_specs=in_specs,
        out_specs=pl.BlockSpec((tm, tn), lambda i, j, kk: (i, j)),
        out_shape=jax.ShapeDtypeStruct((m, n), out_dtype),
        scratch_shapes=[pltpu.VMEM((tm, tn), F32)],
        compiler_params=_params(("parallel", "parallel", "arbitrary")),
    )(*args)


def norm_fwd(name, x, g, tr):
    def body(tv, fv):
        return [_rms(tv[0], fv[0])], []
    rows, d = x.shape
    (h, ht), _ = row_call(name, body, rows // tr, [(x, tr, d, 0)], [g], [(rows, tr, d, BF16)], [], transposed=(0,))
    return h, ht


def norm_bwd(name, x, g, dh, extra, tr):
    def body(tv, fv):
        _, vjp = jax.vjp(_rms, tv[0], fv[0])
        dx, dg = vjp(tv[1])
        if extra is not None:
            dx = dx + tv[2]
        return [dx], [dg]
    rows, d = x.shape
    tiled = [(x, tr, d, 0), (dh, tr, d, 0)] + ([(extra, tr, d, 0)] if extra is not None else [])
    (dx,), (dg,) = row_call(name, body, rows // tr, tiled, [g], [(rows, tr, d, F32)], [g.shape])
    return dx, dg


def _ssd_pre(xbc, halo, dtraw, w0, w1, w2, w3, cb, dtb):
    y = w3 * xbc + w2 * _shift_rows(xbc, halo, 1) + w1 * _shift_rows(xbc, halo, 2) + w0 * _shift_rows(xbc, halo, 3) + cb
    return _silu(y), _softplus(dtraw + dtb)


def _ssd_post(ys, xs, z, dskip, ng):
    w = ys.shape[1]
    y = (ys + xs * _row_vector_expand(dskip, w)) * _silu(z)
    gw = w // SSD_GROUPS
    parts = []
    for gi in range(SSD_GROUPS):
        yg = y[:, gi * gw:(gi + 1) * gw]
        parts.append(yg * lax.rsqrt(jnp.mean(yg * yg, axis=-1, keepdims=True) + NORM_EPS))
    return jnp.concatenate(parts, axis=1) * ng


def _rwkv_pre(urkv, ulora, hrkv, hlora, mu_rkv, mu_lora, w0, a0, kkw, kaw, w2p, a2p, g2):
    w = w0.shape[1]
    urkv = urkv + (_shift_rows(urkv, hrkv, 1) - urkv) * mu_rkv
    ulora = ulora + (_shift_rows(ulora, hlora, 1) - ulora) * mu_lora
    r, k, v = urkv[:, :w], urkv[:, w:2 * w], urkv[:, 2 * w:]
    pw, pa, pg = ulora[:, :LANES], ulora[:, LANES:2 * LANES], ulora[:, 2 * LANES:]
    w_log = -_softplus(-(w0 + mm(jnp.tanh(pw), w2p, 1, 0, 1))) - 0.5
    lw = -jnp.exp(w_log)
    iclr = _sigmoid(a0 + mm(pa, a2p, 1, 0, 1))
    gate = mm(_sigmoid(pg), g2, 1, 0, 1)
    kk = k * kkw
    kk = kk / jnp.maximum(jnp.sqrt(_head_expand(_head_sum(kk * kk), w)), 1e-12)
    k2 = k * (1.0 + (iclr - 1.0) * kaw)
    return r, lw, k2, v, -kk, kk * iclr, gate


def _rwkv_post(ys, r, k2, v, gate, rk, lnw, lnb):
    w = ys.shape[1]
    inv = 1.0 / HEAD_DIM
    mean = _head_expand(_head_sum(ys), w) * inv
    d = ys - mean
    var = _head_expand(_head_sum(d * d), w) * inv
    yn = d * lax.rsqrt(var + RWKV_LN_EPS) * lnw + lnb
    bonus = _head_expand(_head_sum(r * k2 * rk), w) * v
    return (yn + bonus) * gate


def _attn(q, k, v):
    d = q.shape[1]
    hd = d // XATTN_HEADS
    outs = []
    for h in range(XATTN_HEADS):
        sl = slice(h * hd, (h + 1) * hd)
        s = mm(q[:, sl], k[:, sl], 1, 1, 1) * (hd ** -0.5)
        s = s - jnp.max(s, axis=-1, keepdims=True)
        p = jnp.exp(s)
        p = p / jnp.sum(p, axis=-1, keepdims=True)
        outs.append(mm(p, v[:, sl], 1, 0, 1))
    return jnp.concatenate(outs, axis=1)


def _relu2(a):
    return jnp.square(jnp.maximum(a.astype(F32), 0.0))


def fn_fwd(name, fn, n_tiles, tiled, full, out_tiled, transposed=()):
    def body(tv, fv):
        outs = fn(*tv, *fv)
        return (list(outs) if isinstance(outs, (tuple, list)) else [outs]), []
    outs, _ = row_call(name, body, n_tiles, tiled, full, out_tiled, [], transposed)
    return outs


def fn_bwd(name, fn, n_tiles, tiled, full, cts, ct_fn, out_tiled):
    nt = len(tiled)

    def body(tv, fv):
        outs, vjp = jax.vjp(fn, *tv[:nt], *fv)
        ct = ct_fn(tv[nt:])
        grads = vjp(tuple(ct) if isinstance(outs, (tuple, list)) else ct[0])
        return list(grads[:nt]), list(grads[nt:])
    return row_call(name, body, n_tiles, tiled + cts, full, out_tiled, [f.shape for f in full])


def _ssd_chunk(xs, bm, cm, dt_all, a_log, ht, p):
    q = xs.shape[0]
    lane = _iota((1, LANES), 1)
    row = _iota((q, 1), 0)
    tril = _iota((q, q), 0) >= _iota((q, q), 1)
    half = lane < HEAD_DIM
    da = dt_all * (-jnp.exp(a_log))
    cs = _dot_exact(tril.astype(F32), da)

    def col(mat, h):
        return jnp.sum(jnp.where(lane == h, mat, 0.0), axis=1, keepdims=True)

    cs0, cs1 = col(cs, 2 * p), col(cs, 2 * p + 1)
    xdt = xs * jnp.where(half, col(dt_all, 2 * p), col(dt_all, 2 * p + 1))
    csx = jnp.where(half, cs0, cs1)
    last = jnp.sum(jnp.where(row == q - 1, csx, 0.0), axis=0, keepdims=True)
    cb = mm(cm, bm, 1, 1, 1)
    y = mm(cm, ht, 1, 0, 1) * jnp.exp(csx)
    for csh, hm in ((cs0, half), (cs1, jnp.logical_not(half))):
        csl = jnp.broadcast_to(csh, (q, q))
        seg = csl - csl.T
        lmat = jnp.where(tril, jnp.exp(jnp.where(tril, seg, 0.0)), 0.0)
        y = y + jnp.where(hm, mm(cb * lmat, xdt, 1, 0, 1), 0.0)
    st = mm(bm, xdt * jnp.exp(last - csx), 0, 0, 1)
    return y, ht * jnp.exp(last) + st


def _rwkv_chunks(pairs):
    c = pairs[0][0].shape[0]
    ps = RWKV_PASSES
    lane = _iota((1, LANES), 1)
    row = _iota((c, 1), 0)
    ri, ci = _iota((c, c), 0), _iota((c, c), 1)
    tril_i, tril_s = ri >= ci, ri > ci
    eye = (ri == ci).astype(F32)
    half = lane < HEAD_DIM
    halves = (half, jnp.logical_not(half))
    bd = (_iota((LANES, LANES), 0) < HEAD_DIM) == (_iota((LANES, LANES), 1) < HEAD_DIM)
    tri = tril_i.astype(F32)
    n = len(pairs)
    heads = [(j, hm) for j in range(n) for hm in halves]

    cum = [_dot_exact(tri, p[1]) for p in pairs]
    at = [p[4] * jnp.exp(cm - p[1]) for p, cm in zip(pairs, cum)]
    en = [jnp.exp(-cm) for cm in cum]
    bt = [p[5] * e for p, e in zip(pairs, en)]
    kt = [p[2] * e for p, e in zip(pairs, en)]
    rt = [p[0] * jnp.exp(cm) for p, cm in zip(pairs, cum)]
    ah = [mm(at[j], pairs[j][6], 1, 1, ps) for j in range(n)]
    y = [mm(rt[j], pairs[j][6], 1, 1, ps) for j in range(n)]
    atm = [jnp.where(hm, at[j], 0.0) for j, hm in heads]
    rtm = [jnp.where(hm, rt[j], 0.0) for j, hm in heads]
    aab = [jnp.where(tril_s, mm(atm[i], bt[j], 1, 1, ps), 0.0) for i, (j, _) in enumerate(heads)]
    aak = [jnp.where(tril_s, mm(atm[i], kt[j], 1, 1, ps), 0.0) for i, (j, _) in enumerate(heads)]
    arb = [jnp.where(tril_i, mm(rtm[i], bt[j], 1, 1, ps), 0.0) for i, (j, _) in enumerate(heads)]
    ark = [jnp.where(tril_i, mm(rtm[i], kt[j], 1, 1, ps), 0.0) for i, (j, _) in enumerate(heads)]
    rhs = [ah[j] + mm(aak[i], pairs[j][3], 1, 0, ps) for i, (j, _) in enumerate(heads)]
    yv = [mm(ark[i], pairs[j][3], 1, 0, ps) for i, (j, _) in enumerate(heads)]
    tm = [eye + a_ for a_ in aab]
    pm = aab
    for _ in range(int(math.log2(c)) - 1):
        pm = [mm(p_, p_, 1, 0, ps) for p_ in pm]
        tm = [t_ + mm(t_, p_, 1, 0, ps) for t_, p_ in zip(tm, pm)]
    uh = [mm(tm[i], rhs[i], 1, 0, ps) for i in range(len(heads))]
    u = [jnp.where(half, uh[2 * j], uh[2 * j + 1]) for j in range(n)]
    yu = [mm(arb[i], u[j], 1, 0, ps) for i, (j, _) in enumerate(heads)]
    out = []
    for j in range(n):
        yj = y[j] + jnp.where(half, yu[2 * j] + yv[2 * j], yu[2 * j + 1] + yv[2 * j + 1])
        plast = jnp.sum(jnp.where(row == c - 1, cum[j], 0.0), axis=0, keepdims=True)
        upd = pairs[j][6] + mm(u[j], bt[j], 0, 0, ps) + mm(pairs[j][3], kt[j], 0, 0, ps)
        out.append((yj, jnp.where(bd, upd * jnp.exp(plast), 0.0)))
    return out


def _seq_spec(chunk, ppb, col, row_of):
    if col is None:
        return pl.BlockSpec((chunk, ppb * LANES), lambda pb, i: (row_of(i), pb))
    return pl.BlockSpec((chunk, LANES), lambda pb, i: (row_of(i), col(pb * ppb)))


def _pair_vals(refs, seq_in, j):
    return [r[...] if col is not None else r[:, j * LANES:(j + 1) * LANES] for r, (_, col) in zip(refs, seq_in)]


def scan_fwd(name, chunk_fn, chunk, seq_in, const_in, n_pairs, ppb):
    t = seq_in[0][0].shape[0]
    nc = t // chunk
    ns, ncst = len(seq_in), len(const_in)

    def kern(*refs):
        y_ref, st_ref, ht = refs[ns + ncst], refs[ns + ncst + 1], refs[ns + ncst + 2]

        @pl.when(pl.program_id(1) == 0)
        def _():
            ht[...] = jnp.zeros_like(ht)

        cv = [r[...] for r in refs[ns:ns + ncst]]
        h0 = [ht[j] for j in range(ppb)]
        for j in range(ppb):
            st_ref[j] = h0[j]
        sv = [_pair_vals(refs[:ns], seq_in, j) for j in range(ppb)]
        outs = chunk_fn(sv, cv, h0, [pl.program_id(0) * ppb + j for j in range(ppb)])
        for j, (y, hn) in enumerate(outs):
            y_ref[:, j * LANES:(j + 1) * LANES] = y
            ht[j] = hn

    in_specs = [_seq_spec(chunk, ppb, col, lambda i: i) for (_, col) in seq_in]
    in_specs += [pl.BlockSpec(a.shape, lambda pb, i: (0, 0)) for a in const_in]
    return pl.pallas_call(
        kern, name=name, grid=(n_pairs // ppb, nc), in_specs=in_specs,
        out_specs=[pl.BlockSpec((chunk, ppb * LANES), lambda pb, i: (i, pb)),
                   pl.BlockSpec((ppb, None, LANES, LANES), lambda pb, i: (pb, i, 0, 0))],
        out_shape=[jax.ShapeDtypeStruct((t, n_pairs * LANES), F32), jax.ShapeDtypeStruct((n_pairs, nc, LANES, LANES), F32)],
        scratch_shapes=[pltpu.VMEM((ppb, LANES, LANES), F32)],
        compiler_params=_params(("arbitrary", "arbitrary")),
    )(*[s[0] for s in seq_in], *const_in)


def scan_bwd(name, chunk_fn, chunk, seq_in, const_in, states, dy, n_pairs, ppb):
    t = dy.shape[0]
    nc = t // chunk
    ns, ncst = len(seq_in), len(const_in)

    def kern(*refs):
        seq_refs, cst_refs = refs[:ns], refs[ns:ns + ncst]
        st_ref, dy_ref = refs[ns + ncst], refs[ns + ncst + 1]
        o = ns + ncst + 2
        dseq_refs, dcst_refs, dht = refs[o:o + ns], refs[o + ns:o + ns + ncst], refs[o + ns + ncst]
        pb, i = pl.program_id(0), pl.program_id(1)

        @pl.when(i == 0)
        def _():
            dht[...] = jnp.zeros_like(dht)

        ids = [pb * ppb + j for j in range(ppb)]
        lanes = [slice(j * LANES, (j + 1) * LANES) for j in range(ppb)]

        def fn(*flat):
            sv = [list(flat[j * ns:(j + 1) * ns]) for j in range(ppb)]
            outs = chunk_fn(sv, list(flat[ppb * ns:ppb * ns + ncst]), list(flat[ppb * ns + ncst:]), ids)
            return tuple(y for y, _ in outs), tuple(h for _, h in outs)

        flat_in = [v for j in range(ppb) for v in _pair_vals(seq_refs, seq_in, j)]
        flat_in += [r[...] for r in cst_refs] + [st_ref[j] for j in range(ppb)]
        _, vjp = jax.vjp(fn, *flat_in)
        grads = vjp((tuple(dy_ref[:, ln] for ln in lanes), tuple(dht[j] for j in range(ppb))))
        for j in range(ppb):
            for r, g in zip(dseq_refs, grads[j * ns:(j + 1) * ns]):
                r[:, lanes[j]] = g
            dht[j] = grads[ppb * ns + ncst + j]
        dcv = grads[ppb * ns:ppb * ns + ncst]
        if ncst:
            first = jnp.logical_and(pb == 0, i == 0)

            @pl.when(first)
            def _():
                for r, g in zip(dcst_refs, dcv):
                    r[...] = g

            @pl.when(jnp.logical_not(first))
            def _():
                for r, g in zip(dcst_refs, dcv):
                    r[...] += g

    rev = lambda i: nc - 1 - i
    wide = pl.BlockSpec((chunk, ppb * LANES), lambda pb, i: (rev(i), pb))
    in_specs = [_seq_spec(chunk, ppb, col, rev) for (_, col) in seq_in]
    in_specs += [pl.BlockSpec(a.shape, lambda pb, i: (0, 0)) for a in const_in]
    in_specs += [pl.BlockSpec((ppb, None, LANES, LANES), lambda pb, i: (pb, rev(i), 0, 0)), wide]
    out_specs = [wide for _ in seq_in]
    out_specs += [pl.BlockSpec(a.shape, lambda pb, i: (0, 0)) for a in const_in]
    out_shape = [jax.ShapeDtypeStruct((t, n_pairs * LANES), F32) for _ in seq_in]
    out_shape += [jax.ShapeDtypeStruct(a.shape, F32) for a in const_in]
    res = pl.pallas_call(
        kern, name=name, grid=(n_pairs // ppb, nc), in_specs=in_specs, out_specs=out_specs, out_shape=out_shape,
        scratch_shapes=[pltpu.VMEM((ppb, LANES, LANES), F32)],
        compiler_params=_params(("arbitrary", "arbitrary")),
    )(*[s[0] for s in seq_in], *const_in, states, dy)
    return list(res[:ns]), list(res[ns:])


def loss_head(x3, tgt, g, tr):
    rows, d = x3.shape

    def body(tv, fv):
        def f(x, gg):
            e = jnp.square(_rms(x, gg) - tv[1])
            return 0.5 * jnp.sum(jnp.mean(e, axis=-1, keepdims=True), axis=0, keepdims=True)
        l, vjp = jax.vjp(f, tv[0], fv[0])
        dx, dg = vjp(jnp.ones((1, 1), F32))
        return [dx], [dg, jnp.broadcast_to(l, (8, LANES))]
    (dx,), (dg, l) = row_call("loss_head", body, rows // tr, [(x3, tr, d, 0), (tgt, tr, d, 0)], [g],
                              [(rows, tr, d, F32)], [g.shape, (8, LANES)])
    return dx, dg, l


def _adam_math(w, g, m, v):
    m = ADAM_B1 * m + (1.0 - ADAM_B1) * g
    v = ADAM_B2 * v + (1.0 - ADAM_B2) * jnp.square(g)
    m_hat = m / (1.0 - ADAM_B1 ** ADAM_STEP)
    v_hat = v / (1.0 - ADAM_B2 ** ADAM_STEP)
    delta = -ADAM_LR * (m_hat / (jnp.sqrt(v_hat) + ADAM_EPS) + ADAM_WD * w)
    return delta, m, v


def adamw(name, w, m, v, g_parts):
    rows, cols = w.shape
    tr = _pick(rows, (256, 128, 64, 32, 16, 8))
    n_g = len(g_parts)

    def body(tv, fv):
        g = tv[3]
        for extra in tv[4:4 + n_g - 1]:
            g = g + extra
        delta, mn, vn = _adam_math(tv[0], g, tv[1], tv[2])
        return [g, delta, mn, vn], []
    tiled = [(a, tr, cols, 0) for a in (w, m, v, *g_parts)]
    outs, _ = row_call(name, body, rows // tr, tiled, [], [(rows, tr, cols, F32)] * 4, [])
    return outs


def sum_slots(name, r):
    _, rows, cols = r.shape
    tr = _pick(rows, (256, 128, 64, 32, 16, 8))

    def kern(r0, r1, r2, r3, o):
        o[...] = ((r0[...].astype(F32) + r1[...].astype(F32)) + r2[...].astype(F32)) + r3[...].astype(F32)

    in_specs = [pl.BlockSpec((None, tr, cols), functools.partial(lambda i, s: (s, i, 0), s=s)) for s in range(4)]
    return pl.pallas_call(
        kern, name=name, grid=(rows // tr,), in_specs=in_specs, out_specs=pl.BlockSpec((tr, cols), lambda i: (i, 0)),
        out_shape=jax.ShapeDtypeStruct((rows, cols), F32), compiler_params=_params(("arbitrary",)),
    )(r, r, r, r)


def _my_place():
    return lax.axis_index("x"), lax.axis_index("y"), lax.axis_index("c")


def chip_exchange(name, arrays, gather):
    nw = len(arrays)
    ANY = pl.BlockSpec(memory_space=pl.ANY)

    def body(*refs):
        ins, outs = refs[:nw], refs[nw:2 * nw]
        send, recv, loc = refs[2 * nw:]
        x, y, c = _my_place()
        q = 2 * x + y
        peers = [(1 - x, y), (x, 1 - y), (1 - x, 1 - y)]

        def src(w, dest_chip):
            return ins[w] if gather else ins[w].at[dest_chip]

        def remote(w, j):
            px, py = peers[j]
            return pltpu.make_async_remote_copy(
                src_ref=src(w, 2 * px + py), dst_ref=outs[w].at[q], send_sem=send.at[w, j], recv_sem=recv.at[w, j],
                device_id=(px, py, c), device_id_type=MESH_ID)

        def arrival(w, j):
            px, py = peers[j]
            return pltpu.make_async_remote_copy(
                src_ref=src(w, q), dst_ref=outs[w].at[2 * px + py], send_sem=send.at[w, j], recv_sem=recv.at[w, j],
                device_id=(px, py, c), device_id_type=MESH_ID)

        local = [pltpu.make_async_copy(src(w, q), outs[w].at[q], loc.at[w]) for w in range(nw)]
        sends = [[remote(w, j) for j in range(3)] for w in range(nw)]
        for w in range(nw):
            local[w].start()
            for j in range(3):
                sends[w][j].start()
        for w in range(nw):
            local[w].wait()
            for j in range(3):
                sends[w][j].wait_send()
                arrival(w, j).wait_recv()

    out_shape = [jax.ShapeDtypeStruct((4,) + (a.shape if gather else a.shape[1:]), a.dtype) for a in arrays]
    return pl.pallas_call(
        body, name=name, in_specs=[ANY] * nw, out_specs=[ANY] * nw, out_shape=out_shape,
        scratch_shapes=[pltpu.SemaphoreType.DMA((nw, 3)), pltpu.SemaphoreType.DMA((nw, 3)), pltpu.SemaphoreType.DMA((nw,))],
        compiler_params=pltpu.CompilerParams(has_side_effects=True),
    )(*arrays)


def gather_two_level(name, arrays):
    nw = len(arrays)
    ANY = pl.BlockSpec(memory_space=pl.ANY)

    def body(*refs):
        ins, outs = refs[:nw], refs[nw:2 * nw]
        send, recv, loc = refs[2 * nw:]
        x, y, c = _my_place()
        q = 2 * x + y
        me, sibling = (x, y, c), (x, y, 1 - c)
        peers = [(1 - x, y), (x, 1 - y), (1 - x, 1 - y)]
        chips = [2 * px + py for px, py in peers]

        def mine(w):
            hr = ins[w].shape[0] // 2
            return ins[w].at[pl.ds(c * hr, hr)]

        def copy(w, k, src, chip, half, to):
            return pltpu.make_async_remote_copy(
                src_ref=src, dst_ref=outs[w].at[chip, half], send_sem=send.at[w, k], recv_sem=recv.at[w, k],
                device_id=to, device_id_type=MESH_ID)

        local = [pltpu.make_async_copy(mine(w), outs[w].at[q, c], loc.at[w]) for w in range(nw)]
        first = [[copy(w, 0, mine(w), q, c, sibling)] + [copy(w, 1 + j, mine(w), q, c, (*peers[j], c)) for j in range(3)]
                 for w in range(nw)]
        for w in range(nw):
            local[w].start()
            for cp in first[w]:
                cp.start()
        passed = []
        for w in range(nw):
            for j in range(3):
                copy(w, 1 + j, mine(w), chips[j], c, me).wait_recv()
                fwd = copy(w, 4 + j, outs[w].at[chips[j], c], chips[j], c, sibling)
                fwd.start()
                passed.append(fwd)
        for w in range(nw):
            copy(w, 0, mine(w), q, 1 - c, me).wait_recv()
            for j in range(3):
                copy(w, 4 + j, mine(w), chips[j], 1 - c, me).wait_recv()
        for w in range(nw):
            local[w].wait()
            for cp in first[w]:
                cp.wait_send()
        for cp in passed:
            cp.wait_send()

    out_shape = [jax.ShapeDtypeStruct((4, 2, a.shape[0] // 2, a.shape[1]), a.dtype) for a in arrays]
    return pl.pallas_call(
        body, name=name, in_specs=[ANY] * nw, out_specs=[ANY] * nw, out_shape=out_shape,
        scratch_shapes=[pltpu.SemaphoreType.DMA((nw, 7)), pltpu.SemaphoreType.DMA((nw, 7)), pltpu.SemaphoreType.DMA((nw,))],
        compiler_params=pltpu.CompilerParams(has_side_effects=True),
    )(*arrays)


def core_swap(name, arrays):
    nw = len(arrays)
    ANY = pl.BlockSpec(memory_space=pl.ANY)

    def body(*refs):
        ins, outs = refs[:nw], refs[nw:2 * nw]
        send, recv = refs[2 * nw:]
        x, y, c = _my_place()
        copies = [pltpu.make_async_remote_copy(
            src_ref=ins[w], dst_ref=outs[w], send_sem=send.at[w], recv_sem=recv.at[w],
            device_id=(x, y, 1 - c), device_id_type=MESH_ID) for w in range(nw)]
        for cp in copies:
            cp.start()
        for cp in copies:
            cp.wait_send()
            cp.wait_recv()

    return pl.pallas_call(
        body, name=name, in_specs=[ANY] * nw, out_specs=[ANY] * nw,
        out_shape=[jax.ShapeDtypeStruct(a.shape, a.dtype) for a in arrays],
        scratch_shapes=[pltpu.SemaphoreType.DMA((nw,)), pltpu.SemaphoreType.DMA((nw,))],
        compiler_params=pltpu.CompilerParams(has_side_effects=True),
    )(*arrays)


def all_reduce_small(name, v):
    rows = v.shape[0]
    VM = pl.BlockSpec(memory_space=pltpu.VMEM)

    def body(v_ref, o_ref, buf, send, recv):
        x, y, c = _my_place()
        me = 4 * x + 2 * y + c

        def peer(kx):
            return (x ^ ((kx >> 2) & 1), y ^ ((kx >> 1) & 1), c ^ (kx & 1))

        def copy(kx, slot):
            return pltpu.make_async_remote_copy(
                src_ref=v_ref, dst_ref=buf.at[slot], send_sem=send.at[kx - 1], recv_sem=recv.at[kx - 1],
                device_id=peer(kx), device_id_type=MESH_ID)

        sends = [copy(kx, me) for kx in range(1, 8)]
        for cp in sends:
            cp.start()
        buf[me] = v_ref[...]
        for kx in range(1, 8):
            copy(kx, me ^ kx).wait_recv()
        for cp in sends:
            cp.wait_send()
        acc = buf[0]
        for d in range(1, 8):
            acc = acc + buf[d]
        o_ref[...] = acc

    return pl.pallas_call(
        body, name=name, in_specs=[VM], out_specs=VM, out_shape=jax.ShapeDtypeStruct(v.shape, F32),
        scratch_shapes=[pltpu.VMEM((8, rows, LANES), F32), pltpu.SemaphoreType.DMA((7,)), pltpu.SemaphoreType.DMA((7,))],
        compiler_params=pltpu.CompilerParams(has_side_effects=True, vmem_limit_bytes=VMEM_LIMIT),
    )(v)


def _pad_cols(a, n):
    return jnp.pad(a, ((0, 0), (0, n - a.shape[1])))


def _pad_rows(a, n):
    return jnp.pad(a, ((0, n - a.shape[0]), (0, 0)))


def _halo(u, tr):
    t, cdim = u.shape
    tails = u.reshape(t // tr, tr, cdim)[:, tr - HALO:, :]
    tails = jnp.concatenate([jnp.zeros((1, HALO, cdim), u.dtype), tails[:-1]], axis=0)
    return tails.reshape(-1, cdim)


def _unhalo(du, dhalo, tr):
    t, cdim = du.shape
    n = t // tr
    dh = dhalo.reshape(n, HALO, cdim)
    dh = jnp.concatenate([dh[1:], jnp.zeros((1, HALO, cdim), du.dtype)], axis=0)
    d3 = du.reshape(n, tr, cdim)
    d3 = jnp.concatenate([d3[:, :tr - HALO, :], d3[:, tr - HALO:, :] + dh], axis=1)
    return d3.reshape(t, cdim)


def _to_slots(g, axis):
    r, cdim = g.shape
    if axis == 0:
        return g.reshape(4, r // 4, cdim)
    return g.reshape(r, 4, cdim // 4).transpose(1, 0, 2)


def _from_slots(s, axis):
    if axis == 0:
        return s.reshape(s.shape[0] * s.shape[1], s.shape[2])
    return s.transpose(1, 0, 2).reshape(s.shape[1], 4 * s.shape[2])


BIG = ("w_in", "w_out", "xattn_wq", "xattn_wk", "xattn_wv", "xattn_wo", "ffn_w1", "ffn_w2")
BIG_AXIS = {"w_in": 1, "w_out": 0, "xattn_wq": 0, "xattn_wk": 0, "xattn_wv": 0, "xattn_wo": 0, "ffn_w1": 1, "ffn_w2": 0}
SMALL_SHARDED = ("ssd_conv_w", "rwkv_w2", "rwkv_a2", "rwkv_g2")
REDUCED = BIG + ("rwkv_w2", "rwkv_a2", "rwkv_g2")
REDUCE_AXIS = dict(BIG_AXIS, rwkv_w2=1, rwkv_a2=1, rwkv_g2=1)
WEIGHTS = ("norm_mix_g", "w_in", "ssd_conv_w", "ssd_conv_b", "ssd_dt_bias", "ssd_a_log", "ssd_d", "ssd_norm_g",
           "rwkv_mu", "rwkv_w0", "rwkv_w2", "rwkv_a0", "rwkv_a2", "rwkv_g2", "rwkv_k_k", "rwkv_k_a", "rwkv_r_k",
           "rwkv_ln_w", "rwkv_ln_b", "w_out", "norm_x_g", "norm_mem_g", "xattn_wq", "xattn_wk", "xattn_wv", "xattn_wo",
           "norm_ffn_g", "ffn_w1", "ffn_w2", "final_norm_g")


def _local_grads(x, mem, tgt, wt, full):
    t, d = x.shape
    w = d // 2
    nh = w // HEAD_DIM
    n_pairs = nh // 2
    ppg = n_pairs // SSD_GROUPS
    bc = SSD_GROUPS * SSD_STATE
    conv_dim = w + 2 * bc
    tr = ROW_TILE
    nt = t // tr
    dr = wt["rwkv_w2"].shape[0]
    ar = wt["rwkv_a2"].shape[0]
    gr = wt["rwkv_g2"].shape[0]

    w_in = full["w_in"]
    o = 0
    segs = {}
    for nm, width in (("z", w), ("xbc", conv_dim), ("dt", nh), ("rkv", 3 * w), ("pw", dr), ("pa", ar), ("pg", gr)):
        segs[nm] = (o, width)
        o += width
    padded = {"z": w, "xbc": conv_dim, "dt": LANES, "rkv": 3 * w, "pw": LANES, "pa": LANES, "pg": gr}
    order = ("z", "xbc", "dt", "rkv", "pw", "pa", "pg")
    w_perm = jnp.concatenate([_pad_cols(w_in[:, segs[nm][0]:segs[nm][0] + segs[nm][1]], padded[nm]) for nm in order], axis=1)
    offs = {}
    o = 0
    for nm in order:
        offs[nm] = o
        o += padded[nm]
    n_perm = o
    lora_w = 2 * LANES + gr

    def seg_cols(a, nm, width=None):
        return a[:, offs[nm]:offs[nm] + (padded[nm] if width is None else width)]

    mu = wt["rwkv_mu"]
    mo = 3 * w
    mu_rkv = mu[:, :mo]
    mu_lora = jnp.concatenate([_pad_cols(mu[:, mo:mo + dr], LANES), _pad_cols(mu[:, mo + dr:mo + dr + ar], LANES),
                               mu[:, mo + dr + ar:]], axis=1)
    w2p = _pad_rows(full["rwkv_w2"], LANES)
    a2p = _pad_rows(full["rwkv_a2"], LANES)
    g2 = full["rwkv_g2"]
    conv_w = full["ssd_conv_w"]
    cw = [conv_w[i:i + 1] for i in range(SSD_CONV)]
    dt_bias = _pad_cols(wt["ssd_dt_bias"], LANES)
    a_log = _pad_cols(wt["ssd_a_log"], LANES)
    d_skip = _pad_cols(wt["ssd_d"], LANES)
    r_k = wt["rwkv_r_k"].reshape(1, w)

    h1, h1t = norm_fwd("norm_mix", x, wt["norm_mix_g"], tr)
    u = matmul("in_proj", h1, w_perm)
    z, xbc, dtraw = seg_cols(u, "z"), seg_cols(u, "xbc"), seg_cols(u, "dt")
    urkv = seg_cols(u, "rkv")
    ulora = u[:, offs["pw"]:offs["pw"] + lora_w]

    halo_xbc = _halo(xbc, tr)
    ssd_pre_t = [(xbc, tr, conv_dim, 0), (halo_xbc, HALO, conv_dim, 0), (dtraw, tr, LANES, 0)]
    ssd_pre_f = cw + [wt["ssd_conv_b"], dt_bias]
    act, dt = fn_fwd("ssd_pre", _ssd_pre, nt, ssd_pre_t, ssd_pre_f, [(t, tr, conv_dim, F32), (t, tr, LANES, F32)])

    nb = w // LANES
    ssd_seq = [(act, None), (act, lambda p: nb + p // ppg), (act, lambda p: nb + SSD_GROUPS + p // ppg), (dt, lambda p: 0)]
    ssd_ppb = min(ppg, PAIRS_PER_STEP)
    rw_ppb = min(n_pairs, 2 * PAIRS_PER_STEP)

    def ssd_fn(sv, cv, hts, ids):
        return [_ssd_chunk(*s, cv[0], ht, p) for s, ht, p in zip(sv, hts, ids)]

    y_scan, ssd_states = scan_fwd("ssd_scan", ssd_fn, SSD_CHUNK, ssd_seq, [a_log], n_pairs, ssd_ppb)
    ssd_post_t = [(y_scan, tr, w, 0), (act, tr, w, 0), (z, tr, w, 0)]
    ssd_post_f = [d_skip, wt["ssd_norm_g"]]
    y_ssd, y_ssd_t = fn_fwd("ssd_post", _ssd_post, nt, ssd_post_t, ssd_post_f, [(t, tr, w, BF16)], (0,))

    halo_rkv, halo_lora = _halo(urkv, tr), _halo(ulora, tr)
    rw_pre_t = [(urkv, tr, 3 * w, 0), (ulora, tr, lora_w, 0), (halo_rkv, HALO, 3 * w, 0), (halo_lora, HALO, lora_w, 0)]
    rw_pre_f = [mu_rkv, mu_lora, wt["rwkv_w0"], wt["rwkv_a0"], wt["rwkv_k_k"], wt["rwkv_k_a"], w2p, a2p, g2]
    rw = fn_fwd("rwkv_pre", _rwkv_pre, nt, rw_pre_t, rw_pre_f, [(t, tr, w, F32)] * 7)
    r_, lw_, k2_, v_, nkk_, b_, gate_ = rw
    rw_seq = [(a, None) for a in (r_, lw_, k2_, v_, nkk_, b_)]

    def rw_fn(sv, cv, hts, ids):
        return _rwkv_chunks([(*s, ht) for s, ht in zip(sv, hts)])

    yr_scan, rw_states = scan_fwd("rwkv_scan", rw_fn, RWKV_CHUNK, rw_seq, [], n_pairs, rw_ppb)
    rw_post_t = [(a, tr, w, 0) for a in (yr_scan, r_, k2_, v_, gate_)]
    rw_post_f = [r_k, wt["rwkv_ln_w"], wt["rwkv_ln_b"]]
    y_rwkv, y_rwkv_t = fn_fwd("rwkv_post", _rwkv_post, nt, rw_post_t, rw_post_f, [(t, tr, w, BF16)], (0,))

    ymix = jnp.concatenate([y_ssd, y_rwkv], axis=1)
    ymix_t = jnp.concatenate([y_ssd_t, y_rwkv_t], axis=0)
    x1 = matmul("out_proj", ymix, full["w_out"], resid=x)

    h2, h2t = norm_fwd("norm_x", x1, wt["norm_x_g"], tr)
    mrows = mem.shape[0]
    mn, mnt = norm_fwd("norm_mem", mem, wt["norm_mem_g"], mrows)
    q = matmul("xattn_q", h2, full["xattn_wq"])
    kx = matmul("xattn_k", mn, full["xattn_wk"])
    vx = matmul("xattn_v", mn, full["xattn_wv"])
    ao, aot = fn_fwd("xattn_core", _attn, nt, [(q, tr, d, 0)], [kx, vx], [(t, tr, d, BF16)], (0,))
    x2 = matmul("xattn_o", ao, full["xattn_wo"], resid=x1)

    h3, h3t = norm_fwd("norm_ffn", x2, wt["norm_ffn_g"], tr)
    a1 = matmul("ffn_up", h3, full["ffn_w1"], out_dtype=BF16)
    dff = a1.shape[1]
    f1, f1t = fn_fwd("ffn_act", _relu2, nt, [(a1, tr, dff, 0)], [], [(t, tr, dff, BF16)], (0,))
    x3 = matmul("ffn_down", f1, full["ffn_w2"], resid=x2)

    dx3, g_final, loss_tile = loss_head(x3, tgt, wt["final_norm_g"].reshape(1, d), tr)

    grads = {"final_norm_g": g_final.reshape(d)}
    dx3b = dx3.astype(BF16)
    grads["ffn_w2"] = matmul("ffn_down_dw", f1t, dx3b)
    df1 = matmul("ffn_down_dx", dx3b, full["ffn_w2"], tb=True, out_dtype=BF16)
    (da1,), _ = fn_bwd("ffn_act_bwd", _relu2, nt, [(a1, tr, dff, 0)], [], [(df1, tr, dff, 0)], lambda c: [c[0].astype(F32)],
                       [(t, tr, dff, BF16)])
    grads["ffn_w1"] = matmul("ffn_up_dw", h3t, da1)
    dh3 = matmul("ffn_up_dx", da1, full["ffn_w1"], tb=True)
    dx2, grads["norm_ffn_g"] = norm_bwd("norm_ffn_bwd", x2, wt["norm_ffn_g"], dh3, dx3, tr)

    dx2b = dx2.astype(BF16)
    grads["xattn_wo"] = matmul("xattn_o_dw", aot, dx2b)
    dao = matmul("xattn_o_dx", dx2b, full["xattn_wo"], tb=True)
    (dq,), (dkx, dvx) = fn_bwd("xattn_core_bwd", _attn, nt, [(q, tr, d, 0)], [kx, vx], [(dao, tr, d, 0)], lambda c: c,
                               [(t, tr, d, BF16)])
    grads["xattn_wq"] = matmul("xattn_q_dw", h2t, dq)
    dh2 = matmul("xattn_q_dx", dq, full["xattn_wq"], tb=True)
    dkb, dvb = dkx.astype(BF16), dvx.astype(BF16)
    grads["xattn_wk"] = matmul("xattn_k_dw", mnt, dkb)
    grads["xattn_wv"] = matmul("xattn_v_dw", mnt, dvb)
    dmn = matmul("xattn_k_dx", dkb, full["xattn_wk"], tb=True)
    dmn = matmul("xattn_v_dx", dvb, full["xattn_wv"], tb=True, resid=dmn)
    _, grads["norm_mem_g"] = norm_bwd("norm_mem_bwd", mem, wt["norm_mem_g"], dmn, None, mrows)
    dx1, grads["norm_x_g"] = norm_bwd("norm_x_bwd", x1, wt["norm_x_g"], dh2, dx2, tr)

    dx1b = dx1.astype(BF16)
    grads["w_out"] = matmul("out_proj_dw", ymix_t, dx1b)
    dymix = matmul("out_proj_dx", dx1b, full["w_out"], tb=True)

    (dyr, dr1, dk1, dv1, dgate), (g_rk, grads["rwkv_ln_w"], grads["rwkv_ln_b"]) = fn_bwd(
        "rwkv_post_bwd", _rwkv_post, nt, rw_post_t, rw_post_f, [(dymix, tr, w, 1)], lambda c: c, [(t, tr, w, F32)] * 5)
    grads["rwkv_r_k"] = g_rk.reshape(wt["rwkv_r_k"].shape)
    (dr2, dlw, dk2, dv2, dnkk, db), _ = scan_bwd("rwkv_scan_bwd", rw_fn, RWKV_CHUNK, rw_seq, [], rw_states, dyr, n_pairs, rw_ppb)
    rw_ct = [(a, tr, w, 0) for a in (dr1, dr2, dlw, dk1, dk2, dv1, dv2, dnkk, db, dgate)]

    def rw_ct_fn(c):
        return (c[0] + c[1], c[2], c[3] + c[4], c[5] + c[6], c[7], c[8], c[9])

    (durkv, dulora, dhrkv, dhlora), rw_pg = fn_bwd(
        "rwkv_pre_bwd", _rwkv_pre, nt, rw_pre_t, rw_pre_f, rw_ct, rw_ct_fn,
        [(t, tr, 3 * w, F32), (t, tr, lora_w, F32), (nt * HALO, HALO, 3 * w, F32), (nt * HALO, HALO, lora_w, F32)])
    durkv = _unhalo(durkv, dhrkv, tr)
    dulora = _unhalo(dulora, dhlora, tr)
    g_mu_rkv, g_mu_lora, grads["rwkv_w0"], grads["rwkv_a0"], grads["rwkv_k_k"], grads["rwkv_k_a"], g_w2p, g_a2p, grads["rwkv_g2"] = rw_pg
    grads["rwkv_mu"] = jnp.concatenate([g_mu_rkv, g_mu_lora[:, :dr], g_mu_lora[:, LANES:LANES + ar], g_mu_lora[:, 2 * LANES:]], axis=1)
    grads["rwkv_w2"] = g_w2p[:dr]
    grads["rwkv_a2"] = g_a2p[:ar]

    (dys, dxs1, dz), (g_d, grads["ssd_norm_g"]) = fn_bwd(
        "ssd_post_bwd", _ssd_post, nt, ssd_post_t, ssd_post_f, [(dymix, tr, w, 0)], lambda c: c, [(t, tr, w, F32)] * 3)
    grads["ssd_d"] = g_d[:, :nh]
    (dxs2, dbp, dcp, ddtp), (g_alog,) = scan_bwd("ssd_scan_bwd", ssd_fn, SSD_CHUNK, ssd_seq, [a_log], ssd_states, dys, n_pairs, ssd_ppb)
    grads["ssd_a_log"] = g_alog[:, :nh]
    ssd_ct = [(dxs1, tr, w, 0), (dxs2, tr, w, 0), (dbp, tr, w, 0), (dcp, tr, w, 0), (ddtp, tr, w, 0)]

    def ssd_ct_fn(c):
        def group_sum(a):
            parts = []
            for gi in range(SSD_GROUPS):
                s = a[:, gi * ppg * LANES:(gi * ppg + 1) * LANES]
                for j in range(1, ppg):
                    s = s + a[:, (gi * ppg + j) * LANES:(gi * ppg + j + 1) * LANES]
                parts.append(s)
            return parts
        ddt = c[4][:, :LANES]
        for j in range(1, n_pairs):
            ddt = ddt + c[4][:, j * LANES:(j + 1) * LANES]
        return (jnp.concatenate([c[0] + c[1]] + group_sum(c[2]) + group_sum(c[3]), axis=1), ddt)

    (dxbc, dhxbc, ddtraw), ssd_pg = fn_bwd(
        "ssd_pre_bwd", _ssd_pre, nt, ssd_pre_t, ssd_pre_f, ssd_ct, ssd_ct_fn,
        [(t, tr, conv_dim, F32), (nt * HALO, HALO, conv_dim, F32), (t, tr, LANES, F32)])
    dxbc = _unhalo(dxbc, dhxbc, tr)
    grads["ssd_conv_w"] = jnp.concatenate(ssd_pg[:SSD_CONV], axis=0)
    grads["ssd_conv_b"] = ssd_pg[SSD_CONV]
    grads["ssd_dt_bias"] = ssd_pg[SSD_CONV + 1][:, :nh]

    du = jnp.concatenate([dz, dxbc, ddtraw, durkv, dulora], axis=1).astype(BF16)
    g_perm = matmul("in_proj_dw", h1t, du)
    grads["w_in"] = jnp.concatenate([seg_cols(g_perm, nm, segs[nm][1]) for nm in order], axis=1)
    dh1 = matmul("in_proj_dx", du, w_perm, tb=True)
    grad_x, grads["norm_mix_g"] = norm_bwd("norm_mix_bwd", x, wt["norm_mix_g"], dh1, dx1, tr)
    return loss_tile, grad_x, grads


def _pack(arrs):
    flat = jnp.concatenate([a.reshape(-1) for a in arrs])
    n = flat.shape[0]
    rows = -(-n // (8 * LANES)) * 8
    return jnp.pad(flat, (0, rows * LANES - n)).reshape(rows, LANES)


def _unpack(packed, shapes):
    flat = packed.reshape(-1)
    out, o = [], 0
    for s in shapes:
        n = math.prod(s)
        out.append(flat[o:o + n].reshape(s))
        o += n
    return out


def _as2d(a):
    return a.reshape(-1, a.shape[-1])


def _step(a):
    x, mem, tgt = a["x"][0], a["mem"][0], a["loss_target"][0]
    q = 2 * lax.axis_index("x") + lax.axis_index("y")

    shard2d = {n: _as2d(a[n][0]) for n in BIG}
    small_sh = {n: _as2d(a[n][0]) for n in SMALL_SHARDED}
    gathered = gather_two_level("gather_weights", [shard2d[n].astype(BF16) for n in BIG])
    full = {n: _from_slots(g.reshape(4, 2 * g.shape[2], g.shape[3]), BIG_AXIS[n]) for n, g in zip(BIG, gathered)}
    gathered = chip_exchange("gather_small", [small_sh[n] for n in SMALL_SHARDED], True)
    for n, g in zip(SMALL_SHARDED, gathered):
        full[n] = _from_slots(g, 1)

    wt = {n: (a[n] if a[n].ndim <= 2 else a[n][0]) for n in WEIGHTS if n not in BIG and n not in SMALL_SHARDED}
    for n in SMALL_SHARDED:
        wt[n] = small_sh[n]
    loss_tile, grad_x, grads = _local_grads(x, mem, tgt, wt, full)

    c = lax.axis_index("c")
    shards = dict(shard2d)
    shards.update({n: small_sh[n] for n in REDUCED if n not in BIG})
    kept, sent = [], []
    for n in REDUCED:
        s = _to_slots(grads[n], REDUCE_AXIS[n])
        s = s.reshape(4, 2, s.shape[1] // 2, s.shape[2])
        kept.append(lax.dynamic_index_in_dim(s, c, axis=1, keepdims=False))
        sent.append(lax.dynamic_index_in_dim(s, 1 - c, axis=1, keepdims=False).astype(BF16))
    got = core_swap("swap_halves", sent)
    chip_parts = []
    for n, k, g in zip(REDUCED, kept, got):
        _, hr, cols = k.shape
        tr = _pick(4 * hr, (256, 128, 64, 32, 16))
        (part,), _ = row_call("chip_sum_" + n, lambda tv, fv: ([tv[0] + tv[1].astype(F32)], []), 4 * hr // tr,
                              [(k.reshape(4 * hr, cols), tr, cols, 0), (g.reshape(4 * hr, cols), tr, cols, 0)], [],
                              [(4 * hr, tr, cols, BF16)], [])
        chip_parts.append(part.reshape(4, hr, cols))
    slots = chip_exchange("scatter_grads", chip_parts, False)
    halves = [sum_slots("sum_" + n, s) for n, s in zip(REDUCED, slots)]
    others = core_swap("swap_reduced", halves)

    out = {}
    for n, mine, other in zip(REDUCED, halves, others):
        gsum = jnp.concatenate([jnp.where(c == 0, mine, other), jnp.where(c == 0, other, mine)], axis=0)
        g, dlt, mn, vn = adamw("adamw_" + n, shards[n], _as2d(a["m_" + n][0]), _as2d(a["v_" + n][0]), [gsum])
        for key, val in (("grad_", g), ("delta_", dlt), ("new_m_", mn), ("new_v_", vn)):
            out[key + n] = val.reshape(a[n].shape)

    small = [n for n in WEIGHTS if n not in REDUCED]
    red = _unpack(all_reduce_small("all_reduce_small", _pack([grads[n] for n in small])), [grads[n].shape for n in small])
    g_loc = {}
    for n, g in zip(small, red):
        if n in SMALL_SHARDED:
            cols = g.shape[1] // 4
            g = lax.dynamic_slice_in_dim(g, q * cols, cols, axis=1)
        g_loc[n] = g.reshape(a[n].shape)
    res = adamw("adamw_small", *[_pack([src[n] for n in small]) for src in
                                 ({n: a[n] for n in small}, {n: a["m_" + n] for n in small}, {n: a["v_" + n] for n in small})],
                [_pack([g_loc[n] for n in small])])
    shapes = [a[n].shape for n in small]
    for key, packed in zip(("grad_", "delta_", "new_m_", "new_v_"), res):
        for n, val in zip(small, _unpack(packed, shapes)):
            out[key + n] = val

    loss = lax.psum(loss_tile[0, 0], ("x", "y", "c"))
    ordered = [loss, grad_x.reshape(a["x"].shape)]
    for key in ("grad_", "delta_", "new_m_", "new_v_"):
        ordered += [out[key + n] for n in WEIGHTS]
    return tuple(ordered)


def kernel(x, mem, norm_mix_g, w_in, ssd_conv_w, ssd_conv_b, ssd_dt_bias, ssd_a_log, ssd_d, ssd_norm_g, rwkv_mu, rwkv_w0, rwkv_w2, rwkv_a0, rwkv_a2, rwkv_g2, rwkv_k_k, rwkv_k_a, rwkv_r_k, rwkv_ln_w, rwkv_ln_b, w_out, norm_x_g, norm_mem_g, xattn_wq, xattn_wk, xattn_wv, xattn_wo, norm_ffn_g, ffn_w1, ffn_w2, final_norm_g, loss_target, m_norm_mix_g, m_w_in, m_ssd_conv_w, m_ssd_conv_b, m_ssd_dt_bias, m_ssd_a_log, m_ssd_d, m_ssd_norm_g, m_rwkv_mu, m_rwkv_w0, m_rwkv_w2, m_rwkv_a0, m_rwkv_a2, m_rwkv_g2, m_rwkv_k_k, m_rwkv_k_a, m_rwkv_r_k, m_rwkv_ln_w, m_rwkv_ln_b, m_w_out, m_norm_x_g, m_norm_mem_g, m_xattn_wq, m_xattn_wk, m_xattn_wv, m_xattn_wo, m_norm_ffn_g, m_ffn_w1, m_ffn_w2, m_final_norm_g, v_norm_mix_g, v_w_in, v_ssd_conv_w, v_ssd_conv_b, v_ssd_dt_bias, v_ssd_a_log, v_ssd_d, v_ssd_norm_g, v_rwkv_mu, v_rwkv_w0, v_rwkv_w2, v_rwkv_a0, v_rwkv_a2, v_rwkv_g2, v_rwkv_k_k, v_rwkv_k_a, v_rwkv_r_k, v_rwkv_ln_w, v_rwkv_ln_b, v_w_out, v_norm_x_g, v_norm_mem_g, v_xattn_wq, v_xattn_wk, v_xattn_wv, v_xattn_wo, v_norm_ffn_g, v_ffn_w1, v_ffn_w2, v_final_norm_g):
    return _step(dict(locals()))
```

```python
import functools
import math

import jax
import jax.numpy as jnp
from jax import lax
from jax.experimental import pallas as pl
from jax.experimental.pallas import tpu as pltpu
from jax.experimental.pallas import tpu_sc as plsc

F32 = jnp.float32
BF16 = jnp.bfloat16
HIGHEST = lax.Precision.HIGHEST
MESH_ID = pl.DeviceIdType.MESH

NORM_EPS = 1e-6
RWKV_LN_EPS = 64e-5
HEAD_DIM = 64
PAIR = 2 * HEAD_DIM
LANES = 128
SSD_STATE = 128
SSD_CHUNK = 128
SSD_GROUPS = 2
SSD_CONV = 4
RWKV_CHUNK = 64
HALO = 8
ROW_TILE = 128
PAIRS_PER_STEP = 4
XATTN_HEADS = 4
RWKV_PASSES = 1
VMEM_LIMIT = 56 * 1024 * 1024
MATMUL_VMEM = 40 * 1024 * 1024

ADAM_LR = 0.001
ADAM_B1 = 0.9
ADAM_B2 = 0.999
ADAM_EPS = 1e-08
ADAM_WD = 0.01
ADAM_STEP = 10


def _dims(ca, cb):
    return (((ca,), (cb,)), ((), ()))


def _split_bf16(a):
    hi = a.astype(BF16)
    lo = (a - hi.astype(F32)).astype(BF16)
    return hi, lo


def _mm_impl(a, b, ca, cb, passes):
    dn = _dims(ca, cb)
    if passes == 1:
        return lax.dot_general(a.astype(BF16), b.astype(BF16), dn, preferred_element_type=F32)
    ah, al = _split_bf16(a)
    bh, bl = _split_bf16(b)
    out = lax.dot_general(ah, bh, dn, preferred_element_type=F32)
    out = out + lax.dot_general(ah, bl, dn, preferred_element_type=F32)
    return out + lax.dot_general(al, bh, dn, preferred_element_type=F32)


@functools.partial(jax.custom_vjp, nondiff_argnums=(2, 3, 4))
def mm(a, b, ca, cb, passes):
    return _mm_impl(a, b, ca, cb, passes)


def _mm_fwd(a, b, ca, cb, passes):
    return _mm_impl(a, b, ca, cb, passes), (a, b)


def _mm_bwd(ca, cb, passes, res, g):
    a, b = res
    da = mm(g, b, 1, 1 - cb, passes) if ca == 1 else mm(b, g, 1 - cb, 1, passes)
    db = mm(a, g, 1 - ca, 0, passes) if cb == 0 else mm(g, a, 0, 1 - ca, passes)
    return da, db


mm.defvjp(_mm_fwd, _mm_bwd)


def _dot_exact(a, b):
    return lax.dot_general(a, b, _dims(1, 0), precision=HIGHEST, preferred_element_type=F32)


def _iota(shape, dim):
    return lax.broadcasted_iota(jnp.int32, shape, dim)


def _sigmoid(x):
    return 1.0 / (1.0 + jnp.exp(-x))


def _silu(x):
    return x * _sigmoid(x)


def _softplus(x):
    return jnp.maximum(x, 0.0) + jnp.log(1.0 + jnp.exp(-jnp.abs(x)))


def _rms(x, g):
    return x * lax.rsqrt(jnp.mean(x * x, axis=-1, keepdims=True) + NORM_EPS) * g


def _select_mm(x, sel):
    hi = x.astype(BF16)
    r1 = x - hi.astype(F32)
    mid = r1.astype(BF16)
    lo = (r1 - mid.astype(F32)).astype(BF16)
    dn = _dims(1, 0)
    out = lax.dot_general(hi, sel, dn, preferred_element_type=F32)
    out = out + lax.dot_general(mid, sel, dn, preferred_element_type=F32)
    return out + lax.dot_general(lo, sel, dn, preferred_element_type=F32)


def _head_sum_impl(x, n):
    sel = (_iota((n, LANES), 0) // HEAD_DIM == _iota((n, LANES), 1)).astype(BF16)
    return _select_mm(x, sel)


def _head_expand_impl(s, n):
    sel = (_iota((LANES, n), 1) // HEAD_DIM == _iota((LANES, n), 0)).astype(BF16)
    return _select_mm(s, sel)


@functools.partial(jax.custom_vjp, nondiff_argnums=(1,))
def _head_sum_n(x, n):
    return _head_sum_impl(x, n)


@functools.partial(jax.custom_vjp, nondiff_argnums=(1,))
def _head_expand(s, n):
    return _head_expand_impl(s, n)


_head_sum_n.defvjp(lambda x, n: (_head_sum_impl(x, n), None), lambda n, _, g: (_head_expand(g, n),))
_head_expand.defvjp(lambda s, n: (_head_expand_impl(s, n), None), lambda n, _, g: (_head_sum_n(g, n),))


def _head_sum(x):
    return _head_sum_n(x, x.shape[1])


def _row_vector_expand(v, n):
    v8 = jnp.broadcast_to(v, (8, LANES))
    return jnp.sum(_head_expand(v8, n), axis=0, keepdims=True) * 0.125


def _shift_rows_impl(u, halo, s):
    rolled = pltpu.roll(u, s, 0)
    top = jnp.where(_iota((HALO, 1), 0) < s, pltpu.roll(halo, s, 0), rolled[:HALO])
    return jnp.concatenate([top, rolled[HALO:]], axis=0)


@functools.partial(jax.custom_vjp, nondiff_argnums=(2,))
def _shift_rows(u, halo, s):
    return _shift_rows_impl(u, halo, s)


def _shift_rows_bwd(s, _, g):
    tr = g.shape[0]
    rolled = pltpu.roll(g, tr - s, 0)
    hrow = _iota((HALO, 1), 0)
    bottom = jnp.where(hrow < HALO - s, rolled[tr - HALO:], 0.0)
    dhalo = jnp.where(hrow >= HALO - s, pltpu.roll(g[:HALO], HALO - s, 0), 0.0)
    return jnp.concatenate([rolled[:tr - HALO], bottom], axis=0), dhalo


_shift_rows.defvjp(lambda u, halo, s: (_shift_rows_impl(u, halo, s), None), _shift_rows_bwd)


def _params(sem):
    return pltpu.CompilerParams(dimension_semantics=sem, vmem_limit_bytes=VMEM_LIMIT)


def row_call(name, body, n_tiles, tiled, full, out_tiled, out_acc, transposed=()):
    nt, nf, na = len(tiled), len(full), len(out_acc)
    n_plain = len(out_tiled)
    no = n_plain + len(transposed)

    def kern(*refs):
        tv = [r[...] for r in refs[:nt]]
        fv = [r[...] for r in refs[nt:nt + nf]]
        outs, accs = body(tv, fv)
        for r, v in zip(refs[nt + nf:nt + nf + n_plain], outs):
            r[...] = v.astype(r.dtype)
        for r, idx in zip(refs[nt + nf + n_plain:nt + nf + no], transposed):
            r[...] = outs[idx].astype(F32).T.astype(r.dtype)
        if na:
            a_refs = refs[nt + nf + no:]
            first = pl.program_id(0) == 0

            @pl.when(first)
            def _():
                for r, v in zip(a_refs, accs):
                    r[...] = v

            @pl.when(jnp.logical_not(first))
            def _():
                for r, v in zip(a_refs, accs):
                    r[...] += v

    in_specs = [pl.BlockSpec((rt, w), functools.partial(lambda i, cb: (i, cb), cb=cb)) for (_, rt, w, cb) in tiled]
    in_specs += [pl.BlockSpec(a.shape, lambda i: (0, 0)) for a in full]
    out_specs = [pl.BlockSpec((rt, w), lambda i: (i, 0)) for (_, rt, w, _) in out_tiled]
    out_specs += [pl.BlockSpec((out_tiled[idx][2], out_tiled[idx][1]), lambda i: (0, i)) for idx in transposed]
    out_specs += [pl.BlockSpec(s, lambda i: (0, 0)) for s in out_acc]
    out_shape = [jax.ShapeDtypeStruct((rows, w), dt) for (rows, _, w, dt) in out_tiled]
    out_shape += [jax.ShapeDtypeStruct((out_tiled[idx][2], out_tiled[idx][0]), BF16) for idx in transposed]
    out_shape += [jax.ShapeDtypeStruct(s, F32) for s in out_acc]
    res = pl.pallas_call(
        kern, name=name, grid=(n_tiles,), in_specs=in_specs, out_specs=out_specs, out_shape=out_shape,
        compiler_params=_params(("arbitrary",)),
    )(*[t[0] for t in tiled], *full)
    return list(res[:no]), list(res[no:])


def _pick(dim, cands):
    for c in cands:
        if dim % c == 0:
            return c
    return dim


def matmul(name, a, b, tb=False, resid=None, out_dtype=F32):
    m, k = a.shape
    n = b.shape[0] if tb else b.shape[1]
    has_resid = resid is not None
    out_bytes = jnp.dtype(out_dtype).itemsize
    sizes = (2048, 1024, 896, 768, 512, 384, 256, 128)
    tm = _pick(m, sizes[1:])
    tn = _pick(n, sizes[1:])

    def vmem_bytes(tk):
        return 2 * 2 * tk * (tm + tn) + tm * tn * (2 * out_bytes + 4 + (8 if has_resid else 0))

    tk = next((c for c in sizes if k % c == 0 and vmem_bytes(c) <= MATMUL_VMEM), LANES)
    nk = k // tk

    def kern(*refs):
        a_ref, b_ref = refs[0], refs[1]
        o_ref, acc = refs[-2], refs[-1]
        kk = pl.program_id(2)
        part = lax.dot_general(a_ref[...], b_ref[...], _dims(1, 1 if tb else 0), preferred_element_type=F32)

        def finish(out):
            if has_resid:
                out = out + refs[2][...]
            o_ref[...] = out.astype(o_ref.dtype)

        if nk == 1:
            finish(part)
            return

        @pl.when(kk == 0)
        def _():
            acc[...] = part

        @pl.when(jnp.logical_and(kk > 0, kk < nk - 1))
        def _():
            acc[...] += part

        @pl.when(kk == nk - 1)
        def _():
            finish(acc[...] + part)

    in_specs = [pl.BlockSpec((tm, tk), lambda i, j, kk: (i, kk))]
    if tb:
        in_specs.append(pl.BlockSpec((tn, tk), lambda i, j, kk: (j, kk)))
    else:
        in_specs.append(pl.BlockSpec((tk, tn), lambda i, j, kk: (kk, j)))
    args = [a, b]
    if has_resid:
        in_specs.append(pl.BlockSpec((tm, tn), lambda i, j, kk: (i, j)))
        args.append(resid)
    return pl.pallas_call(
        kern, name=name, grid=(m // tm, n // tn, nk), in_specs=in_specs,
        out_specs=pl.BlockSpec((tm, tn), lambda i, j, kk: (i, j)),
        out_shape=jax.ShapeDtypeStruct((m, n), out_dtype),
        scratch_shapes=[pltpu.VMEM((tm, tn), F32)],
        compiler_params=_params(("parallel", "parallel", "arbitrary")),
    )(*args)


def norm_fwd(name, x, g, tr):
    def body(tv, fv):
        return [_rms(tv[0], fv[0])], []
    rows, d = x.shape
    (h, ht), _ = row_call(name, body, rows // tr, [(x, tr, d, 0)], [g], [(rows, tr, d, BF16)], [], transposed=(0,))
    return h, ht


def norm_bwd(name, x, g, dh, extra, tr):
    def body(tv, fv):
        _, vjp = jax.vjp(_rms, tv[0], fv[0])
        dx, dg = vjp(tv[1])
        if extra is not None:
            dx = dx + tv[2]
        return [dx], [dg]
    rows, d = x.shape
    tiled = [(x, tr, d, 0), (dh, tr, d, 0)] + ([(extra, tr, d, 0)] if extra is not None else [])
    (dx,), (dg,) = row_call(name, body, rows // tr, tiled, [g], [(rows, tr, d, F32)], [g.shape])
    return dx, dg


def _ssd_pre(xbc, halo, dtraw, w0, w1, w2, w3, cb, dtb):
    y = w3 * xbc + w2 * _shift_rows(xbc, halo, 1) + w1 * _shift_rows(xbc, halo, 2) + w0 * _shift_rows(xbc, halo, 3) + cb
    return _silu(y), _softplus(dtraw + dtb)


def _ssd_post(ys, xs, z, dskip, ng):
    w = ys.shape[1]
    y = (ys + xs * _row_vector_expand(dskip, w)) * _silu(z)
    gw = w // SSD_GROUPS
    parts = []
    for gi in range(SSD_GROUPS):
        yg = y[:, gi * gw:(gi + 1) * gw]
        parts.append(yg * lax.rsqrt(jnp.mean(yg * yg, axis=-1, keepdims=True) + NORM_EPS))
    return jnp.concatenate(parts, axis=1) * ng


def _rwkv_pre(urkv, ulora, hrkv, hlora, mu_rkv, mu_lora, w0, a0, kkw, kaw, w2p, a2p, g2):
    w = w0.shape[1]
    urkv = urkv + (_shift_rows(urkv, hrkv, 1) - urkv) * mu_rkv
    ulora = ulora + (_shift_rows(ulora, hlora, 1) - ulora) * mu_lora
    r, k, v = urkv[:, :w], urkv[:, w:2 * w], urkv[:, 2 * w:]
    pw, pa, pg = ulora[:, :LANES], ulora[:, LANES:2 * LANES], ulora[:, 2 * LANES:]
    w_log = -_softplus(-(w0 + mm(jnp.tanh(pw), w2p, 1, 0, 1))) - 0.5
    lw = -jnp.exp(w_log)
    iclr = _sigmoid(a0 + mm(pa, a2p, 1, 0, 1))
    gate = mm(_sigmoid(pg), g2, 1, 0, 1)
    kk = k * kkw
    kk = kk / jnp.maximum(jnp.sqrt(_head_expand(_head_sum(kk * kk), w)), 1e-12)
    k2 = k * (1.0 + (iclr - 1.0) * kaw)
    return r, lw, k2, v, -kk, kk * iclr, gate


def _rwkv_post(ys, r, k2, v, gate, rk, lnw, lnb):
    w = ys.shape[1]
    inv = 1.0 / HEAD_DIM
    mean = _head_expand(_head_sum(ys), w) * inv
    d = ys - mean
    var = _head_expand(_head_sum(d * d), w) * inv
    yn = d * lax.rsqrt(var + RWKV_LN_EPS) * lnw + lnb
    bonus = _head_expand(_head_sum(r * k2 * rk), w) * v
    return (yn + bonus) * gate


def _attn(q, k, v):
    d = q.shape[1]
    hd = d // XATTN_HEADS
    outs = []
    for h in range(XATTN_HEADS):
        sl = slice(h * hd, (h + 1) * hd)
        s = mm(q[:, sl], k[:, sl], 1, 1, 1) * (hd ** -0.5)
        s = s - jnp.max(s, axis=-1, keepdims=True)
        p = jnp.exp(s)
        p = p / jnp.sum(p, axis=-1, keepdims=True)
        outs.append(mm(p, v[:, sl], 1, 0, 1))
    return jnp.concatenate(outs, axis=1)


def _relu2(a):
    return jnp.square(jnp.maximum(a.astype(F32), 0.0))


def fn_fwd(name, fn, n_tiles, tiled, full, out_tiled, transposed=()):
    def body(tv, fv):
        outs = fn(*tv, *fv)
        return (list(outs) if isinstance(outs, (tuple, list)) else [outs]), []
    outs, _ = row_call(name, body, n_tiles, tiled, full, out_tiled, [], transposed)
    return outs


def fn_bwd(name, fn, n_tiles, tiled, full, cts, ct_fn, out_tiled):
    nt = len(tiled)

    def body(tv, fv):
        outs, vjp = jax.vjp(fn, *tv[:nt], *fv)
        ct = ct_fn(tv[nt:])
        grads = vjp(tuple(ct) if isinstance(outs, (tuple, list)) else ct[0])
        return list(grads[:nt]), list(grads[nt:])
    return row_call(name, body, n_tiles, tiled + cts, full, out_tiled, [f.shape for f in full])


def _ssd_chunk(xs, bm, cm, dt_all, a_log, ht, p):
    q = xs.shape[0]
    lane = _iota((1, LANES), 1)
    row = _iota((q, 1), 0)
    tril = _iota((q, q), 0) >= _iota((q, q), 1)
    half = lane < HEAD_DIM
    da = dt_all * (-jnp.exp(a_log))
    cs = _dot_exact(tril.astype(F32), da)

    def col(mat, h):
        return jnp.sum(jnp.where(lane == h, mat, 0.0), axis=1, keepdims=True)

    cs0, cs1 = col(cs, 2 * p), col(cs, 2 * p + 1)
    xdt = xs * jnp.where(half, col(dt_all, 2 * p), col(dt_all, 2 * p + 1))
    csx = jnp.where(half, cs0, cs1)
    last = jnp.sum(jnp.where(row == q - 1, csx, 0.0), axis=0, keepdims=True)
    cb = mm(cm, bm, 1, 1, 1)
    y = mm(cm, ht, 1, 0, 1) * jnp.exp(csx)
    for csh, hm in ((cs0, half), (cs1, jnp.logical_not(half))):
        csl = jnp.broadcast_to(csh, (q, q))
        seg = csl - csl.T
        lmat = jnp.where(tril, jnp.exp(jnp.where(tril, seg, 0.0)), 0.0)
        y = y + jnp.where(hm, mm(cb * lmat, xdt, 1, 0, 1), 0.0)
    st = mm(bm, xdt * jnp.exp(last - csx), 0, 0, 1)
    return y, ht * jnp.exp(last) + st


def _rwkv_chunks(pairs):
    c = pairs[0][0].shape[0]
    ps = RWKV_PASSES
    lane = _iota((1, LANES), 1)
    row = _iota((c, 1), 0)
    ri, ci = _iota((c, c), 0), _iota((c, c), 1)
    tril_i, tril_s = ri >= ci, ri > ci
    eye = (ri == ci).astype(F32)
    half = lane < HEAD_DIM
    halves = (half, jnp.logical_not(half))
    bd = (_iota((LANES, LANES), 0) < HEAD_DIM) == (_iota((LANES, LANES), 1) < HEAD_DIM)
    tri = tril_i.astype(F32)
    n = len(pairs)
    heads = [(j, hm) for j in range(n) for hm in halves]

    cum = [_dot_exact(tri, p[1]) for p in pairs]
    at = [p[4] * jnp.exp(cm - p[1]) for p, cm in zip(pairs, cum)]
    en = [jnp.exp(-cm) for cm in cum]
    bt = [p[5] * e for p, e in zip(pairs, en)]
    kt = [p[2] * e for p, e in zip(pairs, en)]
    rt = [p[0] * jnp.exp(cm) for p, cm in zip(pairs, cum)]
    ah = [mm(at[j], pairs[j][6], 1, 1, ps) for j in range(n)]
    y = [mm(rt[j], pairs[j][6], 1, 1, ps) for j in range(n)]
    atm = [jnp.where(hm, at[j], 0.0) for j, hm in heads]
    rtm = [jnp.where(hm, rt[j], 0.0) for j, hm in heads]
    aab = [jnp.where(tril_s, mm(atm[i], bt[j], 1, 1, ps), 0.0) for i, (j, _) in enumerate(heads)]
    aak = [jnp.where(tril_s, mm(atm[i], kt[j], 1, 1, ps), 0.0) for i, (j, _) in enumerate(heads)]
    arb = [jnp.where(tril_i, mm(rtm[i], bt[j], 1, 1, ps), 0.0) for i, (j, _) in enumerate(heads)]
    ark = [jnp.where(tril_i, mm(rtm[i], kt[j], 1, 1, ps), 0.0) for i, (j, _) in enumerate(heads)]
    rhs = [ah[j] + mm(aak[i], pairs[j][3], 1, 0, ps) for i, (j, _) in enumerate(heads)]
    yv = [mm(ark[i], pairs[j][3], 1, 0, ps) for i, (j, _) in enumerate(heads)]
    tm = [eye + a_ for a_ in aab]
    pm = aab
    for _ in range(int(math.log2(c)) - 1):
        pm = [mm(p_, p_, 1, 0, ps) for p_ in pm]
        tm = [t_ + mm(t_, p_, 1, 0, ps) for t_, p_ in zip(tm, pm)]
    uh = [mm(tm[i], rhs[i], 1, 0, ps) for i in range(len(heads))]
    u = [jnp.where(half, uh[2 * j], uh[2 * j + 1]) for j in range(n)]
    yu = [mm(arb[i], u[j], 1, 0, ps) for i, (j, _) in enumerate(heads)]
    out = []
    for j in range(n):
        yj = y[j] + jnp.where(half, yu[2 * j] + yv[2 * j], yu[2 * j + 1] + yv[2 * j + 1])
        plast = jnp.sum(jnp.where(row == c - 1, cum[j], 0.0), axis=0, keepdims=True)
        upd = pairs[j][6] + mm(u[j], bt[j], 0, 0, ps) + mm(pairs[j][3], kt[j], 0, 0, ps)
        out.append((yj, jnp.where(bd, upd * jnp.exp(plast), 0.0)))
    return out


def _seq_spec(chunk, ppb, col, row_of):
    if col is None:
        return pl.BlockSpec((chunk, ppb * LANES), lambda pb, i: (row_of(i), pb))
    return pl.BlockSpec((chunk, LANES), lambda pb, i: (row_of(i), col(pb * ppb)))


def _pair_vals(refs, seq_in, j):
    return [r[...] if col is not None else r[:, j * LANES:(j + 1) * LANES] for r, (_, col) in zip(refs, seq_in)]


def scan_fwd(name, chunk_fn, chunk, seq_in, const_in, n_pairs, ppb):
    t = seq_in[0][0].shape[0]
    nc = t // chunk
    ns, ncst = len(seq_in), len(const_in)

    def kern(*refs):
        y_ref, st_ref, ht = refs[ns + ncst], refs[ns + ncst + 1], refs[ns + ncst + 2]

        @pl.when(pl.program_id(1) == 0)
        def _():
            ht[...] = jnp.zeros_like(ht)

        cv = [r[...] for r in refs[ns:ns + ncst]]
        h0 = [ht[j] for j in range(ppb)]
        for j in range(ppb):
            st_ref[j] = h0[j]
        sv = [_pair_vals(refs[:ns], seq_in, j) for j in range(ppb)]
        outs = chunk_fn(sv, cv, h0, [pl.program_id(0) * ppb + j for j in range(ppb)])
        for j, (y, hn) in enumerate(outs):
            y_ref[:, j * LANES:(j + 1) * LANES] = y
            ht[j] = hn

    in_specs = [_seq_spec(chunk, ppb, col, lambda i: i) for (_, col) in seq_in]
    in_specs += [pl.BlockSpec(a.shape, lambda pb, i: (0, 0)) for a in const_in]
    return pl.pallas_call(
        kern, name=name, grid=(n_pairs // ppb, nc), in_specs=in_specs,
        out_specs=[pl.BlockSpec((chunk, ppb * LANES), lambda pb, i: (i, pb)),
                   pl.BlockSpec((ppb, None, LANES, LANES), lambda pb, i: (pb, i, 0, 0))],
        out_shape=[jax.ShapeDtypeStruct((t, n_pairs * LANES), F32), jax.ShapeDtypeStruct((n_pairs, nc, LANES, LANES), F32)],
        scratch_shapes=[pltpu.VMEM((ppb, LANES, LANES), F32)],
        compiler_params=_params(("arbitrary", "arbitrary")),
    )(*[s[0] for s in seq_in], *const_in)


def scan_bwd(name, chunk_fn, chunk, seq_in, const_in, states, dy, n_pairs, ppb):
    t = dy.shape[0]
    nc = t // chunk
    ns, ncst = len(seq_in), len(const_in)

    def kern(*refs):
        seq_refs, cst_refs = refs[:ns], refs[ns:ns + ncst]
        st_ref, dy_ref = refs[ns + ncst], refs[ns + ncst + 1]
        o = ns + ncst + 2
        dseq_refs, dcst_refs, dht = refs[o:o + ns], refs[o + ns:o + ns + ncst], refs[o + ns + ncst]
        pb, i = pl.program_id(0), pl.program_id(1)

        @pl.when(i == 0)
        def _():
            dht[...] = jnp.zeros_like(dht)

        ids = [pb * ppb + j for j in range(ppb)]
        lanes = [slice(j * LANES, (j + 1) * LANES) for j in range(ppb)]

        def fn(*flat):
            sv = [list(flat[j * ns:(j + 1) * ns]) for j in range(ppb)]
            outs = chunk_fn(sv, list(flat[ppb * ns:ppb * ns + ncst]), list(flat[ppb * ns + ncst:]), ids)
            return tuple(y for y, _ in outs), tuple(h for _, h in outs)

        flat_in = [v for j in range(ppb) for v in _pair_vals(seq_refs, seq_in, j)]
        flat_in += [r[...] for r in cst_refs] + [st_ref[j] for j in range(ppb)]
        _, vjp = jax.vjp(fn, *flat_in)
        grads = vjp((tuple(dy_ref[:, ln] for ln in lanes), tuple(dht[j] for j in range(ppb))))
        for j in range(ppb):
            for r, g in zip(dseq_refs, grads[j * ns:(j + 1) * ns]):
                r[:, lanes[j]] = g
            dht[j] = grads[ppb * ns + ncst + j]
        dcv = grads[ppb * ns:ppb * ns + ncst]
        if ncst:
            first = jnp.logical_and(pb == 0, i == 0)

            @pl.when(first)
            def _():
                for r, g in zip(dcst_refs, dcv):
                    r[...] = g

            @pl.when(jnp.logical_not(first))
            def _():
                for r, g in zip(dcst_refs, dcv):
                    r[...] += g

    rev = lambda i: nc - 1 - i
    wide = pl.BlockSpec((chunk, ppb * LANES), lambda pb, i: (rev(i), pb))
    in_specs = [_seq_spec(chunk, ppb, col, rev) for (_, col) in seq_in]
    in_specs += [pl.BlockSpec(a.shape, lambda pb, i: (0, 0)) for a in const_in]
    in_specs += [pl.BlockSpec((ppb, None, LANES, LANES), lambda pb, i: (pb, rev(i), 0, 0)), wide]
    out_specs = [wide for _ in seq_in]
    out_specs += [pl.BlockSpec(a.shape, lambda pb, i: (0, 0)) for a in const_in]
    out_shape = [jax.ShapeDtypeStruct((t, n_pairs * LANES), F32) for _ in seq_in]
    out_shape += [jax.ShapeDtypeStruct(a.shape, F32) for a in const_in]
    res = pl.pallas_call(
        kern, name=name, grid=(n_pairs // ppb, nc), in_specs=in_specs, out_specs=out_specs, out_shape=out_shape,
        scratch_shapes=[pltpu.VMEM((ppb, LANES, LANES), F32)],
        compiler_params=_params(("arbitrary", "arbitrary")),
    )(*[s[0] for s in seq_in], *const_in, states, dy)
    return list(res[:ns]), list(res[ns:])


def loss_head(x3, tgt, g, tr):
    rows, d = x3.shape

    def body(tv, fv):
        def f(x, gg):
            e = jnp.square(_rms(x, gg) - tv[1])
            return 0.5 * jnp.sum(jnp.mean(e, axis=-1, keepdims=True), axis=0, keepdims=True)
        l, vjp = jax.vjp(f, tv[0], fv[0])
        dx, dg = vjp(jnp.ones((1, 1), F32))
        return [dx], [dg, jnp.broadcast_to(l, (8, LANES))]
    (dx,), (dg, l) = row_call("loss_head", body, rows // tr, [(x3, tr, d, 0), (tgt, tr, d, 0)], [g],
                              [(rows, tr, d, F32)], [g.shape, (8, LANES)])
    return dx, dg, l


def _adam_math(w, g, m, v):
    m = ADAM_B1 * m + (1.0 - ADAM_B1) * g
    v = ADAM_B2 * v + (1.0 - ADAM_B2) * jnp.square(g)
    m_hat = m / (1.0 - ADAM_B1 ** ADAM_STEP)
    v_hat = v / (1.0 - ADAM_B2 ** ADAM_STEP)
    delta = -ADAM_LR * (m_hat / (jnp.sqrt(v_hat) + ADAM_EPS) + ADAM_WD * w)
    return delta, m, v


def adamw(name, w, m, v, g_parts):
    rows, cols = w.shape
    tr = _pick(rows, (256, 128, 64, 32, 16, 8))
    n_g = len(g_parts)

    def body(tv, fv):
        g = tv[3]
        for extra in tv[4:4 + n_g - 1]:
            g = g + extra
        delta, mn, vn = _adam_math(tv[0], g, tv[1], tv[2])
        return [g, delta, mn, vn], []
    tiled = [(a, tr, cols, 0) for a in (w, m, v, *g_parts)]
    outs, _ = row_call(name, body, rows // tr, tiled, [], [(rows, tr, cols, F32)] * 4, [])
    return outs


def sum_slots(name, r):
    _, rows, cols = r.shape
    tr = _pick(rows, (256, 128, 64, 32, 16, 8))

    def kern(r0, r1, r2, r3, o):
        o[...] = ((r0[...].astype(F32) + r1[...].astype(F32)) + r2[...].astype(F32)) + r3[...].astype(F32)

    in_specs = [pl.BlockSpec((None, tr, cols), functools.partial(lambda i, s: (s, i, 0), s=s)) for s in range(4)]
    return pl.pallas_call(
        kern, name=name, grid=(rows // tr,), in_specs=in_specs, out_specs=pl.BlockSpec((tr, cols), lambda i: (i, 0)),
        out_shape=jax.ShapeDtypeStruct((rows, cols), F32), compiler_params=_params(("arbitrary",)),
    )(r, r, r, r)


def _my_place():
    return lax.axis_index("x"), lax.axis_index("y"), lax.axis_index("c")


def chip_exchange(name, arrays, gather):
    nw = len(arrays)
    ANY = pl.BlockSpec(memory_space=pl.ANY)

    def body(*refs):
        ins, outs = refs[:nw], refs[nw:2 * nw]
        send, recv, loc = refs[2 * nw:]
        x, y, c = _my_place()
        q = 2 * x + y
        peers = [(1 - x, y), (x, 1 - y), (1 - x, 1 - y)]

        def src(w, dest_chip):
            return ins[w] if gather else ins[w].at[dest_chip]

        def remote(w, j):
            px, py = peers[j]
            return pltpu.make_async_remote_copy(
                src_ref=src(w, 2 * px + py), dst_ref=outs[w].at[q], send_sem=send.at[w, j], recv_sem=recv.at[w, j],
                device_id=(px, py, c), device_id_type=MESH_ID)

        def arrival(w, j):
            px, py = peers[j]
            return pltpu.make_async_remote_copy(
                src_ref=src(w, q), dst_ref=outs[w].at[2 * px + py], send_sem=send.at[w, j], recv_sem=recv.at[w, j],
                device_id=(px, py, c), device_id_type=MESH_ID)

        local = [pltpu.make_async_copy(src(w, q), outs[w].at[q], loc.at[w]) for w in range(nw)]
        sends = [[remote(w, j) for j in range(3)] for w in range(nw)]
        for w in range(nw):
            local[w].start()
            for j in range(3):
                sends[w][j].start()
        for w in range(nw):
            local[w].wait()
            for j in range(3):
                sends[w][j].wait_send()
                arrival(w, j).wait_recv()

    out_shape = [jax.ShapeDtypeStruct((4,) + (a.shape if gather else a.shape[1:]), a.dtype) for a in arrays]
    return pl.pallas_call(
        body, name=name, in_specs=[ANY] * nw, out_specs=[ANY] * nw, out_shape=out_shape,
        scratch_shapes=[pltpu.SemaphoreType.DMA((nw, 3)), pltpu.SemaphoreType.DMA((nw, 3)), pltpu.SemaphoreType.DMA((nw,))],
        compiler_params=pltpu.CompilerParams(has_side_effects=True),
    )(*arrays)


def gather_two_level(name, arrays, collective_id):
    nw = len(arrays)

    def body(*refs):
        ins, outs = refs[:nw], refs[nw:2 * nw]
        send, recv = refs[2 * nw:]
        x, y, c = _my_place()
        q = 2 * x + y
        me, sibling = (x, y, c), (x, y, 1 - c)
        peers = [(1 - x, y), (x, 1 - y), (1 - x, 1 - y)]
        chips = [2 * px + py for px, py in peers]
        barrier = pltpu.get_barrier_semaphore()
        for dev in [sibling] + [(*p, c) for p in peers]:
            pl.semaphore_signal(barrier, inc=1, device_id=dev, device_id_type=MESH_ID)
        pl.semaphore_wait(barrier, 4)

        def mine(w):
            hr = ins[w].shape[0] // 2
            return ins[w].at[pl.ds(c * hr, hr)]

        def copy(w, k, src, chip, half, to):
            return pltpu.make_async_remote_copy(
                src_ref=src, dst_ref=outs[w].at[chip, half], send_sem=send.at[w, k], recv_sem=recv.at[w, k],
                device_id=to, device_id_type=MESH_ID)

        first = [[copy(w, 0, mine(w), q, c, sibling)] + [copy(w, 1 + j, mine(w), q, c, (*peers[j], c)) for j in range(3)]
                 for w in range(nw)]
        for w in range(nw):
            for cp in first[w]:
                cp.start()
        passed = []
        for w in range(nw):
            for j in range(3):
                copy(w, 1 + j, mine(w), chips[j], c, me).wait_recv()
                fwd = copy(w, 4 + j, outs[w].at[chips[j], c], chips[j], c, sibling)
                fwd.start()
                passed.append(fwd)
        for w in range(nw):
            copy(w, 0, mine(w), q, 1 - c, me).wait_recv()
            for j in range(3):
                copy(w, 4 + j, mine(w), chips[j], 1 - c, me).wait_recv()
        for w in range(nw):
            for cp in first[w]:
                cp.wait_send()
        for cp in passed:
            cp.wait_send()

    out_type = [jax.ShapeDtypeStruct((4, 2, a.shape[0] // 2, a.shape[1]), a.dtype) for a in arrays]
    return pl.kernel(
        body, out_type=out_type, mesh=plsc.ScalarSubcoreMesh(axis_name="sequencer", num_cores=1), name=name,
        scratch_types=[pltpu.SemaphoreType.DMA((nw, 7)), pltpu.SemaphoreType.DMA((nw, 7))],
        compiler_params=pltpu.CompilerParams(collective_id=collective_id),
    )(*arrays)


def core_swap(name, arrays):
    nw = len(arrays)
    ANY = pl.BlockSpec(memory_space=pl.ANY)

    def body(*refs):
        ins, outs = refs[:nw], refs[nw:2 * nw]
        send, recv = refs[2 * nw:]
        x, y, c = _my_place()
        copies = [pltpu.make_async_remote_copy(
            src_ref=ins[w], dst_ref=outs[w], send_sem=send.at[w], recv_sem=recv.at[w],
            device_id=(x, y, 1 - c), device_id_type=MESH_ID) for w in range(nw)]
        for cp in copies:
            cp.start()
        for cp in copies:
            cp.wait_send()
            cp.wait_recv()

    return pl.pallas_call(
        body, name=name, in_specs=[ANY] * nw, out_specs=[ANY] * nw,
        out_shape=[jax.ShapeDtypeStruct(a.shape, a.dtype) for a in arrays],
        scratch_shapes=[pltpu.SemaphoreType.DMA((nw,)), pltpu.SemaphoreType.DMA((nw,))],
        compiler_params=pltpu.CompilerParams(has_side_effects=True),
    )(*arrays)


def all_reduce_small(name, v):
    rows = v.shape[0]
    VM = pl.BlockSpec(memory_space=pltpu.VMEM)

    def body(v_ref, o_ref, buf, send, recv):
        x, y, c = _my_place()
        me = 4 * x + 2 * y + c

        def peer(kx):
            return (x ^ ((kx >> 2) & 1), y ^ ((kx >> 1) & 1), c ^ (kx & 1))

        def copy(kx, slot):
            return pltpu.make_async_remote_copy(
                src_ref=v_ref, dst_ref=buf.at[slot], send_sem=send.at[kx - 1], recv_sem=recv.at[kx - 1],
                device_id=peer(kx), device_id_type=MESH_ID)

        sends = [copy(kx, me) for kx in range(1, 8)]
        for cp in sends:
            cp.start()
        buf[me] = v_ref[...]
        for kx in range(1, 8):
            copy(kx, me ^ kx).wait_recv()
        for cp in sends:
            cp.wait_send()
        acc = buf[0]
        for d in range(1, 8):
            acc = acc + buf[d]
        o_ref[...] = acc

    return pl.pallas_call(
        body, name=name, in_specs=[VM], out_specs=VM, out_shape=jax.ShapeDtypeStruct(v.shape, F32),
        scratch_shapes=[pltpu.VMEM((8, rows, LANES), F32), pltpu.SemaphoreType.DMA((7,)), pltpu.SemaphoreType.DMA((7,))],
        compiler_params=pltpu.CompilerParams(has_side_effects=True, vmem_limit_bytes=VMEM_LIMIT),
    )(v)


def _pad_cols(a, n):
    return jnp.pad(a, ((0, 0), (0, n - a.shape[1])))


def _pad_rows(a, n):
    return jnp.pad(a, ((0, n - a.shape[0]), (0, 0)))


def _halo(u, tr):
    t, cdim = u.shape
    tails = u.reshape(t // tr, tr, cdim)[:, tr - HALO:, :]
    tails = jnp.concatenate([jnp.zeros((1, HALO, cdim), u.dtype), tails[:-1]], axis=0)
    return tails.reshape(-1, cdim)


def _unhalo(du, dhalo, tr):
    t, cdim = du.shape
    n = t // tr
    dh = dhalo.reshape(n, HALO, cdim)
    dh = jnp.concatenate([dh[1:], jnp.zeros((1, HALO, cdim), du.dtype)], axis=0)
    d3 = du.reshape(n, tr, cdim)
    d3 = jnp.concatenate([d3[:, :tr - HALO, :], d3[:, tr - HALO:, :] + dh], axis=1)
    return d3.reshape(t, cdim)


def _to_slots(g, axis):
    r, cdim = g.shape
    if axis == 0:
        return g.reshape(4, r // 4, cdim)
    return g.reshape(r, 4, cdim // 4).transpose(1, 0, 2)


def _from_slots(s, axis):
    if axis == 0:
        return s.reshape(s.shape[0] * s.shape[1], s.shape[2])
    return s.transpose(1, 0, 2).reshape(s.shape[1], 4 * s.shape[2])


BIG = ("w_in", "w_out", "xattn_wq", "xattn_wk", "xattn_wv", "xattn_wo", "ffn_w1", "ffn_w2")
BIG_AXIS = {"w_in": 1, "w_out": 0, "xattn_wq": 0, "xattn_wk": 0, "xattn_wv": 0, "xattn_wo": 0, "ffn_w1": 1, "ffn_w2": 0}
SMALL_SHARDED = ("ssd_conv_w", "rwkv_w2", "rwkv_a2", "rwkv_g2")
GATHER_GROUPS = (("w_in",), ("w_out", "xattn_wq", "xattn_wk", "xattn_wv", "xattn_wo"), ("ffn_w1", "ffn_w2"))
REDUCED = BIG + ("rwkv_w2", "rwkv_a2", "rwkv_g2")
REDUCE_AXIS = dict(BIG_AXIS, rwkv_w2=1, rwkv_a2=1, rwkv_g2=1)
WEIGHTS = ("norm_mix_g", "w_in", "ssd_conv_w", "ssd_conv_b", "ssd_dt_bias", "ssd_a_log", "ssd_d", "ssd_norm_g",
           "rwkv_mu", "rwkv_w0", "rwkv_w2", "rwkv_a0", "rwkv_a2", "rwkv_g2", "rwkv_k_k", "rwkv_k_a", "rwkv_r_k",
           "rwkv_ln_w", "rwkv_ln_b", "w_out", "norm_x_g", "norm_mem_g", "xattn_wq", "xattn_wk", "xattn_wv", "xattn_wo",
           "norm_ffn_g", "ffn_w1", "ffn_w2", "final_norm_g")


def _local_grads(x, mem, tgt, wt, full):
    t, d = x.shape
    w = d // 2
    nh = w // HEAD_DIM
    n_pairs = nh // 2
    ppg = n_pairs // SSD_GROUPS
    bc = SSD_GROUPS * SSD_STATE
    conv_dim = w + 2 * bc
    tr = ROW_TILE
    nt = t // tr
    dr = wt["rwkv_w2"].shape[0]
    ar = wt["rwkv_a2"].shape[0]
    gr = wt["rwkv_g2"].shape[0]

    w_in = full["w_in"]
    o = 0
    segs = {}
    for nm, width in (("z", w), ("xbc", conv_dim), ("dt", nh), ("rkv", 3 * w), ("pw", dr), ("pa", ar), ("pg", gr)):
        segs[nm] = (o, width)
        o += width
    padded = {"z": w, "xbc": conv_dim, "dt": LANES, "rkv": 3 * w, "pw": LANES, "pa": LANES, "pg": gr}
    order = ("z", "xbc", "dt", "rkv", "pw", "pa", "pg")
    w_perm = jnp.concatenate([_pad_cols(w_in[:, segs[nm][0]:segs[nm][0] + segs[nm][1]], padded[nm]) for nm in order], axis=1)
    offs = {}
    o = 0
    for nm in order:
        offs[nm] = o
        o += padded[nm]
    n_perm = o
    lora_w = 2 * LANES + gr

    def seg_cols(a, nm, width=None):
        return a[:, offs[nm]:offs[nm] + (padded[nm] if width is None else width)]

    mu = wt["rwkv_mu"]
    mo = 3 * w
    mu_rkv = mu[:, :mo]
    mu_lora = jnp.concatenate([_pad_cols(mu[:, mo:mo + dr], LANES), _pad_cols(mu[:, mo + dr:mo + dr + ar], LANES),
                               mu[:, mo + dr + ar:]], axis=1)
    w2p = _pad_rows(full["rwkv_w2"], LANES)
    a2p = _pad_rows(full["rwkv_a2"], LANES)
    g2 = full["rwkv_g2"]
    conv_w = full["ssd_conv_w"]
    cw = [conv_w[i:i + 1] for i in range(SSD_CONV)]
    dt_bias = _pad_cols(wt["ssd_dt_bias"], LANES)
    a_log = _pad_cols(wt["ssd_a_log"], LANES)
    d_skip = _pad_cols(wt["ssd_d"], LANES)
    r_k = wt["rwkv_r_k"].reshape(1, w)

    h1, h1t = norm_fwd("norm_mix", x, wt["norm_mix_g"], tr)
    u = matmul("in_proj", h1, w_perm)
    z, xbc, dtraw = seg_cols(u, "z"), seg_cols(u, "xbc"), seg_cols(u, "dt")
    urkv = seg_cols(u, "rkv")
    ulora = u[:, offs["pw"]:offs["pw"] + lora_w]

    halo_xbc = _halo(xbc, tr)
    ssd_pre_t = [(xbc, tr, conv_dim, 0), (halo_xbc, HALO, conv_dim, 0), (dtraw, tr, LANES, 0)]
    ssd_pre_f = cw + [wt["ssd_conv_b"], dt_bias]
    act, dt = fn_fwd("ssd_pre", _ssd_pre, nt, ssd_pre_t, ssd_pre_f, [(t, tr, conv_dim, F32), (t, tr, LANES, F32)])

    nb = w // LANES
    ssd_seq = [(act, None), (act, lambda p: nb + p // ppg), (act, lambda p: nb + SSD_GROUPS + p // ppg), (dt, lambda p: 0)]
    ssd_ppb = min(ppg, PAIRS_PER_STEP)
    rw_ppb = min(n_pairs, 2 * PAIRS_PER_STEP)

    def ssd_fn(sv, cv, hts, ids):
        return [_ssd_chunk(*s, cv[0], ht, p) for s, ht, p in zip(sv, hts, ids)]

    y_scan, ssd_states = scan_fwd("ssd_scan", ssd_fn, SSD_CHUNK, ssd_seq, [a_log], n_pairs, ssd_ppb)
    ssd_post_t = [(y_scan, tr, w, 0), (act, tr, w, 0), (z, tr, w, 0)]
    ssd_post_f = [d_skip, wt["ssd_norm_g"]]
    y_ssd, y_ssd_t = fn_fwd("ssd_post", _ssd_post, nt, ssd_post_t, ssd_post_f, [(t, tr, w, BF16)], (0,))

    halo_rkv, halo_lora = _halo(urkv, tr), _halo(ulora, tr)
    rw_pre_t = [(urkv, tr, 3 * w, 0), (ulora, tr, lora_w, 0), (halo_rkv, HALO, 3 * w, 0), (halo_lora, HALO, lora_w, 0)]
    rw_pre_f = [mu_rkv, mu_lora, wt["rwkv_w0"], wt["rwkv_a0"], wt["rwkv_k_k"], wt["rwkv_k_a"], w2p, a2p, g2]
    rw = fn_fwd("rwkv_pre", _rwkv_pre, nt, rw_pre_t, rw_pre_f, [(t, tr, w, F32)] * 7)
    r_, lw_, k2_, v_, nkk_, b_, gate_ = rw
    rw_seq = [(a, None) for a in (r_, lw_, k2_, v_, nkk_, b_)]

    def rw_fn(sv, cv, hts, ids):
        return _rwkv_chunks([(*s, ht) for s, ht in zip(sv, hts)])

    yr_scan, rw_states = scan_fwd("rwkv_scan", rw_fn, RWKV_CHUNK, rw_seq, [], n_pairs, rw_ppb)
    rw_post_t = [(a, tr, w, 0) for a in (yr_scan, r_, k2_, v_, gate_)]
    rw_post_f = [r_k, wt["rwkv_ln_w"], wt["rwkv_ln_b"]]
    y_rwkv, y_rwkv_t = fn_fwd("rwkv_post", _rwkv_post, nt, rw_post_t, rw_post_f, [(t, tr, w, BF16)], (0,))

    ymix = jnp.concatenate([y_ssd, y_rwkv], axis=1)
    ymix_t = jnp.concatenate([y_ssd_t, y_rwkv_t], axis=0)
    x1 = matmul("out_proj", ymix, full["w_out"], resid=x)

    h2, h2t = norm_fwd("norm_x", x1, wt["norm_x_g"], tr)
    mrows = mem.shape[0]
    mn, mnt = norm_fwd("norm_mem", mem, wt["norm_mem_g"], mrows)
    q = matmul("xattn_q", h2, full["xattn_wq"])
    kx = matmul("xattn_k", mn, full["xattn_wk"])
    vx = matmul("xattn_v", mn, full["xattn_wv"])
    ao, aot = fn_fwd("xattn_core", _attn, nt, [(q, tr, d, 0)], [kx, vx], [(t, tr, d, BF16)], (0,))
    x2 = matmul("xattn_o", ao, full["xattn_wo"], resid=x1)

    h3, h3t = norm_fwd("norm_ffn", x2, wt["norm_ffn_g"], tr)
    a1 = matmul("ffn_up", h3, full["ffn_w1"], out_dtype=BF16)
    dff = a1.shape[1]
    f1, f1t = fn_fwd("ffn_act", _relu2, nt, [(a1, tr, dff, 0)], [], [(t, tr, dff, BF16)], (0,))
    x3 = matmul("ffn_down", f1, full["ffn_w2"], resid=x2)

    dx3, g_final, loss_tile = loss_head(x3, tgt, wt["final_norm_g"].reshape(1, d), tr)

    grads = {"final_norm_g": g_final.reshape(d)}
    dx3b = dx3.astype(BF16)
    grads["ffn_w2"] = matmul("ffn_down_dw", f1t, dx3b)
    df1 = matmul("ffn_down_dx", dx3b, full["ffn_w2"], tb=True, out_dtype=BF16)
    (da1,), _ = fn_bwd("ffn_act_bwd", _relu2, nt, [(a1, tr, dff, 0)], [], [(df1, tr, dff, 0)], lambda c: [c[0].astype(F32)],
                       [(t, tr, dff, BF16)])
    grads["ffn_w1"] = matmul("ffn_up_dw", h3t, da1)
    dh3 = matmul("ffn_up_dx", da1, full["ffn_w1"], tb=True)
    dx2, grads["norm_ffn_g"] = norm_bwd("norm_ffn_bwd", x2, wt["norm_ffn_g"], dh3, dx3, tr)

    dx2b = dx2.astype(BF16)
    grads["xattn_wo"] = matmul("xattn_o_dw", aot, dx2b)
    dao = matmul("xattn_o_dx", dx2b, full["xattn_wo"], tb=True)
    (dq,), (dkx, dvx) = fn_bwd("xattn_core_bwd", _attn, nt, [(q, tr, d, 0)], [kx, vx], [(dao, tr, d, 0)], lambda c: c,
                               [(t, tr, d, BF16)])
    grads["xattn_wq"] = matmul("xattn_q_dw", h2t, dq)
    dh2 = matmul("xattn_q_dx", dq, full["xattn_wq"], tb=True)
    dkb, dvb = dkx.astype(BF16), dvx.astype(BF16)
    grads["xattn_wk"] = matmul("xattn_k_dw", mnt, dkb)
    grads["xattn_wv"] = matmul("xattn_v_dw", mnt, dvb)
    dmn = matmul("xattn_k_dx", dkb, full["xattn_wk"], tb=True)
    dmn = matmul("xattn_v_dx", dvb, full["xattn_wv"], tb=True, resid=dmn)
    _, grads["norm_mem_g"] = norm_bwd("norm_mem_bwd", mem, wt["norm_mem_g"], dmn, None, mrows)
    dx1, grads["norm_x_g"] = norm_bwd("norm_x_bwd", x1, wt["norm_x_g"], dh2, dx2, tr)

    dx1b = dx1.astype(BF16)
    grads["w_out"] = matmul("out_proj_dw", ymix_t, dx1b)
    dymix = matmul("out_proj_dx", dx1b, full["w_out"], tb=True)

    (dyr, dr1, dk1, dv1, dgate), (g_rk, grads["rwkv_ln_w"], grads["rwkv_ln_b"]) = fn_bwd(
        "rwkv_post_bwd", _rwkv_post, nt, rw_post_t, rw_post_f, [(dymix, tr, w, 1)], lambda c: c, [(t, tr, w, F32)] * 5)
    grads["rwkv_r_k"] = g_rk.reshape(wt["rwkv_r_k"].shape)
    (dr2, dlw, dk2, dv2, dnkk, db), _ = scan_bwd("rwkv_scan_bwd", rw_fn, RWKV_CHUNK, rw_seq, [], rw_states, dyr, n_pairs, rw_ppb)
    rw_ct = [(a, tr, w, 0) for a in (dr1, dr2, dlw, dk1, dk2, dv1, dv2, dnkk, db, dgate)]

    def rw_ct_fn(c):
        return (c[0] + c[1], c[2], c[3] + c[4], c[5] + c[6], c[7], c[8], c[9])

    (durkv, dulora, dhrkv, dhlora), rw_pg = fn_bwd(
        "rwkv_pre_bwd", _rwkv_pre, nt, rw_pre_t, rw_pre_f, rw_ct, rw_ct_fn,
        [(t, tr, 3 * w, F32), (t, tr, lora_w, F32), (nt * HALO, HALO, 3 * w, F32), (nt * HALO, HALO, lora_w, F32)])
    durkv = _unhalo(durkv, dhrkv, tr)
    dulora = _unhalo(dulora, dhlora, tr)
    g_mu_rkv, g_mu_lora, grads["rwkv_w0"], grads["rwkv_a0"], grads["rwkv_k_k"], grads["rwkv_k_a"], g_w2p, g_a2p, grads["rwkv_g2"] = rw_pg
    grads["rwkv_mu"] = jnp.concatenate([g_mu_rkv, g_mu_lora[:, :dr], g_mu_lora[:, LANES:LANES + ar], g_mu_lora[:, 2 * LANES:]], axis=1)
    grads["rwkv_w2"] = g_w2p[:dr]
    grads["rwkv_a2"] = g_a2p[:ar]

    (dys, dxs1, dz), (g_d, grads["ssd_norm_g"]) = fn_bwd(
        "ssd_post_bwd", _ssd_post, nt, ssd_post_t, ssd_post_f, [(dymix, tr, w, 0)], lambda c: c, [(t, tr, w, F32)] * 3)
    grads["ssd_d"] = g_d[:, :nh]
    (dxs2, dbp, dcp, ddtp), (g_alog,) = scan_bwd("ssd_scan_bwd", ssd_fn, SSD_CHUNK, ssd_seq, [a_log], ssd_states, dys, n_pairs, ssd_ppb)
    grads["ssd_a_log"] = g_alog[:, :nh]
    ssd_ct = [(dxs1, tr, w, 0), (dxs2, tr, w, 0), (dbp, tr, w, 0), (dcp, tr, w, 0), (ddtp, tr, w, 0)]

    def ssd_ct_fn(c):
        def group_sum(a):
            parts = []
            for gi in range(SSD_GROUPS):
                s = a[:, gi * ppg * LANES:(gi * ppg + 1) * LANES]
                for j in range(1, ppg):
                    s = s + a[:, (gi * ppg + j) * LANES:(gi * ppg + j + 1) * LANES]
                parts.append(s)
            return parts
        ddt = c[4][:, :LANES]
        for j in range(1, n_pairs):
            ddt = ddt + c[4][:, j * LANES:(j + 1) * LANES]
        return (jnp.concatenate([c[0] + c[1]] + group_sum(c[2]) + group_sum(c[3]), axis=1), ddt)

    (dxbc, dhxbc, ddtraw), ssd_pg = fn_bwd(
        "ssd_pre_bwd", _ssd_pre, nt, ssd_pre_t, ssd_pre_f, ssd_ct, ssd_ct_fn,
        [(t, tr, conv_dim, F32), (nt * HALO, HALO, conv_dim, F32), (t, tr, LANES, F32)])
    dxbc = _unhalo(dxbc, dhxbc, tr)
    grads["ssd_conv_w"] = jnp.concatenate(ssd_pg[:SSD_CONV], axis=0)
    grads["ssd_conv_b"] = ssd_pg[SSD_CONV]
    grads["ssd_dt_bias"] = ssd_pg[SSD_CONV + 1][:, :nh]

    du = jnp.concatenate([dz, dxbc, ddtraw, durkv, dulora], axis=1).astype(BF16)
    g_perm = matmul("in_proj_dw", h1t, du)
    grads["w_in"] = jnp.concatenate([seg_cols(g_perm, nm, segs[nm][1]) for nm in order], axis=1)
    dh1 = matmul("in_proj_dx", du, w_perm, tb=True)
    grad_x, grads["norm_mix_g"] = norm_bwd("norm_mix_bwd", x, wt["norm_mix_g"], dh1, dx1, tr)
    return loss_tile, grad_x, grads


def _pack(arrs):
    flat = jnp.concatenate([a.reshape(-1) for a in arrs])
    n = flat.shape[0]
    rows = -(-n // (8 * LANES)) * 8
    return jnp.pad(flat, (0, rows * LANES - n)).reshape(rows, LANES)


def _unpack(packed, shapes):
    flat = packed.reshape(-1)
    out, o = [], 0
    for s in shapes:
        n = math.prod(s)
        out.append(flat[o:o + n].reshape(s))
        o += n
    return out


def _as2d(a):
    return a.reshape(-1, a.shape[-1])


def _step(a):
    x, mem, tgt = a["x"][0], a["mem"][0], a["loss_target"][0]
    q = 2 * lax.axis_index("x") + lax.axis_index("y")

    shard2d = {n: _as2d(a[n][0]) for n in BIG}
    small_sh = {n: _as2d(a[n][0]) for n in SMALL_SHARDED}
    c = lax.axis_index("c")
    full = {}
    for gi, group in enumerate(GATHER_GROUPS):
        shards_bf16 = [shard2d[n].astype(BF16) for n in group]
        gathered = gather_two_level("gather_weights_%d" % gi, shards_bf16, gi + 1)
        for n, sh, g in zip(group, shards_bf16, gathered):
            hr = sh.shape[0] // 2
            own = lax.dynamic_slice_in_dim(sh, c * hr, hr, axis=0)
            g = lax.dynamic_update_slice(g, own[None, None], (q, c, 0, 0))
            full[n] = _from_slots(g.reshape(4, 2 * hr, g.shape[3]), BIG_AXIS[n])
    gathered = chip_exchange("gather_small", [small_sh[n] for n in SMALL_SHARDED], True)
    for n, g in zip(SMALL_SHARDED, gathered):
        full[n] = _from_slots(g, 1)

    wt = {n: (a[n] if a[n].ndim <= 2 else a[n][0]) for n in WEIGHTS if n not in BIG and n not in SMALL_SHARDED}
    for n in SMALL_SHARDED:
        wt[n] = small_sh[n]
    loss_tile, grad_x, grads = _local_grads(x, mem, tgt, wt, full)

    shards = dict(shard2d)
    shards.update({n: small_sh[n] for n in REDUCED if n not in BIG})
    kept, sent = [], []
    for n in REDUCED:
        s = _to_slots(grads[n], REDUCE_AXIS[n])
        s = s.reshape(4, 2, s.shape[1] // 2, s.shape[2])
        kept.append(lax.dynamic_index_in_dim(s, c, axis=1, keepdims=False))
        sent.append(lax.dynamic_index_in_dim(s, 1 - c, axis=1, keepdims=False).astype(BF16))
    got = core_swap("swap_halves", sent)
    chip_parts = []
    for n, k, g in zip(REDUCED, kept, got):
        _, hr, cols = k.shape
        tr = _pick(4 * hr, (256, 128, 64, 32, 16))
        (part,), _ = row_call("chip_sum_" + n, lambda tv, fv: ([tv[0] + tv[1].astype(F32)], []), 4 * hr // tr,
                              [(k.reshape(4 * hr, cols), tr, cols, 0), (g.reshape(4 * hr, cols), tr, cols, 0)], [],
                              [(4 * hr, tr, cols, BF16)], [])
        chip_parts.append(part.reshape(4, hr, cols))
    slots = chip_exchange("scatter_grads", chip_parts, False)
    halves = [sum_slots("sum_" + n, s) for n, s in zip(REDUCED, slots)]
    others = core_swap("swap_reduced", halves)

    out = {}
    for n, mine, other in zip(REDUCED, halves, others):
        gsum = jnp.concatenate([jnp.where(c == 0, mine, other), jnp.where(c == 0, other, mine)], axis=0)
        g, dlt, mn, vn = adamw("adamw_" + n, shards[n], _as2d(a["m_" + n][0]), _as2d(a["v_" + n][0]), [gsum])
        for key, val in (("grad_", g), ("delta_", dlt), ("new_m_", mn), ("new_v_", vn)):
            out[key + n] = val.reshape(a[n].shape)

    small = [n for n in WEIGHTS if n not in REDUCED]
    red = _unpack(all_reduce_small("all_reduce_small", _pack([grads[n] for n in small])), [grads[n].shape for n in small])
    g_loc = {}
    for n, g in zip(small, red):
        if n in SMALL_SHARDED:
            cols = g.shape[1] // 4
            g = lax.dynamic_slice_in_dim(g, q * cols, cols, axis=1)
        g_loc[n] = g.reshape(a[n].shape)
    res = adamw("adamw_small", *[_pack([src[n] for n in small]) for src in
                                 ({n: a[n] for n in small}, {n: a["m_" + n] for n in small}, {n: a["v_" + n] for n in small})],
                [_pack([g_loc[n] for n in small])])
    shapes = [a[n].shape for n in small]
    for key, packed in zip(("grad_", "delta_", "new_m_", "new_v_"), res):
        for n, val in zip(small, _unpack(packed, shapes)):
            out[key + n] = val

    loss = lax.psum(loss_tile[0, 0], ("x", "y", "c"))
    ordered = [loss, grad_x.reshape(a["x"].shape)]
    for key in ("grad_", "delta_", "new_m_", "new_v_"):
        ordered += [out[key + n] for n in WEIGHTS]
    return tuple(ordered)


def kernel(x, mem, norm_mix_g, w_in, ssd_conv_w, ssd_conv_b, ssd_dt_bias, ssd_a_log, ssd_d, ssd_norm_g, rwkv_mu, rwkv_w0, rwkv_w2, rwkv_a0, rwkv_a2, rwkv_g2, rwkv_k_k, rwkv_k_a, rwkv_r_k, rwkv_ln_w, rwkv_ln_b, w_out, norm_x_g, norm_mem_g, xattn_wq, xattn_wk, xattn_wv, xattn_wo, norm_ffn_g, ffn_w1, ffn_w2, final_norm_g, loss_target, m_norm_mix_g, m_w_in, m_ssd_conv_w, m_ssd_conv_b, m_ssd_dt_bias, m_ssd_a_log, m_ssd_d, m_ssd_norm_g, m_rwkv_mu, m_rwkv_w0, m_rwkv_w2, m_rwkv_a0, m_rwkv_a2, m_rwkv_g2, m_rwkv_k_k, m_rwkv_k_a, m_rwkv_r_k, m_rwkv_ln_w, m_rwkv_ln_b, m_w_out, m_norm_x_g, m_norm_mem_g, m_xattn_wq, m_xattn_wk, m_xattn_wv, m_xattn_wo, m_norm_ffn_g, m_ffn_w1, m_ffn_w2, m_final_norm_g, v_norm_mix_g, v_w_in, v_ssd_conv_w, v_ssd_conv_b, v_ssd_dt_bias, v_ssd_a_log, v_ssd_d, v_ssd_norm_g, v_rwkv_mu, v_rwkv_w0, v_rwkv_w2, v_rwkv_a0, v_rwkv_a2, v_rwkv_g2, v_rwkv_k_k, v_rwkv_k_a, v_rwkv_r_k, v_rwkv_ln_w, v_rwkv_ln_b, v_w_out, v_norm_x_g, v_norm_mem_g, v_xattn_wq, v_xattn_wk, v_xattn_wv, v_xattn_wo, v_norm_ffn_g, v_ffn_w1, v_ffn_w2, v_final_norm_g):
    return _step(dict(locals()))
```

```python
import functools
import math

import jax
import jax.numpy as jnp
from jax import lax
from jax.experimental import pallas as pl
from jax.experimental.pallas import tpu as pltpu
from jax.experimental.pallas import tpu_sc as plsc

F32 = jnp.float32
BF16 = jnp.bfloat16
HIGHEST = lax.Precision.HIGHEST
MESH_ID = pl.DeviceIdType.MESH

NORM_EPS = 1e-6
RWKV_LN_EPS = 64e-5
HEAD_DIM = 64
PAIR = 2 * HEAD_DIM
LANES = 128
SSD_STATE = 128
SSD_CHUNK = 128
SSD_GROUPS = 2
SSD_CONV = 4
RWKV_CHUNK = 64
HALO = 8
ROW_TILE = 128
PAIRS_PER_STEP = 4
XATTN_HEADS = 4
RWKV_PASSES = 1
VMEM_LIMIT = 56 * 1024 * 1024
MATMUL_VMEM = 40 * 1024 * 1024

ADAM_LR = 0.001
ADAM_B1 = 0.9
ADAM_B2 = 0.999
ADAM_EPS = 1e-08
ADAM_WD = 0.01
ADAM_STEP = 10


def _dims(ca, cb):
    return (((ca,), (cb,)), ((), ()))


def _split_bf16(a):
    hi = a.astype(BF16)
    lo = (a - hi.astype(F32)).astype(BF16)
    return hi, lo


def _mm_impl(a, b, ca, cb, passes):
    dn = _dims(ca, cb)
    if passes == 1:
        return lax.dot_general(a.astype(BF16), b.astype(BF16), dn, preferred_element_type=F32)
    ah, al = _split_bf16(a)
    bh, bl = _split_bf16(b)
    out = lax.dot_general(ah, bh, dn, preferred_element_type=F32)
    out = out + lax.dot_general(ah, bl, dn, preferred_element_type=F32)
    return out + lax.dot_general(al, bh, dn, preferred_element_type=F32)


@functools.partial(jax.custom_vjp, nondiff_argnums=(2, 3, 4))
def mm(a, b, ca, cb, passes):
    return _mm_impl(a, b, ca, cb, passes)


def _mm_fwd(a, b, ca, cb, passes):
    return _mm_impl(a, b, ca, cb, passes), (a, b)


def _mm_bwd(ca, cb, passes, res, g):
    a, b = res
    da = mm(g, b, 1, 1 - cb, passes) if ca == 1 else mm(b, g, 1 - cb, 1, passes)
    db = mm(a, g, 1 - ca, 0, passes) if cb == 0 else mm(g, a, 0, 1 - ca, passes)
    return da, db


mm.defvjp(_mm_fwd, _mm_bwd)


def _dot_exact(a, b):
    return lax.dot_general(a, b, _dims(1, 0), precision=HIGHEST, preferred_element_type=F32)


def _iota(shape, dim):
    return lax.broadcasted_iota(jnp.int32, shape, dim)


def _sigmoid(x):
    return 1.0 / (1.0 + jnp.exp(-x))


def _silu(x):
    return x * _sigmoid(x)


def _softplus(x):
    return jnp.maximum(x, 0.0) + jnp.log(1.0 + jnp.exp(-jnp.abs(x)))


def _rms(x, g):
    return x * lax.rsqrt(jnp.mean(x * x, axis=-1, keepdims=True) + NORM_EPS) * g


def _select_mm(x, sel):
    hi = x.astype(BF16)
    r1 = x - hi.astype(F32)
    mid = r1.astype(BF16)
    lo = (r1 - mid.astype(F32)).astype(BF16)
    dn = _dims(1, 0)
    out = lax.dot_general(hi, sel, dn, preferred_element_type=F32)
    out = out + lax.dot_general(mid, sel, dn, preferred_element_type=F32)
    return out + lax.dot_general(lo, sel, dn, preferred_element_type=F32)


def _head_sum_impl(x, n):
    sel = (_iota((n, LANES), 0) // HEAD_DIM == _iota((n, LANES), 1)).astype(BF16)
    return _select_mm(x, sel)


def _head_expand_impl(s, n):
    sel = (_iota((LANES, n), 1) // HEAD_DIM == _iota((LANES, n), 0)).astype(BF16)
    return _select_mm(s, sel)


@functools.partial(jax.custom_vjp, nondiff_argnums=(1,))
def _head_sum_n(x, n):
    return _head_sum_impl(x, n)


@functools.partial(jax.custom_vjp, nondiff_argnums=(1,))
def _head_expand(s, n):
    return _head_expand_impl(s, n)


_head_sum_n.defvjp(lambda x, n: (_head_sum_impl(x, n), None), lambda n, _, g: (_head_expand(g, n),))
_head_expand.defvjp(lambda s, n: (_head_expand_impl(s, n), None), lambda n, _, g: (_head_sum_n(g, n),))


def _head_sum(x):
    return _head_sum_n(x, x.shape[1])


def _row_vector_expand(v, n):
    v8 = jnp.broadcast_to(v, (8, LANES))
    return jnp.sum(_head_expand(v8, n), axis=0, keepdims=True) * 0.125


def _shift_rows_impl(u, halo, s):
    rolled = pltpu.roll(u, s, 0)
    top = jnp.where(_iota((HALO, 1), 0) < s, pltpu.roll(halo, s, 0), rolled[:HALO])
    return jnp.concatenate([top, rolled[HALO:]], axis=0)


@functools.partial(jax.custom_vjp, nondiff_argnums=(2,))
def _shift_rows(u, halo, s):
    return _shift_rows_impl(u, halo, s)


def _shift_rows_bwd(s, _, g):
    tr = g.shape[0]
    rolled = pltpu.roll(g, tr - s, 0)
    hrow = _iota((HALO, 1), 0)
    bottom = jnp.where(hrow < HALO - s, rolled[tr - HALO:], 0.0)
    dhalo = jnp.where(hrow >= HALO - s, pltpu.roll(g[:HALO], HALO - s, 0), 0.0)
    return jnp.concatenate([rolled[:tr - HALO], bottom], axis=0), dhalo


_shift_rows.defvjp(lambda u, halo, s: (_shift_rows_impl(u, halo, s), None), _shift_rows_bwd)


def _params(sem):
    return pltpu.CompilerParams(dimension_semantics=sem, vmem_limit_bytes=VMEM_LIMIT)


def row_call(name, body, n_tiles, tiled, full, out_tiled, out_acc, transposed=()):
    nt, nf, na = len(tiled), len(full), len(out_acc)
    n_plain = len(out_tiled)
    no = n_plain + len(transposed)

    def kern(*refs):
        tv = [r[...] for r in refs[:nt]]
        fv = [r[...] for r in refs[nt:nt + nf]]
        outs, accs = body(tv, fv)
        for r, v in zip(refs[nt + nf:nt + nf + n_plain], outs):
            r[...] = v.astype(r.dtype)
        for r, idx in zip(refs[nt + nf + n_plain:nt + nf + no], transposed):
            r[...] = outs[idx].astype(F32).T.astype(r.dtype)
        if na:
            a_refs = refs[nt + nf + no:]
            first = pl.program_id(0) == 0

            @pl.when(first)
            def _():
                for r, v in zip(a_refs, accs):
                    r[...] = v

            @pl.when(jnp.logical_not(first))
            def _():
                for r, v in zip(a_refs, accs):
                    r[...] += v

    in_specs = [pl.BlockSpec((rt, w), functools.partial(lambda i, cb: (i, cb), cb=cb)) for (_, rt, w, cb) in tiled]
    in_specs += [pl.BlockSpec(a.shape, lambda i: (0, 0)) for a in full]
    out_specs = [pl.BlockSpec((rt, w), lambda i: (i, 0)) for (_, rt, w, _) in out_tiled]
    out_specs += [pl.BlockSpec((out_tiled[idx][2], out_tiled[idx][1]), lambda i: (0, i)) for idx in transposed]
    out_specs += [pl.BlockSpec(s, lambda i: (0, 0)) for s in out_acc]
    out_shape = [jax.ShapeDtypeStruct((rows, w), dt) for (rows, _, w, dt) in out_tiled]
    out_shape += [jax.ShapeDtypeStruct((out_tiled[idx][2], out_tiled[idx][0]), BF16) for idx in transposed]
    out_shape += [jax.ShapeDtypeStruct(s, F32) for s in out_acc]
    res = pl.pallas_call(
        kern, name=name, grid=(n_tiles,), in_specs=in_specs, out_specs=out_specs, out_shape=out_shape,
        compiler_params=_params(("arbitrary",)),
    )(*[t[0] for t in tiled], *full)
    return list(res[:no]), list(res[no:])


def _pick(dim, cands):
    for c in cands:
        if dim % c == 0:
            return c
    return dim


def matmul(name, a, b, tb=False, resid=None, out_dtype=F32):
    m, k = a.shape
    n = b.shape[0] if tb else b.shape[1]
    has_resid = resid is not None
    out_bytes = jnp.dtype(out_dtype).itemsize
    sizes = (2048, 1024, 896, 768, 512, 384, 256, 128)
    tm = _pick(m, sizes[1:])
    tn = _pick(n, sizes[1:])

    def vmem_bytes(tk):
        return 2 * 2 * tk * (tm + tn) + tm * tn * (2 * out_bytes + 4 + (8 if has_resid else 0))

    tk = next((c for c in sizes if k % c == 0 and vmem_bytes(c) <= MATMUL_VMEM), LANES)
    nk = k // tk

    def kern(*refs):
        a_ref, b_ref = refs[0], refs[1]
        o_ref, acc = refs[-2], refs[-1]
        kk = pl.program_id(2)
        part = lax.dot_general(a_ref[...], b_ref[...], _dims(1, 1 if tb else 0), preferred_element_type=F32)

        def finish(out):
            if has_resid:
                out = out + refs[2][...]
            o_ref[...] = out.astype(o_ref.dtype)

        if nk == 1:
            finish(part)
            return

        @pl.when(kk == 0)
        def _():
            acc[...] = part

        @pl.when(jnp.logical_and(kk > 0, kk < nk - 1))
        def _():
            acc[...] += part

        @pl.when(kk == nk - 1)
        def _():
            finish(acc[...] + part)

    in_specs = [pl.BlockSpec((tm, tk), lambda i, j, kk: (i, kk))]
    if tb:
        in_specs.append(pl.BlockSpec((tn, tk), lambda i, j, kk: (j, kk)))
    else:
        in_specs.append(pl.BlockSpec((tk, tn), lambda i, j, kk: (kk, j)))
    args = [a, b]
    if has_resid:
        in_specs.append(pl.BlockSpec((tm, tn), lambda i, j, kk: (i, j)))
        args.append(resid)
    return pl.pallas_call(
        kern, name=name, grid=(m // tm, n // tn, nk), in_specs=in_specs,
        out_specs=pl.BlockSpec((tm, tn), lambda i, j, kk: (i, j)),
        out_shape=jax.ShapeDtypeStruct((m, n), out_dtype),
        scratch_shapes=[pltpu.VMEM((tm, tn), F32)],
        compiler_params=_params(("parallel", "parallel", "arbitrary")),
    )(*args)


def norm_fwd(name, x, g, tr):
    def body(tv, fv):
        return [_rms(tv[0], fv[0])], []
    rows, d = x.shape
    (h, ht), _ = row_call(name, body, rows // tr, [(x, tr, d, 0)], [g], [(rows, tr, d, BF16)], [], transposed=(0,))
    return h, ht


def norm_bwd(name, x, g, dh, extra, tr):
    def body(tv, fv):
        _, vjp = jax.vjp(_rms, tv[0], fv[0])
        dx, dg = vjp(tv[1])
        if extra is not None:
            dx = dx + tv[2]
        return [dx], [dg]
    rows, d = x.shape
    tiled = [(x, tr, d, 0), (dh, tr, d, 0)] + ([(extra, tr, d, 0)] if extra is not None else [])
    (dx,), (dg,) = row_call(name, body, rows // tr, tiled, [g], [(rows, tr, d, F32)], [g.shape])
    return dx, dg


def _ssd_pre(xbc, halo, dtraw, w0, w1, w2, w3, cb, dtb):
    y = w3 * xbc + w2 * _shift_rows(xbc, halo, 1) + w1 * _shift_rows(xbc, halo, 2) + w0 * _shift_rows(xbc, halo, 3) + cb
    return _silu(y), _softplus(dtraw + dtb)


def _ssd_post(ys, xs, z, dskip, ng):
    w = ys.shape[1]
    y = (ys + xs * _row_vector_expand(dskip, w)) * _silu(z)
    gw = w // SSD_GROUPS
    parts = []
    for gi in range(SSD_GROUPS):
        yg = y[:, gi * gw:(gi + 1) * gw]
        parts.append(yg * lax.rsqrt(jnp.mean(yg * yg, axis=-1, keepdims=True) + NORM_EPS))
    return jnp.concatenate(parts, axis=1) * ng


def _rwkv_pre(urkv, ulora, hrkv, hlora, mu_rkv, mu_lora, w0, a0, kkw, kaw, w2p, a2p, g2):
    w = w0.shape[1]
    urkv = urkv + (_shift_rows(urkv, hrkv, 1) - urkv) * mu_rkv
    ulora = ulora + (_shift_rows(ulora, hlora, 1) - ulora) * mu_lora
    r, k, v = urkv[:, :w], urkv[:, w:2 * w], urkv[:, 2 * w:]
    pw, pa, pg = ulora[:, :LANES], ulora[:, LANES:2 * LANES], ulora[:, 2 * LANES:]
    w_log = -_softplus(-(w0 + mm(jnp.tanh(pw), w2p, 1, 0, 1))) - 0.5
    lw = -jnp.exp(w_log)
    iclr = _sigmoid(a0 + mm(pa, a2p, 1, 0, 1))
    gate = mm(_sigmoid(pg), g2, 1, 0, 1)
    kk = k * kkw
    kk = kk / jnp.maximum(jnp.sqrt(_head_expand(_head_sum(kk * kk), w)), 1e-12)
    k2 = k * (1.0 + (iclr - 1.0) * kaw)
    return r, lw, k2, v, -kk, kk * iclr, gate


def _rwkv_post(ys, r, k2, v, gate, rk, lnw, lnb):
    w = ys.shape[1]
    inv = 1.0 / HEAD_DIM
    mean = _head_expand(_head_sum(ys), w) * inv
    d = ys - mean
    var = _head_expand(_head_sum(d * d), w) * inv
    yn = d * lax.rsqrt(var + RWKV_LN_EPS) * lnw + lnb
    bonus = _head_expand(_head_sum(r * k2 * rk), w) * v
    return (yn + bonus) * gate


def _attn(q, k, v):
    d = q.shape[1]
    hd = d // XATTN_HEADS
    outs = []
    for h in range(XATTN_HEADS):
        sl = slice(h * hd, (h + 1) * hd)
        s = mm(q[:, sl], k[:, sl], 1, 1, 1) * (hd ** -0.5)
        s = s - jnp.max(s, axis=-1, keepdims=True)
        p = jnp.exp(s)
        p = p / jnp.sum(p, axis=-1, keepdims=True)
        outs.append(mm(p, v[:, sl], 1, 0, 1))
    return jnp.concatenate(outs, axis=1)


def _relu2(a):
    return jnp.square(jnp.maximum(a.astype(F32), 0.0))


def fn_fwd(name, fn, n_tiles, tiled, full, out_tiled, transposed=()):
    def body(tv, fv):
        outs = fn(*tv, *fv)
        return (list(outs) if isinstance(outs, (tuple, list)) else [outs]), []
    outs, _ = row_call(name, body, n_tiles, tiled, full, out_tiled, [], transposed)
    return outs


def fn_bwd(name, fn, n_tiles, tiled, full, cts, ct_fn, out_tiled):
    nt = len(tiled)

    def body(tv, fv):
        outs, vjp = jax.vjp(fn, *tv[:nt], *fv)
        ct = ct_fn(tv[nt:])
        grads = vjp(tuple(ct) if isinstance(outs, (tuple, list)) else ct[0])
        return list(grads[:nt]), list(grads[nt:])
    return row_call(name, body, n_tiles, tiled + cts, full, out_tiled, [f.shape for f in full])


def _ssd_chunk(xs, bm, cm, dt_all, a_log, ht, p):
    q = xs.shape[0]
    lane = _iota((1, LANES), 1)
    row = _iota((q, 1), 0)
    tril = _iota((q, q), 0) >= _iota((q, q), 1)
    half = lane < HEAD_DIM
    da = dt_all * (-jnp.exp(a_log))
    cs = _dot_exact(tril.astype(F32), da)

    def col(mat, h):
        return jnp.sum(jnp.where(lane == h, mat, 0.0), axis=1, keepdims=True)

    cs0, cs1 = col(cs, 2 * p), col(cs, 2 * p + 1)
    xdt = xs * jnp.where(half, col(dt_all, 2 * p), col(dt_all, 2 * p + 1))
    csx = jnp.where(half, cs0, cs1)
    last = jnp.sum(jnp.where(row == q - 1, csx, 0.0), axis=0, keepdims=True)
    cb = mm(cm, bm, 1, 1, 1)
    y = mm(cm, ht, 1, 0, 1) * jnp.exp(csx)
    for csh, hm in ((cs0, half), (cs1, jnp.logical_not(half))):
        csl = jnp.broadcast_to(csh, (q, q))
        seg = csl - csl.T
        lmat = jnp.where(tril, jnp.exp(jnp.where(tril, seg, 0.0)), 0.0)
        y = y + jnp.where(hm, mm(cb * lmat, xdt, 1, 0, 1), 0.0)
    st = mm(bm, xdt * jnp.exp(last - csx), 0, 0, 1)
    return y, ht * jnp.exp(last) + st


def _rwkv_chunks(pairs):
    c = pairs[0][0].shape[0]
    ps = RWKV_PASSES
    lane = _iota((1, LANES), 1)
    row = _iota((c, 1), 0)
    ri, ci = _iota((c, c), 0), _iota((c, c), 1)
    tril_i, tril_s = ri >= ci, ri > ci
    eye = (ri == ci).astype(F32)
    half = lane < HEAD_DIM
    halves = (half, jnp.logical_not(half))
    bd = (_iota((LANES, LANES), 0) < HEAD_DIM) == (_iota((LANES, LANES), 1) < HEAD_DIM)
    tri = tril_i.astype(F32)
    n = len(pairs)
    heads = [(j, hm) for j in range(n) for hm in halves]

    cum = [_dot_exact(tri, p[1]) for p in pairs]
    at = [p[4] * jnp.exp(cm - p[1]) for p, cm in zip(pairs, cum)]
    en = [jnp.exp(-cm) for cm in cum]
    bt = [p[5] * e for p, e in zip(pairs, en)]
    kt = [p[2] * e for p, e in zip(pairs, en)]
    rt = [p[0] * jnp.exp(cm) for p, cm in zip(pairs, cum)]
    ah = [mm(at[j], pairs[j][6], 1, 1, ps) for j in range(n)]
    y = [mm(rt[j], pairs[j][6], 1, 1, ps) for j in range(n)]
    atm = [jnp.where(hm, at[j], 0.0) for j, hm in heads]
    rtm = [jnp.where(hm, rt[j], 0.0) for j, hm in heads]
    aab = [jnp.where(tril_s, mm(atm[i], bt[j], 1, 1, ps), 0.0) for i, (j, _) in enumerate(heads)]
    aak = [jnp.where(tril_s, mm(atm[i], kt[j], 1, 1, ps), 0.0) for i, (j, _) in enumerate(heads)]
    arb = [jnp.where(tril_i, mm(rtm[i], bt[j], 1, 1, ps), 0.0) for i, (j, _) in enumerate(heads)]
    ark = [jnp.where(tril_i, mm(rtm[i], kt[j], 1, 1, ps), 0.0) for i, (j, _) in enumerate(heads)]
    rhs = [ah[j] + mm(aak[i], pairs[j][3], 1, 0, ps) for i, (j, _) in enumerate(heads)]
    yv = [mm(ark[i], pairs[j][3], 1, 0, ps) for i, (j, _) in enumerate(heads)]
    tm = [eye + a_ for a_ in aab]
    pm = aab
    for _ in range(int(math.log2(c)) - 1):
        pm = [mm(p_, p_, 1, 0, ps) for p_ in pm]
        tm = [t_ + mm(t_, p_, 1, 0, ps) for t_, p_ in zip(tm, pm)]
    uh = [mm(tm[i], rhs[i], 1, 0, ps) for i in range(len(heads))]
    u = [jnp.where(half, uh[2 * j], uh[2 * j + 1]) for j in range(n)]
    yu = [mm(arb[i], u[j], 1, 0, ps) for i, (j, _) in enumerate(heads)]
    out = []
    for j in range(n):
        yj = y[j] + jnp.where(half, yu[2 * j] + yv[2 * j], yu[2 * j + 1] + yv[2 * j + 1])
        plast = jnp.sum(jnp.where(row == c - 1, cum[j], 0.0), axis=0, keepdims=True)
        upd = pairs[j][6] + mm(u[j], bt[j], 0, 0, ps) + mm(pairs[j][3], kt[j], 0, 0, ps)
        out.append((yj, jnp.where(bd, upd * jnp.exp(plast), 0.0)))
    return out


def _seq_spec(chunk, ppb, col, row_of):
    if col is None:
        return pl.BlockSpec((chunk, ppb * LANES), lambda pb, i: (row_of(i), pb))
    return pl.BlockSpec((chunk, LANES), lambda pb, i: (row_of(i), col(pb * ppb)))


def _pair_vals(refs, seq_in, j):
    return [r[...] if col is not None else r[:, j * LANES:(j + 1) * LANES] for r, (_, col) in zip(refs, seq_in)]


def scan_fwd(name, chunk_fn, chunk, seq_in, const_in, n_pairs, ppb):
    t = seq_in[0][0].shape[0]
    nc = t // chunk
    ns, ncst = len(seq_in), len(const_in)

    def kern(*refs):
        y_ref, st_ref, ht = refs[ns + ncst], refs[ns + ncst + 1], refs[ns + ncst + 2]

        @pl.when(pl.program_id(1) == 0)
        def _():
            ht[...] = jnp.zeros_like(ht)

        cv = [r[...] for r in refs[ns:ns + ncst]]
        h0 = [ht[j] for j in range(ppb)]
        for j in range(ppb):
            st_ref[j] = h0[j]
        sv = [_pair_vals(refs[:ns], seq_in, j) for j in range(ppb)]
        outs = chunk_fn(sv, cv, h0, [pl.program_id(0) * ppb + j for j in range(ppb)])
        for j, (y, hn) in enumerate(outs):
            y_ref[:, j * LANES:(j + 1) * LANES] = y
            ht[j] = hn

    in_specs = [_seq_spec(chunk, ppb, col, lambda i: i) for (_, col) in seq_in]
    in_specs += [pl.BlockSpec(a.shape, lambda pb, i: (0, 0)) for a in const_in]
    return pl.pallas_call(
        kern, name=name, grid=(n_pairs // ppb, nc), in_specs=in_specs,
        out_specs=[pl.BlockSpec((chunk, ppb * LANES), lambda pb, i: (i, pb)),
                   pl.BlockSpec((ppb, None, LANES, LANES), lambda pb, i: (pb, i, 0, 0))],
        out_shape=[jax.ShapeDtypeStruct((t, n_pairs * LANES), F32), jax.ShapeDtypeStruct((n_pairs, nc, LANES, LANES), F32)],
        scratch_shapes=[pltpu.VMEM((ppb, LANES, LANES), F32)],
        compiler_params=_params(("arbitrary", "arbitrary")),
    )(*[s[0] for s in seq_in], *const_in)


def scan_bwd(name, chunk_fn, chunk, seq_in, const_in, states, dy, n_pairs, ppb):
    t = dy.shape[0]
    nc = t // chunk
    ns, ncst = len(seq_in), len(const_in)

    def kern(*refs):
        seq_refs, cst_refs = refs[:ns], refs[ns:ns + ncst]
        st_ref, dy_ref = refs[ns + ncst], refs[ns + ncst + 1]
        o = ns + ncst + 2
        dseq_refs, dcst_refs, dht = refs[o:o + ns], refs[o + ns:o + ns + ncst], refs[o + ns + ncst]
        pb, i = pl.program_id(0), pl.program_id(1)

        @pl.when(i == 0)
        def _():
            dht[...] = jnp.zeros_like(dht)

        ids = [pb * ppb + j for j in range(ppb)]
        lanes = [slice(j * LANES, (j + 1) * LANES) for j in range(ppb)]

        def fn(*flat):
            sv = [list(flat[j * ns:(j + 1) * ns]) for j in range(ppb)]
            outs = chunk_fn(sv, list(flat[ppb * ns:ppb * ns + ncst]), list(flat[ppb * ns + ncst:]), ids)
            return tuple(y for y, _ in outs), tuple(h for _, h in outs)

        flat_in = [v for j in range(ppb) for v in _pair_vals(seq_refs, seq_in, j)]
        flat_in += [r[...] for r in cst_refs] + [st_ref[j] for j in range(ppb)]
        _, vjp = jax.vjp(fn, *flat_in)
        grads = vjp((tuple(dy_ref[:, ln] for ln in lanes), tuple(dht[j] for j in range(ppb))))
        for j in range(ppb):
            for r, g in zip(dseq_refs, grads[j * ns:(j + 1) * ns]):
                r[:, lanes[j]] = g
            dht[j] = grads[ppb * ns + ncst + j]
        dcv = grads[ppb * ns:ppb * ns + ncst]
        if ncst:
            first = jnp.logical_and(pb == 0, i == 0)

            @pl.when(first)
            def _():
                for r, g in zip(dcst_refs, dcv):
                    r[...] = g

            @pl.when(jnp.logical_not(first))
            def _():
                for r, g in zip(dcst_refs, dcv):
                    r[...] += g

    rev = lambda i: nc - 1 - i
    wide = pl.BlockSpec((chunk, ppb * LANES), lambda pb, i: (rev(i), pb))
    in_specs = [_seq_spec(chunk, ppb, col, rev) for (_, col) in seq_in]
    in_specs += [pl.BlockSpec(a.shape, lambda pb, i: (0, 0)) for a in const_in]
    in_specs += [pl.BlockSpec((ppb, None, LANES, LANES), lambda pb, i: (pb, rev(i), 0, 0)), wide]
    out_specs = [wide for _ in seq_in]
    out_specs += [pl.BlockSpec(a.shape, lambda pb, i: (0, 0)) for a in const_in]
    out_shape = [jax.ShapeDtypeStruct((t, n_pairs * LANES), F32) for _ in seq_in]
    out_shape += [jax.ShapeDtypeStruct(a.shape, F32) for a in const_in]
    res = pl.pallas_call(
        kern, name=name, grid=(n_pairs // ppb, nc), in_specs=in_specs, out_specs=out_specs, out_shape=out_shape,
        scratch_shapes=[pltpu.VMEM((ppb, LANES, LANES), F32)],
        compiler_params=_params(("arbitrary", "arbitrary")),
    )(*[s[0] for s in seq_in], *const_in, states, dy)
    return list(res[:ns]), list(res[ns:])


def loss_head(x3, tgt, g, tr):
    rows, d = x3.shape

    def body(tv, fv):
        def f(x, gg):
            e = jnp.square(_rms(x, gg) - tv[1])
            return 0.5 * jnp.sum(jnp.mean(e, axis=-1, keepdims=True), axis=0, keepdims=True)
        l, vjp = jax.vjp(f, tv[0], fv[0])
        dx, dg = vjp(jnp.ones((1, 1), F32))
        return [dx], [dg, jnp.broadcast_to(l, (8, LANES))]
    (dx,), (dg, l) = row_call("loss_head", body, rows // tr, [(x3, tr, d, 0), (tgt, tr, d, 0)], [g],
                              [(rows, tr, d, F32)], [g.shape, (8, LANES)])
    return dx, dg, l


def _adam_math(w, g, m, v):
    m = ADAM_B1 * m + (1.0 - ADAM_B1) * g
    v = ADAM_B2 * v + (1.0 - ADAM_B2) * jnp.square(g)
    m_hat = m / (1.0 - ADAM_B1 ** ADAM_STEP)
    v_hat = v / (1.0 - ADAM_B2 ** ADAM_STEP)
    delta = -ADAM_LR * (m_hat / (jnp.sqrt(v_hat) + ADAM_EPS) + ADAM_WD * w)
    return delta, m, v


def adamw(name, w, m, v, g_parts):
    rows, cols = w.shape
    tr = _pick(rows, (256, 128, 64, 32, 16, 8))
    n_g = len(g_parts)

    def body(tv, fv):
        g = tv[3]
        for extra in tv[4:4 + n_g - 1]:
            g = g + extra
        delta, mn, vn = _adam_math(tv[0], g, tv[1], tv[2])
        return [g, delta, mn, vn], []
    tiled = [(a, tr, cols, 0) for a in (w, m, v, *g_parts)]
    outs, _ = row_call(name, body, rows // tr, tiled, [], [(rows, tr, cols, F32)] * 4, [])
    return outs


def sum_slots(name, r):
    _, rows, cols = r.shape
    tr = _pick(rows, (256, 128, 64, 32, 16, 8))

    def kern(r0, r1, r2, r3, o):
        o[...] = ((r0[...].astype(F32) + r1[...].astype(F32)) + r2[...].astype(F32)) + r3[...].astype(F32)

    in_specs = [pl.BlockSpec((None, tr, cols), functools.partial(lambda i, s: (s, i, 0), s=s)) for s in range(4)]
    return pl.pallas_call(
        kern, name=name, grid=(rows // tr,), in_specs=in_specs, out_specs=pl.BlockSpec((tr, cols), lambda i: (i, 0)),
        out_shape=jax.ShapeDtypeStruct((rows, cols), F32), compiler_params=_params(("arbitrary",)),
    )(r, r, r, r)


def _my_place():
    return lax.axis_index("x"), lax.axis_index("y"), lax.axis_index("c")


def _chip_peers(x, y):
    peers = [(1 - x, y), (x, 1 - y), (1 - x, 1 - y)]
    return peers, [2 * px + py for px, py in peers]


def gather_shards(name, arrays):
    nw = len(arrays)
    ANY = pl.BlockSpec(memory_space=pl.ANY)

    def body(*refs):
        ins, outs = refs[:nw], refs[nw:2 * nw]
        send, recv, loc = refs[2 * nw:]
        x, y, c = _my_place()
        q = 2 * x + y
        peers, chips = _chip_peers(x, y)

        def remote(w, j, slot):
            return pltpu.make_async_remote_copy(
                src_ref=ins[w], dst_ref=outs[w].at[slot], send_sem=send.at[w, j], recv_sem=recv.at[w, j],
                device_id=(*peers[j], c), device_id_type=MESH_ID)

        local = [pltpu.make_async_copy(ins[w], outs[w].at[q], loc.at[w]) for w in range(nw)]
        sends = [[remote(w, j, q) for j in range(3)] for w in range(nw)]
        for w in range(nw):
            local[w].start()
            for j in range(3):
                sends[w][j].start()
        for w in range(nw):
            local[w].wait()
            for j in range(3):
                sends[w][j].wait_send()
                remote(w, j, chips[j]).wait_recv()

    return pl.pallas_call(
        body, name=name, in_specs=[ANY] * nw, out_specs=[ANY] * nw,
        out_shape=[jax.ShapeDtypeStruct((4,) + a.shape, a.dtype) for a in arrays],
        scratch_shapes=[pltpu.SemaphoreType.DMA((nw, 3)), pltpu.SemaphoreType.DMA((nw, 3)), pltpu.SemaphoreType.DMA((nw,))],
        compiler_params=pltpu.CompilerParams(has_side_effects=True),
    )(*arrays)


def scatter_slots(name, arrays, collective_id):
    nw = len(arrays)

    def body(*refs):
        ins, outs = refs[:nw], refs[nw:2 * nw]
        send, recv = refs[2 * nw:]
        x, y, c = _my_place()
        q = 2 * x + y
        peers, chips = _chip_peers(x, y)
        barrier = pltpu.get_barrier_semaphore()
        for p in peers:
            pl.semaphore_signal(barrier, inc=1, device_id=(*p, c), device_id_type=MESH_ID)
        pl.semaphore_wait(barrier, 3)

        def remote(w, j, src_slot, dst_slot):
            return pltpu.make_async_remote_copy(
                src_ref=ins[w].at[src_slot], dst_ref=outs[w].at[dst_slot], send_sem=send.at[w, j], recv_sem=recv.at[w, j],
                device_id=(*peers[j], c), device_id_type=MESH_ID)

        sends = [[remote(w, j, chips[j], q) for j in range(3)] for w in range(nw)]
        for w in range(nw):
            for j in range(3):
                sends[w][j].start()
        for w in range(nw):
            for j in range(3):
                sends[w][j].wait_send()
                remote(w, j, q, chips[j]).wait_recv()

    return pl.kernel(
        body, out_type=[jax.ShapeDtypeStruct(a.shape, a.dtype) for a in arrays],
        mesh=plsc.ScalarSubcoreMesh(axis_name="sequencer", num_cores=1), name=name,
        scratch_types=[pltpu.SemaphoreType.DMA((nw, 3)), pltpu.SemaphoreType.DMA((nw, 3))],
        compiler_params=pltpu.CompilerParams(collective_id=collective_id),
    )(*arrays)


def gather_two_level(name, arrays, collective_id):
    nw = len(arrays)

    def body(*refs):
        ins, outs = refs[:nw], refs[nw:2 * nw]
        send, recv = refs[2 * nw:]
        x, y, c = _my_place()
        q = 2 * x + y
        me, sibling = (x, y, c), (x, y, 1 - c)
        peers = [(1 - x, y), (x, 1 - y), (1 - x, 1 - y)]
        chips = [2 * px + py for px, py in peers]
        barrier = pltpu.get_barrier_semaphore()
        for dev in [sibling] + [(*p, c) for p in peers]:
            pl.semaphore_signal(barrier, inc=1, device_id=dev, device_id_type=MESH_ID)
        pl.semaphore_wait(barrier, 4)

        def mine(w):
            hr = ins[w].shape[0] // 2
            return ins[w].at[pl.ds(c * hr, hr)]

        def copy(w, k, src, chip, half, to):
            return pltpu.make_async_remote_copy(
                src_ref=src, dst_ref=outs[w].at[chip, half], send_sem=send.at[w, k], recv_sem=recv.at[w, k],
                device_id=to, device_id_type=MESH_ID)

        first = [[copy(w, 0, mine(w), q, c, sibling)] + [copy(w, 1 + j, mine(w), q, c, (*peers[j], c)) for j in range(3)]
                 for w in range(nw)]
        for w in range(nw):
            for cp in first[w]:
                cp.start()
        passed = []
        for w in range(nw):
            for j in range(3):
                copy(w, 1 + j, mine(w), chips[j], c, me).wait_recv()
                fwd = copy(w, 4 + j, outs[w].at[chips[j], c], chips[j], c, sibling)
                fwd.start()
                passed.append(fwd)
        for w in range(nw):
            copy(w, 0, mine(w), q, 1 - c, me).wait_recv()
            for j in range(3):
                copy(w, 4 + j, mine(w), chips[j], 1 - c, me).wait_recv()
        for w in range(nw):
            for cp in first[w]:
                cp.wait_send()
        for cp in passed:
            cp.wait_send()

    out_type = [jax.ShapeDtypeStruct((4, 2, a.shape[0] // 2, a.shape[1]), a.dtype) for a in arrays]
    return pl.kernel(
        body, out_type=out_type, mesh=plsc.ScalarSubcoreMesh(axis_name="sequencer", num_cores=1), name=name,
        scratch_types=[pltpu.SemaphoreType.DMA((nw, 7)), pltpu.SemaphoreType.DMA((nw, 7))],
        compiler_params=pltpu.CompilerParams(collective_id=collective_id),
    )(*arrays)


def core_swap(name, arrays):
    nw = len(arrays)
    ANY = pl.BlockSpec(memory_space=pl.ANY)

    def body(*refs):
        ins, outs = refs[:nw], refs[nw:2 * nw]
        send, recv = refs[2 * nw:]
        x, y, c = _my_place()
        copies = [pltpu.make_async_remote_copy(
            src_ref=ins[w], dst_ref=outs[w], send_sem=send.at[w], recv_sem=recv.at[w],
            device_id=(x, y, 1 - c), device_id_type=MESH_ID) for w in range(nw)]
        for cp in copies:
            cp.start()
        for cp in copies:
            cp.wait_send()
            cp.wait_recv()

    return pl.pallas_call(
        body, name=name, in_specs=[ANY] * nw, out_specs=[ANY] * nw,
        out_shape=[jax.ShapeDtypeStruct(a.shape, a.dtype) for a in arrays],
        scratch_shapes=[pltpu.SemaphoreType.DMA((nw,)), pltpu.SemaphoreType.DMA((nw,))],
        compiler_params=pltpu.CompilerParams(has_side_effects=True),
    )(*arrays)


def all_reduce_small(name, v):
    rows = v.shape[0]
    VM = pl.BlockSpec(memory_space=pltpu.VMEM)

    def body(v_ref, o_ref, buf, send, recv):
        x, y, c = _my_place()
        me = 4 * x + 2 * y + c

        def peer(kx):
            return (x ^ ((kx >> 2) & 1), y ^ ((kx >> 1) & 1), c ^ (kx & 1))

        def copy(kx, slot):
            return pltpu.make_async_remote_copy(
                src_ref=v_ref, dst_ref=buf.at[slot], send_sem=send.at[kx - 1], recv_sem=recv.at[kx - 1],
                device_id=peer(kx), device_id_type=MESH_ID)

        sends = [copy(kx, me) for kx in range(1, 8)]
        for cp in sends:
            cp.start()
        buf[me] = v_ref[...]
        for kx in range(1, 8):
            copy(kx, me ^ kx).wait_recv()
        for cp in sends:
            cp.wait_send()
        acc = buf[0]
        for d in range(1, 8):
            acc = acc + buf[d]
        o_ref[...] = acc

    return pl.pallas_call(
        body, name=name, in_specs=[VM], out_specs=VM, out_shape=jax.ShapeDtypeStruct(v.shape, F32),
        scratch_shapes=[pltpu.VMEM((8, rows, LANES), F32), pltpu.SemaphoreType.DMA((7,)), pltpu.SemaphoreType.DMA((7,))],
        compiler_params=pltpu.CompilerParams(has_side_effects=True, vmem_limit_bytes=VMEM_LIMIT),
    )(v)


def _pad_cols(a, n):
    return jnp.pad(a, ((0, 0), (0, n - a.shape[1])))


def _pad_rows(a, n):
    return jnp.pad(a, ((0, n - a.shape[0]), (0, 0)))


def _halo(u, tr):
    t, cdim = u.shape
    tails = u.reshape(t // tr, tr, cdim)[:, tr - HALO:, :]
    tails = jnp.concatenate([jnp.zeros((1, HALO, cdim), u.dtype), tails[:-1]], axis=0)
    return tails.reshape(-1, cdim)


def _unhalo(du, dhalo, tr):
    t, cdim = du.shape
    n = t // tr
    dh = dhalo.reshape(n, HALO, cdim)
    dh = jnp.concatenate([dh[1:], jnp.zeros((1, HALO, cdim), du.dtype)], axis=0)
    d3 = du.reshape(n, tr, cdim)
    d3 = jnp.concatenate([d3[:, :tr - HALO, :], d3[:, tr - HALO:, :] + dh], axis=1)
    return d3.reshape(t, cdim)


def _to_slots(g, axis):
    r, cdim = g.shape
    if axis == 0:
        return g.reshape(4, r // 4, cdim)
    return g.reshape(r, 4, cdim // 4).transpose(1, 0, 2)


def _from_slots(s, axis):
    if axis == 0:
        return s.reshape(s.shape[0] * s.shape[1], s.shape[2])
    return s.transpose(1, 0, 2).reshape(s.shape[1], 4 * s.shape[2])


BIG = ("w_in", "w_out", "xattn_wq", "xattn_wk", "xattn_wv", "xattn_wo", "ffn_w1", "ffn_w2")
BIG_AXIS = {"w_in": 1, "w_out": 0, "xattn_wq": 0, "xattn_wk": 0, "xattn_wv": 0, "xattn_wo": 0, "ffn_w1": 1, "ffn_w2": 0}
SMALL_SHARDED = ("ssd_conv_w", "rwkv_w2", "rwkv_a2", "rwkv_g2")
GATHER_GROUPS = (("w_in",), ("w_out", "xattn_wq", "xattn_wk", "xattn_wv", "xattn_wo"), ("ffn_w1", "ffn_w2"))
REDUCE_GROUPS = (("ffn_w2", "ffn_w1"), ("xattn_wo", "xattn_wq", "xattn_wk", "xattn_wv", "w_out"),
                 ("rwkv_w2", "rwkv_a2", "rwkv_g2", "w_in"))
REDUCED = BIG + ("rwkv_w2", "rwkv_a2", "rwkv_g2")
REDUCE_AXIS = dict(BIG_AXIS, rwkv_w2=1, rwkv_a2=1, rwkv_g2=1)
WEIGHTS = ("norm_mix_g", "w_in", "ssd_conv_w", "ssd_conv_b", "ssd_dt_bias", "ssd_a_log", "ssd_d", "ssd_norm_g",
           "rwkv_mu", "rwkv_w0", "rwkv_w2", "rwkv_a0", "rwkv_a2", "rwkv_g2", "rwkv_k_k", "rwkv_k_a", "rwkv_r_k",
           "rwkv_ln_w", "rwkv_ln_b", "w_out", "norm_x_g", "norm_mem_g", "xattn_wq", "xattn_wk", "xattn_wv", "xattn_wo",
           "norm_ffn_g", "ffn_w1", "ffn_w2", "final_norm_g")


def _local_grads(x, mem, tgt, wt, full, big, reducer):
    t, d = x.shape
    w = d // 2
    nh = w // HEAD_DIM
    n_pairs = nh // 2
    ppg = n_pairs // SSD_GROUPS
    bc = SSD_GROUPS * SSD_STATE
    conv_dim = w + 2 * bc
    tr = ROW_TILE
    nt = t // tr
    dr = wt["rwkv_w2"].shape[0]
    ar = wt["rwkv_a2"].shape[0]
    gr = wt["rwkv_g2"].shape[0]

    big.start(0, None)
    big.start(1, None)
    w_in = big.get("w_in", None)
    o = 0
    segs = {}
    for nm, width in (("z", w), ("xbc", conv_dim), ("dt", nh), ("rkv", 3 * w), ("pw", dr), ("pa", ar), ("pg", gr)):
        segs[nm] = (o, width)
        o += width
    padded = {"z": w, "xbc": conv_dim, "dt": LANES, "rkv": 3 * w, "pw": LANES, "pa": LANES, "pg": gr}
    order = ("z", "xbc", "dt", "rkv", "pw", "pa", "pg")
    w_perm = jnp.concatenate([_pad_cols(w_in[:, segs[nm][0]:segs[nm][0] + segs[nm][1]], padded[nm]) for nm in order], axis=1)
    offs = {}
    o = 0
    for nm in order:
        offs[nm] = o
        o += padded[nm]
    n_perm = o
    lora_w = 2 * LANES + gr

    def seg_cols(a, nm, width=None):
        return a[:, offs[nm]:offs[nm] + (padded[nm] if width is None else width)]

    mu = wt["rwkv_mu"]
    mo = 3 * w
    mu_rkv = mu[:, :mo]
    mu_lora = jnp.concatenate([_pad_cols(mu[:, mo:mo + dr], LANES), _pad_cols(mu[:, mo + dr:mo + dr + ar], LANES),
                               mu[:, mo + dr + ar:]], axis=1)
    w2p = _pad_rows(full["rwkv_w2"], LANES)
    a2p = _pad_rows(full["rwkv_a2"], LANES)
    g2 = full["rwkv_g2"]
    conv_w = full["ssd_conv_w"]
    cw = [conv_w[i:i + 1] for i in range(SSD_CONV)]
    dt_bias = _pad_cols(wt["ssd_dt_bias"], LANES)
    a_log = _pad_cols(wt["ssd_a_log"], LANES)
    d_skip = _pad_cols(wt["ssd_d"], LANES)
    r_k = wt["rwkv_r_k"].reshape(1, w)

    h1, h1t = norm_fwd("norm_mix", x, wt["norm_mix_g"], tr)
    u = matmul("in_proj", h1, w_perm)
    big.start(2, u)
    z, xbc, dtraw = seg_cols(u, "z"), seg_cols(u, "xbc"), seg_cols(u, "dt")
    urkv = seg_cols(u, "rkv")
    ulora = u[:, offs["pw"]:offs["pw"] + lora_w]

    halo_xbc = _halo(xbc, tr)
    ssd_pre_t = [(xbc, tr, conv_dim, 0), (halo_xbc, HALO, conv_dim, 0), (dtraw, tr, LANES, 0)]
    ssd_pre_f = cw + [wt["ssd_conv_b"], dt_bias]
    act, dt = fn_fwd("ssd_pre", _ssd_pre, nt, ssd_pre_t, ssd_pre_f, [(t, tr, conv_dim, F32), (t, tr, LANES, F32)])

    nb = w // LANES
    ssd_seq = [(act, None), (act, lambda p: nb + p // ppg), (act, lambda p: nb + SSD_GROUPS + p // ppg), (dt, lambda p: 0)]
    ssd_ppb = min(ppg, PAIRS_PER_STEP)
    rw_ppb = min(n_pairs, 2 * PAIRS_PER_STEP)

    def ssd_fn(sv, cv, hts, ids):
        return [_ssd_chunk(*s, cv[0], ht, p) for s, ht, p in zip(sv, hts, ids)]

    y_scan, ssd_states = scan_fwd("ssd_scan", ssd_fn, SSD_CHUNK, ssd_seq, [a_log], n_pairs, ssd_ppb)
    ssd_post_t = [(y_scan, tr, w, 0), (act, tr, w, 0), (z, tr, w, 0)]
    ssd_post_f = [d_skip, wt["ssd_norm_g"]]
    y_ssd, y_ssd_t = fn_fwd("ssd_post", _ssd_post, nt, ssd_post_t, ssd_post_f, [(t, tr, w, BF16)], (0,))

    halo_rkv, halo_lora = _halo(urkv, tr), _halo(ulora, tr)
    rw_pre_t = [(urkv, tr, 3 * w, 0), (ulora, tr, lora_w, 0), (halo_rkv, HALO, 3 * w, 0), (halo_lora, HALO, lora_w, 0)]
    rw_pre_f = [mu_rkv, mu_lora, wt["rwkv_w0"], wt["rwkv_a0"], wt["rwkv_k_k"], wt["rwkv_k_a"], w2p, a2p, g2]
    rw = fn_fwd("rwkv_pre", _rwkv_pre, nt, rw_pre_t, rw_pre_f, [(t, tr, w, F32)] * 7)
    r_, lw_, k2_, v_, nkk_, b_, gate_ = rw
    rw_seq = [(a, None) for a in (r_, lw_, k2_, v_, nkk_, b_)]

    def rw_fn(sv, cv, hts, ids):
        return _rwkv_chunks([(*s, ht) for s, ht in zip(sv, hts)])

    yr_scan, rw_states = scan_fwd("rwkv_scan", rw_fn, RWKV_CHUNK, rw_seq, [], n_pairs, rw_ppb)
    rw_post_t = [(a, tr, w, 0) for a in (yr_scan, r_, k2_, v_, gate_)]
    rw_post_f = [r_k, wt["rwkv_ln_w"], wt["rwkv_ln_b"]]
    y_rwkv, y_rwkv_t = fn_fwd("rwkv_post", _rwkv_post, nt, rw_post_t, rw_post_f, [(t, tr, w, BF16)], (0,))

    ymix = jnp.concatenate([y_ssd, y_rwkv], axis=1)
    ymix_t = jnp.concatenate([y_ssd_t, y_rwkv_t], axis=0)
    w_out = big.get("w_out", ymix)
    x1 = matmul("out_proj", ymix, w_out, resid=x)

    h2, h2t = norm_fwd("norm_x", x1, wt["norm_x_g"], tr)
    mrows = mem.shape[0]
    mn, mnt = norm_fwd("norm_mem", mem, wt["norm_mem_g"], mrows)
    wq, wk, wv, wo = [big.get(nm, ymix) for nm in ("xattn_wq", "xattn_wk", "xattn_wv", "xattn_wo")]
    q = matmul("xattn_q", h2, wq)
    kx = matmul("xattn_k", mn, wk)
    vx = matmul("xattn_v", mn, wv)
    ao, aot = fn_fwd("xattn_core", _attn, nt, [(q, tr, d, 0)], [kx, vx], [(t, tr, d, BF16)], (0,))
    x2 = matmul("xattn_o", ao, wo, resid=x1)

    h3, h3t = norm_fwd("norm_ffn", x2, wt["norm_ffn_g"], tr)
    w1, w2 = big.get("ffn_w1", h3), big.get("ffn_w2", h3)
    a1 = matmul("ffn_up", h3, w1, out_dtype=BF16)
    dff = a1.shape[1]
    f1, f1t = fn_fwd("ffn_act", _relu2, nt, [(a1, tr, dff, 0)], [], [(t, tr, dff, BF16)], (0,))
    x3 = matmul("ffn_down", f1, w2, resid=x2)

    dx3, g_final, loss_tile = loss_head(x3, tgt, wt["final_norm_g"].reshape(1, d), tr)

    grads = {"final_norm_g": g_final.reshape(d)}
    dx3b = dx3.astype(BF16)
    grads["ffn_w2"] = matmul("ffn_down_dw", f1t, dx3b)
    df1 = matmul("ffn_down_dx", dx3b, w2, tb=True, out_dtype=BF16)
    (da1,), _ = fn_bwd("ffn_act_bwd", _relu2, nt, [(a1, tr, dff, 0)], [], [(df1, tr, dff, 0)], lambda c: [c[0].astype(F32)],
                       [(t, tr, dff, BF16)])
    grads["ffn_w1"] = matmul("ffn_up_dw", h3t, da1)
    dh3 = reducer.launch(0, grads, matmul("ffn_up_dx", da1, w1, tb=True))
    dx2, grads["norm_ffn_g"] = norm_bwd("norm_ffn_bwd", x2, wt["norm_ffn_g"], dh3, dx3, tr)

    dx2b = dx2.astype(BF16)
    grads["xattn_wo"] = matmul("xattn_o_dw", aot, dx2b)
    dao = matmul("xattn_o_dx", dx2b, wo, tb=True)
    (dq,), (dkx, dvx) = fn_bwd("xattn_core_bwd", _attn, nt, [(q, tr, d, 0)], [kx, vx], [(dao, tr, d, 0)], lambda c: c,
                               [(t, tr, d, BF16)])
    grads["xattn_wq"] = matmul("xattn_q_dw", h2t, dq)
    dh2 = matmul("xattn_q_dx", dq, wq, tb=True)
    dkb, dvb = dkx.astype(BF16), dvx.astype(BF16)
    grads["xattn_wk"] = matmul("xattn_k_dw", mnt, dkb)
    grads["xattn_wv"] = matmul("xattn_v_dw", mnt, dvb)
    dmn = matmul("xattn_k_dx", dkb, wk, tb=True)
    dmn = matmul("xattn_v_dx", dvb, wv, tb=True, resid=dmn)
    _, grads["norm_mem_g"] = norm_bwd("norm_mem_bwd", mem, wt["norm_mem_g"], dmn, None, mrows)
    dx1, grads["norm_x_g"] = norm_bwd("norm_x_bwd", x1, wt["norm_x_g"], dh2, dx2, tr)

    dx1b = dx1.astype(BF16)
    grads["w_out"] = matmul("out_proj_dw", ymix_t, dx1b)
    dymix = reducer.launch(1, grads, matmul("out_proj_dx", dx1b, w_out, tb=True))

    (dyr, dr1, dk1, dv1, dgate), (g_rk, grads["rwkv_ln_w"], grads["rwkv_ln_b"]) = fn_bwd(
        "rwkv_post_bwd", _rwkv_post, nt, rw_post_t, rw_post_f, [(dymix, tr, w, 1)], lambda c: c, [(t, tr, w, F32)] * 5)
    grads["rwkv_r_k"] = g_rk.reshape(wt["rwkv_r_k"].shape)
    (dr2, dlw, dk2, dv2, dnkk, db), _ = scan_bwd("rwkv_scan_bwd", rw_fn, RWKV_CHUNK, rw_seq, [], rw_states, dyr, n_pairs, rw_ppb)
    rw_ct = [(a, tr, w, 0) for a in (dr1, dr2, dlw, dk1, dk2, dv1, dv2, dnkk, db, dgate)]

    def rw_ct_fn(c):
        return (c[0] + c[1], c[2], c[3] + c[4], c[5] + c[6], c[7], c[8], c[9])

    (durkv, dulora, dhrkv, dhlora), rw_pg = fn_bwd(
        "rwkv_pre_bwd", _rwkv_pre, nt, rw_pre_t, rw_pre_f, rw_ct, rw_ct_fn,
        [(t, tr, 3 * w, F32), (t, tr, lora_w, F32), (nt * HALO, HALO, 3 * w, F32), (nt * HALO, HALO, lora_w, F32)])
    durkv = _unhalo(durkv, dhrkv, tr)
    dulora = _unhalo(dulora, dhlora, tr)
    g_mu_rkv, g_mu_lora, grads["rwkv_w0"], grads["rwkv_a0"], grads["rwkv_k_k"], grads["rwkv_k_a"], g_w2p, g_a2p, grads["rwkv_g2"] = rw_pg
    grads["rwkv_mu"] = jnp.concatenate([g_mu_rkv, g_mu_lora[:, :dr], g_mu_lora[:, LANES:LANES + ar], g_mu_lora[:, 2 * LANES:]], axis=1)
    grads["rwkv_w2"] = g_w2p[:dr]
    grads["rwkv_a2"] = g_a2p[:ar]

    (dys, dxs1, dz), (g_d, grads["ssd_norm_g"]) = fn_bwd(
        "ssd_post_bwd", _ssd_post, nt, ssd_post_t, ssd_post_f, [(dymix, tr, w, 0)], lambda c: c, [(t, tr, w, F32)] * 3)
    grads["ssd_d"] = g_d[:, :nh]
    (dxs2, dbp, dcp, ddtp), (g_alog,) = scan_bwd("ssd_scan_bwd", ssd_fn, SSD_CHUNK, ssd_seq, [a_log], ssd_states, dys, n_pairs, ssd_ppb)
    grads["ssd_a_log"] = g_alog[:, :nh]
    ssd_ct = [(dxs1, tr, w, 0), (dxs2, tr, w, 0), (dbp, tr, w, 0), (dcp, tr, w, 0), (ddtp, tr, w, 0)]

    def ssd_ct_fn(c):
        def group_sum(a):
            parts = []
            for gi in range(SSD_GROUPS):
                s = a[:, gi * ppg * LANES:(gi * ppg + 1) * LANES]
                for j in range(1, ppg):
                    s = s + a[:, (gi * ppg + j) * LANES:(gi * ppg + j + 1) * LANES]
                parts.append(s)
            return parts
        ddt = c[4][:, :LANES]
        for j in range(1, n_pairs):
            ddt = ddt + c[4][:, j * LANES:(j + 1) * LANES]
        return (jnp.concatenate([c[0] + c[1]] + group_sum(c[2]) + group_sum(c[3]), axis=1), ddt)

    (dxbc, dhxbc, ddtraw), ssd_pg = fn_bwd(
        "ssd_pre_bwd", _ssd_pre, nt, ssd_pre_t, ssd_pre_f, ssd_ct, ssd_ct_fn,
        [(t, tr, conv_dim, F32), (nt * HALO, HALO, conv_dim, F32), (t, tr, LANES, F32)])
    dxbc = _unhalo(dxbc, dhxbc, tr)
    grads["ssd_conv_w"] = jnp.concatenate(ssd_pg[:SSD_CONV], axis=0)
    grads["ssd_conv_b"] = ssd_pg[SSD_CONV]
    grads["ssd_dt_bias"] = ssd_pg[SSD_CONV + 1][:, :nh]

    du = jnp.concatenate([dz, dxbc, ddtraw, durkv, dulora], axis=1).astype(BF16)
    g_perm = matmul("in_proj_dw", h1t, du)
    grads["w_in"] = jnp.concatenate([seg_cols(g_perm, nm, segs[nm][1]) for nm in order], axis=1)
    dh1 = matmul("in_proj_dx", du, w_perm, tb=True)
    dh1 = reducer.launch(2, grads, dh1)
    grad_x, grads["norm_mix_g"] = norm_bwd("norm_mix_bwd", x, wt["norm_mix_g"], dh1, dx1, tr)
    return loss_tile, grad_x, grads


def _pack(arrs):
    flat = jnp.concatenate([a.reshape(-1) for a in arrs])
    n = flat.shape[0]
    rows = -(-n // (8 * LANES)) * 8
    return jnp.pad(flat, (0, rows * LANES - n)).reshape(rows, LANES)


def _unpack(packed, shapes):
    flat = packed.reshape(-1)
    out, o = [], 0
    for s in shapes:
        n = math.prod(s)
        out.append(flat[o:o + n].reshape(s))
        o += n
    return out


def _as2d(a):
    return a.reshape(-1, a.shape[-1])


class _GatheredWeights:
    def __init__(self, shard2d, q, c):
        self.shard2d, self.q, self.c = shard2d, q, c
        self.raw, self.ready = {}, {}

    def start(self, gi, after):
        shards = [self.shard2d[n].astype(BF16) for n in GATHER_GROUPS[gi]]
        if after is not None:
            shards, _ = lax.optimization_barrier((shards, after))
        gathered = gather_two_level("gather_weights_%d" % gi, shards, gi + 1)
        for n, sh, g in zip(GATHER_GROUPS[gi], shards, gathered):
            self.raw[n] = (sh, g)

    def get(self, name, after):
        if name not in self.ready:
            sh, g = self.raw[name]
            if after is not None:
                g, _ = lax.optimization_barrier((g, after))
            hr = sh.shape[0] // 2
            own = lax.dynamic_slice_in_dim(sh, self.c * hr, hr, axis=0)
            g = lax.dynamic_update_slice(g, own[None, None], (self.q, self.c, 0, 0))
            self.ready[name] = _from_slots(g.reshape(4, 2 * hr, g.shape[3]), BIG_AXIS[name])
        return self.ready[name]


class _GradReducer:
    def __init__(self, q, c):
        self.q, self.c = q, c
        self.pending = {}

    def launch(self, gi, grads, nxt):
        names = REDUCE_GROUPS[gi]
        kept, sent = [], []
        for n in names:
            s = _to_slots(grads[n], REDUCE_AXIS[n])
            s = s.reshape(4, 2, s.shape[1] // 2, s.shape[2])
            kept.append(lax.dynamic_index_in_dim(s, self.c, axis=1, keepdims=False))
            sent.append(lax.dynamic_index_in_dim(s, 1 - self.c, axis=1, keepdims=False).astype(BF16))
        got = core_swap("swap_halves_%d" % gi, sent)
        parts = []
        for n, k, g in zip(names, kept, got):
            _, hr, cols = k.shape
            tr = _pick(4 * hr, (256, 128, 64, 32, 16))
            (part,), _ = row_call("chip_sum_" + n, lambda tv, fv: ([tv[0] + tv[1].astype(F32)], []), 4 * hr // tr,
                                  [(k.reshape(4 * hr, cols), tr, cols, 0), (g.reshape(4 * hr, cols), tr, cols, 0)], [],
                                  [(4 * hr, tr, cols, BF16)], [])
            parts.append(part.reshape(4, hr, cols))
        parts, nxt = lax.optimization_barrier((parts, nxt))
        self.pending[gi] = (parts, scatter_slots("scatter_grads_%d" % gi, parts, len(GATHER_GROUPS) + 1 + gi))
        return nxt

    def finish(self, gi, after):
        names = REDUCE_GROUPS[gi]
        parts, slots = self.pending[gi]
        if after is not None:
            slots, _ = lax.optimization_barrier((slots, after))
        halves = []
        for n, p, s in zip(names, parts, slots):
            own = lax.dynamic_index_in_dim(p, self.q, axis=0, keepdims=True)
            halves.append(sum_slots("sum_" + n, lax.dynamic_update_slice(s, own, (self.q, 0, 0))))
        others = core_swap("swap_reduced_%d" % gi, halves)
        lo = [jnp.where(self.c == 0, mine, other) for mine, other in zip(halves, others)]
        hi = [jnp.where(self.c == 0, other, mine) for mine, other in zip(halves, others)]
        return {n: jnp.concatenate([l, h], axis=0) for n, l, h in zip(names, lo, hi)}


def _step(a):
    x, mem, tgt = a["x"][0], a["mem"][0], a["loss_target"][0]
    q = 2 * lax.axis_index("x") + lax.axis_index("y")

    shard2d = {n: _as2d(a[n][0]) for n in BIG}
    small_sh = {n: _as2d(a[n][0]) for n in SMALL_SHARDED}
    c = lax.axis_index("c")
    full = {}
    big = _GatheredWeights(shard2d, q, c)
    gathered = gather_shards("gather_small", [small_sh[n] for n in SMALL_SHARDED])
    for n, g in zip(SMALL_SHARDED, gathered):
        full[n] = _from_slots(g, 1)

    wt = {n: (a[n] if a[n].ndim <= 2 else a[n][0]) for n in WEIGHTS if n not in BIG and n not in SMALL_SHARDED}
    for n in SMALL_SHARDED:
        wt[n] = small_sh[n]
    reducer = _GradReducer(q, c)
    loss_tile, grad_x, grads = _local_grads(x, mem, tgt, wt, full, big, reducer)

    shards = dict(shard2d)
    shards.update({n: small_sh[n] for n in REDUCED if n not in BIG})
    out = {}
    for gi in range(len(REDUCE_GROUPS)):
        for n, gsum in reducer.finish(gi, grad_x if gi + 1 < len(REDUCE_GROUPS) else None).items():
            g, dlt, mn, vn = adamw("adamw_" + n, shards[n], _as2d(a["m_" + n][0]), _as2d(a["v_" + n][0]), [gsum])
            for key, val in (("grad_", g), ("delta_", dlt), ("new_m_", mn), ("new_v_", vn)):
                out[key + n] = val.reshape(a[n].shape)

    small = [n for n in WEIGHTS if n not in REDUCED]
    red = _unpack(all_reduce_small("all_reduce_small", _pack([grads[n] for n in small])), [grads[n].shape for n in small])
    g_loc = {}
    for n, g in zip(small, red):
        if n in SMALL_SHARDED:
            cols = g.shape[1] // 4
            g = lax.dynamic_slice_in_dim(g, q * cols, cols, axis=1)
        g_loc[n] = g.reshape(a[n].shape)
    res = adamw("adamw_small", *[_pack([src[n] for n in small]) for src in
                                 ({n: a[n] for n in small}, {n: a["m_" + n] for n in small}, {n: a["v_" + n] for n in small})],
                [_pack([g_loc[n] for n in small])])
    shapes = [a[n].shape for n in small]
    for key, packed in zip(("grad_", "delta_", "new_m_", "new_v_"), res):
        for n, val in zip(small, _unpack(packed, shapes)):
            out[key + n] = val

    loss = lax.psum(loss_tile[0, 0], ("x", "y", "c"))
    ordered = [loss, grad_x.reshape(a["x"].shape)]
    for key in ("grad_", "delta_", "new_m_", "new_v_"):
        ordered += [out[key + n] for n in WEIGHTS]
    return tuple(ordered)


def kernel(x, mem, norm_mix_g, w_in, ssd_conv_w, ssd_conv_b, ssd_dt_bias, ssd_a_log, ssd_d, ssd_norm_g, rwkv_mu, rwkv_w0, rwkv_w2, rwkv_a0, rwkv_a2, rwkv_g2, rwkv_k_k, rwkv_k_a, rwkv_r_k, rwkv_ln_w, rwkv_ln_b, w_out, norm_x_g, norm_mem_g, xattn_wq, xattn_wk, xattn_wv, xattn_wo, norm_ffn_g, ffn_w1, ffn_w2, final_norm_g, loss_target, m_norm_mix_g, m_w_in, m_ssd_conv_w, m_ssd_conv_b, m_ssd_dt_bias, m_ssd_a_log, m_ssd_d, m_ssd_norm_g, m_rwkv_mu, m_rwkv_w0, m_rwkv_w2, m_rwkv_a0, m_rwkv_a2, m_rwkv_g2, m_rwkv_k_k, m_rwkv_k_a, m_rwkv_r_k, m_rwkv_ln_w, m_rwkv_ln_b, m_w_out, m_norm_x_g, m_norm_mem_g, m_xattn_wq, m_xattn_wk, m_xattn_wv, m_xattn_wo, m_norm_ffn_g, m_ffn_w1, m_ffn_w2, m_final_norm_g, v_norm_mix_g, v_w_in, v_ssd_conv_w, v_ssd_conv_b, v_ssd_dt_bias, v_ssd_a_log, v_ssd_d, v_ssd_norm_g, v_rwkv_mu, v_rwkv_w0, v_rwkv_w2, v_rwkv_a0, v_rwkv_a2, v_rwkv_g2, v_rwkv_k_k, v_rwkv_k_a, v_rwkv_r_k, v_rwkv_ln_w, v_rwkv_ln_b, v_w_out, v_norm_x_g, v_norm_mem_g, v_xattn_wq, v_xattn_wk, v_xattn_wv, v_xattn_wo, v_norm_ffn_g, v_ffn_w1, v_ffn_w2, v_final_norm_g):
    return _step(dict(locals()))
```

```python
import functools
import math

import jax
import jax.numpy as jnp
from jax import lax
from jax.experimental import pallas as pl
from jax.experimental.pallas import tpu as pltpu
from jax.experimental.pallas import tpu_sc as plsc

F32 = jnp.float32
BF16 = jnp.bfloat16
HIGHEST = lax.Precision.HIGHEST
MESH_ID = pl.DeviceIdType.MESH

NORM_EPS = 1e-6
RWKV_LN_EPS = 64e-5
HEAD_DIM = 64
PAIR = 2 * HEAD_DIM
LANES = 128
SSD_STATE = 128
SSD_CHUNK = 128
SSD_GROUPS = 2
SSD_CONV = 4
RWKV_CHUNK = 64
HALO = 8
ROW_TILE = 128
PAIRS_PER_STEP = 4
XATTN_HEADS = 4
RWKV_PASSES = 1
VMEM_LIMIT = 56 * 1024 * 1024
MATMUL_VMEM = 40 * 1024 * 1024

ADAM_LR = 0.001
ADAM_B1 = 0.9
ADAM_B2 = 0.999
ADAM_EPS = 1e-08
ADAM_WD = 0.01
ADAM_STEP = 10


def _dims(ca, cb):
    return (((ca,), (cb,)), ((), ()))


def _split_bf16(a):
    hi = a.astype(BF16)
    lo = (a - hi.astype(F32)).astype(BF16)
    return hi, lo


def _mm_impl(a, b, ca, cb, passes):
    dn = _dims(ca, cb)
    if passes == 1:
        return lax.dot_general(a.astype(BF16), b.astype(BF16), dn, preferred_element_type=F32)
    ah, al = _split_bf16(a)
    bh, bl = _split_bf16(b)
    out = lax.dot_general(ah, bh, dn, preferred_element_type=F32)
    out = out + lax.dot_general(ah, bl, dn, preferred_element_type=F32)
    return out + lax.dot_general(al, bh, dn, preferred_element_type=F32)


@functools.partial(jax.custom_vjp, nondiff_argnums=(2, 3, 4))
def mm(a, b, ca, cb, passes):
    return _mm_impl(a, b, ca, cb, passes)


def _mm_fwd(a, b, ca, cb, passes):
    return _mm_impl(a, b, ca, cb, passes), (a, b)


def _mm_bwd(ca, cb, passes, res, g):
    a, b = res
    da = mm(g, b, 1, 1 - cb, passes) if ca == 1 else mm(b, g, 1 - cb, 1, passes)
    db = mm(a, g, 1 - ca, 0, passes) if cb == 0 else mm(g, a, 0, 1 - ca, passes)
    return da, db


mm.defvjp(_mm_fwd, _mm_bwd)


def _dot_exact(a, b):
    return lax.dot_general(a, b, _dims(1, 0), precision=HIGHEST, preferred_element_type=F32)


def _iota(shape, dim):
    return lax.broadcasted_iota(jnp.int32, shape, dim)


def _sigmoid(x):
    return 1.0 / (1.0 + jnp.exp(-x))


def _silu(x):
    return x * _sigmoid(x)


def _softplus(x):
    return jnp.maximum(x, 0.0) + jnp.log(1.0 + jnp.exp(-jnp.abs(x)))


def _rms(x, g):
    return x * lax.rsqrt(jnp.mean(x * x, axis=-1, keepdims=True) + NORM_EPS) * g


def _select_mm(x, sel):
    hi = x.astype(BF16)
    r1 = x - hi.astype(F32)
    mid = r1.astype(BF16)
    lo = (r1 - mid.astype(F32)).astype(BF16)
    dn = _dims(1, 0)
    out = lax.dot_general(hi, sel, dn, preferred_element_type=F32)
    out = out + lax.dot_general(mid, sel, dn, preferred_element_type=F32)
    return out + lax.dot_general(lo, sel, dn, preferred_element_type=F32)


def _head_sum_impl(x, n):
    sel = (_iota((n, LANES), 0) // HEAD_DIM == _iota((n, LANES), 1)).astype(BF16)
    return _select_mm(x, sel)


def _head_expand_impl(s, n):
    sel = (_iota((LANES, n), 1) // HEAD_DIM == _iota((LANES, n), 0)).astype(BF16)
    return _select_mm(s, sel)


@functools.partial(jax.custom_vjp, nondiff_argnums=(1,))
def _head_sum_n(x, n):
    return _head_sum_impl(x, n)


@functools.partial(jax.custom_vjp, nondiff_argnums=(1,))
def _head_expand(s, n):
    return _head_expand_impl(s, n)


_head_sum_n.defvjp(lambda x, n: (_head_sum_impl(x, n), None), lambda n, _, g: (_head_expand(g, n),))
_head_expand.defvjp(lambda s, n: (_head_expand_impl(s, n), None), lambda n, _, g: (_head_sum_n(g, n),))


def _head_sum(x):
    return _head_sum_n(x, x.shape[1])


def _row_vector_expand(v, n):
    v8 = jnp.broadcast_to(v, (8, LANES))
    return jnp.sum(_head_expand(v8, n), axis=0, keepdims=True) * 0.125


def _shift_rows_impl(u, halo, s):
    rolled = pltpu.roll(u, s, 0)
    top = jnp.where(_iota((HALO, 1), 0) < s, pltpu.roll(halo, s, 0), rolled[:HALO])
    return jnp.concatenate([top, rolled[HALO:]], axis=0)


@functools.partial(jax.custom_vjp, nondiff_argnums=(2,))
def _shift_rows(u, halo, s):
    return _shift_rows_impl(u, halo, s)


def _shift_rows_bwd(s, _, g):
    tr = g.shape[0]
    rolled = pltpu.roll(g, tr - s, 0)
    hrow = _iota((HALO, 1), 0)
    bottom = jnp.where(hrow < HALO - s, rolled[tr - HALO:], 0.0)
    dhalo = jnp.where(hrow >= HALO - s, pltpu.roll(g[:HALO], HALO - s, 0), 0.0)
    return jnp.concatenate([rolled[:tr - HALO], bottom], axis=0), dhalo


_shift_rows.defvjp(lambda u, halo, s: (_shift_rows_impl(u, halo, s), None), _shift_rows_bwd)


def _params(sem):
    return pltpu.CompilerParams(dimension_semantics=sem, vmem_limit_bytes=VMEM_LIMIT)


def row_call(name, body, n_tiles, tiled, full, out_tiled, out_acc, transposed=()):
    nt, nf, na = len(tiled), len(full), len(out_acc)
    n_plain = len(out_tiled)
    no = n_plain + len(transposed)

    def kern(*refs):
        tv = [r[...] for r in refs[:nt]]
        fv = [r[...] for r in refs[nt:nt + nf]]
        outs, accs = body(tv, fv)
        for r, v in zip(refs[nt + nf:nt + nf + n_plain], outs):
            r[...] = v.astype(r.dtype)
        for r, idx in zip(refs[nt + nf + n_plain:nt + nf + no], transposed):
            r[...] = outs[idx].astype(F32).T.astype(r.dtype)
        if na:
            a_refs = refs[nt + nf + no:]
            first = pl.program_id(0) == 0

            @pl.when(first)
            def _():
                for r, v in zip(a_refs, accs):
                    r[...] = v

            @pl.when(jnp.logical_not(first))
            def _():
                for r, v in zip(a_refs, accs):
                    r[...] += v

    in_specs = [pl.BlockSpec((rt, w), functools.partial(lambda i, cb: (i, cb), cb=cb)) for (_, rt, w, cb) in tiled]
    in_specs += [pl.BlockSpec(a.shape, lambda i: (0, 0)) for a in full]
    out_specs = [pl.BlockSpec((rt, w), lambda i: (i, 0)) for (_, rt, w, _) in out_tiled]
    out_specs += [pl.BlockSpec((out_tiled[idx][2], out_tiled[idx][1]), lambda i: (0, i)) for idx in transposed]
    out_specs += [pl.BlockSpec(s, lambda i: (0, 0)) for s in out_acc]
    out_shape = [jax.ShapeDtypeStruct((rows, w), dt) for (rows, _, w, dt) in out_tiled]
    out_shape += [jax.ShapeDtypeStruct((out_tiled[idx][2], out_tiled[idx][0]), BF16) for idx in transposed]
    out_shape += [jax.ShapeDtypeStruct(s, F32) for s in out_acc]
    res = pl.pallas_call(
        kern, name=name, grid=(n_tiles,), in_specs=in_specs, out_specs=out_specs, out_shape=out_shape,
        compiler_params=_params(("arbitrary",)),
    )(*[t[0] for t in tiled], *full)
    return list(res[:no]), list(res[no:])


def _pick(dim, cands):
    for c in cands:
        if dim % c == 0:
            return c
    return dim


def matmul(name, a, b, tb=False, resid=None, out_dtype=F32):
    m, k = a.shape
    n = b.shape[0] if tb else b.shape[1]
    has_resid = resid is not None
    out_bytes = jnp.dtype(out_dtype).itemsize
    sizes = (2048, 1024, 896, 768, 512, 384, 256, 128)
    tm = _pick(m, sizes[1:])
    tn = _pick(n, sizes[1:])

    def vmem_bytes(tk):
        return 2 * 2 * tk * (tm + tn) + tm * tn * (2 * out_bytes + 4 + (8 if has_resid else 0))

    tk = next((c for c in sizes if k % c == 0 and vmem_bytes(c) <= MATMUL_VMEM), LANES)
    nk = k // tk

    def kern(*refs):
        a_ref, b_ref = refs[0], refs[1]
        o_ref, acc = refs[-2], refs[-1]
        kk = pl.program_id(2)
        part = lax.dot_general(a_ref[...], b_ref[...], _dims(1, 1 if tb else 0), preferred_element_type=F32)

        def finish(out):
            if has_resid:
                out = out + refs[2][...]
            o_ref[...] = out.astype(o_ref.dtype)

        if nk == 1:
            finish(part)
            return

        @pl.when(kk == 0)
        def _():
            acc[...] = part

        @pl.when(jnp.logical_and(kk > 0, kk < nk - 1))
        def _():
            acc[...] += part

        @pl.when(kk == nk - 1)
        def _():
            finish(acc[...] + part)

    in_specs = [pl.BlockSpec((tm, tk), lambda i, j, kk: (i, kk))]
    if tb:
        in_specs.append(pl.BlockSpec((tn, tk), lambda i, j, kk: (j, kk)))
    else:
        in_specs.append(pl.BlockSpec((tk, tn), lambda i, j, kk: (kk, j)))
    args = [a, b]
    if has_resid:
        in_specs.append(pl.BlockSpec((tm, tn), lambda i, j, kk: (i, j)))
        args.append(resid)
    return pl.pallas_call(
        kern, name=name, grid=(m // tm, n // tn, nk), in_specs=in_specs,
        out_specs=pl.BlockSpec((tm, tn), lambda i, j, kk: (i, j)),
        out_shape=jax.ShapeDtypeStruct((m, n), out_dtype),
        scratch_shapes=[pltpu.VMEM((tm, tn), F32)],
        compiler_params=_params(("parallel", "parallel", "arbitrary")),
    )(*args)


def norm_fwd(name, x, g, tr):
    def body(tv, fv):
        return [_rms(tv[0], fv[0])], []
    rows, d = x.shape
    (h, ht), _ = row_call(name, body, rows // tr, [(x, tr, d, 0)], [g], [(rows, tr, d, BF16)], [], transposed=(0,))
    return h, ht


def norm_bwd(name, x, g, dh, extra, tr):
    def body(tv, fv):
        _, vjp = jax.vjp(_rms, tv[0], fv[0])
        dx, dg = vjp(tv[1])
        if extra is not None:
            dx = dx + tv[2]
        return [dx, dx], [dg]
    rows, d = x.shape
    tiled = [(x, tr, d, 0), (dh, tr, d, 0)] + ([(extra, tr, d, 0)] if extra is not None else [])
    (dx, dxb), (dg,) = row_call(name, body, rows // tr, tiled, [g], [(rows, tr, d, F32), (rows, tr, d, BF16)], [g.shape])
    return dx, dxb, dg


def _ssd_pre(xbc, halo, dtraw, w0, w1, w2, w3, cb, dtb):
    y = w3 * xbc + w2 * _shift_rows(xbc, halo, 1) + w1 * _shift_rows(xbc, halo, 2) + w0 * _shift_rows(xbc, halo, 3) + cb
    return _silu(y), _softplus(dtraw + dtb)


def _ssd_post(ys, xs, z, dskip, ng):
    w = ys.shape[1]
    y = (ys + xs * _row_vector_expand(dskip, w)) * _silu(z)
    gw = w // SSD_GROUPS
    parts = []
    for gi in range(SSD_GROUPS):
        yg = y[:, gi * gw:(gi + 1) * gw]
        parts.append(yg * lax.rsqrt(jnp.mean(yg * yg, axis=-1, keepdims=True) + NORM_EPS))
    return jnp.concatenate(parts, axis=1) * ng


def _rwkv_pre(urkv, ulora, hrkv, hlora, mu_rkv, mu_lora, w0, a0, kkw, kaw, w2p, a2p, g2):
    w = w0.shape[1]
    urkv = urkv + (_shift_rows(urkv, hrkv, 1) - urkv) * mu_rkv
    ulora = ulora + (_shift_rows(ulora, hlora, 1) - ulora) * mu_lora
    r, k, v = urkv[:, :w], urkv[:, w:2 * w], urkv[:, 2 * w:]
    pw, pa, pg = ulora[:, :LANES], ulora[:, LANES:2 * LANES], ulora[:, 2 * LANES:]
    w_log = -_softplus(-(w0 + mm(jnp.tanh(pw), w2p, 1, 0, 1))) - 0.5
    lw = -jnp.exp(w_log)
    iclr = _sigmoid(a0 + mm(pa, a2p, 1, 0, 1))
    gate = mm(_sigmoid(pg), g2, 1, 0, 1)
    kk = k * kkw
    kk = kk / jnp.maximum(jnp.sqrt(_head_expand(_head_sum(kk * kk), w)), 1e-12)
    k2 = k * (1.0 + (iclr - 1.0) * kaw)
    return r, lw, k2, v, -kk, kk * iclr, gate


def _rwkv_post(ys, r, k2, v, gate, rk, lnw, lnb):
    w = ys.shape[1]
    inv = 1.0 / HEAD_DIM
    mean = _head_expand(_head_sum(ys), w) * inv
    d = ys - mean
    var = _head_expand(_head_sum(d * d), w) * inv
    yn = d * lax.rsqrt(var + RWKV_LN_EPS) * lnw + lnb
    bonus = _head_expand(_head_sum(r * k2 * rk), w) * v
    return (yn + bonus) * gate


def _attn(q, k, v):
    d = q.shape[1]
    hd = d // XATTN_HEADS
    outs = []
    for h in range(XATTN_HEADS):
        sl = slice(h * hd, (h + 1) * hd)
        s = mm(q[:, sl], k[:, sl], 1, 1, 1) * (hd ** -0.5)
        s = s - jnp.max(s, axis=-1, keepdims=True)
        p = jnp.exp(s)
        p = p / jnp.sum(p, axis=-1, keepdims=True)
        outs.append(mm(p, v[:, sl], 1, 0, 1))
    return jnp.concatenate(outs, axis=1)


def _relu2(a):
    return jnp.square(jnp.maximum(a.astype(F32), 0.0))


def fn_fwd(name, fn, n_tiles, tiled, full, out_tiled, transposed=()):
    def body(tv, fv):
        outs = fn(*tv, *fv)
        return (list(outs) if isinstance(outs, (tuple, list)) else [outs]), []
    outs, _ = row_call(name, body, n_tiles, tiled, full, out_tiled, [], transposed)
    return outs


def fn_bwd(name, fn, n_tiles, tiled, full, cts, ct_fn, out_tiled):
    nt = len(tiled)

    def body(tv, fv):
        outs, vjp = jax.vjp(fn, *tv[:nt], *fv)
        ct = ct_fn(tv[nt:])
        grads = vjp(tuple(ct) if isinstance(outs, (tuple, list)) else ct[0])
        return list(grads[:nt]), list(grads[nt:])
    return row_call(name, body, n_tiles, tiled + cts, full, out_tiled, [f.shape for f in full])


def _ssd_chunk(xs, bm, cm, dt_all, a_log, ht, p):
    q = xs.shape[0]
    lane = _iota((1, LANES), 1)
    row = _iota((q, 1), 0)
    tril = _iota((q, q), 0) >= _iota((q, q), 1)
    half = lane < HEAD_DIM
    da = dt_all * (-jnp.exp(a_log))
    cs = _dot_exact(tril.astype(F32), da)

    def col(mat, h):
        return jnp.sum(jnp.where(lane == h, mat, 0.0), axis=1, keepdims=True)

    cs0, cs1 = col(cs, 2 * p), col(cs, 2 * p + 1)
    xdt = xs * jnp.where(half, col(dt_all, 2 * p), col(dt_all, 2 * p + 1))
    csx = jnp.where(half, cs0, cs1)
    last = jnp.sum(jnp.where(row == q - 1, csx, 0.0), axis=0, keepdims=True)
    cb = mm(cm, bm, 1, 1, 1)
    y = mm(cm, ht, 1, 0, 1) * jnp.exp(csx)
    for csh, hm in ((cs0, half), (cs1, jnp.logical_not(half))):
        csl = jnp.broadcast_to(csh, (q, q))
        seg = csl - csl.T
        lmat = jnp.where(tril, jnp.exp(jnp.where(tril, seg, 0.0)), 0.0)
        y = y + jnp.where(hm, mm(cb * lmat, xdt, 1, 0, 1), 0.0)
    st = mm(bm, xdt * jnp.exp(last - csx), 0, 0, 1)
    return y, ht * jnp.exp(last) + st


def _rwkv_chunks(pairs):
    c = pairs[0][0].shape[0]
    ps = RWKV_PASSES
    lane = _iota((1, LANES), 1)
    row = _iota((c, 1), 0)
    ri, ci = _iota((c, c), 0), _iota((c, c), 1)
    tril_i, tril_s = ri >= ci, ri > ci
    eye = (ri == ci).astype(F32)
    half = lane < HEAD_DIM
    halves = (half, jnp.logical_not(half))
    bd = (_iota((LANES, LANES), 0) < HEAD_DIM) == (_iota((LANES, LANES), 1) < HEAD_DIM)
    tri = tril_i.astype(F32)
    n = len(pairs)
    heads = [(j, hm) for j in range(n) for hm in halves]

    cum = [_dot_exact(tri, p[1]) for p in pairs]
    at = [p[4] * jnp.exp(cm - p[1]) for p, cm in zip(pairs, cum)]
    en = [jnp.exp(-cm) for cm in cum]
    bt = [p[5] * e for p, e in zip(pairs, en)]
    kt = [p[2] * e for p, e in zip(pairs, en)]
    rt = [p[0] * jnp.exp(cm) for p, cm in zip(pairs, cum)]
    ah = [mm(at[j], pairs[j][6], 1, 1, ps) for j in range(n)]
    y = [mm(rt[j], pairs[j][6], 1, 1, ps) for j in range(n)]
    atm = [jnp.where(hm, at[j], 0.0) for j, hm in heads]
    rtm = [jnp.where(hm, rt[j], 0.0) for j, hm in heads]
    aab = [jnp.where(tril_s, mm(atm[i], bt[j], 1, 1, ps), 0.0) for i, (j, _) in enumerate(heads)]
    aak = [jnp.where(tril_s, mm(atm[i], kt[j], 1, 1, ps), 0.0) for i, (j, _) in enumerate(heads)]
    arb = [jnp.where(tril_i, mm(rtm[i], bt[j], 1, 1, ps), 0.0) for i, (j, _) in enumerate(heads)]
    ark = [jnp.where(tril_i, mm(rtm[i], kt[j], 1, 1, ps), 0.0) for i, (j, _) in enumerate(heads)]
    rhs = [ah[j] + mm(aak[i], pairs[j][3], 1, 0, ps) for i, (j, _) in enumerate(heads)]
    yv = [mm(ark[i], pairs[j][3], 1, 0, ps) for i, (j, _) in enumerate(heads)]
    tm = [eye + a_ for a_ in aab]
    pm = aab
    for _ in range(int(math.log2(c)) - 1):
        pm = [mm(p_, p_, 1, 0, ps) for p_ in pm]
        tm = [t_ + mm(t_, p_, 1, 0, ps) for t_, p_ in zip(tm, pm)]
    uh = [mm(tm[i], rhs[i], 1, 0, ps) for i in range(len(heads))]
    u = [jnp.where(half, uh[2 * j], uh[2 * j + 1]) for j in range(n)]
    yu = [mm(arb[i], u[j], 1, 0, ps) for i, (j, _) in enumerate(heads)]
    out = []
    for j in range(n):
        yj = y[j] + jnp.where(half, yu[2 * j] + yv[2 * j], yu[2 * j + 1] + yv[2 * j + 1])
        plast = jnp.sum(jnp.where(row == c - 1, cum[j], 0.0), axis=0, keepdims=True)
        upd = pairs[j][6] + mm(u[j], bt[j], 0, 0, ps) + mm(pairs[j][3], kt[j], 0, 0, ps)
        out.append((yj, jnp.where(bd, upd * jnp.exp(plast), 0.0)))
    return out


def _seq_spec(chunk, ppb, col, row_of):
    if col is None:
        return pl.BlockSpec((chunk, ppb * LANES), lambda pb, i: (row_of(i), pb))
    return pl.BlockSpec((chunk, LANES), lambda pb, i: (row_of(i), col(pb * ppb)))


def _pair_vals(refs, seq_in, j):
    return [r[...] if col is not None else r[:, j * LANES:(j + 1) * LANES] for r, (_, col) in zip(refs, seq_in)]


def scan_fwd(name, chunk_fn, chunk, seq_in, const_in, n_pairs, ppb):
    t = seq_in[0][0].shape[0]
    nc = t // chunk
    ns, ncst = len(seq_in), len(const_in)

    def kern(*refs):
        y_ref, st_ref, ht = refs[ns + ncst], refs[ns + ncst + 1], refs[ns + ncst + 2]

        @pl.when(pl.program_id(1) == 0)
        def _():
            ht[...] = jnp.zeros_like(ht)

        cv = [r[...] for r in refs[ns:ns + ncst]]
        h0 = [ht[j] for j in range(ppb)]
        for j in range(ppb):
            st_ref[j] = h0[j]
        sv = [_pair_vals(refs[:ns], seq_in, j) for j in range(ppb)]
        outs = chunk_fn(sv, cv, h0, [pl.program_id(0) * ppb + j for j in range(ppb)])
        for j, (y, hn) in enumerate(outs):
            y_ref[:, j * LANES:(j + 1) * LANES] = y
            ht[j] = hn

    in_specs = [_seq_spec(chunk, ppb, col, lambda i: i) for (_, col) in seq_in]
    in_specs += [pl.BlockSpec(a.shape, lambda pb, i: (0, 0)) for a in const_in]
    return pl.pallas_call(
        kern, name=name, grid=(n_pairs // ppb, nc), in_specs=in_specs,
        out_specs=[pl.BlockSpec((chunk, ppb * LANES), lambda pb, i: (i, pb)),
                   pl.BlockSpec((ppb, None, LANES, LANES), lambda pb, i: (pb, i, 0, 0))],
        out_shape=[jax.ShapeDtypeStruct((t, n_pairs * LANES), F32), jax.ShapeDtypeStruct((n_pairs, nc, LANES, LANES), F32)],
        scratch_shapes=[pltpu.VMEM((ppb, LANES, LANES), F32)],
        compiler_params=_params(("arbitrary", "arbitrary")),
    )(*[s[0] for s in seq_in], *const_in)


def scan_bwd(name, chunk_fn, chunk, seq_in, const_in, states, dy, n_pairs, ppb):
    t = dy.shape[0]
    nc = t // chunk
    ns, ncst = len(seq_in), len(const_in)

    def kern(*refs):
        seq_refs, cst_refs = refs[:ns], refs[ns:ns + ncst]
        st_ref, dy_ref = refs[ns + ncst], refs[ns + ncst + 1]
        o = ns + ncst + 2
        dseq_refs, dcst_refs, dht = refs[o:o + ns], refs[o + ns:o + ns + ncst], refs[o + ns + ncst]
        pb, i = pl.program_id(0), pl.program_id(1)

        @pl.when(i == 0)
        def _():
            dht[...] = jnp.zeros_like(dht)

        ids = [pb * ppb + j for j in range(ppb)]
        lanes = [slice(j * LANES, (j + 1) * LANES) for j in range(ppb)]

        def fn(*flat):
            sv = [list(flat[j * ns:(j + 1) * ns]) for j in range(ppb)]
            outs = chunk_fn(sv, list(flat[ppb * ns:ppb * ns + ncst]), list(flat[ppb * ns + ncst:]), ids)
            return tuple(y for y, _ in outs), tuple(h for _, h in outs)

        flat_in = [v for j in range(ppb) for v in _pair_vals(seq_refs, seq_in, j)]
        flat_in += [r[...] for r in cst_refs] + [st_ref[j] for j in range(ppb)]
        _, vjp = jax.vjp(fn, *flat_in)
        grads = vjp((tuple(dy_ref[:, ln] for ln in lanes), tuple(dht[j] for j in range(ppb))))
        for j in range(ppb):
            for r, g in zip(dseq_refs, grads[j * ns:(j + 1) * ns]):
                r[:, lanes[j]] = g
            dht[j] = grads[ppb * ns + ncst + j]
        dcv = grads[ppb * ns:ppb * ns + ncst]
        if ncst:
            first = jnp.logical_and(pb == 0, i == 0)

            @pl.when(first)
            def _():
                for r, g in zip(dcst_refs, dcv):
                    r[...] = g

            @pl.when(jnp.logical_not(first))
            def _():
                for r, g in zip(dcst_refs, dcv):
                    r[...] += g

    rev = lambda i: nc - 1 - i
    wide = pl.BlockSpec((chunk, ppb * LANES), lambda pb, i: (rev(i), pb))
    in_specs = [_seq_spec(chunk, ppb, col, rev) for (_, col) in seq_in]
    in_specs += [pl.BlockSpec(a.shape, lambda pb, i: (0, 0)) for a in const_in]
    in_specs += [pl.BlockSpec((ppb, None, LANES, LANES), lambda pb, i: (pb, rev(i), 0, 0)), wide]
    out_specs = [wide for _ in seq_in]
    out_specs += [pl.BlockSpec(a.shape, lambda pb, i: (0, 0)) for a in const_in]
    out_shape = [jax.ShapeDtypeStruct((t, n_pairs * LANES), F32) for _ in seq_in]
    out_shape += [jax.ShapeDtypeStruct(a.shape, F32) for a in const_in]
    res = pl.pallas_call(
        kern, name=name, grid=(n_pairs // ppb, nc), in_specs=in_specs, out_specs=out_specs, out_shape=out_shape,
        scratch_shapes=[pltpu.VMEM((ppb, LANES, LANES), F32)],
        compiler_params=_params(("arbitrary", "arbitrary")),
    )(*[s[0] for s in seq_in], *const_in, states, dy)
    return list(res[:ns]), list(res[ns:])


def loss_head(x3, tgt, g, tr):
    rows, d = x3.shape

    def body(tv, fv):
        def f(x, gg):
            e = jnp.square(_rms(x, gg) - tv[1])
            return 0.5 * jnp.sum(jnp.mean(e, axis=-1, keepdims=True), axis=0, keepdims=True)
        l, vjp = jax.vjp(f, tv[0], fv[0])
        dx, dg = vjp(jnp.ones((1, 1), F32))
        return [dx, dx], [dg, jnp.broadcast_to(l, (8, LANES))]
    (dx, dxb), (dg, l) = row_call("loss_head", body, rows // tr, [(x3, tr, d, 0), (tgt, tr, d, 0)], [g],
                                  [(rows, tr, d, F32), (rows, tr, d, BF16)], [g.shape, (8, LANES)])
    return dx, dxb, dg, l


def _adam_math(w, g, m, v):
    m = ADAM_B1 * m + (1.0 - ADAM_B1) * g
    v = ADAM_B2 * v + (1.0 - ADAM_B2) * jnp.square(g)
    m_hat = m / (1.0 - ADAM_B1 ** ADAM_STEP)
    v_hat = v / (1.0 - ADAM_B2 ** ADAM_STEP)
    delta = -ADAM_LR * (m_hat / (jnp.sqrt(v_hat) + ADAM_EPS) + ADAM_WD * w)
    return delta, m, v


def adamw(name, w, m, v, g_parts):
    rows, cols = w.shape
    tr = _pick(rows, (256, 128, 64, 32, 16, 8))
    n_g = len(g_parts)

    def body(tv, fv):
        g = tv[3]
        for extra in tv[4:4 + n_g - 1]:
            g = g + extra
        delta, mn, vn = _adam_math(tv[0], g, tv[1], tv[2])
        return [g, delta, mn, vn], []
    tiled = [(a, tr, cols, 0) for a in (w, m, v, *g_parts)]
    outs, _ = row_call(name, body, rows // tr, tiled, [], [(rows, tr, cols, F32)] * 4, [])
    return outs


def sum_slots(name, r):
    _, rows, cols = r.shape
    tr = _pick(rows, (256, 128, 64, 32, 16, 8))

    def kern(r0, r1, r2, r3, o):
        o[...] = ((r0[...].astype(F32) + r1[...].astype(F32)) + r2[...].astype(F32)) + r3[...].astype(F32)

    in_specs = [pl.BlockSpec((None, tr, cols), functools.partial(lambda i, s: (s, i, 0), s=s)) for s in range(4)]
    return pl.pallas_call(
        kern, name=name, grid=(rows // tr,), in_specs=in_specs, out_specs=pl.BlockSpec((tr, cols), lambda i: (i, 0)),
        out_shape=jax.ShapeDtypeStruct((rows, cols), F32), compiler_params=_params(("arbitrary",)),
    )(r, r, r, r)


def _my_place():
    return lax.axis_index("x"), lax.axis_index("y"), lax.axis_index("c")


def _chip_peers(x, y):
    peers = [(1 - x, y), (x, 1 - y), (1 - x, 1 - y)]
    return peers, [2 * px + py for px, py in peers]


def gather_shards(name, arrays):
    nw = len(arrays)
    ANY = pl.BlockSpec(memory_space=pl.ANY)

    def body(*refs):
        ins, outs = refs[:nw], refs[nw:2 * nw]
        send, recv, loc = refs[2 * nw:]
        x, y, c = _my_place()
        q = 2 * x + y
        peers, chips = _chip_peers(x, y)

        def remote(w, j, slot):
            return pltpu.make_async_remote_copy(
                src_ref=ins[w], dst_ref=outs[w].at[slot], send_sem=send.at[w, j], recv_sem=recv.at[w, j],
                device_id=(*peers[j], c), device_id_type=MESH_ID)

        local = [pltpu.make_async_copy(ins[w], outs[w].at[q], loc.at[w]) for w in range(nw)]
        sends = [[remote(w, j, q) for j in range(3)] for w in range(nw)]
        for w in range(nw):
            local[w].start()
            for j in range(3):
                sends[w][j].start()
        for w in range(nw):
            local[w].wait()
            for j in range(3):
                sends[w][j].wait_send()
                remote(w, j, chips[j]).wait_recv()

    return pl.pallas_call(
        body, name=name, in_specs=[ANY] * nw, out_specs=[ANY] * nw,
        out_shape=[jax.ShapeDtypeStruct((4,) + a.shape, a.dtype) for a in arrays],
        scratch_shapes=[pltpu.SemaphoreType.DMA((nw, 3)), pltpu.SemaphoreType.DMA((nw, 3)), pltpu.SemaphoreType.DMA((nw,))],
        compiler_params=pltpu.CompilerParams(has_side_effects=True),
    )(*arrays)


def scatter_slots(name, arrays, collective_id):
    nw = len(arrays)

    def body(*refs):
        ins, outs = refs[:nw], refs[nw:2 * nw]
        send, recv = refs[2 * nw:]
        x, y, c = _my_place()
        q = 2 * x + y
        peers, chips = _chip_peers(x, y)
        barrier = pltpu.get_barrier_semaphore()
        for p in peers:
            pl.semaphore_signal(barrier, inc=1, device_id=(*p, c), device_id_type=MESH_ID)
        pl.semaphore_wait(barrier, 3)

        def remote(w, j, src_slot, dst_slot):
            return pltpu.make_async_remote_copy(
                src_ref=ins[w].at[src_slot], dst_ref=outs[w].at[dst_slot], send_sem=send.at[w, j], recv_sem=recv.at[w, j],
                device_id=(*peers[j], c), device_id_type=MESH_ID)

        sends = [[remote(w, j, chips[j], q) for j in range(3)] for w in range(nw)]
        for w in range(nw):
            for j in range(3):
                sends[w][j].start()
        for w in range(nw):
            for j in range(3):
                sends[w][j].wait_send()
                remote(w, j, q, chips[j]).wait_recv()

    return pl.kernel(
        body, out_type=[jax.ShapeDtypeStruct(a.shape, a.dtype) for a in arrays],
        mesh=plsc.ScalarSubcoreMesh(axis_name="sequencer", num_cores=1), name=name,
        scratch_types=[pltpu.SemaphoreType.DMA((nw, 3)), pltpu.SemaphoreType.DMA((nw, 3))],
        compiler_params=pltpu.CompilerParams(collective_id=collective_id),
    )(*arrays)


def gather_two_level(name, arrays, collective_id):
    nw = len(arrays)

    def body(*refs):
        ins, outs = refs[:nw], refs[nw:2 * nw]
        send, recv = refs[2 * nw:]
        x, y, c = _my_place()
        q = 2 * x + y
        me, sibling = (x, y, c), (x, y, 1 - c)
        peers = [(1 - x, y), (x, 1 - y), (1 - x, 1 - y)]
        chips = [2 * px + py for px, py in peers]
        barrier = pltpu.get_barrier_semaphore()
        for dev in [sibling] + [(*p, c) for p in peers]:
            pl.semaphore_signal(barrier, inc=1, device_id=dev, device_id_type=MESH_ID)
        pl.semaphore_wait(barrier, 4)

        def mine(w):
            hr = ins[w].shape[0] // 2
            return ins[w].at[pl.ds(c * hr, hr)]

        def copy(w, k, src, chip, half, to):
            return pltpu.make_async_remote_copy(
                src_ref=src, dst_ref=outs[w].at[chip, half], send_sem=send.at[w, k], recv_sem=recv.at[w, k],
                device_id=to, device_id_type=MESH_ID)

        first = [[copy(w, 0, mine(w), q, c, sibling)] + [copy(w, 1 + j, mine(w), q, c, (*peers[j], c)) for j in range(3)]
                 for w in range(nw)]
        for w in range(nw):
            for cp in first[w]:
                cp.start()
        passed = []
        for w in range(nw):
            for j in range(3):
                copy(w, 1 + j, mine(w), chips[j], c, me).wait_recv()
                fwd = copy(w, 4 + j, outs[w].at[chips[j], c], chips[j], c, sibling)
                fwd.start()
                passed.append(fwd)
        for w in range(nw):
            copy(w, 0, mine(w), q, 1 - c, me).wait_recv()
            for j in range(3):
                copy(w, 4 + j, mine(w), chips[j], 1 - c, me).wait_recv()
        for w in range(nw):
            for cp in first[w]:
                cp.wait_send()
        for cp in passed:
            cp.wait_send()

    out_type = [jax.ShapeDtypeStruct((4, 2, a.shape[0] // 2, a.shape[1]), a.dtype) for a in arrays]
    return pl.kernel(
        body, out_type=out_type, mesh=plsc.ScalarSubcoreMesh(axis_name="sequencer", num_cores=1), name=name,
        scratch_types=[pltpu.SemaphoreType.DMA((nw, 7)), pltpu.SemaphoreType.DMA((nw, 7))],
        compiler_params=pltpu.CompilerParams(collective_id=collective_id),
    )(*arrays)


def core_swap(name, arrays):
    nw = len(arrays)
    ANY = pl.BlockSpec(memory_space=pl.ANY)

    def body(*refs):
        ins, outs = refs[:nw], refs[nw:2 * nw]
        send, recv = refs[2 * nw:]
        x, y, c = _my_place()
        copies = [pltpu.make_async_remote_copy(
            src_ref=ins[w], dst_ref=outs[w], send_sem=send.at[w], recv_sem=recv.at[w],
            device_id=(x, y, 1 - c), device_id_type=MESH_ID) for w in range(nw)]
        for cp in copies:
            cp.start()
        for cp in copies:
            cp.wait_send()
            cp.wait_recv()

    return pl.pallas_call(
        body, name=name, in_specs=[ANY] * nw, out_specs=[ANY] * nw,
        out_shape=[jax.ShapeDtypeStruct(a.shape, a.dtype) for a in arrays],
        scratch_shapes=[pltpu.SemaphoreType.DMA((nw,)), pltpu.SemaphoreType.DMA((nw,))],
        compiler_params=pltpu.CompilerParams(has_side_effects=True),
    )(*arrays)


def all_reduce_small(name, v):
    rows = v.shape[0]
    VM = pl.BlockSpec(memory_space=pltpu.VMEM)

    def body(v_ref, o_ref, buf, send, recv):
        x, y, c = _my_place()
        me = 4 * x + 2 * y + c

        def peer(kx):
            return (x ^ ((kx >> 2) & 1), y ^ ((kx >> 1) & 1), c ^ (kx & 1))

        def copy(kx, slot):
            return pltpu.make_async_remote_copy(
                src_ref=v_ref, dst_ref=buf.at[slot], send_sem=send.at[kx - 1], recv_sem=recv.at[kx - 1],
                device_id=peer(kx), device_id_type=MESH_ID)

        sends = [copy(kx, me) for kx in range(1, 8)]
        for cp in sends:
            cp.start()
        buf[me] = v_ref[...]
        for kx in range(1, 8):
            copy(kx, me ^ kx).wait_recv()
        for cp in sends:
            cp.wait_send()
        acc = buf[0]
        for d in range(1, 8):
            acc = acc + buf[d]
        o_ref[...] = acc

    return pl.pallas_call(
        body, name=name, in_specs=[VM], out_specs=VM, out_shape=jax.ShapeDtypeStruct(v.shape, F32),
        scratch_shapes=[pltpu.VMEM((8, rows, LANES), F32), pltpu.SemaphoreType.DMA((7,)), pltpu.SemaphoreType.DMA((7,))],
        compiler_params=pltpu.CompilerParams(has_side_effects=True, vmem_limit_bytes=VMEM_LIMIT),
    )(v)


def _pad_cols(a, n):
    return jnp.pad(a, ((0, 0), (0, n - a.shape[1])))


def _pad_rows(a, n):
    return jnp.pad(a, ((0, n - a.shape[0]), (0, 0)))


def _halo(u, tr):
    t, cdim = u.shape
    tails = u.reshape(t // tr, tr, cdim)[:, tr - HALO:, :]
    tails = jnp.concatenate([jnp.zeros((1, HALO, cdim), u.dtype), tails[:-1]], axis=0)
    return tails.reshape(-1, cdim)


def _unhalo(du, dhalo, tr):
    t, cdim = du.shape
    n = t // tr
    dh = dhalo.reshape(n, HALO, cdim)
    dh = jnp.concatenate([dh[1:], jnp.zeros((1, HALO, cdim), du.dtype)], axis=0)
    d3 = du.reshape(n, tr, cdim)
    d3 = jnp.concatenate([d3[:, :tr - HALO, :], d3[:, tr - HALO:, :] + dh], axis=1)
    return d3.reshape(t, cdim)


def _to_slots(g, axis):
    r, cdim = g.shape
    if axis == 0:
        return g.reshape(4, r // 4, cdim)
    return g.reshape(r, 4, cdim // 4).transpose(1, 0, 2)


def _from_slots(s, axis):
    if axis == 0:
        return s.reshape(s.shape[0] * s.shape[1], s.shape[2])
    return s.transpose(1, 0, 2).reshape(s.shape[1], 4 * s.shape[2])


BIG = ("w_in", "w_out", "xattn_wq", "xattn_wk", "xattn_wv", "xattn_wo", "ffn_w1", "ffn_w2")
BIG_AXIS = {"w_in": 1, "w_out": 0, "xattn_wq": 0, "xattn_wk": 0, "xattn_wv": 0, "xattn_wo": 0, "ffn_w1": 1, "ffn_w2": 0}
SMALL_SHARDED = ("ssd_conv_w", "rwkv_w2", "rwkv_a2", "rwkv_g2")
GATHER_GROUPS = (("w_in",), ("w_out", "xattn_wq", "xattn_wk", "xattn_wv", "xattn_wo"), ("ffn_w1", "ffn_w2"))
REDUCE_GROUPS = (("ffn_w2", "ffn_w1"), ("xattn_wo", "xattn_wq", "xattn_wk", "xattn_wv", "w_out"),
                 ("rwkv_w2", "rwkv_a2", "rwkv_g2", "w_in"))
REDUCED = BIG + ("rwkv_w2", "rwkv_a2", "rwkv_g2")
REDUCE_AXIS = dict(BIG_AXIS, rwkv_w2=1, rwkv_a2=1, rwkv_g2=1)
WEIGHTS = ("norm_mix_g", "w_in", "ssd_conv_w", "ssd_conv_b", "ssd_dt_bias", "ssd_a_log", "ssd_d", "ssd_norm_g",
           "rwkv_mu", "rwkv_w0", "rwkv_w2", "rwkv_a0", "rwkv_a2", "rwkv_g2", "rwkv_k_k", "rwkv_k_a", "rwkv_r_k",
           "rwkv_ln_w", "rwkv_ln_b", "w_out", "norm_x_g", "norm_mem_g", "xattn_wq", "xattn_wk", "xattn_wv", "xattn_wo",
           "norm_ffn_g", "ffn_w1", "ffn_w2", "final_norm_g")


def _local_grads(x, mem, tgt, wt, full, big, reducer):
    t, d = x.shape
    w = d // 2
    nh = w // HEAD_DIM
    n_pairs = nh // 2
    ppg = n_pairs // SSD_GROUPS
    bc = SSD_GROUPS * SSD_STATE
    conv_dim = w + 2 * bc
    tr = ROW_TILE
    nt = t // tr
    dr = wt["rwkv_w2"].shape[0]
    ar = wt["rwkv_a2"].shape[0]
    gr = wt["rwkv_g2"].shape[0]

    big.start(0, None)
    big.start(1, None)
    h1, h1t = norm_fwd("norm_mix", x, wt["norm_mix_g"], tr)
    w_in = big.get("w_in", h1)
    o = 0
    segs = {}
    for nm, width in (("z", w), ("xbc", conv_dim), ("dt", nh), ("rkv", 3 * w), ("pw", dr), ("pa", ar), ("pg", gr)):
        segs[nm] = (o, width)
        o += width
    padded = {"z": w, "xbc": conv_dim, "dt": LANES, "rkv": 3 * w, "pw": LANES, "pa": LANES, "pg": gr}
    order = ("z", "xbc", "dt", "rkv", "pw", "pa", "pg")
    w_perm = jnp.concatenate([_pad_cols(w_in[:, segs[nm][0]:segs[nm][0] + segs[nm][1]], padded[nm]) for nm in order], axis=1)
    offs = {}
    o = 0
    for nm in order:
        offs[nm] = o
        o += padded[nm]
    n_perm = o
    lora_w = 2 * LANES + gr

    def seg_cols(a, nm, width=None):
        return a[:, offs[nm]:offs[nm] + (padded[nm] if width is None else width)]

    mu = wt["rwkv_mu"]
    mo = 3 * w
    mu_rkv = mu[:, :mo]
    mu_lora = jnp.concatenate([_pad_cols(mu[:, mo:mo + dr], LANES), _pad_cols(mu[:, mo + dr:mo + dr + ar], LANES),
                               mu[:, mo + dr + ar:]], axis=1)
    w2p = _pad_rows(full["rwkv_w2"], LANES)
    a2p = _pad_rows(full["rwkv_a2"], LANES)
    g2 = full["rwkv_g2"]
    conv_w = full["ssd_conv_w"]
    cw = [conv_w[i:i + 1] for i in range(SSD_CONV)]
    dt_bias = _pad_cols(wt["ssd_dt_bias"], LANES)
    a_log = _pad_cols(wt["ssd_a_log"], LANES)
    d_skip = _pad_cols(wt["ssd_d"], LANES)
    r_k = wt["rwkv_r_k"].reshape(1, w)

    u = matmul("in_proj", h1, w_perm)
    big.start(2, u)
    z, xbc, dtraw = seg_cols(u, "z"), seg_cols(u, "xbc"), seg_cols(u, "dt")
    urkv = seg_cols(u, "rkv")
    ulora = u[:, offs["pw"]:offs["pw"] + lora_w]

    halo_xbc = _halo(xbc, tr)
    ssd_pre_t = [(xbc, tr, conv_dim, 0), (halo_xbc, HALO, conv_dim, 0), (dtraw, tr, LANES, 0)]
    ssd_pre_f = cw + [wt["ssd_conv_b"], dt_bias]
    act, dt = fn_fwd("ssd_pre", _ssd_pre, nt, ssd_pre_t, ssd_pre_f, [(t, tr, conv_dim, F32), (t, tr, LANES, F32)])

    nb = w // LANES
    ssd_seq = [(act, None), (act, lambda p: nb + p // ppg), (act, lambda p: nb + SSD_GROUPS + p // ppg), (dt, lambda p: 0)]
    ssd_ppb = min(ppg, PAIRS_PER_STEP)
    rw_ppb = min(n_pairs, 2 * PAIRS_PER_STEP)

    def ssd_fn(sv, cv, hts, ids):
        return [_ssd_chunk(*s, cv[0], ht, p) for s, ht, p in zip(sv, hts, ids)]

    y_scan, ssd_states = scan_fwd("ssd_scan", ssd_fn, SSD_CHUNK, ssd_seq, [a_log], n_pairs, ssd_ppb)
    ssd_post_t = [(y_scan, tr, w, 0), (act, tr, w, 0), (z, tr, w, 0)]
    ssd_post_f = [d_skip, wt["ssd_norm_g"]]
    y_ssd, y_ssd_t = fn_fwd("ssd_post", _ssd_post, nt, ssd_post_t, ssd_post_f, [(t, tr, w, BF16)], (0,))

    halo_rkv, halo_lora = _halo(urkv, tr), _halo(ulora, tr)
    rw_pre_t = [(urkv, tr, 3 * w, 0), (ulora, tr, lora_w, 0), (halo_rkv, HALO, 3 * w, 0), (halo_lora, HALO, lora_w, 0)]
    rw_pre_f = [mu_rkv, mu_lora, wt["rwkv_w0"], wt["rwkv_a0"], wt["rwkv_k_k"], wt["rwkv_k_a"], w2p, a2p, g2]
    rw = fn_fwd("rwkv_pre", _rwkv_pre, nt, rw_pre_t, rw_pre_f, [(t, tr, w, F32)] * 7)
    r_, lw_, k2_, v_, nkk_, b_, gate_ = rw
    rw_seq = [(a, None) for a in (r_, lw_, k2_, v_, nkk_, b_)]

    def rw_fn(sv, cv, hts, ids):
        return _rwkv_chunks([(*s, ht) for s, ht in zip(sv, hts)])

    yr_scan, rw_states = scan_fwd("rwkv_scan", rw_fn, RWKV_CHUNK, rw_seq, [], n_pairs, rw_ppb)
    rw_post_t = [(a, tr, w, 0) for a in (yr_scan, r_, k2_, v_, gate_)]
    rw_post_f = [r_k, wt["rwkv_ln_w"], wt["rwkv_ln_b"]]
    y_rwkv, y_rwkv_t = fn_fwd("rwkv_post", _rwkv_post, nt, rw_post_t, rw_post_f, [(t, tr, w, BF16)], (0,))

    ymix = jnp.concatenate([y_ssd, y_rwkv], axis=1)
    ymix_t = jnp.concatenate([y_ssd_t, y_rwkv_t], axis=0)
    w_out = big.get("w_out", ymix)
    x1 = matmul("out_proj", ymix, w_out, resid=x)

    h2, h2t = norm_fwd("norm_x", x1, wt["norm_x_g"], tr)
    mrows = mem.shape[0]
    mn, mnt = norm_fwd("norm_mem", mem, wt["norm_mem_g"], mrows)
    wq, wk, wv, wo = [big.get(nm, ymix) for nm in ("xattn_wq", "xattn_wk", "xattn_wv", "xattn_wo")]
    q = matmul("xattn_q", h2, wq)
    kx = matmul("xattn_k", mn, wk)
    vx = matmul("xattn_v", mn, wv)
    ao, aot = fn_fwd("xattn_core", _attn, nt, [(q, tr, d, 0)], [kx, vx], [(t, tr, d, BF16)], (0,))
    x2 = matmul("xattn_o", ao, wo, resid=x1)

    h3, h3t = norm_fwd("norm_ffn", x2, wt["norm_ffn_g"], tr)
    w1, w2 = big.get("ffn_w1", h3), big.get("ffn_w2", h3)
    a1 = matmul("ffn_up", h3, w1, out_dtype=BF16)
    dff = a1.shape[1]
    f1, f1t = fn_fwd("ffn_act", _relu2, nt, [(a1, tr, dff, 0)], [], [(t, tr, dff, BF16)], (0,))
    x3 = matmul("ffn_down", f1, w2, resid=x2)

    dx3, dx3b, g_final, loss_tile = loss_head(x3, tgt, wt["final_norm_g"].reshape(1, d), tr)

    grads = {"final_norm_g": g_final.reshape(d)}
    grads["ffn_w2"] = matmul("ffn_down_dw", f1t, dx3b)
    df1 = matmul("ffn_down_dx", dx3b, w2, tb=True, out_dtype=BF16)
    (da1,), _ = fn_bwd("ffn_act_bwd", _relu2, nt, [(a1, tr, dff, 0)], [], [(df1, tr, dff, 0)], lambda c: [c[0].astype(F32)],
                       [(t, tr, dff, BF16)])
    grads["ffn_w1"] = matmul("ffn_up_dw", h3t, da1)
    dh3 = reducer.launch(0, grads, matmul("ffn_up_dx", da1, w1, tb=True))
    dx2, dx2b, grads["norm_ffn_g"] = norm_bwd("norm_ffn_bwd", x2, wt["norm_ffn_g"], dh3, dx3, tr)

    grads["xattn_wo"] = matmul("xattn_o_dw", aot, dx2b)
    dao = matmul("xattn_o_dx", dx2b, wo, tb=True)
    (dq,), (dkx, dvx) = fn_bwd("xattn_core_bwd", _attn, nt, [(q, tr, d, 0)], [kx, vx], [(dao, tr, d, 0)], lambda c: c,
                               [(t, tr, d, BF16)])
    grads["xattn_wq"] = matmul("xattn_q_dw", h2t, dq)
    dh2 = matmul("xattn_q_dx", dq, wq, tb=True)
    dkb, dvb = dkx.astype(BF16), dvx.astype(BF16)
    grads["xattn_wk"] = matmul("xattn_k_dw", mnt, dkb)
    grads["xattn_wv"] = matmul("xattn_v_dw", mnt, dvb)
    dmn = matmul("xattn_k_dx", dkb, wk, tb=True)
    dmn = matmul("xattn_v_dx", dvb, wv, tb=True, resid=dmn)
    _, _, grads["norm_mem_g"] = norm_bwd("norm_mem_bwd", mem, wt["norm_mem_g"], dmn, None, mrows)
    dx1, dx1b, grads["norm_x_g"] = norm_bwd("norm_x_bwd", x1, wt["norm_x_g"], dh2, dx2, tr)

    grads["w_out"] = matmul("out_proj_dw", ymix_t, dx1b)
    dymix = reducer.launch(1, grads, matmul("out_proj_dx", dx1b, w_out, tb=True))

    (dyr, dr1, dk1, dv1, dgate), (g_rk, grads["rwkv_ln_w"], grads["rwkv_ln_b"]) = fn_bwd(
        "rwkv_post_bwd", _rwkv_post, nt, rw_post_t, rw_post_f, [(dymix, tr, w, 1)], lambda c: c, [(t, tr, w, F32)] * 5)
    grads["rwkv_r_k"] = g_rk.reshape(wt["rwkv_r_k"].shape)
    (dr2, dlw, dk2, dv2, dnkk, db), _ = scan_bwd("rwkv_scan_bwd", rw_fn, RWKV_CHUNK, rw_seq, [], rw_states, dyr, n_pairs, rw_ppb)
    rw_ct = [(a, tr, w, 0) for a in (dr1, dr2, dlw, dk1, dk2, dv1, dv2, dnkk, db, dgate)]

    def rw_ct_fn(c):
        return (c[0] + c[1], c[2], c[3] + c[4], c[5] + c[6], c[7], c[8], c[9])

    (durkv, dulora, dhrkv, dhlora), rw_pg = fn_bwd(
        "rwkv_pre_bwd", _rwkv_pre, nt, rw_pre_t, rw_pre_f, rw_ct, rw_ct_fn,
        [(t, tr, 3 * w, F32), (t, tr, lora_w, F32), (nt * HALO, HALO, 3 * w, F32), (nt * HALO, HALO, lora_w, F32)])
    durkv = _unhalo(durkv, dhrkv, tr)
    dulora = _unhalo(dulora, dhlora, tr)
    g_mu_rkv, g_mu_lora, grads["rwkv_w0"], grads["rwkv_a0"], grads["rwkv_k_k"], grads["rwkv_k_a"], g_w2p, g_a2p, grads["rwkv_g2"] = rw_pg
    grads["rwkv_mu"] = jnp.concatenate([g_mu_rkv, g_mu_lora[:, :dr], g_mu_lora[:, LANES:LANES + ar], g_mu_lora[:, 2 * LANES:]], axis=1)
    grads["rwkv_w2"] = g_w2p[:dr]
    grads["rwkv_a2"] = g_a2p[:ar]

    (dys, dxs1, dz), (g_d, grads["ssd_norm_g"]) = fn_bwd(
        "ssd_post_bwd", _ssd_post, nt, ssd_post_t, ssd_post_f, [(dymix, tr, w, 0)], lambda c: c, [(t, tr, w, F32)] * 3)
    grads["ssd_d"] = g_d[:, :nh]
    (dxs2, dbp, dcp, ddtp), (g_alog,) = scan_bwd("ssd_scan_bwd", ssd_fn, SSD_CHUNK, ssd_seq, [a_log], ssd_states, dys, n_pairs, ssd_ppb)
    grads["ssd_a_log"] = g_alog[:, :nh]
    ssd_ct = [(dxs1, tr, w, 0), (dxs2, tr, w, 0), (dbp, tr, w, 0), (dcp, tr, w, 0), (ddtp, tr, w, 0)]

    def ssd_ct_fn(c):
        def group_sum(a):
            parts = []
            for gi in range(SSD_GROUPS):
                s = a[:, gi * ppg * LANES:(gi * ppg + 1) * LANES]
                for j in range(1, ppg):
                    s = s + a[:, (gi * ppg + j) * LANES:(gi * ppg + j + 1) * LANES]
                parts.append(s)
            return parts
        ddt = c[4][:, :LANES]
        for j in range(1, n_pairs):
            ddt = ddt + c[4][:, j * LANES:(j + 1) * LANES]
        return (jnp.concatenate([c[0] + c[1]] + group_sum(c[2]) + group_sum(c[3]), axis=1), ddt)

    (dxbc, dhxbc, ddtraw), ssd_pg = fn_bwd(
        "ssd_pre_bwd", _ssd_pre, nt, ssd_pre_t, ssd_pre_f, ssd_ct, ssd_ct_fn,
        [(t, tr, conv_dim, F32), (nt * HALO, HALO, conv_dim, F32), (t, tr, LANES, F32)])
    dxbc = _unhalo(dxbc, dhxbc, tr)
    grads["ssd_conv_w"] = jnp.concatenate(ssd_pg[:SSD_CONV], axis=0)
    grads["ssd_conv_b"] = ssd_pg[SSD_CONV]
    grads["ssd_dt_bias"] = ssd_pg[SSD_CONV + 1][:, :nh]

    du = jnp.concatenate([dz, dxbc, ddtraw, durkv, dulora], axis=1).astype(BF16)
    g_perm = matmul("in_proj_dw", h1t, du)
    grads["w_in"] = jnp.concatenate([seg_cols(g_perm, nm, segs[nm][1]) for nm in order], axis=1)
    dh1 = matmul("in_proj_dx", du, w_perm, tb=True)
    dh1 = reducer.launch(2, grads, dh1)
    grad_x, _, grads["norm_mix_g"] = norm_bwd("norm_mix_bwd", x, wt["norm_mix_g"], dh1, dx1, tr)
    return loss_tile, grad_x, grads


def _pack(arrs):
    flat = jnp.concatenate([a.reshape(-1) for a in arrs])
    n = flat.shape[0]
    rows = -(-n // (8 * LANES)) * 8
    return jnp.pad(flat, (0, rows * LANES - n)).reshape(rows, LANES)


def _unpack(packed, shapes):
    flat = packed.reshape(-1)
    out, o = [], 0
    for s in shapes:
        n = math.prod(s)
        out.append(flat[o:o + n].reshape(s))
        o += n
    return out


def _as2d(a):
    return a.reshape(-1, a.shape[-1])


class _GatheredWeights:
    def __init__(self, shard2d, q, c):
        self.shard2d, self.q, self.c = shard2d, q, c
        self.raw, self.ready = {}, {}

    def start(self, gi, after):
        shards = [self.shard2d[n].astype(BF16) for n in GATHER_GROUPS[gi]]
        if after is not None:
            shards, _ = lax.optimization_barrier((shards, after))
        gathered = gather_two_level("gather_weights_%d" % gi, shards, gi + 1)
        for n, sh, g in zip(GATHER_GROUPS[gi], shards, gathered):
            self.raw[n] = (sh, g)

    def get(self, name, after):
        if name not in self.ready:
            sh, g = self.raw[name]
            if after is not None:
                g, _ = lax.optimization_barrier((g, after))
            hr = sh.shape[0] // 2
            own = lax.dynamic_slice_in_dim(sh, self.c * hr, hr, axis=0)
            g = lax.dynamic_update_slice(g, own[None, None], (self.q, self.c, 0, 0))
            self.ready[name] = _from_slots(g.reshape(4, 2 * hr, g.shape[3]), BIG_AXIS[name])
        return self.ready[name]


class _GradReducer:
    def __init__(self, q, c, update):
        self.q, self.c, self.update = q, c, update
        self.pending, self.updated = {}, {}

    def launch(self, gi, grads, nxt):
        names = REDUCE_GROUPS[gi]
        kept, sent = [], []
        for n in names:
            s = _to_slots(grads[n], REDUCE_AXIS[n])
            s = s.reshape(4, 2, s.shape[1] // 2, s.shape[2])
            kept.append(lax.dynamic_index_in_dim(s, self.c, axis=1, keepdims=False))
            sent.append(lax.dynamic_index_in_dim(s, 1 - self.c, axis=1, keepdims=False).astype(BF16))
        got = core_swap("swap_halves_%d" % gi, sent)
        parts = []
        for n, k, g in zip(names, kept, got):
            _, hr, cols = k.shape
            tr = _pick(4 * hr, (256, 128, 64, 32, 16))
            (part,), _ = row_call("chip_sum_" + n, lambda tv, fv: ([tv[0] + tv[1].astype(F32)], []), 4 * hr // tr,
                                  [(k.reshape(4 * hr, cols), tr, cols, 0), (g.reshape(4 * hr, cols), tr, cols, 0)], [],
                                  [(4 * hr, tr, cols, BF16)], [])
            parts.append(part.reshape(4, hr, cols))
        parts, nxt = lax.optimization_barrier((parts, nxt))
        self.pending[gi] = (parts, scatter_slots("scatter_grads_%d" % gi, parts, len(GATHER_GROUPS) + 1 + gi))
        return self.finish(gi - 1, nxt) if gi > 0 else nxt

    def finish(self, gi, nxt):
        names = REDUCE_GROUPS[gi]
        parts, slots = self.pending[gi]
        halves = []
        for n, p, s in zip(names, parts, slots):
            own = lax.dynamic_index_in_dim(p, self.q, axis=0, keepdims=True)
            halves.append(sum_slots("sum_" + n, lax.dynamic_update_slice(s, own, (self.q, 0, 0))))
        others = core_swap("swap_reduced_%d" % gi, halves)
        lo = [jnp.where(self.c == 0, mine, other) for mine, other in zip(halves, others)]
        hi = [jnp.where(self.c == 0, other, mine) for mine, other in zip(halves, others)]
        results = [self.update(n, jnp.concatenate([l, h], axis=0)) for n, l, h in zip(names, lo, hi)]
        if nxt is not None:
            results, nxt = lax.optimization_barrier((results, nxt))
        self.updated.update(zip(names, results))
        return nxt


def _step(a):
    x, mem, tgt = a["x"][0], a["mem"][0], a["loss_target"][0]
    q = 2 * lax.axis_index("x") + lax.axis_index("y")

    shard2d = {n: _as2d(a[n][0]) for n in BIG}
    small_sh = {n: _as2d(a[n][0]) for n in SMALL_SHARDED}
    c = lax.axis_index("c")
    full = {}
    big = _GatheredWeights(shard2d, q, c)
    gathered = gather_shards("gather_small", [small_sh[n] for n in SMALL_SHARDED])
    for n, g in zip(SMALL_SHARDED, gathered):
        full[n] = _from_slots(g, 1)

    wt = {n: (a[n] if a[n].ndim <= 2 else a[n][0]) for n in WEIGHTS if n not in BIG and n not in SMALL_SHARDED}
    for n in SMALL_SHARDED:
        wt[n] = small_sh[n]
    shards = dict(shard2d)
    shards.update({n: small_sh[n] for n in REDUCED if n not in BIG})

    def update(n, gsum):
        return adamw("adamw_" + n, shards[n], _as2d(a["m_" + n][0]), _as2d(a["v_" + n][0]), [gsum])

    reducer = _GradReducer(q, c, update)
    loss_tile, grad_x, grads = _local_grads(x, mem, tgt, wt, full, big, reducer)
    reducer.finish(len(REDUCE_GROUPS) - 1, None)
    out = {}
    for n, vals in reducer.updated.items():
        for key, val in zip(("grad_", "delta_", "new_m_", "new_v_"), vals):
            out[key + n] = val.reshape(a[n].shape)

    small = [n for n in WEIGHTS if n not in REDUCED]
    red = _unpack(all_reduce_small("all_reduce_small", _pack([grads[n] for n in small])), [grads[n].shape for n in small])
    g_loc = {}
    for n, g in zip(small, red):
        if n in SMALL_SHARDED:
            cols = g.shape[1] // 4
            g = lax.dynamic_slice_in_dim(g, q * cols, cols, axis=1)
        g_loc[n] = g.reshape(a[n].shape)
    res = adamw("adamw_small", *[_pack([src[n] for n in small]) for src in
                                 ({n: a[n] for n in small}, {n: a["m_" + n] for n in small}, {n: a["v_" + n] for n in small})],
                [_pack([g_loc[n] for n in small])])
    shapes = [a[n].shape for n in small]
    for key, packed in zip(("grad_", "delta_", "new_m_", "new_v_"), res):
        for n, val in zip(small, _unpack(packed, shapes)):
            out[key + n] = val

    loss = lax.psum(loss_tile[0, 0], ("x", "y", "c"))
    ordered = [loss, grad_x.reshape(a["x"].shape)]
    for key in ("grad_", "delta_", "new_m_", "new_v_"):
        ordered += [out[key + n] for n in WEIGHTS]
    return tuple(ordered)


def kernel(x, mem, norm_mix_g, w_in, ssd_conv_w, ssd_conv_b, ssd_dt_bias, ssd_a_log, ssd_d, ssd_norm_g, rwkv_mu, rwkv_w0, rwkv_w2, rwkv_a0, rwkv_a2, rwkv_g2, rwkv_k_k, rwkv_k_a, rwkv_r_k, rwkv_ln_w, rwkv_ln_b, w_out, norm_x_g, norm_mem_g, xattn_wq, xattn_wk, xattn_wv, xattn_wo, norm_ffn_g, ffn_w1, ffn_w2, final_norm_g, loss_target, m_norm_mix_g, m_w_in, m_ssd_conv_w, m_ssd_conv_b, m_ssd_dt_bias, m_ssd_a_log, m_ssd_d, m_ssd_norm_g, m_rwkv_mu, m_rwkv_w0, m_rwkv_w2, m_rwkv_a0, m_rwkv_a2, m_rwkv_g2, m_rwkv_k_k, m_rwkv_k_a, m_rwkv_r_k, m_rwkv_ln_w, m_rwkv_ln_b, m_w_out, m_norm_x_g, m_norm_mem_g, m_xattn_wq, m_xattn_wk, m_xattn_wv, m_xattn_wo, m_norm_ffn_g, m_ffn_w1, m_ffn_w2, m_final_norm_g, v_norm_mix_g, v_w_in, v_ssd_conv_w, v_ssd_conv_b, v_ssd_dt_bias, v_ssd_a_log, v_ssd_d, v_ssd_norm_g, v_rwkv_mu, v_rwkv_w0, v_rwkv_w2, v_rwkv_a0, v_rwkv_a2, v_rwkv_g2, v_rwkv_k_k, v_rwkv_k_a, v_rwkv_r_k, v_rwkv_ln_w, v_rwkv_ln_b, v_w_out, v_norm_x_g, v_norm_mem_g, v_xattn_wq, v_xattn_wk, v_xattn_wv, v_xattn_wo, v_norm_ffn_g, v_ffn_w1, v_ffn_w2, v_final_norm_g):
    return _step(dict(locals()))
```

```python
import functools
import math

import jax
import jax.numpy as jnp
from jax import lax
from jax.experimental import pallas as pl
from jax.experimental.pallas import tpu as pltpu
from jax.experimental.pallas import tpu_sc as plsc

F32 = jnp.float32
BF16 = jnp.bfloat16
HIGHEST = lax.Precision.HIGHEST
MESH_ID = pl.DeviceIdType.MESH

NORM_EPS = 1e-6
RWKV_LN_EPS = 64e-5
HEAD_DIM = 64
PAIR = 2 * HEAD_DIM
LANES = 128
SSD_STATE = 128
SSD_CHUNK = 128
SSD_GROUPS = 2
SSD_CONV = 4
RWKV_CHUNK = 64
HALO = 8
ROW_TILE = 128
PAIRS_PER_STEP = 4
XATTN_HEADS = 4
RWKV_PASSES = 1
VMEM_LIMIT = 56 * 1024 * 1024
MATMUL_VMEM = 40 * 1024 * 1024

ADAM_LR = 0.001
ADAM_B1 = 0.9
ADAM_B2 = 0.999
ADAM_EPS = 1e-08
ADAM_WD = 0.01
ADAM_STEP = 10


def _dims(ca, cb):
    return (((ca,), (cb,)), ((), ()))


def _split_bf16(a):
    hi = a.astype(BF16)
    lo = (a - hi.astype(F32)).astype(BF16)
    return hi, lo


def _mm_impl(a, b, ca, cb, passes):
    dn = _dims(ca, cb)
    if passes == 1:
        return lax.dot_general(a.astype(BF16), b.astype(BF16), dn, preferred_element_type=F32)
    ah, al = _split_bf16(a)
    bh, bl = _split_bf16(b)
    out = lax.dot_general(ah, bh, dn, preferred_element_type=F32)
    out = out + lax.dot_general(ah, bl, dn, preferred_element_type=F32)
    return out + lax.dot_general(al, bh, dn, preferred_element_type=F32)


@functools.partial(jax.custom_vjp, nondiff_argnums=(2, 3, 4))
def mm(a, b, ca, cb, passes):
    return _mm_impl(a, b, ca, cb, passes)


def _mm_fwd(a, b, ca, cb, passes):
    return _mm_impl(a, b, ca, cb, passes), (a, b)


def _mm_bwd(ca, cb, passes, res, g):
    a, b = res
    da = mm(g, b, 1, 1 - cb, passes) if ca == 1 else mm(b, g, 1 - cb, 1, passes)
    db = mm(a, g, 1 - ca, 0, passes) if cb == 0 else mm(g, a, 0, 1 - ca, passes)
    return da, db


mm.defvjp(_mm_fwd, _mm_bwd)


def _dot_exact(a, b):
    return lax.dot_general(a, b, _dims(1, 0), precision=HIGHEST, preferred_element_type=F32)


def _iota(shape, dim):
    return lax.broadcasted_iota(jnp.int32, shape, dim)


def _sigmoid(x):
    return 1.0 / (1.0 + jnp.exp(-x))


def _silu(x):
    return x * _sigmoid(x)


def _softplus(x):
    return jnp.maximum(x, 0.0) + jnp.log(1.0 + jnp.exp(-jnp.abs(x)))


def _rms(x, g):
    return x * lax.rsqrt(jnp.mean(x * x, axis=-1, keepdims=True) + NORM_EPS) * g


def _select_mm(x, sel):
    hi = x.astype(BF16)
    r1 = x - hi.astype(F32)
    mid = r1.astype(BF16)
    lo = (r1 - mid.astype(F32)).astype(BF16)
    dn = _dims(1, 0)
    out = lax.dot_general(hi, sel, dn, preferred_element_type=F32)
    out = out + lax.dot_general(mid, sel, dn, preferred_element_type=F32)
    return out + lax.dot_general(lo, sel, dn, preferred_element_type=F32)


def _head_sum_impl(x, n):
    sel = (_iota((n, LANES), 0) // HEAD_DIM == _iota((n, LANES), 1)).astype(BF16)
    return _select_mm(x, sel)


def _head_expand_impl(s, n):
    sel = (_iota((LANES, n), 1) // HEAD_DIM == _iota((LANES, n), 0)).astype(BF16)
    return _select_mm(s, sel)


@functools.partial(jax.custom_vjp, nondiff_argnums=(1,))
def _head_sum_n(x, n):
    return _head_sum_impl(x, n)


@functools.partial(jax.custom_vjp, nondiff_argnums=(1,))
def _head_expand(s, n):
    return _head_expand_impl(s, n)


_head_sum_n.defvjp(lambda x, n: (_head_sum_impl(x, n), None), lambda n, _, g: (_head_expand(g, n),))
_head_expand.defvjp(lambda s, n: (_head_expand_impl(s, n), None), lambda n, _, g: (_head_sum_n(g, n),))


def _head_sum(x):
    return _head_sum_n(x, x.shape[1])


def _row_vector_expand(v, n):
    v8 = jnp.broadcast_to(v, (8, LANES))
    return jnp.sum(_head_expand(v8, n), axis=0, keepdims=True) * 0.125


def _shift_rows_impl(u, halo, s):
    rolled = pltpu.roll(u, s, 0)
    top = jnp.where(_iota((HALO, 1), 0) < s, pltpu.roll(halo, s, 0), rolled[:HALO])
    return jnp.concatenate([top, rolled[HALO:]], axis=0)


@functools.partial(jax.custom_vjp, nondiff_argnums=(2,))
def _shift_rows(u, halo, s):
    return _shift_rows_impl(u, halo, s)


def _shift_rows_bwd(s, _, g):
    tr = g.shape[0]
    rolled = pltpu.roll(g, tr - s, 0)
    hrow = _iota((HALO, 1), 0)
    bottom = jnp.where(hrow < HALO - s, rolled[tr - HALO:], 0.0)
    dhalo = jnp.where(hrow >= HALO - s, pltpu.roll(g[:HALO], HALO - s, 0), 0.0)
    return jnp.concatenate([rolled[:tr - HALO], bottom], axis=0), dhalo


_shift_rows.defvjp(lambda u, halo, s: (_shift_rows_impl(u, halo, s), None), _shift_rows_bwd)


def _params(sem):
    return pltpu.CompilerParams(dimension_semantics=sem, vmem_limit_bytes=VMEM_LIMIT)


def row_call(name, body, n_tiles, tiled, full, out_tiled, out_acc, transposed=()):
    nt, nf, na = len(tiled), len(full), len(out_acc)
    n_plain = len(out_tiled)
    no = n_plain + len(transposed)

    def kern(*refs):
        tv = [r[...] for r in refs[:nt]]
        fv = [r[...] for r in refs[nt:nt + nf]]
        outs, accs = body(tv, fv)
        for r, v in zip(refs[nt + nf:nt + nf + n_plain], outs):
            r[...] = v.astype(r.dtype)
        for r, idx in zip(refs[nt + nf + n_plain:nt + nf + no], transposed):
            r[...] = outs[idx].astype(F32).T.astype(r.dtype)
        if na:
            a_refs = refs[nt + nf + no:]
            first = pl.program_id(0) == 0

            @pl.when(first)
            def _():
                for r, v in zip(a_refs, accs):
                    r[...] = v

            @pl.when(jnp.logical_not(first))
            def _():
                for r, v in zip(a_refs, accs):
                    r[...] += v

    in_specs = [pl.BlockSpec((rt, w), functools.partial(lambda i, cb: (i, cb), cb=cb)) for (_, rt, w, cb) in tiled]
    in_specs += [pl.BlockSpec(a.shape, lambda i: (0, 0)) for a in full]
    out_specs = [pl.BlockSpec((rt, w), lambda i: (i, 0)) for (_, rt, w, _) in out_tiled]
    out_specs += [pl.BlockSpec((out_tiled[idx][2], out_tiled[idx][1]), lambda i: (0, i)) for idx in transposed]
    out_specs += [pl.BlockSpec(s, lambda i: (0, 0)) for s in out_acc]
    out_shape = [jax.ShapeDtypeStruct((rows, w), dt) for (rows, _, w, dt) in out_tiled]
    out_shape += [jax.ShapeDtypeStruct((out_tiled[idx][2], out_tiled[idx][0]), BF16) for idx in transposed]
    out_shape += [jax.ShapeDtypeStruct(s, F32) for s in out_acc]
    res = pl.pallas_call(
        kern, name=name, grid=(n_tiles,), in_specs=in_specs, out_specs=out_specs, out_shape=out_shape,
        compiler_params=_params(("arbitrary",)),
    )(*[t[0] for t in tiled], *full)
    return list(res[:no]), list(res[no:])


def _pick(dim, cands):
    for c in cands:
        if dim % c == 0:
            return c
    return dim


def matmul(name, a, b, tb=False, resid=None, out_dtype=F32):
    m, k = a.shape
    n = b.shape[0] if tb else b.shape[1]
    has_resid = resid is not None
    out_bytes = jnp.dtype(out_dtype).itemsize
    sizes = (2048, 1024, 896, 768, 512, 384, 256, 128)
    tm = _pick(m, sizes[1:])
    tn = _pick(n, sizes[1:])

    def vmem_bytes(tk):
        return 2 * 2 * tk * (tm + tn) + tm * tn * (2 * out_bytes + 4 + (8 if has_resid else 0))

    tk = next((c for c in sizes if k % c == 0 and vmem_bytes(c) <= MATMUL_VMEM), LANES)
    nk = k // tk

    def kern(*refs):
        a_ref, b_ref = refs[0], refs[1]
        o_ref, acc = refs[-2], refs[-1]
        kk = pl.program_id(2)
        part = lax.dot_general(a_ref[...], b_ref[...], _dims(1, 1 if tb else 0), preferred_element_type=F32)

        def finish(out):
            if has_resid:
                out = out + refs[2][...]
            o_ref[...] = out.astype(o_ref.dtype)

        if nk == 1:
            finish(part)
            return

        @pl.when(kk == 0)
        def _():
            acc[...] = part

        @pl.when(jnp.logical_and(kk > 0, kk < nk - 1))
        def _():
            acc[...] += part

        @pl.when(kk == nk - 1)
        def _():
            finish(acc[...] + part)

    in_specs = [pl.BlockSpec((tm, tk), lambda i, j, kk: (i, kk))]
    if tb:
        in_specs.append(pl.BlockSpec((tn, tk), lambda i, j, kk: (j, kk)))
    else:
        in_specs.append(pl.BlockSpec((tk, tn), lambda i, j, kk: (kk, j)))
    args = [a, b]
    if has_resid:
        in_specs.append(pl.BlockSpec((tm, tn), lambda i, j, kk: (i, j)))
        args.append(resid)
    return pl.pallas_call(
        kern, name=name, grid=(m // tm, n // tn, nk), in_specs=in_specs,
        out_specs=pl.BlockSpec((tm, tn), lambda i, j, kk: (i, j)),
        out_shape=jax.ShapeDtypeStruct((m, n), out_dtype),
        scratch_shapes=[pltpu.VMEM((tm, tn), F32)],
        compiler_params=_params(("parallel", "parallel", "arbitrary")),
    )(*args)


def norm_fwd(name, x, g, tr, with_transpose=True):
    def body(tv, fv):
        return [_rms(tv[0], fv[0])], []
    rows, d = x.shape
    outs, _ = row_call(name, body, rows // tr, [(x, tr, d, 0)], [g], [(rows, tr, d, BF16)], [],
                       transposed=(0,) if with_transpose else ())
    return outs[0], (outs[1] if with_transpose else None)


def norm_bwd(name, x, g, dh, extra, tr):
    def body(tv, fv):
        _, vjp = jax.vjp(_rms, tv[0], fv[0])
        dx, dg = vjp(tv[1])
        if extra is not None:
            dx = dx + tv[2]
        return [dx, dx], [dg]
    rows, d = x.shape
    tiled = [(x, tr, d, 0), (dh, tr, d, 0)] + ([(extra, tr, d, 0)] if extra is not None else [])
    (dx, dxb), (dg,) = row_call(name, body, rows // tr, tiled, [g], [(rows, tr, d, F32), (rows, tr, d, BF16)], [g.shape])
    return dx, dxb, dg


def _ssd_pre(xbc, halo, dtraw, w0, w1, w2, w3, cb, dtb):
    y = w3 * xbc + w2 * _shift_rows(xbc, halo, 1) + w1 * _shift_rows(xbc, halo, 2) + w0 * _shift_rows(xbc, halo, 3) + cb
    return _silu(y), _softplus(dtraw + dtb)


def _ssd_post(ys, xs, z, dskip, ng):
    w = ys.shape[1]
    y = (ys + xs * _row_vector_expand(dskip, w)) * _silu(z)
    gw = w // SSD_GROUPS
    parts = []
    for gi in range(SSD_GROUPS):
        yg = y[:, gi * gw:(gi + 1) * gw]
        parts.append(yg * lax.rsqrt(jnp.mean(yg * yg, axis=-1, keepdims=True) + NORM_EPS))
    return jnp.concatenate(parts, axis=1) * ng


def _rwkv_pre(urkv, ulora, hrkv, hlora, mu_rkv, mu_lora, w0, a0, kkw, kaw, w2p, a2p, g2):
    w = w0.shape[1]
    urkv = urkv + (_shift_rows(urkv, hrkv, 1) - urkv) * mu_rkv
    ulora = ulora + (_shift_rows(ulora, hlora, 1) - ulora) * mu_lora
    r, k, v = urkv[:, :w], urkv[:, w:2 * w], urkv[:, 2 * w:]
    pw, pa, pg = ulora[:, :LANES], ulora[:, LANES:2 * LANES], ulora[:, 2 * LANES:]
    w_log = -_softplus(-(w0 + mm(jnp.tanh(pw), w2p, 1, 0, 1))) - 0.5
    lw = -jnp.exp(w_log)
    iclr = _sigmoid(a0 + mm(pa, a2p, 1, 0, 1))
    gate = mm(_sigmoid(pg), g2, 1, 0, 1)
    kk = k * kkw
    kk = kk / jnp.maximum(jnp.sqrt(_head_expand(_head_sum(kk * kk), w)), 1e-12)
    k2 = k * (1.0 + (iclr - 1.0) * kaw)
    return r, lw, k2, v, -kk, kk * iclr, gate


def _rwkv_post(ys, r, k2, v, gate, rk, lnw, lnb):
    w = ys.shape[1]
    inv = 1.0 / HEAD_DIM
    mean = _head_expand(_head_sum(ys), w) * inv
    d = ys - mean
    var = _head_expand(_head_sum(d * d), w) * inv
    yn = d * lax.rsqrt(var + RWKV_LN_EPS) * lnw + lnb
    bonus = _head_expand(_head_sum(r * k2 * rk), w) * v
    return (yn + bonus) * gate


def _attn(q, k, v):
    d = q.shape[1]
    hd = d // XATTN_HEADS
    outs = []
    for h in range(XATTN_HEADS):
        sl = slice(h * hd, (h + 1) * hd)
        s = mm(q[:, sl], k[:, sl], 1, 1, 1) * (hd ** -0.5)
        s = s - jnp.max(s, axis=-1, keepdims=True)
        p = jnp.exp(s)
        p = p / jnp.sum(p, axis=-1, keepdims=True)
        outs.append(mm(p, v[:, sl], 1, 0, 1))
    return jnp.concatenate(outs, axis=1)


def _relu2(a):
    return jnp.square(jnp.maximum(a.astype(F32), 0.0))


def fn_fwd(name, fn, n_tiles, tiled, full, out_tiled, transposed=()):
    def body(tv, fv):
        outs = fn(*tv, *fv)
        return (list(outs) if isinstance(outs, (tuple, list)) else [outs]), []
    outs, _ = row_call(name, body, n_tiles, tiled, full, out_tiled, [], transposed)
    return outs


def fn_bwd(name, fn, n_tiles, tiled, full, cts, ct_fn, out_tiled):
    nt = len(tiled)

    def body(tv, fv):
        outs, vjp = jax.vjp(fn, *tv[:nt], *fv)
        ct = ct_fn(tv[nt:])
        grads = vjp(tuple(ct) if isinstance(outs, (tuple, list)) else ct[0])
        return list(grads[:nt]), list(grads[nt:])
    return row_call(name, body, n_tiles, tiled + cts, full, out_tiled, [f.shape for f in full])


def _ssd_chunk(xs, bm, cm, dt_all, a_log, ht, p):
    q = xs.shape[0]
    lane = _iota((1, LANES), 1)
    row = _iota((q, 1), 0)
    tril = _iota((q, q), 0) >= _iota((q, q), 1)
    half = lane < HEAD_DIM
    da = dt_all * (-jnp.exp(a_log))
    cs = _dot_exact(tril.astype(F32), da)

    def col(mat, h):
        return jnp.sum(jnp.where(lane == h, mat, 0.0), axis=1, keepdims=True)

    cs0, cs1 = col(cs, 2 * p), col(cs, 2 * p + 1)
    xdt = xs * jnp.where(half, col(dt_all, 2 * p), col(dt_all, 2 * p + 1))
    csx = jnp.where(half, cs0, cs1)
    last = jnp.sum(jnp.where(row == q - 1, csx, 0.0), axis=0, keepdims=True)
    cb = mm(cm, bm, 1, 1, 1)
    y = mm(cm, ht, 1, 0, 1) * jnp.exp(csx)
    for csh, hm in ((cs0, half), (cs1, jnp.logical_not(half))):
        csl = jnp.broadcast_to(csh, (q, q))
        seg = csl - csl.T
        lmat = jnp.where(tril, jnp.exp(jnp.where(tril, seg, 0.0)), 0.0)
        y = y + jnp.where(hm, mm(cb * lmat, xdt, 1, 0, 1), 0.0)
    st = mm(bm, xdt * jnp.exp(last - csx), 0, 0, 1)
    return y, ht * jnp.exp(last) + st


def _rwkv_chunks(pairs):
    c = pairs[0][0].shape[0]
    ps = RWKV_PASSES
    lane = _iota((1, LANES), 1)
    row = _iota((c, 1), 0)
    ri, ci = _iota((c, c), 0), _iota((c, c), 1)
    tril_i, tril_s = ri >= ci, ri > ci
    eye = (ri == ci).astype(F32)
    half = lane < HEAD_DIM
    halves = (half, jnp.logical_not(half))
    bd = (_iota((LANES, LANES), 0) < HEAD_DIM) == (_iota((LANES, LANES), 1) < HEAD_DIM)
    tri = tril_i.astype(F32)
    n = len(pairs)
    heads = [(j, hm) for j in range(n) for hm in halves]

    cum = [_dot_exact(tri, p[1]) for p in pairs]
    at = [p[4] * jnp.exp(cm - p[1]) for p, cm in zip(pairs, cum)]
    en = [jnp.exp(-cm) for cm in cum]
    bt = [p[5] * e for p, e in zip(pairs, en)]
    kt = [p[2] * e for p, e in zip(pairs, en)]
    rt = [p[0] * jnp.exp(cm) for p, cm in zip(pairs, cum)]
    ah = [mm(at[j], pairs[j][6], 1, 1, ps) for j in range(n)]
    y = [mm(rt[j], pairs[j][6], 1, 1, ps) for j in range(n)]
    atm = [jnp.where(hm, at[j], 0.0) for j, hm in heads]
    rtm = [jnp.where(hm, rt[j], 0.0) for j, hm in heads]
    aab = [jnp.where(tril_s, mm(atm[i], bt[j], 1, 1, ps), 0.0) for i, (j, _) in enumerate(heads)]
    aak = [jnp.where(tril_s, mm(atm[i], kt[j], 1, 1, ps), 0.0) for i, (j, _) in enumerate(heads)]
    arb = [jnp.where(tril_i, mm(rtm[i], bt[j], 1, 1, ps), 0.0) for i, (j, _) in enumerate(heads)]
    ark = [jnp.where(tril_i, mm(rtm[i], kt[j], 1, 1, ps), 0.0) for i, (j, _) in enumerate(heads)]
    rhs = [ah[j] + mm(aak[i], pairs[j][3], 1, 0, ps) for i, (j, _) in enumerate(heads)]
    yv = [mm(ark[i], pairs[j][3], 1, 0, ps) for i, (j, _) in enumerate(heads)]
    tm = [eye + a_ for a_ in aab]
    pm = aab
    for _ in range(int(math.log2(c)) - 1):
        pm = [mm(p_, p_, 1, 0, ps) for p_ in pm]
        tm = [t_ + mm(t_, p_, 1, 0, ps) for t_, p_ in zip(tm, pm)]
    uh = [mm(tm[i], rhs[i], 1, 0, ps) for i in range(len(heads))]
    u = [jnp.where(half, uh[2 * j], uh[2 * j + 1]) for j in range(n)]
    yu = [mm(arb[i], u[j], 1, 0, ps) for i, (j, _) in enumerate(heads)]
    out = []
    for j in range(n):
        yj = y[j] + jnp.where(half, yu[2 * j] + yv[2 * j], yu[2 * j + 1] + yv[2 * j + 1])
        plast = jnp.sum(jnp.where(row == c - 1, cum[j], 0.0), axis=0, keepdims=True)
        upd = pairs[j][6] + mm(u[j], bt[j], 0, 0, ps) + mm(pairs[j][3], kt[j], 0, 0, ps)
        out.append((yj, jnp.where(bd, upd * jnp.exp(plast), 0.0)))
    return out


def _seq_spec(chunk, ppb, col, row_of):
    if col is None:
        return pl.BlockSpec((chunk, ppb * LANES), lambda pb, i: (row_of(i), pb))
    return pl.BlockSpec((chunk, LANES), lambda pb, i: (row_of(i), col(pb * ppb)))


def _pair_vals(refs, seq_in, j):
    return [r[...] if col is not None else r[:, j * LANES:(j + 1) * LANES] for r, (_, col) in zip(refs, seq_in)]


def scan_fwd(name, chunk_fn, chunk, seq_in, const_in, n_pairs, ppb):
    t = seq_in[0][0].shape[0]
    nc = t // chunk
    ns, ncst = len(seq_in), len(const_in)

    def kern(*refs):
        y_ref, st_ref, ht = refs[ns + ncst], refs[ns + ncst + 1], refs[ns + ncst + 2]

        @pl.when(pl.program_id(1) == 0)
        def _():
            ht[...] = jnp.zeros_like(ht)

        cv = [r[...] for r in refs[ns:ns + ncst]]
        h0 = [ht[j] for j in range(ppb)]
        for j in range(ppb):
            st_ref[j] = h0[j]
        sv = [_pair_vals(refs[:ns], seq_in, j) for j in range(ppb)]
        outs = chunk_fn(sv, cv, h0, [pl.program_id(0) * ppb + j for j in range(ppb)])
        for j, (y, hn) in enumerate(outs):
            y_ref[:, j * LANES:(j + 1) * LANES] = y
            ht[j] = hn

    in_specs = [_seq_spec(chunk, ppb, col, lambda i: i) for (_, col) in seq_in]
    in_specs += [pl.BlockSpec(a.shape, lambda pb, i: (0, 0)) for a in const_in]
    return pl.pallas_call(
        kern, name=name, grid=(n_pairs // ppb, nc), in_specs=in_specs,
        out_specs=[pl.BlockSpec((chunk, ppb * LANES), lambda pb, i: (i, pb)),
                   pl.BlockSpec((ppb, None, LANES, LANES), lambda pb, i: (pb, i, 0, 0))],
        out_shape=[jax.ShapeDtypeStruct((t, n_pairs * LANES), F32), jax.ShapeDtypeStruct((n_pairs, nc, LANES, LANES), F32)],
        scratch_shapes=[pltpu.VMEM((ppb, LANES, LANES), F32)],
        compiler_params=_params(("arbitrary", "arbitrary")),
    )(*[s[0] for s in seq_in], *const_in)


def scan_bwd(name, chunk_fn, chunk, seq_in, const_in, states, dy, n_pairs, ppb):
    t = dy.shape[0]
    nc = t // chunk
    ns, ncst = len(seq_in), len(const_in)

    def kern(*refs):
        seq_refs, cst_refs = refs[:ns], refs[ns:ns + ncst]
        st_ref, dy_ref = refs[ns + ncst], refs[ns + ncst + 1]
        o = ns + ncst + 2
        dseq_refs, dcst_refs, dht = refs[o:o + ns], refs[o + ns:o + ns + ncst], refs[o + ns + ncst]
        pb, i = pl.program_id(0), pl.program_id(1)

        @pl.when(i == 0)
        def _():
            dht[...] = jnp.zeros_like(dht)

        ids = [pb * ppb + j for j in range(ppb)]
        lanes = [slice(j * LANES, (j + 1) * LANES) for j in range(ppb)]

        def fn(*flat):
            sv = [list(flat[j * ns:(j + 1) * ns]) for j in range(ppb)]
            outs = chunk_fn(sv, list(flat[ppb * ns:ppb * ns + ncst]), list(flat[ppb * ns + ncst:]), ids)
            return tuple(y for y, _ in outs), tuple(h for _, h in outs)

        flat_in = [v for j in range(ppb) for v in _pair_vals(seq_refs, seq_in, j)]
        flat_in += [r[...] for r in cst_refs] + [st_ref[j] for j in range(ppb)]
        _, vjp = jax.vjp(fn, *flat_in)
        grads = vjp((tuple(dy_ref[:, ln] for ln in lanes), tuple(dht[j] for j in range(ppb))))
        for j in range(ppb):
            for r, g in zip(dseq_refs, grads[j * ns:(j + 1) * ns]):
                r[:, lanes[j]] = g
            dht[j] = grads[ppb * ns + ncst + j]
        dcv = grads[ppb * ns:ppb * ns + ncst]
        if ncst:
            first = jnp.logical_and(pb == 0, i == 0)

            @pl.when(first)
            def _():
                for r, g in zip(dcst_refs, dcv):
                    r[...] = g

            @pl.when(jnp.logical_not(first))
            def _():
                for r, g in zip(dcst_refs, dcv):
                    r[...] += g

    rev = lambda i: nc - 1 - i
    wide = pl.BlockSpec((chunk, ppb * LANES), lambda pb, i: (rev(i), pb))
    in_specs = [_seq_spec(chunk, ppb, col, rev) for (_, col) in seq_in]
    in_specs += [pl.BlockSpec(a.shape, lambda pb, i: (0, 0)) for a in const_in]
    in_specs += [pl.BlockSpec((ppb, None, LANES, LANES), lambda pb, i: (pb, rev(i), 0, 0)), wide]
    out_specs = [wide for _ in seq_in]
    out_specs += [pl.BlockSpec(a.shape, lambda pb, i: (0, 0)) for a in const_in]
    out_shape = [jax.ShapeDtypeStruct((t, n_pairs * LANES), F32) for _ in seq_in]
    out_shape += [jax.ShapeDtypeStruct(a.shape, F32) for a in const_in]
    res = pl.pallas_call(
        kern, name=name, grid=(n_pairs // ppb, nc), in_specs=in_specs, out_specs=out_specs, out_shape=out_shape,
        scratch_shapes=[pltpu.VMEM((ppb, LANES, LANES), F32)],
        compiler_params=_params(("arbitrary", "arbitrary")),
    )(*[s[0] for s in seq_in], *const_in, states, dy)
    return list(res[:ns]), list(res[ns:])


def loss_head(x3, tgt, g, tr):
    rows, d = x3.shape

    def body(tv, fv):
        def f(x, gg):
            e = jnp.square(_rms(x, gg) - tv[1])
            return 0.5 * jnp.sum(jnp.mean(e, axis=-1, keepdims=True), axis=0, keepdims=True)
        l, vjp = jax.vjp(f, tv[0], fv[0])
        dx, dg = vjp(jnp.ones((1, 1), F32))
        return [dx, dx], [dg, jnp.broadcast_to(l, (8, LANES))]
    (dx, dxb), (dg, l) = row_call("loss_head", body, rows // tr, [(x3, tr, d, 0), (tgt, tr, d, 0)], [g],
                                  [(rows, tr, d, F32), (rows, tr, d, BF16)], [g.shape, (8, LANES)])
    return dx, dxb, dg, l


def _adam_math(w, g, m, v):
    m = ADAM_B1 * m + (1.0 - ADAM_B1) * g
    v = ADAM_B2 * v + (1.0 - ADAM_B2) * jnp.square(g)
    m_hat = m / (1.0 - ADAM_B1 ** ADAM_STEP)
    v_hat = v / (1.0 - ADAM_B2 ** ADAM_STEP)
    delta = -ADAM_LR * (m_hat / (jnp.sqrt(v_hat) + ADAM_EPS) + ADAM_WD * w)
    return delta, m, v


def _tiling(rows, cols, limit):
    row_tile = max([d for d in range(16, rows + 1, 16) if rows % d == 0 and d * cols <= limit], default=0)
    col_tile = max([ct for ct in range(LANES, cols + 1, LANES) if cols % ct == 0 and rows * ct <= limit], default=0)
    if row_tile and row_tile * cols >= rows * col_tile:
        return row_tile, cols
    return (rows, col_tile) if col_tile else (rows, cols)


def ew_call(name, fn, ins, out_dtypes, limit=1 << 20):
    rows, cols = ins[0].shape
    br, bc = _tiling(rows, cols, limit)
    spec = pl.BlockSpec((br, bc), lambda i, j: (i, j))
    n_in = len(ins)

    def kern(*refs):
        for r, v in zip(refs[n_in:], fn(*[r[...] for r in refs[:n_in]])):
            r[...] = v.astype(r.dtype)

    return pl.pallas_call(
        kern, name=name, grid=(rows // br, cols // bc), in_specs=[spec] * n_in, out_specs=[spec] * len(out_dtypes),
        out_shape=[jax.ShapeDtypeStruct((rows, cols), dt) for dt in out_dtypes],
        compiler_params=_params(("parallel", "parallel")),
    )(*ins)


def adamw(name, w, m, v, g):
    return ew_call(name, lambda wv, mv, vv, gv: (gv, *_adam_math(wv, gv, mv, vv)), [w, m, v, g], [F32] * 4, 1 << 18)


def sum_slots(name, r):
    _, rows, cols = r.shape
    br, bc = _tiling(rows, cols, 1 << 20)

    def kern(r0, r1, r2, r3, o):
        o[...] = ((r0[...].astype(F32) + r1[...].astype(F32)) + r2[...].astype(F32)) + r3[...].astype(F32)

    in_specs = [pl.BlockSpec((None, br, bc), functools.partial(lambda i, j, s: (s, i, j), s=s)) for s in range(4)]
    return pl.pallas_call(
        kern, name=name, grid=(rows // br, cols // bc), in_specs=in_specs,
        out_specs=pl.BlockSpec((br, bc), lambda i, j: (i, j)),
        out_shape=jax.ShapeDtypeStruct((rows, cols), F32), compiler_params=_params(("parallel", "parallel")),
    )(r, r, r, r)


def _my_place():
    return lax.axis_index("x"), lax.axis_index("y"), lax.axis_index("c")


def _chip_peers(x, y):
    peers = [(1 - x, y), (x, 1 - y), (1 - x, 1 - y)]
    return peers, [2 * px + py for px, py in peers]


def gather_shards(name, arrays):
    nw = len(arrays)
    ANY = pl.BlockSpec(memory_space=pl.ANY)

    def body(*refs):
        ins, outs = refs[:nw], refs[nw:2 * nw]
        send, recv, loc = refs[2 * nw:]
        x, y, c = _my_place()
        q = 2 * x + y
        peers, chips = _chip_peers(x, y)

        def remote(w, j, slot):
            return pltpu.make_async_remote_copy(
                src_ref=ins[w], dst_ref=outs[w].at[slot], send_sem=send.at[w, j], recv_sem=recv.at[w, j],
                device_id=(*peers[j], c), device_id_type=MESH_ID)

        local = [pltpu.make_async_copy(ins[w], outs[w].at[q], loc.at[w]) for w in range(nw)]
        sends = [[remote(w, j, q) for j in range(3)] for w in range(nw)]
        for w in range(nw):
            local[w].start()
            for j in range(3):
                sends[w][j].start()
        for w in range(nw):
            local[w].wait()
            for j in range(3):
                sends[w][j].wait_send()
                remote(w, j, chips[j]).wait_recv()

    return pl.pallas_call(
        body, name=name, in_specs=[ANY] * nw, out_specs=[ANY] * nw,
        out_shape=[jax.ShapeDtypeStruct((4,) + a.shape, a.dtype) for a in arrays],
        scratch_shapes=[pltpu.SemaphoreType.DMA((nw, 3)), pltpu.SemaphoreType.DMA((nw, 3)), pltpu.SemaphoreType.DMA((nw,))],
        compiler_params=pltpu.CompilerParams(has_side_effects=True),
    )(*arrays)


def scatter_slots(name, arrays, collective_id):
    nw = len(arrays)

    def body(*refs):
        ins, outs = refs[:nw], refs[nw:2 * nw]
        send, recv = refs[2 * nw:]
        x, y, c = _my_place()
        q = 2 * x + y
        peers, chips = _chip_peers(x, y)
        barrier = pltpu.get_barrier_semaphore()
        for p in peers:
            pl.semaphore_signal(barrier, inc=1, device_id=(*p, c), device_id_type=MESH_ID)
        pl.semaphore_wait(barrier, 3)

        def remote(w, j, src_slot, dst_slot):
            return pltpu.make_async_remote_copy(
                src_ref=ins[w].at[src_slot], dst_ref=outs[w].at[dst_slot], send_sem=send.at[w, j], recv_sem=recv.at[w, j],
                device_id=(*peers[j], c), device_id_type=MESH_ID)

        sends = [[remote(w, j, chips[j], q) for j in range(3)] for w in range(nw)]
        for w in range(nw):
            for j in range(3):
                sends[w][j].start()
        for w in range(nw):
            for j in range(3):
                sends[w][j].wait_send()
                remote(w, j, q, chips[j]).wait_recv()

    return pl.kernel(
        body, out_type=[jax.ShapeDtypeStruct(a.shape, a.dtype) for a in arrays],
        mesh=plsc.ScalarSubcoreMesh(axis_name="sequencer", num_cores=1), name=name,
        scratch_types=[pltpu.SemaphoreType.DMA((nw, 3)), pltpu.SemaphoreType.DMA((nw, 3))],
        compiler_params=pltpu.CompilerParams(collective_id=collective_id),
    )(*arrays)


def _halves_by_cols(rows):
    return rows % 32 != 0


def _half_of(ref, shape, h):
    rows, cols = shape
    if _halves_by_cols(rows):
        return ref.at[:, pl.ds(h * (cols // 2), cols // 2)]
    return ref.at[pl.ds(h * (rows // 2), rows // 2)]


def _half_value(a, h):
    rows, cols = a.shape[-2:]
    if _halves_by_cols(rows):
        return lax.dynamic_slice_in_dim(a, h * (cols // 2), cols // 2, axis=a.ndim - 1)
    return lax.dynamic_slice_in_dim(a, h * (rows // 2), rows // 2, axis=a.ndim - 2)


def _join_halves(lo, hi, rows):
    return jnp.concatenate([lo, hi], axis=lo.ndim - 1 if _halves_by_cols(rows) else lo.ndim - 2)


def gather_two_level(name, arrays, collective_id):
    nw = len(arrays)
    shapes = [a.shape for a in arrays]

    def body(*refs):
        ins, outs = refs[:nw], refs[nw:2 * nw]
        send, recv = refs[2 * nw:]
        x, y, c = _my_place()
        q = 2 * x + y
        me, sibling = (x, y, c), (x, y, 1 - c)
        peers = [(1 - x, y), (x, 1 - y), (1 - x, 1 - y)]
        chips = [2 * px + py for px, py in peers]
        barrier = pltpu.get_barrier_semaphore()
        for dev in [sibling] + [(*p, c) for p in peers]:
            pl.semaphore_signal(barrier, inc=1, device_id=dev, device_id_type=MESH_ID)
        pl.semaphore_wait(barrier, 4)

        def mine(w):
            return _half_of(ins[w], shapes[w], c)

        def landed(w, chip, half):
            return _half_of(outs[w].at[chip], shapes[w], half)

        def copy(w, k, src, chip, half, to):
            return pltpu.make_async_remote_copy(
                src_ref=src, dst_ref=landed(w, chip, half), send_sem=send.at[w, k], recv_sem=recv.at[w, k],
                device_id=to, device_id_type=MESH_ID)

        first = [[copy(w, 0, mine(w), q, c, sibling)] + [copy(w, 1 + j, mine(w), q, c, (*peers[j], c)) for j in range(3)]
                 for w in range(nw)]
        for w in range(nw):
            for cp in first[w]:
                cp.start()
        passed = []
        for w in range(nw):
            for j in range(3):
                copy(w, 1 + j, mine(w), chips[j], c, me).wait_recv()
                fwd = copy(w, 4 + j, landed(w, chips[j], c), chips[j], c, sibling)
                fwd.start()
                passed.append(fwd)
        for w in range(nw):
            copy(w, 0, mine(w), q, 1 - c, me).wait_recv()
            for j in range(3):
                copy(w, 4 + j, mine(w), chips[j], 1 - c, me).wait_recv()
        for w in range(nw):
            for cp in first[w]:
                cp.wait_send()
        for cp in passed:
            cp.wait_send()

    out_type = [jax.ShapeDtypeStruct((4,) + a.shape, a.dtype) for a in arrays]
    return pl.kernel(
        body, out_type=out_type, mesh=plsc.ScalarSubcoreMesh(axis_name="sequencer", num_cores=1), name=name,
        scratch_types=[pltpu.SemaphoreType.DMA((nw, 7)), pltpu.SemaphoreType.DMA((nw, 7))],
        compiler_params=pltpu.CompilerParams(collective_id=collective_id),
    )(*arrays)


def core_swap(name, arrays):
    nw = len(arrays)
    ANY = pl.BlockSpec(memory_space=pl.ANY)

    def body(*refs):
        ins, outs = refs[:nw], refs[nw:2 * nw]
        send, recv = refs[2 * nw:]
        x, y, c = _my_place()
        copies = [pltpu.make_async_remote_copy(
            src_ref=ins[w], dst_ref=outs[w], send_sem=send.at[w], recv_sem=recv.at[w],
            device_id=(x, y, 1 - c), device_id_type=MESH_ID) for w in range(nw)]
        for cp in copies:
            cp.start()
        for cp in copies:
            cp.wait_send()
            cp.wait_recv()

    return pl.pallas_call(
        body, name=name, in_specs=[ANY] * nw, out_specs=[ANY] * nw,
        out_shape=[jax.ShapeDtypeStruct(a.shape, a.dtype) for a in arrays],
        scratch_shapes=[pltpu.SemaphoreType.DMA((nw,)), pltpu.SemaphoreType.DMA((nw,))],
        compiler_params=pltpu.CompilerParams(has_side_effects=True),
    )(*arrays)


def all_reduce_small(name, v):
    rows = v.shape[0]
    VM = pl.BlockSpec(memory_space=pltpu.VMEM)

    def body(v_ref, o_ref, buf, send, recv):
        x, y, c = _my_place()
        me = 4 * x + 2 * y + c

        def peer(kx):
            return (x ^ ((kx >> 2) & 1), y ^ ((kx >> 1) & 1), c ^ (kx & 1))

        def copy(kx, slot):
            return pltpu.make_async_remote_copy(
                src_ref=v_ref, dst_ref=buf.at[slot], send_sem=send.at[kx - 1], recv_sem=recv.at[kx - 1],
                device_id=peer(kx), device_id_type=MESH_ID)

        sends = [copy(kx, me) for kx in range(1, 8)]
        for cp in sends:
            cp.start()
        buf[me] = v_ref[...]
        for kx in range(1, 8):
            copy(kx, me ^ kx).wait_recv()
        for cp in sends:
            cp.wait_send()
        acc = buf[0]
        for d in range(1, 8):
            acc = acc + buf[d]
        o_ref[...] = acc

    return pl.pallas_call(
        body, name=name, in_specs=[VM], out_specs=VM, out_shape=jax.ShapeDtypeStruct(v.shape, F32),
        scratch_shapes=[pltpu.VMEM((8, rows, LANES), F32), pltpu.SemaphoreType.DMA((7,)), pltpu.SemaphoreType.DMA((7,))],
        compiler_params=pltpu.CompilerParams(has_side_effects=True, vmem_limit_bytes=VMEM_LIMIT),
    )(v)


def _pad_cols(a, n):
    return jnp.pad(a, ((0, 0), (0, n - a.shape[1])))


def _pad_rows(a, n):
    return jnp.pad(a, ((0, n - a.shape[0]), (0, 0)))


def _halo(u, tr):
    t, cdim = u.shape
    tails = u.reshape(t // tr, tr, cdim)[:, tr - HALO:, :]
    tails = jnp.concatenate([jnp.zeros((1, HALO, cdim), u.dtype), tails[:-1]], axis=0)
    return tails.reshape(-1, cdim)


def _unhalo(du, dhalo, tr):
    t, cdim = du.shape
    n = t // tr
    dh = dhalo.reshape(n, HALO, cdim)
    dh = jnp.concatenate([dh[1:], jnp.zeros((1, HALO, cdim), du.dtype)], axis=0)
    d3 = du.reshape(n, tr, cdim)
    d3 = jnp.concatenate([d3[:, :tr - HALO, :], d3[:, tr - HALO:, :] + dh], axis=1)
    return d3.reshape(t, cdim)


def _to_slots(g, axis):
    r, cdim = g.shape
    if axis == 0:
        return g.reshape(4, r // 4, cdim)
    return g.reshape(r, 4, cdim // 4).transpose(1, 0, 2)


def _from_slots(s, axis):
    if axis == 0:
        return s.reshape(s.shape[0] * s.shape[1], s.shape[2])
    return s.transpose(1, 0, 2).reshape(s.shape[1], 4 * s.shape[2])


BIG = ("w_in", "w_out", "xattn_wq", "xattn_wk", "xattn_wv", "xattn_wo", "ffn_w1", "ffn_w2")
TRANSPOSED = ("w_in",)
BIG_AXIS = {"w_in": 0, "w_out": 0, "xattn_wq": 0, "xattn_wk": 0, "xattn_wv": 0, "xattn_wo": 0, "ffn_w1": 1, "ffn_w2": 0}
SMALL_SHARDED = ("ssd_conv_w", "rwkv_w2", "rwkv_a2", "rwkv_g2")
GATHER_GROUPS = (("w_in",), ("w_out", "xattn_wq", "xattn_wk", "xattn_wv", "xattn_wo"), ("ffn_w1", "ffn_w2"))
REDUCE_GROUPS = (("ffn_w2", "ffn_w1"), ("xattn_wo", "xattn_wq", "xattn_wk", "xattn_wv", "w_out"),
                 ("rwkv_w2", "rwkv_a2", "rwkv_g2", "w_in"))
REDUCED = BIG + ("rwkv_w2", "rwkv_a2", "rwkv_g2")
REDUCE_AXIS = dict(BIG_AXIS, rwkv_w2=1, rwkv_a2=1, rwkv_g2=1)
WEIGHTS = ("norm_mix_g", "w_in", "ssd_conv_w", "ssd_conv_b", "ssd_dt_bias", "ssd_a_log", "ssd_d", "ssd_norm_g",
           "rwkv_mu", "rwkv_w0", "rwkv_w2", "rwkv_a0", "rwkv_a2", "rwkv_g2", "rwkv_k_k", "rwkv_k_a", "rwkv_r_k",
           "rwkv_ln_w", "rwkv_ln_b", "w_out", "norm_x_g", "norm_mem_g", "xattn_wq", "xattn_wk", "xattn_wv", "xattn_wo",
           "norm_ffn_g", "ffn_w1", "ffn_w2", "final_norm_g")


def _local_grads(x, mem, tgt, wt, full, big, reducer):
    t, d = x.shape
    w = d // 2
    nh = w // HEAD_DIM
    n_pairs = nh // 2
    ppg = n_pairs // SSD_GROUPS
    bc = SSD_GROUPS * SSD_STATE
    conv_dim = w + 2 * bc
    tr = ROW_TILE
    nt = t // tr
    dr = wt["rwkv_w2"].shape[0]
    ar = wt["rwkv_a2"].shape[0]
    gr = wt["rwkv_g2"].shape[0]

    big.start(0, None)
    big.start(1, None)
    h1, _ = norm_fwd("norm_mix", x, wt["norm_mix_g"], tr, with_transpose=False)
    w_in_t = big.get("w_in", h1)
    o = 0
    segs = {}
    for nm, width in (("z", w), ("xbc", conv_dim), ("dt", nh), ("rkv", 3 * w), ("pw", dr), ("pa", ar), ("pg", gr)):
        segs[nm] = (o, width)
        o += width
    padded = {"z": w, "xbc": conv_dim, "dt": LANES, "rkv": 3 * w, "pw": LANES, "pa": LANES, "pg": gr}
    order = ("z", "xbc", "dt", "rkv", "pw", "pa", "pg")
    w_perm_t = jnp.concatenate([_pad_rows(w_in_t[segs[nm][0]:segs[nm][0] + segs[nm][1]], padded[nm]) for nm in order], axis=0)
    offs = {}
    o = 0
    for nm in order:
        offs[nm] = o
        o += padded[nm]
    n_perm = o
    lora_w = 2 * LANES + gr

    def seg_cols(a, nm, width=None):
        return a[:, offs[nm]:offs[nm] + (padded[nm] if width is None else width)]

    mu = wt["rwkv_mu"]
    mo = 3 * w
    mu_rkv = mu[:, :mo]
    mu_lora = jnp.concatenate([_pad_cols(mu[:, mo:mo + dr], LANES), _pad_cols(mu[:, mo + dr:mo + dr + ar], LANES),
                               mu[:, mo + dr + ar:]], axis=1)
    w2p = _pad_rows(full["rwkv_w2"], LANES)
    a2p = _pad_rows(full["rwkv_a2"], LANES)
    g2 = full["rwkv_g2"]
    conv_w = full["ssd_conv_w"]
    cw = [conv_w[i:i + 1] for i in range(SSD_CONV)]
    dt_bias = _pad_cols(wt["ssd_dt_bias"], LANES)
    a_log = _pad_cols(wt["ssd_a_log"], LANES)
    d_skip = _pad_cols(wt["ssd_d"], LANES)
    r_k = wt["rwkv_r_k"].reshape(1, w)

    u = matmul("in_proj", h1, w_perm_t, tb=True)
    big.start(2, u)
    z, xbc, dtraw = seg_cols(u, "z"), seg_cols(u, "xbc"), seg_cols(u, "dt")
    urkv = seg_cols(u, "rkv")
    ulora = u[:, offs["pw"]:offs["pw"] + lora_w]

    halo_xbc = _halo(xbc, tr)
    ssd_pre_t = [(xbc, tr, conv_dim, 0), (halo_xbc, HALO, conv_dim, 0), (dtraw, tr, LANES, 0)]
    ssd_pre_f = cw + [wt["ssd_conv_b"], dt_bias]
    act, dt = fn_fwd("ssd_pre", _ssd_pre, nt, ssd_pre_t, ssd_pre_f, [(t, tr, conv_dim, F32), (t, tr, LANES, F32)])

    nb = w // LANES
    ssd_seq = [(act, None), (act, lambda p: nb + p // ppg), (act, lambda p: nb + SSD_GROUPS + p // ppg), (dt, lambda p: 0)]
    ssd_ppb = min(ppg, PAIRS_PER_STEP)
    rw_ppb = min(n_pairs, 2 * PAIRS_PER_STEP)

    def ssd_fn(sv, cv, hts, ids):
        return [_ssd_chunk(*s, cv[0], ht, p) for s, ht, p in zip(sv, hts, ids)]

    y_scan, ssd_states = scan_fwd("ssd_scan", ssd_fn, SSD_CHUNK, ssd_seq, [a_log], n_pairs, ssd_ppb)
    ssd_post_t = [(y_scan, tr, w, 0), (act, tr, w, 0), (z, tr, w, 0)]
    ssd_post_f = [d_skip, wt["ssd_norm_g"]]
    y_ssd, y_ssd_t = fn_fwd("ssd_post", _ssd_post, nt, ssd_post_t, ssd_post_f, [(t, tr, w, BF16)], (0,))

    halo_rkv, halo_lora = _halo(urkv, tr), _halo(ulora, tr)
    rw_pre_t = [(urkv, tr, 3 * w, 0), (ulora, tr, lora_w, 0), (halo_rkv, HALO, 3 * w, 0), (halo_lora, HALO, lora_w, 0)]
    rw_pre_f = [mu_rkv, mu_lora, wt["rwkv_w0"], wt["rwkv_a0"], wt["rwkv_k_k"], wt["rwkv_k_a"], w2p, a2p, g2]
    rw = fn_fwd("rwkv_pre", _rwkv_pre, nt, rw_pre_t, rw_pre_f, [(t, tr, w, F32)] * 7)
    r_, lw_, k2_, v_, nkk_, b_, gate_ = rw
    rw_seq = [(a, None) for a in (r_, lw_, k2_, v_, nkk_, b_)]

    def rw_fn(sv, cv, hts, ids):
        return _rwkv_chunks([(*s, ht) for s, ht in zip(sv, hts)])

    yr_scan, rw_states = scan_fwd("rwkv_scan", rw_fn, RWKV_CHUNK, rw_seq, [], n_pairs, rw_ppb)
    rw_post_t = [(a, tr, w, 0) for a in (yr_scan, r_, k2_, v_, gate_)]
    rw_post_f = [r_k, wt["rwkv_ln_w"], wt["rwkv_ln_b"]]
    y_rwkv, y_rwkv_t = fn_fwd("rwkv_post", _rwkv_post, nt, rw_post_t, rw_post_f, [(t, tr, w, BF16)], (0,))

    ymix = jnp.concatenate([y_ssd, y_rwkv], axis=1)
    ymix_t = jnp.concatenate([y_ssd_t, y_rwkv_t], axis=0)
    w_out = big.get("w_out", ymix)
    x1 = matmul("out_proj", ymix, w_out, resid=x)

    h2, h2t = norm_fwd("norm_x", x1, wt["norm_x_g"], tr)
    mrows = mem.shape[0]
    mn, mnt = norm_fwd("norm_mem", mem, wt["norm_mem_g"], mrows)
    wq, wk, wv, wo = [big.get(nm, ymix) for nm in ("xattn_wq", "xattn_wk", "xattn_wv", "xattn_wo")]
    q = matmul("xattn_q", h2, wq)
    kx = matmul("xattn_k", mn, wk)
    vx = matmul("xattn_v", mn, wv)
    ao, aot = fn_fwd("xattn_core", _attn, nt, [(q, tr, d, 0)], [kx, vx], [(t, tr, d, BF16)], (0,))
    x2 = matmul("xattn_o", ao, wo, resid=x1)

    h3, h3t = norm_fwd("norm_ffn", x2, wt["norm_ffn_g"], tr)
    w1, w2 = big.get("ffn_w1", h3), big.get("ffn_w2", h3)
    a1 = matmul("ffn_up", h3, w1, out_dtype=BF16)
    dff = a1.shape[1]
    f1, f1t = fn_fwd("ffn_act", _relu2, nt, [(a1, tr, dff, 0)], [], [(t, tr, dff, BF16)], (0,))
    x3 = matmul("ffn_down", f1, w2, resid=x2)

    dx3, dx3b, g_final, loss_tile = loss_head(x3, tgt, wt["final_norm_g"].reshape(1, d), tr)

    grads = {"final_norm_g": g_final.reshape(d)}
    grads["ffn_w2"] = matmul("ffn_down_dw", f1t, dx3b)
    df1 = matmul("ffn_down_dx", dx3b, w2, tb=True, out_dtype=BF16)
    (da1,), _ = fn_bwd("ffn_act_bwd", _relu2, nt, [(a1, tr, dff, 0)], [], [(df1, tr, dff, 0)], lambda c: [c[0].astype(F32)],
                       [(t, tr, dff, BF16)])
    grads["ffn_w1"] = matmul("ffn_up_dw", h3t, da1)
    dh3 = reducer.launch(0, grads, matmul("ffn_up_dx", da1, w1, tb=True))
    dx2, dx2b, grads["norm_ffn_g"] = norm_bwd("norm_ffn_bwd", x2, wt["norm_ffn_g"], dh3, dx3, tr)

    grads["xattn_wo"] = matmul("xattn_o_dw", aot, dx2b)
    dao = matmul("xattn_o_dx", dx2b, wo, tb=True)
    (dq,), (dkx, dvx) = fn_bwd("xattn_core_bwd", _attn, nt, [(q, tr, d, 0)], [kx, vx], [(dao, tr, d, 0)], lambda c: c,
                               [(t, tr, d, BF16)])
    grads["xattn_wq"] = matmul("xattn_q_dw", h2t, dq)
    dh2 = matmul("xattn_q_dx", dq, wq, tb=True)
    dkb, dvb = dkx.astype(BF16), dvx.astype(BF16)
    grads["xattn_wk"] = matmul("xattn_k_dw", mnt, dkb)
    grads["xattn_wv"] = matmul("xattn_v_dw", mnt, dvb)
    dmn = matmul("xattn_k_dx", dkb, wk, tb=True)
    dmn = matmul("xattn_v_dx", dvb, wv, tb=True, resid=dmn)
    _, _, grads["norm_mem_g"] = norm_bwd("norm_mem_bwd", mem, wt["norm_mem_g"], dmn, None, mrows)
    dx1, dx1b, grads["norm_x_g"] = norm_bwd("norm_x_bwd", x1, wt["norm_x_g"], dh2, dx2, tr)

    grads["w_out"] = matmul("out_proj_dw", ymix_t, dx1b)
    dymix = reducer.launch(1, grads, matmul("out_proj_dx", dx1b, w_out, tb=True))

    (dyr, dr1, dk1, dv1, dgate), (g_rk, grads["rwkv_ln_w"], grads["rwkv_ln_b"]) = fn_bwd(
        "rwkv_post_bwd", _rwkv_post, nt, rw_post_t, rw_post_f, [(dymix, tr, w, 1)], lambda c: c, [(t, tr, w, F32)] * 5)
    grads["rwkv_r_k"] = g_rk.reshape(wt["rwkv_r_k"].shape)
    (dr2, dlw, dk2, dv2, dnkk, db), _ = scan_bwd("rwkv_scan_bwd", rw_fn, RWKV_CHUNK, rw_seq, [], rw_states, dyr, n_pairs, rw_ppb)
    rw_ct = [(a, tr, w, 0) for a in (dr1, dr2, dlw, dk1, dk2, dv1, dv2, dnkk, db, dgate)]

    def rw_ct_fn(c):
        return (c[0] + c[1], c[2], c[3] + c[4], c[5] + c[6], c[7], c[8], c[9])

    (durkv, dulora, dhrkv, dhlora), rw_pg = fn_bwd(
        "rwkv_pre_bwd", _rwkv_pre, nt, rw_pre_t, rw_pre_f, rw_ct, rw_ct_fn,
        [(t, tr, 3 * w, F32), (t, tr, lora_w, F32), (nt * HALO, HALO, 3 * w, F32), (nt * HALO, HALO, lora_w, F32)])
    durkv = _unhalo(durkv, dhrkv, tr)
    dulora = _unhalo(dulora, dhlora, tr)
    g_mu_rkv, g_mu_lora, grads["rwkv_w0"], grads["rwkv_a0"], grads["rwkv_k_k"], grads["rwkv_k_a"], g_w2p, g_a2p, grads["rwkv_g2"] = rw_pg
    grads["rwkv_mu"] = jnp.concatenate([g_mu_rkv, g_mu_lora[:, :dr], g_mu_lora[:, LANES:LANES + ar], g_mu_lora[:, 2 * LANES:]], axis=1)
    grads["rwkv_w2"] = g_w2p[:dr]
    grads["rwkv_a2"] = g_a2p[:ar]

    (dys, dxs1, dz), (g_d, grads["ssd_norm_g"]) = fn_bwd(
        "ssd_post_bwd", _ssd_post, nt, ssd_post_t, ssd_post_f, [(dymix, tr, w, 0)], lambda c: c, [(t, tr, w, F32)] * 3)
    grads["ssd_d"] = g_d[:, :nh]
    (dxs2, dbp, dcp, ddtp), (g_alog,) = scan_bwd("ssd_scan_bwd", ssd_fn, SSD_CHUNK, ssd_seq, [a_log], ssd_states, dys, n_pairs, ssd_ppb)
    grads["ssd_a_log"] = g_alog[:, :nh]
    ssd_ct = [(dxs1, tr, w, 0), (dxs2, tr, w, 0), (dbp, tr, w, 0), (dcp, tr, w, 0), (ddtp, tr, w, 0)]

    def ssd_ct_fn(c):
        def group_sum(a):
            parts = []
            for gi in range(SSD_GROUPS):
                s = a[:, gi * ppg * LANES:(gi * ppg + 1) * LANES]
                for j in range(1, ppg):
                    s = s + a[:, (gi * ppg + j) * LANES:(gi * ppg + j + 1) * LANES]
                parts.append(s)
            return parts
        ddt = c[4][:, :LANES]
        for j in range(1, n_pairs):
            ddt = ddt + c[4][:, j * LANES:(j + 1) * LANES]
        return (jnp.concatenate([c[0] + c[1]] + group_sum(c[2]) + group_sum(c[3]), axis=1), ddt)

    (dxbc, dhxbc, ddtraw), ssd_pg = fn_bwd(
        "ssd_pre_bwd", _ssd_pre, nt, ssd_pre_t, ssd_pre_f, ssd_ct, ssd_ct_fn,
        [(t, tr, conv_dim, F32), (nt * HALO, HALO, conv_dim, F32), (t, tr, LANES, F32)])
    dxbc = _unhalo(dxbc, dhxbc, tr)
    grads["ssd_conv_w"] = jnp.concatenate(ssd_pg[:SSD_CONV], axis=0)
    grads["ssd_conv_b"] = ssd_pg[SSD_CONV]
    grads["ssd_dt_bias"] = ssd_pg[SSD_CONV + 1][:, :nh]

    du = jnp.concatenate([dz, dxbc, ddtraw, durkv, dulora], axis=1).astype(BF16)
    g_perm_t = matmul("in_proj_dw", du.T, h1)
    grads["w_in"] = jnp.concatenate([g_perm_t[offs[nm]:offs[nm] + segs[nm][1]] for nm in order], axis=0)
    dh1 = matmul("in_proj_dx", du, w_perm_t)
    dh1 = reducer.launch(2, grads, dh1)
    grad_x, _, grads["norm_mix_g"] = norm_bwd("norm_mix_bwd", x, wt["norm_mix_g"], dh1, dx1, tr)
    return loss_tile, grad_x, grads


def _pack(arrs):
    flat = jnp.concatenate([a.reshape(-1) for a in arrs])
    n = flat.shape[0]
    rows = -(-n // (8 * LANES)) * 8
    return jnp.pad(flat, (0, rows * LANES - n)).reshape(rows, LANES)


def _unpack(packed, shapes):
    flat = packed.reshape(-1)
    out, o = [], 0
    for s in shapes:
        n = math.prod(s)
        out.append(flat[o:o + n].reshape(s))
        o += n
    return out


def _as2d(a):
    return a.reshape(-1, a.shape[-1])


def _shard_view(n, a):
    return _as2d(a[0]).T if n in TRANSPOSED else _as2d(a[0])


class _GatheredWeights:
    def __init__(self, shard2d, q, c):
        self.shard2d, self.q, self.c = shard2d, q, c
        self.raw, self.ready = {}, {}

    def start(self, gi, after):
        shards = [self.shard2d[n].astype(BF16) for n in GATHER_GROUPS[gi]]
        if after is not None:
            shards, _ = lax.optimization_barrier((shards, after))
        gathered = gather_two_level("gather_weights_%d" % gi, shards, gi + 1)
        for n, sh, g in zip(GATHER_GROUPS[gi], shards, gathered):
            self.raw[n] = (sh, g)

    def get(self, name, after):
        if name not in self.ready:
            sh, g = self.raw[name]
            if after is not None:
                g, _ = lax.optimization_barrier((g, after))
            rows, cols = sh.shape
            at = (self.q, 0, self.c * (cols // 2)) if _halves_by_cols(rows) else (self.q, self.c * (rows // 2), 0)
            g = lax.dynamic_update_slice(g, _half_value(sh, self.c)[None], at)
            self.ready[name] = _from_slots(g, BIG_AXIS[name])
        return self.ready[name]


class _GradReducer:
    def __init__(self, q, c, update):
        self.q, self.c, self.update = q, c, update
        self.pending, self.updated = {}, {}

    def launch(self, gi, grads, nxt):
        names = REDUCE_GROUPS[gi]
        kept, sent, rows = [], [], []
        for n in names:
            s = _to_slots(grads[n], REDUCE_AXIS[n])
            rows.append(s.shape[1])
            kept.append(_half_value(s, self.c))
            sent.append(_half_value(s, 1 - self.c).astype(BF16))
        got = core_swap("swap_halves_%d" % gi, sent)
        parts = []
        for n, k, g in zip(names, kept, got):
            _, hr, hc = k.shape
            (part,) = ew_call("chip_sum_" + n, lambda kv, gv: (kv + gv.astype(F32),),
                              [k.reshape(4 * hr, hc), g.reshape(4 * hr, hc)], [BF16])
            parts.append(part.reshape(4, hr, hc))
        parts, nxt = lax.optimization_barrier((parts, nxt))
        slots = scatter_slots("scatter_grads_%d" % gi, parts, len(GATHER_GROUPS) + 1 + gi)
        self.pending[gi] = (parts, slots, rows)
        return self.finish(gi - 1, nxt) if gi > 0 else nxt

    def finish(self, gi, nxt):
        names = REDUCE_GROUPS[gi]
        parts, slots, rows = self.pending[gi]
        halves = []
        for n, p, s in zip(names, parts, slots):
            own = lax.dynamic_index_in_dim(p, self.q, axis=0, keepdims=True)
            halves.append(sum_slots("sum_" + n, lax.dynamic_update_slice(s, own, (self.q, 0, 0))))
        others = core_swap("swap_reduced_%d" % gi, halves)
        lo = [jnp.where(self.c == 0, mine, other) for mine, other in zip(halves, others)]
        hi = [jnp.where(self.c == 0, other, mine) for mine, other in zip(halves, others)]
        results = [self.update(n, _join_halves(l, h, r)) for n, l, h, r in zip(names, lo, hi, rows)]
        if nxt is not None:
            results, nxt = lax.optimization_barrier((results, nxt))
        self.updated.update(zip(names, results))
        return nxt


def _step(a):
    x, mem, tgt = a["x"][0], a["mem"][0], a["loss_target"][0]
    q = 2 * lax.axis_index("x") + lax.axis_index("y")

    shard2d = {n: _shard_view(n, a[n]) for n in BIG}
    small_sh = {n: _as2d(a[n][0]) for n in SMALL_SHARDED}
    c = lax.axis_index("c")
    full = {}
    big = _GatheredWeights(shard2d, q, c)
    gathered = gather_shards("gather_small", [small_sh[n] for n in SMALL_SHARDED])
    for n, g in zip(SMALL_SHARDED, gathered):
        full[n] = _from_slots(g, 1)

    wt = {n: (a[n] if a[n].ndim <= 2 else a[n][0]) for n in WEIGHTS if n not in BIG and n not in SMALL_SHARDED}
    for n in SMALL_SHARDED:
        wt[n] = small_sh[n]
    shards = dict(shard2d)
    shards.update({n: small_sh[n] for n in REDUCED if n not in BIG})

    def update(n, gsum):
        return adamw("adamw_" + n, shards[n], _shard_view(n, a["m_" + n]), _shard_view(n, a["v_" + n]), gsum)

    reducer = _GradReducer(q, c, update)
    loss_tile, grad_x, grads = _local_grads(x, mem, tgt, wt, full, big, reducer)
    reducer.finish(len(REDUCE_GROUPS) - 1, None)
    out = {}
    for n, vals in reducer.updated.items():
        for key, val in zip(("grad_", "delta_", "new_m_", "new_v_"), vals):
            out[key + n] = (val.T if n in TRANSPOSED else val).reshape(a[n].shape)

    small = [n for n in WEIGHTS if n not in REDUCED]
    red = _unpack(all_reduce_small("all_reduce_small", _pack([grads[n] for n in small])), [grads[n].shape for n in small])
    g_loc = {}
    for n, g in zip(small, red):
        if n in SMALL_SHARDED:
            cols = g.shape[1] // 4
            g = lax.dynamic_slice_in_dim(g, q * cols, cols, axis=1)
        g_loc[n] = g.reshape(a[n].shape)
    res = adamw("adamw_small", *[_pack([src[n] for n in small]) for src in
                                 ({n: a[n] for n in small}, {n: a["m_" + n] for n in small}, {n: a["v_" + n] for n in small})],
                _pack([g_loc[n] for n in small]))
    shapes = [a[n].shape for n in small]
    for key, packed in zip(("grad_", "delta_", "new_m_", "new_v_"), res):
        for n, val in zip(small, _unpack(packed, shapes)):
            out[key + n] = val

    loss = lax.psum(loss_tile[0, 0], ("x", "y", "c"))
    ordered = [loss, grad_x.reshape(a["x"].shape)]
    for key in ("grad_", "delta_", "new_m_", "new_v_"):
        ordered += [out[key + n] for n in WEIGHTS]
    return tuple(ordered)


def kernel(x, mem, norm_mix_g, w_in, ssd_conv_w, ssd_conv_b, ssd_dt_bias, ssd_a_log, ssd_d, ssd_norm_g, rwkv_mu, rwkv_w0, rwkv_w2, rwkv_a0, rwkv_a2, rwkv_g2, rwkv_k_k, rwkv_k_a, rwkv_r_k, rwkv_ln_w, rwkv_ln_b, w_out, norm_x_g, norm_mem_g, xattn_wq, xattn_wk, xattn_wv, xattn_wo, norm_ffn_g, ffn_w1, ffn_w2, final_norm_g, loss_target, m_norm_mix_g, m_w_in, m_ssd_conv_w, m_ssd_conv_b, m_ssd_dt_bias, m_ssd_a_log, m_ssd_d, m_ssd_norm_g, m_rwkv_mu, m_rwkv_w0, m_rwkv_w2, m_rwkv_a0, m_rwkv_a2, m_rwkv_g2, m_rwkv_k_k, m_rwkv_k_a, m_rwkv_r_k, m_rwkv_ln_w, m_rwkv_ln_b, m_w_out, m_norm_x_g, m_norm_mem_g, m_xattn_wq, m_xattn_wk, m_xattn_wv, m_xattn_wo, m_norm_ffn_g, m_ffn_w1, m_ffn_w2, m_final_norm_g, v_norm_mix_g, v_w_in, v_ssd_conv_w, v_ssd_conv_b, v_ssd_dt_bias, v_ssd_a_log, v_ssd_d, v_ssd_norm_g, v_rwkv_mu, v_rwkv_w0, v_rwkv_w2, v_rwkv_a0, v_rwkv_a2, v_rwkv_g2, v_rwkv_k_k, v_rwkv_k_a, v_rwkv_r_k, v_rwkv_ln_w, v_rwkv_ln_b, v_w_out, v_norm_x_g, v_norm_mem_g, v_xattn_wq, v_xattn_wk, v_xattn_wv, v_xattn_wo, v_norm_ffn_g, v_ffn_w1, v_ffn_w2, v_final_norm_g):
    return _step(dict(locals()))
```

```python
import functools
import math

import jax
import jax.numpy as jnp
from jax import lax
from jax.experimental import pallas as pl
from jax.experimental.pallas import tpu as pltpu
from jax.experimental.pallas import tpu_sc as plsc

F32 = jnp.float32
BF16 = jnp.bfloat16
HIGHEST = lax.Precision.HIGHEST
MESH_ID = pl.DeviceIdType.MESH

NORM_EPS = 1e-6
RWKV_LN_EPS = 64e-5
HEAD_DIM = 64
PAIR = 2 * HEAD_DIM
LANES = 128
SSD_STATE = 128
SSD_CHUNK = 128
SSD_GROUPS = 2
SSD_CONV = 4
RWKV_CHUNK = 64
HALO = 8
ROW_TILE = 128
PAIRS_PER_STEP = 4
XATTN_HEADS = 4
RWKV_PASSES = 1
VMEM_LIMIT = 56 * 1024 * 1024
MATMUL_VMEM = 40 * 1024 * 1024

ADAM_LR = 0.001
ADAM_B1 = 0.9
ADAM_B2 = 0.999
ADAM_EPS = 1e-08
ADAM_WD = 0.01
ADAM_STEP = 10


def _dims(ca, cb):
    return (((ca,), (cb,)), ((), ()))


def _split_bf16(a):
    hi = a.astype(BF16)
    lo = (a - hi.astype(F32)).astype(BF16)
    return hi, lo


def _mm_impl(a, b, ca, cb, passes):
    dn = _dims(ca, cb)
    if passes == 1:
        return lax.dot_general(a.astype(BF16), b.astype(BF16), dn, preferred_element_type=F32)
    ah, al = _split_bf16(a)
    bh, bl = _split_bf16(b)
    out = lax.dot_general(ah, bh, dn, preferred_element_type=F32)
    out = out + lax.dot_general(ah, bl, dn, preferred_element_type=F32)
    return out + lax.dot_general(al, bh, dn, preferred_element_type=F32)


@functools.partial(jax.custom_vjp, nondiff_argnums=(2, 3, 4))
def mm(a, b, ca, cb, passes):
    return _mm_impl(a, b, ca, cb, passes)


def _mm_fwd(a, b, ca, cb, passes):
    return _mm_impl(a, b, ca, cb, passes), (a, b)


def _mm_bwd(ca, cb, passes, res, g):
    a, b = res
    da = mm(g, b, 1, 1 - cb, passes) if ca == 1 else mm(b, g, 1 - cb, 1, passes)
    db = mm(a, g, 1 - ca, 0, passes) if cb == 0 else mm(g, a, 0, 1 - ca, passes)
    return da, db


mm.defvjp(_mm_fwd, _mm_bwd)


def _dot_exact(a, b):
    return lax.dot_general(a, b, _dims(1, 0), precision=HIGHEST, preferred_element_type=F32)


def _iota(shape, dim):
    return lax.broadcasted_iota(jnp.int32, shape, dim)


def _sigmoid(x):
    return 1.0 / (1.0 + jnp.exp(-x))


def _silu(x):
    return x * _sigmoid(x)


def _softplus(x):
    return jnp.maximum(x, 0.0) + jnp.log(1.0 + jnp.exp(-jnp.abs(x)))


def _rms(x, g):
    return x * lax.rsqrt(jnp.mean(x * x, axis=-1, keepdims=True) + NORM_EPS) * g


def _select_mm(x, sel):
    hi = x.astype(BF16)
    r1 = x - hi.astype(F32)
    mid = r1.astype(BF16)
    lo = (r1 - mid.astype(F32)).astype(BF16)
    dn = _dims(1, 0)
    out = lax.dot_general(hi, sel, dn, preferred_element_type=F32)
    out = out + lax.dot_general(mid, sel, dn, preferred_element_type=F32)
    return out + lax.dot_general(lo, sel, dn, preferred_element_type=F32)


def _head_sum_impl(x, n):
    sel = (_iota((n, LANES), 0) // HEAD_DIM == _iota((n, LANES), 1)).astype(BF16)
    return _select_mm(x, sel)


def _head_expand_impl(s, n):
    sel = (_iota((LANES, n), 1) // HEAD_DIM == _iota((LANES, n), 0)).astype(BF16)
    return _select_mm(s, sel)


@functools.partial(jax.custom_vjp, nondiff_argnums=(1,))
def _head_sum_n(x, n):
    return _head_sum_impl(x, n)


@functools.partial(jax.custom_vjp, nondiff_argnums=(1,))
def _head_expand(s, n):
    return _head_expand_impl(s, n)


_head_sum_n.defvjp(lambda x, n: (_head_sum_impl(x, n), None), lambda n, _, g: (_head_expand(g, n),))
_head_expand.defvjp(lambda s, n: (_head_expand_impl(s, n), None), lambda n, _, g: (_head_sum_n(g, n),))


def _head_sum(x):
    return _head_sum_n(x, x.shape[1])


def _row_vector_expand(v, n):
    v8 = jnp.broadcast_to(v, (8, LANES))
    return jnp.sum(_head_expand(v8, n), axis=0, keepdims=True) * 0.125


def _shift_rows_impl(u, halo, s):
    rolled = pltpu.roll(u, s, 0)
    top = jnp.where(_iota((HALO, 1), 0) < s, pltpu.roll(halo, s, 0), rolled[:HALO])
    return jnp.concatenate([top, rolled[HALO:]], axis=0)


@functools.partial(jax.custom_vjp, nondiff_argnums=(2,))
def _shift_rows(u, halo, s):
    return _shift_rows_impl(u, halo, s)


def _shift_rows_bwd(s, _, g):
    tr = g.shape[0]
    rolled = pltpu.roll(g, tr - s, 0)
    hrow = _iota((HALO, 1), 0)
    bottom = jnp.where(hrow < HALO - s, rolled[tr - HALO:], 0.0)
    dhalo = jnp.where(hrow >= HALO - s, pltpu.roll(g[:HALO], HALO - s, 0), 0.0)
    return jnp.concatenate([rolled[:tr - HALO], bottom], axis=0), dhalo


_shift_rows.defvjp(lambda u, halo, s: (_shift_rows_impl(u, halo, s), None), _shift_rows_bwd)


def _params(sem):
    return pltpu.CompilerParams(dimension_semantics=sem, vmem_limit_bytes=VMEM_LIMIT)


def row_call(name, body, n_tiles, tiled, full, out_tiled, out_acc, transposed=()):
    nt, nf, na = len(tiled), len(full), len(out_acc)
    n_plain = len(out_tiled)
    no = n_plain + len(transposed)

    def kern(*refs):
        tv = [r[...] for r in refs[:nt]]
        fv = [r[...] for r in refs[nt:nt + nf]]
        outs, accs = body(tv, fv)
        for r, v in zip(refs[nt + nf:nt + nf + n_plain], outs):
            r[...] = v.astype(r.dtype)
        for r, idx in zip(refs[nt + nf + n_plain:nt + nf + no], transposed):
            r[...] = outs[idx].astype(F32).T.astype(r.dtype)
        if na:
            a_refs = refs[nt + nf + no:]
            first = pl.program_id(0) == 0

            @pl.when(first)
            def _():
                for r, v in zip(a_refs, accs):
                    r[...] = v

            @pl.when(jnp.logical_not(first))
            def _():
                for r, v in zip(a_refs, accs):
                    r[...] += v

    in_specs = [pl.BlockSpec((rt, w), functools.partial(lambda i, cb: (i, cb), cb=cb)) for (_, rt, w, cb) in tiled]
    in_specs += [pl.BlockSpec(a.shape, lambda i: (0, 0)) for a in full]
    out_specs = [pl.BlockSpec((rt, w), lambda i: (i, 0)) for (_, rt, w, _) in out_tiled]
    out_specs += [pl.BlockSpec((out_tiled[idx][2], out_tiled[idx][1]), lambda i: (0, i)) for idx in transposed]
    out_specs += [pl.BlockSpec(s, lambda i: (0, 0)) for s in out_acc]
    out_shape = [jax.ShapeDtypeStruct((rows, w), dt) for (rows, _, w, dt) in out_tiled]
    out_shape += [jax.ShapeDtypeStruct((out_tiled[idx][2], out_tiled[idx][0]), BF16) for idx in transposed]
    out_shape += [jax.ShapeDtypeStruct(s, F32) for s in out_acc]
    res = pl.pallas_call(
        kern, name=name, grid=(n_tiles,), in_specs=in_specs, out_specs=out_specs, out_shape=out_shape,
        compiler_params=_params(("arbitrary",)),
    )(*[t[0] for t in tiled], *full)
    return list(res[:no]), list(res[no:])


def _pick(dim, cands):
    for c in cands:
        if dim % c == 0:
            return c
    return dim


def matmul(name, a, b, tb=False, resid=None, out_dtype=F32):
    m, k = a.shape
    n = b.shape[0] if tb else b.shape[1]
    has_resid = resid is not None
    out_bytes = jnp.dtype(out_dtype).itemsize
    sizes = (2048, 1024, 896, 768, 512, 384, 256, 128)
    tm = _pick(m, sizes[1:])
    tn = _pick(n, sizes[1:])

    def vmem_bytes(tk):
        return 2 * 2 * tk * (tm + tn) + tm * tn * (2 * out_bytes + 4 + (8 if has_resid else 0))

    tk = next((c for c in sizes if k % c == 0 and vmem_bytes(c) <= MATMUL_VMEM), LANES)
    nk = k // tk

    def kern(*refs):
        a_ref, b_ref = refs[0], refs[1]
        o_ref, acc = refs[-2], refs[-1]
        kk = pl.program_id(2)
        part = lax.dot_general(a_ref[...], b_ref[...], _dims(1, 1 if tb else 0), preferred_element_type=F32)

        def finish(out):
            if has_resid:
                out = out + refs[2][...]
            o_ref[...] = out.astype(o_ref.dtype)

        if nk == 1:
            finish(part)
            return

        @pl.when(kk == 0)
        def _():
            acc[...] = part

        @pl.when(jnp.logical_and(kk > 0, kk < nk - 1))
        def _():
            acc[...] += part

        @pl.when(kk == nk - 1)
        def _():
            finish(acc[...] + part)

    in_specs = [pl.BlockSpec((tm, tk), lambda i, j, kk: (i, kk))]
    if tb:
        in_specs.append(pl.BlockSpec((tn, tk), lambda i, j, kk: (j, kk)))
    else:
        in_specs.append(pl.BlockSpec((tk, tn), lambda i, j, kk: (kk, j)))
    args = [a, b]
    if has_resid:
        in_specs.append(pl.BlockSpec((tm, tn), lambda i, j, kk: (i, j)))
        args.append(resid)
    return pl.pallas_call(
        kern, name=name, grid=(m // tm, n // tn, nk), in_specs=in_specs,
        out_specs=pl.BlockSpec((tm, tn), lambda i, j, kk: (i, j)),
        out_shape=jax.ShapeDtypeStruct((m, n), out_dtype),
        scratch_shapes=[pltpu.VMEM((tm, tn), F32)],
        compiler_params=_params(("parallel", "parallel", "arbitrary")),
    )(*args)


def norm_fwd(name, x, g, tr, with_transpose=True):
    def body(tv, fv):
        return [_rms(tv[0], fv[0])], []
    rows, d = x.shape
    outs, _ = row_call(name, body, rows // tr, [(x, tr, d, 0)], [g], [(rows, tr, d, BF16)], [],
                       transposed=(0,) if with_transpose else ())
    return outs[0], (outs[1] if with_transpose else None)


def norm_bwd(name, x, g, dh, extra, tr):
    def body(tv, fv):
        _, vjp = jax.vjp(_rms, tv[0], fv[0])
        dx, dg = vjp(tv[1])
        if extra is not None:
            dx = dx + tv[2]
        return [dx, dx], [dg]
    rows, d = x.shape
    tiled = [(x, tr, d, 0), (dh, tr, d, 0)] + ([(extra, tr, d, 0)] if extra is not None else [])
    (dx, dxb), (dg,) = row_call(name, body, rows // tr, tiled, [g], [(rows, tr, d, F32), (rows, tr, d, BF16)], [g.shape])
    return dx, dxb, dg


def _ssd_pre(xbc, halo, dtraw, w0, w1, w2, w3, cb, dtb):
    y = w3 * xbc + w2 * _shift_rows(xbc, halo, 1) + w1 * _shift_rows(xbc, halo, 2) + w0 * _shift_rows(xbc, halo, 3) + cb
    return _silu(y), _softplus(dtraw + dtb)


def _ssd_post(ys, xs, z, dskip, ng):
    w = ys.shape[1]
    y = (ys + xs * _row_vector_expand(dskip, w)) * _silu(z)
    gw = w // SSD_GROUPS
    parts = []
    for gi in range(SSD_GROUPS):
        yg = y[:, gi * gw:(gi + 1) * gw]
        parts.append(yg * lax.rsqrt(jnp.mean(yg * yg, axis=-1, keepdims=True) + NORM_EPS))
    return jnp.concatenate(parts, axis=1) * ng


def _rwkv_pre(urkv, ulora, hrkv, hlora, mu_rkv, mu_lora, w0, a0, kkw, kaw, w2p, a2p, g2):
    w = w0.shape[1]
    urkv = urkv + (_shift_rows(urkv, hrkv, 1) - urkv) * mu_rkv
    ulora = ulora + (_shift_rows(ulora, hlora, 1) - ulora) * mu_lora
    r, k, v = urkv[:, :w], urkv[:, w:2 * w], urkv[:, 2 * w:]
    pw, pa, pg = ulora[:, :LANES], ulora[:, LANES:2 * LANES], ulora[:, 2 * LANES:]
    w_log = -_softplus(-(w0 + mm(jnp.tanh(pw), w2p, 1, 0, 1))) - 0.5
    lw = -jnp.exp(w_log)
    iclr = _sigmoid(a0 + mm(pa, a2p, 1, 0, 1))
    gate = mm(_sigmoid(pg), g2, 1, 0, 1)
    kk = k * kkw
    kk = kk / jnp.maximum(jnp.sqrt(_head_expand(_head_sum(kk * kk), w)), 1e-12)
    k2 = k * (1.0 + (iclr - 1.0) * kaw)
    return r, lw, k2, v, -kk, kk * iclr, gate


def _rwkv_post(ys, r, k2, v, gate, rk, lnw, lnb):
    w = ys.shape[1]
    inv = 1.0 / HEAD_DIM
    mean = _head_expand(_head_sum(ys), w) * inv
    d = ys - mean
    var = _head_expand(_head_sum(d * d), w) * inv
    yn = d * lax.rsqrt(var + RWKV_LN_EPS) * lnw + lnb
    bonus = _head_expand(_head_sum(r * k2 * rk), w) * v
    return (yn + bonus) * gate


def _attn(q, k, v):
    d = q.shape[1]
    hd = d // XATTN_HEADS
    outs = []
    for h in range(XATTN_HEADS):
        sl = slice(h * hd, (h + 1) * hd)
        s = mm(q[:, sl], k[:, sl], 1, 1, 1) * (hd ** -0.5)
        s = s - jnp.max(s, axis=-1, keepdims=True)
        p = jnp.exp(s)
        p = p / jnp.sum(p, axis=-1, keepdims=True)
        outs.append(mm(p, v[:, sl], 1, 0, 1))
    return jnp.concatenate(outs, axis=1)


def _relu2(a):
    return jnp.square(jnp.maximum(a.astype(F32), 0.0))


def fn_fwd(name, fn, n_tiles, tiled, full, out_tiled, transposed=()):
    def body(tv, fv):
        outs = fn(*tv, *fv)
        return (list(outs) if isinstance(outs, (tuple, list)) else [outs]), []
    outs, _ = row_call(name, body, n_tiles, tiled, full, out_tiled, [], transposed)
    return outs


def fn_bwd(name, fn, n_tiles, tiled, full, cts, ct_fn, out_tiled):
    nt = len(tiled)

    def body(tv, fv):
        outs, vjp = jax.vjp(fn, *tv[:nt], *fv)
        ct = ct_fn(tv[nt:])
        grads = vjp(tuple(ct) if isinstance(outs, (tuple, list)) else ct[0])
        return list(grads[:nt]), list(grads[nt:])
    return row_call(name, body, n_tiles, tiled + cts, full, out_tiled, [f.shape for f in full])


def _ssd_chunk(xs, bm, cm, dt_all, a_log, ht, p):
    q = xs.shape[0]
    lane = _iota((1, LANES), 1)
    row = _iota((q, 1), 0)
    tril = _iota((q, q), 0) >= _iota((q, q), 1)
    half = lane < HEAD_DIM
    da = dt_all * (-jnp.exp(a_log))
    cs = _dot_exact(tril.astype(F32), da)

    def col(mat, h):
        return jnp.sum(jnp.where(lane == h, mat, 0.0), axis=1, keepdims=True)

    cs0, cs1 = col(cs, 2 * p), col(cs, 2 * p + 1)
    xdt = xs * jnp.where(half, col(dt_all, 2 * p), col(dt_all, 2 * p + 1))
    csx = jnp.where(half, cs0, cs1)
    last = jnp.sum(jnp.where(row == q - 1, csx, 0.0), axis=0, keepdims=True)
    cb = mm(cm, bm, 1, 1, 1)
    y = mm(cm, ht, 1, 0, 1) * jnp.exp(csx)
    for csh, hm in ((cs0, half), (cs1, jnp.logical_not(half))):
        csl = jnp.broadcast_to(csh, (q, q))
        seg = csl - csl.T
        lmat = jnp.where(tril, jnp.exp(jnp.where(tril, seg, 0.0)), 0.0)
        y = y + jnp.where(hm, mm(cb * lmat, xdt, 1, 0, 1), 0.0)
    st = mm(bm, xdt * jnp.exp(last - csx), 0, 0, 1)
    return y, ht * jnp.exp(last) + st


def _rwkv_chunks(pairs):
    c = pairs[0][0].shape[0]
    ps = RWKV_PASSES
    lane = _iota((1, LANES), 1)
    row = _iota((c, 1), 0)
    ri, ci = _iota((c, c), 0), _iota((c, c), 1)
    tril_i, tril_s = ri >= ci, ri > ci
    eye = (ri == ci).astype(F32)
    half = lane < HEAD_DIM
    halves = (half, jnp.logical_not(half))
    bd = (_iota((LANES, LANES), 0) < HEAD_DIM) == (_iota((LANES, LANES), 1) < HEAD_DIM)
    tri = tril_i.astype(F32)
    n = len(pairs)
    heads = [(j, hm) for j in range(n) for hm in halves]

    cum = [_dot_exact(tri, p[1]) for p in pairs]
    at = [p[4] * jnp.exp(cm - p[1]) for p, cm in zip(pairs, cum)]
    en = [jnp.exp(-cm) for cm in cum]
    bt = [p[5] * e for p, e in zip(pairs, en)]
    kt = [p[2] * e for p, e in zip(pairs, en)]
    rt = [p[0] * jnp.exp(cm) for p, cm in zip(pairs, cum)]
    ah = [mm(at[j], pairs[j][6], 1, 1, ps) for j in range(n)]
    y = [mm(rt[j], pairs[j][6], 1, 1, ps) for j in range(n)]
    atm = [jnp.where(hm, at[j], 0.0) for j, hm in heads]
    rtm = [jnp.where(hm, rt[j], 0.0) for j, hm in heads]
    aab = [jnp.where(tril_s, mm(atm[i], bt[j], 1, 1, ps), 0.0) for i, (j, _) in enumerate(heads)]
    aak = [jnp.where(tril_s, mm(atm[i], kt[j], 1, 1, ps), 0.0) for i, (j, _) in enumerate(heads)]
    arb = [jnp.where(tril_i, mm(rtm[i], bt[j], 1, 1, ps), 0.0) for i, (j, _) in enumerate(heads)]
    ark = [jnp.where(tril_i, mm(rtm[i], kt[j], 1, 1, ps), 0.0) for i, (j, _) in enumerate(heads)]
    rhs = [ah[j] + mm(aak[i], pairs[j][3], 1, 0, ps) for i, (j, _) in enumerate(heads)]
    yv = [mm(ark[i], pairs[j][3], 1, 0, ps) for i, (j, _) in enumerate(heads)]
    tm = [eye + a_ for a_ in aab]
    pm = aab
    for _ in range(int(math.log2(c)) - 1):
        pm = [mm(p_, p_, 1, 0, ps) for p_ in pm]
        tm = [t_ + mm(t_, p_, 1, 0, ps) for t_, p_ in zip(tm, pm)]
    uh = [mm(tm[i], rhs[i], 1, 0, ps) for i in range(len(heads))]
    u = [jnp.where(half, uh[2 * j], uh[2 * j + 1]) for j in range(n)]
    yu = [mm(arb[i], u[j], 1, 0, ps) for i, (j, _) in enumerate(heads)]
    out = []
    for j in range(n):
        yj = y[j] + jnp.where(half, yu[2 * j] + yv[2 * j], yu[2 * j + 1] + yv[2 * j + 1])
        plast = jnp.sum(jnp.where(row == c - 1, cum[j], 0.0), axis=0, keepdims=True)
        upd = pairs[j][6] + mm(u[j], bt[j], 0, 0, ps) + mm(pairs[j][3], kt[j], 0, 0, ps)
        out.append((yj, jnp.where(bd, upd * jnp.exp(plast), 0.0)))
    return out


def _seq_spec(chunk, ppb, col, row_of):
    if col is None:
        return pl.BlockSpec((chunk, ppb * LANES), lambda pb, i: (row_of(i), pb))
    return pl.BlockSpec((chunk, LANES), lambda pb, i: (row_of(i), col(pb * ppb)))


def _pair_vals(refs, seq_in, j):
    return [r[...] if col is not None else r[:, j * LANES:(j + 1) * LANES] for r, (_, col) in zip(refs, seq_in)]


def scan_fwd(name, chunk_fn, chunk, seq_in, const_in, n_pairs, ppb):
    t = seq_in[0][0].shape[0]
    nc = t // chunk
    ns, ncst = len(seq_in), len(const_in)

    def kern(*refs):
        y_ref, st_ref, ht = refs[ns + ncst], refs[ns + ncst + 1], refs[ns + ncst + 2]

        @pl.when(pl.program_id(1) == 0)
        def _():
            ht[...] = jnp.zeros_like(ht)

        cv = [r[...] for r in refs[ns:ns + ncst]]
        h0 = [ht[j] for j in range(ppb)]
        for j in range(ppb):
            st_ref[j] = h0[j]
        sv = [_pair_vals(refs[:ns], seq_in, j) for j in range(ppb)]
        outs = chunk_fn(sv, cv, h0, [pl.program_id(0) * ppb + j for j in range(ppb)])
        for j, (y, hn) in enumerate(outs):
            y_ref[:, j * LANES:(j + 1) * LANES] = y
            ht[j] = hn

    in_specs = [_seq_spec(chunk, ppb, col, lambda i: i) for (_, col) in seq_in]
    in_specs += [pl.BlockSpec(a.shape, lambda pb, i: (0, 0)) for a in const_in]
    return pl.pallas_call(
        kern, name=name, grid=(n_pairs // ppb, nc), in_specs=in_specs,
        out_specs=[pl.BlockSpec((chunk, ppb * LANES), lambda pb, i: (i, pb)),
                   pl.BlockSpec((ppb, None, LANES, LANES), lambda pb, i: (pb, i, 0, 0))],
        out_shape=[jax.ShapeDtypeStruct((t, n_pairs * LANES), F32), jax.ShapeDtypeStruct((n_pairs, nc, LANES, LANES), F32)],
        scratch_shapes=[pltpu.VMEM((ppb, LANES, LANES), F32)],
        compiler_params=_params(("arbitrary", "arbitrary")),
    )(*[s[0] for s in seq_in], *const_in)


def scan_bwd(name, chunk_fn, chunk, seq_in, const_in, states, dy, n_pairs, ppb):
    t = dy.shape[0]
    nc = t // chunk
    ns, ncst = len(seq_in), len(const_in)

    def kern(*refs):
        seq_refs, cst_refs = refs[:ns], refs[ns:ns + ncst]
        st_ref, dy_ref = refs[ns + ncst], refs[ns + ncst + 1]
        o = ns + ncst + 2
        dseq_refs, dcst_refs, dht = refs[o:o + ns], refs[o + ns:o + ns + ncst], refs[o + ns + ncst]
        pb, i = pl.program_id(0), pl.program_id(1)

        @pl.when(i == 0)
        def _():
            dht[...] = jnp.zeros_like(dht)

        ids = [pb * ppb + j for j in range(ppb)]
        lanes = [slice(j * LANES, (j + 1) * LANES) for j in range(ppb)]

        def fn(*flat):
            sv = [list(flat[j * ns:(j + 1) * ns]) for j in range(ppb)]
            outs = chunk_fn(sv, list(flat[ppb * ns:ppb * ns + ncst]), list(flat[ppb * ns + ncst:]), ids)
            return tuple(y for y, _ in outs), tuple(h for _, h in outs)

        flat_in = [v for j in range(ppb) for v in _pair_vals(seq_refs, seq_in, j)]
        flat_in += [r[...] for r in cst_refs] + [st_ref[j] for j in range(ppb)]
        _, vjp = jax.vjp(fn, *flat_in)
        grads = vjp((tuple(dy_ref[:, ln] for ln in lanes), tuple(dht[j] for j in range(ppb))))
        for j in range(ppb):
            for r, g in zip(dseq_refs, grads[j * ns:(j + 1) * ns]):
                r[:, lanes[j]] = g
            dht[j] = grads[ppb * ns + ncst + j]
        dcv = grads[ppb * ns:ppb * ns + ncst]
        if ncst:
            first = jnp.logical_and(pb == 0, i == 0)

            @pl.when(first)
            def _():
                for r, g in zip(dcst_refs, dcv):
                    r[...] = g

            @pl.when(jnp.logical_not(first))
            def _():
                for r, g in zip(dcst_refs, dcv):
                    r[...] += g

    rev = lambda i: nc - 1 - i
    wide = pl.BlockSpec((chunk, ppb * LANES), lambda pb, i: (rev(i), pb))
    in_specs = [_seq_spec(chunk, ppb, col, rev) for (_, col) in seq_in]
    in_specs += [pl.BlockSpec(a.shape, lambda pb, i: (0, 0)) for a in const_in]
    in_specs += [pl.BlockSpec((ppb, None, LANES, LANES), lambda pb, i: (pb, rev(i), 0, 0)), wide]
    out_specs = [wide for _ in seq_in]
    out_specs += [pl.BlockSpec(a.shape, lambda pb, i: (0, 0)) for a in const_in]
    out_shape = [jax.ShapeDtypeStruct((t, n_pairs * LANES), F32) for _ in seq_in]
    out_shape += [jax.ShapeDtypeStruct(a.shape, F32) for a in const_in]
    res = pl.pallas_call(
        kern, name=name, grid=(n_pairs // ppb, nc), in_specs=in_specs, out_specs=out_specs, out_shape=out_shape,
        scratch_shapes=[pltpu.VMEM((ppb, LANES, LANES), F32)],
        compiler_params=_params(("arbitrary", "arbitrary")),
    )(*[s[0] for s in seq_in], *const_in, states, dy)
    return list(res[:ns]), list(res[ns:])


def loss_head(x3, tgt, g, tr):
    rows, d = x3.shape

    def body(tv, fv):
        def f(x, gg):
            e = jnp.square(_rms(x, gg) - tv[1])
            return 0.5 * jnp.sum(jnp.mean(e, axis=-1, keepdims=True), axis=0, keepdims=True)
        l, vjp = jax.vjp(f, tv[0], fv[0])
        dx, dg = vjp(jnp.ones((1, 1), F32))
        return [dx, dx], [dg, jnp.broadcast_to(l, (8, LANES))]
    (dx, dxb), (dg, l) = row_call("loss_head", body, rows // tr, [(x3, tr, d, 0), (tgt, tr, d, 0)], [g],
                                  [(rows, tr, d, F32), (rows, tr, d, BF16)], [g.shape, (8, LANES)])
    return dx, dxb, dg, l


def _adam_math(w, g, m, v):
    m = ADAM_B1 * m + (1.0 - ADAM_B1) * g
    v = ADAM_B2 * v + (1.0 - ADAM_B2) * jnp.square(g)
    m_hat = m / (1.0 - ADAM_B1 ** ADAM_STEP)
    v_hat = v / (1.0 - ADAM_B2 ** ADAM_STEP)
    delta = -ADAM_LR * (m_hat / (jnp.sqrt(v_hat) + ADAM_EPS) + ADAM_WD * w)
    return delta, m, v


def _tiling(rows, cols, limit):
    row_tile = max([d for d in range(16, rows + 1, 16) if rows % d == 0 and d * cols <= limit], default=0)
    col_tile = max([ct for ct in range(LANES, cols + 1, LANES) if cols % ct == 0 and rows * ct <= limit], default=0)
    if row_tile and row_tile * cols >= rows * col_tile:
        return row_tile, cols
    return (rows, col_tile) if col_tile else (rows, cols)


def ew_call(name, fn, ins, out_dtypes, limit=1 << 20):
    rows, cols = ins[0].shape
    br, bc = _tiling(rows, cols, limit)
    spec = pl.BlockSpec((br, bc), lambda i, j: (i, j))
    n_in = len(ins)

    def kern(*refs):
        for r, v in zip(refs[n_in:], fn(*[r[...] for r in refs[:n_in]])):
            r[...] = v.astype(r.dtype)

    return pl.pallas_call(
        kern, name=name, grid=(rows // br, cols // bc), in_specs=[spec] * n_in, out_specs=[spec] * len(out_dtypes),
        out_shape=[jax.ShapeDtypeStruct((rows, cols), dt) for dt in out_dtypes],
        compiler_params=_params(("parallel", "parallel")),
    )(*ins)


def adamw(name, w, m, v, g):
    return ew_call(name, lambda wv, mv, vv, gv: (gv, *_adam_math(wv, gv, mv, vv)), [w, m, v, g], [F32] * 4, 1 << 18)


def sum_slots(name, r):
    _, rows, cols = r.shape
    br, bc = _tiling(rows, cols, 1 << 20)

    def kern(r0, r1, r2, r3, o):
        o[...] = ((r0[...].astype(F32) + r1[...].astype(F32)) + r2[...].astype(F32)) + r3[...].astype(F32)

    in_specs = [pl.BlockSpec((None, br, bc), functools.partial(lambda i, j, s: (s, i, j), s=s)) for s in range(4)]
    return pl.pallas_call(
        kern, name=name, grid=(rows // br, cols // bc), in_specs=in_specs,
        out_specs=pl.BlockSpec((br, bc), lambda i, j: (i, j)),
        out_shape=jax.ShapeDtypeStruct((rows, cols), F32), compiler_params=_params(("parallel", "parallel")),
    )(r, r, r, r)


def _my_place():
    return lax.axis_index("x"), lax.axis_index("y"), lax.axis_index("c")


def _chip_peers(x, y):
    peers = [(1 - x, y), (x, 1 - y), (1 - x, 1 - y)]
    return peers, [2 * px + py for px, py in peers]


def gather_shards(name, arrays):
    nw = len(arrays)
    ANY = pl.BlockSpec(memory_space=pl.ANY)

    def body(*refs):
        ins, outs = refs[:nw], refs[nw:2 * nw]
        send, recv, loc = refs[2 * nw:]
        x, y, c = _my_place()
        q = 2 * x + y
        peers, chips = _chip_peers(x, y)

        def remote(w, j, slot):
            return pltpu.make_async_remote_copy(
                src_ref=ins[w], dst_ref=outs[w].at[slot], send_sem=send.at[w, j], recv_sem=recv.at[w, j],
                device_id=(*peers[j], c), device_id_type=MESH_ID)

        local = [pltpu.make_async_copy(ins[w], outs[w].at[q], loc.at[w]) for w in range(nw)]
        sends = [[remote(w, j, q) for j in range(3)] for w in range(nw)]
        for w in range(nw):
            local[w].start()
            for j in range(3):
                sends[w][j].start()
        for w in range(nw):
            local[w].wait()
            for j in range(3):
                sends[w][j].wait_send()
                remote(w, j, chips[j]).wait_recv()

    return pl.pallas_call(
        body, name=name, in_specs=[ANY] * nw, out_specs=[ANY] * nw,
        out_shape=[jax.ShapeDtypeStruct((4,) + a.shape, a.dtype) for a in arrays],
        scratch_shapes=[pltpu.SemaphoreType.DMA((nw, 3)), pltpu.SemaphoreType.DMA((nw, 3)), pltpu.SemaphoreType.DMA((nw,))],
        compiler_params=pltpu.CompilerParams(has_side_effects=True),
    )(*arrays)


def scatter_slots(name, arrays, collective_id):
    nw = len(arrays)

    def body(*refs):
        ins, outs = refs[:nw], refs[nw:2 * nw]
        send, recv = refs[2 * nw:]
        x, y, c = _my_place()
        q = 2 * x + y
        peers, chips = _chip_peers(x, y)
        barrier = pltpu.get_barrier_semaphore()
        for p in peers:
            pl.semaphore_signal(barrier, inc=1, device_id=(*p, c), device_id_type=MESH_ID)
        pl.semaphore_wait(barrier, 3)

        def remote(w, j, src_slot, dst_slot):
            return pltpu.make_async_remote_copy(
                src_ref=ins[w].at[src_slot], dst_ref=outs[w].at[dst_slot], send_sem=send.at[w, j], recv_sem=recv.at[w, j],
                device_id=(*peers[j], c), device_id_type=MESH_ID)

        sends = [[remote(w, j, chips[j], q) for j in range(3)] for w in range(nw)]
        for w in range(nw):
            for j in range(3):
                sends[w][j].start()
        for w in range(nw):
            for j in range(3):
                sends[w][j].wait_send()
                remote(w, j, q, chips[j]).wait_recv()

    return pl.kernel(
        body, out_type=[jax.ShapeDtypeStruct(a.shape, a.dtype) for a in arrays],
        mesh=plsc.ScalarSubcoreMesh(axis_name="sequencer", num_cores=1), name=name,
        scratch_types=[pltpu.SemaphoreType.DMA((nw, 3)), pltpu.SemaphoreType.DMA((nw, 3))],
        compiler_params=pltpu.CompilerParams(collective_id=collective_id),
    )(*arrays)


def _halves_by_cols(rows):
    return rows % 32 != 0


def _half_of(ref, shape, h):
    rows, cols = shape
    if _halves_by_cols(rows):
        return ref.at[:, pl.ds(h * (cols // 2), cols // 2)]
    return ref.at[pl.ds(h * (rows // 2), rows // 2)]


def _half_value(a, h):
    rows, cols = a.shape[-2:]
    if _halves_by_cols(rows):
        return lax.dynamic_slice_in_dim(a, h * (cols // 2), cols // 2, axis=a.ndim - 1)
    return lax.dynamic_slice_in_dim(a, h * (rows // 2), rows // 2, axis=a.ndim - 2)


def _join_halves(lo, hi, rows):
    return jnp.concatenate([lo, hi], axis=lo.ndim - 1 if _halves_by_cols(rows) else lo.ndim - 2)


def gather_two_level(name, arrays, collective_id):
    nw = len(arrays)
    shapes = [a.shape for a in arrays]

    def body(*refs):
        ins, outs = refs[:nw], refs[nw:2 * nw]
        send, recv = refs[2 * nw:]
        x, y, c = _my_place()
        q = 2 * x + y
        me, sibling = (x, y, c), (x, y, 1 - c)
        peers = [(1 - x, y), (x, 1 - y), (1 - x, 1 - y)]
        chips = [2 * px + py for px, py in peers]
        barrier = pltpu.get_barrier_semaphore()
        for dev in [sibling] + [(*p, c) for p in peers]:
            pl.semaphore_signal(barrier, inc=1, device_id=dev, device_id_type=MESH_ID)
        pl.semaphore_wait(barrier, 4)

        def mine(w):
            return _half_of(ins[w], shapes[w], c)

        def landed(w, chip, half):
            return _half_of(outs[w].at[chip], shapes[w], half)

        def copy(w, k, src, chip, half, to):
            return pltpu.make_async_remote_copy(
                src_ref=src, dst_ref=landed(w, chip, half), send_sem=send.at[w, k], recv_sem=recv.at[w, k],
                device_id=to, device_id_type=MESH_ID)

        first = [[copy(w, 0, mine(w), q, c, sibling)] + [copy(w, 1 + j, mine(w), q, c, (*peers[j], c)) for j in range(3)]
                 for w in range(nw)]
        for w in range(nw):
            for cp in first[w]:
                cp.start()
        passed = []
        for w in range(nw):
            for j in range(3):
                copy(w, 1 + j, mine(w), chips[j], c, me).wait_recv()
                fwd = copy(w, 4 + j, landed(w, chips[j], c), chips[j], c, sibling)
                fwd.start()
                passed.append(fwd)
        for w in range(nw):
            copy(w, 0, mine(w), q, 1 - c, me).wait_recv()
            for j in range(3):
                copy(w, 4 + j, mine(w), chips[j], 1 - c, me).wait_recv()
        for w in range(nw):
            for cp in first[w]:
                cp.wait_send()
        for cp in passed:
            cp.wait_send()

    out_type = [jax.ShapeDtypeStruct((4,) + a.shape, a.dtype) for a in arrays]
    return pl.kernel(
        body, out_type=out_type, mesh=plsc.ScalarSubcoreMesh(axis_name="sequencer", num_cores=1), name=name,
        scratch_types=[pltpu.SemaphoreType.DMA((nw, 7)), pltpu.SemaphoreType.DMA((nw, 7))],
        compiler_params=pltpu.CompilerParams(collective_id=collective_id),
    )(*arrays)


def core_swap(name, arrays):
    nw = len(arrays)
    ANY = pl.BlockSpec(memory_space=pl.ANY)

    def body(*refs):
        ins, outs = refs[:nw], refs[nw:2 * nw]
        send, recv = refs[2 * nw:]
        x, y, c = _my_place()
        copies = [pltpu.make_async_remote_copy(
            src_ref=ins[w], dst_ref=outs[w], send_sem=send.at[w], recv_sem=recv.at[w],
            device_id=(x, y, 1 - c), device_id_type=MESH_ID) for w in range(nw)]
        for cp in copies:
            cp.start()
        for cp in copies:
            cp.wait_send()
            cp.wait_recv()

    return pl.pallas_call(
        body, name=name, in_specs=[ANY] * nw, out_specs=[ANY] * nw,
        out_shape=[jax.ShapeDtypeStruct(a.shape, a.dtype) for a in arrays],
        scratch_shapes=[pltpu.SemaphoreType.DMA((nw,)), pltpu.SemaphoreType.DMA((nw,))],
        compiler_params=pltpu.CompilerParams(has_side_effects=True),
    )(*arrays)


def all_reduce_small(name, v):
    rows = v.shape[0]
    VM = pl.BlockSpec(memory_space=pltpu.VMEM)

    def body(v_ref, o_ref, buf, send, recv):
        x, y, c = _my_place()
        me = 4 * x + 2 * y + c

        def peer(kx):
            return (x ^ ((kx >> 2) & 1), y ^ ((kx >> 1) & 1), c ^ (kx & 1))

        def copy(kx, slot):
            return pltpu.make_async_remote_copy(
                src_ref=v_ref, dst_ref=buf.at[slot], send_sem=send.at[kx - 1], recv_sem=recv.at[kx - 1],
                device_id=peer(kx), device_id_type=MESH_ID)

        sends = [copy(kx, me) for kx in range(1, 8)]
        for cp in sends:
            cp.start()
        buf[me] = v_ref[...]
        for kx in range(1, 8):
            copy(kx, me ^ kx).wait_recv()
        for cp in sends:
            cp.wait_send()
        acc = buf[0]
        for d in range(1, 8):
            acc = acc + buf[d]
        o_ref[...] = acc

    return pl.pallas_call(
        body, name=name, in_specs=[VM], out_specs=VM, out_shape=jax.ShapeDtypeStruct(v.shape, F32),
        scratch_shapes=[pltpu.VMEM((8, rows, LANES), F32), pltpu.SemaphoreType.DMA((7,)), pltpu.SemaphoreType.DMA((7,))],
        compiler_params=pltpu.CompilerParams(has_side_effects=True, vmem_limit_bytes=VMEM_LIMIT),
    )(v)


def _pad_cols(a, n):
    return jnp.pad(a, ((0, 0), (0, n - a.shape[1])))


def _pad_rows(a, n):
    return jnp.pad(a, ((0, n - a.shape[0]), (0, 0)))


def _halo(u, tr):
    t, cdim = u.shape
    tails = u.reshape(t // tr, tr, cdim)[:, tr - HALO:, :]
    tails = jnp.concatenate([jnp.zeros((1, HALO, cdim), u.dtype), tails[:-1]], axis=0)
    return tails.reshape(-1, cdim)


def _unhalo(du, dhalo, tr):
    t, cdim = du.shape
    n = t // tr
    dh = dhalo.reshape(n, HALO, cdim)
    dh = jnp.concatenate([dh[1:], jnp.zeros((1, HALO, cdim), du.dtype)], axis=0)
    d3 = du.reshape(n, tr, cdim)
    d3 = jnp.concatenate([d3[:, :tr - HALO, :], d3[:, tr - HALO:, :] + dh], axis=1)
    return d3.reshape(t, cdim)


def _to_slots(g, axis):
    r, cdim = g.shape
    if axis == 0:
        return g.reshape(4, r // 4, cdim)
    return g.reshape(r, 4, cdim // 4).transpose(1, 0, 2)


def _from_slots(s, axis):
    if axis == 0:
        return s.reshape(s.shape[0] * s.shape[1], s.shape[2])
    return s.transpose(1, 0, 2).reshape(s.shape[1], 4 * s.shape[2])


BIG = ("w_in", "w_out", "xattn_wq", "xattn_wk", "xattn_wv", "xattn_wo", "ffn_w1", "ffn_w2")
TRANSPOSED = ("w_in",)
BIG_AXIS = {"w_in": 0, "w_out": 0, "xattn_wq": 0, "xattn_wk": 0, "xattn_wv": 0, "xattn_wo": 0, "ffn_w1": 1, "ffn_w2": 0}
SMALL_SHARDED = ("ssd_conv_w", "rwkv_w2", "rwkv_a2", "rwkv_g2")
GATHER_GROUPS = (("w_in",), ("w_out", "xattn_wq", "xattn_wk", "xattn_wv", "xattn_wo"), ("ffn_w1", "ffn_w2"))
REDUCE_GROUPS = (("ffn_w2", "ffn_w1"), ("xattn_wo", "xattn_wq", "xattn_wk", "xattn_wv", "w_out"),
                 ("rwkv_w2", "rwkv_a2", "rwkv_g2", "w_in"))
REDUCED = BIG + ("rwkv_w2", "rwkv_a2", "rwkv_g2")
REDUCE_AXIS = dict(BIG_AXIS, rwkv_w2=1, rwkv_a2=1, rwkv_g2=1)
WEIGHTS = ("norm_mix_g", "w_in", "ssd_conv_w", "ssd_conv_b", "ssd_dt_bias", "ssd_a_log", "ssd_d", "ssd_norm_g",
           "rwkv_mu", "rwkv_w0", "rwkv_w2", "rwkv_a0", "rwkv_a2", "rwkv_g2", "rwkv_k_k", "rwkv_k_a", "rwkv_r_k",
           "rwkv_ln_w", "rwkv_ln_b", "w_out", "norm_x_g", "norm_mem_g", "xattn_wq", "xattn_wk", "xattn_wv", "xattn_wo",
           "norm_ffn_g", "ffn_w1", "ffn_w2", "final_norm_g")


def _local_grads(x, mem, tgt, wt, full, big, reducer):
    t, d = x.shape
    w = d // 2
    nh = w // HEAD_DIM
    n_pairs = nh // 2
    ppg = n_pairs // SSD_GROUPS
    bc = SSD_GROUPS * SSD_STATE
    conv_dim = w + 2 * bc
    tr = ROW_TILE
    nt = t // tr
    tr2 = 2 * tr if t % (2 * tr) == 0 else tr
    nt2 = t // tr2
    dr = wt["rwkv_w2"].shape[0]
    ar = wt["rwkv_a2"].shape[0]
    gr = wt["rwkv_g2"].shape[0]

    big.start(0, None)
    big.start(1, None)
    h1, _ = norm_fwd("norm_mix", x, wt["norm_mix_g"], tr2, with_transpose=False)
    w_in_t = big.get("w_in", h1)
    o = 0
    segs = {}
    for nm, width in (("z", w), ("xbc", conv_dim), ("dt", nh), ("rkv", 3 * w), ("pw", dr), ("pa", ar), ("pg", gr)):
        segs[nm] = (o, width)
        o += width
    padded = {"z": w, "xbc": conv_dim, "dt": LANES, "rkv": 3 * w, "pw": LANES, "pa": LANES, "pg": gr}
    order = ("z", "xbc", "dt", "rkv", "pw", "pa", "pg")
    w_segs = [jnp.concatenate([_pad_rows(w_in_t[segs[nm][0]:segs[nm][0] + segs[nm][1]], padded[nm]) for nm in grp], axis=0)
              for grp in (("z",), ("xbc",), ("dt",), ("rkv",), ("pw", "pa", "pg"))]
    w_perm_t = jnp.concatenate(w_segs, axis=0)
    offs = {}
    o = 0
    for nm in order:
        offs[nm] = o
        o += padded[nm]
    lora_w = 2 * LANES + gr

    mu = wt["rwkv_mu"]
    mo = 3 * w
    mu_rkv = mu[:, :mo]
    mu_lora = jnp.concatenate([_pad_cols(mu[:, mo:mo + dr], LANES), _pad_cols(mu[:, mo + dr:mo + dr + ar], LANES),
                               mu[:, mo + dr + ar:]], axis=1)
    w2p = _pad_rows(full["rwkv_w2"], LANES)
    a2p = _pad_rows(full["rwkv_a2"], LANES)
    g2 = full["rwkv_g2"]
    conv_w = full["ssd_conv_w"]
    cw = [conv_w[i:i + 1] for i in range(SSD_CONV)]
    dt_bias = _pad_cols(wt["ssd_dt_bias"], LANES)
    a_log = _pad_cols(wt["ssd_a_log"], LANES)
    d_skip = _pad_cols(wt["ssd_d"], LANES)
    r_k = wt["rwkv_r_k"].reshape(1, w)

    z, xbc, dtraw, urkv, ulora = [matmul("in_proj_%d" % i, h1, ws, tb=True) for i, ws in enumerate(w_segs)]
    big.start(2, urkv)

    halo_xbc = _halo(xbc, tr)
    ssd_pre_t = [(xbc, tr, conv_dim, 0), (halo_xbc, HALO, conv_dim, 0), (dtraw, tr, LANES, 0)]
    ssd_pre_f = cw + [wt["ssd_conv_b"], dt_bias]
    act, dt = fn_fwd("ssd_pre", _ssd_pre, nt, ssd_pre_t, ssd_pre_f, [(t, tr, conv_dim, F32), (t, tr, LANES, F32)])

    nb = w // LANES
    ssd_seq = [(act, None), (act, lambda p: nb + p // ppg), (act, lambda p: nb + SSD_GROUPS + p // ppg), (dt, lambda p: 0)]
    ssd_ppb = min(ppg, PAIRS_PER_STEP)
    rw_ppb = min(n_pairs, 2 * PAIRS_PER_STEP)

    def ssd_fn(sv, cv, hts, ids):
        return [_ssd_chunk(*s, cv[0], ht, p) for s, ht, p in zip(sv, hts, ids)]

    y_scan, ssd_states = scan_fwd("ssd_scan", ssd_fn, SSD_CHUNK, ssd_seq, [a_log], n_pairs, ssd_ppb)
    ssd_post_t = [(y_scan, tr, w, 0), (act, tr, w, 0), (z, tr, w, 0)]
    ssd_post_f = [d_skip, wt["ssd_norm_g"]]
    y_ssd, y_ssd_t = fn_fwd("ssd_post", _ssd_post, nt, ssd_post_t, ssd_post_f, [(t, tr, w, BF16)], (0,))

    halo_rkv, halo_lora = _halo(urkv, tr), _halo(ulora, tr)
    rw_pre_t = [(urkv, tr, 3 * w, 0), (ulora, tr, lora_w, 0), (halo_rkv, HALO, 3 * w, 0), (halo_lora, HALO, lora_w, 0)]
    rw_pre_f = [mu_rkv, mu_lora, wt["rwkv_w0"], wt["rwkv_a0"], wt["rwkv_k_k"], wt["rwkv_k_a"], w2p, a2p, g2]
    rw = fn_fwd("rwkv_pre", _rwkv_pre, nt, rw_pre_t, rw_pre_f, [(t, tr, w, F32)] * 7)
    r_, lw_, k2_, v_, nkk_, b_, gate_ = rw
    rw_seq = [(a, None) for a in (r_, lw_, k2_, v_, nkk_, b_)]

    def rw_fn(sv, cv, hts, ids):
        return _rwkv_chunks([(*s, ht) for s, ht in zip(sv, hts)])

    yr_scan, rw_states = scan_fwd("rwkv_scan", rw_fn, RWKV_CHUNK, rw_seq, [], n_pairs, rw_ppb)
    rw_post_t = [(a, tr, w, 0) for a in (yr_scan, r_, k2_, v_, gate_)]
    rw_post_f = [r_k, wt["rwkv_ln_w"], wt["rwkv_ln_b"]]
    y_rwkv, y_rwkv_t = fn_fwd("rwkv_post", _rwkv_post, nt, rw_post_t, rw_post_f, [(t, tr, w, BF16)], (0,))

    ymix = jnp.concatenate([y_ssd, y_rwkv], axis=1)
    ymix_t = jnp.concatenate([y_ssd_t, y_rwkv_t], axis=0)
    w_out = big.get("w_out", ymix)
    x1 = matmul("out_proj", ymix, w_out, resid=x)

    h2, h2t = norm_fwd("norm_x", x1, wt["norm_x_g"], tr2)
    mrows = mem.shape[0]
    mn, mnt = norm_fwd("norm_mem", mem, wt["norm_mem_g"], mrows)
    wq, wk, wv, wo = [big.get(nm, ymix) for nm in ("xattn_wq", "xattn_wk", "xattn_wv", "xattn_wo")]
    q = matmul("xattn_q", h2, wq)
    kx = matmul("xattn_k", mn, wk)
    vx = matmul("xattn_v", mn, wv)
    ao, aot = fn_fwd("xattn_core", _attn, nt2, [(q, tr2, d, 0)], [kx, vx], [(t, tr2, d, BF16)], (0,))
    x2 = matmul("xattn_o", ao, wo, resid=x1)

    h3, h3t = norm_fwd("norm_ffn", x2, wt["norm_ffn_g"], tr2)
    w1, w2 = big.get("ffn_w1", h3), big.get("ffn_w2", h3)
    a1 = matmul("ffn_up", h3, w1, out_dtype=BF16)
    dff = a1.shape[1]
    f1, f1t = fn_fwd("ffn_act", _relu2, nt, [(a1, tr, dff, 0)], [], [(t, tr, dff, BF16)], (0,))
    x3 = matmul("ffn_down", f1, w2, resid=x2)

    dx3, dx3b, g_final, loss_tile = loss_head(x3, tgt, wt["final_norm_g"].reshape(1, d), tr2)

    grads = {"final_norm_g": g_final.reshape(d)}
    grads["ffn_w2"] = matmul("ffn_down_dw", f1t, dx3b)
    df1 = matmul("ffn_down_dx", dx3b, w2, tb=True, out_dtype=BF16)
    (da1,), _ = fn_bwd("ffn_act_bwd", _relu2, nt, [(a1, tr, dff, 0)], [], [(df1, tr, dff, 0)], lambda c: [c[0].astype(F32)],
                       [(t, tr, dff, BF16)])
    grads["ffn_w1"] = matmul("ffn_up_dw", h3t, da1)
    dh3 = reducer.launch(0, grads, matmul("ffn_up_dx", da1, w1, tb=True))
    dx2, dx2b, grads["norm_ffn_g"] = norm_bwd("norm_ffn_bwd", x2, wt["norm_ffn_g"], dh3, dx3, tr2)

    grads["xattn_wo"] = matmul("xattn_o_dw", aot, dx2b)
    dao = matmul("xattn_o_dx", dx2b, wo, tb=True)
    (dq,), (dkx, dvx) = fn_bwd("xattn_core_bwd", _attn, nt2, [(q, tr2, d, 0)], [kx, vx], [(dao, tr2, d, 0)], lambda c: c,
                               [(t, tr2, d, BF16)])
    grads["xattn_wq"] = matmul("xattn_q_dw", h2t, dq)
    dh2 = matmul("xattn_q_dx", dq, wq, tb=True)
    dkb, dvb = dkx.astype(BF16), dvx.astype(BF16)
    grads["xattn_wk"] = matmul("xattn_k_dw", mnt, dkb)
    grads["xattn_wv"] = matmul("xattn_v_dw", mnt, dvb)
    dmn = matmul("xattn_k_dx", dkb, wk, tb=True)
    dmn = matmul("xattn_v_dx", dvb, wv, tb=True, resid=dmn)
    _, _, grads["norm_mem_g"] = norm_bwd("norm_mem_bwd", mem, wt["norm_mem_g"], dmn, None, mrows)
    dx1, dx1b, grads["norm_x_g"] = norm_bwd("norm_x_bwd", x1, wt["norm_x_g"], dh2, dx2, tr2)

    grads["w_out"] = matmul("out_proj_dw", ymix_t, dx1b)
    dymix = reducer.launch(1, grads, matmul("out_proj_dx", dx1b, w_out, tb=True))

    (dyr, dr1, dk1, dv1, dgate), (g_rk, grads["rwkv_ln_w"], grads["rwkv_ln_b"]) = fn_bwd(
        "rwkv_post_bwd", _rwkv_post, nt, rw_post_t, rw_post_f, [(dymix, tr, w, 1)], lambda c: c, [(t, tr, w, F32)] * 5)
    grads["rwkv_r_k"] = g_rk.reshape(wt["rwkv_r_k"].shape)
    (dr2, dlw, dk2, dv2, dnkk, db), _ = scan_bwd("rwkv_scan_bwd", rw_fn, RWKV_CHUNK, rw_seq, [], rw_states, dyr, n_pairs, rw_ppb)
    rw_ct = [(a, tr, w, 0) for a in (dr1, dr2, dlw, dk1, dk2, dv1, dv2, dnkk, db, dgate)]

    def rw_ct_fn(c):
        return (c[0] + c[1], c[2], c[3] + c[4], c[5] + c[6], c[7], c[8], c[9])

    (durkv, dulora, dhrkv, dhlora), rw_pg = fn_bwd(
        "rwkv_pre_bwd", _rwkv_pre, nt, rw_pre_t, rw_pre_f, rw_ct, rw_ct_fn,
        [(t, tr, 3 * w, F32), (t, tr, lora_w, F32), (nt * HALO, HALO, 3 * w, F32), (nt * HALO, HALO, lora_w, F32)])
    durkv = _unhalo(durkv, dhrkv, tr)
    dulora = _unhalo(dulora, dhlora, tr)
    g_mu_rkv, g_mu_lora, grads["rwkv_w0"], grads["rwkv_a0"], grads["rwkv_k_k"], grads["rwkv_k_a"], g_w2p, g_a2p, grads["rwkv_g2"] = rw_pg
    grads["rwkv_mu"] = jnp.concatenate([g_mu_rkv, g_mu_lora[:, :dr], g_mu_lora[:, LANES:LANES + ar], g_mu_lora[:, 2 * LANES:]], axis=1)
    grads["rwkv_w2"] = g_w2p[:dr]
    grads["rwkv_a2"] = g_a2p[:ar]

    (dys, dxs1, dz), (g_d, grads["ssd_norm_g"]) = fn_bwd(
        "ssd_post_bwd", _ssd_post, nt, ssd_post_t, ssd_post_f, [(dymix, tr, w, 0)], lambda c: c, [(t, tr, w, F32)] * 3)
    grads["ssd_d"] = g_d[:, :nh]
    (dxs2, dbp, dcp, ddtp), (g_alog,) = scan_bwd("ssd_scan_bwd", ssd_fn, SSD_CHUNK, ssd_seq, [a_log], ssd_states, dys, n_pairs, ssd_ppb)
    grads["ssd_a_log"] = g_alog[:, :nh]
    ssd_ct = [(dxs1, tr, w, 0), (dxs2, tr, w, 0), (dbp, tr, w, 0), (dcp, tr, w, 0), (ddtp, tr, w, 0)]

    def ssd_ct_fn(c):
        def group_sum(a):
            parts = []
            for gi in range(SSD_GROUPS):
                s = a[:, gi * ppg * LANES:(gi * ppg + 1) * LANES]
                for j in range(1, ppg):
                    s = s + a[:, (gi * ppg + j) * LANES:(gi * ppg + j + 1) * LANES]
                parts.append(s)
            return parts
        ddt = c[4][:, :LANES]
        for j in range(1, n_pairs):
            ddt = ddt + c[4][:, j * LANES:(j + 1) * LANES]
        return (jnp.concatenate([c[0] + c[1]] + group_sum(c[2]) + group_sum(c[3]), axis=1), ddt)

    (dxbc, dhxbc, ddtraw), ssd_pg = fn_bwd(
        "ssd_pre_bwd", _ssd_pre, nt, ssd_pre_t, ssd_pre_f, ssd_ct, ssd_ct_fn,
        [(t, tr, conv_dim, F32), (nt * HALO, HALO, conv_dim, F32), (t, tr, LANES, F32)])
    dxbc = _unhalo(dxbc, dhxbc, tr)
    grads["ssd_conv_w"] = jnp.concatenate(ssd_pg[:SSD_CONV], axis=0)
    grads["ssd_conv_b"] = ssd_pg[SSD_CONV]
    grads["ssd_dt_bias"] = ssd_pg[SSD_CONV + 1][:, :nh]

    du = jnp.concatenate([dz, dxbc, ddtraw, durkv, dulora], axis=1).astype(BF16)
    g_perm_t = matmul("in_proj_dw", du.T, h1)
    grads["w_in"] = jnp.concatenate([g_perm_t[offs[nm]:offs[nm] + segs[nm][1]] for nm in order], axis=0)
    dh1 = matmul("in_proj_dx", du, w_perm_t)
    dh1 = reducer.launch(2, grads, dh1)
    grad_x, _, grads["norm_mix_g"] = norm_bwd("norm_mix_bwd", x, wt["norm_mix_g"], dh1, dx1, tr2)
    return loss_tile, grad_x, grads


def _pack(arrs):
    flat = jnp.concatenate([a.reshape(-1) for a in arrs])
    n = flat.shape[0]
    rows = -(-n // (8 * LANES)) * 8
    return jnp.pad(flat, (0, rows * LANES - n)).reshape(rows, LANES)


def _unpack(packed, shapes):
    flat = packed.reshape(-1)
    out, o = [], 0
    for s in shapes:
        n = math.prod(s)
        out.append(flat[o:o + n].reshape(s))
        o += n
    return out


def _as2d(a):
    return a.reshape(-1, a.shape[-1])


def _shard_view(n, a):
    return _as2d(a[0]).T if n in TRANSPOSED else _as2d(a[0])


class _GatheredWeights:
    def __init__(self, shard2d, q, c):
        self.shard2d, self.q, self.c = shard2d, q, c
        self.raw, self.ready = {}, {}

    def start(self, gi, after):
        shards = [self.shard2d[n].astype(BF16) for n in GATHER_GROUPS[gi]]
        if after is not None:
            shards, _ = lax.optimization_barrier((shards, after))
        gathered = gather_two_level("gather_weights_%d" % gi, shards, gi + 1)
        for n, sh, g in zip(GATHER_GROUPS[gi], shards, gathered):
            self.raw[n] = (sh, g)

    def get(self, name, after):
        if name not in self.ready:
            sh, g = self.raw[name]
            if after is not None:
                g, _ = lax.optimization_barrier((g, after))
            rows, cols = sh.shape
            at = (self.q, 0, self.c * (cols // 2)) if _halves_by_cols(rows) else (self.q, self.c * (rows // 2), 0)
            g = lax.dynamic_update_slice(g, _half_value(sh, self.c)[None], at)
            self.ready[name] = _from_slots(g, BIG_AXIS[name])
        return self.ready[name]


class _GradReducer:
    def __init__(self, q, c, update):
        self.q, self.c, self.update = q, c, update
        self.pending, self.updated = {}, {}

    def launch(self, gi, grads, nxt):
        names = REDUCE_GROUPS[gi]
        kept, sent, rows = [], [], []
        for n in names:
            s = _to_slots(grads[n], REDUCE_AXIS[n])
            rows.append(s.shape[1])
            kept.append(_half_value(s, self.c))
            sent.append(_half_value(s, 1 - self.c).astype(BF16))
        got = core_swap("swap_halves_%d" % gi, sent)
        parts = []
        for n, k, g in zip(names, kept, got):
            _, hr, hc = k.shape
            (part,) = ew_call("chip_sum_" + n, lambda kv, gv: (kv + gv.astype(F32),),
                              [k.reshape(4 * hr, hc), g.reshape(4 * hr, hc)], [BF16])
            parts.append(part.reshape(4, hr, hc))
        parts, nxt = lax.optimization_barrier((parts, nxt))
        slots = scatter_slots("scatter_grads_%d" % gi, parts, len(GATHER_GROUPS) + 1 + gi)
        self.pending[gi] = (parts, slots, rows)
        return self.finish(gi - 1, nxt) if gi > 0 else nxt

    def finish(self, gi, nxt):
        names = REDUCE_GROUPS[gi]
        parts, slots, rows = self.pending[gi]
        halves = []
        for n, p, s in zip(names, parts, slots):
            own = lax.dynamic_index_in_dim(p, self.q, axis=0, keepdims=True)
            halves.append(sum_slots("sum_" + n, lax.dynamic_update_slice(s, own, (self.q, 0, 0))))
        others = core_swap("swap_reduced_%d" % gi, halves)
        lo = [jnp.where(self.c == 0, mine, other) for mine, other in zip(halves, others)]
        hi = [jnp.where(self.c == 0, other, mine) for mine, other in zip(halves, others)]
        results = [self.update(n, _join_halves(l, h, r)) for n, l, h, r in zip(names, lo, hi, rows)]
        if nxt is not None:
            results, nxt = lax.optimization_barrier((results, nxt))
        self.updated.update(zip(names, results))
        return nxt


def _step(a):
    x, mem, tgt = a["x"][0], a["mem"][0], a["loss_target"][0]
    q = 2 * lax.axis_index("x") + lax.axis_index("y")

    shard2d = {n: _shard_view(n, a[n]) for n in BIG}
    small_sh = {n: _as2d(a[n][0]) for n in SMALL_SHARDED}
    c = lax.axis_index("c")
    full = {}
    big = _GatheredWeights(shard2d, q, c)
    gathered = gather_shards("gather_small", [small_sh[n] for n in SMALL_SHARDED])
    for n, g in zip(SMALL_SHARDED, gathered):
        full[n] = _from_slots(g, 1)

    wt = {n: (a[n] if a[n].ndim <= 2 else a[n][0]) for n in WEIGHTS if n not in BIG and n not in SMALL_SHARDED}
    for n in SMALL_SHARDED:
        wt[n] = small_sh[n]
    shards = dict(shard2d)
    shards.update({n: small_sh[n] for n in REDUCED if n not in BIG})

    def update(n, gsum):
        return adamw("adamw_" + n, shards[n], _shard_view(n, a["m_" + n]), _shard_view(n, a["v_" + n]), gsum)

    reducer = _GradReducer(q, c, update)
    loss_tile, grad_x, grads = _local_grads(x, mem, tgt, wt, full, big, reducer)
    reducer.finish(len(REDUCE_GROUPS) - 1, None)
    out = {}
    for n, vals in reducer.updated.items():
        for key, val in zip(("grad_", "delta_", "new_m_", "new_v_"), vals):
            out[key + n] = (val.T if n in TRANSPOSED else val).reshape(a[n].shape)

    small = [n for n in WEIGHTS if n not in REDUCED]
    red = _unpack(all_reduce_small("all_reduce_small", _pack([grads[n] for n in small])), [grads[n].shape for n in small])
    g_loc = {}
    for n, g in zip(small, red):
        if n in SMALL_SHARDED:
            cols = g.shape[1] // 4
            g = lax.dynamic_slice_in_dim(g, q * cols, cols, axis=1)
        g_loc[n] = g.reshape(a[n].shape)
    res = adamw("adamw_small", *[_pack([src[n] for n in small]) for src in
                                 ({n: a[n] for n in small}, {n: a["m_" + n] for n in small}, {n: a["v_" + n] for n in small})],
                _pack([g_loc[n] for n in small]))
    shapes = [a[n].shape for n in small]
    for key, packed in zip(("grad_", "delta_", "new_m_", "new_v_"), res):
        for n, val in zip(small, _unpack(packed, shapes)):
            out[key + n] = val

    loss = lax.psum(loss_tile[0, 0], ("x", "y", "c"))
    ordered = [loss, grad_x.reshape(a["x"].shape)]
    for key in ("grad_", "delta_", "new_m_", "new_v_"):
        ordered += [out[key + n] for n in WEIGHTS]
    return tuple(ordered)


def kernel(x, mem, norm_mix_g, w_in, ssd_conv_w, ssd_conv_b, ssd_dt_bias, ssd_a_log, ssd_d, ssd_norm_g, rwkv_mu, rwkv_w0, rwkv_w2, rwkv_a0, rwkv_a2, rwkv_g2, rwkv_k_k, rwkv_k_a, rwkv_r_k, rwkv_ln_w, rwkv_ln_b, w_out, norm_x_g, norm_mem_g, xattn_wq, xattn_wk, xattn_wv, xattn_wo, norm_ffn_g, ffn_w1, ffn_w2, final_norm_g, loss_target, m_norm_mix_g, m_w_in, m_ssd_conv_w, m_ssd_conv_b, m_ssd_dt_bias, m_ssd_a_log, m_ssd_d, m_ssd_norm_g, m_rwkv_mu, m_rwkv_w0, m_rwkv_w2, m_rwkv_a0, m_rwkv_a2, m_rwkv_g2, m_rwkv_k_k, m_rwkv_k_a, m_rwkv_r_k, m_rwkv_ln_w, m_rwkv_ln_b, m_w_out, m_norm_x_g, m_norm_mem_g, m_xattn_wq, m_xattn_wk, m_xattn_wv, m_xattn_wo, m_norm_ffn_g, m_ffn_w1, m_ffn_w2, m_final_norm_g, v_norm_mix_g, v_w_in, v_ssd_conv_w, v_ssd_conv_b, v_ssd_dt_bias, v_ssd_a_log, v_ssd_d, v_ssd_norm_g, v_rwkv_mu, v_rwkv_w0, v_rwkv_w2, v_rwkv_a0, v_rwkv_a2, v_rwkv_g2, v_rwkv_k_k, v_rwkv_k_a, v_rwkv_r_k, v_rwkv_ln_w, v_rwkv_ln_b, v_w_out, v_norm_x_g, v_norm_mem_g, v_xattn_wq, v_xattn_wk, v_xattn_wv, v_xattn_wo, v_norm_ffn_g, v_ffn_w1, v_ffn_w2, v_final_norm_g):
    return _step(dict(locals()))
```

```python
import functools
import math

import jax
import jax.numpy as jnp
from jax import lax
from jax.experimental import pallas as pl
from jax.experimental.pallas import tpu as pltpu
from jax.experimental.pallas import tpu_sc as plsc

F32 = jnp.float32
BF16 = jnp.bfloat16
HIGHEST = lax.Precision.HIGHEST
MESH_ID = pl.DeviceIdType.MESH

NORM_EPS = 1e-6
RWKV_LN_EPS = 64e-5
HEAD_DIM = 64
PAIR = 2 * HEAD_DIM
LANES = 128
SSD_STATE = 128
SSD_CHUNK = 128
SSD_GROUPS = 2
SSD_CONV = 4
RWKV_CHUNK = 64
HALO = 8
ROW_TILE = 128
PAIRS_PER_STEP = 4
XATTN_HEADS = 4
RWKV_PASSES = 1
VMEM_LIMIT = 56 * 1024 * 1024
MATMUL_VMEM = 40 * 1024 * 1024

ADAM_LR = 0.001
ADAM_B1 = 0.9
ADAM_B2 = 0.999
ADAM_EPS = 1e-08
ADAM_WD = 0.01
ADAM_STEP = 10


def _dims(ca, cb):
    return (((ca,), (cb,)), ((), ()))


def _split_bf16(a):
    hi = a.astype(BF16)
    lo = (a - hi.astype(F32)).astype(BF16)
    return hi, lo


def _mm_impl(a, b, ca, cb, passes):
    dn = _dims(ca, cb)
    if passes == 1:
        return lax.dot_general(a.astype(BF16), b.astype(BF16), dn, preferred_element_type=F32)
    ah, al = _split_bf16(a)
    bh, bl = _split_bf16(b)
    out = lax.dot_general(ah, bh, dn, preferred_element_type=F32)
    out = out + lax.dot_general(ah, bl, dn, preferred_element_type=F32)
    return out + lax.dot_general(al, bh, dn, preferred_element_type=F32)


@functools.partial(jax.custom_vjp, nondiff_argnums=(2, 3, 4))
def mm(a, b, ca, cb, passes):
    return _mm_impl(a, b, ca, cb, passes)


def _mm_fwd(a, b, ca, cb, passes):
    return _mm_impl(a, b, ca, cb, passes), (a, b)


def _mm_bwd(ca, cb, passes, res, g):
    a, b = res
    da = mm(g, b, 1, 1 - cb, passes) if ca == 1 else mm(b, g, 1 - cb, 1, passes)
    db = mm(a, g, 1 - ca, 0, passes) if cb == 0 else mm(g, a, 0, 1 - ca, passes)
    return da, db


mm.defvjp(_mm_fwd, _mm_bwd)


def _dot_exact(a, b):
    return lax.dot_general(a, b, _dims(1, 0), precision=HIGHEST, preferred_element_type=F32)


def _iota(shape, dim):
    return lax.broadcasted_iota(jnp.int32, shape, dim)


def _sigmoid(x):
    return 1.0 / (1.0 + jnp.exp(-x))


def _silu(x):
    return x * _sigmoid(x)


def _softplus(x):
    return jnp.maximum(x, 0.0) + jnp.log(1.0 + jnp.exp(-jnp.abs(x)))


def _rms(x, g):
    return x * lax.rsqrt(jnp.mean(x * x, axis=-1, keepdims=True) + NORM_EPS) * g


def _select_mm(x, sel):
    hi = x.astype(BF16)
    r1 = x - hi.astype(F32)
    mid = r1.astype(BF16)
    lo = (r1 - mid.astype(F32)).astype(BF16)
    dn = _dims(1, 0)
    out = lax.dot_general(hi, sel, dn, preferred_element_type=F32)
    out = out + lax.dot_general(mid, sel, dn, preferred_element_type=F32)
    return out + lax.dot_general(lo, sel, dn, preferred_element_type=F32)


def _head_sum_impl(x, n):
    sel = (_iota((n, LANES), 0) // HEAD_DIM == _iota((n, LANES), 1)).astype(BF16)
    return _select_mm(x, sel)


def _head_expand_impl(s, n):
    sel = (_iota((LANES, n), 1) // HEAD_DIM == _iota((LANES, n), 0)).astype(BF16)
    return _select_mm(s, sel)


@functools.partial(jax.custom_vjp, nondiff_argnums=(1,))
def _head_sum_n(x, n):
    return _head_sum_impl(x, n)


@functools.partial(jax.custom_vjp, nondiff_argnums=(1,))
def _head_expand(s, n):
    return _head_expand_impl(s, n)


_head_sum_n.defvjp(lambda x, n: (_head_sum_impl(x, n), None), lambda n, _, g: (_head_expand(g, n),))
_head_expand.defvjp(lambda s, n: (_head_expand_impl(s, n), None), lambda n, _, g: (_head_sum_n(g, n),))


def _head_sum(x):
    return _head_sum_n(x, x.shape[1])


def _row_vector_expand(v, n):
    v8 = jnp.broadcast_to(v, (8, LANES))
    return jnp.sum(_head_expand(v8, n), axis=0, keepdims=True) * 0.125


def _shift_rows_impl(u, halo, s):
    rolled = pltpu.roll(u, s, 0)
    top = jnp.where(_iota((HALO, 1), 0) < s, pltpu.roll(halo, s, 0), rolled[:HALO])
    return jnp.concatenate([top, rolled[HALO:]], axis=0)


@functools.partial(jax.custom_vjp, nondiff_argnums=(2,))
def _shift_rows(u, halo, s):
    return _shift_rows_impl(u, halo, s)


def _shift_rows_bwd(s, _, g):
    tr = g.shape[0]
    rolled = pltpu.roll(g, tr - s, 0)
    hrow = _iota((HALO, 1), 0)
    bottom = jnp.where(hrow < HALO - s, rolled[tr - HALO:], 0.0)
    dhalo = jnp.where(hrow >= HALO - s, pltpu.roll(g[:HALO], HALO - s, 0), 0.0)
    return jnp.concatenate([rolled[:tr - HALO], bottom], axis=0), dhalo


_shift_rows.defvjp(lambda u, halo, s: (_shift_rows_impl(u, halo, s), None), _shift_rows_bwd)


def _params(sem):
    return pltpu.CompilerParams(dimension_semantics=sem, vmem_limit_bytes=VMEM_LIMIT)


def row_call(name, body, n_tiles, tiled, full, out_tiled, out_acc, transposed=()):
    nt, nf, na = len(tiled), len(full), len(out_acc)
    n_plain = len(out_tiled)
    no = n_plain + len(transposed)

    def kern(*refs):
        tv = [r[...] for r in refs[:nt]]
        fv = [r[...] for r in refs[nt:nt + nf]]
        outs, accs = body(tv, fv)
        for r, v in zip(refs[nt + nf:nt + nf + n_plain], outs):
            r[...] = v.astype(r.dtype)
        for r, idx in zip(refs[nt + nf + n_plain:nt + nf + no], transposed):
            r[...] = outs[idx].astype(F32).T.astype(r.dtype)
        if na:
            a_refs = refs[nt + nf + no:]
            first = pl.program_id(0) == 0

            @pl.when(first)
            def _():
                for r, v in zip(a_refs, accs):
                    r[...] = v

            @pl.when(jnp.logical_not(first))
            def _():
                for r, v in zip(a_refs, accs):
                    r[...] += v

    in_specs = [pl.BlockSpec((rt, w), functools.partial(lambda i, cb: (i, cb), cb=cb)) for (_, rt, w, cb) in tiled]
    in_specs += [pl.BlockSpec(a.shape, lambda i: (0, 0)) for a in full]
    out_specs = [pl.BlockSpec((rt, w), lambda i: (i, 0)) for (_, rt, w, _) in out_tiled]
    out_specs += [pl.BlockSpec((out_tiled[idx][2], out_tiled[idx][1]), lambda i: (0, i)) for idx in transposed]
    out_specs += [pl.BlockSpec(s, lambda i: (0, 0)) for s in out_acc]
    out_shape = [jax.ShapeDtypeStruct((rows, w), dt) for (rows, _, w, dt) in out_tiled]
    out_shape += [jax.ShapeDtypeStruct((out_tiled[idx][2], out_tiled[idx][0]), BF16) for idx in transposed]
    out_shape += [jax.ShapeDtypeStruct(s, F32) for s in out_acc]
    res = pl.pallas_call(
        kern, name=name, grid=(n_tiles,), in_specs=in_specs, out_specs=out_specs, out_shape=out_shape,
        compiler_params=_params(("arbitrary",)),
    )(*[t[0] for t in tiled], *full)
    return list(res[:no]), list(res[no:])


def _pick(dim, cands):
    for c in cands:
        if dim % c == 0:
            return c
    return dim


def matmul(name, a, b, tb=False, resid=None, out_dtype=F32, out_slots=1):
    m, k = a.shape
    b_slots = b.shape[0] if b.ndim == 3 else 1
    n = b.shape[-2] if tb else b.shape[-1] * b_slots
    has_resid = resid is not None
    out_bytes = jnp.dtype(out_dtype).itemsize
    sizes = (2048, 1024, 896, 768, 512, 384, 256, 128)
    tm = _pick(m, sizes[1:])
    tn = _pick(n // max(out_slots, 1 if tb else b_slots), sizes[1:])

    def vmem_bytes(tk):
        return 2 * 2 * tk * (tm + tn) + tm * tn * (2 * out_bytes + 4 + (8 if has_resid else 0))

    k_slot = k // b_slots if tb else k
    tk = next((c for c in sizes if k_slot % c == 0 and vmem_bytes(c) <= MATMUL_VMEM), LANES)
    nk = k // tk
    n_per = n // (b_slots if not tb else 1) // tn
    k_per = k_slot // tk
    o_per = n // out_slots // tn

    def kern(*refs):
        a_ref, b_ref = refs[0], refs[1]
        o_ref, acc = refs[-2], refs[-1]
        kk = pl.program_id(2)
        part = lax.dot_general(a_ref[...], b_ref[...], _dims(1, 1 if tb else 0), preferred_element_type=F32)

        def finish(out):
            if has_resid:
                out = out + refs[2][...]
            o_ref[...] = out.astype(o_ref.dtype)

        if nk == 1:
            finish(part)
            return

        @pl.when(kk == 0)
        def _():
            acc[...] = part

        @pl.when(jnp.logical_and(kk > 0, kk < nk - 1))
        def _():
            acc[...] += part

        @pl.when(kk == nk - 1)
        def _():
            finish(acc[...] + part)

    in_specs = [pl.BlockSpec((tm, tk), lambda i, j, kk: (i, kk))]
    if b.ndim == 3 and tb:
        in_specs.append(pl.BlockSpec((None, tn, tk), lambda i, j, kk: (kk // k_per, j, kk % k_per)))
    elif b.ndim == 3:
        in_specs.append(pl.BlockSpec((None, tk, tn), lambda i, j, kk: (j // n_per, kk, j % n_per)))
    elif tb:
        in_specs.append(pl.BlockSpec((tn, tk), lambda i, j, kk: (j, kk)))
    else:
        in_specs.append(pl.BlockSpec((tk, tn), lambda i, j, kk: (kk, j)))
    args = [a, b]
    if has_resid:
        in_specs.append(pl.BlockSpec((tm, tn), lambda i, j, kk: (i, j)))
        args.append(resid)
    if out_slots > 1:
        out_spec = pl.BlockSpec((None, tm, tn), lambda i, j, kk: (j // o_per, i, j % o_per))
        out_shape = jax.ShapeDtypeStruct((out_slots, m, n // out_slots), out_dtype)
    else:
        out_spec = pl.BlockSpec((tm, tn), lambda i, j, kk: (i, j))
        out_shape = jax.ShapeDtypeStruct((m, n), out_dtype)
    return pl.pallas_call(
        kern, name=name, grid=(m // tm, n // tn, nk), in_specs=in_specs,
        out_specs=out_spec, out_shape=out_shape,
        scratch_shapes=[pltpu.VMEM((tm, tn), F32)],
        compiler_params=_params(("parallel", "parallel", "arbitrary")),
    )(*args)


def norm_fwd(name, x, g, tr, with_transpose=True):
    def body(tv, fv):
        return [_rms(tv[0], fv[0])], []
    rows, d = x.shape
    outs, _ = row_call(name, body, rows // tr, [(x, tr, d, 0)], [g], [(rows, tr, d, BF16)], [],
                       transposed=(0,) if with_transpose else ())
    return outs[0], (outs[1] if with_transpose else None)


def norm_bwd(name, x, g, dh, extra, tr):
    def body(tv, fv):
        _, vjp = jax.vjp(_rms, tv[0], fv[0])
        dx, dg = vjp(tv[1])
        if extra is not None:
            dx = dx + tv[2]
        return [dx, dx], [dg]
    rows, d = x.shape
    tiled = [(x, tr, d, 0), (dh, tr, d, 0)] + ([(extra, tr, d, 0)] if extra is not None else [])
    (dx, dxb), (dg,) = row_call(name, body, rows // tr, tiled, [g], [(rows, tr, d, F32), (rows, tr, d, BF16)], [g.shape])
    return dx, dxb, dg


def _ssd_pre(xbc, halo, dtraw, w0, w1, w2, w3, cb, dtb):
    y = w3 * xbc + w2 * _shift_rows(xbc, halo, 1) + w1 * _shift_rows(xbc, halo, 2) + w0 * _shift_rows(xbc, halo, 3) + cb
    return _silu(y), _softplus(dtraw + dtb)


def _ssd_post(ys, xs, z, dskip, ng):
    w = ys.shape[1]
    y = (ys + xs * _row_vector_expand(dskip, w)) * _silu(z)
    gw = w // SSD_GROUPS
    parts = []
    for gi in range(SSD_GROUPS):
        yg = y[:, gi * gw:(gi + 1) * gw]
        parts.append(yg * lax.rsqrt(jnp.mean(yg * yg, axis=-1, keepdims=True) + NORM_EPS))
    return jnp.concatenate(parts, axis=1) * ng


def _rwkv_pre(urkv, ulora, hrkv, hlora, mu_rkv, mu_lora, w0, a0, kkw, kaw, w2p, a2p, g2):
    w = w0.shape[1]
    urkv = urkv + (_shift_rows(urkv, hrkv, 1) - urkv) * mu_rkv
    ulora = ulora + (_shift_rows(ulora, hlora, 1) - ulora) * mu_lora
    r, k, v = urkv[:, :w], urkv[:, w:2 * w], urkv[:, 2 * w:]
    pw, pa, pg = ulora[:, :LANES], ulora[:, LANES:2 * LANES], ulora[:, 2 * LANES:]
    w_log = -_softplus(-(w0 + mm(jnp.tanh(pw), w2p, 1, 0, 1))) - 0.5
    lw = -jnp.exp(w_log)
    iclr = _sigmoid(a0 + mm(pa, a2p, 1, 0, 1))
    gate = mm(_sigmoid(pg), g2, 1, 0, 1)
    kk = k * kkw
    kk = kk / jnp.maximum(jnp.sqrt(_head_expand(_head_sum(kk * kk), w)), 1e-12)
    k2 = k * (1.0 + (iclr - 1.0) * kaw)
    return r, lw, k2, v, -kk, kk * iclr, gate


def _rwkv_post(ys, r, k2, v, gate, rk, lnw, lnb):
    w = ys.shape[1]
    inv = 1.0 / HEAD_DIM
    mean = _head_expand(_head_sum(ys), w) * inv
    d = ys - mean
    var = _head_expand(_head_sum(d * d), w) * inv
    yn = d * lax.rsqrt(var + RWKV_LN_EPS) * lnw + lnb
    bonus = _head_expand(_head_sum(r * k2 * rk), w) * v
    return (yn + bonus) * gate


def _attn(q, k, v):
    d = q.shape[1]
    hd = d // XATTN_HEADS
    outs = []
    for h in range(XATTN_HEADS):
        sl = slice(h * hd, (h + 1) * hd)
        s = mm(q[:, sl], k[:, sl], 1, 1, 1) * (hd ** -0.5)
        s = s - jnp.max(s, axis=-1, keepdims=True)
        p = jnp.exp(s)
        p = p / jnp.sum(p, axis=-1, keepdims=True)
        outs.append(mm(p, v[:, sl], 1, 0, 1))
    return jnp.concatenate(outs, axis=1)


def _relu2(a):
    return jnp.square(jnp.maximum(a.astype(F32), 0.0))


def fn_fwd(name, fn, n_tiles, tiled, full, out_tiled, transposed=()):
    def body(tv, fv):
        outs = fn(*tv, *fv)
        return (list(outs) if isinstance(outs, (tuple, list)) else [outs]), []
    outs, _ = row_call(name, body, n_tiles, tiled, full, out_tiled, [], transposed)
    return outs


def fn_bwd(name, fn, n_tiles, tiled, full, cts, ct_fn, out_tiled):
    nt = len(tiled)

    def body(tv, fv):
        outs, vjp = jax.vjp(fn, *tv[:nt], *fv)
        ct = ct_fn(tv[nt:])
        grads = vjp(tuple(ct) if isinstance(outs, (tuple, list)) else ct[0])
        return list(grads[:nt]), list(grads[nt:])
    return row_call(name, body, n_tiles, tiled + cts, full, out_tiled, [f.shape for f in full])


def _ssd_chunk(xs, bm, cm, dt_all, a_log, ht, p):
    q = xs.shape[0]
    lane = _iota((1, LANES), 1)
    row = _iota((q, 1), 0)
    tril = _iota((q, q), 0) >= _iota((q, q), 1)
    half = lane < HEAD_DIM
    da = dt_all * (-jnp.exp(a_log))
    cs = _dot_exact(tril.astype(F32), da)

    def col(mat, h):
        return jnp.sum(jnp.where(lane == h, mat, 0.0), axis=1, keepdims=True)

    cs0, cs1 = col(cs, 2 * p), col(cs, 2 * p + 1)
    xdt = xs * jnp.where(half, col(dt_all, 2 * p), col(dt_all, 2 * p + 1))
    csx = jnp.where(half, cs0, cs1)
    last = jnp.sum(jnp.where(row == q - 1, csx, 0.0), axis=0, keepdims=True)
    cb = mm(cm, bm, 1, 1, 1)
    y = mm(cm, ht, 1, 0, 1) * jnp.exp(csx)
    for csh, hm in ((cs0, half), (cs1, jnp.logical_not(half))):
        csl = jnp.broadcast_to(csh, (q, q))
        seg = csl - csl.T
        lmat = jnp.where(tril, jnp.exp(jnp.where(tril, seg, 0.0)), 0.0)
        y = y + jnp.where(hm, mm(cb * lmat, xdt, 1, 0, 1), 0.0)
    st = mm(bm, xdt * jnp.exp(last - csx), 0, 0, 1)
    return y, ht * jnp.exp(last) + st


def _rwkv_chunks(pairs):
    c = pairs[0][0].shape[0]
    ps = RWKV_PASSES
    lane = _iota((1, LANES), 1)
    row = _iota((c, 1), 0)
    ri, ci = _iota((c, c), 0), _iota((c, c), 1)
    tril_i, tril_s = ri >= ci, ri > ci
    eye = (ri == ci).astype(F32)
    half = lane < HEAD_DIM
    halves = (half, jnp.logical_not(half))
    bd = (_iota((LANES, LANES), 0) < HEAD_DIM) == (_iota((LANES, LANES), 1) < HEAD_DIM)
    tri = tril_i.astype(F32)
    n = len(pairs)
    heads = [(j, hm) for j in range(n) for hm in halves]

    cum = [_dot_exact(tri, p[1]) for p in pairs]
    at = [p[4] * jnp.exp(cm - p[1]) for p, cm in zip(pairs, cum)]
    en = [jnp.exp(-cm) for cm in cum]
    bt = [p[5] * e for p, e in zip(pairs, en)]
    kt = [p[2] * e for p, e in zip(pairs, en)]
    rt = [p[0] * jnp.exp(cm) for p, cm in zip(pairs, cum)]
    ah = [mm(at[j], pairs[j][6], 1, 1, ps) for j in range(n)]
    y = [mm(rt[j], pairs[j][6], 1, 1, ps) for j in range(n)]
    atm = [jnp.where(hm, at[j], 0.0) for j, hm in heads]
    rtm = [jnp.where(hm, rt[j], 0.0) for j, hm in heads]
    aab = [jnp.where(tril_s, mm(atm[i], bt[j], 1, 1, ps), 0.0) for i, (j, _) in enumerate(heads)]
    aak = [jnp.where(tril_s, mm(atm[i], kt[j], 1, 1, ps), 0.0) for i, (j, _) in enumerate(heads)]
    arb = [jnp.where(tril_i, mm(rtm[i], bt[j], 1, 1, ps), 0.0) for i, (j, _) in enumerate(heads)]
    ark = [jnp.where(tril_i, mm(rtm[i], kt[j], 1, 1, ps), 0.0) for i, (j, _) in enumerate(heads)]
    rhs = [ah[j] + mm(aak[i], pairs[j][3], 1, 0, ps) for i, (j, _) in enumerate(heads)]
    yv = [mm(ark[i], pairs[j][3], 1, 0, ps) for i, (j, _) in enumerate(heads)]
    tm = [eye + a_ for a_ in aab]
    pm = aab
    for _ in range(int(math.log2(c)) - 1):
        pm = [mm(p_, p_, 1, 0, ps) for p_ in pm]
        tm = [t_ + mm(t_, p_, 1, 0, ps) for t_, p_ in zip(tm, pm)]
    uh = [mm(tm[i], rhs[i], 1, 0, ps) for i in range(len(heads))]
    u = [jnp.where(half, uh[2 * j], uh[2 * j + 1]) for j in range(n)]
    yu = [mm(arb[i], u[j], 1, 0, ps) for i, (j, _) in enumerate(heads)]
    out = []
    for j in range(n):
        yj = y[j] + jnp.where(half, yu[2 * j] + yv[2 * j], yu[2 * j + 1] + yv[2 * j + 1])
        plast = jnp.sum(jnp.where(row == c - 1, cum[j], 0.0), axis=0, keepdims=True)
        upd = pairs[j][6] + mm(u[j], bt[j], 0, 0, ps) + mm(pairs[j][3], kt[j], 0, 0, ps)
        out.append((yj, jnp.where(bd, upd * jnp.exp(plast), 0.0)))
    return out


def _seq_spec(chunk, ppb, col, row_of):
    if col is None:
        return pl.BlockSpec((chunk, ppb * LANES), lambda pb, i: (row_of(i), pb))
    return pl.BlockSpec((chunk, LANES), lambda pb, i: (row_of(i), col(pb * ppb)))


def _pair_vals(refs, seq_in, j):
    return [r[...] if col is not None else r[:, j * LANES:(j + 1) * LANES] for r, (_, col) in zip(refs, seq_in)]


def scan_fwd(name, chunk_fn, chunk, seq_in, const_in, n_pairs, ppb):
    t = seq_in[0][0].shape[0]
    nc = t // chunk
    ns, ncst = len(seq_in), len(const_in)

    def kern(*refs):
        y_ref, st_ref, ht = refs[ns + ncst], refs[ns + ncst + 1], refs[ns + ncst + 2]

        @pl.when(pl.program_id(1) == 0)
        def _():
            ht[...] = jnp.zeros_like(ht)

        cv = [r[...] for r in refs[ns:ns + ncst]]
        h0 = [ht[j] for j in range(ppb)]
        for j in range(ppb):
            st_ref[j] = h0[j]
        sv = [_pair_vals(refs[:ns], seq_in, j) for j in range(ppb)]
        outs = chunk_fn(sv, cv, h0, [pl.program_id(0) * ppb + j for j in range(ppb)])
        for j, (y, hn) in enumerate(outs):
            y_ref[:, j * LANES:(j + 1) * LANES] = y
            ht[j] = hn

    in_specs = [_seq_spec(chunk, ppb, col, lambda i: i) for (_, col) in seq_in]
    in_specs += [pl.BlockSpec(a.shape, lambda pb, i: (0, 0)) for a in const_in]
    return pl.pallas_call(
        kern, name=name, grid=(n_pairs // ppb, nc), in_specs=in_specs,
        out_specs=[pl.BlockSpec((chunk, ppb * LANES), lambda pb, i: (i, pb)),
                   pl.BlockSpec((ppb, None, LANES, LANES), lambda pb, i: (pb, i, 0, 0))],
        out_shape=[jax.ShapeDtypeStruct((t, n_pairs * LANES), F32), jax.ShapeDtypeStruct((n_pairs, nc, LANES, LANES), F32)],
        scratch_shapes=[pltpu.VMEM((ppb, LANES, LANES), F32)],
        compiler_params=_params(("arbitrary", "arbitrary")),
    )(*[s[0] for s in seq_in], *const_in)


def scan_bwd(name, chunk_fn, chunk, seq_in, const_in, states, dy, n_pairs, ppb):
    t = dy.shape[0]
    nc = t // chunk
    ns, ncst = len(seq_in), len(const_in)

    def kern(*refs):
        seq_refs, cst_refs = refs[:ns], refs[ns:ns + ncst]
        st_ref, dy_ref = refs[ns + ncst], refs[ns + ncst + 1]
        o = ns + ncst + 2
        dseq_refs, dcst_refs, dht = refs[o:o + ns], refs[o + ns:o + ns + ncst], refs[o + ns + ncst]
        pb, i = pl.program_id(0), pl.program_id(1)

        @pl.when(i == 0)
        def _():
            dht[...] = jnp.zeros_like(dht)

        ids = [pb * ppb + j for j in range(ppb)]
        lanes = [slice(j * LANES, (j + 1) * LANES) for j in range(ppb)]

        def fn(*flat):
            sv = [list(flat[j * ns:(j + 1) * ns]) for j in range(ppb)]
            outs = chunk_fn(sv, list(flat[ppb * ns:ppb * ns + ncst]), list(flat[ppb * ns + ncst:]), ids)
            return tuple(y for y, _ in outs), tuple(h for _, h in outs)

        flat_in = [v for j in range(ppb) for v in _pair_vals(seq_refs, seq_in, j)]
        flat_in += [r[...] for r in cst_refs] + [st_ref[j] for j in range(ppb)]
        _, vjp = jax.vjp(fn, *flat_in)
        grads = vjp((tuple(dy_ref[:, ln] for ln in lanes), tuple(dht[j] for j in range(ppb))))
        for j in range(ppb):
            for r, g in zip(dseq_refs, grads[j * ns:(j + 1) * ns]):
                r[:, lanes[j]] = g
            dht[j] = grads[ppb * ns + ncst + j]
        dcv = grads[ppb * ns:ppb * ns + ncst]
        if ncst:
            first = jnp.logical_and(pb == 0, i == 0)

            @pl.when(first)
            def _():
                for r, g in zip(dcst_refs, dcv):
                    r[...] = g

            @pl.when(jnp.logical_not(first))
            def _():
                for r, g in zip(dcst_refs, dcv):
                    r[...] += g

    rev = lambda i: nc - 1 - i
    wide = pl.BlockSpec((chunk, ppb * LANES), lambda pb, i: (rev(i), pb))
    in_specs = [_seq_spec(chunk, ppb, col, rev) for (_, col) in seq_in]
    in_specs += [pl.BlockSpec(a.shape, lambda pb, i: (0, 0)) for a in const_in]
    in_specs += [pl.BlockSpec((ppb, None, LANES, LANES), lambda pb, i: (pb, rev(i), 0, 0)), wide]
    out_specs = [wide for _ in seq_in]
    out_specs += [pl.BlockSpec(a.shape, lambda pb, i: (0, 0)) for a in const_in]
    out_shape = [jax.ShapeDtypeStruct((t, n_pairs * LANES), F32) for _ in seq_in]
    out_shape += [jax.ShapeDtypeStruct(a.shape, F32) for a in const_in]
    res = pl.pallas_call(
        kern, name=name, grid=(n_pairs // ppb, nc), in_specs=in_specs, out_specs=out_specs, out_shape=out_shape,
        scratch_shapes=[pltpu.VMEM((ppb, LANES, LANES), F32)],
        compiler_params=_params(("arbitrary", "arbitrary")),
    )(*[s[0] for s in seq_in], *const_in, states, dy)
    return list(res[:ns]), list(res[ns:])


def loss_head(x3, tgt, g, tr):
    rows, d = x3.shape

    def body(tv, fv):
        def f(x, gg):
            e = jnp.square(_rms(x, gg) - tv[1])
            return 0.5 * jnp.sum(jnp.mean(e, axis=-1, keepdims=True), axis=0, keepdims=True)
        l, vjp = jax.vjp(f, tv[0], fv[0])
        dx, dg = vjp(jnp.ones((1, 1), F32))
        return [dx, dx], [dg, jnp.broadcast_to(l, (8, LANES))]
    (dx, dxb), (dg, l) = row_call("loss_head", body, rows // tr, [(x3, tr, d, 0), (tgt, tr, d, 0)], [g],
                                  [(rows, tr, d, F32), (rows, tr, d, BF16)], [g.shape, (8, LANES)])
    return dx, dxb, dg, l


def _adam_math(w, g, m, v):
    m = ADAM_B1 * m + (1.0 - ADAM_B1) * g
    v = ADAM_B2 * v + (1.0 - ADAM_B2) * jnp.square(g)
    m_hat = m / (1.0 - ADAM_B1 ** ADAM_STEP)
    v_hat = v / (1.0 - ADAM_B2 ** ADAM_STEP)
    delta = -ADAM_LR * (m_hat / (jnp.sqrt(v_hat) + ADAM_EPS) + ADAM_WD * w)
    return delta, m, v


def _tiling(rows, cols, limit):
    row_tile = max([d for d in range(16, rows + 1, 16) if rows % d == 0 and d * cols <= limit], default=0)
    col_tile = max([ct for ct in range(LANES, cols + 1, LANES) if cols % ct == 0 and rows * ct <= limit], default=0)
    if row_tile and row_tile * cols >= rows * col_tile:
        return row_tile, cols
    return (rows, col_tile) if col_tile else (rows, cols)


def ew_call(name, fn, ins, out_dtypes, limit=1 << 20):
    rows, cols = ins[0].shape
    br, bc = _tiling(rows, cols, limit)
    spec = pl.BlockSpec((br, bc), lambda i, j: (i, j))
    n_in = len(ins)

    def kern(*refs):
        for r, v in zip(refs[n_in:], fn(*[r[...] for r in refs[:n_in]])):
            r[...] = v.astype(r.dtype)

    return pl.pallas_call(
        kern, name=name, grid=(rows // br, cols // bc), in_specs=[spec] * n_in, out_specs=[spec] * len(out_dtypes),
        out_shape=[jax.ShapeDtypeStruct((rows, cols), dt) for dt in out_dtypes],
        compiler_params=_params(("parallel", "parallel")),
    )(*ins)


def adamw(name, w, m, v, g):
    return ew_call(name, lambda wv, mv, vv, gv: (gv, *_adam_math(wv, gv, mv, vv)), [w, m, v, g], [F32] * 4, 1 << 18)


def sum_slots(name, r):
    _, rows, cols = r.shape
    br, bc = _tiling(rows, cols, 1 << 20)

    def kern(r0, r1, r2, r3, o):
        o[...] = ((r0[...].astype(F32) + r1[...].astype(F32)) + r2[...].astype(F32)) + r3[...].astype(F32)

    in_specs = [pl.BlockSpec((None, br, bc), functools.partial(lambda i, j, s: (s, i, j), s=s)) for s in range(4)]
    return pl.pallas_call(
        kern, name=name, grid=(rows // br, cols // bc), in_specs=in_specs,
        out_specs=pl.BlockSpec((br, bc), lambda i, j: (i, j)),
        out_shape=jax.ShapeDtypeStruct((rows, cols), F32), compiler_params=_params(("parallel", "parallel")),
    )(r, r, r, r)


def _my_place():
    return lax.axis_index("x"), lax.axis_index("y"), lax.axis_index("c")


def _chip_peers(x, y):
    peers = [(1 - x, y), (x, 1 - y), (1 - x, 1 - y)]
    return peers, [2 * px + py for px, py in peers]


def gather_shards(name, arrays):
    nw = len(arrays)
    ANY = pl.BlockSpec(memory_space=pl.ANY)

    def body(*refs):
        ins, outs = refs[:nw], refs[nw:2 * nw]
        send, recv, loc = refs[2 * nw:]
        x, y, c = _my_place()
        q = 2 * x + y
        peers, chips = _chip_peers(x, y)

        def remote(w, j, slot):
            return pltpu.make_async_remote_copy(
                src_ref=ins[w], dst_ref=outs[w].at[slot], send_sem=send.at[w, j], recv_sem=recv.at[w, j],
                device_id=(*peers[j], c), device_id_type=MESH_ID)

        local = [pltpu.make_async_copy(ins[w], outs[w].at[q], loc.at[w]) for w in range(nw)]
        sends = [[remote(w, j, q) for j in range(3)] for w in range(nw)]
        for w in range(nw):
            local[w].start()
            for j in range(3):
                sends[w][j].start()
        for w in range(nw):
            local[w].wait()
            for j in range(3):
                sends[w][j].wait_send()
                remote(w, j, chips[j]).wait_recv()

    return pl.pallas_call(
        body, name=name, in_specs=[ANY] * nw, out_specs=[ANY] * nw,
        out_shape=[jax.ShapeDtypeStruct((4,) + a.shape, a.dtype) for a in arrays],
        scratch_shapes=[pltpu.SemaphoreType.DMA((nw, 3)), pltpu.SemaphoreType.DMA((nw, 3)), pltpu.SemaphoreType.DMA((nw,))],
        compiler_params=pltpu.CompilerParams(has_side_effects=True),
    )(*arrays)


def scatter_slots(name, arrays, collective_id):
    nw = len(arrays)

    def body(*refs):
        ins, outs = refs[:nw], refs[nw:2 * nw]
        send, recv, loc = refs[2 * nw:]
        x, y, c = _my_place()
        q = 2 * x + y
        peers, chips = _chip_peers(x, y)
        barrier = pltpu.get_barrier_semaphore()
        for p in peers:
            pl.semaphore_signal(barrier, inc=1, device_id=(*p, c), device_id_type=MESH_ID)
        pl.semaphore_wait(barrier, 3)

        def remote(w, j, src_slot, dst_slot):
            return pltpu.make_async_remote_copy(
                src_ref=ins[w].at[src_slot], dst_ref=outs[w].at[dst_slot], send_sem=send.at[w, j], recv_sem=recv.at[w, j],
                device_id=(*peers[j], c), device_id_type=MESH_ID)

        sends = [[remote(w, j, chips[j], q) for j in range(3)] for w in range(nw)]
        own = [pltpu.make_async_copy(ins[w].at[q], outs[w].at[q], loc.at[w]) for w in range(nw)]
        for w in range(nw):
            for j in range(3):
                sends[w][j].start()
            own[w].start()
        for w in range(nw):
            for j in range(3):
                sends[w][j].wait_send()
                remote(w, j, q, chips[j]).wait_recv()
            own[w].wait()

    return pl.kernel(
        body, out_type=[jax.ShapeDtypeStruct(a.shape, a.dtype) for a in arrays],
        mesh=plsc.ScalarSubcoreMesh(axis_name="sequencer", num_cores=1), name=name,
        scratch_types=[pltpu.SemaphoreType.DMA((nw, 3)), pltpu.SemaphoreType.DMA((nw, 3)), pltpu.SemaphoreType.DMA((nw,))],
        compiler_params=pltpu.CompilerParams(collective_id=collective_id),
    )(*arrays)


def _halves_by_cols(rows):
    return rows % 32 != 0


def _half_of(ref, shape, h):
    rows, cols = shape
    if _halves_by_cols(rows):
        return ref.at[:, pl.ds(h * (cols // 2), cols // 2)]
    return ref.at[pl.ds(h * (rows // 2), rows // 2)]


def _half_value(a, h):
    rows, cols = a.shape[-2:]
    if _halves_by_cols(rows):
        return lax.dynamic_slice_in_dim(a, h * (cols // 2), cols // 2, axis=a.ndim - 1)
    return lax.dynamic_slice_in_dim(a, h * (rows // 2), rows // 2, axis=a.ndim - 2)


def _join_halves(lo, hi, rows):
    return jnp.concatenate([lo, hi], axis=lo.ndim - 1 if _halves_by_cols(rows) else lo.ndim - 2)


def gather_two_level(name, arrays, collective_id):
    nw = len(arrays)
    shapes = [a.shape for a in arrays]

    def body(*refs):
        ins, outs = refs[:nw], refs[nw:2 * nw]
        send, recv, loc = refs[2 * nw:]
        x, y, c = _my_place()
        q = 2 * x + y
        me, sibling = (x, y, c), (x, y, 1 - c)
        peers = [(1 - x, y), (x, 1 - y), (1 - x, 1 - y)]
        chips = [2 * px + py for px, py in peers]
        barrier = pltpu.get_barrier_semaphore()
        for dev in [sibling] + [(*p, c) for p in peers]:
            pl.semaphore_signal(barrier, inc=1, device_id=dev, device_id_type=MESH_ID)
        pl.semaphore_wait(barrier, 4)

        def mine(w):
            return _half_of(ins[w], shapes[w], c)

        def landed(w, chip, half):
            return _half_of(outs[w].at[chip], shapes[w], half)

        def copy(w, k, src, chip, half, to):
            return pltpu.make_async_remote_copy(
                src_ref=src, dst_ref=landed(w, chip, half), send_sem=send.at[w, k], recv_sem=recv.at[w, k],
                device_id=to, device_id_type=MESH_ID)

        first = [[copy(w, 0, mine(w), q, c, sibling)] + [copy(w, 1 + j, mine(w), q, c, (*peers[j], c)) for j in range(3)]
                 for w in range(nw)]
        own = [pltpu.make_async_copy(mine(w), landed(w, q, c), loc.at[w]) for w in range(nw)]
        for w in range(nw):
            for cp in first[w]:
                cp.start()
            own[w].start()
        passed = []
        for w in range(nw):
            for j in range(3):
                copy(w, 1 + j, mine(w), chips[j], c, me).wait_recv()
                fwd = copy(w, 4 + j, landed(w, chips[j], c), chips[j], c, sibling)
                fwd.start()
                passed.append(fwd)
        for w in range(nw):
            copy(w, 0, mine(w), q, 1 - c, me).wait_recv()
            for j in range(3):
                copy(w, 4 + j, mine(w), chips[j], 1 - c, me).wait_recv()
        for w in range(nw):
            for cp in first[w]:
                cp.wait_send()
            own[w].wait()
        for cp in passed:
            cp.wait_send()

    out_type = [jax.ShapeDtypeStruct((4,) + a.shape, a.dtype) for a in arrays]
    return pl.kernel(
        body, out_type=out_type, mesh=plsc.ScalarSubcoreMesh(axis_name="sequencer", num_cores=1), name=name,
        scratch_types=[pltpu.SemaphoreType.DMA((nw, 7)), pltpu.SemaphoreType.DMA((nw, 7)), pltpu.SemaphoreType.DMA((nw,))],
        compiler_params=pltpu.CompilerParams(collective_id=collective_id),
    )(*arrays)


def core_swap(name, arrays):
    nw = len(arrays)
    ANY = pl.BlockSpec(memory_space=pl.ANY)

    def body(*refs):
        ins, outs = refs[:nw], refs[nw:2 * nw]
        send, recv = refs[2 * nw:]
        x, y, c = _my_place()
        copies = [pltpu.make_async_remote_copy(
            src_ref=ins[w], dst_ref=outs[w], send_sem=send.at[w], recv_sem=recv.at[w],
            device_id=(x, y, 1 - c), device_id_type=MESH_ID) for w in range(nw)]
        for cp in copies:
            cp.start()
        for cp in copies:
            cp.wait_send()
            cp.wait_recv()

    return pl.pallas_call(
        body, name=name, in_specs=[ANY] * nw, out_specs=[ANY] * nw,
        out_shape=[jax.ShapeDtypeStruct(a.shape, a.dtype) for a in arrays],
        scratch_shapes=[pltpu.SemaphoreType.DMA((nw,)), pltpu.SemaphoreType.DMA((nw,))],
        compiler_params=pltpu.CompilerParams(has_side_effects=True),
    )(*arrays)


def all_reduce_small(name, v):
    rows = v.shape[0]
    VM = pl.BlockSpec(memory_space=pltpu.VMEM)

    def body(v_ref, o_ref, buf, send, recv):
        x, y, c = _my_place()
        me = 4 * x + 2 * y + c

        def peer(kx):
            return (x ^ ((kx >> 2) & 1), y ^ ((kx >> 1) & 1), c ^ (kx & 1))

        def copy(kx, slot):
            return pltpu.make_async_remote_copy(
                src_ref=v_ref, dst_ref=buf.at[slot], send_sem=send.at[kx - 1], recv_sem=recv.at[kx - 1],
                device_id=peer(kx), device_id_type=MESH_ID)

        sends = [copy(kx, me) for kx in range(1, 8)]
        for cp in sends:
            cp.start()
        buf[me] = v_ref[...]
        for kx in range(1, 8):
            copy(kx, me ^ kx).wait_recv()
        for cp in sends:
            cp.wait_send()
        acc = buf[0]
        for d in range(1, 8):
            acc = acc + buf[d]
        o_ref[...] = acc

    return pl.pallas_call(
        body, name=name, in_specs=[VM], out_specs=VM, out_shape=jax.ShapeDtypeStruct(v.shape, F32),
        scratch_shapes=[pltpu.VMEM((8, rows, LANES), F32), pltpu.SemaphoreType.DMA((7,)), pltpu.SemaphoreType.DMA((7,))],
        compiler_params=pltpu.CompilerParams(has_side_effects=True, vmem_limit_bytes=VMEM_LIMIT),
    )(v)


def _pad_cols(a, n):
    return jnp.pad(a, ((0, 0), (0, n - a.shape[1])))


def _pad_rows(a, n):
    return jnp.pad(a, ((0, n - a.shape[0]), (0, 0)))


def _halo(u, tr):
    t, cdim = u.shape
    tails = u.reshape(t // tr, tr, cdim)[:, tr - HALO:, :]
    tails = jnp.concatenate([jnp.zeros((1, HALO, cdim), u.dtype), tails[:-1]], axis=0)
    return tails.reshape(-1, cdim)


def _unhalo(du, dhalo, tr):
    t, cdim = du.shape
    n = t // tr
    dh = dhalo.reshape(n, HALO, cdim)
    dh = jnp.concatenate([dh[1:], jnp.zeros((1, HALO, cdim), du.dtype)], axis=0)
    d3 = du.reshape(n, tr, cdim)
    d3 = jnp.concatenate([d3[:, :tr - HALO, :], d3[:, tr - HALO:, :] + dh], axis=1)
    return d3.reshape(t, cdim)


def _to_slots(g, axis):
    r, cdim = g.shape
    if axis == 0:
        return g.reshape(4, r // 4, cdim)
    return g.reshape(r, 4, cdim // 4).transpose(1, 0, 2)


def _from_slots(s, axis):
    if axis == 0:
        return s.reshape(s.shape[0] * s.shape[1], s.shape[2])
    return s.transpose(1, 0, 2).reshape(s.shape[1], 4 * s.shape[2])


BIG = ("w_in", "w_out", "xattn_wq", "xattn_wk", "xattn_wv", "xattn_wo", "ffn_w1", "ffn_w2")
TRANSPOSED = ("w_in",)
BIG_AXIS = {"w_in": 0, "w_out": 0, "xattn_wq": 0, "xattn_wk": 0, "xattn_wv": 0, "xattn_wo": 0, "ffn_w1": 1, "ffn_w2": 0}
SMALL_SHARDED = ("ssd_conv_w", "rwkv_w2", "rwkv_a2", "rwkv_g2")
GATHER_GROUPS = (("w_in",), ("w_out", "xattn_wq", "xattn_wk", "xattn_wv", "xattn_wo"), ("ffn_w1", "ffn_w2"))
REDUCE_GROUPS = (("ffn_w2", "ffn_w1"), ("xattn_wo", "xattn_wq", "xattn_wk", "xattn_wv", "w_out"),
                 ("rwkv_w2", "rwkv_a2", "rwkv_g2", "w_in"))
REDUCED = BIG + ("rwkv_w2", "rwkv_a2", "rwkv_g2")
REDUCE_AXIS = dict(BIG_AXIS, rwkv_w2=1, rwkv_a2=1, rwkv_g2=1)
WEIGHTS = ("norm_mix_g", "w_in", "ssd_conv_w", "ssd_conv_b", "ssd_dt_bias", "ssd_a_log", "ssd_d", "ssd_norm_g",
           "rwkv_mu", "rwkv_w0", "rwkv_w2", "rwkv_a0", "rwkv_a2", "rwkv_g2", "rwkv_k_k", "rwkv_k_a", "rwkv_r_k",
           "rwkv_ln_w", "rwkv_ln_b", "w_out", "norm_x_g", "norm_mem_g", "xattn_wq", "xattn_wk", "xattn_wv", "xattn_wo",
           "norm_ffn_g", "ffn_w1", "ffn_w2", "final_norm_g")


def _local_grads(x, mem, tgt, wt, full, big, reducer):
    t, d = x.shape
    w = d // 2
    nh = w // HEAD_DIM
    n_pairs = nh // 2
    ppg = n_pairs // SSD_GROUPS
    bc = SSD_GROUPS * SSD_STATE
    conv_dim = w + 2 * bc
    tr = ROW_TILE
    nt = t // tr
    tr2 = 2 * tr if t % (2 * tr) == 0 else tr
    nt2 = t // tr2
    dr = wt["rwkv_w2"].shape[0]
    ar = wt["rwkv_a2"].shape[0]
    gr = wt["rwkv_g2"].shape[0]

    big.start(0, None)
    big.start(1, None)
    h1, _ = norm_fwd("norm_mix", x, wt["norm_mix_g"], tr2, with_transpose=False)
    w_in_t = big.get("w_in", h1)
    o = 0
    segs = {}
    for nm, width in (("z", w), ("xbc", conv_dim), ("dt", nh), ("rkv", 3 * w), ("pw", dr), ("pa", ar), ("pg", gr)):
        segs[nm] = (o, width)
        o += width
    padded = {"z": w, "xbc": conv_dim, "dt": LANES, "rkv": 3 * w, "pw": LANES, "pa": LANES, "pg": gr}
    order = ("z", "xbc", "dt", "rkv", "pw", "pa", "pg")
    w_segs = [jnp.concatenate([_pad_rows(w_in_t[segs[nm][0]:segs[nm][0] + segs[nm][1]], padded[nm]) for nm in grp], axis=0)
              for grp in (("z",), ("xbc",), ("dt",), ("rkv",), ("pw", "pa", "pg"))]
    w_perm_t = jnp.concatenate(w_segs, axis=0)
    offs = {}
    o = 0
    for nm in order:
        offs[nm] = o
        o += padded[nm]
    lora_w = 2 * LANES + gr

    mu = wt["rwkv_mu"]
    mo = 3 * w
    mu_rkv = mu[:, :mo]
    mu_lora = jnp.concatenate([_pad_cols(mu[:, mo:mo + dr], LANES), _pad_cols(mu[:, mo + dr:mo + dr + ar], LANES),
                               mu[:, mo + dr + ar:]], axis=1)
    w2p = _pad_rows(full["rwkv_w2"], LANES)
    a2p = _pad_rows(full["rwkv_a2"], LANES)
    g2 = full["rwkv_g2"]
    conv_w = full["ssd_conv_w"]
    cw = [conv_w[i:i + 1] for i in range(SSD_CONV)]
    dt_bias = _pad_cols(wt["ssd_dt_bias"], LANES)
    a_log = _pad_cols(wt["ssd_a_log"], LANES)
    d_skip = _pad_cols(wt["ssd_d"], LANES)
    r_k = wt["rwkv_r_k"].reshape(1, w)

    z, xbc, dtraw, urkv, ulora = [matmul("in_proj_%d" % i, h1, ws, tb=True) for i, ws in enumerate(w_segs)]
    big.start(2, urkv)

    halo_xbc = _halo(xbc, tr)
    ssd_pre_t = [(xbc, tr, conv_dim, 0), (halo_xbc, HALO, conv_dim, 0), (dtraw, tr, LANES, 0)]
    ssd_pre_f = cw + [wt["ssd_conv_b"], dt_bias]
    act, dt = fn_fwd("ssd_pre", _ssd_pre, nt, ssd_pre_t, ssd_pre_f, [(t, tr, conv_dim, F32), (t, tr, LANES, F32)])

    nb = w // LANES
    ssd_seq = [(act, None), (act, lambda p: nb + p // ppg), (act, lambda p: nb + SSD_GROUPS + p // ppg), (dt, lambda p: 0)]
    ssd_ppb = min(ppg, PAIRS_PER_STEP)
    rw_ppb = min(n_pairs, 2 * PAIRS_PER_STEP)

    def ssd_fn(sv, cv, hts, ids):
        return [_ssd_chunk(*s, cv[0], ht, p) for s, ht, p in zip(sv, hts, ids)]

    y_scan, ssd_states = scan_fwd("ssd_scan", ssd_fn, SSD_CHUNK, ssd_seq, [a_log], n_pairs, ssd_ppb)
    ssd_post_t = [(y_scan, tr, w, 0), (act, tr, w, 0), (z, tr, w, 0)]
    ssd_post_f = [d_skip, wt["ssd_norm_g"]]
    y_ssd, y_ssd_t = fn_fwd("ssd_post", _ssd_post, nt, ssd_post_t, ssd_post_f, [(t, tr, w, BF16)], (0,))

    halo_rkv, halo_lora = _halo(urkv, tr), _halo(ulora, tr)
    rw_pre_t = [(urkv, tr, 3 * w, 0), (ulora, tr, lora_w, 0), (halo_rkv, HALO, 3 * w, 0), (halo_lora, HALO, lora_w, 0)]
    rw_pre_f = [mu_rkv, mu_lora, wt["rwkv_w0"], wt["rwkv_a0"], wt["rwkv_k_k"], wt["rwkv_k_a"], w2p, a2p, g2]
    rw = fn_fwd("rwkv_pre", _rwkv_pre, nt, rw_pre_t, rw_pre_f, [(t, tr, w, F32)] * 7)
    r_, lw_, k2_, v_, nkk_, b_, gate_ = rw
    rw_seq = [(a, None) for a in (r_, lw_, k2_, v_, nkk_, b_)]

    def rw_fn(sv, cv, hts, ids):
        return _rwkv_chunks([(*s, ht) for s, ht in zip(sv, hts)])

    yr_scan, rw_states = scan_fwd("rwkv_scan", rw_fn, RWKV_CHUNK, rw_seq, [], n_pairs, rw_ppb)
    rw_post_t = [(a, tr, w, 0) for a in (yr_scan, r_, k2_, v_, gate_)]
    rw_post_f = [r_k, wt["rwkv_ln_w"], wt["rwkv_ln_b"]]
    y_rwkv, y_rwkv_t = fn_fwd("rwkv_post", _rwkv_post, nt, rw_post_t, rw_post_f, [(t, tr, w, BF16)], (0,))

    ymix = jnp.concatenate([y_ssd, y_rwkv], axis=1)
    ymix_t = jnp.concatenate([y_ssd_t, y_rwkv_t], axis=0)
    w_out = big.get("w_out", ymix)
    x1 = matmul("out_proj", ymix, w_out, resid=x)

    h2, h2t = norm_fwd("norm_x", x1, wt["norm_x_g"], tr2)
    mrows = mem.shape[0]
    mn, mnt = norm_fwd("norm_mem", mem, wt["norm_mem_g"], mrows)
    wq, wk, wv, wo = [big.get(nm, ymix) for nm in ("xattn_wq", "xattn_wk", "xattn_wv", "xattn_wo")]
    q = matmul("xattn_q", h2, wq)
    kx = matmul("xattn_k", mn, wk)
    vx = matmul("xattn_v", mn, wv)
    ao, aot = fn_fwd("xattn_core", _attn, nt2, [(q, tr2, d, 0)], [kx, vx], [(t, tr2, d, BF16)], (0,))
    x2 = matmul("xattn_o", ao, wo, resid=x1)

    h3, h3t = norm_fwd("norm_ffn", x2, wt["norm_ffn_g"], tr2)
    w1, w2 = big.get("ffn_w1", h3), big.get("ffn_w2", h3)
    a1 = matmul("ffn_up", h3, w1, out_dtype=BF16)
    dff = a1.shape[1]
    f1, f1t = fn_fwd("ffn_act", _relu2, nt, [(a1, tr, dff, 0)], [], [(t, tr, dff, BF16)], (0,))
    x3 = matmul("ffn_down", f1, w2, resid=x2)

    dx3, dx3b, g_final, loss_tile = loss_head(x3, tgt, wt["final_norm_g"].reshape(1, d), tr2)

    grads = {"final_norm_g": g_final.reshape(d)}
    grads["ffn_w2"] = matmul("ffn_down_dw", f1t, dx3b)
    df1 = matmul("ffn_down_dx", dx3b, w2, tb=True, out_dtype=BF16)
    (da1,), _ = fn_bwd("ffn_act_bwd", _relu2, nt, [(a1, tr, dff, 0)], [], [(df1, tr, dff, 0)], lambda c: [c[0].astype(F32)],
                       [(t, tr, dff, BF16)])
    grads["ffn_w1"] = matmul("ffn_up_dw", h3t, da1, out_slots=4)
    dh3 = reducer.launch(0, grads, matmul("ffn_up_dx", da1, w1, tb=True))
    dx2, dx2b, grads["norm_ffn_g"] = norm_bwd("norm_ffn_bwd", x2, wt["norm_ffn_g"], dh3, dx3, tr2)

    grads["xattn_wo"] = matmul("xattn_o_dw", aot, dx2b)
    dao = matmul("xattn_o_dx", dx2b, wo, tb=True)
    (dq,), (dkx, dvx) = fn_bwd("xattn_core_bwd", _attn, nt2, [(q, tr2, d, 0)], [kx, vx], [(dao, tr2, d, 0)], lambda c: c,
                               [(t, tr2, d, BF16)])
    grads["xattn_wq"] = matmul("xattn_q_dw", h2t, dq)
    dh2 = matmul("xattn_q_dx", dq, wq, tb=True)
    dkb, dvb = dkx.astype(BF16), dvx.astype(BF16)
    grads["xattn_wk"] = matmul("xattn_k_dw", mnt, dkb)
    grads["xattn_wv"] = matmul("xattn_v_dw", mnt, dvb)
    dmn = matmul("xattn_k_dx", dkb, wk, tb=True)
    dmn = matmul("xattn_v_dx", dvb, wv, tb=True, resid=dmn)
    _, _, grads["norm_mem_g"] = norm_bwd("norm_mem_bwd", mem, wt["norm_mem_g"], dmn, None, mrows)
    dx1, dx1b, grads["norm_x_g"] = norm_bwd("norm_x_bwd", x1, wt["norm_x_g"], dh2, dx2, tr2)

    grads["w_out"] = matmul("out_proj_dw", ymix_t, dx1b)
    dymix = reducer.launch(1, grads, matmul("out_proj_dx", dx1b, w_out, tb=True))

    (dyr, dr1, dk1, dv1, dgate), (g_rk, grads["rwkv_ln_w"], grads["rwkv_ln_b"]) = fn_bwd(
        "rwkv_post_bwd", _rwkv_post, nt, rw_post_t, rw_post_f, [(dymix, tr, w, 1)], lambda c: c, [(t, tr, w, F32)] * 5)
    grads["rwkv_r_k"] = g_rk.reshape(wt["rwkv_r_k"].shape)
    (dr2, dlw, dk2, dv2, dnkk, db), _ = scan_bwd("rwkv_scan_bwd", rw_fn, RWKV_CHUNK, rw_seq, [], rw_states, dyr, n_pairs, rw_ppb)
    rw_ct = [(a, tr, w, 0) for a in (dr1, dr2, dlw, dk1, dk2, dv1, dv2, dnkk, db, dgate)]

    def rw_ct_fn(c):
        return (c[0] + c[1], c[2], c[3] + c[4], c[5] + c[6], c[7], c[8], c[9])

    (durkv, dulora, dhrkv, dhlora), rw_pg = fn_bwd(
        "rwkv_pre_bwd", _rwkv_pre, nt, rw_pre_t, rw_pre_f, rw_ct, rw_ct_fn,
        [(t, tr, 3 * w, F32), (t, tr, lora_w, F32), (nt * HALO, HALO, 3 * w, F32), (nt * HALO, HALO, lora_w, F32)])
    durkv = _unhalo(durkv, dhrkv, tr)
    dulora = _unhalo(dulora, dhlora, tr)
    g_mu_rkv, g_mu_lora, grads["rwkv_w0"], grads["rwkv_a0"], grads["rwkv_k_k"], grads["rwkv_k_a"], g_w2p, g_a2p, grads["rwkv_g2"] = rw_pg
    grads["rwkv_mu"] = jnp.concatenate([g_mu_rkv, g_mu_lora[:, :dr], g_mu_lora[:, LANES:LANES + ar], g_mu_lora[:, 2 * LANES:]], axis=1)
    grads["rwkv_w2"] = g_w2p[:dr]
    grads["rwkv_a2"] = g_a2p[:ar]

    (dys, dxs1, dz), (g_d, grads["ssd_norm_g"]) = fn_bwd(
        "ssd_post_bwd", _ssd_post, nt, ssd_post_t, ssd_post_f, [(dymix, tr, w, 0)], lambda c: c, [(t, tr, w, F32)] * 3)
    grads["ssd_d"] = g_d[:, :nh]
    (dxs2, dbp, dcp, ddtp), (g_alog,) = scan_bwd("ssd_scan_bwd", ssd_fn, SSD_CHUNK, ssd_seq, [a_log], ssd_states, dys, n_pairs, ssd_ppb)
    grads["ssd_a_log"] = g_alog[:, :nh]
    ssd_ct = [(dxs1, tr, w, 0), (dxs2, tr, w, 0), (dbp, tr, w, 0), (dcp, tr, w, 0), (ddtp, tr, w, 0)]

    def ssd_ct_fn(c):
        def group_sum(a):
            parts = []
            for gi in range(SSD_GROUPS):
                s = a[:, gi * ppg * LANES:(gi * ppg + 1) * LANES]
                for j in range(1, ppg):
                    s = s + a[:, (gi * ppg + j) * LANES:(gi * ppg + j + 1) * LANES]
                parts.append(s)
            return parts
        ddt = c[4][:, :LANES]
        for j in range(1, n_pairs):
            ddt = ddt + c[4][:, j * LANES:(j + 1) * LANES]
        return (jnp.concatenate([c[0] + c[1]] + group_sum(c[2]) + group_sum(c[3]), axis=1), ddt)

    (dxbc, dhxbc, ddtraw), ssd_pg = fn_bwd(
        "ssd_pre_bwd", _ssd_pre, nt, ssd_pre_t, ssd_pre_f, ssd_ct, ssd_ct_fn,
        [(t, tr, conv_dim, F32), (nt * HALO, HALO, conv_dim, F32), (t, tr, LANES, F32)])
    dxbc = _unhalo(dxbc, dhxbc, tr)
    grads["ssd_conv_w"] = jnp.concatenate(ssd_pg[:SSD_CONV], axis=0)
    grads["ssd_conv_b"] = ssd_pg[SSD_CONV]
    grads["ssd_dt_bias"] = ssd_pg[SSD_CONV + 1][:, :nh]

    du = jnp.concatenate([dz, dxbc, ddtraw, durkv, dulora], axis=1).astype(BF16)
    g_perm_t = matmul("in_proj_dw", du.T, h1)
    grads["w_in"] = jnp.concatenate([g_perm_t[offs[nm]:offs[nm] + segs[nm][1]] for nm in order], axis=0)
    dh1 = matmul("in_proj_dx", du, w_perm_t)
    dh1 = reducer.launch(2, grads, dh1)
    grad_x, _, grads["norm_mix_g"] = norm_bwd("norm_mix_bwd", x, wt["norm_mix_g"], dh1, dx1, tr2)
    return loss_tile, grad_x, grads


def _pack(arrs):
    flat = jnp.concatenate([a.reshape(-1) for a in arrs])
    n = flat.shape[0]
    rows = -(-n // (8 * LANES)) * 8
    return jnp.pad(flat, (0, rows * LANES - n)).reshape(rows, LANES)


def _unpack(packed, shapes):
    flat = packed.reshape(-1)
    out, o = [], 0
    for s in shapes:
        n = math.prod(s)
        out.append(flat[o:o + n].reshape(s))
        o += n
    return out


def _as2d(a):
    return a.reshape(-1, a.shape[-1])


def _shard_view(n, a):
    return _as2d(a[0]).T if n in TRANSPOSED else _as2d(a[0])


class _GatheredWeights:
    def __init__(self, shard2d, q, c):
        self.shard2d, self.q, self.c = shard2d, q, c
        self.raw, self.ready = {}, {}

    def start(self, gi, after):
        shards = [self.shard2d[n].astype(BF16) for n in GATHER_GROUPS[gi]]
        if after is not None:
            shards, _ = lax.optimization_barrier((shards, after))
        gathered = gather_two_level("gather_weights_%d" % gi, shards, gi + 1)
        self.raw.update(zip(GATHER_GROUPS[gi], gathered))

    def get(self, name, after):
        if name not in self.ready:
            g = self.raw[name]
            if after is not None:
                g, _ = lax.optimization_barrier((g, after))
            self.ready[name] = _from_slots(g, 0) if BIG_AXIS[name] == 0 else g
        return self.ready[name]


class _GradReducer:
    def __init__(self, q, c, update):
        self.q, self.c, self.update = q, c, update
        self.pending, self.updated = {}, {}

    def launch(self, gi, grads, nxt):
        names = REDUCE_GROUPS[gi]
        kept, sent, rows = [], [], []
        for n in names:
            s = grads[n] if grads[n].ndim == 3 else _to_slots(grads[n], REDUCE_AXIS[n])
            rows.append(s.shape[1])
            kept.append(_half_value(s, self.c))
            sent.append(_half_value(s, 1 - self.c).astype(BF16))
        got = core_swap("swap_halves_%d" % gi, sent)
        parts = []
        for n, k, g in zip(names, kept, got):
            _, hr, hc = k.shape
            (part,) = ew_call("chip_sum_" + n, lambda kv, gv: (kv + gv.astype(F32),),
                              [k.reshape(4 * hr, hc), g.reshape(4 * hr, hc)], [BF16])
            parts.append(part.reshape(4, hr, hc))
        parts, nxt = lax.optimization_barrier((parts, nxt))
        slots = scatter_slots("scatter_grads_%d" % gi, parts, len(GATHER_GROUPS) + 1 + gi)
        self.pending[gi] = (slots, rows)
        return self.finish(gi - 1, nxt) if gi > 0 else nxt

    def finish(self, gi, nxt):
        names = REDUCE_GROUPS[gi]
        slots, rows = self.pending[gi]
        halves = []
        for n, s in zip(names, slots):
            halves.append(sum_slots("sum_" + n, s))
        others = core_swap("swap_reduced_%d" % gi, halves)
        lo = [jnp.where(self.c == 0, mine, other) for mine, other in zip(halves, others)]
        hi = [jnp.where(self.c == 0, other, mine) for mine, other in zip(halves, others)]
        results = [self.update(n, _join_halves(l, h, r)) for n, l, h, r in zip(names, lo, hi, rows)]
        if nxt is not None:
            results, nxt = lax.optimization_barrier((results, nxt))
        self.updated.update(zip(names, results))
        return nxt


def _step(a):
    x, mem, tgt = a["x"][0], a["mem"][0], a["loss_target"][0]
    q = 2 * lax.axis_index("x") + lax.axis_index("y")

    shard2d = {n: _shard_view(n, a[n]) for n in BIG}
    small_sh = {n: _as2d(a[n][0]) for n in SMALL_SHARDED}
    c = lax.axis_index("c")
    full = {}
    big = _GatheredWeights(shard2d, q, c)
    gathered = gather_shards("gather_small", [small_sh[n] for n in SMALL_SHARDED])
    for n, g in zip(SMALL_SHARDED, gathered):
        full[n] = _from_slots(g, 1)

    wt = {n: (a[n] if a[n].ndim <= 2 else a[n][0]) for n in WEIGHTS if n not in BIG and n not in SMALL_SHARDED}
    for n in SMALL_SHARDED:
        wt[n] = small_sh[n]
    shards = dict(shard2d)
    shards.update({n: small_sh[n] for n in REDUCED if n not in BIG})

    def update(n, gsum):
        return adamw("adamw_" + n, shards[n], _shard_view(n, a["m_" + n]), _shard_view(n, a["v_" + n]), gsum)

    reducer = _GradReducer(q, c, update)
    loss_tile, grad_x, grads = _local_grads(x, mem, tgt, wt, full, big, reducer)
    reducer.finish(len(REDUCE_GROUPS) - 1, None)
    out = {}
    for n, vals in reducer.updated.items():
        for key, val in zip(("grad_", "delta_", "new_m_", "new_v_"), vals):
            out[key + n] = (val.T if n in TRANSPOSED else val).reshape(a[n].shape)

    small = [n for n in WEIGHTS if n not in REDUCED]
    red = _unpack(all_reduce_small("all_reduce_small", _pack([grads[n] for n in small])), [grads[n].shape for n in small])
    g_loc = {}
    for n, g in zip(small, red):
        if n in SMALL_SHARDED:
            cols = g.shape[1] // 4
            g = lax.dynamic_slice_in_dim(g, q * cols, cols, axis=1)
        g_loc[n] = g.reshape(a[n].shape)
    res = adamw("adamw_small", *[_pack([src[n] for n in small]) for src in
                                 ({n: a[n] for n in small}, {n: a["m_" + n] for n in small}, {n: a["v_" + n] for n in small})],
                _pack([g_loc[n] for n in small]))
    shapes = [a[n].shape for n in small]
    for key, packed in zip(("grad_", "delta_", "new_m_", "new_v_"), res):
        for n, val in zip(small, _unpack(packed, shapes)):
            out[key + n] = val

    loss = lax.psum(loss_tile[0, 0], ("x", "y", "c"))
    ordered = [loss, grad_x.reshape(a["x"].shape)]
    for key in ("grad_", "delta_", "new_m_", "new_v_"):
        ordered += [out[key + n] for n in WEIGHTS]
    return tuple(ordered)


def kernel(x, mem, norm_mix_g, w_in, ssd_conv_w, ssd_conv_b, ssd_dt_bias, ssd_a_log, ssd_d, ssd_norm_g, rwkv_mu, rwkv_w0, rwkv_w2, rwkv_a0, rwkv_a2, rwkv_g2, rwkv_k_k, rwkv_k_a, rwkv_r_k, rwkv_ln_w, rwkv_ln_b, w_out, norm_x_g, norm_mem_g, xattn_wq, xattn_wk, xattn_wv, xattn_wo, norm_ffn_g, ffn_w1, ffn_w2, final_norm_g, loss_target, m_norm_mix_g, m_w_in, m_ssd_conv_w, m_ssd_conv_b, m_ssd_dt_bias, m_ssd_a_log, m_ssd_d, m_ssd_norm_g, m_rwkv_mu, m_rwkv_w0, m_rwkv_w2, m_rwkv_a0, m_rwkv_a2, m_rwkv_g2, m_rwkv_k_k, m_rwkv_k_a, m_rwkv_r_k, m_rwkv_ln_w, m_rwkv_ln_b, m_w_out, m_norm_x_g, m_norm_mem_g, m_xattn_wq, m_xattn_wk, m_xattn_wv, m_xattn_wo, m_norm_ffn_g, m_ffn_w1, m_ffn_w2, m_final_norm_g, v_norm_mix_g, v_w_in, v_ssd_conv_w, v_ssd_conv_b, v_ssd_dt_bias, v_ssd_a_log, v_ssd_d, v_ssd_norm_g, v_rwkv_mu, v_rwkv_w0, v_rwkv_w2, v_rwkv_a0, v_rwkv_a2, v_rwkv_g2, v_rwkv_k_k, v_rwkv_k_a, v_rwkv_r_k, v_rwkv_ln_w, v_rwkv_ln_b, v_w_out, v_norm_x_g, v_norm_mem_g, v_xattn_wq, v_xattn_wk, v_xattn_wv, v_xattn_wo, v_norm_ffn_g, v_ffn_w1, v_ffn_w2, v_final_norm_g):
    return _step(dict(locals()))
```

```python
import functools
import math

import jax
import jax.numpy as jnp
from jax import lax
from jax.experimental import pallas as pl
from jax.experimental.pallas import tpu as pltpu
from jax.experimental.pallas import tpu_sc as plsc

F32 = jnp.float32
BF16 = jnp.bfloat16
HIGHEST = lax.Precision.HIGHEST
MESH_ID = pl.DeviceIdType.MESH

NORM_EPS = 1e-6
RWKV_LN_EPS = 64e-5
HEAD_DIM = 64
PAIR = 2 * HEAD_DIM
LANES = 128
SSD_STATE = 128
SSD_CHUNK = 128
SSD_GROUPS = 2
SSD_CONV = 4
RWKV_CHUNK = 64
HALO = 8
ROW_TILE = 128
PAIRS_PER_STEP = 4
XATTN_HEADS = 4
RWKV_PASSES = 1
VMEM_LIMIT = 56 * 1024 * 1024
MATMUL_VMEM = 40 * 1024 * 1024

ADAM_LR = 0.001
ADAM_B1 = 0.9
ADAM_B2 = 0.999
ADAM_EPS = 1e-08
ADAM_WD = 0.01
ADAM_STEP = 10


def _dims(ca, cb):
    return (((ca,), (cb,)), ((), ()))


def _split_bf16(a):
    hi = a.astype(BF16)
    lo = (a - hi.astype(F32)).astype(BF16)
    return hi, lo


def _mm_impl(a, b, ca, cb, passes):
    dn = _dims(ca, cb)
    if passes == 1:
        return lax.dot_general(a.astype(BF16), b.astype(BF16), dn, preferred_element_type=F32)
    ah, al = _split_bf16(a)
    bh, bl = _split_bf16(b)
    out = lax.dot_general(ah, bh, dn, preferred_element_type=F32)
    out = out + lax.dot_general(ah, bl, dn, preferred_element_type=F32)
    return out + lax.dot_general(al, bh, dn, preferred_element_type=F32)


@functools.partial(jax.custom_vjp, nondiff_argnums=(2, 3, 4))
def mm(a, b, ca, cb, passes):
    return _mm_impl(a, b, ca, cb, passes)


def _mm_fwd(a, b, ca, cb, passes):
    return _mm_impl(a, b, ca, cb, passes), (a, b)


def _mm_bwd(ca, cb, passes, res, g):
    a, b = res
    da = mm(g, b, 1, 1 - cb, passes) if ca == 1 else mm(b, g, 1 - cb, 1, passes)
    db = mm(a, g, 1 - ca, 0, passes) if cb == 0 else mm(g, a, 0, 1 - ca, passes)
    return da, db


mm.defvjp(_mm_fwd, _mm_bwd)


def _dot_exact(a, b):
    return lax.dot_general(a, b, _dims(1, 0), precision=HIGHEST, preferred_element_type=F32)


def _iota(shape, dim):
    return lax.broadcasted_iota(jnp.int32, shape, dim)


def _sigmoid(x):
    return 1.0 / (1.0 + jnp.exp(-x))


def _silu(x):
    return x * _sigmoid(x)


def _softplus(x):
    return jnp.maximum(x, 0.0) + jnp.log(1.0 + jnp.exp(-jnp.abs(x)))


def _rms(x, g):
    return x * lax.rsqrt(jnp.mean(x * x, axis=-1, keepdims=True) + NORM_EPS) * g


def _select_mm(x, sel):
    hi = x.astype(BF16)
    r1 = x - hi.astype(F32)
    mid = r1.astype(BF16)
    lo = (r1 - mid.astype(F32)).astype(BF16)
    dn = _dims(1, 0)
    out = lax.dot_general(hi, sel, dn, preferred_element_type=F32)
    out = out + lax.dot_general(mid, sel, dn, preferred_element_type=F32)
    return out + lax.dot_general(lo, sel, dn, preferred_element_type=F32)


def _head_sum_impl(x, n):
    sel = (_iota((n, LANES), 0) // HEAD_DIM == _iota((n, LANES), 1)).astype(BF16)
    return _select_mm(x, sel)


def _head_expand_impl(s, n):
    sel = (_iota((LANES, n), 1) // HEAD_DIM == _iota((LANES, n), 0)).astype(BF16)
    return _select_mm(s, sel)


@functools.partial(jax.custom_vjp, nondiff_argnums=(1,))
def _head_sum_n(x, n):
    return _head_sum_impl(x, n)


@functools.partial(jax.custom_vjp, nondiff_argnums=(1,))
def _head_expand(s, n):
    return _head_expand_impl(s, n)


_head_sum_n.defvjp(lambda x, n: (_head_sum_impl(x, n), None), lambda n, _, g: (_head_expand(g, n),))
_head_expand.defvjp(lambda s, n: (_head_expand_impl(s, n), None), lambda n, _, g: (_head_sum_n(g, n),))


def _head_sum(x):
    return _head_sum_n(x, x.shape[1])


def _row_vector_expand(v, n):
    v8 = jnp.broadcast_to(v, (8, LANES))
    return jnp.sum(_head_expand(v8, n), axis=0, keepdims=True) * 0.125


def _shift_rows_impl(u, halo, s):
    rolled = pltpu.roll(u, s, 0)
    top = jnp.where(_iota((HALO, 1), 0) < s, pltpu.roll(halo, s, 0), rolled[:HALO])
    return jnp.concatenate([top, rolled[HALO:]], axis=0)


@functools.partial(jax.custom_vjp, nondiff_argnums=(2,))
def _shift_rows(u, halo, s):
    return _shift_rows_impl(u, halo, s)


def _shift_rows_bwd(s, _, g):
    tr = g.shape[0]
    rolled = pltpu.roll(g, tr - s, 0)
    hrow = _iota((HALO, 1), 0)
    bottom = jnp.where(hrow < HALO - s, rolled[tr - HALO:], 0.0)
    dhalo = jnp.where(hrow >= HALO - s, pltpu.roll(g[:HALO], HALO - s, 0), 0.0)
    return jnp.concatenate([rolled[:tr - HALO], bottom], axis=0), dhalo


_shift_rows.defvjp(lambda u, halo, s: (_shift_rows_impl(u, halo, s), None), _shift_rows_bwd)


def _params(sem):
    return pltpu.CompilerParams(dimension_semantics=sem, vmem_limit_bytes=VMEM_LIMIT)


def row_call(name, body, n_tiles, tiled, full, out_tiled, out_acc, transposed=()):
    nt, nf, na = len(tiled), len(full), len(out_acc)
    n_plain = len(out_tiled)
    no = n_plain + len(transposed)

    def kern(*refs):
        tv = [r[...] for r in refs[:nt]]
        fv = [r[...] for r in refs[nt:nt + nf]]
        outs, accs = body(tv, fv)
        for r, v in zip(refs[nt + nf:nt + nf + n_plain], outs):
            r[...] = v.astype(r.dtype)
        for r, idx in zip(refs[nt + nf + n_plain:nt + nf + no], transposed):
            r[...] = outs[idx].astype(F32).T.astype(r.dtype)
        if na:
            a_refs = refs[nt + nf + no:]
            first = pl.program_id(0) == 0

            @pl.when(first)
            def _():
                for r, v in zip(a_refs, accs):
                    r[...] = v

            @pl.when(jnp.logical_not(first))
            def _():
                for r, v in zip(a_refs, accs):
                    r[...] += v

    in_specs = [pl.BlockSpec((rt, w), functools.partial(lambda i, cb: (i, cb), cb=cb)) for (_, rt, w, cb) in tiled]
    in_specs += [pl.BlockSpec(a.shape, lambda i: (0, 0)) for a in full]
    out_specs = [pl.BlockSpec((rt, w), lambda i: (i, 0)) for (_, rt, w, _) in out_tiled]
    out_specs += [pl.BlockSpec((out_tiled[idx][2], out_tiled[idx][1]), lambda i: (0, i)) for idx in transposed]
    out_specs += [pl.BlockSpec(s, lambda i: (0, 0)) for s in out_acc]
    out_shape = [jax.ShapeDtypeStruct((rows, w), dt) for (rows, _, w, dt) in out_tiled]
    out_shape += [jax.ShapeDtypeStruct((out_tiled[idx][2], out_tiled[idx][0]), BF16) for idx in transposed]
    out_shape += [jax.ShapeDtypeStruct(s, F32) for s in out_acc]
    res = pl.pallas_call(
        kern, name=name, grid=(n_tiles,), in_specs=in_specs, out_specs=out_specs, out_shape=out_shape,
        compiler_params=_params(("arbitrary",)),
    )(*[t[0] for t in tiled], *full)
    return list(res[:no]), list(res[no:])


def _pick(dim, cands):
    for c in cands:
        if dim % c == 0:
            return c
    return dim


def matmul(name, a, b, tb=False, resid=None, out_dtype=F32, out_slots=1):
    m, k = a.shape
    b_slots = b.shape[0] if b.ndim == 3 else 1
    n = b.shape[-2] if tb else b.shape[-1] * b_slots
    has_resid = resid is not None
    out_bytes = jnp.dtype(out_dtype).itemsize
    sizes = (2048, 1024, 896, 768, 512, 384, 256, 128)
    tm = _pick(m, sizes[1:])
    tn = _pick(n // max(out_slots, 1 if tb else b_slots), sizes[1:])

    def vmem_bytes(tk):
        return 2 * 2 * tk * (tm + tn) + tm * tn * (2 * out_bytes + 4 + (8 if has_resid else 0))

    k_slot = k // b_slots if tb else k
    tk = next((c for c in sizes if k_slot % c == 0 and vmem_bytes(c) <= MATMUL_VMEM), LANES)
    nk = k // tk
    n_per = n // (b_slots if not tb else 1) // tn
    k_per = k_slot // tk
    o_per = n // out_slots // tn

    def kern(*refs):
        a_ref, b_ref = refs[0], refs[1]
        o_ref, acc = refs[-2], refs[-1]
        kk = pl.program_id(2)
        part = lax.dot_general(a_ref[...], b_ref[...], _dims(1, 1 if tb else 0), preferred_element_type=F32)

        def finish(out):
            if has_resid:
                out = out + refs[2][...]
            o_ref[...] = out.astype(o_ref.dtype)

        if nk == 1:
            finish(part)
            return

        @pl.when(kk == 0)
        def _():
            acc[...] = part

        @pl.when(jnp.logical_and(kk > 0, kk < nk - 1))
        def _():
            acc[...] += part

        @pl.when(kk == nk - 1)
        def _():
            finish(acc[...] + part)

    in_specs = [pl.BlockSpec((tm, tk), lambda i, j, kk: (i, kk))]
    if b.ndim == 3 and tb:
        in_specs.append(pl.BlockSpec((None, tn, tk), lambda i, j, kk: (kk // k_per, j, kk % k_per)))
    elif b.ndim == 3:
        in_specs.append(pl.BlockSpec((None, tk, tn), lambda i, j, kk: (j // n_per, kk, j % n_per)))
    elif tb:
        in_specs.append(pl.BlockSpec((tn, tk), lambda i, j, kk: (j, kk)))
    else:
        in_specs.append(pl.BlockSpec((tk, tn), lambda i, j, kk: (kk, j)))
    args = [a, b]
    if has_resid:
        in_specs.append(pl.BlockSpec((tm, tn), lambda i, j, kk: (i, j)))
        args.append(resid)
    if out_slots > 1:
        out_spec = pl.BlockSpec((None, tm, tn), lambda i, j, kk: (j // o_per, i, j % o_per))
        out_shape = jax.ShapeDtypeStruct((out_slots, m, n // out_slots), out_dtype)
    else:
        out_spec = pl.BlockSpec((tm, tn), lambda i, j, kk: (i, j))
        out_shape = jax.ShapeDtypeStruct((m, n), out_dtype)
    return pl.pallas_call(
        kern, name=name, grid=(m // tm, n // tn, nk), in_specs=in_specs,
        out_specs=out_spec, out_shape=out_shape,
        scratch_shapes=[pltpu.VMEM((tm, tn), F32)],
        compiler_params=_params(("parallel", "parallel", "arbitrary")),
    )(*args)


def norm_fwd(name, x, g, tr, with_transpose=True):
    def body(tv, fv):
        return [_rms(tv[0], fv[0])], []
    rows, d = x.shape
    outs, _ = row_call(name, body, rows // tr, [(x, tr, d, 0)], [g], [(rows, tr, d, BF16)], [],
                       transposed=(0,) if with_transpose else ())
    return outs[0], (outs[1] if with_transpose else None)


def norm_bwd(name, x, g, dh, extra, tr):
    def body(tv, fv):
        _, vjp = jax.vjp(_rms, tv[0], fv[0])
        dx, dg = vjp(tv[1])
        if extra is not None:
            dx = dx + tv[2]
        return [dx, dx], [dg]
    rows, d = x.shape
    tiled = [(x, tr, d, 0), (dh, tr, d, 0)] + ([(extra, tr, d, 0)] if extra is not None else [])
    (dx, dxb), (dg,) = row_call(name, body, rows // tr, tiled, [g], [(rows, tr, d, F32), (rows, tr, d, BF16)], [g.shape])
    return dx, dxb, dg


def _ssd_pre(xbc, halo, dtraw, w0, w1, w2, w3, cb, dtb):
    y = w3 * xbc + w2 * _shift_rows(xbc, halo, 1) + w1 * _shift_rows(xbc, halo, 2) + w0 * _shift_rows(xbc, halo, 3) + cb
    return _silu(y), _softplus(dtraw + dtb)


def _ssd_post(ys, xs, z, dskip, ng):
    w = ys.shape[1]
    y = (ys + xs * _row_vector_expand(dskip, w)) * _silu(z)
    gw = w // SSD_GROUPS
    parts = []
    for gi in range(SSD_GROUPS):
        yg = y[:, gi * gw:(gi + 1) * gw]
        parts.append(yg * lax.rsqrt(jnp.mean(yg * yg, axis=-1, keepdims=True) + NORM_EPS))
    return jnp.concatenate(parts, axis=1) * ng


def _rwkv_pre(urkv, ulora, hrkv, hlora, mu_rkv, mu_lora, w0, a0, kkw, kaw, w2p, a2p, g2):
    w = w0.shape[1]
    urkv = urkv + (_shift_rows(urkv, hrkv, 1) - urkv) * mu_rkv
    ulora = ulora + (_shift_rows(ulora, hlora, 1) - ulora) * mu_lora
    r, k, v = urkv[:, :w], urkv[:, w:2 * w], urkv[:, 2 * w:]
    pw, pa, pg = ulora[:, :LANES], ulora[:, LANES:2 * LANES], ulora[:, 2 * LANES:]
    w_log = -_softplus(-(w0 + mm(jnp.tanh(pw), w2p, 1, 0, 1))) - 0.5
    lw = -jnp.exp(w_log)
    iclr = _sigmoid(a0 + mm(pa, a2p, 1, 0, 1))
    gate = mm(_sigmoid(pg), g2, 1, 0, 1)
    kk = k * kkw
    kk = kk / jnp.maximum(jnp.sqrt(_head_expand(_head_sum(kk * kk), w)), 1e-12)
    k2 = k * (1.0 + (iclr - 1.0) * kaw)
    return r, lw, k2, v, -kk, kk * iclr, gate


def _rwkv_post(ys, r, k2, v, gate, rk, lnw, lnb):
    w = ys.shape[1]
    inv = 1.0 / HEAD_DIM
    mean = _head_expand(_head_sum(ys), w) * inv
    d = ys - mean
    var = _head_expand(_head_sum(d * d), w) * inv
    yn = d * lax.rsqrt(var + RWKV_LN_EPS) * lnw + lnb
    bonus = _head_expand(_head_sum(r * k2 * rk), w) * v
    return (yn + bonus) * gate


def _attn(q, k, v):
    d = q.shape[1]
    hd = d // XATTN_HEADS
    outs = []
    for h in range(XATTN_HEADS):
        sl = slice(h * hd, (h + 1) * hd)
        s = mm(q[:, sl], k[:, sl], 1, 1, 1) * (hd ** -0.5)
        s = s - jnp.max(s, axis=-1, keepdims=True)
        p = jnp.exp(s)
        p = p / jnp.sum(p, axis=-1, keepdims=True)
        outs.append(mm(p, v[:, sl], 1, 0, 1))
    return jnp.concatenate(outs, axis=1)


def _relu2(a):
    return jnp.square(jnp.maximum(a.astype(F32), 0.0))


def fn_fwd(name, fn, n_tiles, tiled, full, out_tiled, transposed=()):
    def body(tv, fv):
        outs = fn(*tv, *fv)
        return (list(outs) if isinstance(outs, (tuple, list)) else [outs]), []
    outs, _ = row_call(name, body, n_tiles, tiled, full, out_tiled, [], transposed)
    return outs


def fn_bwd(name, fn, n_tiles, tiled, full, cts, ct_fn, out_tiled):
    nt = len(tiled)

    def body(tv, fv):
        outs, vjp = jax.vjp(fn, *tv[:nt], *fv)
        ct = ct_fn(tv[nt:])
        grads = vjp(tuple(ct) if isinstance(outs, (tuple, list)) else ct[0])
        return list(grads[:nt]), list(grads[nt:])
    return row_call(name, body, n_tiles, tiled + cts, full, out_tiled, [f.shape for f in full])


def _ssd_chunk(xs, bm, cm, dt_all, a_log, ht, p):
    q = xs.shape[0]
    lane = _iota((1, LANES), 1)
    row = _iota((q, 1), 0)
    tril = _iota((q, q), 0) >= _iota((q, q), 1)
    half = lane < HEAD_DIM
    da = dt_all * (-jnp.exp(a_log))
    cs = _dot_exact(tril.astype(F32), da)

    def col(mat, h):
        return jnp.sum(jnp.where(lane == h, mat, 0.0), axis=1, keepdims=True)

    cs0, cs1 = col(cs, 2 * p), col(cs, 2 * p + 1)
    xdt = xs * jnp.where(half, col(dt_all, 2 * p), col(dt_all, 2 * p + 1))
    csx = jnp.where(half, cs0, cs1)
    last = jnp.sum(jnp.where(row == q - 1, csx, 0.0), axis=0, keepdims=True)
    cb = mm(cm, bm, 1, 1, 1)
    y = mm(cm, ht, 1, 0, 1) * jnp.exp(csx)
    for csh, hm in ((cs0, half), (cs1, jnp.logical_not(half))):
        csl = jnp.broadcast_to(csh, (q, q))
        seg = csl - csl.T
        lmat = jnp.where(tril, jnp.exp(jnp.where(tril, seg, 0.0)), 0.0)
        y = y + jnp.where(hm, mm(cb * lmat, xdt, 1, 0, 1), 0.0)
    st = mm(bm, xdt * jnp.exp(last - csx), 0, 0, 1)
    return y, ht * jnp.exp(last) + st


def _rwkv_chunks(pairs):
    c = pairs[0][0].shape[0]
    ps = RWKV_PASSES
    lane = _iota((1, LANES), 1)
    row = _iota((c, 1), 0)
    ri, ci = _iota((c, c), 0), _iota((c, c), 1)
    tril_i, tril_s = ri >= ci, ri > ci
    eye = (ri == ci).astype(F32)
    half = lane < HEAD_DIM
    halves = (half, jnp.logical_not(half))
    bd = (_iota((LANES, LANES), 0) < HEAD_DIM) == (_iota((LANES, LANES), 1) < HEAD_DIM)
    tri = tril_i.astype(F32)
    n = len(pairs)
    heads = [(j, hm) for j in range(n) for hm in halves]

    cum = [_dot_exact(tri, p[1]) for p in pairs]
    at = [p[4] * jnp.exp(cm - p[1]) for p, cm in zip(pairs, cum)]
    en = [jnp.exp(-cm) for cm in cum]
    bt = [p[5] * e for p, e in zip(pairs, en)]
    kt = [p[2] * e for p, e in zip(pairs, en)]
    rt = [p[0] * jnp.exp(cm) for p, cm in zip(pairs, cum)]
    ah = [mm(at[j], pairs[j][6], 1, 1, ps) for j in range(n)]
    y = [mm(rt[j], pairs[j][6], 1, 1, ps) for j in range(n)]
    atm = [jnp.where(hm, at[j], 0.0) for j, hm in heads]
    rtm = [jnp.where(hm, rt[j], 0.0) for j, hm in heads]
    aab = [jnp.where(tril_s, mm(atm[i], bt[j], 1, 1, ps), 0.0) for i, (j, _) in enumerate(heads)]
    aak = [jnp.where(tril_s, mm(atm[i], kt[j], 1, 1, ps), 0.0) for i, (j, _) in enumerate(heads)]
    arb = [jnp.where(tril_i, mm(rtm[i], bt[j], 1, 1, ps), 0.0) for i, (j, _) in enumerate(heads)]
    ark = [jnp.where(tril_i, mm(rtm[i], kt[j], 1, 1, ps), 0.0) for i, (j, _) in enumerate(heads)]
    rhs = [ah[j] + mm(aak[i], pairs[j][3], 1, 0, ps) for i, (j, _) in enumerate(heads)]
    yv = [mm(ark[i], pairs[j][3], 1, 0, ps) for i, (j, _) in enumerate(heads)]
    tm = [eye + a_ for a_ in aab]
    pm = aab
    for _ in range(int(math.log2(c)) - 1):
        pm = [mm(p_, p_, 1, 0, ps) for p_ in pm]
        tm = [t_ + mm(t_, p_, 1, 0, ps) for t_, p_ in zip(tm, pm)]
    uh = [mm(tm[i], rhs[i], 1, 0, ps) for i in range(len(heads))]
    u = [jnp.where(half, uh[2 * j], uh[2 * j + 1]) for j in range(n)]
    yu = [mm(arb[i], u[j], 1, 0, ps) for i, (j, _) in enumerate(heads)]
    out = []
    for j in range(n):
        yj = y[j] + jnp.where(half, yu[2 * j] + yv[2 * j], yu[2 * j + 1] + yv[2 * j + 1])
        plast = jnp.sum(jnp.where(row == c - 1, cum[j], 0.0), axis=0, keepdims=True)
        upd = pairs[j][6] + mm(u[j], bt[j], 0, 0, ps) + mm(pairs[j][3], kt[j], 0, 0, ps)
        out.append((yj, jnp.where(bd, upd * jnp.exp(plast), 0.0)))
    return out


def _seq_spec(chunk, ppb, col, row_of):
    if col is None:
        return pl.BlockSpec((chunk, ppb * LANES), lambda pb, i: (row_of(i), pb))
    return pl.BlockSpec((chunk, LANES), lambda pb, i: (row_of(i), col(pb * ppb)))


def _pair_vals(refs, seq_in, j):
    return [r[...] if col is not None else r[:, j * LANES:(j + 1) * LANES] for r, (_, col) in zip(refs, seq_in)]


def scan_fwd(name, chunk_fn, chunk, seq_in, const_in, n_pairs, ppb):
    t = seq_in[0][0].shape[0]
    nc = t // chunk
    ns, ncst = len(seq_in), len(const_in)

    def kern(*refs):
        y_ref, st_ref, ht = refs[ns + ncst], refs[ns + ncst + 1], refs[ns + ncst + 2]

        @pl.when(pl.program_id(1) == 0)
        def _():
            ht[...] = jnp.zeros_like(ht)

        cv = [r[...] for r in refs[ns:ns + ncst]]
        h0 = [ht[j] for j in range(ppb)]
        for j in range(ppb):
            st_ref[j] = h0[j]
        sv = [_pair_vals(refs[:ns], seq_in, j) for j in range(ppb)]
        outs = chunk_fn(sv, cv, h0, [pl.program_id(0) * ppb + j for j in range(ppb)])
        for j, (y, hn) in enumerate(outs):
            y_ref[:, j * LANES:(j + 1) * LANES] = y
            ht[j] = hn

    in_specs = [_seq_spec(chunk, ppb, col, lambda i: i) for (_, col) in seq_in]
    in_specs += [pl.BlockSpec(a.shape, lambda pb, i: (0, 0)) for a in const_in]
    return pl.pallas_call(
        kern, name=name, grid=(n_pairs // ppb, nc), in_specs=in_specs,
        out_specs=[pl.BlockSpec((chunk, ppb * LANES), lambda pb, i: (i, pb)),
                   pl.BlockSpec((ppb, None, LANES, LANES), lambda pb, i: (pb, i, 0, 0))],
        out_shape=[jax.ShapeDtypeStruct((t, n_pairs * LANES), F32), jax.ShapeDtypeStruct((n_pairs, nc, LANES, LANES), F32)],
        scratch_shapes=[pltpu.VMEM((ppb, LANES, LANES), F32)],
        compiler_params=_params(("arbitrary", "arbitrary")),
    )(*[s[0] for s in seq_in], *const_in)


def scan_bwd(name, chunk_fn, chunk, seq_in, const_in, states, dy, n_pairs, ppb):
    t = dy.shape[0]
    nc = t // chunk
    ns, ncst = len(seq_in), len(const_in)

    def kern(*refs):
        seq_refs, cst_refs = refs[:ns], refs[ns:ns + ncst]
        st_ref, dy_ref = refs[ns + ncst], refs[ns + ncst + 1]
        o = ns + ncst + 2
        dseq_refs, dcst_refs, dht = refs[o:o + ns], refs[o + ns:o + ns + ncst], refs[o + ns + ncst]
        pb, i = pl.program_id(0), pl.program_id(1)

        @pl.when(i == 0)
        def _():
            dht[...] = jnp.zeros_like(dht)

        ids = [pb * ppb + j for j in range(ppb)]
        lanes = [slice(j * LANES, (j + 1) * LANES) for j in range(ppb)]

        def fn(*flat):
            sv = [list(flat[j * ns:(j + 1) * ns]) for j in range(ppb)]
            outs = chunk_fn(sv, list(flat[ppb * ns:ppb * ns + ncst]), list(flat[ppb * ns + ncst:]), ids)
            return tuple(y for y, _ in outs), tuple(h for _, h in outs)

        flat_in = [v for j in range(ppb) for v in _pair_vals(seq_refs, seq_in, j)]
        flat_in += [r[...] for r in cst_refs] + [st_ref[j] for j in range(ppb)]
        _, vjp = jax.vjp(fn, *flat_in)
        grads = vjp((tuple(dy_ref[:, ln] for ln in lanes), tuple(dht[j] for j in range(ppb))))
        for j in range(ppb):
            for r, g in zip(dseq_refs, grads[j * ns:(j + 1) * ns]):
                r[:, lanes[j]] = g
            dht[j] = grads[ppb * ns + ncst + j]
        dcv = grads[ppb * ns:ppb * ns + ncst]
        if ncst:
            first = jnp.logical_and(pb == 0, i == 0)

            @pl.when(first)
            def _():
                for r, g in zip(dcst_refs, dcv):
                    r[...] = g

            @pl.when(jnp.logical_not(first))
            def _():
                for r, g in zip(dcst_refs, dcv):
                    r[...] += g

    rev = lambda i: nc - 1 - i
    wide = pl.BlockSpec((chunk, ppb * LANES), lambda pb, i: (rev(i), pb))
    in_specs = [_seq_spec(chunk, ppb, col, rev) for (_, col) in seq_in]
    in_specs += [pl.BlockSpec(a.shape, lambda pb, i: (0, 0)) for a in const_in]
    in_specs += [pl.BlockSpec((ppb, None, LANES, LANES), lambda pb, i: (pb, rev(i), 0, 0)), wide]
    out_specs = [wide for _ in seq_in]
    out_specs += [pl.BlockSpec(a.shape, lambda pb, i: (0, 0)) for a in const_in]
    out_shape = [jax.ShapeDtypeStruct((t, n_pairs * LANES), F32) for _ in seq_in]
    out_shape += [jax.ShapeDtypeStruct(a.shape, F32) for a in const_in]
    res = pl.pallas_call(
        kern, name=name, grid=(n_pairs // ppb, nc), in_specs=in_specs, out_specs=out_specs, out_shape=out_shape,
        scratch_shapes=[pltpu.VMEM((ppb, LANES, LANES), F32)],
        compiler_params=_params(("arbitrary", "arbitrary")),
    )(*[s[0] for s in seq_in], *const_in, states, dy)
    return list(res[:ns]), list(res[ns:])


def loss_head(x3, tgt, g, tr):
    rows, d = x3.shape

    def body(tv, fv):
        def f(x, gg):
            e = jnp.square(_rms(x, gg) - tv[1])
            return 0.5 * jnp.sum(jnp.mean(e, axis=-1, keepdims=True), axis=0, keepdims=True)
        l, vjp = jax.vjp(f, tv[0], fv[0])
        dx, dg = vjp(jnp.ones((1, 1), F32))
        return [dx, dx], [dg, jnp.broadcast_to(l, (8, LANES))]
    (dx, dxb), (dg, l) = row_call("loss_head", body, rows // tr, [(x3, tr, d, 0), (tgt, tr, d, 0)], [g],
                                  [(rows, tr, d, F32), (rows, tr, d, BF16)], [g.shape, (8, LANES)])
    return dx, dxb, dg, l


def _adam_math(w, g, m, v):
    m = ADAM_B1 * m + (1.0 - ADAM_B1) * g
    v = ADAM_B2 * v + (1.0 - ADAM_B2) * jnp.square(g)
    m_hat = m / (1.0 - ADAM_B1 ** ADAM_STEP)
    v_hat = v / (1.0 - ADAM_B2 ** ADAM_STEP)
    delta = -ADAM_LR * (m_hat / (jnp.sqrt(v_hat) + ADAM_EPS) + ADAM_WD * w)
    return delta, m, v


def _tiling(rows, cols, limit):
    row_tile = max([d for d in range(16, rows + 1, 16) if rows % d == 0 and d * cols <= limit], default=0)
    col_tile = max([ct for ct in range(LANES, cols + 1, LANES) if cols % ct == 0 and rows * ct <= limit], default=0)
    if row_tile and row_tile * cols >= rows * col_tile:
        return row_tile, cols
    return (rows, col_tile) if col_tile else (rows, cols)


def ew_call(name, fn, ins, out_dtypes, limit=1 << 20):
    rows, cols = ins[0].shape
    br, bc = _tiling(rows, cols, limit)
    spec = pl.BlockSpec((br, bc), lambda i, j: (i, j))
    n_in = len(ins)

    def kern(*refs):
        for r, v in zip(refs[n_in:], fn(*[r[...] for r in refs[:n_in]])):
            r[...] = v.astype(r.dtype)

    return pl.pallas_call(
        kern, name=name, grid=(rows // br, cols // bc), in_specs=[spec] * n_in, out_specs=[spec] * len(out_dtypes),
        out_shape=[jax.ShapeDtypeStruct((rows, cols), dt) for dt in out_dtypes],
        compiler_params=_params(("parallel", "parallel")),
    )(*ins)


def adamw(name, w, m, v, g):
    return ew_call(name, lambda wv, mv, vv, gv: (gv, *_adam_math(wv, gv, mv, vv)), [w, m, v, g], [F32] * 4, 1 << 18)


def half_call(name, s, h, extra, out_dtype):
    n_slots, rows, cols = s.shape
    by_cols = _halves_by_cols(rows)
    hr, hc = (rows, cols // 2) if by_cols else (rows // 2, cols)
    br, bc = _tiling(hr, hc, 1 << 20)
    ni, nj = hr // br, hc // bc
    if by_cols:
        s_spec = pl.BlockSpec((None, br, bc), lambda sl, i, j, href: (sl, i, href[0] * nj + j))
    else:
        s_spec = pl.BlockSpec((None, br, bc), lambda sl, i, j, href: (sl, href[0] * ni + i, j))
    flat = pl.BlockSpec((None, br, bc), lambda sl, i, j, href: (sl, i, j))
    has_extra = extra is not None

    def kern(href, s_ref, *rest):
        v = s_ref[...]
        if has_extra:
            v = v + rest[0][...].astype(F32)
        rest[-1][...] = v.astype(out_dtype)

    grid_spec = pltpu.PrefetchScalarGridSpec(
        num_scalar_prefetch=1, grid=(n_slots, ni, nj), in_specs=[s_spec] + ([flat] if has_extra else []), out_specs=flat)
    return pl.pallas_call(
        kern, name=name, grid_spec=grid_spec, out_shape=jax.ShapeDtypeStruct((n_slots, hr, hc), out_dtype),
        compiler_params=_params(("parallel", "parallel", "parallel")),
    )(jnp.reshape(h, (1,)).astype(jnp.int32), s, *([extra] if has_extra else []))


def sum_slots(name, r):
    _, rows, cols = r.shape
    br, bc = _tiling(rows, cols, 1 << 20)

    def kern(r0, r1, r2, r3, o):
        o[...] = ((r0[...].astype(F32) + r1[...].astype(F32)) + r2[...].astype(F32)) + r3[...].astype(F32)

    in_specs = [pl.BlockSpec((None, br, bc), functools.partial(lambda i, j, s: (s, i, j), s=s)) for s in range(4)]
    return pl.pallas_call(
        kern, name=name, grid=(rows // br, cols // bc), in_specs=in_specs,
        out_specs=pl.BlockSpec((br, bc), lambda i, j: (i, j)),
        out_shape=jax.ShapeDtypeStruct((rows, cols), F32), compiler_params=_params(("parallel", "parallel")),
    )(r, r, r, r)


def _my_place():
    return lax.axis_index("x"), lax.axis_index("y"), lax.axis_index("c")


def _chip_peers(x, y):
    peers = [(1 - x, y), (x, 1 - y), (1 - x, 1 - y)]
    return peers, [2 * px + py for px, py in peers]


def gather_shards(name, arrays):
    nw = len(arrays)
    ANY = pl.BlockSpec(memory_space=pl.ANY)

    def body(*refs):
        ins, outs = refs[:nw], refs[nw:2 * nw]
        send, recv, loc = refs[2 * nw:]
        x, y, c = _my_place()
        q = 2 * x + y
        peers, chips = _chip_peers(x, y)

        def remote(w, j, slot):
            return pltpu.make_async_remote_copy(
                src_ref=ins[w], dst_ref=outs[w].at[slot], send_sem=send.at[w, j], recv_sem=recv.at[w, j],
                device_id=(*peers[j], c), device_id_type=MESH_ID)

        local = [pltpu.make_async_copy(ins[w], outs[w].at[q], loc.at[w]) for w in range(nw)]
        sends = [[remote(w, j, q) for j in range(3)] for w in range(nw)]
        for w in range(nw):
            local[w].start()
            for j in range(3):
                sends[w][j].start()
        for w in range(nw):
            local[w].wait()
            for j in range(3):
                sends[w][j].wait_send()
                remote(w, j, chips[j]).wait_recv()

    return pl.pallas_call(
        body, name=name, in_specs=[ANY] * nw, out_specs=[ANY] * nw,
        out_shape=[jax.ShapeDtypeStruct((4,) + a.shape, a.dtype) for a in arrays],
        scratch_shapes=[pltpu.SemaphoreType.DMA((nw, 3)), pltpu.SemaphoreType.DMA((nw, 3)), pltpu.SemaphoreType.DMA((nw,))],
        compiler_params=pltpu.CompilerParams(has_side_effects=True),
    )(*arrays)


def scatter_slots(name, arrays, collective_id):
    nw = len(arrays)

    def body(*refs):
        ins, outs = refs[:nw], refs[nw:2 * nw]
        send, recv, loc = refs[2 * nw:]
        x, y, c = _my_place()
        q = 2 * x + y
        peers, chips = _chip_peers(x, y)
        barrier = pltpu.get_barrier_semaphore()
        for p in peers:
            pl.semaphore_signal(barrier, inc=1, device_id=(*p, c), device_id_type=MESH_ID)
        pl.semaphore_wait(barrier, 3)

        def remote(w, j, src_slot, dst_slot):
            return pltpu.make_async_remote_copy(
                src_ref=ins[w].at[src_slot], dst_ref=outs[w].at[dst_slot], send_sem=send.at[w, j], recv_sem=recv.at[w, j],
                device_id=(*peers[j], c), device_id_type=MESH_ID)

        sends = [[remote(w, j, chips[j], q) for j in range(3)] for w in range(nw)]
        own = [pltpu.make_async_copy(ins[w].at[q], outs[w].at[q], loc.at[w]) for w in range(nw)]
        for w in range(nw):
            for j in range(3):
                sends[w][j].start()
            own[w].start()
        for w in range(nw):
            for j in range(3):
                sends[w][j].wait_send()
                remote(w, j, q, chips[j]).wait_recv()
            own[w].wait()

    return pl.kernel(
        body, out_type=[jax.ShapeDtypeStruct(a.shape, a.dtype) for a in arrays],
        mesh=plsc.ScalarSubcoreMesh(axis_name="sequencer", num_cores=1), name=name,
        scratch_types=[pltpu.SemaphoreType.DMA((nw, 3)), pltpu.SemaphoreType.DMA((nw, 3)), pltpu.SemaphoreType.DMA((nw,))],
        compiler_params=pltpu.CompilerParams(collective_id=collective_id),
    )(*arrays)


def _halves_by_cols(rows):
    return rows % 32 != 0


def _half_of(ref, shape, h):
    rows, cols = shape
    if _halves_by_cols(rows):
        return ref.at[:, pl.ds(h * (cols // 2), cols // 2)]
    return ref.at[pl.ds(h * (rows // 2), rows // 2)]


def _join_halves(lo, hi, rows):
    return jnp.concatenate([lo, hi], axis=lo.ndim - 1 if _halves_by_cols(rows) else lo.ndim - 2)


def gather_two_level(name, arrays, collective_id):
    nw = len(arrays)
    shapes = [a.shape for a in arrays]

    def body(*refs):
        ins, outs = refs[:nw], refs[nw:2 * nw]
        send, recv, loc = refs[2 * nw:]
        x, y, c = _my_place()
        q = 2 * x + y
        me, sibling = (x, y, c), (x, y, 1 - c)
        peers = [(1 - x, y), (x, 1 - y), (1 - x, 1 - y)]
        chips = [2 * px + py for px, py in peers]
        barrier = pltpu.get_barrier_semaphore()
        for dev in [sibling] + [(*p, c) for p in peers]:
            pl.semaphore_signal(barrier, inc=1, device_id=dev, device_id_type=MESH_ID)
        pl.semaphore_wait(barrier, 4)

        def mine(w):
            return _half_of(ins[w], shapes[w], c)

        def landed(w, chip, half):
            return _half_of(outs[w].at[chip], shapes[w], half)

        def copy(w, k, src, chip, half, to):
            return pltpu.make_async_remote_copy(
                src_ref=src, dst_ref=landed(w, chip, half), send_sem=send.at[w, k], recv_sem=recv.at[w, k],
                device_id=to, device_id_type=MESH_ID)

        first = [[copy(w, 0, mine(w), q, c, sibling)] + [copy(w, 1 + j, mine(w), q, c, (*peers[j], c)) for j in range(3)]
                 for w in range(nw)]
        own = [pltpu.make_async_copy(mine(w), landed(w, q, c), loc.at[w]) for w in range(nw)]
        for w in range(nw):
            for cp in first[w]:
                cp.start()
            own[w].start()
        passed = []
        for w in range(nw):
            for j in range(3):
                copy(w, 1 + j, mine(w), chips[j], c, me).wait_recv()
                fwd = copy(w, 4 + j, landed(w, chips[j], c), chips[j], c, sibling)
                fwd.start()
                passed.append(fwd)
        for w in range(nw):
            copy(w, 0, mine(w), q, 1 - c, me).wait_recv()
            for j in range(3):
                copy(w, 4 + j, mine(w), chips[j], 1 - c, me).wait_recv()
        for w in range(nw):
            for cp in first[w]:
                cp.wait_send()
            own[w].wait()
        for cp in passed:
            cp.wait_send()

    out_type = [jax.ShapeDtypeStruct((4,) + a.shape, a.dtype) for a in arrays]
    return pl.kernel(
        body, out_type=out_type, mesh=plsc.ScalarSubcoreMesh(axis_name="sequencer", num_cores=1), name=name,
        scratch_types=[pltpu.SemaphoreType.DMA((nw, 7)), pltpu.SemaphoreType.DMA((nw, 7)), pltpu.SemaphoreType.DMA((nw,))],
        compiler_params=pltpu.CompilerParams(collective_id=collective_id),
    )(*arrays)


def core_swap(name, arrays):
    nw = len(arrays)
    ANY = pl.BlockSpec(memory_space=pl.ANY)

    def body(*refs):
        ins, outs = refs[:nw], refs[nw:2 * nw]
        send, recv = refs[2 * nw:]
        x, y, c = _my_place()
        copies = [pltpu.make_async_remote_copy(
            src_ref=ins[w], dst_ref=outs[w], send_sem=send.at[w], recv_sem=recv.at[w],
            device_id=(x, y, 1 - c), device_id_type=MESH_ID) for w in range(nw)]
        for cp in copies:
            cp.start()
        for cp in copies:
            cp.wait_send()
            cp.wait_recv()

    return pl.pallas_call(
        body, name=name, in_specs=[ANY] * nw, out_specs=[ANY] * nw,
        out_shape=[jax.ShapeDtypeStruct(a.shape, a.dtype) for a in arrays],
        scratch_shapes=[pltpu.SemaphoreType.DMA((nw,)), pltpu.SemaphoreType.DMA((nw,))],
        compiler_params=pltpu.CompilerParams(has_side_effects=True),
    )(*arrays)


def all_reduce_small(name, v):
    rows = v.shape[0]
    VM = pl.BlockSpec(memory_space=pltpu.VMEM)

    def body(v_ref, o_ref, buf, send, recv):
        x, y, c = _my_place()
        me = 4 * x + 2 * y + c

        def peer(kx):
            return (x ^ ((kx >> 2) & 1), y ^ ((kx >> 1) & 1), c ^ (kx & 1))

        def copy(kx, slot):
            return pltpu.make_async_remote_copy(
                src_ref=v_ref, dst_ref=buf.at[slot], send_sem=send.at[kx - 1], recv_sem=recv.at[kx - 1],
                device_id=peer(kx), device_id_type=MESH_ID)

        sends = [copy(kx, me) for kx in range(1, 8)]
        for cp in sends:
            cp.start()
        buf[me] = v_ref[...]
        for kx in range(1, 8):
            copy(kx, me ^ kx).wait_recv()
        for cp in sends:
            cp.wait_send()
        acc = buf[0]
        for d in range(1, 8):
            acc = acc + buf[d]
        o_ref[...] = acc

    return pl.pallas_call(
        body, name=name, in_specs=[VM], out_specs=VM, out_shape=jax.ShapeDtypeStruct(v.shape, F32),
        scratch_shapes=[pltpu.VMEM((8, rows, LANES), F32), pltpu.SemaphoreType.DMA((7,)), pltpu.SemaphoreType.DMA((7,))],
        compiler_params=pltpu.CompilerParams(has_side_effects=True, vmem_limit_bytes=VMEM_LIMIT),
    )(v)


def _pad_cols(a, n):
    return jnp.pad(a, ((0, 0), (0, n - a.shape[1])))


def _pad_rows(a, n):
    return jnp.pad(a, ((0, n - a.shape[0]), (0, 0)))


def _halo(u, tr):
    t, cdim = u.shape
    tails = u.reshape(t // tr, tr, cdim)[:, tr - HALO:, :]
    tails = jnp.concatenate([jnp.zeros((1, HALO, cdim), u.dtype), tails[:-1]], axis=0)
    return tails.reshape(-1, cdim)


def _unhalo(du, dhalo, tr):
    t, cdim = du.shape
    n = t // tr
    dh = dhalo.reshape(n, HALO, cdim)
    dh = jnp.concatenate([dh[1:], jnp.zeros((1, HALO, cdim), du.dtype)], axis=0)
    d3 = du.reshape(n, tr, cdim)
    d3 = jnp.concatenate([d3[:, :tr - HALO, :], d3[:, tr - HALO:, :] + dh], axis=1)
    return d3.reshape(t, cdim)


def _to_slots(g, axis):
    r, cdim = g.shape
    if axis == 0:
        return g.reshape(4, r // 4, cdim)
    return g.reshape(r, 4, cdim // 4).transpose(1, 0, 2)


def _from_slots(s, axis):
    if axis == 0:
        return s.reshape(s.shape[0] * s.shape[1], s.shape[2])
    return s.transpose(1, 0, 2).reshape(s.shape[1], 4 * s.shape[2])


BIG = ("w_in", "w_out", "xattn_wq", "xattn_wk", "xattn_wv", "xattn_wo", "ffn_w1", "ffn_w2")
TRANSPOSED = ("w_in",)
BIG_AXIS = {"w_in": 0, "w_out": 0, "xattn_wq": 0, "xattn_wk": 0, "xattn_wv": 0, "xattn_wo": 0, "ffn_w1": 1, "ffn_w2": 0}
SMALL_SHARDED = ("ssd_conv_w", "rwkv_w2", "rwkv_a2", "rwkv_g2")
GATHER_GROUPS = (("w_in",), ("w_out", "xattn_wq", "xattn_wk", "xattn_wv", "xattn_wo"), ("ffn_w1", "ffn_w2"))
REDUCE_GROUPS = (("ffn_w2", "ffn_w1"), ("xattn_wo", "xattn_wq", "xattn_wk", "xattn_wv", "w_out"),
                 ("rwkv_w2", "rwkv_a2", "rwkv_g2", "w_in"))
REDUCED = BIG + ("rwkv_w2", "rwkv_a2", "rwkv_g2")
REDUCE_AXIS = dict(BIG_AXIS, rwkv_w2=1, rwkv_a2=1, rwkv_g2=1)
WEIGHTS = ("norm_mix_g", "w_in", "ssd_conv_w", "ssd_conv_b", "ssd_dt_bias", "ssd_a_log", "ssd_d", "ssd_norm_g",
           "rwkv_mu", "rwkv_w0", "rwkv_w2", "rwkv_a0", "rwkv_a2", "rwkv_g2", "rwkv_k_k", "rwkv_k_a", "rwkv_r_k",
           "rwkv_ln_w", "rwkv_ln_b", "w_out", "norm_x_g", "norm_mem_g", "xattn_wq", "xattn_wk", "xattn_wv", "xattn_wo",
           "norm_ffn_g", "ffn_w1", "ffn_w2", "final_norm_g")


def _local_grads(x, mem, tgt, wt, full, big, reducer):
    t, d = x.shape
    w = d // 2
    nh = w // HEAD_DIM
    n_pairs = nh // 2
    ppg = n_pairs // SSD_GROUPS
    bc = SSD_GROUPS * SSD_STATE
    conv_dim = w + 2 * bc
    tr = ROW_TILE
    nt = t // tr
    tr2 = 2 * tr if t % (2 * tr) == 0 else tr
    nt2 = t // tr2
    dr = wt["rwkv_w2"].shape[0]
    ar = wt["rwkv_a2"].shape[0]
    gr = wt["rwkv_g2"].shape[0]

    big.start(0, None)
    big.start(1, None)
    h1, _ = norm_fwd("norm_mix", x, wt["norm_mix_g"], tr2, with_transpose=False)
    w_in_t = big.get("w_in", (h1, full))
    o = 0
    segs = {}
    for nm, width in (("z", w), ("xbc", conv_dim), ("dt", nh), ("rkv", 3 * w), ("pw", dr), ("pa", ar), ("pg", gr)):
        segs[nm] = (o, width)
        o += width
    padded = {"z": w, "xbc": conv_dim, "dt": LANES, "rkv": 3 * w, "pw": LANES, "pa": LANES, "pg": gr}
    order = ("z", "xbc", "dt", "rkv", "pw", "pa", "pg")
    w_segs = [jnp.concatenate([_pad_rows(w_in_t[segs[nm][0]:segs[nm][0] + segs[nm][1]], padded[nm]) for nm in grp], axis=0)
              for grp in (("z",), ("xbc",), ("dt",), ("rkv",), ("pw", "pa", "pg"))]
    w_perm_t = jnp.concatenate(w_segs, axis=0)
    offs = {}
    o = 0
    for nm in order:
        offs[nm] = o
        o += padded[nm]
    lora_w = 2 * LANES + gr

    mu = wt["rwkv_mu"]
    mo = 3 * w
    mu_rkv = mu[:, :mo]
    mu_lora = jnp.concatenate([_pad_cols(mu[:, mo:mo + dr], LANES), _pad_cols(mu[:, mo + dr:mo + dr + ar], LANES),
                               mu[:, mo + dr + ar:]], axis=1)
    w2p = _pad_rows(full["rwkv_w2"], LANES)
    a2p = _pad_rows(full["rwkv_a2"], LANES)
    g2 = full["rwkv_g2"]
    conv_w = full["ssd_conv_w"]
    cw = [conv_w[i:i + 1] for i in range(SSD_CONV)]
    dt_bias = _pad_cols(wt["ssd_dt_bias"], LANES)
    a_log = _pad_cols(wt["ssd_a_log"], LANES)
    d_skip = _pad_cols(wt["ssd_d"], LANES)
    r_k = wt["rwkv_r_k"].reshape(1, w)

    z, xbc, dtraw, urkv, ulora = [matmul("in_proj_%d" % i, h1, ws, tb=True) for i, ws in enumerate(w_segs)]
    big.start(2, urkv)

    halo_xbc = _halo(xbc, tr)
    ssd_pre_t = [(xbc, tr, conv_dim, 0), (halo_xbc, HALO, conv_dim, 0), (dtraw, tr, LANES, 0)]
    ssd_pre_f = cw + [wt["ssd_conv_b"], dt_bias]
    act, dt = fn_fwd("ssd_pre", _ssd_pre, nt, ssd_pre_t, ssd_pre_f, [(t, tr, conv_dim, F32), (t, tr, LANES, F32)])

    nb = w // LANES
    ssd_seq = [(act, None), (act, lambda p: nb + p // ppg), (act, lambda p: nb + SSD_GROUPS + p // ppg), (dt, lambda p: 0)]
    ssd_ppb = min(ppg, PAIRS_PER_STEP)
    rw_ppb = min(n_pairs, 2 * PAIRS_PER_STEP)

    def ssd_fn(sv, cv, hts, ids):
        return [_ssd_chunk(*s, cv[0], ht, p) for s, ht, p in zip(sv, hts, ids)]

    y_scan, ssd_states = scan_fwd("ssd_scan", ssd_fn, SSD_CHUNK, ssd_seq, [a_log], n_pairs, ssd_ppb)
    ssd_post_t = [(y_scan, tr, w, 0), (act, tr, w, 0), (z, tr, w, 0)]
    ssd_post_f = [d_skip, wt["ssd_norm_g"]]
    y_ssd, y_ssd_t = fn_fwd("ssd_post", _ssd_post, nt, ssd_post_t, ssd_post_f, [(t, tr, w, BF16)], (0,))

    halo_rkv, halo_lora = _halo(urkv, tr), _halo(ulora, tr)
    rw_pre_t = [(urkv, tr, 3 * w, 0), (ulora, tr, lora_w, 0), (halo_rkv, HALO, 3 * w, 0), (halo_lora, HALO, lora_w, 0)]
    rw_pre_f = [mu_rkv, mu_lora, wt["rwkv_w0"], wt["rwkv_a0"], wt["rwkv_k_k"], wt["rwkv_k_a"], w2p, a2p, g2]
    rw = fn_fwd("rwkv_pre", _rwkv_pre, nt, rw_pre_t, rw_pre_f, [(t, tr, w, F32)] * 7)
    r_, lw_, k2_, v_, nkk_, b_, gate_ = rw
    rw_seq = [(a, None) for a in (r_, lw_, k2_, v_, nkk_, b_)]

    def rw_fn(sv, cv, hts, ids):
        return _rwkv_chunks([(*s, ht) for s, ht in zip(sv, hts)])

    yr_scan, rw_states = scan_fwd("rwkv_scan", rw_fn, RWKV_CHUNK, rw_seq, [], n_pairs, rw_ppb)
    rw_post_t = [(a, tr, w, 0) for a in (yr_scan, r_, k2_, v_, gate_)]
    rw_post_f = [r_k, wt["rwkv_ln_w"], wt["rwkv_ln_b"]]
    y_rwkv, y_rwkv_t = fn_fwd("rwkv_post", _rwkv_post, nt, rw_post_t, rw_post_f, [(t, tr, w, BF16)], (0,))

    ymix = jnp.concatenate([y_ssd, y_rwkv], axis=1)
    ymix_t = jnp.concatenate([y_ssd_t, y_rwkv_t], axis=0)
    w_out = big.get("w_out", ymix)
    x1 = matmul("out_proj", ymix, w_out, resid=x)

    h2, h2t = norm_fwd("norm_x", x1, wt["norm_x_g"], tr2)
    mrows = mem.shape[0]
    mn, mnt = norm_fwd("norm_mem", mem, wt["norm_mem_g"], mrows)
    wq, wk, wv, wo = [big.get(nm, ymix) for nm in ("xattn_wq", "xattn_wk", "xattn_wv", "xattn_wo")]
    q = matmul("xattn_q", h2, wq)
    kx = matmul("xattn_k", mn, wk)
    vx = matmul("xattn_v", mn, wv)
    ao, aot = fn_fwd("xattn_core", _attn, nt2, [(q, tr2, d, 0)], [kx, vx], [(t, tr2, d, BF16)], (0,))
    x2 = matmul("xattn_o", ao, wo, resid=x1)

    h3, h3t = norm_fwd("norm_ffn", x2, wt["norm_ffn_g"], tr2)
    w1, w2 = big.get("ffn_w1", h3), big.get("ffn_w2", h3)
    a1 = matmul("ffn_up", h3, w1, out_dtype=BF16)
    dff = a1.shape[1]
    f1, f1t = fn_fwd("ffn_act", _relu2, nt, [(a1, tr, dff, 0)], [], [(t, tr, dff, BF16)], (0,))
    x3 = matmul("ffn_down", f1, w2, resid=x2)

    dx3, dx3b, g_final, loss_tile = loss_head(x3, tgt, wt["final_norm_g"].reshape(1, d), tr2)

    grads = {"final_norm_g": g_final.reshape(d)}
    grads["ffn_w2"] = matmul("ffn_down_dw", f1t, dx3b)
    df1 = matmul("ffn_down_dx", dx3b, w2, tb=True, out_dtype=BF16)
    (da1,), _ = fn_bwd("ffn_act_bwd", _relu2, nt, [(a1, tr, dff, 0)], [], [(df1, tr, dff, 0)], lambda c: [c[0].astype(F32)],
                       [(t, tr, dff, BF16)])
    grads["ffn_w1"] = matmul("ffn_up_dw", h3t, da1, out_slots=4)
    dh3 = reducer.launch(0, grads, matmul("ffn_up_dx", da1, w1, tb=True))
    dx2, dx2b, grads["norm_ffn_g"] = norm_bwd("norm_ffn_bwd", x2, wt["norm_ffn_g"], dh3, dx3, tr2)

    grads["xattn_wo"] = matmul("xattn_o_dw", aot, dx2b)
    dao = matmul("xattn_o_dx", dx2b, wo, tb=True)
    (dq,), (dkx, dvx) = fn_bwd("xattn_core_bwd", _attn, nt2, [(q, tr2, d, 0)], [kx, vx], [(dao, tr2, d, 0)], lambda c: c,
                               [(t, tr2, d, BF16)])
    grads["xattn_wq"] = matmul("xattn_q_dw", h2t, dq)
    dh2 = matmul("xattn_q_dx", dq, wq, tb=True)
    dkb, dvb = dkx.astype(BF16), dvx.astype(BF16)
    grads["xattn_wk"] = matmul("xattn_k_dw", mnt, dkb)
    grads["xattn_wv"] = matmul("xattn_v_dw", mnt, dvb)
    dmn = matmul("xattn_k_dx", dkb, wk, tb=True)
    dmn = matmul("xattn_v_dx", dvb, wv, tb=True, resid=dmn)
    _, _, grads["norm_mem_g"] = norm_bwd("norm_mem_bwd", mem, wt["norm_mem_g"], dmn, None, mrows)
    dx1, dx1b, grads["norm_x_g"] = norm_bwd("norm_x_bwd", x1, wt["norm_x_g"], dh2, dx2, tr2)

    grads["w_out"] = matmul("out_proj_dw", ymix_t, dx1b)
    dymix = reducer.launch(1, grads, matmul("out_proj_dx", dx1b, w_out, tb=True))

    (dyr, dr1, dk1, dv1, dgate), (g_rk, grads["rwkv_ln_w"], grads["rwkv_ln_b"]) = fn_bwd(
        "rwkv_post_bwd", _rwkv_post, nt, rw_post_t, rw_post_f, [(dymix, tr, w, 1)], lambda c: c, [(t, tr, w, F32)] * 5)
    grads["rwkv_r_k"] = g_rk.reshape(wt["rwkv_r_k"].shape)
    (dr2, dlw, dk2, dv2, dnkk, db), _ = scan_bwd("rwkv_scan_bwd", rw_fn, RWKV_CHUNK, rw_seq, [], rw_states, dyr, n_pairs, rw_ppb)
    rw_ct = [(a, tr, w, 0) for a in (dr1, dr2, dlw, dk1, dk2, dv1, dv2, dnkk, db, dgate)]

    def rw_ct_fn(c):
        return (c[0] + c[1], c[2], c[3] + c[4], c[5] + c[6], c[7], c[8], c[9])

    (durkv, dulora, dhrkv, dhlora), rw_pg = fn_bwd(
        "rwkv_pre_bwd", _rwkv_pre, nt, rw_pre_t, rw_pre_f, rw_ct, rw_ct_fn,
        [(t, tr, 3 * w, F32), (t, tr, lora_w, F32), (nt * HALO, HALO, 3 * w, F32), (nt * HALO, HALO, lora_w, F32)])
    durkv = _unhalo(durkv, dhrkv, tr)
    dulora = _unhalo(dulora, dhlora, tr)
    g_mu_rkv, g_mu_lora, grads["rwkv_w0"], grads["rwkv_a0"], grads["rwkv_k_k"], grads["rwkv_k_a"], g_w2p, g_a2p, grads["rwkv_g2"] = rw_pg
    grads["rwkv_mu"] = jnp.concatenate([g_mu_rkv, g_mu_lora[:, :dr], g_mu_lora[:, LANES:LANES + ar], g_mu_lora[:, 2 * LANES:]], axis=1)
    grads["rwkv_w2"] = g_w2p[:dr]
    grads["rwkv_a2"] = g_a2p[:ar]

    (dys, dxs1, dz), (g_d, grads["ssd_norm_g"]) = fn_bwd(
        "ssd_post_bwd", _ssd_post, nt, ssd_post_t, ssd_post_f, [(dymix, tr, w, 0)], lambda c: c, [(t, tr, w, F32)] * 3)
    grads["ssd_d"] = g_d[:, :nh]
    (dxs2, dbp, dcp, ddtp), (g_alog,) = scan_bwd("ssd_scan_bwd", ssd_fn, SSD_CHUNK, ssd_seq, [a_log], ssd_states, dys, n_pairs, ssd_ppb)
    grads["ssd_a_log"] = g_alog[:, :nh]
    ssd_ct = [(dxs1, tr, w, 0), (dxs2, tr, w, 0), (dbp, tr, w, 0), (dcp, tr, w, 0), (ddtp, tr, w, 0)]

    def ssd_ct_fn(c):
        def group_sum(a):
            parts = []
            for gi in range(SSD_GROUPS):
                s = a[:, gi * ppg * LANES:(gi * ppg + 1) * LANES]
                for j in range(1, ppg):
                    s = s + a[:, (gi * ppg + j) * LANES:(gi * ppg + j + 1) * LANES]
                parts.append(s)
            return parts
        ddt = c[4][:, :LANES]
        for j in range(1, n_pairs):
            ddt = ddt + c[4][:, j * LANES:(j + 1) * LANES]
        return (jnp.concatenate([c[0] + c[1]] + group_sum(c[2]) + group_sum(c[3]), axis=1), ddt)

    (dxbc, dhxbc, ddtraw), ssd_pg = fn_bwd(
        "ssd_pre_bwd", _ssd_pre, nt, ssd_pre_t, ssd_pre_f, ssd_ct, ssd_ct_fn,
        [(t, tr, conv_dim, F32), (nt * HALO, HALO, conv_dim, F32), (t, tr, LANES, F32)])
    dxbc = _unhalo(dxbc, dhxbc, tr)
    grads["ssd_conv_w"] = jnp.concatenate(ssd_pg[:SSD_CONV], axis=0)
    grads["ssd_conv_b"] = ssd_pg[SSD_CONV]
    grads["ssd_dt_bias"] = ssd_pg[SSD_CONV + 1][:, :nh]

    du = jnp.concatenate([dz, dxbc, ddtraw, durkv, dulora], axis=1).astype(BF16)
    g_perm_t = matmul("in_proj_dw", du.T, h1)
    grads["w_in"] = jnp.concatenate([g_perm_t[offs[nm]:offs[nm] + segs[nm][1]] for nm in order], axis=0)
    dh1 = matmul("in_proj_dx", du, w_perm_t)
    dh1 = reducer.launch(2, grads, dh1)
    grad_x, _, grads["norm_mix_g"] = norm_bwd("norm_mix_bwd", x, wt["norm_mix_g"], dh1, dx1, tr2)
    return loss_tile, grad_x, grads


def _pack(arrs):
    flat = jnp.concatenate([a.reshape(-1) for a in arrs])
    n = flat.shape[0]
    rows = -(-n // (8 * LANES)) * 8
    return jnp.pad(flat, (0, rows * LANES - n)).reshape(rows, LANES)


def _unpack(packed, shapes):
    flat = packed.reshape(-1)
    out, o = [], 0
    for s in shapes:
        n = math.prod(s)
        out.append(flat[o:o + n].reshape(s))
        o += n
    return out


def _as2d(a):
    return a.reshape(-1, a.shape[-1])


def _shard_view(n, a):
    return _as2d(a[0]).T if n in TRANSPOSED else _as2d(a[0])


class _GatheredWeights:
    def __init__(self, shard2d, q, c):
        self.shard2d, self.q, self.c = shard2d, q, c
        self.raw, self.ready = {}, {}

    def start(self, gi, after):
        shards = [self.shard2d[n].astype(BF16) for n in GATHER_GROUPS[gi]]
        if after is not None:
            shards, _ = lax.optimization_barrier((shards, after))
        gathered = gather_two_level("gather_weights_%d" % gi, shards, gi + 1)
        self.raw.update(zip(GATHER_GROUPS[gi], gathered))

    def get(self, name, after):
        if name not in self.ready:
            g = self.raw[name]
            if after is not None:
                g, _ = lax.optimization_barrier((g, after))
            self.ready[name] = _from_slots(g, 0) if BIG_AXIS[name] == 0 else g
        return self.ready[name]


class _GradReducer:
    def __init__(self, q, c, update):
        self.q, self.c, self.update = q, c, update
        self.pending, self.updated = {}, {}

    def launch(self, gi, grads, nxt):
        names = REDUCE_GROUPS[gi]
        slots = [grads[n] if grads[n].ndim == 3 else _to_slots(grads[n], REDUCE_AXIS[n]) for n in names]
        rows = [s.shape[1] for s in slots]
        sent = [half_call("send_half_" + n, s, 1 - self.c, None, BF16) for n, s in zip(names, slots)]
        got = core_swap("swap_halves_%d" % gi, sent)
        parts = [half_call("chip_sum_" + n, s, self.c, g, BF16) for n, s, g in zip(names, slots, got)]
        parts, nxt = lax.optimization_barrier((parts, nxt))
        slots = scatter_slots("scatter_grads_%d" % gi, parts, len(GATHER_GROUPS) + 1 + gi)
        self.pending[gi] = (slots, rows)
        return self.finish(gi - 1, nxt) if gi > 0 else nxt

    def finish(self, gi, nxt):
        names = REDUCE_GROUPS[gi]
        slots, rows = self.pending[gi]
        halves = []
        for n, s in zip(names, slots):
            halves.append(sum_slots("sum_" + n, s))
        others = core_swap("swap_reduced_%d" % gi, halves)
        lo = [jnp.where(self.c == 0, mine, other) for mine, other in zip(halves, others)]
        hi = [jnp.where(self.c == 0, other, mine) for mine, other in zip(halves, others)]
        results = [self.update(n, _join_halves(l, h, r)) for n, l, h, r in zip(names, lo, hi, rows)]
        if nxt is not None:
            results, nxt = lax.optimization_barrier((results, nxt))
        self.updated.update(zip(names, results))
        return nxt


def _step(a):
    x, mem, tgt = a["x"][0], a["mem"][0], a["loss_target"][0]
    q = 2 * lax.axis_index("x") + lax.axis_index("y")

    shard2d = {n: _shard_view(n, a[n]) for n in BIG}
    small_sh = {n: _as2d(a[n][0]) for n in SMALL_SHARDED}
    c = lax.axis_index("c")
    full = {}
    big = _GatheredWeights(shard2d, q, c)
    gathered = gather_shards("gather_small", [small_sh[n] for n in SMALL_SHARDED])
    for n, g in zip(SMALL_SHARDED, gathered):
        full[n] = _from_slots(g, 1)

    wt = {n: (a[n] if a[n].ndim <= 2 else a[n][0]) for n in WEIGHTS if n not in BIG and n not in SMALL_SHARDED}
    for n in SMALL_SHARDED:
        wt[n] = small_sh[n]
    shards = dict(shard2d)
    shards.update({n: small_sh[n] for n in REDUCED if n not in BIG})

    def update(n, gsum):
        return adamw("adamw_" + n, shards[n], _shard_view(n, a["m_" + n]), _shard_view(n, a["v_" + n]), gsum)

    reducer = _GradReducer(q, c, update)
    loss_tile, grad_x, grads = _local_grads(x, mem, tgt, wt, full, big, reducer)
    reducer.finish(len(REDUCE_GROUPS) - 1, None)
    out = {}
    for n, vals in reducer.updated.items():
        for key, val in zip(("grad_", "delta_", "new_m_", "new_v_"), vals):
            out[key + n] = (val.T if n in TRANSPOSED else val).reshape(a[n].shape)

    small = [n for n in WEIGHTS if n not in REDUCED]
    red = _unpack(all_reduce_small("all_reduce_small", _pack([grads[n] for n in small])), [grads[n].shape for n in small])
    g_loc = {}
    for n, g in zip(small, red):
        if n in SMALL_SHARDED:
            cols = g.shape[1] // 4
            g = lax.dynamic_slice_in_dim(g, q * cols, cols, axis=1)
        g_loc[n] = g.reshape(a[n].shape)
    res = adamw("adamw_small", *[_pack([src[n] for n in small]) for src in
                                 ({n: a[n] for n in small}, {n: a["m_" + n] for n in small}, {n: a["v_" + n] for n in small})],
                _pack([g_loc[n] for n in small]))
    shapes = [a[n].shape for n in small]
    for key, packed in zip(("grad_", "delta_", "new_m_", "new_v_"), res):
        for n, val in zip(small, _unpack(packed, shapes)):
            out[key + n] = val

    loss = lax.psum(loss_tile[0, 0], ("x", "y", "c"))
    ordered = [loss, grad_x.reshape(a["x"].shape)]
    for key in ("grad_", "delta_", "new_m_", "new_v_"):
        ordered += [out[key + n] for n in WEIGHTS]
    return tuple(ordered)


def kernel(x, mem, norm_mix_g, w_in, ssd_conv_w, ssd_conv_b, ssd_dt_bias, ssd_a_log, ssd_d, ssd_norm_g, rwkv_mu, rwkv_w0, rwkv_w2, rwkv_a0, rwkv_a2, rwkv_g2, rwkv_k_k, rwkv_k_a, rwkv_r_k, rwkv_ln_w, rwkv_ln_b, w_out, norm_x_g, norm_mem_g, xattn_wq, xattn_wk, xattn_wv, xattn_wo, norm_ffn_g, ffn_w1, ffn_w2, final_norm_g, loss_target, m_norm_mix_g, m_w_in, m_ssd_conv_w, m_ssd_conv_b, m_ssd_dt_bias, m_ssd_a_log, m_ssd_d, m_ssd_norm_g, m_rwkv_mu, m_rwkv_w0, m_rwkv_w2, m_rwkv_a0, m_rwkv_a2, m_rwkv_g2, m_rwkv_k_k, m_rwkv_k_a, m_rwkv_r_k, m_rwkv_ln_w, m_rwkv_ln_b, m_w_out, m_norm_x_g, m_norm_mem_g, m_xattn_wq, m_xattn_wk, m_xattn_wv, m_xattn_wo, m_norm_ffn_g, m_ffn_w1, m_ffn_w2, m_final_norm_g, v_norm_mix_g, v_w_in, v_ssd_conv_w, v_ssd_conv_b, v_ssd_dt_bias, v_ssd_a_log, v_ssd_d, v_ssd_norm_g, v_rwkv_mu, v_rwkv_w0, v_rwkv_w2, v_rwkv_a0, v_rwkv_a2, v_rwkv_g2, v_rwkv_k_k, v_rwkv_k_a, v_rwkv_r_k, v_rwkv_ln_w, v_rwkv_ln_b, v_w_out, v_norm_x_g, v_norm_mem_g, v_xattn_wq, v_xattn_wk, v_xattn_wv, v_xattn_wo, v_norm_ffn_g, v_ffn_w1, v_ffn_w2, v_final_norm_g):
    return _step(dict(locals()))
```

```python
import functools
import math

import jax
import jax.numpy as jnp
from jax import lax
from jax.experimental import pallas as pl
from jax.experimental.pallas import tpu as pltpu
from jax.experimental.pallas import tpu_sc as plsc

F32 = jnp.float32
BF16 = jnp.bfloat16
HIGHEST = lax.Precision.HIGHEST
MESH_ID = pl.DeviceIdType.MESH

NORM_EPS = 1e-6
RWKV_LN_EPS = 64e-5
HEAD_DIM = 64
PAIR = 2 * HEAD_DIM
LANES = 128
SSD_STATE = 128
SSD_CHUNK = 128
SSD_GROUPS = 2
SSD_CONV = 4
RWKV_CHUNK = 64
HALO = 8
ROW_TILE = 128
PAIRS_PER_STEP = 4
XATTN_HEADS = 4
RWKV_PASSES = 1
VMEM_LIMIT = 56 * 1024 * 1024
MATMUL_VMEM = 40 * 1024 * 1024

ADAM_LR = 0.001
ADAM_B1 = 0.9
ADAM_B2 = 0.999
ADAM_EPS = 1e-08
ADAM_WD = 0.01
ADAM_STEP = 10


def _dims(ca, cb):
    return (((ca,), (cb,)), ((), ()))


def _split_bf16(a):
    hi = a.astype(BF16)
    lo = (a - hi.astype(F32)).astype(BF16)
    return hi, lo


def _mm_impl(a, b, ca, cb, passes):
    dn = _dims(ca, cb)
    if passes == 1:
        return lax.dot_general(a.astype(BF16), b.astype(BF16), dn, preferred_element_type=F32)
    ah, al = _split_bf16(a)
    bh, bl = _split_bf16(b)
    out = lax.dot_general(ah, bh, dn, preferred_element_type=F32)
    out = out + lax.dot_general(ah, bl, dn, preferred_element_type=F32)
    return out + lax.dot_general(al, bh, dn, preferred_element_type=F32)


@functools.partial(jax.custom_vjp, nondiff_argnums=(2, 3, 4))
def mm(a, b, ca, cb, passes):
    return _mm_impl(a, b, ca, cb, passes)


def _mm_fwd(a, b, ca, cb, passes):
    return _mm_impl(a, b, ca, cb, passes), (a, b)


def _mm_bwd(ca, cb, passes, res, g):
    a, b = res
    da = mm(g, b, 1, 1 - cb, passes) if ca == 1 else mm(b, g, 1 - cb, 1, passes)
    db = mm(a, g, 1 - ca, 0, passes) if cb == 0 else mm(g, a, 0, 1 - ca, passes)
    return da, db


mm.defvjp(_mm_fwd, _mm_bwd)


def _dot_exact(a, b):
    return lax.dot_general(a, b, _dims(1, 0), precision=HIGHEST, preferred_element_type=F32)


def _iota(shape, dim):
    return lax.broadcasted_iota(jnp.int32, shape, dim)


def _sigmoid(x):
    return 1.0 / (1.0 + jnp.exp(-x))


def _silu(x):
    return x * _sigmoid(x)


def _softplus(x):
    return jnp.maximum(x, 0.0) + jnp.log(1.0 + jnp.exp(-jnp.abs(x)))


def _rms(x, g):
    return x * lax.rsqrt(jnp.mean(x * x, axis=-1, keepdims=True) + NORM_EPS) * g


def _select_mm(x, sel):
    hi = x.astype(BF16)
    r1 = x - hi.astype(F32)
    mid = r1.astype(BF16)
    lo = (r1 - mid.astype(F32)).astype(BF16)
    dn = _dims(1, 0)
    out = lax.dot_general(hi, sel, dn, preferred_element_type=F32)
    out = out + lax.dot_general(mid, sel, dn, preferred_element_type=F32)
    return out + lax.dot_general(lo, sel, dn, preferred_element_type=F32)


def _head_sum_impl(x, n):
    sel = (_iota((n, LANES), 0) // HEAD_DIM == _iota((n, LANES), 1)).astype(BF16)
    return _select_mm(x, sel)


def _head_expand_impl(s, n):
    sel = (_iota((LANES, n), 1) // HEAD_DIM == _iota((LANES, n), 0)).astype(BF16)
    return _select_mm(s, sel)


@functools.partial(jax.custom_vjp, nondiff_argnums=(1,))
def _head_sum_n(x, n):
    return _head_sum_impl(x, n)


@functools.partial(jax.custom_vjp, nondiff_argnums=(1,))
def _head_expand(s, n):
    return _head_expand_impl(s, n)


_head_sum_n.defvjp(lambda x, n: (_head_sum_impl(x, n), None), lambda n, _, g: (_head_expand(g, n),))
_head_expand.defvjp(lambda s, n: (_head_expand_impl(s, n), None), lambda n, _, g: (_head_sum_n(g, n),))


def _head_sum(x):
    return _head_sum_n(x, x.shape[1])


def _row_vector_expand(v, n):
    v8 = jnp.broadcast_to(v, (8, LANES))
    return jnp.sum(_head_expand(v8, n), axis=0, keepdims=True) * 0.125


def _shift_rows_impl(u, halo, s):
    rolled = pltpu.roll(u, s, 0)
    top = jnp.where(_iota((HALO, 1), 0) < s, pltpu.roll(halo, s, 0), rolled[:HALO])
    return jnp.concatenate([top, rolled[HALO:]], axis=0)


@functools.partial(jax.custom_vjp, nondiff_argnums=(2,))
def _shift_rows(u, halo, s):
    return _shift_rows_impl(u, halo, s)


def _shift_rows_bwd(s, _, g):
    tr = g.shape[0]
    rolled = pltpu.roll(g, tr - s, 0)
    hrow = _iota((HALO, 1), 0)
    bottom = jnp.where(hrow < HALO - s, rolled[tr - HALO:], 0.0)
    dhalo = jnp.where(hrow >= HALO - s, pltpu.roll(g[:HALO], HALO - s, 0), 0.0)
    return jnp.concatenate([rolled[:tr - HALO], bottom], axis=0), dhalo


_shift_rows.defvjp(lambda u, halo, s: (_shift_rows_impl(u, halo, s), None), _shift_rows_bwd)


def _params(sem):
    return pltpu.CompilerParams(dimension_semantics=sem, vmem_limit_bytes=VMEM_LIMIT)


def row_call(name, body, n_tiles, tiled, full, out_tiled, out_acc):
    nt, nf, no, na = len(tiled), len(full), len(out_tiled), len(out_acc)

    def kern(*refs):
        tv = [r[...] for r in refs[:nt]]
        fv = [r[...] for r in refs[nt:nt + nf]]
        outs, accs = body(tv, fv)
        for r, v in zip(refs[nt + nf:nt + nf + no], outs):
            r[...] = v.astype(r.dtype)
        if na:
            a_refs = refs[nt + nf + no:]
            first = pl.program_id(0) == 0

            @pl.when(first)
            def _():
                for r, v in zip(a_refs, accs):
                    r[...] = v

            @pl.when(jnp.logical_not(first))
            def _():
                for r, v in zip(a_refs, accs):
                    r[...] += v

    in_specs = [pl.BlockSpec((rt, w), functools.partial(lambda i, cb: (i, cb), cb=cb)) for (_, rt, w, cb) in tiled]
    in_specs += [pl.BlockSpec(a.shape, lambda i: (0, 0)) for a in full]
    out_specs = [pl.BlockSpec((rt, w), lambda i: (i, 0)) for (_, rt, w, _) in out_tiled]
    out_specs += [pl.BlockSpec(s, lambda i: (0, 0)) for s in out_acc]
    out_shape = [jax.ShapeDtypeStruct((rows, w), dt) for (rows, _, w, dt) in out_tiled]
    out_shape += [jax.ShapeDtypeStruct(s, F32) for s in out_acc]
    res = pl.pallas_call(
        kern, name=name, grid=(n_tiles,), in_specs=in_specs, out_specs=out_specs, out_shape=out_shape,
        compiler_params=_params(("arbitrary",)),
    )(*[t[0] for t in tiled], *full)
    return list(res[:no]), list(res[no:])


def _pick(dim, cands):
    for c in cands:
        if dim % c == 0:
            return c
    return dim


def matmul(name, a, b, tb=False, resid=None, out_dtype=F32, out_slots=1, ta=False):
    (k, m) = a.shape if ta else a.shape[::-1]
    b_slots = b.shape[0] if b.ndim == 3 else 1
    n = b.shape[-2] if tb else b.shape[-1] * b_slots
    has_resid = resid is not None
    out_bytes = jnp.dtype(out_dtype).itemsize
    sizes = (2048, 1024, 896, 768, 512, 384, 256, 128)
    tm = _pick(m, sizes[1:])
    tn = _pick(n // max(out_slots, 1 if tb else b_slots), sizes[1:])

    def vmem_bytes(tk):
        return 2 * 2 * tk * (tm + tn) + tm * tn * (2 * out_bytes + 4 + (8 if has_resid else 0))

    k_slot = k // b_slots if tb else k
    tk = next((c for c in sizes if k_slot % c == 0 and vmem_bytes(c) <= MATMUL_VMEM), LANES)
    nk = k // tk
    n_per = n // (b_slots if not tb else 1) // tn
    k_per = k_slot // tk
    o_per = n // out_slots // tn

    def kern(*refs):
        a_ref, b_ref = refs[0], refs[1]
        o_ref, acc = refs[-2], refs[-1]
        kk = pl.program_id(2)
        part = lax.dot_general(a_ref[...], b_ref[...], _dims(0 if ta else 1, 1 if tb else 0), preferred_element_type=F32)

        def finish(out):
            if has_resid:
                out = out + refs[2][...]
            o_ref[...] = out.astype(o_ref.dtype)

        if nk == 1:
            finish(part)
            return

        @pl.when(kk == 0)
        def _():
            acc[...] = part

        @pl.when(jnp.logical_and(kk > 0, kk < nk - 1))
        def _():
            acc[...] += part

        @pl.when(kk == nk - 1)
        def _():
            finish(acc[...] + part)

    in_specs = [pl.BlockSpec((tk, tm), lambda i, j, kk: (kk, i)) if ta else pl.BlockSpec((tm, tk), lambda i, j, kk: (i, kk))]
    if b.ndim == 3 and tb:
        in_specs.append(pl.BlockSpec((None, tn, tk), lambda i, j, kk: (kk // k_per, j, kk % k_per)))
    elif b.ndim == 3:
        in_specs.append(pl.BlockSpec((None, tk, tn), lambda i, j, kk: (j // n_per, kk, j % n_per)))
    elif tb:
        in_specs.append(pl.BlockSpec((tn, tk), lambda i, j, kk: (j, kk)))
    else:
        in_specs.append(pl.BlockSpec((tk, tn), lambda i, j, kk: (kk, j)))
    args = [a, b]
    if has_resid:
        in_specs.append(pl.BlockSpec((tm, tn), lambda i, j, kk: (i, j)))
        args.append(resid)
    if out_slots > 1:
        out_spec = pl.BlockSpec((None, tm, tn), lambda i, j, kk: (j // o_per, i, j % o_per))
        out_shape = jax.ShapeDtypeStruct((out_slots, m, n // out_slots), out_dtype)
    else:
        out_spec = pl.BlockSpec((tm, tn), lambda i, j, kk: (i, j))
        out_shape = jax.ShapeDtypeStruct((m, n), out_dtype)
    return pl.pallas_call(
        kern, name=name, grid=(m // tm, n // tn, nk), in_specs=in_specs,
        out_specs=out_spec, out_shape=out_shape,
        scratch_shapes=[pltpu.VMEM((tm, tn), F32)],
        compiler_params=_params(("parallel", "parallel", "arbitrary")),
    )(*args)


def norm_fwd(name, x, g, tr):
    def body(tv, fv):
        return [_rms(tv[0], fv[0])], []
    rows, d = x.shape
    (h,), _ = row_call(name, body, rows // tr, [(x, tr, d, 0)], [g], [(rows, tr, d, BF16)], [])
    return h


def norm_bwd(name, x, g, dh, extra, tr):
    def body(tv, fv):
        _, vjp = jax.vjp(_rms, tv[0], fv[0])
        dx, dg = vjp(tv[1])
        if extra is not None:
            dx = dx + tv[2]
        return [dx, dx], [dg]
    rows, d = x.shape
    tiled = [(x, tr, d, 0), (dh, tr, d, 0)] + ([(extra, tr, d, 0)] if extra is not None else [])
    (dx, dxb), (dg,) = row_call(name, body, rows // tr, tiled, [g], [(rows, tr, d, F32), (rows, tr, d, BF16)], [g.shape])
    return dx, dxb, dg


def _ssd_pre(xbc, halo, dtraw, w0, w1, w2, w3, cb, dtb):
    y = w3 * xbc + w2 * _shift_rows(xbc, halo, 1) + w1 * _shift_rows(xbc, halo, 2) + w0 * _shift_rows(xbc, halo, 3) + cb
    return _silu(y), _softplus(dtraw + dtb)


def _ssd_post(ys, xs, z, dskip, ng):
    w = ys.shape[1]
    y = (ys + xs * _row_vector_expand(dskip, w)) * _silu(z)
    gw = w // SSD_GROUPS
    parts = []
    for gi in range(SSD_GROUPS):
        yg = y[:, gi * gw:(gi + 1) * gw]
        parts.append(yg * lax.rsqrt(jnp.mean(yg * yg, axis=-1, keepdims=True) + NORM_EPS))
    return jnp.concatenate(parts, axis=1) * ng


def _rwkv_pre(urkv, ulora, hrkv, hlora, mu_rkv, mu_lora, w0, a0, kkw, kaw, w2p, a2p, g2):
    w = w0.shape[1]
    urkv = urkv + (_shift_rows(urkv, hrkv, 1) - urkv) * mu_rkv
    ulora = ulora + (_shift_rows(ulora, hlora, 1) - ulora) * mu_lora
    r, k, v = urkv[:, :w], urkv[:, w:2 * w], urkv[:, 2 * w:]
    pw, pa, pg = ulora[:, :LANES], ulora[:, LANES:2 * LANES], ulora[:, 2 * LANES:]
    w_log = -_softplus(-(w0 + mm(jnp.tanh(pw), w2p, 1, 0, 1))) - 0.5
    lw = -jnp.exp(w_log)
    iclr = _sigmoid(a0 + mm(pa, a2p, 1, 0, 1))
    gate = mm(_sigmoid(pg), g2, 1, 0, 1)
    kk = k * kkw
    kk = kk / jnp.maximum(jnp.sqrt(_head_expand(_head_sum(kk * kk), w)), 1e-12)
    k2 = k * (1.0 + (iclr - 1.0) * kaw)
    return r, lw, k2, v, -kk, kk * iclr, gate


def _rwkv_post(ys, r, k2, v, gate, rk, lnw, lnb):
    w = ys.shape[1]
    inv = 1.0 / HEAD_DIM
    mean = _head_expand(_head_sum(ys), w) * inv
    d = ys - mean
    var = _head_expand(_head_sum(d * d), w) * inv
    yn = d * lax.rsqrt(var + RWKV_LN_EPS) * lnw + lnb
    bonus = _head_expand(_head_sum(r * k2 * rk), w) * v
    return (yn + bonus) * gate


def _attn(q, k, v):
    d = q.shape[1]
    hd = d // XATTN_HEADS
    outs = []
    for h in range(XATTN_HEADS):
        sl = slice(h * hd, (h + 1) * hd)
        s = mm(q[:, sl], k[:, sl], 1, 1, 1) * (hd ** -0.5)
        s = s - jnp.max(s, axis=-1, keepdims=True)
        p = jnp.exp(s)
        p = p / jnp.sum(p, axis=-1, keepdims=True)
        outs.append(mm(p, v[:, sl], 1, 0, 1))
    return jnp.concatenate(outs, axis=1)


def _relu2(a):
    return jnp.square(jnp.maximum(a.astype(F32), 0.0))


def fn_fwd(name, fn, n_tiles, tiled, full, out_tiled):
    def body(tv, fv):
        outs = fn(*tv, *fv)
        return (list(outs) if isinstance(outs, (tuple, list)) else [outs]), []
    outs, _ = row_call(name, body, n_tiles, tiled, full, out_tiled, [])
    return outs


def fn_bwd(name, fn, n_tiles, tiled, full, cts, ct_fn, out_tiled):
    nt = len(tiled)

    def body(tv, fv):
        outs, vjp = jax.vjp(fn, *tv[:nt], *fv)
        ct = ct_fn(tv[nt:])
        grads = vjp(tuple(ct) if isinstance(outs, (tuple, list)) else ct[0])
        return list(grads[:nt]), list(grads[nt:])
    return row_call(name, body, n_tiles, tiled + cts, full, out_tiled, [f.shape for f in full])


def _ssd_chunk(xs, bm, cm, dt_all, a_log, ht, p):
    q = xs.shape[0]
    lane = _iota((1, LANES), 1)
    row = _iota((q, 1), 0)
    tril = _iota((q, q), 0) >= _iota((q, q), 1)
    half = lane < HEAD_DIM
    da = dt_all * (-jnp.exp(a_log))
    cs = _dot_exact(tril.astype(F32), da)

    def col(mat, h):
        return jnp.sum(jnp.where(lane == h, mat, 0.0), axis=1, keepdims=True)

    cs0, cs1 = col(cs, 2 * p), col(cs, 2 * p + 1)
    xdt = xs * jnp.where(half, col(dt_all, 2 * p), col(dt_all, 2 * p + 1))
    csx = jnp.where(half, cs0, cs1)
    last = jnp.sum(jnp.where(row == q - 1, csx, 0.0), axis=0, keepdims=True)
    cb = mm(cm, bm, 1, 1, 1)
    y = mm(cm, ht, 1, 0, 1) * jnp.exp(csx)
    for csh, hm in ((cs0, half), (cs1, jnp.logical_not(half))):
        csl = jnp.broadcast_to(csh, (q, q))
        seg = csl - csl.T
        lmat = jnp.where(tril, jnp.exp(jnp.where(tril, seg, 0.0)), 0.0)
        y = y + jnp.where(hm, mm(cb * lmat, xdt, 1, 0, 1), 0.0)
    st = mm(bm, xdt * jnp.exp(last - csx), 0, 0, 1)
    return y, ht * jnp.exp(last) + st


def _rwkv_chunks(pairs):
    c = pairs[0][0].shape[0]
    ps = RWKV_PASSES
    lane = _iota((1, LANES), 1)
    row = _iota((c, 1), 0)
    ri, ci = _iota((c, c), 0), _iota((c, c), 1)
    tril_i, tril_s = ri >= ci, ri > ci
    eye = (ri == ci).astype(F32)
    half = lane < HEAD_DIM
    halves = (half, jnp.logical_not(half))
    bd = (_iota((LANES, LANES), 0) < HEAD_DIM) == (_iota((LANES, LANES), 1) < HEAD_DIM)
    tri = tril_i.astype(F32)
    n = len(pairs)
    heads = [(j, hm) for j in range(n) for hm in halves]

    cum = [_dot_exact(tri, p[1]) for p in pairs]
    at = [p[4] * jnp.exp(cm - p[1]) for p, cm in zip(pairs, cum)]
    en = [jnp.exp(-cm) for cm in cum]
    bt = [p[5] * e for p, e in zip(pairs, en)]
    kt = [p[2] * e for p, e in zip(pairs, en)]
    rt = [p[0] * jnp.exp(cm) for p, cm in zip(pairs, cum)]
    ah = [mm(at[j], pairs[j][6], 1, 1, ps) for j in range(n)]
    y = [mm(rt[j], pairs[j][6], 1, 1, ps) for j in range(n)]
    atm = [jnp.where(hm, at[j], 0.0) for j, hm in heads]
    rtm = [jnp.where(hm, rt[j], 0.0) for j, hm in heads]
    aab = [jnp.where(tril_s, mm(atm[i], bt[j], 1, 1, ps), 0.0) for i, (j, _) in enumerate(heads)]
    aak = [jnp.where(tril_s, mm(atm[i], kt[j], 1, 1, ps), 0.0) for i, (j, _) in enumerate(heads)]
    arb = [jnp.where(tril_i, mm(rtm[i], bt[j], 1, 1, ps), 0.0) for i, (j, _) in enumerate(heads)]
    ark = [jnp.where(tril_i, mm(rtm[i], kt[j], 1, 1, ps), 0.0) for i, (j, _) in enumerate(heads)]
    rhs = [ah[j] + mm(aak[i], pairs[j][3], 1, 0, ps) for i, (j, _) in enumerate(heads)]
    yv = [mm(ark[i], pairs[j][3], 1, 0, ps) for i, (j, _) in enumerate(heads)]
    tm = [eye + a_ for a_ in aab]
    pm = aab
    for _ in range(int(math.log2(c)) - 1):
        pm = [mm(p_, p_, 1, 0, ps) for p_ in pm]
        tm = [t_ + mm(t_, p_, 1, 0, ps) for t_, p_ in zip(tm, pm)]
    uh = [mm(tm[i], rhs[i], 1, 0, ps) for i in range(len(heads))]
    u = [jnp.where(half, uh[2 * j], uh[2 * j + 1]) for j in range(n)]
    yu = [mm(arb[i], u[j], 1, 0, ps) for i, (j, _) in enumerate(heads)]
    out = []
    for j in range(n):
        yj = y[j] + jnp.where(half, yu[2 * j] + yv[2 * j], yu[2 * j + 1] + yv[2 * j + 1])
        plast = jnp.sum(jnp.where(row == c - 1, cum[j], 0.0), axis=0, keepdims=True)
        upd = pairs[j][6] + mm(u[j], bt[j], 0, 0, ps) + mm(pairs[j][3], kt[j], 0, 0, ps)
        out.append((yj, jnp.where(bd, upd * jnp.exp(plast), 0.0)))
    return out


def _seq_spec(chunk, ppb, col, row_of):
    if col is None:
        return pl.BlockSpec((chunk, ppb * LANES), lambda pb, i: (row_of(i), pb))
    return pl.BlockSpec((chunk, LANES), lambda pb, i: (row_of(i), col(pb * ppb)))


def _pair_vals(refs, seq_in, j):
    return [r[...] if col is not None else r[:, j * LANES:(j + 1) * LANES] for r, (_, col) in zip(refs, seq_in)]


def scan_fwd(name, chunk_fn, chunk, seq_in, const_in, n_pairs, ppb):
    t = seq_in[0][0].shape[0]
    nc = t // chunk
    ns, ncst = len(seq_in), len(const_in)

    def kern(*refs):
        y_ref, st_ref, ht = refs[ns + ncst], refs[ns + ncst + 1], refs[ns + ncst + 2]

        @pl.when(pl.program_id(1) == 0)
        def _():
            ht[...] = jnp.zeros_like(ht)

        cv = [r[...] for r in refs[ns:ns + ncst]]
        h0 = [ht[j] for j in range(ppb)]
        for j in range(ppb):
            st_ref[j] = h0[j]
        sv = [_pair_vals(refs[:ns], seq_in, j) for j in range(ppb)]
        outs = chunk_fn(sv, cv, h0, [pl.program_id(0) * ppb + j for j in range(ppb)])
        for j, (y, hn) in enumerate(outs):
            y_ref[:, j * LANES:(j + 1) * LANES] = y
            ht[j] = hn

    in_specs = [_seq_spec(chunk, ppb, col, lambda i: i) for (_, col) in seq_in]
    in_specs += [pl.BlockSpec(a.shape, lambda pb, i: (0, 0)) for a in const_in]
    return pl.pallas_call(
        kern, name=name, grid=(n_pairs // ppb, nc), in_specs=in_specs,
        out_specs=[pl.BlockSpec((chunk, ppb * LANES), lambda pb, i: (i, pb)),
                   pl.BlockSpec((ppb, None, LANES, LANES), lambda pb, i: (pb, i, 0, 0))],
        out_shape=[jax.ShapeDtypeStruct((t, n_pairs * LANES), F32), jax.ShapeDtypeStruct((n_pairs, nc, LANES, LANES), F32)],
        scratch_shapes=[pltpu.VMEM((ppb, LANES, LANES), F32)],
        compiler_params=_params(("arbitrary", "arbitrary")),
    )(*[s[0] for s in seq_in], *const_in)


def scan_bwd(name, chunk_fn, chunk, seq_in, const_in, states, dy, n_pairs, ppb):
    t = dy.shape[0]
    nc = t // chunk
    ns, ncst = len(seq_in), len(const_in)

    def kern(*refs):
        seq_refs, cst_refs = refs[:ns], refs[ns:ns + ncst]
        st_ref, dy_ref = refs[ns + ncst], refs[ns + ncst + 1]
        o = ns + ncst + 2
        dseq_refs, dcst_refs, dht = refs[o:o + ns], refs[o + ns:o + ns + ncst], refs[o + ns + ncst]
        pb, i = pl.program_id(0), pl.program_id(1)

        @pl.when(i == 0)
        def _():
            dht[...] = jnp.zeros_like(dht)

        ids = [pb * ppb + j for j in range(ppb)]
        lanes = [slice(j * LANES, (j + 1) * LANES) for j in range(ppb)]

        def fn(*flat):
            sv = [list(flat[j * ns:(j + 1) * ns]) for j in range(ppb)]
            outs = chunk_fn(sv, list(flat[ppb * ns:ppb * ns + ncst]), list(flat[ppb * ns + ncst:]), ids)
            return tuple(y for y, _ in outs), tuple(h for _, h in outs)

        flat_in = [v for j in range(ppb) for v in _pair_vals(seq_refs, seq_in, j)]
        flat_in += [r[...] for r in cst_refs] + [st_ref[j] for j in range(ppb)]
        _, vjp = jax.vjp(fn, *flat_in)
        grads = vjp((tuple(dy_ref[:, ln] for ln in lanes), tuple(dht[j] for j in range(ppb))))
        for j in range(ppb):
            for r, g in zip(dseq_refs, grads[j * ns:(j + 1) * ns]):
                r[:, lanes[j]] = g
            dht[j] = grads[ppb * ns + ncst + j]
        dcv = grads[ppb * ns:ppb * ns + ncst]
        if ncst:
            first = jnp.logical_and(pb == 0, i == 0)

            @pl.when(first)
            def _():
                for r, g in zip(dcst_refs, dcv):
                    r[...] = g

            @pl.when(jnp.logical_not(first))
            def _():
                for r, g in zip(dcst_refs, dcv):
                    r[...] += g

    rev = lambda i: nc - 1 - i
    wide = pl.BlockSpec((chunk, ppb * LANES), lambda pb, i: (rev(i), pb))
    in_specs = [_seq_spec(chunk, ppb, col, rev) for (_, col) in seq_in]
    in_specs += [pl.BlockSpec(a.shape, lambda pb, i: (0, 0)) for a in const_in]
    in_specs += [pl.BlockSpec((ppb, None, LANES, LANES), lambda pb, i: (pb, rev(i), 0, 0)), wide]
    out_specs = [wide for _ in seq_in]
    out_specs += [pl.BlockSpec(a.shape, lambda pb, i: (0, 0)) for a in const_in]
    out_shape = [jax.ShapeDtypeStruct((t, n_pairs * LANES), F32) for _ in seq_in]
    out_shape += [jax.ShapeDtypeStruct(a.shape, F32) for a in const_in]
    res = pl.pallas_call(
        kern, name=name, grid=(n_pairs // ppb, nc), in_specs=in_specs, out_specs=out_specs, out_shape=out_shape,
        scratch_shapes=[pltpu.VMEM((ppb, LANES, LANES), F32)],
        compiler_params=_params(("arbitrary", "arbitrary")),
    )(*[s[0] for s in seq_in], *const_in, states, dy)
    return list(res[:ns]), list(res[ns:])


def loss_head(x3, tgt, g, tr):
    rows, d = x3.shape

    def body(tv, fv):
        def f(x, gg):
            e = jnp.square(_rms(x, gg) - tv[1])
            return 0.5 * jnp.sum(jnp.mean(e, axis=-1, keepdims=True), axis=0, keepdims=True)
        l, vjp = jax.vjp(f, tv[0], fv[0])
        dx, dg = vjp(jnp.ones((1, 1), F32))
        return [dx, dx], [dg, jnp.broadcast_to(l, (8, LANES))]
    (dx, dxb), (dg, l) = row_call("loss_head", body, rows // tr, [(x3, tr, d, 0), (tgt, tr, d, 0)], [g],
                                  [(rows, tr, d, F32), (rows, tr, d, BF16)], [g.shape, (8, LANES)])
    return dx, dxb, dg, l


def _adam_math(w, g, m, v):
    m = ADAM_B1 * m + (1.0 - ADAM_B1) * g
    v = ADAM_B2 * v + (1.0 - ADAM_B2) * jnp.square(g)
    m_hat = m / (1.0 - ADAM_B1 ** ADAM_STEP)
    v_hat = v / (1.0 - ADAM_B2 ** ADAM_STEP)
    delta = -ADAM_LR * (m_hat / (jnp.sqrt(v_hat) + ADAM_EPS) + ADAM_WD * w)
    return delta, m, v


def _tiling(rows, cols, limit):
    row_tile = max([d for d in range(16, rows + 1, 16) if rows % d == 0 and d * cols <= limit], default=0)
    col_tile = max([ct for ct in range(LANES, cols + 1, LANES) if cols % ct == 0 and rows * ct <= limit], default=0)
    if row_tile and row_tile * cols >= rows * col_tile:
        return row_tile, cols
    return (rows, col_tile) if col_tile else (rows, cols)


def ew_call(name, fn, ins, out_dtypes, limit=1 << 20):
    rows, cols = ins[0].shape
    br, bc = _tiling(rows, cols, limit)
    spec = pl.BlockSpec((br, bc), lambda i, j: (i, j))
    n_in = len(ins)

    def kern(*refs):
        for r, v in zip(refs[n_in:], fn(*[r[...] for r in refs[:n_in]])):
            r[...] = v.astype(r.dtype)

    return pl.pallas_call(
        kern, name=name, grid=(rows // br, cols // bc), in_specs=[spec] * n_in, out_specs=[spec] * len(out_dtypes),
        out_shape=[jax.ShapeDtypeStruct((rows, cols), dt) for dt in out_dtypes],
        compiler_params=_params(("parallel", "parallel")),
    )(*ins)


def adamw(name, w, m, v, g):
    return ew_call(name, lambda wv, mv, vv, gv: (gv, *_adam_math(wv, gv, mv, vv)), [w, m, v, g], [F32] * 4, 1 << 18)


def half_call(name, s, h, extra, out_dtype):
    n_slots, rows, cols = s.shape
    by_cols = _halves_by_cols(rows)
    hr, hc = (rows, cols // 2) if by_cols else (rows // 2, cols)
    br, bc = _tiling(hr, hc, 1 << 20)
    ni, nj = hr // br, hc // bc
    if by_cols:
        s_spec = pl.BlockSpec((None, br, bc), lambda sl, i, j, href: (sl, i, href[0] * nj + j))
    else:
        s_spec = pl.BlockSpec((None, br, bc), lambda sl, i, j, href: (sl, href[0] * ni + i, j))
    flat = pl.BlockSpec((None, br, bc), lambda sl, i, j, href: (sl, i, j))
    has_extra = extra is not None

    def kern(href, s_ref, *rest):
        v = s_ref[...]
        if has_extra:
            v = v + rest[0][...].astype(F32)
        rest[-1][...] = v.astype(out_dtype)

    grid_spec = pltpu.PrefetchScalarGridSpec(
        num_scalar_prefetch=1, grid=(n_slots, ni, nj), in_specs=[s_spec] + ([flat] if has_extra else []), out_specs=flat)
    return pl.pallas_call(
        kern, name=name, grid_spec=grid_spec, out_shape=jax.ShapeDtypeStruct((n_slots, hr, hc), out_dtype),
        compiler_params=_params(("parallel", "parallel", "parallel")),
    )(jnp.reshape(h, (1,)).astype(jnp.int32), s, *([extra] if has_extra else []))


def sum_slots(name, r):
    _, rows, cols = r.shape
    br, bc = _tiling(rows, cols, 1 << 20)

    def kern(r0, r1, r2, r3, o):
        o[...] = ((r0[...].astype(F32) + r1[...].astype(F32)) + r2[...].astype(F32)) + r3[...].astype(F32)

    in_specs = [pl.BlockSpec((None, br, bc), functools.partial(lambda i, j, s: (s, i, j), s=s)) for s in range(4)]
    return pl.pallas_call(
        kern, name=name, grid=(rows // br, cols // bc), in_specs=in_specs,
        out_specs=pl.BlockSpec((br, bc), lambda i, j: (i, j)),
        out_shape=jax.ShapeDtypeStruct((rows, cols), F32), compiler_params=_params(("parallel", "parallel")),
    )(r, r, r, r)


def _my_place():
    return lax.axis_index("x"), lax.axis_index("y"), lax.axis_index("c")


def _chip_peers(x, y):
    peers = [(1 - x, y), (x, 1 - y), (1 - x, 1 - y)]
    return peers, [2 * px + py for px, py in peers]


def gather_shards(name, arrays):
    nw = len(arrays)
    ANY = pl.BlockSpec(memory_space=pl.ANY)

    def body(*refs):
        ins, outs = refs[:nw], refs[nw:2 * nw]
        send, recv, loc = refs[2 * nw:]
        x, y, c = _my_place()
        q = 2 * x + y
        peers, chips = _chip_peers(x, y)

        def remote(w, j, slot):
            return pltpu.make_async_remote_copy(
                src_ref=ins[w], dst_ref=outs[w].at[slot], send_sem=send.at[w, j], recv_sem=recv.at[w, j],
                device_id=(*peers[j], c), device_id_type=MESH_ID)

        local = [pltpu.make_async_copy(ins[w], outs[w].at[q], loc.at[w]) for w in range(nw)]
        sends = [[remote(w, j, q) for j in range(3)] for w in range(nw)]
        for w in range(nw):
            local[w].start()
            for j in range(3):
                sends[w][j].start()
        for w in range(nw):
            local[w].wait()
            for j in range(3):
                sends[w][j].wait_send()
                remote(w, j, chips[j]).wait_recv()

    return pl.pallas_call(
        body, name=name, in_specs=[ANY] * nw, out_specs=[ANY] * nw,
        out_shape=[jax.ShapeDtypeStruct((4,) + a.shape, a.dtype) for a in arrays],
        scratch_shapes=[pltpu.SemaphoreType.DMA((nw, 3)), pltpu.SemaphoreType.DMA((nw, 3)), pltpu.SemaphoreType.DMA((nw,))],
        compiler_params=pltpu.CompilerParams(has_side_effects=True),
    )(*arrays)


def scatter_slots(name, arrays, collective_id):
    nw = len(arrays)

    def body(*refs):
        ins, outs = refs[:nw], refs[nw:2 * nw]
        send, recv, loc = refs[2 * nw:]
        x, y, c = _my_place()
        q = 2 * x + y
        peers, chips = _chip_peers(x, y)
        barrier = pltpu.get_barrier_semaphore()
        for p in peers:
            pl.semaphore_signal(barrier, inc=1, device_id=(*p, c), device_id_type=MESH_ID)
        pl.semaphore_wait(barrier, 3)

        def remote(w, j, src_slot, dst_slot):
            return pltpu.make_async_remote_copy(
                src_ref=ins[w].at[src_slot], dst_ref=outs[w].at[dst_slot], send_sem=send.at[w, j], recv_sem=recv.at[w, j],
                device_id=(*peers[j], c), device_id_type=MESH_ID)

        sends = [[remote(w, j, chips[j], q) for j in range(3)] for w in range(nw)]
        own = [pltpu.make_async_copy(ins[w].at[q], outs[w].at[q], loc.at[w]) for w in range(nw)]
        for w in range(nw):
            for j in range(3):
                sends[w][j].start()
            own[w].start()
        for w in range(nw):
            for j in range(3):
                sends[w][j].wait_send()
                remote(w, j, q, chips[j]).wait_recv()
            own[w].wait()

    return pl.kernel(
        body, out_type=[jax.ShapeDtypeStruct(a.shape, a.dtype) for a in arrays],
        mesh=plsc.ScalarSubcoreMesh(axis_name="sequencer", num_cores=1), name=name,
        scratch_types=[pltpu.SemaphoreType.DMA((nw, 3)), pltpu.SemaphoreType.DMA((nw, 3)), pltpu.SemaphoreType.DMA((nw,))],
        compiler_params=pltpu.CompilerParams(collective_id=collective_id),
    )(*arrays)


def _halves_by_cols(rows):
    return rows % 32 != 0


def _half_of(ref, shape, h):
    rows, cols = shape
    if _halves_by_cols(rows):
        return ref.at[:, pl.ds(h * (cols // 2), cols // 2)]
    return ref.at[pl.ds(h * (rows // 2), rows // 2)]


def _join_halves(lo, hi, rows):
    return jnp.concatenate([lo, hi], axis=lo.ndim - 1 if _halves_by_cols(rows) else lo.ndim - 2)


def gather_two_level(name, arrays, collective_id):
    nw = len(arrays)
    shapes = [a.shape for a in arrays]

    def body(*refs):
        ins, outs = refs[:nw], refs[nw:2 * nw]
        send, recv, loc = refs[2 * nw:]
        x, y, c = _my_place()
        q = 2 * x + y
        me, sibling = (x, y, c), (x, y, 1 - c)
        peers = [(1 - x, y), (x, 1 - y), (1 - x, 1 - y)]
        chips = [2 * px + py for px, py in peers]
        barrier = pltpu.get_barrier_semaphore()
        for dev in [sibling] + [(*p, c) for p in peers]:
            pl.semaphore_signal(barrier, inc=1, device_id=dev, device_id_type=MESH_ID)
        pl.semaphore_wait(barrier, 4)

        def mine(w):
            return _half_of(ins[w], shapes[w], c)

        def landed(w, chip, half):
            return _half_of(outs[w].at[chip], shapes[w], half)

        def copy(w, k, src, chip, half, to):
            return pltpu.make_async_remote_copy(
                src_ref=src, dst_ref=landed(w, chip, half), send_sem=send.at[w, k], recv_sem=recv.at[w, k],
                device_id=to, device_id_type=MESH_ID)

        first = [[copy(w, 0, mine(w), q, c, sibling)] + [copy(w, 1 + j, mine(w), q, c, (*peers[j], c)) for j in range(3)]
                 for w in range(nw)]
        own = [pltpu.make_async_copy(mine(w), landed(w, q, c), loc.at[w]) for w in range(nw)]
        for w in range(nw):
            for cp in first[w]:
                cp.start()
            own[w].start()
        passed = []
        for w in range(nw):
            for j in range(3):
                copy(w, 1 + j, mine(w), chips[j], c, me).wait_recv()
                fwd = copy(w, 4 + j, landed(w, chips[j], c), chips[j], c, sibling)
                fwd.start()
                passed.append(fwd)
        for w in range(nw):
            copy(w, 0, mine(w), q, 1 - c, me).wait_recv()
            for j in range(3):
                copy(w, 4 + j, mine(w), chips[j], 1 - c, me).wait_recv()
        for w in range(nw):
            for cp in first[w]:
                cp.wait_send()
            own[w].wait()
        for cp in passed:
            cp.wait_send()

    out_type = [jax.ShapeDtypeStruct((4,) + a.shape, a.dtype) for a in arrays]
    return pl.kernel(
        body, out_type=out_type, mesh=plsc.ScalarSubcoreMesh(axis_name="sequencer", num_cores=1), name=name,
        scratch_types=[pltpu.SemaphoreType.DMA((nw, 7)), pltpu.SemaphoreType.DMA((nw, 7)), pltpu.SemaphoreType.DMA((nw,))],
        compiler_params=pltpu.CompilerParams(collective_id=collective_id),
    )(*arrays)


def core_swap(name, arrays):
    nw = len(arrays)
    ANY = pl.BlockSpec(memory_space=pl.ANY)

    def body(*refs):
        ins, outs = refs[:nw], refs[nw:2 * nw]
        send, recv = refs[2 * nw:]
        x, y, c = _my_place()
        copies = [pltpu.make_async_remote_copy(
            src_ref=ins[w], dst_ref=outs[w], send_sem=send.at[w], recv_sem=recv.at[w],
            device_id=(x, y, 1 - c), device_id_type=MESH_ID) for w in range(nw)]
        for cp in copies:
            cp.start()
        for cp in copies:
            cp.wait_send()
            cp.wait_recv()

    return pl.pallas_call(
        body, name=name, in_specs=[ANY] * nw, out_specs=[ANY] * nw,
        out_shape=[jax.ShapeDtypeStruct(a.shape, a.dtype) for a in arrays],
        scratch_shapes=[pltpu.SemaphoreType.DMA((nw,)), pltpu.SemaphoreType.DMA((nw,))],
        compiler_params=pltpu.CompilerParams(has_side_effects=True),
    )(*arrays)


def all_reduce_small(name, v):
    rows = v.shape[0]
    VM = pl.BlockSpec(memory_space=pltpu.VMEM)

    def body(v_ref, o_ref, buf, send, recv):
        x, y, c = _my_place()
        me = 4 * x + 2 * y + c

        def peer(kx):
            return (x ^ ((kx >> 2) & 1), y ^ ((kx >> 1) & 1), c ^ (kx & 1))

        def copy(kx, slot):
            return pltpu.make_async_remote_copy(
                src_ref=v_ref, dst_ref=buf.at[slot], send_sem=send.at[kx - 1], recv_sem=recv.at[kx - 1],
                device_id=peer(kx), device_id_type=MESH_ID)

        sends = [copy(kx, me) for kx in range(1, 8)]
        for cp in sends:
            cp.start()
        buf[me] = v_ref[...]
        for kx in range(1, 8):
            copy(kx, me ^ kx).wait_recv()
        for cp in sends:
            cp.wait_send()
        acc = buf[0]
        for d in range(1, 8):
            acc = acc + buf[d]
        o_ref[...] = acc

    return pl.pallas_call(
        body, name=name, in_specs=[VM], out_specs=VM, out_shape=jax.ShapeDtypeStruct(v.shape, F32),
        scratch_shapes=[pltpu.VMEM((8, rows, LANES), F32), pltpu.SemaphoreType.DMA((7,)), pltpu.SemaphoreType.DMA((7,))],
        compiler_params=pltpu.CompilerParams(has_side_effects=True, vmem_limit_bytes=VMEM_LIMIT),
    )(v)


def _pad_cols(a, n):
    return jnp.pad(a, ((0, 0), (0, n - a.shape[1])))


def _pad_rows(a, n):
    return jnp.pad(a, ((0, n - a.shape[0]), (0, 0)))


def _halo(u, tr):
    t, cdim = u.shape
    tails = u.reshape(t // tr, tr, cdim)[:, tr - HALO:, :]
    tails = jnp.concatenate([jnp.zeros((1, HALO, cdim), u.dtype), tails[:-1]], axis=0)
    return tails.reshape(-1, cdim)


def _unhalo(du, dhalo, tr):
    t, cdim = du.shape
    n = t // tr
    dh = dhalo.reshape(n, HALO, cdim)
    dh = jnp.concatenate([dh[1:], jnp.zeros((1, HALO, cdim), du.dtype)], axis=0)
    d3 = du.reshape(n, tr, cdim)
    d3 = jnp.concatenate([d3[:, :tr - HALO, :], d3[:, tr - HALO:, :] + dh], axis=1)
    return d3.reshape(t, cdim)


def _to_slots(g, axis):
    r, cdim = g.shape
    if axis == 0:
        return g.reshape(4, r // 4, cdim)
    return g.reshape(r, 4, cdim // 4).transpose(1, 0, 2)


def _from_slots(s, axis):
    if axis == 0:
        return s.reshape(s.shape[0] * s.shape[1], s.shape[2])
    return s.transpose(1, 0, 2).reshape(s.shape[1], 4 * s.shape[2])


BIG = ("w_in", "w_out", "xattn_wq", "xattn_wk", "xattn_wv", "xattn_wo", "ffn_w1", "ffn_w2")
TRANSPOSED = ("w_in",)
BIG_AXIS = {"w_in": 0, "w_out": 0, "xattn_wq": 0, "xattn_wk": 0, "xattn_wv": 0, "xattn_wo": 0, "ffn_w1": 1, "ffn_w2": 0}
SMALL_SHARDED = ("ssd_conv_w", "rwkv_w2", "rwkv_a2", "rwkv_g2")
GATHER_GROUPS = (("w_in",), ("w_out", "xattn_wq", "xattn_wk", "xattn_wv", "xattn_wo"), ("ffn_w1", "ffn_w2"))
REDUCE_GROUPS = (("ffn_w2", "ffn_w1"), ("xattn_wo", "xattn_wq", "xattn_wk", "xattn_wv", "w_out"),
                 ("rwkv_w2", "rwkv_a2", "rwkv_g2", "w_in"))
REDUCED = BIG + ("rwkv_w2", "rwkv_a2", "rwkv_g2")
REDUCE_AXIS = dict(BIG_AXIS, rwkv_w2=1, rwkv_a2=1, rwkv_g2=1)
WEIGHTS = ("norm_mix_g", "w_in", "ssd_conv_w", "ssd_conv_b", "ssd_dt_bias", "ssd_a_log", "ssd_d", "ssd_norm_g",
           "rwkv_mu", "rwkv_w0", "rwkv_w2", "rwkv_a0", "rwkv_a2", "rwkv_g2", "rwkv_k_k", "rwkv_k_a", "rwkv_r_k",
           "rwkv_ln_w", "rwkv_ln_b", "w_out", "norm_x_g", "norm_mem_g", "xattn_wq", "xattn_wk", "xattn_wv", "xattn_wo",
           "norm_ffn_g", "ffn_w1", "ffn_w2", "final_norm_g")


def _local_grads(x, mem, tgt, wt, full, big, reducer):
    t, d = x.shape
    w = d // 2
    nh = w // HEAD_DIM
    n_pairs = nh // 2
    ppg = n_pairs // SSD_GROUPS
    bc = SSD_GROUPS * SSD_STATE
    conv_dim = w + 2 * bc
    tr = ROW_TILE
    nt = t // tr
    tr2 = 2 * tr if t % (2 * tr) == 0 else tr
    nt2 = t // tr2
    dr = wt["rwkv_w2"].shape[0]
    ar = wt["rwkv_a2"].shape[0]
    gr = wt["rwkv_g2"].shape[0]

    big.start(0, None)
    big.start(1, None)
    h1 = norm_fwd("norm_mix", x, wt["norm_mix_g"], tr2)
    w_in_t = big.get("w_in", (h1, full))
    o = 0
    segs = {}
    for nm, width in (("z", w), ("xbc", conv_dim), ("dt", nh), ("rkv", 3 * w), ("pw", dr), ("pa", ar), ("pg", gr)):
        segs[nm] = (o, width)
        o += width
    padded = {"z": w, "xbc": conv_dim, "dt": LANES, "rkv": 3 * w, "pw": LANES, "pa": LANES, "pg": gr}
    order = ("z", "xbc", "dt", "rkv", "pw", "pa", "pg")
    w_segs = [jnp.concatenate([_pad_rows(w_in_t[segs[nm][0]:segs[nm][0] + segs[nm][1]], padded[nm]) for nm in grp], axis=0)
              for grp in (("z",), ("xbc",), ("dt",), ("rkv",), ("pw", "pa", "pg"))]
    w_perm_t = jnp.concatenate(w_segs, axis=0)
    offs = {}
    o = 0
    for nm in order:
        offs[nm] = o
        o += padded[nm]
    lora_w = 2 * LANES + gr

    mu = wt["rwkv_mu"]
    mo = 3 * w
    mu_rkv = mu[:, :mo]
    mu_lora = jnp.concatenate([_pad_cols(mu[:, mo:mo + dr], LANES), _pad_cols(mu[:, mo + dr:mo + dr + ar], LANES),
                               mu[:, mo + dr + ar:]], axis=1)
    w2p = _pad_rows(full["rwkv_w2"], LANES)
    a2p = _pad_rows(full["rwkv_a2"], LANES)
    g2 = full["rwkv_g2"]
    conv_w = full["ssd_conv_w"]
    cw = [conv_w[i:i + 1] for i in range(SSD_CONV)]
    dt_bias = _pad_cols(wt["ssd_dt_bias"], LANES)
    a_log = _pad_cols(wt["ssd_a_log"], LANES)
    d_skip = _pad_cols(wt["ssd_d"], LANES)
    r_k = wt["rwkv_r_k"].reshape(1, w)

    z, xbc, dtraw, urkv, ulora = [matmul("in_proj_%d" % i, h1, ws, tb=True) for i, ws in enumerate(w_segs)]
    big.start(2, urkv)

    halo_xbc = _halo(xbc, tr)
    ssd_pre_t = [(xbc, tr, conv_dim, 0), (halo_xbc, HALO, conv_dim, 0), (dtraw, tr, LANES, 0)]
    ssd_pre_f = cw + [wt["ssd_conv_b"], dt_bias]
    act, dt = fn_fwd("ssd_pre", _ssd_pre, nt, ssd_pre_t, ssd_pre_f, [(t, tr, conv_dim, F32), (t, tr, LANES, F32)])

    nb = w // LANES
    ssd_seq = [(act, None), (act, lambda p: nb + p // ppg), (act, lambda p: nb + SSD_GROUPS + p // ppg), (dt, lambda p: 0)]
    ssd_ppb = min(ppg, PAIRS_PER_STEP)
    rw_ppb = min(n_pairs, 2 * PAIRS_PER_STEP)

    def ssd_fn(sv, cv, hts, ids):
        return [_ssd_chunk(*s, cv[0], ht, p) for s, ht, p in zip(sv, hts, ids)]

    y_scan, ssd_states = scan_fwd("ssd_scan", ssd_fn, SSD_CHUNK, ssd_seq, [a_log], n_pairs, ssd_ppb)
    ssd_post_t = [(y_scan, tr, w, 0), (act, tr, w, 0), (z, tr, w, 0)]
    ssd_post_f = [d_skip, wt["ssd_norm_g"]]
    (y_ssd,) = fn_fwd("ssd_post", _ssd_post, nt, ssd_post_t, ssd_post_f, [(t, tr, w, BF16)])

    halo_rkv, halo_lora = _halo(urkv, tr), _halo(ulora, tr)
    rw_pre_t = [(urkv, tr, 3 * w, 0), (ulora, tr, lora_w, 0), (halo_rkv, HALO, 3 * w, 0), (halo_lora, HALO, lora_w, 0)]
    rw_pre_f = [mu_rkv, mu_lora, wt["rwkv_w0"], wt["rwkv_a0"], wt["rwkv_k_k"], wt["rwkv_k_a"], w2p, a2p, g2]
    rw = fn_fwd("rwkv_pre", _rwkv_pre, nt, rw_pre_t, rw_pre_f, [(t, tr, w, F32)] * 7)
    r_, lw_, k2_, v_, nkk_, b_, gate_ = rw
    rw_seq = [(a, None) for a in (r_, lw_, k2_, v_, nkk_, b_)]

    def rw_fn(sv, cv, hts, ids):
        return _rwkv_chunks([(*s, ht) for s, ht in zip(sv, hts)])

    yr_scan, rw_states = scan_fwd("rwkv_scan", rw_fn, RWKV_CHUNK, rw_seq, [], n_pairs, rw_ppb)
    rw_post_t = [(a, tr, w, 0) for a in (yr_scan, r_, k2_, v_, gate_)]
    rw_post_f = [r_k, wt["rwkv_ln_w"], wt["rwkv_ln_b"]]
    (y_rwkv,) = fn_fwd("rwkv_post", _rwkv_post, nt, rw_post_t, rw_post_f, [(t, tr, w, BF16)])

    ymix = jnp.concatenate([y_ssd, y_rwkv], axis=1)
    w_out = big.get("w_out", ymix)
    x1 = matmul("out_proj", ymix, w_out, resid=x)

    h2 = norm_fwd("norm_x", x1, wt["norm_x_g"], tr2)
    mrows = mem.shape[0]
    mn = norm_fwd("norm_mem", mem, wt["norm_mem_g"], mrows)
    wq, wk, wv, wo = [big.get(nm, ymix) for nm in ("xattn_wq", "xattn_wk", "xattn_wv", "xattn_wo")]
    q = matmul("xattn_q", h2, wq)
    kx = matmul("xattn_k", mn, wk)
    vx = matmul("xattn_v", mn, wv)
    (ao,) = fn_fwd("xattn_core", _attn, nt2, [(q, tr2, d, 0)], [kx, vx], [(t, tr2, d, BF16)])
    x2 = matmul("xattn_o", ao, wo, resid=x1)

    h3 = norm_fwd("norm_ffn", x2, wt["norm_ffn_g"], tr2)
    w1, w2 = big.get("ffn_w1", h3), big.get("ffn_w2", h3)
    a1 = matmul("ffn_up", h3, w1, out_dtype=BF16)
    dff = a1.shape[1]
    (f1,) = fn_fwd("ffn_act", _relu2, nt, [(a1, tr, dff, 0)], [], [(t, tr, dff, BF16)])
    x3 = matmul("ffn_down", f1, w2, resid=x2)

    dx3, dx3b, g_final, loss_tile = loss_head(x3, tgt, wt["final_norm_g"].reshape(1, d), tr2)

    grads = {"final_norm_g": g_final.reshape(d)}
    grads["ffn_w2"] = matmul("ffn_down_dw", f1, dx3b, ta=True)
    df1 = matmul("ffn_down_dx", dx3b, w2, tb=True, out_dtype=BF16)
    (da1,), _ = fn_bwd("ffn_act_bwd", _relu2, nt, [(a1, tr, dff, 0)], [], [(df1, tr, dff, 0)], lambda c: [c[0].astype(F32)],
                       [(t, tr, dff, BF16)])
    grads["ffn_w1"] = matmul("ffn_up_dw", h3, da1, out_slots=4, ta=True)
    dh3 = reducer.launch(0, grads, matmul("ffn_up_dx", da1, w1, tb=True))
    dx2, dx2b, grads["norm_ffn_g"] = norm_bwd("norm_ffn_bwd", x2, wt["norm_ffn_g"], dh3, dx3, tr2)

    grads["xattn_wo"] = matmul("xattn_o_dw", ao, dx2b, ta=True)
    dao = matmul("xattn_o_dx", dx2b, wo, tb=True)
    (dq,), (dkx, dvx) = fn_bwd("xattn_core_bwd", _attn, nt2, [(q, tr2, d, 0)], [kx, vx], [(dao, tr2, d, 0)], lambda c: c,
                               [(t, tr2, d, BF16)])
    grads["xattn_wq"] = matmul("xattn_q_dw", h2, dq, ta=True)
    dh2 = matmul("xattn_q_dx", dq, wq, tb=True)
    dkb, dvb = dkx.astype(BF16), dvx.astype(BF16)
    grads["xattn_wk"] = matmul("xattn_k_dw", mn, dkb, ta=True)
    grads["xattn_wv"] = matmul("xattn_v_dw", mn, dvb, ta=True)
    dmn = matmul("xattn_k_dx", dkb, wk, tb=True)
    dmn = matmul("xattn_v_dx", dvb, wv, tb=True, resid=dmn)
    _, _, grads["norm_mem_g"] = norm_bwd("norm_mem_bwd", mem, wt["norm_mem_g"], dmn, None, mrows)
    dx1, dx1b, grads["norm_x_g"] = norm_bwd("norm_x_bwd", x1, wt["norm_x_g"], dh2, dx2, tr2)

    grads["w_out"] = matmul("out_proj_dw", ymix, dx1b, ta=True)
    dymix = reducer.launch(1, grads, matmul("out_proj_dx", dx1b, w_out, tb=True))

    (dyr, dr1, dk1, dv1, dgate), (g_rk, grads["rwkv_ln_w"], grads["rwkv_ln_b"]) = fn_bwd(
        "rwkv_post_bwd", _rwkv_post, nt, rw_post_t, rw_post_f, [(dymix, tr, w, 1)], lambda c: c, [(t, tr, w, F32)] * 5)
    grads["rwkv_r_k"] = g_rk.reshape(wt["rwkv_r_k"].shape)
    (dr2, dlw, dk2, dv2, dnkk, db), _ = scan_bwd("rwkv_scan_bwd", rw_fn, RWKV_CHUNK, rw_seq, [], rw_states, dyr, n_pairs, rw_ppb)
    rw_ct = [(a, tr, w, 0) for a in (dr1, dr2, dlw, dk1, dk2, dv1, dv2, dnkk, db, dgate)]

    def rw_ct_fn(c):
        return (c[0] + c[1], c[2], c[3] + c[4], c[5] + c[6], c[7], c[8], c[9])

    (durkv, dulora, dhrkv, dhlora), rw_pg = fn_bwd(
        "rwkv_pre_bwd", _rwkv_pre, nt, rw_pre_t, rw_pre_f, rw_ct, rw_ct_fn,
        [(t, tr, 3 * w, F32), (t, tr, lora_w, F32), (nt * HALO, HALO, 3 * w, F32), (nt * HALO, HALO, lora_w, F32)])
    durkv = _unhalo(durkv, dhrkv, tr)
    dulora = _unhalo(dulora, dhlora, tr)
    g_mu_rkv, g_mu_lora, grads["rwkv_w0"], grads["rwkv_a0"], grads["rwkv_k_k"], grads["rwkv_k_a"], g_w2p, g_a2p, grads["rwkv_g2"] = rw_pg
    grads["rwkv_mu"] = jnp.concatenate([g_mu_rkv, g_mu_lora[:, :dr], g_mu_lora[:, LANES:LANES + ar], g_mu_lora[:, 2 * LANES:]], axis=1)
    grads["rwkv_w2"] = g_w2p[:dr]
    grads["rwkv_a2"] = g_a2p[:ar]

    (dys, dxs1, dz), (g_d, grads["ssd_norm_g"]) = fn_bwd(
        "ssd_post_bwd", _ssd_post, nt, ssd_post_t, ssd_post_f, [(dymix, tr, w, 0)], lambda c: c, [(t, tr, w, F32)] * 3)
    grads["ssd_d"] = g_d[:, :nh]
    (dxs2, dbp, dcp, ddtp), (g_alog,) = scan_bwd("ssd_scan_bwd", ssd_fn, SSD_CHUNK, ssd_seq, [a_log], ssd_states, dys, n_pairs, ssd_ppb)
    grads["ssd_a_log"] = g_alog[:, :nh]
    ssd_ct = [(dxs1, tr, w, 0), (dxs2, tr, w, 0), (dbp, tr, w, 0), (dcp, tr, w, 0), (ddtp, tr, w, 0)]

    def ssd_ct_fn(c):
        def group_sum(a):
            parts = []
            for gi in range(SSD_GROUPS):
                s = a[:, gi * ppg * LANES:(gi * ppg + 1) * LANES]
                for j in range(1, ppg):
                    s = s + a[:, (gi * ppg + j) * LANES:(gi * ppg + j + 1) * LANES]
                parts.append(s)
            return parts
        ddt = c[4][:, :LANES]
        for j in range(1, n_pairs):
            ddt = ddt + c[4][:, j * LANES:(j + 1) * LANES]
        return (jnp.concatenate([c[0] + c[1]] + group_sum(c[2]) + group_sum(c[3]), axis=1), ddt)

    (dxbc, dhxbc, ddtraw), ssd_pg = fn_bwd(
        "ssd_pre_bwd", _ssd_pre, nt, ssd_pre_t, ssd_pre_f, ssd_ct, ssd_ct_fn,
        [(t, tr, conv_dim, F32), (nt * HALO, HALO, conv_dim, F32), (t, tr, LANES, F32)])
    dxbc = _unhalo(dxbc, dhxbc, tr)
    grads["ssd_conv_w"] = jnp.concatenate(ssd_pg[:SSD_CONV], axis=0)
    grads["ssd_conv_b"] = ssd_pg[SSD_CONV]
    grads["ssd_dt_bias"] = ssd_pg[SSD_CONV + 1][:, :nh]

    du = jnp.concatenate([dz, dxbc, ddtraw, durkv, dulora], axis=1).astype(BF16)
    g_perm_t = matmul("in_proj_dw", du, h1, ta=True)
    grads["w_in"] = jnp.concatenate([g_perm_t[offs[nm]:offs[nm] + segs[nm][1]] for nm in order], axis=0)
    dh1 = matmul("in_proj_dx", du, w_perm_t)
    dh1 = reducer.launch(2, grads, dh1)
    grad_x, _, grads["norm_mix_g"] = norm_bwd("norm_mix_bwd", x, wt["norm_mix_g"], dh1, dx1, tr2)
    return loss_tile, grad_x, grads


def _pack(arrs):
    flat = jnp.concatenate([a.reshape(-1) for a in arrs])
    n = flat.shape[0]
    rows = -(-n // (8 * LANES)) * 8
    return jnp.pad(flat, (0, rows * LANES - n)).reshape(rows, LANES)


def _unpack(packed, shapes):
    flat = packed.reshape(-1)
    out, o = [], 0
    for s in shapes:
        n = math.prod(s)
        out.append(flat[o:o + n].reshape(s))
        o += n
    return out


def _as2d(a):
    return a.reshape(-1, a.shape[-1])


def _shard_view(n, a):
    return _as2d(a[0]).T if n in TRANSPOSED else _as2d(a[0])


class _GatheredWeights:
    def __init__(self, shard2d, q, c):
        self.shard2d, self.q, self.c = shard2d, q, c
        self.raw, self.ready = {}, {}

    def start(self, gi, after):
        shards = [self.shard2d[n].astype(BF16) for n in GATHER_GROUPS[gi]]
        if after is not None:
            shards, _ = lax.optimization_barrier((shards, after))
        gathered = gather_two_level("gather_weights_%d" % gi, shards, gi + 1)
        self.raw.update(zip(GATHER_GROUPS[gi], gathered))

    def get(self, name, after):
        if name not in self.ready:
            g = self.raw[name]
            if after is not None:
                g, _ = lax.optimization_barrier((g, after))
            self.ready[name] = _from_slots(g, 0) if BIG_AXIS[name] == 0 else g
        return self.ready[name]


class _GradReducer:
    def __init__(self, q, c, update):
        self.q, self.c, self.update = q, c, update
        self.pending, self.updated = {}, {}

    def launch(self, gi, grads, nxt):
        names = REDUCE_GROUPS[gi]
        slots = [grads[n] if grads[n].ndim == 3 else _to_slots(grads[n], REDUCE_AXIS[n]) for n in names]
        rows = [s.shape[1] for s in slots]
        sent = [half_call("send_half_" + n, s, 1 - self.c, None, BF16) for n, s in zip(names, slots)]
        got = core_swap("swap_halves_%d" % gi, sent)
        parts = [half_call("chip_sum_" + n, s, self.c, g, BF16) for n, s, g in zip(names, slots, got)]
        parts, nxt = lax.optimization_barrier((parts, nxt))
        slots = scatter_slots("scatter_grads_%d" % gi, parts, len(GATHER_GROUPS) + 1 + gi)
        self.pending[gi] = (slots, rows)
        return self.finish(gi - 1, nxt) if gi > 0 else nxt

    def finish(self, gi, nxt):
        names = REDUCE_GROUPS[gi]
        slots, rows = self.pending[gi]
        halves = []
        for n, s in zip(names, slots):
            halves.append(sum_slots("sum_" + n, s))
        others = core_swap("swap_reduced_%d" % gi, halves)
        lo = [jnp.where(self.c == 0, mine, other) for mine, other in zip(halves, others)]
        hi = [jnp.where(self.c == 0, other, mine) for mine, other in zip(halves, others)]
        results = [self.update(n, _join_halves(l, h, r)) for n, l, h, r in zip(names, lo, hi, rows)]
        if nxt is not None:
            results, nxt = lax.optimization_barrier((results, nxt))
        self.updated.update(zip(names, results))
        return nxt


def _step(a):
    x, mem, tgt = a["x"][0], a["mem"][0], a["loss_target"][0]
    q = 2 * lax.axis_index("x") + lax.axis_index("y")

    shard2d = {n: _shard_view(n, a[n]) for n in BIG}
    small_sh = {n: _as2d(a[n][0]) for n in SMALL_SHARDED}
    c = lax.axis_index("c")
    full = {}
    big = _GatheredWeights(shard2d, q, c)
    gathered = gather_shards("gather_small", [small_sh[n] for n in SMALL_SHARDED])
    for n, g in zip(SMALL_SHARDED, gathered):
        full[n] = _from_slots(g, 1)

    wt = {n: (a[n] if a[n].ndim <= 2 else a[n][0]) for n in WEIGHTS if n not in BIG and n not in SMALL_SHARDED}
    for n in SMALL_SHARDED:
        wt[n] = small_sh[n]
    shards = dict(shard2d)
    shards.update({n: small_sh[n] for n in REDUCED if n not in BIG})

    def update(n, gsum):
        return adamw("adamw_" + n, shards[n], _shard_view(n, a["m_" + n]), _shard_view(n, a["v_" + n]), gsum)

    reducer = _GradReducer(q, c, update)
    loss_tile, grad_x, grads = _local_grads(x, mem, tgt, wt, full, big, reducer)
    reducer.finish(len(REDUCE_GROUPS) - 1, None)
    out = {}
    for n, vals in reducer.updated.items():
        for key, val in zip(("grad_", "delta_", "new_m_", "new_v_"), vals):
            out[key + n] = (val.T if n in TRANSPOSED else val).reshape(a[n].shape)

    small = [n for n in WEIGHTS if n not in REDUCED]
    red = _unpack(all_reduce_small("all_reduce_small", _pack([grads[n] for n in small])), [grads[n].shape for n in small])
    g_loc = {}
    for n, g in zip(small, red):
        if n in SMALL_SHARDED:
            cols = g.shape[1] // 4
            g = lax.dynamic_slice_in_dim(g, q * cols, cols, axis=1)
        g_loc[n] = g.reshape(a[n].shape)
    res = adamw("adamw_small", *[_pack([src[n] for n in small]) for src in
                                 ({n: a[n] for n in small}, {n: a["m_" + n] for n in small}, {n: a["v_" + n] for n in small})],
                _pack([g_loc[n] for n in small]))
    shapes = [a[n].shape for n in small]
    for key, packed in zip(("grad_", "delta_", "new_m_", "new_v_"), res):
        for n, val in zip(small, _unpack(packed, shapes)):
            out[key + n] = val

    loss = lax.psum(loss_tile[0, 0], ("x", "y", "c"))
    ordered = [loss, grad_x.reshape(a["x"].shape)]
    for key in ("grad_", "delta_", "new_m_", "new_v_"):
        ordered += [out[key + n] for n in WEIGHTS]
    return tuple(ordered)


def kernel(x, mem, norm_mix_g, w_in, ssd_conv_w, ssd_conv_b, ssd_dt_bias, ssd_a_log, ssd_d, ssd_norm_g, rwkv_mu, rwkv_w0, rwkv_w2, rwkv_a0, rwkv_a2, rwkv_g2, rwkv_k_k, rwkv_k_a, rwkv_r_k, rwkv_ln_w, rwkv_ln_b, w_out, norm_x_g, norm_mem_g, xattn_wq, xattn_wk, xattn_wv, xattn_wo, norm_ffn_g, ffn_w1, ffn_w2, final_norm_g, loss_target, m_norm_mix_g, m_w_in, m_ssd_conv_w, m_ssd_conv_b, m_ssd_dt_bias, m_ssd_a_log, m_ssd_d, m_ssd_norm_g, m_rwkv_mu, m_rwkv_w0, m_rwkv_w2, m_rwkv_a0, m_rwkv_a2, m_rwkv_g2, m_rwkv_k_k, m_rwkv_k_a, m_rwkv_r_k, m_rwkv_ln_w, m_rwkv_ln_b, m_w_out, m_norm_x_g, m_norm_mem_g, m_xattn_wq, m_xattn_wk, m_xattn_wv, m_xattn_wo, m_norm_ffn_g, m_ffn_w1, m_ffn_w2, m_final_norm_g, v_norm_mix_g, v_w_in, v_ssd_conv_w, v_ssd_conv_b, v_ssd_dt_bias, v_ssd_a_log, v_ssd_d, v_ssd_norm_g, v_rwkv_mu, v_rwkv_w0, v_rwkv_w2, v_rwkv_a0, v_rwkv_a2, v_rwkv_g2, v_rwkv_k_k, v_rwkv_k_a, v_rwkv_r_k, v_rwkv_ln_w, v_rwkv_ln_b, v_w_out, v_norm_x_g, v_norm_mem_g, v_xattn_wq, v_xattn_wk, v_xattn_wv, v_xattn_wo, v_norm_ffn_g, v_ffn_w1, v_ffn_w2, v_final_norm_g):
    return _step(dict(locals()))
```

```python
import functools
import math

import jax
import jax.numpy as jnp
from jax import lax
from jax.experimental import pallas as pl
from jax.experimental.pallas import tpu as pltpu
from jax.experimental.pallas import tpu_sc as plsc

F32 = jnp.float32
BF16 = jnp.bfloat16
HIGHEST = lax.Precision.HIGHEST
MESH_ID = pl.DeviceIdType.MESH

NORM_EPS = 1e-6
RWKV_LN_EPS = 64e-5
HEAD_DIM = 64
PAIR = 2 * HEAD_DIM
LANES = 128
SSD_STATE = 128
SSD_CHUNK = 128
SSD_GROUPS = 2
SSD_CONV = 4
RWKV_CHUNK = 64
HALO = 8
ROW_TILE = 128
PAIRS_PER_STEP = 4
XATTN_HEADS = 4
RWKV_PASSES = 1
VMEM_LIMIT = 56 * 1024 * 1024
MATMUL_VMEM = 40 * 1024 * 1024

ADAM_LR = 0.001
ADAM_B1 = 0.9
ADAM_B2 = 0.999
ADAM_EPS = 1e-08
ADAM_WD = 0.01
ADAM_STEP = 10


def _dims(ca, cb):
    return (((ca,), (cb,)), ((), ()))


def _split_bf16(a):
    hi = a.astype(BF16)
    lo = (a - hi.astype(F32)).astype(BF16)
    return hi, lo


def _mm_impl(a, b, ca, cb, passes):
    dn = _dims(ca, cb)
    if passes == 1:
        return lax.dot_general(a.astype(BF16), b.astype(BF16), dn, preferred_element_type=F32)
    ah, al = _split_bf16(a)
    bh, bl = _split_bf16(b)
    out = lax.dot_general(ah, bh, dn, preferred_element_type=F32)
    out = out + lax.dot_general(ah, bl, dn, preferred_element_type=F32)
    return out + lax.dot_general(al, bh, dn, preferred_element_type=F32)


@functools.partial(jax.custom_vjp, nondiff_argnums=(2, 3, 4))
def mm(a, b, ca, cb, passes):
    return _mm_impl(a, b, ca, cb, passes)


def _mm_fwd(a, b, ca, cb, passes):
    return _mm_impl(a, b, ca, cb, passes), (a, b)


def _mm_bwd(ca, cb, passes, res, g):
    a, b = res
    da = mm(g, b, 1, 1 - cb, passes) if ca == 1 else mm(b, g, 1 - cb, 1, passes)
    db = mm(a, g, 1 - ca, 0, passes) if cb == 0 else mm(g, a, 0, 1 - ca, passes)
    return da, db


mm.defvjp(_mm_fwd, _mm_bwd)


def _dot_exact(a, b):
    return lax.dot_general(a, b, _dims(1, 0), precision=HIGHEST, preferred_element_type=F32)


def _iota(shape, dim):
    return lax.broadcasted_iota(jnp.int32, shape, dim)


def _sigmoid(x):
    return 1.0 / (1.0 + jnp.exp(-x))


def _silu(x):
    return x * _sigmoid(x)


def _softplus(x):
    return jnp.maximum(x, 0.0) + jnp.log(1.0 + jnp.exp(-jnp.abs(x)))


def _rms(x, g):
    return x * lax.rsqrt(jnp.mean(x * x, axis=-1, keepdims=True) + NORM_EPS) * g


def _select_mm(x, sel):
    hi = x.astype(BF16)
    r1 = x - hi.astype(F32)
    mid = r1.astype(BF16)
    lo = (r1 - mid.astype(F32)).astype(BF16)
    dn = _dims(1, 0)
    out = lax.dot_general(hi, sel, dn, preferred_element_type=F32)
    out = out + lax.dot_general(mid, sel, dn, preferred_element_type=F32)
    return out + lax.dot_general(lo, sel, dn, preferred_element_type=F32)


def _head_sum_impl(x, n):
    sel = (_iota((n, LANES), 0) // HEAD_DIM == _iota((n, LANES), 1)).astype(BF16)
    return _select_mm(x, sel)


def _head_expand_impl(s, n):
    sel = (_iota((LANES, n), 1) // HEAD_DIM == _iota((LANES, n), 0)).astype(BF16)
    return _select_mm(s, sel)


@functools.partial(jax.custom_vjp, nondiff_argnums=(1,))
def _head_sum_n(x, n):
    return _head_sum_impl(x, n)


@functools.partial(jax.custom_vjp, nondiff_argnums=(1,))
def _head_expand(s, n):
    return _head_expand_impl(s, n)


_head_sum_n.defvjp(lambda x, n: (_head_sum_impl(x, n), None), lambda n, _, g: (_head_expand(g, n),))
_head_expand.defvjp(lambda s, n: (_head_expand_impl(s, n), None), lambda n, _, g: (_head_sum_n(g, n),))


def _head_sum(x):
    return _head_sum_n(x, x.shape[1])


def _row_vector_expand(v, n):
    v8 = jnp.broadcast_to(v, (8, LANES))
    return jnp.sum(_head_expand(v8, n), axis=0, keepdims=True) * 0.125


def _shift_rows_impl(u, halo, s):
    rolled = pltpu.roll(u, s, 0)
    top = jnp.where(_iota((HALO, 1), 0) < s, pltpu.roll(halo, s, 0), rolled[:HALO])
    return jnp.concatenate([top, rolled[HALO:]], axis=0)


@functools.partial(jax.custom_vjp, nondiff_argnums=(2,))
def _shift_rows(u, halo, s):
    return _shift_rows_impl(u, halo, s)


def _shift_rows_bwd(s, _, g):
    tr = g.shape[0]
    rolled = pltpu.roll(g, tr - s, 0)
    hrow = _iota((HALO, 1), 0)
    bottom = jnp.where(hrow < HALO - s, rolled[tr - HALO:], 0.0)
    dhalo = jnp.where(hrow >= HALO - s, pltpu.roll(g[:HALO], HALO - s, 0), 0.0)
    return jnp.concatenate([rolled[:tr - HALO], bottom], axis=0), dhalo


_shift_rows.defvjp(lambda u, halo, s: (_shift_rows_impl(u, halo, s), None), _shift_rows_bwd)


def _params(sem):
    return pltpu.CompilerParams(dimension_semantics=sem, vmem_limit_bytes=VMEM_LIMIT)


def row_call(name, body, n_tiles, tiled, full, out_tiled, out_acc):
    nt, nf, no, na = len(tiled), len(full), len(out_tiled), len(out_acc)

    def kern(*refs):
        tv = [r[...] for r in refs[:nt]]
        fv = [r[...] for r in refs[nt:nt + nf]]
        outs, accs = body(tv, fv)
        for r, v in zip(refs[nt + nf:nt + nf + no], outs):
            r[...] = v.astype(r.dtype)
        if na:
            a_refs = refs[nt + nf + no:]
            first = pl.program_id(0) == 0

            @pl.when(first)
            def _():
                for r, v in zip(a_refs, accs):
                    r[...] = v

            @pl.when(jnp.logical_not(first))
            def _():
                for r, v in zip(a_refs, accs):
                    r[...] += v

    in_specs = [pl.BlockSpec((rt, w), functools.partial(lambda i, cb: (i, cb), cb=cb)) for (_, rt, w, cb) in tiled]
    in_specs += [pl.BlockSpec(a.shape, lambda i: (0, 0)) for a in full]
    out_specs = [pl.BlockSpec((rt, w), lambda i: (i, 0)) for (_, rt, w, _) in out_tiled]
    out_specs += [pl.BlockSpec(s, lambda i: (0, 0)) for s in out_acc]
    out_shape = [jax.ShapeDtypeStruct((rows, w), dt) for (rows, _, w, dt) in out_tiled]
    out_shape += [jax.ShapeDtypeStruct(s, F32) for s in out_acc]
    res = pl.pallas_call(
        kern, name=name, grid=(n_tiles,), in_specs=in_specs, out_specs=out_specs, out_shape=out_shape,
        compiler_params=_params(("arbitrary",)),
    )(*[t[0] for t in tiled], *full)
    return list(res[:no]), list(res[no:])


def _pick(dim, cands):
    for c in cands:
        if dim % c == 0:
            return c
    return dim


def matmul(name, a, b, tb=False, resid=None, out_dtype=F32, out_slots=1, ta=False):
    (k, m) = a.shape if ta else a.shape[::-1]
    b_slots = b.shape[0] if b.ndim == 3 else 1
    n = b.shape[-2] if tb else b.shape[-1] * b_slots
    has_resid = resid is not None
    out_bytes = jnp.dtype(out_dtype).itemsize
    sizes = (2048, 1024, 896, 768, 512, 384, 256, 128)
    tm = _pick(m, sizes[1:])
    tn = _pick(n // max(out_slots, 1 if tb else b_slots), sizes[1:])

    def vmem_bytes(tk):
        return 2 * 2 * tk * (tm + tn) + tm * tn * (2 * out_bytes + 4 + (8 if has_resid else 0))

    k_slot = k // b_slots if tb else k
    tk = next((c for c in sizes if k_slot % c == 0 and vmem_bytes(c) <= MATMUL_VMEM), LANES)
    nk = k // tk
    n_per = n // (b_slots if not tb else 1) // tn
    k_per = k_slot // tk
    o_per = n // out_slots // tn

    def kern(*refs):
        a_ref, b_ref = refs[0], refs[1]
        o_ref, acc = refs[-2], refs[-1]
        kk = pl.program_id(2)
        part = lax.dot_general(a_ref[...], b_ref[...], _dims(0 if ta else 1, 1 if tb else 0), preferred_element_type=F32)

        def finish(out):
            if has_resid:
                out = out + refs[2][...]
            o_ref[...] = out.astype(o_ref.dtype)

        if nk == 1:
            finish(part)
            return

        @pl.when(kk == 0)
        def _():
            acc[...] = part

        @pl.when(jnp.logical_and(kk > 0, kk < nk - 1))
        def _():
            acc[...] += part

        @pl.when(kk == nk - 1)
        def _():
            finish(acc[...] + part)

    in_specs = [pl.BlockSpec((tk, tm), lambda i, j, kk: (kk, i)) if ta else pl.BlockSpec((tm, tk), lambda i, j, kk: (i, kk))]
    if b.ndim == 3 and tb:
        in_specs.append(pl.BlockSpec((None, tn, tk), lambda i, j, kk: (kk // k_per, j, kk % k_per)))
    elif b.ndim == 3:
        in_specs.append(pl.BlockSpec((None, tk, tn), lambda i, j, kk: (j // n_per, kk, j % n_per)))
    elif tb:
        in_specs.append(pl.BlockSpec((tn, tk), lambda i, j, kk: (j, kk)))
    else:
        in_specs.append(pl.BlockSpec((tk, tn), lambda i, j, kk: (kk, j)))
    args = [a, b]
    if has_resid:
        in_specs.append(pl.BlockSpec((tm, tn), lambda i, j, kk: (i, j)))
        args.append(resid)
    if out_slots > 1:
        out_spec = pl.BlockSpec((None, tm, tn), lambda i, j, kk: (j // o_per, i, j % o_per))
        out_shape = jax.ShapeDtypeStruct((out_slots, m, n // out_slots), out_dtype)
    else:
        out_spec = pl.BlockSpec((tm, tn), lambda i, j, kk: (i, j))
        out_shape = jax.ShapeDtypeStruct((m, n), out_dtype)
    return pl.pallas_call(
        kern, name=name, grid=(m // tm, n // tn, nk), in_specs=in_specs,
        out_specs=out_spec, out_shape=out_shape,
        scratch_shapes=[pltpu.VMEM((tm, tn), F32)],
        compiler_params=_params(("parallel", "parallel", "arbitrary")),
    )(*args)


def norm_fwd(name, x, g, tr):
    def body(tv, fv):
        return [_rms(tv[0], fv[0])], []
    rows, d = x.shape
    (h,), _ = row_call(name, body, rows // tr, [(x, tr, d, 0)], [g], [(rows, tr, d, BF16)], [])
    return h


def norm_bwd(name, x, g, dh, extra, tr):
    def body(tv, fv):
        _, vjp = jax.vjp(_rms, tv[0], fv[0])
        dx, dg = vjp(tv[1])
        if extra is not None:
            dx = dx + tv[2]
        return [dx, dx], [dg]
    rows, d = x.shape
    tiled = [(x, tr, d, 0), (dh, tr, d, 0)] + ([(extra, tr, d, 0)] if extra is not None else [])
    (dx, dxb), (dg,) = row_call(name, body, rows // tr, tiled, [g], [(rows, tr, d, F32), (rows, tr, d, BF16)], [g.shape])
    return dx, dxb, dg


def _ssd_pre(xbc, halo, dtraw, w0, w1, w2, w3, cb, dtb):
    y = w3 * xbc + w2 * _shift_rows(xbc, halo, 1) + w1 * _shift_rows(xbc, halo, 2) + w0 * _shift_rows(xbc, halo, 3) + cb
    return _silu(y), _softplus(dtraw + dtb)


def _ssd_post(ys, xs, z, dskip, ng):
    w = ys.shape[1]
    y = (ys + xs * _row_vector_expand(dskip, w)) * _silu(z)
    gw = w // SSD_GROUPS
    parts = []
    for gi in range(SSD_GROUPS):
        yg = y[:, gi * gw:(gi + 1) * gw]
        parts.append(yg * lax.rsqrt(jnp.mean(yg * yg, axis=-1, keepdims=True) + NORM_EPS))
    return jnp.concatenate(parts, axis=1) * ng


def _rwkv_pre(urkv, ulora, hrkv, hlora, mu_rkv, mu_lora, w0, a0, kkw, kaw, w2p, a2p, g2):
    w = w0.shape[1]
    urkv = urkv + (_shift_rows(urkv, hrkv, 1) - urkv) * mu_rkv
    ulora = ulora + (_shift_rows(ulora, hlora, 1) - ulora) * mu_lora
    r, k, v = urkv[:, :w], urkv[:, w:2 * w], urkv[:, 2 * w:]
    pw, pa, pg = ulora[:, :LANES], ulora[:, LANES:2 * LANES], ulora[:, 2 * LANES:]
    w_log = -_softplus(-(w0 + mm(jnp.tanh(pw), w2p, 1, 0, 1))) - 0.5
    lw = -jnp.exp(w_log)
    iclr = _sigmoid(a0 + mm(pa, a2p, 1, 0, 1))
    gate = mm(_sigmoid(pg), g2, 1, 0, 1)
    kk = k * kkw
    kk = kk / jnp.maximum(jnp.sqrt(_head_expand(_head_sum(kk * kk), w)), 1e-12)
    k2 = k * (1.0 + (iclr - 1.0) * kaw)
    return r, lw, k2, v, -kk, kk * iclr, gate


def _rwkv_post(ys, r, k2, v, gate, rk, lnw, lnb):
    w = ys.shape[1]
    inv = 1.0 / HEAD_DIM
    mean = _head_expand(_head_sum(ys), w) * inv
    d = ys - mean
    var = _head_expand(_head_sum(d * d), w) * inv
    yn = d * lax.rsqrt(var + RWKV_LN_EPS) * lnw + lnb
    bonus = _head_expand(_head_sum(r * k2 * rk), w) * v
    return (yn + bonus) * gate


def _attn(q, k, v):
    d = q.shape[1]
    hd = d // XATTN_HEADS
    outs = []
    for h in range(XATTN_HEADS):
        sl = slice(h * hd, (h + 1) * hd)
        s = mm(q[:, sl], k[:, sl], 1, 1, 1) * (hd ** -0.5)
        s = s - jnp.max(s, axis=-1, keepdims=True)
        p = jnp.exp(s)
        p = p / jnp.sum(p, axis=-1, keepdims=True)
        outs.append(mm(p, v[:, sl], 1, 0, 1))
    return jnp.concatenate(outs, axis=1)


def _relu2(a):
    return jnp.square(jnp.maximum(a.astype(F32), 0.0))


def fn_fwd(name, fn, n_tiles, tiled, full, out_tiled):
    def body(tv, fv):
        outs = fn(*tv, *fv)
        return (list(outs) if isinstance(outs, (tuple, list)) else [outs]), []
    outs, _ = row_call(name, body, n_tiles, tiled, full, out_tiled, [])
    return outs


def fn_bwd(name, fn, n_tiles, tiled, full, cts, ct_fn, out_tiled):
    nt = len(tiled)

    def body(tv, fv):
        outs, vjp = jax.vjp(fn, *tv[:nt], *fv)
        ct = ct_fn(tv[nt:])
        grads = vjp(tuple(ct) if isinstance(outs, (tuple, list)) else ct[0])
        return list(grads[:nt]), list(grads[nt:])
    return row_call(name, body, n_tiles, tiled + cts, full, out_tiled, [f.shape for f in full])


def _ssd_chunk(xs, bm, cm, dt_all, a_log, ht, p):
    q = xs.shape[0]
    lane = _iota((1, LANES), 1)
    row = _iota((q, 1), 0)
    tril = _iota((q, q), 0) >= _iota((q, q), 1)
    half = lane < HEAD_DIM
    da = dt_all * (-jnp.exp(a_log))
    cs = _dot_exact(tril.astype(F32), da)

    def col(mat, h):
        return jnp.sum(jnp.where(lane == h, mat, 0.0), axis=1, keepdims=True)

    cs0, cs1 = col(cs, 2 * p), col(cs, 2 * p + 1)
    xdt = xs * jnp.where(half, col(dt_all, 2 * p), col(dt_all, 2 * p + 1))
    csx = jnp.where(half, cs0, cs1)
    last = jnp.sum(jnp.where(row == q - 1, csx, 0.0), axis=0, keepdims=True)
    cb = mm(cm, bm, 1, 1, 1)
    y = mm(cm, ht, 1, 0, 1) * jnp.exp(csx)
    for csh, hm in ((cs0, half), (cs1, jnp.logical_not(half))):
        csl = jnp.broadcast_to(csh, (q, q))
        seg = csl - csl.T
        lmat = jnp.where(tril, jnp.exp(jnp.where(tril, seg, 0.0)), 0.0)
        y = y + jnp.where(hm, mm(cb * lmat, xdt, 1, 0, 1), 0.0)
    st = mm(bm, xdt * jnp.exp(last - csx), 0, 0, 1)
    return y, ht * jnp.exp(last) + st


def _unit_lower_inverses_impl(mats):
    c = mats[0].shape[0]
    eye = (_iota((c, c), 0) == _iota((c, c), 1)).astype(F32)
    tm = [eye + a_ for a_ in mats]
    pm = mats
    for _ in range(int(math.log2(c)) - 1):
        pm = [mm(p_, p_, 1, 0, RWKV_PASSES) for p_ in pm]
        tm = [t_ + mm(t_, p_, 1, 0, RWKV_PASSES) for t_, p_ in zip(tm, pm)]
    return tm


@jax.custom_vjp
def _unit_lower_inverses(mats):
    return _unit_lower_inverses_impl(mats)


def _unit_lower_inverses_fwd(mats):
    tm = _unit_lower_inverses_impl(mats)
    return tm, tm


def _unit_lower_inverses_bwd(tm, g):
    left = [mm(t_, g_, 0, 0, RWKV_PASSES) for t_, g_ in zip(tm, g)]
    return ([mm(l_, t_, 1, 1, RWKV_PASSES) for l_, t_ in zip(left, tm)],)


_unit_lower_inverses.defvjp(_unit_lower_inverses_fwd, _unit_lower_inverses_bwd)


def _rwkv_chunks(pairs):
    c = pairs[0][0].shape[0]
    ps = RWKV_PASSES
    lane = _iota((1, LANES), 1)
    row = _iota((c, 1), 0)
    ri, ci = _iota((c, c), 0), _iota((c, c), 1)
    tril_i, tril_s = ri >= ci, ri > ci
    half = lane < HEAD_DIM
    halves = (half, jnp.logical_not(half))
    bd = (_iota((LANES, LANES), 0) < HEAD_DIM) == (_iota((LANES, LANES), 1) < HEAD_DIM)
    tri = tril_i.astype(F32)
    n = len(pairs)
    heads = [(j, hm) for j in range(n) for hm in halves]

    cum = [_dot_exact(tri, p[1]) for p in pairs]
    at = [p[4] * jnp.exp(cm - p[1]) for p, cm in zip(pairs, cum)]
    en = [jnp.exp(-cm) for cm in cum]
    bt = [p[5] * e for p, e in zip(pairs, en)]
    kt = [p[2] * e for p, e in zip(pairs, en)]
    rt = [p[0] * jnp.exp(cm) for p, cm in zip(pairs, cum)]
    ah = [mm(at[j], pairs[j][6], 1, 1, ps) for j in range(n)]
    y = [mm(rt[j], pairs[j][6], 1, 1, ps) for j in range(n)]
    atm = [jnp.where(hm, at[j], 0.0) for j, hm in heads]
    rtm = [jnp.where(hm, rt[j], 0.0) for j, hm in heads]
    aab = [jnp.where(tril_s, mm(atm[i], bt[j], 1, 1, ps), 0.0) for i, (j, _) in enumerate(heads)]
    aak = [jnp.where(tril_s, mm(atm[i], kt[j], 1, 1, ps), 0.0) for i, (j, _) in enumerate(heads)]
    arb = [jnp.where(tril_i, mm(rtm[i], bt[j], 1, 1, ps), 0.0) for i, (j, _) in enumerate(heads)]
    ark = [jnp.where(tril_i, mm(rtm[i], kt[j], 1, 1, ps), 0.0) for i, (j, _) in enumerate(heads)]
    rhs = [ah[j] + mm(aak[i], pairs[j][3], 1, 0, ps) for i, (j, _) in enumerate(heads)]
    yv = [mm(ark[i], pairs[j][3], 1, 0, ps) for i, (j, _) in enumerate(heads)]
    tm = _unit_lower_inverses(aab)
    uh =[mm(tm[i], rhs[i], 1, 0, ps) for i in range(len(heads))]
    u = [jnp.where(half, uh[2 * j], uh[2 * j + 1]) for j in range(n)]
    yu = [mm(arb[i], u[j], 1, 0, ps) for i, (j, _) in enumerate(heads)]
    out = []
    for j in range(n):
        yj = y[j] + jnp.where(half, yu[2 * j] + yv[2 * j], yu[2 * j + 1] + yv[2 * j + 1])
        plast = jnp.sum(jnp.where(row == c - 1, cum[j], 0.0), axis=0, keepdims=True)
        upd = pairs[j][6] + mm(u[j], bt[j], 0, 0, ps) + mm(pairs[j][3], kt[j], 0, 0, ps)
        out.append((yj, jnp.where(bd, upd * jnp.exp(plast), 0.0)))
    return out


def _seq_spec(chunk, ppb, col, row_of):
    if col is None:
        return pl.BlockSpec((chunk, ppb * LANES), lambda pb, i: (row_of(i), pb))
    return pl.BlockSpec((chunk, LANES), lambda pb, i: (row_of(i), col(pb * ppb)))


def _pair_vals(refs, seq_in, j):
    return [r[...] if col is not None else r[:, j * LANES:(j + 1) * LANES] for r, (_, col) in zip(refs, seq_in)]


def scan_fwd(name, chunk_fn, chunk, seq_in, const_in, n_pairs, ppb):
    t = seq_in[0][0].shape[0]
    nc = t // chunk
    ns, ncst = len(seq_in), len(const_in)

    def kern(*refs):
        y_ref, st_ref, ht = refs[ns + ncst], refs[ns + ncst + 1], refs[ns + ncst + 2]

        @pl.when(pl.program_id(1) == 0)
        def _():
            ht[...] = jnp.zeros_like(ht)

        cv = [r[...] for r in refs[ns:ns + ncst]]
        h0 = [ht[j] for j in range(ppb)]
        for j in range(ppb):
            st_ref[j] = h0[j]
        sv = [_pair_vals(refs[:ns], seq_in, j) for j in range(ppb)]
        outs = chunk_fn(sv, cv, h0, [pl.program_id(0) * ppb + j for j in range(ppb)])
        for j, (y, hn) in enumerate(outs):
            y_ref[:, j * LANES:(j + 1) * LANES] = y
            ht[j] = hn

    in_specs = [_seq_spec(chunk, ppb, col, lambda i: i) for (_, col) in seq_in]
    in_specs += [pl.BlockSpec(a.shape, lambda pb, i: (0, 0)) for a in const_in]
    return pl.pallas_call(
        kern, name=name, grid=(n_pairs // ppb, nc), in_specs=in_specs,
        out_specs=[pl.BlockSpec((chunk, ppb * LANES), lambda pb, i: (i, pb)),
                   pl.BlockSpec((ppb, None, LANES, LANES), lambda pb, i: (pb, i, 0, 0))],
        out_shape=[jax.ShapeDtypeStruct((t, n_pairs * LANES), F32), jax.ShapeDtypeStruct((n_pairs, nc, LANES, LANES), F32)],
        scratch_shapes=[pltpu.VMEM((ppb, LANES, LANES), F32)],
        compiler_params=_params(("arbitrary", "arbitrary")),
    )(*[s[0] for s in seq_in], *const_in)


def scan_bwd(name, chunk_fn, chunk, seq_in, const_in, states, dy, n_pairs, ppb):
    t = dy.shape[0]
    nc = t // chunk
    ns, ncst = len(seq_in), len(const_in)

    def kern(*refs):
        seq_refs, cst_refs = refs[:ns], refs[ns:ns + ncst]
        st_ref, dy_ref = refs[ns + ncst], refs[ns + ncst + 1]
        o = ns + ncst + 2
        dseq_refs, dcst_refs, dht = refs[o:o + ns], refs[o + ns:o + ns + ncst], refs[o + ns + ncst]
        pb, i = pl.program_id(0), pl.program_id(1)

        @pl.when(i == 0)
        def _():
            dht[...] = jnp.zeros_like(dht)

        ids = [pb * ppb + j for j in range(ppb)]
        lanes = [slice(j * LANES, (j + 1) * LANES) for j in range(ppb)]

        def fn(*flat):
            sv = [list(flat[j * ns:(j + 1) * ns]) for j in range(ppb)]
            outs = chunk_fn(sv, list(flat[ppb * ns:ppb * ns + ncst]), list(flat[ppb * ns + ncst:]), ids)
            return tuple(y for y, _ in outs), tuple(h for _, h in outs)

        flat_in = [v for j in range(ppb) for v in _pair_vals(seq_refs, seq_in, j)]
        flat_in += [r[...] for r in cst_refs] + [st_ref[j] for j in range(ppb)]
        _, vjp = jax.vjp(fn, *flat_in)
        grads = vjp((tuple(dy_ref[:, ln] for ln in lanes), tuple(dht[j] for j in range(ppb))))
        for j in range(ppb):
            for r, g in zip(dseq_refs, grads[j * ns:(j + 1) * ns]):
                r[:, lanes[j]] = g
            dht[j] = grads[ppb * ns + ncst + j]
        dcv = grads[ppb * ns:ppb * ns + ncst]
        if ncst:
            first = jnp.logical_and(pb == 0, i == 0)

            @pl.when(first)
            def _():
                for r, g in zip(dcst_refs, dcv):
                    r[...] = g

            @pl.when(jnp.logical_not(first))
            def _():
                for r, g in zip(dcst_refs, dcv):
                    r[...] += g

    rev = lambda i: nc - 1 - i
    wide = pl.BlockSpec((chunk, ppb * LANES), lambda pb, i: (rev(i), pb))
    in_specs = [_seq_spec(chunk, ppb, col, rev) for (_, col) in seq_in]
    in_specs += [pl.BlockSpec(a.shape, lambda pb, i: (0, 0)) for a in const_in]
    in_specs += [pl.BlockSpec((ppb, None, LANES, LANES), lambda pb, i: (pb, rev(i), 0, 0)), wide]
    out_specs = [wide for _ in seq_in]
    out_specs += [pl.BlockSpec(a.shape, lambda pb, i: (0, 0)) for a in const_in]
    out_shape = [jax.ShapeDtypeStruct((t, n_pairs * LANES), F32) for _ in seq_in]
    out_shape += [jax.ShapeDtypeStruct(a.shape, F32) for a in const_in]
    res = pl.pallas_call(
        kern, name=name, grid=(n_pairs // ppb, nc), in_specs=in_specs, out_specs=out_specs, out_shape=out_shape,
        scratch_shapes=[pltpu.VMEM((ppb, LANES, LANES), F32)],
        compiler_params=_params(("arbitrary", "arbitrary")),
    )(*[s[0] for s in seq_in], *const_in, states, dy)
    return list(res[:ns]), list(res[ns:])


def loss_head(x3, tgt, g, tr):
    rows, d = x3.shape

    def body(tv, fv):
        def f(x, gg):
            e = jnp.square(_rms(x, gg) - tv[1])
            return 0.5 * jnp.sum(jnp.mean(e, axis=-1, keepdims=True), axis=0, keepdims=True)
        l, vjp = jax.vjp(f, tv[0], fv[0])
        dx, dg = vjp(jnp.ones((1, 1), F32))
        return [dx, dx], [dg, jnp.broadcast_to(l, (8, LANES))]
    (dx, dxb), (dg, l) = row_call("loss_head", body, rows // tr, [(x3, tr, d, 0), (tgt, tr, d, 0)], [g],
                                  [(rows, tr, d, F32), (rows, tr, d, BF16)], [g.shape, (8, LANES)])
    return dx, dxb, dg, l


def _adam_math(w, g, m, v):
    m = ADAM_B1 * m + (1.0 - ADAM_B1) * g
    v = ADAM_B2 * v + (1.0 - ADAM_B2) * jnp.square(g)
    m_hat = m / (1.0 - ADAM_B1 ** ADAM_STEP)
    v_hat = v / (1.0 - ADAM_B2 ** ADAM_STEP)
    delta = -ADAM_LR * (m_hat / (jnp.sqrt(v_hat) + ADAM_EPS) + ADAM_WD * w)
    return delta, m, v


def _tiling(rows, cols, limit):
    row_tile = max([d for d in range(16, rows + 1, 16) if rows % d == 0 and d * cols <= limit], default=0)
    col_tile = max([ct for ct in range(LANES, cols + 1, LANES) if cols % ct == 0 and rows * ct <= limit], default=0)
    if row_tile and row_tile * cols >= rows * col_tile:
        return row_tile, cols
    return (rows, col_tile) if col_tile else (rows, cols)


def ew_call(name, fn, ins, out_dtypes, limit=1 << 20):
    rows, cols = ins[0].shape
    br, bc = _tiling(rows, cols, limit)
    spec = pl.BlockSpec((br, bc), lambda i, j: (i, j))
    n_in = len(ins)

    def kern(*refs):
        for r, v in zip(refs[n_in:], fn(*[r[...] for r in refs[:n_in]])):
            r[...] = v.astype(r.dtype)

    return pl.pallas_call(
        kern, name=name, grid=(rows // br, cols // bc), in_specs=[spec] * n_in, out_specs=[spec] * len(out_dtypes),
        out_shape=[jax.ShapeDtypeStruct((rows, cols), dt) for dt in out_dtypes],
        compiler_params=_params(("parallel", "parallel")),
    )(*ins)


def adamw(name, w, m, v, g):
    return ew_call(name, lambda wv, mv, vv, gv: (gv, *_adam_math(wv, gv, mv, vv)), [w, m, v, g], [F32] * 4, 1 << 18)


def half_call(name, s, h, extra, out_dtype):
    n_slots, rows, cols = s.shape
    by_cols = _halves_by_cols(rows)
    hr, hc = (rows, cols // 2) if by_cols else (rows // 2, cols)
    br, bc = _tiling(hr, hc, 1 << 20)
    ni, nj = hr // br, hc // bc
    if by_cols:
        s_spec = pl.BlockSpec((None, br, bc), lambda sl, i, j, href: (sl, i, href[0] * nj + j))
    else:
        s_spec = pl.BlockSpec((None, br, bc), lambda sl, i, j, href: (sl, href[0] * ni + i, j))
    flat = pl.BlockSpec((None, br, bc), lambda sl, i, j, href: (sl, i, j))
    has_extra = extra is not None

    def kern(href, s_ref, *rest):
        v = s_ref[...]
        if has_extra:
            v = v + rest[0][...].astype(F32)
        rest[-1][...] = v.astype(out_dtype)

    grid_spec = pltpu.PrefetchScalarGridSpec(
        num_scalar_prefetch=1, grid=(n_slots, ni, nj), in_specs=[s_spec] + ([flat] if has_extra else []), out_specs=flat)
    return pl.pallas_call(
        kern, name=name, grid_spec=grid_spec, out_shape=jax.ShapeDtypeStruct((n_slots, hr, hc), out_dtype),
        compiler_params=_params(("parallel", "parallel", "parallel")),
    )(jnp.reshape(h, (1,)).astype(jnp.int32), s, *([extra] if has_extra else []))


def sum_slots(name, r):
    _, rows, cols = r.shape
    br, bc = _tiling(rows, cols, 1 << 20)

    def kern(r0, r1, r2, r3, o):
        o[...] = ((r0[...].astype(F32) + r1[...].astype(F32)) + r2[...].astype(F32)) + r3[...].astype(F32)

    in_specs = [pl.BlockSpec((None, br, bc), functools.partial(lambda i, j, s: (s, i, j), s=s)) for s in range(4)]
    return pl.pallas_call(
        kern, name=name, grid=(rows // br, cols // bc), in_specs=in_specs,
        out_specs=pl.BlockSpec((br, bc), lambda i, j: (i, j)),
        out_shape=jax.ShapeDtypeStruct((rows, cols), F32), compiler_params=_params(("parallel", "parallel")),
    )(r, r, r, r)


def _my_place():
    return lax.axis_index("x"), lax.axis_index("y"), lax.axis_index("c")


def _chip_peers(x, y):
    peers = [(1 - x, y), (x, 1 - y), (1 - x, 1 - y)]
    return peers, [2 * px + py for px, py in peers]


def gather_shards(name, arrays):
    nw = len(arrays)
    ANY = pl.BlockSpec(memory_space=pl.ANY)

    def body(*refs):
        ins, outs = refs[:nw], refs[nw:2 * nw]
        send, recv, loc = refs[2 * nw:]
        x, y, c = _my_place()
        q = 2 * x + y
        peers, chips = _chip_peers(x, y)

        def remote(w, j, slot):
            return pltpu.make_async_remote_copy(
                src_ref=ins[w], dst_ref=outs[w].at[slot], send_sem=send.at[w, j], recv_sem=recv.at[w, j],
                device_id=(*peers[j], c), device_id_type=MESH_ID)

        local = [pltpu.make_async_copy(ins[w], outs[w].at[q], loc.at[w]) for w in range(nw)]
        sends = [[remote(w, j, q) for j in range(3)] for w in range(nw)]
        for w in range(nw):
            local[w].start()
            for j in range(3):
                sends[w][j].start()
        for w in range(nw):
            local[w].wait()
            for j in range(3):
                sends[w][j].wait_send()
                remote(w, j, chips[j]).wait_recv()

    return pl.pallas_call(
        body, name=name, in_specs=[ANY] * nw, out_specs=[ANY] * nw,
        out_shape=[jax.ShapeDtypeStruct((4,) + a.shape, a.dtype) for a in arrays],
        scratch_shapes=[pltpu.SemaphoreType.DMA((nw, 3)), pltpu.SemaphoreType.DMA((nw, 3)), pltpu.SemaphoreType.DMA((nw,))],
        compiler_params=pltpu.CompilerParams(has_side_effects=True),
    )(*arrays)


def scatter_slots(name, arrays, collective_id):
    nw = len(arrays)

    def body(*refs):
        ins, outs = refs[:nw], refs[nw:2 * nw]
        send, recv, loc = refs[2 * nw:]
        x, y, c = _my_place()
        q = 2 * x + y
        peers, chips = _chip_peers(x, y)
        barrier = pltpu.get_barrier_semaphore()
        for p in peers:
            pl.semaphore_signal(barrier, inc=1, device_id=(*p, c), device_id_type=MESH_ID)
        pl.semaphore_wait(barrier, 3)

        def remote(w, j, src_slot, dst_slot):
            return pltpu.make_async_remote_copy(
                src_ref=ins[w].at[src_slot], dst_ref=outs[w].at[dst_slot], send_sem=send.at[w, j], recv_sem=recv.at[w, j],
                device_id=(*peers[j], c), device_id_type=MESH_ID)

        sends = [[remote(w, j, chips[j], q) for j in range(3)] for w in range(nw)]
        own = [pltpu.make_async_copy(ins[w].at[q], outs[w].at[q], loc.at[w]) for w in range(nw)]
        for w in range(nw):
            for j in range(3):
                sends[w][j].start()
            own[w].start()
        for w in range(nw):
            for j in range(3):
                sends[w][j].wait_send()
                remote(w, j, q, chips[j]).wait_recv()
            own[w].wait()

    return pl.kernel(
        body, out_type=[jax.ShapeDtypeStruct(a.shape, a.dtype) for a in arrays],
        mesh=plsc.ScalarSubcoreMesh(axis_name="sequencer", num_cores=1), name=name,
        scratch_types=[pltpu.SemaphoreType.DMA((nw, 3)), pltpu.SemaphoreType.DMA((nw, 3)), pltpu.SemaphoreType.DMA((nw,))],
        compiler_params=pltpu.CompilerParams(collective_id=collective_id),
    )(*arrays)


def _halves_by_cols(rows):
    return rows % 32 != 0


def _half_of(ref, shape, h):
    rows, cols = shape
    if _halves_by_cols(rows):
        return ref.at[:, pl.ds(h * (cols // 2), cols // 2)]
    return ref.at[pl.ds(h * (rows // 2), rows // 2)]


def _join_halves(lo, hi, rows):
    return jnp.concatenate([lo, hi], axis=lo.ndim - 1 if _halves_by_cols(rows) else lo.ndim - 2)


def gather_two_level(name, arrays, collective_id):
    nw = len(arrays)
    shapes = [a.shape for a in arrays]

    def body(*refs):
        ins, outs = refs[:nw], refs[nw:2 * nw]
        send, recv, loc = refs[2 * nw:]
        x, y, c = _my_place()
        q = 2 * x + y
        me, sibling = (x, y, c), (x, y, 1 - c)
        peers = [(1 - x, y), (x, 1 - y), (1 - x, 1 - y)]
        chips = [2 * px + py for px, py in peers]
        barrier = pltpu.get_barrier_semaphore()
        for dev in [sibling] + [(*p, c) for p in peers]:
            pl.semaphore_signal(barrier, inc=1, device_id=dev, device_id_type=MESH_ID)
        pl.semaphore_wait(barrier, 4)

        def mine(w):
            return _half_of(ins[w], shapes[w], c)

        def landed(w, chip, half):
            return _half_of(outs[w].at[chip], shapes[w], half)

        def copy(w, k, src, chip, half, to):
            return pltpu.make_async_remote_copy(
                src_ref=src, dst_ref=landed(w, chip, half), send_sem=send.at[w, k], recv_sem=recv.at[w, k],
                device_id=to, device_id_type=MESH_ID)

        first = [[copy(w, 0, mine(w), q, c, sibling)] + [copy(w, 1 + j, mine(w), q, c, (*peers[j], c)) for j in range(3)]
                 for w in range(nw)]
        own = [pltpu.make_async_copy(mine(w), landed(w, q, c), loc.at[w]) for w in range(nw)]
        for w in range(nw):
            for cp in first[w]:
                cp.start()
            own[w].start()
        passed = []
        for w in range(nw):
            for j in range(3):
                copy(w, 1 + j, mine(w), chips[j], c, me).wait_recv()
                fwd = copy(w, 4 + j, landed(w, chips[j], c), chips[j], c, sibling)
                fwd.start()
                passed.append(fwd)
        for w in range(nw):
            copy(w, 0, mine(w), q, 1 - c, me).wait_recv()
            for j in range(3):
                copy(w, 4 + j, mine(w), chips[j], 1 - c, me).wait_recv()
        for w in range(nw):
            for cp in first[w]:
                cp.wait_send()
            own[w].wait()
        for cp in passed:
            cp.wait_send()

    out_type = [jax.ShapeDtypeStruct((4,) + a.shape, a.dtype) for a in arrays]
    return pl.kernel(
        body, out_type=out_type, mesh=plsc.ScalarSubcoreMesh(axis_name="sequencer", num_cores=1), name=name,
        scratch_types=[pltpu.SemaphoreType.DMA((nw, 7)), pltpu.SemaphoreType.DMA((nw, 7)), pltpu.SemaphoreType.DMA((nw,))],
        compiler_params=pltpu.CompilerParams(collective_id=collective_id),
    )(*arrays)


def core_swap(name, arrays):
    nw = len(arrays)
    ANY = pl.BlockSpec(memory_space=pl.ANY)

    def body(*refs):
        ins, outs = refs[:nw], refs[nw:2 * nw]
        send, recv = refs[2 * nw:]
        x, y, c = _my_place()
        copies = [pltpu.make_async_remote_copy(
            src_ref=ins[w], dst_ref=outs[w], send_sem=send.at[w], recv_sem=recv.at[w],
            device_id=(x, y, 1 - c), device_id_type=MESH_ID) for w in range(nw)]
        for cp in copies:
            cp.start()
        for cp in copies:
            cp.wait_send()
            cp.wait_recv()

    return pl.pallas_call(
        body, name=name, in_specs=[ANY] * nw, out_specs=[ANY] * nw,
        out_shape=[jax.ShapeDtypeStruct(a.shape, a.dtype) for a in arrays],
        scratch_shapes=[pltpu.SemaphoreType.DMA((nw,)), pltpu.SemaphoreType.DMA((nw,))],
        compiler_params=pltpu.CompilerParams(has_side_effects=True),
    )(*arrays)


def all_reduce_small(name, v):
    rows = v.shape[0]
    VM = pl.BlockSpec(memory_space=pltpu.VMEM)

    def body(v_ref, o_ref, buf, send, recv):
        x, y, c = _my_place()
        me = 4 * x + 2 * y + c

        def peer(kx):
            return (x ^ ((kx >> 2) & 1), y ^ ((kx >> 1) & 1), c ^ (kx & 1))

        def copy(kx, slot):
            return pltpu.make_async_remote_copy(
                src_ref=v_ref, dst_ref=buf.at[slot], send_sem=send.at[kx - 1], recv_sem=recv.at[kx - 1],
                device_id=peer(kx), device_id_type=MESH_ID)

        sends = [copy(kx, me) for kx in range(1, 8)]
        for cp in sends:
            cp.start()
        buf[me] = v_ref[...]
        for kx in range(1, 8):
            copy(kx, me ^ kx).wait_recv()
        for cp in sends:
            cp.wait_send()
        acc = buf[0]
        for d in range(1, 8):
            acc = acc + buf[d]
        o_ref[...] = acc

    return pl.pallas_call(
        body, name=name, in_specs=[VM], out_specs=VM, out_shape=jax.ShapeDtypeStruct(v.shape, F32),
        scratch_shapes=[pltpu.VMEM((8, rows, LANES), F32), pltpu.SemaphoreType.DMA((7,)), pltpu.SemaphoreType.DMA((7,))],
        compiler_params=pltpu.CompilerParams(has_side_effects=True, vmem_limit_bytes=VMEM_LIMIT),
    )(v)


def _pad_cols(a, n):
    return jnp.pad(a, ((0, 0), (0, n - a.shape[1])))


def _pad_rows(a, n):
    return jnp.pad(a, ((0, n - a.shape[0]), (0, 0)))


def _halo(u, tr):
    t, cdim = u.shape
    tails = u.reshape(t // tr, tr, cdim)[:, tr - HALO:, :]
    tails = jnp.concatenate([jnp.zeros((1, HALO, cdim), u.dtype), tails[:-1]], axis=0)
    return tails.reshape(-1, cdim)


def _unhalo(du, dhalo, tr):
    t, cdim = du.shape
    n = t // tr
    dh = dhalo.reshape(n, HALO, cdim)
    dh = jnp.concatenate([dh[1:], jnp.zeros((1, HALO, cdim), du.dtype)], axis=0)
    d3 = du.reshape(n, tr, cdim)
    d3 = jnp.concatenate([d3[:, :tr - HALO, :], d3[:, tr - HALO:, :] + dh], axis=1)
    return d3.reshape(t, cdim)


def _to_slots(g, axis):
    r, cdim = g.shape
    if axis == 0:
        return g.reshape(4, r // 4, cdim)
    return g.reshape(r, 4, cdim // 4).transpose(1, 0, 2)


def _from_slots(s, axis):
    if axis == 0:
        return s.reshape(s.shape[0] * s.shape[1], s.shape[2])
    return s.transpose(1, 0, 2).reshape(s.shape[1], 4 * s.shape[2])


BIG = ("w_in", "w_out", "xattn_wq", "xattn_wk", "xattn_wv", "xattn_wo", "ffn_w1", "ffn_w2")
TRANSPOSED = ("w_in",)
BIG_AXIS = {"w_in": 0, "w_out": 0, "xattn_wq": 0, "xattn_wk": 0, "xattn_wv": 0, "xattn_wo": 0, "ffn_w1": 1, "ffn_w2": 0}
SMALL_SHARDED = ("ssd_conv_w", "rwkv_w2", "rwkv_a2", "rwkv_g2")
GATHER_GROUPS = (("w_in",), ("w_out", "xattn_wq", "xattn_wk", "xattn_wv", "xattn_wo"), ("ffn_w1", "ffn_w2"))
REDUCE_GROUPS = (("ffn_w2", "ffn_w1"), ("xattn_wo", "xattn_wq", "xattn_wk", "xattn_wv", "w_out"),
                 ("rwkv_w2", "rwkv_a2", "rwkv_g2", "w_in"))
REDUCED = BIG + ("rwkv_w2", "rwkv_a2", "rwkv_g2")
REDUCE_AXIS = dict(BIG_AXIS, rwkv_w2=1, rwkv_a2=1, rwkv_g2=1)
WEIGHTS = ("norm_mix_g", "w_in", "ssd_conv_w", "ssd_conv_b", "ssd_dt_bias", "ssd_a_log", "ssd_d", "ssd_norm_g",
           "rwkv_mu", "rwkv_w0", "rwkv_w2", "rwkv_a0", "rwkv_a2", "rwkv_g2", "rwkv_k_k", "rwkv_k_a", "rwkv_r_k",
           "rwkv_ln_w", "rwkv_ln_b", "w_out", "norm_x_g", "norm_mem_g", "xattn_wq", "xattn_wk", "xattn_wv", "xattn_wo",
           "norm_ffn_g", "ffn_w1", "ffn_w2", "final_norm_g")


def _local_grads(x, mem, tgt, wt, full, big, reducer):
    t, d = x.shape
    w = d // 2
    nh = w // HEAD_DIM
    n_pairs = nh // 2
    ppg = n_pairs // SSD_GROUPS
    bc = SSD_GROUPS * SSD_STATE
    conv_dim = w + 2 * bc
    tr = ROW_TILE
    nt = t // tr
    tr2 = 2 * tr if t % (2 * tr) == 0 else tr
    nt2 = t // tr2
    dr = wt["rwkv_w2"].shape[0]
    ar = wt["rwkv_a2"].shape[0]
    gr = wt["rwkv_g2"].shape[0]

    big.start(0, None)
    big.start(1, None)
    h1 = norm_fwd("norm_mix", x, wt["norm_mix_g"], tr2)
    w_in_t = big.get("w_in", (h1, full))
    o = 0
    segs = {}
    for nm, width in (("z", w), ("xbc", conv_dim), ("dt", nh), ("rkv", 3 * w), ("pw", dr), ("pa", ar), ("pg", gr)):
        segs[nm] = (o, width)
        o += width
    padded = {"z": w, "xbc": conv_dim, "dt": LANES, "rkv": 3 * w, "pw": LANES, "pa": LANES, "pg": gr}
    order = ("z", "xbc", "dt", "rkv", "pw", "pa", "pg")
    w_segs = [jnp.concatenate([_pad_rows(w_in_t[segs[nm][0]:segs[nm][0] + segs[nm][1]], padded[nm]) for nm in grp], axis=0)
              for grp in (("z",), ("xbc",), ("dt",), ("rkv",), ("pw", "pa", "pg"))]
    w_perm_t = jnp.concatenate(w_segs, axis=0)
    offs = {}
    o = 0
    for nm in order:
        offs[nm] = o
        o += padded[nm]
    lora_w = 2 * LANES + gr

    mu = wt["rwkv_mu"]
    mo = 3 * w
    mu_rkv = mu[:, :mo]
    mu_lora = jnp.concatenate([_pad_cols(mu[:, mo:mo + dr], LANES), _pad_cols(mu[:, mo + dr:mo + dr + ar], LANES),
                               mu[:, mo + dr + ar:]], axis=1)
    w2p = _pad_rows(full["rwkv_w2"], LANES)
    a2p = _pad_rows(full["rwkv_a2"], LANES)
    g2 = full["rwkv_g2"]
    conv_w = full["ssd_conv_w"]
    cw = [conv_w[i:i + 1] for i in range(SSD_CONV)]
    dt_bias = _pad_cols(wt["ssd_dt_bias"], LANES)
    a_log = _pad_cols(wt["ssd_a_log"], LANES)
    d_skip = _pad_cols(wt["ssd_d"], LANES)
    r_k = wt["rwkv_r_k"].reshape(1, w)

    z, xbc, dtraw, urkv, ulora = [matmul("in_proj_%d" % i, h1, ws, tb=True) for i, ws in enumerate(w_segs)]
    big.start(2, urkv)

    halo_xbc = _halo(xbc, tr)
    ssd_pre_t = [(xbc, tr, conv_dim, 0), (halo_xbc, HALO, conv_dim, 0), (dtraw, tr, LANES, 0)]
    ssd_pre_f = cw + [wt["ssd_conv_b"], dt_bias]
    act, dt = fn_fwd("ssd_pre", _ssd_pre, nt, ssd_pre_t, ssd_pre_f, [(t, tr, conv_dim, F32), (t, tr, LANES, F32)])

    nb = w // LANES
    ssd_seq = [(act, None), (act, lambda p: nb + p // ppg), (act, lambda p: nb + SSD_GROUPS + p // ppg), (dt, lambda p: 0)]
    ssd_ppb = min(ppg, PAIRS_PER_STEP)
    rw_ppb = min(n_pairs, 2 * PAIRS_PER_STEP)

    def ssd_fn(sv, cv, hts, ids):
        return [_ssd_chunk(*s, cv[0], ht, p) for s, ht, p in zip(sv, hts, ids)]

    y_scan, ssd_states = scan_fwd("ssd_scan", ssd_fn, SSD_CHUNK, ssd_seq, [a_log], n_pairs, ssd_ppb)
    ssd_post_t = [(y_scan, tr, w, 0), (act, tr, w, 0), (z, tr, w, 0)]
    ssd_post_f = [d_skip, wt["ssd_norm_g"]]
    (y_ssd,) = fn_fwd("ssd_post", _ssd_post, nt, ssd_post_t, ssd_post_f, [(t, tr, w, BF16)])

    halo_rkv, halo_lora = _halo(urkv, tr), _halo(ulora, tr)
    rw_pre_t = [(urkv, tr, 3 * w, 0), (ulora, tr, lora_w, 0), (halo_rkv, HALO, 3 * w, 0), (halo_lora, HALO, lora_w, 0)]
    rw_pre_f = [mu_rkv, mu_lora, wt["rwkv_w0"], wt["rwkv_a0"], wt["rwkv_k_k"], wt["rwkv_k_a"], w2p, a2p, g2]
    rw = fn_fwd("rwkv_pre", _rwkv_pre, nt, rw_pre_t, rw_pre_f, [(t, tr, w, F32)] * 7)
    r_, lw_, k2_, v_, nkk_, b_, gate_ = rw
    rw_seq = [(a, None) for a in (r_, lw_, k2_, v_, nkk_, b_)]

    def rw_fn(sv, cv, hts, ids):
        return _rwkv_chunks([(*s, ht) for s, ht in zip(sv, hts)])

    yr_scan, rw_states = scan_fwd("rwkv_scan", rw_fn, RWKV_CHUNK, rw_seq, [], n_pairs, rw_ppb)
    rw_post_t = [(a, tr, w, 0) for a in (yr_scan, r_, k2_, v_, gate_)]
    rw_post_f = [r_k, wt["rwkv_ln_w"], wt["rwkv_ln_b"]]
    (y_rwkv,) = fn_fwd("rwkv_post", _rwkv_post, nt, rw_post_t, rw_post_f, [(t, tr, w, BF16)])

    ymix = jnp.concatenate([y_ssd, y_rwkv], axis=1)
    w_out = big.get("w_out", ymix)
    x1 = matmul("out_proj", ymix, w_out, resid=x)

    h2 = norm_fwd("norm_x", x1, wt["norm_x_g"], tr2)
    mrows = mem.shape[0]
    mn = norm_fwd("norm_mem", mem, wt["norm_mem_g"], mrows)
    wq, wk, wv, wo = [big.get(nm, ymix) for nm in ("xattn_wq", "xattn_wk", "xattn_wv", "xattn_wo")]
    q = matmul("xattn_q", h2, wq)
    kx = matmul("xattn_k", mn, wk)
    vx = matmul("xattn_v", mn, wv)
    (ao,) = fn_fwd("xattn_core", _attn, nt2, [(q, tr2, d, 0)], [kx, vx], [(t, tr2, d, BF16)])
    x2 = matmul("xattn_o", ao, wo, resid=x1)

    h3 = norm_fwd("norm_ffn", x2, wt["norm_ffn_g"], tr2)
    w1, w2 = big.get("ffn_w1", h3), big.get("ffn_w2", h3)
    a1 = matmul("ffn_up", h3, w1, out_dtype=BF16)
    dff = a1.shape[1]
    (f1,) = fn_fwd("ffn_act", _relu2, nt, [(a1, tr, dff, 0)], [], [(t, tr, dff, BF16)])
    x3 = matmul("ffn_down", f1, w2, resid=x2)

    dx3, dx3b, g_final, loss_tile = loss_head(x3, tgt, wt["final_norm_g"].reshape(1, d), tr2)

    grads = {"final_norm_g": g_final.reshape(d)}
    grads["ffn_w2"] = matmul("ffn_down_dw", f1, dx3b, ta=True)
    df1 = matmul("ffn_down_dx", dx3b, w2, tb=True, out_dtype=BF16)
    (da1,), _ = fn_bwd("ffn_act_bwd", _relu2, nt, [(a1, tr, dff, 0)], [], [(df1, tr, dff, 0)], lambda c: [c[0].astype(F32)],
                       [(t, tr, dff, BF16)])
    grads["ffn_w1"] = matmul("ffn_up_dw", h3, da1, out_slots=4, ta=True)
    dh3 = reducer.launch(0, grads, matmul("ffn_up_dx", da1, w1, tb=True))
    dx2, dx2b, grads["norm_ffn_g"] = norm_bwd("norm_ffn_bwd", x2, wt["norm_ffn_g"], dh3, dx3, tr2)

    grads["xattn_wo"] = matmul("xattn_o_dw", ao, dx2b, ta=True)
    dao = matmul("xattn_o_dx", dx2b, wo, tb=True)
    (dq,), (dkx, dvx) = fn_bwd("xattn_core_bwd", _attn, nt2, [(q, tr2, d, 0)], [kx, vx], [(dao, tr2, d, 0)], lambda c: c,
                               [(t, tr2, d, BF16)])
    grads["xattn_wq"] = matmul("xattn_q_dw", h2, dq, ta=True)
    dh2 = matmul("xattn_q_dx", dq, wq, tb=True)
    dkb, dvb = dkx.astype(BF16), dvx.astype(BF16)
    grads["xattn_wk"] = matmul("xattn_k_dw", mn, dkb, ta=True)
    grads["xattn_wv"] = matmul("xattn_v_dw", mn, dvb, ta=True)
    dmn = matmul("xattn_k_dx", dkb, wk, tb=True)
    dmn = matmul("xattn_v_dx", dvb, wv, tb=True, resid=dmn)
    _, _, grads["norm_mem_g"] = norm_bwd("norm_mem_bwd", mem, wt["norm_mem_g"], dmn, None, mrows)
    dx1, dx1b, grads["norm_x_g"] = norm_bwd("norm_x_bwd", x1, wt["norm_x_g"], dh2, dx2, tr2)

    grads["w_out"] = matmul("out_proj_dw", ymix, dx1b, ta=True)
    dymix = reducer.launch(1, grads, matmul("out_proj_dx", dx1b, w_out, tb=True))

    (dyr, dr1, dk1, dv1, dgate), (g_rk, grads["rwkv_ln_w"], grads["rwkv_ln_b"]) = fn_bwd(
        "rwkv_post_bwd", _rwkv_post, nt, rw_post_t, rw_post_f, [(dymix, tr, w, 1)], lambda c: c, [(t, tr, w, F32)] * 5)
    grads["rwkv_r_k"] = g_rk.reshape(wt["rwkv_r_k"].shape)
    (dr2, dlw, dk2, dv2, dnkk, db), _ = scan_bwd("rwkv_scan_bwd", rw_fn, RWKV_CHUNK, rw_seq, [], rw_states, dyr, n_pairs, rw_ppb)
    rw_ct = [(a, tr, w, 0) for a in (dr1, dr2, dlw, dk1, dk2, dv1, dv2, dnkk, db, dgate)]

    def rw_ct_fn(c):
        return (c[0] + c[1], c[2], c[3] + c[4], c[5] + c[6], c[7], c[8], c[9])

    (durkv, dulora, dhrkv, dhlora), rw_pg = fn_bwd(
        "rwkv_pre_bwd", _rwkv_pre, nt, rw_pre_t, rw_pre_f, rw_ct, rw_ct_fn,
        [(t, tr, 3 * w, F32), (t, tr, lora_w, F32), (nt * HALO, HALO, 3 * w, F32), (nt * HALO, HALO, lora_w, F32)])
    durkv = _unhalo(durkv, dhrkv, tr)
    dulora = _unhalo(dulora, dhlora, tr)
    g_mu_rkv, g_mu_lora, grads["rwkv_w0"], grads["rwkv_a0"], grads["rwkv_k_k"], grads["rwkv_k_a"], g_w2p, g_a2p, grads["rwkv_g2"] = rw_pg
    grads["rwkv_mu"] = jnp.concatenate([g_mu_rkv, g_mu_lora[:, :dr], g_mu_lora[:, LANES:LANES + ar], g_mu_lora[:, 2 * LANES:]], axis=1)
    grads["rwkv_w2"] = g_w2p[:dr]
    grads["rwkv_a2"] = g_a2p[:ar]

    (dys, dxs1, dz), (g_d, grads["ssd_norm_g"]) = fn_bwd(
        "ssd_post_bwd", _ssd_post, nt, ssd_post_t, ssd_post_f, [(dymix, tr, w, 0)], lambda c: c, [(t, tr, w, F32)] * 3)
    grads["ssd_d"] = g_d[:, :nh]
    (dxs2, dbp, dcp, ddtp), (g_alog,) = scan_bwd("ssd_scan_bwd", ssd_fn, SSD_CHUNK, ssd_seq, [a_log], ssd_states, dys, n_pairs, ssd_ppb)
    grads["ssd_a_log"] = g_alog[:, :nh]
    ssd_ct = [(dxs1, tr, w, 0), (dxs2, tr, w, 0), (dbp, tr, w, 0), (dcp, tr, w, 0), (ddtp, tr, w, 0)]

    def ssd_ct_fn(c):
        def group_sum(a):
            parts = []
            for gi in range(SSD_GROUPS):
                s = a[:, gi * ppg * LANES:(gi * ppg + 1) * LANES]
                for j in range(1, ppg):
                    s = s + a[:, (gi * ppg + j) * LANES:(gi * ppg + j + 1) * LANES]
                parts.append(s)
            return parts
        ddt = c[4][:, :LANES]
        for j in range(1, n_pairs):
            ddt = ddt + c[4][:, j * LANES:(j + 1) * LANES]
        return (jnp.concatenate([c[0] + c[1]] + group_sum(c[2]) + group_sum(c[3]), axis=1), ddt)

    (dxbc, dhxbc, ddtraw), ssd_pg = fn_bwd(
        "ssd_pre_bwd", _ssd_pre, nt, ssd_pre_t, ssd_pre_f, ssd_ct, ssd_ct_fn,
        [(t, tr, conv_dim, F32), (nt * HALO, HALO, conv_dim, F32), (t, tr, LANES, F32)])
    dxbc = _unhalo(dxbc, dhxbc, tr)
    grads["ssd_conv_w"] = jnp.concatenate(ssd_pg[:SSD_CONV], axis=0)
    grads["ssd_conv_b"] = ssd_pg[SSD_CONV]
    grads["ssd_dt_bias"] = ssd_pg[SSD_CONV + 1][:, :nh]

    du = jnp.concatenate([dz, dxbc, ddtraw, durkv, dulora], axis=1).astype(BF16)
    g_perm_t = matmul("in_proj_dw", du, h1, ta=True)
    grads["w_in"] = jnp.concatenate([g_perm_t[offs[nm]:offs[nm] + segs[nm][1]] for nm in order], axis=0)
    dh1 = matmul("in_proj_dx", du, w_perm_t)
    dh1 = reducer.launch(2, grads, dh1)
    grad_x, _, grads["norm_mix_g"] = norm_bwd("norm_mix_bwd", x, wt["norm_mix_g"], dh1, dx1, tr2)
    return loss_tile, grad_x, grads


def _pack(arrs):
    flat = jnp.concatenate([a.reshape(-1) for a in arrs])
    n = flat.shape[0]
    rows = -(-n // (8 * LANES)) * 8
    return jnp.pad(flat, (0, rows * LANES - n)).reshape(rows, LANES)


def _unpack(packed, shapes):
    flat = packed.reshape(-1)
    out, o = [], 0
    for s in shapes:
        n = math.prod(s)
        out.append(flat[o:o + n].reshape(s))
        o += n
    return out


def _as2d(a):
    return a.reshape(-1, a.shape[-1])


def _shard_view(n, a):
    return _as2d(a[0]).T if n in TRANSPOSED else _as2d(a[0])


class _GatheredWeights:
    def __init__(self, shard2d, q, c):
        self.shard2d, self.q, self.c = shard2d, q, c
        self.raw, self.ready = {}, {}

    def start(self, gi, after):
        shards = [self.shard2d[n].astype(BF16) for n in GATHER_GROUPS[gi]]
        if after is not None:
            shards, _ = lax.optimization_barrier((shards, after))
        gathered = gather_two_level("gather_weights_%d" % gi, shards, gi + 1)
        self.raw.update(zip(GATHER_GROUPS[gi], gathered))

    def get(self, name, after):
        if name not in self.ready:
            g = self.raw[name]
            if after is not None:
                g, _ = lax.optimization_barrier((g, after))
            self.ready[name] = _from_slots(g, 0) if BIG_AXIS[name] == 0 else g
        return self.ready[name]


class _GradReducer:
    def __init__(self, q, c, update):
        self.q, self.c, self.update = q, c, update
        self.pending, self.updated = {}, {}

    def launch(self, gi, grads, nxt):
        names = REDUCE_GROUPS[gi]
        slots = [grads[n] if grads[n].ndim == 3 else _to_slots(grads[n], REDUCE_AXIS[n]) for n in names]
        rows = [s.shape[1] for s in slots]
        sent = [half_call("send_half_" + n, s, 1 - self.c, None, BF16) for n, s in zip(names, slots)]
        got = core_swap("swap_halves_%d" % gi, sent)
        parts = [half_call("chip_sum_" + n, s, self.c, g, BF16) for n, s, g in zip(names, slots, got)]
        parts, nxt = lax.optimization_barrier((parts, nxt))
        slots = scatter_slots("scatter_grads_%d" % gi, parts, len(GATHER_GROUPS) + 1 + gi)
        self.pending[gi] = (slots, rows)
        return self.finish(gi - 1, nxt) if gi > 0 else nxt

    def finish(self, gi, nxt):
        names = REDUCE_GROUPS[gi]
        slots, rows = self.pending[gi]
        halves = []
        for n, s in zip(names, slots):
            halves.append(sum_slots("sum_" + n, s))
        others = core_swap("swap_reduced_%d" % gi, halves)
        lo = [jnp.where(self.c == 0, mine, other) for mine, other in zip(halves, others)]
        hi = [jnp.where(self.c == 0, other, mine) for mine, other in zip(halves, others)]
        results = [self.update(n, _join_halves(l, h, r)) for n, l, h, r in zip(names, lo, hi, rows)]
        if nxt is not None:
            results, nxt = lax.optimization_barrier((results, nxt))
        self.updated.update(zip(names, results))
        return nxt


def _step(a):
    x, mem, tgt = a["x"][0], a["mem"][0], a["loss_target"][0]
    q = 2 * lax.axis_index("x") + lax.axis_index("y")

    shard2d = {n: _shard_view(n, a[n]) for n in BIG}
    small_sh = {n: _as2d(a[n][0]) for n in SMALL_SHARDED}
    c = lax.axis_index("c")
    full = {}
    big = _GatheredWeights(shard2d, q, c)
    gathered = gather_shards("gather_small", [small_sh[n] for n in SMALL_SHARDED])
    for n, g in zip(SMALL_SHARDED, gathered):
        full[n] = _from_slots(g, 1)

    wt = {n: (a[n] if a[n].ndim <= 2 else a[n][0]) for n in WEIGHTS if n not in BIG and n not in SMALL_SHARDED}
    for n in SMALL_SHARDED:
        wt[n] = small_sh[n]
    shards = dict(shard2d)
    shards.update({n: small_sh[n] for n in REDUCED if n not in BIG})

    def update(n, gsum):
        return adamw("adamw_" + n, shards[n], _shard_view(n, a["m_" + n]), _shard_view(n, a["v_" + n]), gsum)

    reducer = _GradReducer(q, c, update)
    loss_tile, grad_x, grads = _local_grads(x, mem, tgt, wt, full, big, reducer)
    reducer.finish(len(REDUCE_GROUPS) - 1, None)
    out = {}
    for n, vals in reducer.updated.items():
        for key, val in zip(("grad_", "delta_", "new_m_", "new_v_"), vals):
            out[key + n] = (val.T if n in TRANSPOSED else val).reshape(a[n].shape)

    small = [n for n in WEIGHTS if n not in REDUCED]
    red = _unpack(all_reduce_small("all_reduce_small", _pack([grads[n] for n in small])), [grads[n].shape for n in small])
    g_loc = {}
    for n, g in zip(small, red):
        if n in SMALL_SHARDED:
            cols = g.shape[1] // 4
            g = lax.dynamic_slice_in_dim(g, q * cols, cols, axis=1)
        g_loc[n] = g.reshape(a[n].shape)
    res = adamw("adamw_small", *[_pack([src[n] for n in small]) for src in
                                 ({n: a[n] for n in small}, {n: a["m_" + n] for n in small}, {n: a["v_" + n] for n in small})],
                _pack([g_loc[n] for n in small]))
    shapes = [a[n].shape for n in small]
    for key, packed in zip(("grad_", "delta_", "new_m_", "new_v_"), res):
        for n, val in zip(small, _unpack(packed, shapes)):
            out[key + n] = val

    loss = lax.psum(loss_tile[0, 0], ("x", "y", "c"))
    ordered = [loss, grad_x.reshape(a["x"].shape)]
    for key in ("grad_", "delta_", "new_m_", "new_v_"):
        ordered += [out[key + n] for n in WEIGHTS]
    return tuple(ordered)


def kernel(x, mem, norm_mix_g, w_in, ssd_conv_w, ssd_conv_b, ssd_dt_bias, ssd_a_log, ssd_d, ssd_norm_g, rwkv_mu, rwkv_w0, rwkv_w2, rwkv_a0, rwkv_a2, rwkv_g2, rwkv_k_k, rwkv_k_a, rwkv_r_k, rwkv_ln_w, rwkv_ln_b, w_out, norm_x_g, norm_mem_g, xattn_wq, xattn_wk, xattn_wv, xattn_wo, norm_ffn_g, ffn_w1, ffn_w2, final_norm_g, loss_target, m_norm_mix_g, m_w_in, m_ssd_conv_w, m_ssd_conv_b, m_ssd_dt_bias, m_ssd_a_log, m_ssd_d, m_ssd_norm_g, m_rwkv_mu, m_rwkv_w0, m_rwkv_w2, m_rwkv_a0, m_rwkv_a2, m_rwkv_g2, m_rwkv_k_k, m_rwkv_k_a, m_rwkv_r_k, m_rwkv_ln_w, m_rwkv_ln_b, m_w_out, m_norm_x_g, m_norm_mem_g, m_xattn_wq, m_xattn_wk, m_xattn_wv, m_xattn_wo, m_norm_ffn_g, m_ffn_w1, m_ffn_w2, m_final_norm_g, v_norm_mix_g, v_w_in, v_ssd_conv_w, v_ssd_conv_b, v_ssd_dt_bias, v_ssd_a_log, v_ssd_d, v_ssd_norm_g, v_rwkv_mu, v_rwkv_w0, v_rwkv_w2, v_rwkv_a0, v_rwkv_a2, v_rwkv_g2, v_rwkv_k_k, v_rwkv_k_a, v_rwkv_r_k, v_rwkv_ln_w, v_rwkv_ln_b, v_w_out, v_norm_x_g, v_norm_mem_g, v_xattn_wq, v_xattn_wk, v_xattn_wv, v_xattn_wo, v_norm_ffn_g, v_ffn_w1, v_ffn_w2, v_final_norm_g):
    return _step(dict(locals()))
```

```python
import functools
import math

import jax
import jax.numpy as jnp
from jax import lax
from jax.experimental import pallas as pl
from jax.experimental.pallas import tpu as pltpu
from jax.experimental.pallas import tpu_sc as plsc

F32 = jnp.float32
BF16 = jnp.bfloat16
HIGHEST = lax.Precision.HIGHEST
MESH_ID = pl.DeviceIdType.MESH

NORM_EPS = 1e-6
RWKV_LN_EPS = 64e-5
HEAD_DIM = 64
PAIR = 2 * HEAD_DIM
LANES = 128
SSD_STATE = 128
SSD_CHUNK = 128
SSD_GROUPS = 2
SSD_CONV = 4
RWKV_CHUNK = 64
HALO = 8
ROW_TILE = 128
PAIRS_PER_STEP = 4
XATTN_HEADS = 4
RWKV_PASSES = 1
VMEM_LIMIT = 56 * 1024 * 1024
MATMUL_VMEM = 40 * 1024 * 1024

ADAM_LR = 0.001
ADAM_B1 = 0.9
ADAM_B2 = 0.999
ADAM_EPS = 1e-08
ADAM_WD = 0.01
ADAM_STEP = 10


def _dims(ca, cb):
    return (((ca,), (cb,)), ((), ()))


def _split_bf16(a):
    hi = a.astype(BF16)
    lo = (a - hi.astype(F32)).astype(BF16)
    return hi, lo


def _mm_impl(a, b, ca, cb, passes):
    dn = _dims(ca, cb)
    if passes == 1:
        return lax.dot_general(a.astype(BF16), b.astype(BF16), dn, preferred_element_type=F32)
    ah, al = _split_bf16(a)
    bh, bl = _split_bf16(b)
    out = lax.dot_general(ah, bh, dn, preferred_element_type=F32)
    out = out + lax.dot_general(ah, bl, dn, preferred_element_type=F32)
    return out + lax.dot_general(al, bh, dn, preferred_element_type=F32)


@functools.partial(jax.custom_vjp, nondiff_argnums=(2, 3, 4))
def mm(a, b, ca, cb, passes):
    return _mm_impl(a, b, ca, cb, passes)


def _mm_fwd(a, b, ca, cb, passes):
    return _mm_impl(a, b, ca, cb, passes), (a, b)


def _mm_bwd(ca, cb, passes, res, g):
    a, b = res
    da = mm(g, b, 1, 1 - cb, passes) if ca == 1 else mm(b, g, 1 - cb, 1, passes)
    db = mm(a, g, 1 - ca, 0, passes) if cb == 0 else mm(g, a, 0, 1 - ca, passes)
    return da, db


mm.defvjp(_mm_fwd, _mm_bwd)


def _dot_exact(a, b):
    return lax.dot_general(a, b, _dims(1, 0), precision=HIGHEST, preferred_element_type=F32)


def _iota(shape, dim):
    return lax.broadcasted_iota(jnp.int32, shape, dim)


def _sigmoid(x):
    return 1.0 / (1.0 + jnp.exp(-x))


def _silu(x):
    return x * _sigmoid(x)


def _softplus(x):
    return jnp.maximum(x, 0.0) + jnp.log(1.0 + jnp.exp(-jnp.abs(x)))


def _rms(x, g):
    return x * lax.rsqrt(jnp.mean(x * x, axis=-1, keepdims=True) + NORM_EPS) * g


def _select_mm(x, sel):
    hi = x.astype(BF16)
    r1 = x - hi.astype(F32)
    mid = r1.astype(BF16)
    lo = (r1 - mid.astype(F32)).astype(BF16)
    dn = _dims(1, 0)
    out = lax.dot_general(hi, sel, dn, preferred_element_type=F32)
    out = out + lax.dot_general(mid, sel, dn, preferred_element_type=F32)
    return out + lax.dot_general(lo, sel, dn, preferred_element_type=F32)


def _head_sum_impl(x, n):
    sel = (_iota((n, LANES), 0) // HEAD_DIM == _iota((n, LANES), 1)).astype(BF16)
    return _select_mm(x, sel)


def _head_expand_impl(s, n):
    sel = (_iota((LANES, n), 1) // HEAD_DIM == _iota((LANES, n), 0)).astype(BF16)
    return _select_mm(s, sel)


@functools.partial(jax.custom_vjp, nondiff_argnums=(1,))
def _head_sum_n(x, n):
    return _head_sum_impl(x, n)


@functools.partial(jax.custom_vjp, nondiff_argnums=(1,))
def _head_expand(s, n):
    return _head_expand_impl(s, n)


_head_sum_n.defvjp(lambda x, n: (_head_sum_impl(x, n), None), lambda n, _, g: (_head_expand(g, n),))
_head_expand.defvjp(lambda s, n: (_head_expand_impl(s, n), None), lambda n, _, g: (_head_sum_n(g, n),))


def _head_sum(x):
    return _head_sum_n(x, x.shape[1])


def _row_vector_expand(v, n):
    v8 = jnp.broadcast_to(v, (8, LANES))
    return jnp.sum(_head_expand(v8, n), axis=0, keepdims=True) * 0.125


def _shift_rows_impl(u, halo, s):
    rolled = pltpu.roll(u, s, 0)
    top = jnp.where(_iota((HALO, 1), 0) < s, pltpu.roll(halo, s, 0), rolled[:HALO])
    return jnp.concatenate([top, rolled[HALO:]], axis=0)


@functools.partial(jax.custom_vjp, nondiff_argnums=(2,))
def _shift_rows(u, halo, s):
    return _shift_rows_impl(u, halo, s)


def _shift_rows_bwd(s, _, g):
    tr = g.shape[0]
    rolled = pltpu.roll(g, tr - s, 0)
    hrow = _iota((HALO, 1), 0)
    bottom = jnp.where(hrow < HALO - s, rolled[tr - HALO:], 0.0)
    dhalo = jnp.where(hrow >= HALO - s, pltpu.roll(g[:HALO], HALO - s, 0), 0.0)
    return jnp.concatenate([rolled[:tr - HALO], bottom], axis=0), dhalo


_shift_rows.defvjp(lambda u, halo, s: (_shift_rows_impl(u, halo, s), None), _shift_rows_bwd)


def _params(sem):
    return pltpu.CompilerParams(dimension_semantics=sem, vmem_limit_bytes=VMEM_LIMIT)


def row_call(name, body, n_tiles, tiled, full, out_tiled, out_acc):
    nt, nf, no, na = len(tiled), len(full), len(out_tiled), len(out_acc)

    def kern(*refs):
        tv = [r[...] for r in refs[:nt]]
        fv = [r[...] for r in refs[nt:nt + nf]]
        outs, accs = body(tv, fv)
        for r, v in zip(refs[nt + nf:nt + nf + no], outs):
            r[...] = v.astype(r.dtype)
        if na:
            a_refs = refs[nt + nf + no:]
            first = pl.program_id(0) == 0

            @pl.when(first)
            def _():
                for r, v in zip(a_refs, accs):
                    r[...] = v

            @pl.when(jnp.logical_not(first))
            def _():
                for r, v in zip(a_refs, accs):
                    r[...] += v

    in_specs = [pl.BlockSpec((rt, w), functools.partial(lambda i, cb: (i, cb), cb=cb)) for (_, rt, w, cb) in tiled]
    in_specs += [pl.BlockSpec(a.shape, lambda i: (0, 0)) for a in full]
    out_specs = [pl.BlockSpec((rt, w), lambda i: (i, 0)) for (_, rt, w, _) in out_tiled]
    out_specs += [pl.BlockSpec(s, lambda i: (0, 0)) for s in out_acc]
    out_shape = [jax.ShapeDtypeStruct((rows, w), dt) for (rows, _, w, dt) in out_tiled]
    out_shape += [jax.ShapeDtypeStruct(s, F32) for s in out_acc]
    res = pl.pallas_call(
        kern, name=name, grid=(n_tiles,), in_specs=in_specs, out_specs=out_specs, out_shape=out_shape,
        compiler_params=_params(("arbitrary",)),
    )(*[t[0] for t in tiled], *full)
    return list(res[:no]), list(res[no:])


def _pick(dim, cands):
    for c in cands:
        if dim % c == 0:
            return c
    return dim


def matmul(name, a, b, tb=False, resid=None, out_dtype=F32, out_slots=1, ta=False):
    (k, m) = a.shape if ta else a.shape[::-1]
    b_slots = b.shape[0] if b.ndim == 3 else 1
    n = b.shape[-2] if tb else b.shape[-1] * b_slots
    has_resid = resid is not None
    out_bytes = jnp.dtype(out_dtype).itemsize
    sizes = (2048, 1024, 896, 768, 512, 384, 256, 128)
    tm = _pick(m, sizes[1:])
    tn = _pick(n // max(out_slots, 1 if tb else b_slots), sizes[1:])

    def vmem_bytes(tk):
        return 2 * 2 * tk * (tm + tn) + tm * tn * (2 * out_bytes + 4 + (8 if has_resid else 0))

    k_slot = k // b_slots if tb else k
    tk = next((c for c in sizes if k_slot % c == 0 and vmem_bytes(c) <= MATMUL_VMEM), LANES)
    nk = k // tk
    n_per = n // (b_slots if not tb else 1) // tn
    k_per = k_slot // tk
    o_per = n // out_slots // tn

    def kern(*refs):
        a_ref, b_ref = refs[0], refs[1]
        o_ref, acc = refs[-2], refs[-1]
        kk = pl.program_id(2)
        part = lax.dot_general(a_ref[...], b_ref[...], _dims(0 if ta else 1, 1 if tb else 0), preferred_element_type=F32)

        def finish(out):
            if has_resid:
                out = out + refs[2][...]
            o_ref[...] = out.astype(o_ref.dtype)

        if nk == 1:
            finish(part)
            return

        @pl.when(kk == 0)
        def _():
            acc[...] = part

        @pl.when(jnp.logical_and(kk > 0, kk < nk - 1))
        def _():
            acc[...] += part

        @pl.when(kk == nk - 1)
        def _():
            finish(acc[...] + part)

    in_specs = [pl.BlockSpec((tk, tm), lambda i, j, kk: (kk, i)) if ta else pl.BlockSpec((tm, tk), lambda i, j, kk: (i, kk))]
    if b.ndim == 3 and tb:
        in_specs.append(pl.BlockSpec((None, tn, tk), lambda i, j, kk: (kk // k_per, j, kk % k_per)))
    elif b.ndim == 3:
        in_specs.append(pl.BlockSpec((None, tk, tn), lambda i, j, kk: (j // n_per, kk, j % n_per)))
    elif tb:
        in_specs.append(pl.BlockSpec((tn, tk), lambda i, j, kk: (j, kk)))
    else:
        in_specs.append(pl.BlockSpec((tk, tn), lambda i, j, kk: (kk, j)))
    args = [a, b]
    if has_resid:
        in_specs.append(pl.BlockSpec((tm, tn), lambda i, j, kk: (i, j)))
        args.append(resid)
    if out_slots > 1:
        out_spec = pl.BlockSpec((None, tm, tn), lambda i, j, kk: (j // o_per, i, j % o_per))
        out_shape = jax.ShapeDtypeStruct((out_slots, m, n // out_slots), out_dtype)
    else:
        out_spec = pl.BlockSpec((tm, tn), lambda i, j, kk: (i, j))
        out_shape = jax.ShapeDtypeStruct((m, n), out_dtype)
    return pl.pallas_call(
        kern, name=name, grid=(m // tm, n // tn, nk), in_specs=in_specs,
        out_specs=out_spec, out_shape=out_shape,
        scratch_shapes=[pltpu.VMEM((tm, tn), F32)],
        compiler_params=_params(("parallel", "parallel", "arbitrary")),
    )(*args)


def norm_fwd(name, x, g, tr):
    def body(tv, fv):
        return [_rms(tv[0], fv[0])], []
    rows, d = x.shape
    (h,), _ = row_call(name, body, rows // tr, [(x, tr, d, 0)], [g], [(rows, tr, d, BF16)], [])
    return h


def norm_bwd(name, x, g, dh, extra, tr):
    def body(tv, fv):
        _, vjp = jax.vjp(_rms, tv[0], fv[0])
        dx, dg = vjp(tv[1])
        if extra is not None:
            dx = dx + tv[2]
        return [dx, dx], [dg]
    rows, d = x.shape
    tiled = [(x, tr, d, 0), (dh, tr, d, 0)] + ([(extra, tr, d, 0)] if extra is not None else [])
    (dx, dxb), (dg,) = row_call(name, body, rows // tr, tiled, [g], [(rows, tr, d, F32), (rows, tr, d, BF16)], [g.shape])
    return dx, dxb, dg


def _ssd_pre(xbc, halo, dtraw, w0, w1, w2, w3, cb, dtb):
    y = w3 * xbc + w2 * _shift_rows(xbc, halo, 1) + w1 * _shift_rows(xbc, halo, 2) + w0 * _shift_rows(xbc, halo, 3) + cb
    return _silu(y), _softplus(dtraw + dtb)


def _ssd_post(ys, xs, z, dskip, ng):
    w = ys.shape[1]
    y = (ys + xs * _row_vector_expand(dskip, w)) * _silu(z)
    gw = w // SSD_GROUPS
    parts = []
    for gi in range(SSD_GROUPS):
        yg = y[:, gi * gw:(gi + 1) * gw]
        parts.append(yg * lax.rsqrt(jnp.mean(yg * yg, axis=-1, keepdims=True) + NORM_EPS))
    return jnp.concatenate(parts, axis=1) * ng


def _rwkv_pre(urkv, ulora, hrkv, hlora, mu_rkv, mu_lora, w0, a0, kkw, kaw, w2p, a2p, g2):
    w = w0.shape[1]
    urkv = urkv + (_shift_rows(urkv, hrkv, 1) - urkv) * mu_rkv
    ulora = ulora + (_shift_rows(ulora, hlora, 1) - ulora) * mu_lora
    r, k, v = urkv[:, :w], urkv[:, w:2 * w], urkv[:, 2 * w:]
    pw, pa, pg = ulora[:, :LANES], ulora[:, LANES:2 * LANES], ulora[:, 2 * LANES:]
    w_log = -_softplus(-(w0 + mm(jnp.tanh(pw), w2p, 1, 0, 1))) - 0.5
    lw = -jnp.exp(w_log)
    iclr = _sigmoid(a0 + mm(pa, a2p, 1, 0, 1))
    gate = mm(_sigmoid(pg), g2, 1, 0, 1)
    kk = k * kkw
    kk = kk / jnp.maximum(jnp.sqrt(_head_expand(_head_sum(kk * kk), w)), 1e-12)
    k2 = k * (1.0 + (iclr - 1.0) * kaw)
    return r, lw, k2, v, -kk, kk * iclr, gate


def _rwkv_post(ys, r, k2, v, gate, rk, lnw, lnb):
    w = ys.shape[1]
    inv = 1.0 / HEAD_DIM
    mean = _head_expand(_head_sum(ys), w) * inv
    d = ys - mean
    var = _head_expand(_head_sum(d * d), w) * inv
    yn = d * lax.rsqrt(var + RWKV_LN_EPS) * lnw + lnb
    bonus = _head_expand(_head_sum(r * k2 * rk), w) * v
    return (yn + bonus) * gate


def _attn(q, k, v):
    d = q.shape[1]
    hd = d // XATTN_HEADS
    outs = []
    for h in range(XATTN_HEADS):
        sl = slice(h * hd, (h + 1) * hd)
        s = mm(q[:, sl], k[:, sl], 1, 1, 1) * (hd ** -0.5)
        s = s - jnp.max(s, axis=-1, keepdims=True)
        p = jnp.exp(s)
        p = p / jnp.sum(p, axis=-1, keepdims=True)
        outs.append(mm(p, v[:, sl], 1, 0, 1))
    return jnp.concatenate(outs, axis=1)


def _relu2(a):
    return jnp.square(jnp.maximum(a.astype(F32), 0.0))


def fn_fwd(name, fn, n_tiles, tiled, full, out_tiled):
    def body(tv, fv):
        outs = fn(*tv, *fv)
        return (list(outs) if isinstance(outs, (tuple, list)) else [outs]), []
    outs, _ = row_call(name, body, n_tiles, tiled, full, out_tiled, [])
    return outs


def fn_bwd(name, fn, n_tiles, tiled, full, cts, ct_fn, out_tiled):
    nt = len(tiled)

    def body(tv, fv):
        outs, vjp = jax.vjp(fn, *tv[:nt], *fv)
        ct = ct_fn(tv[nt:])
        grads = vjp(tuple(ct) if isinstance(outs, (tuple, list)) else ct[0])
        return list(grads[:nt]), list(grads[nt:])
    return row_call(name, body, n_tiles, tiled + cts, full, out_tiled, [f.shape for f in full])


def _ssd_chunks(pairs, a_log, ids):
    q = pairs[0][0].shape[0]
    lane = _iota((1, LANES), 1)
    row = _iota((q, 1), 0)
    tril = _iota((q, q), 0) >= _iota((q, q), 1)
    half = lane < HEAD_DIM
    not_half = jnp.logical_not(half)
    n = len(pairs)
    bm, cm, dt_all = pairs[0][1], pairs[0][2], pairs[0][3]
    da = dt_all * (-jnp.exp(a_log))
    cs = _dot_exact(tril.astype(F32), da)

    def col(mat, h):
        return jnp.sum(jnp.where(lane == h, mat, 0.0), axis=1, keepdims=True)

    cs0 = [col(cs, 2 * p) for p in ids]
    cs1 = [col(cs, 2 * p + 1) for p in ids]
    xdt = [pairs[j][0] * jnp.where(half, col(dt_all, 2 * p), col(dt_all, 2 * p + 1)) for j, p in enumerate(ids)]
    csx = [jnp.where(half, a0, a1) for a0, a1 in zip(cs0, cs1)]
    last = [jnp.sum(jnp.where(row == q - 1, c_, 0.0), axis=0, keepdims=True) for c_ in csx]
    cb = mm(cm, bm, 1, 1, 1)
    y0 = [mm(cm, pairs[j][4], 1, 0, 1) for j in range(n)]
    st = [mm(bm, xdt[j] * jnp.exp(last[j] - csx[j]), 0, 0, 1) for j in range(n)]

    def decay(csh):
        csl = jnp.broadcast_to(csh, (q, q))
        return jnp.where(tril, jnp.exp(jnp.where(tril, csl - csl.T, 0.0)), 0.0)

    lm = [(decay(cs0[j]), decay(cs1[j])) for j in range(n)]
    yd = [(mm(cb * lm[j][0], xdt[j], 1, 0, 1), mm(cb * lm[j][1], xdt[j], 1, 0, 1)) for j in range(n)]
    out = []
    for j in range(n):
        y = y0[j] * jnp.exp(csx[j]) + jnp.where(half, yd[j][0], 0.0) + jnp.where(not_half, yd[j][1], 0.0)
        out.append((y, pairs[j][4] * jnp.exp(last[j]) + st[j]))
    return out


def _unit_lower_inverses_impl(mats):
    c = mats[0].shape[0]
    eye = (_iota((c, c), 0) == _iota((c, c), 1)).astype(F32)
    tm = [eye + a_ for a_ in mats]
    pm = mats
    for _ in range(int(math.log2(c)) - 1):
        pm = [mm(p_, p_, 1, 0, RWKV_PASSES) for p_ in pm]
        tm = [t_ + mm(t_, p_, 1, 0, RWKV_PASSES) for t_, p_ in zip(tm, pm)]
    return tm


@jax.custom_vjp
def _unit_lower_inverses(mats):
    return _unit_lower_inverses_impl(mats)


def _unit_lower_inverses_fwd(mats):
    tm = _unit_lower_inverses_impl(mats)
    return tm, tm


def _unit_lower_inverses_bwd(tm, g):
    left = [mm(t_, g_, 0, 0, RWKV_PASSES) for t_, g_ in zip(tm, g)]
    return ([mm(l_, t_, 1, 1, RWKV_PASSES) for l_, t_ in zip(left, tm)],)


_unit_lower_inverses.defvjp(_unit_lower_inverses_fwd, _unit_lower_inverses_bwd)


def _rwkv_chunks(pairs):
    c = pairs[0][0].shape[0]
    ps = RWKV_PASSES
    lane = _iota((1, LANES), 1)
    row = _iota((c, 1), 0)
    ri, ci = _iota((c, c), 0), _iota((c, c), 1)
    tril_i, tril_s = ri >= ci, ri > ci
    half = lane < HEAD_DIM
    halves = (half, jnp.logical_not(half))
    bd = (_iota((LANES, LANES), 0) < HEAD_DIM) == (_iota((LANES, LANES), 1) < HEAD_DIM)
    tri = tril_i.astype(F32)
    n = len(pairs)
    heads = [(j, hm) for j in range(n) for hm in halves]

    cum = [_dot_exact(tri, p[1]) for p in pairs]
    at = [p[4] * jnp.exp(cm - p[1]) for p, cm in zip(pairs, cum)]
    en = [jnp.exp(-cm) for cm in cum]
    bt = [p[5] * e for p, e in zip(pairs, en)]
    kt = [p[2] * e for p, e in zip(pairs, en)]
    rt = [p[0] * jnp.exp(cm) for p, cm in zip(pairs, cum)]
    ah = [mm(at[j], pairs[j][6], 1, 1, ps) for j in range(n)]
    y = [mm(rt[j], pairs[j][6], 1, 1, ps) for j in range(n)]
    atm = [jnp.where(hm, at[j], 0.0) for j, hm in heads]
    rtm = [jnp.where(hm, rt[j], 0.0) for j, hm in heads]
    aab = [jnp.where(tril_s, mm(atm[i], bt[j], 1, 1, ps), 0.0) for i, (j, _) in enumerate(heads)]
    aak = [jnp.where(tril_s, mm(atm[i], kt[j], 1, 1, ps), 0.0) for i, (j, _) in enumerate(heads)]
    arb = [jnp.where(tril_i, mm(rtm[i], bt[j], 1, 1, ps), 0.0) for i, (j, _) in enumerate(heads)]
    ark = [jnp.where(tril_i, mm(rtm[i], kt[j], 1, 1, ps), 0.0) for i, (j, _) in enumerate(heads)]
    rhs = [ah[j] + mm(aak[i], pairs[j][3], 1, 0, ps) for i, (j, _) in enumerate(heads)]
    yv = [mm(ark[i], pairs[j][3], 1, 0, ps) for i, (j, _) in enumerate(heads)]
    tm = _unit_lower_inverses(aab)
    uh =[mm(tm[i], rhs[i], 1, 0, ps) for i in range(len(heads))]
    u = [jnp.where(half, uh[2 * j], uh[2 * j + 1]) for j in range(n)]
    yu = [mm(arb[i], u[j], 1, 0, ps) for i, (j, _) in enumerate(heads)]
    out = []
    for j in range(n):
        yj = y[j] + jnp.where(half, yu[2 * j] + yv[2 * j], yu[2 * j + 1] + yv[2 * j + 1])
        plast = jnp.sum(jnp.where(row == c - 1, cum[j], 0.0), axis=0, keepdims=True)
        upd = pairs[j][6] + mm(u[j], bt[j], 0, 0, ps) + mm(pairs[j][3], kt[j], 0, 0, ps)
        out.append((yj, jnp.where(bd, upd * jnp.exp(plast), 0.0)))
    return out


def _seq_spec(chunk, ppb, col, row_of):
    if col is None:
        return pl.BlockSpec((chunk, ppb * LANES), lambda pb, i: (row_of(i), pb))
    return pl.BlockSpec((chunk, LANES), lambda pb, i: (row_of(i), col(pb * ppb)))


def _pair_vals(refs, seq_in, j):
    return [r[...] if col is not None else r[:, j * LANES:(j + 1) * LANES] for r, (_, col) in zip(refs, seq_in)]


def scan_fwd(name, chunk_fn, chunk, seq_in, const_in, n_pairs, ppb):
    t = seq_in[0][0].shape[0]
    nc = t // chunk
    ns, ncst = len(seq_in), len(const_in)

    def kern(*refs):
        y_ref, st_ref, ht = refs[ns + ncst], refs[ns + ncst + 1], refs[ns + ncst + 2]

        @pl.when(pl.program_id(1) == 0)
        def _():
            ht[...] = jnp.zeros_like(ht)

        cv = [r[...] for r in refs[ns:ns + ncst]]
        h0 = [ht[j] for j in range(ppb)]
        for j in range(ppb):
            st_ref[j] = h0[j]
        sv = [_pair_vals(refs[:ns], seq_in, j) for j in range(ppb)]
        outs = chunk_fn(sv, cv, h0, [pl.program_id(0) * ppb + j for j in range(ppb)])
        for j, (y, hn) in enumerate(outs):
            y_ref[:, j * LANES:(j + 1) * LANES] = y
            ht[j] = hn

    in_specs = [_seq_spec(chunk, ppb, col, lambda i: i) for (_, col) in seq_in]
    in_specs += [pl.BlockSpec(a.shape, lambda pb, i: (0, 0)) for a in const_in]
    return pl.pallas_call(
        kern, name=name, grid=(n_pairs // ppb, nc), in_specs=in_specs,
        out_specs=[pl.BlockSpec((chunk, ppb * LANES), lambda pb, i: (i, pb)),
                   pl.BlockSpec((ppb, None, LANES, LANES), lambda pb, i: (pb, i, 0, 0))],
        out_shape=[jax.ShapeDtypeStruct((t, n_pairs * LANES), F32), jax.ShapeDtypeStruct((n_pairs, nc, LANES, LANES), F32)],
        scratch_shapes=[pltpu.VMEM((ppb, LANES, LANES), F32)],
        compiler_params=_params(("arbitrary", "arbitrary")),
    )(*[s[0] for s in seq_in], *const_in)


def scan_bwd(name, chunk_fn, chunk, seq_in, const_in, states, dy, n_pairs, ppb):
    t = dy.shape[0]
    nc = t // chunk
    ns, ncst = len(seq_in), len(const_in)

    def kern(*refs):
        seq_refs, cst_refs = refs[:ns], refs[ns:ns + ncst]
        st_ref, dy_ref = refs[ns + ncst], refs[ns + ncst + 1]
        o = ns + ncst + 2
        dseq_refs, dcst_refs, dht = refs[o:o + ns], refs[o + ns:o + ns + ncst], refs[o + ns + ncst]
        pb, i = pl.program_id(0), pl.program_id(1)

        @pl.when(i == 0)
        def _():
            dht[...] = jnp.zeros_like(dht)

        ids = [pb * ppb + j for j in range(ppb)]
        lanes = [slice(j * LANES, (j + 1) * LANES) for j in range(ppb)]

        def fn(*flat):
            sv = [list(flat[j * ns:(j + 1) * ns]) for j in range(ppb)]
            outs = chunk_fn(sv, list(flat[ppb * ns:ppb * ns + ncst]), list(flat[ppb * ns + ncst:]), ids)
            return tuple(y for y, _ in outs), tuple(h for _, h in outs)

        flat_in = [v for j in range(ppb) for v in _pair_vals(seq_refs, seq_in, j)]
        flat_in += [r[...] for r in cst_refs] + [st_ref[j] for j in range(ppb)]
        _, vjp = jax.vjp(fn, *flat_in)
        grads = vjp((tuple(dy_ref[:, ln] for ln in lanes), tuple(dht[j] for j in range(ppb))))
        for j in range(ppb):
            for r, g in zip(dseq_refs, grads[j * ns:(j + 1) * ns]):
                r[:, lanes[j]] = g
            dht[j] = grads[ppb * ns + ncst + j]
        dcv = grads[ppb * ns:ppb * ns + ncst]
        if ncst:
            first = jnp.logical_and(pb == 0, i == 0)

            @pl.when(first)
            def _():
                for r, g in zip(dcst_refs, dcv):
                    r[...] = g

            @pl.when(jnp.logical_not(first))
            def _():
                for r, g in zip(dcst_refs, dcv):
                    r[...] += g

    rev = lambda i: nc - 1 - i
    wide = pl.BlockSpec((chunk, ppb * LANES), lambda pb, i: (rev(i), pb))
    in_specs = [_seq_spec(chunk, ppb, col, rev) for (_, col) in seq_in]
    in_specs += [pl.BlockSpec(a.shape, lambda pb, i: (0, 0)) for a in const_in]
    in_specs += [pl.BlockSpec((ppb, None, LANES, LANES), lambda pb, i: (pb, rev(i), 0, 0)), wide]
    out_specs = [wide for _ in seq_in]
    out_specs += [pl.BlockSpec(a.shape, lambda pb, i: (0, 0)) for a in const_in]
    out_shape = [jax.ShapeDtypeStruct((t, n_pairs * LANES), F32) for _ in seq_in]
    out_shape += [jax.ShapeDtypeStruct(a.shape, F32) for a in const_in]
    res = pl.pallas_call(
        kern, name=name, grid=(n_pairs // ppb, nc), in_specs=in_specs, out_specs=out_specs, out_shape=out_shape,
        scratch_shapes=[pltpu.VMEM((ppb, LANES, LANES), F32)],
        compiler_params=_params(("arbitrary", "arbitrary")),
    )(*[s[0] for s in seq_in], *const_in, states, dy)
    return list(res[:ns]), list(res[ns:])


def loss_head(x3, tgt, g, tr):
    rows, d = x3.shape

    def body(tv, fv):
        def f(x, gg):
            e = jnp.square(_rms(x, gg) - tv[1])
            return 0.5 * jnp.sum(jnp.mean(e, axis=-1, keepdims=True), axis=0, keepdims=True)
        l, vjp = jax.vjp(f, tv[0], fv[0])
        dx, dg = vjp(jnp.ones((1, 1), F32))
        return [dx, dx], [dg, jnp.broadcast_to(l, (8, LANES))]
    (dx, dxb), (dg, l) = row_call("loss_head", body, rows // tr, [(x3, tr, d, 0), (tgt, tr, d, 0)], [g],
                                  [(rows, tr, d, F32), (rows, tr, d, BF16)], [g.shape, (8, LANES)])
    return dx, dxb, dg, l


def _adam_math(w, g, m, v):
    m = ADAM_B1 * m + (1.0 - ADAM_B1) * g
    v = ADAM_B2 * v + (1.0 - ADAM_B2) * jnp.square(g)
    m_hat = m / (1.0 - ADAM_B1 ** ADAM_STEP)
    v_hat = v / (1.0 - ADAM_B2 ** ADAM_STEP)
    delta = -ADAM_LR * (m_hat / (jnp.sqrt(v_hat) + ADAM_EPS) + ADAM_WD * w)
    return delta, m, v


def _tiling(rows, cols, limit):
    row_tile = max([d for d in range(16, rows + 1, 16) if rows % d == 0 and d * cols <= limit], default=0)
    col_tile = max([ct for ct in range(LANES, cols + 1, LANES) if cols % ct == 0 and rows * ct <= limit], default=0)
    if row_tile and row_tile * cols >= rows * col_tile:
        return row_tile, cols
    return (rows, col_tile) if col_tile else (rows, cols)


def ew_call(name, fn, ins, out_dtypes, limit=1 << 20):
    rows, cols = ins[0].shape
    br, bc = _tiling(rows, cols, limit)
    spec = pl.BlockSpec((br, bc), lambda i, j: (i, j))
    n_in = len(ins)

    def kern(*refs):
        for r, v in zip(refs[n_in:], fn(*[r[...] for r in refs[:n_in]])):
            r[...] = v.astype(r.dtype)

    return pl.pallas_call(
        kern, name=name, grid=(rows // br, cols // bc), in_specs=[spec] * n_in, out_specs=[spec] * len(out_dtypes),
        out_shape=[jax.ShapeDtypeStruct((rows, cols), dt) for dt in out_dtypes],
        compiler_params=_params(("parallel", "parallel")),
    )(*ins)


def adamw(name, w, m, v, g):
    return ew_call(name, lambda wv, mv, vv, gv: (gv, *_adam_math(wv, gv, mv, vv)), [w, m, v, g], [F32] * 4, 1 << 18)


def half_call(name, s, h, extra, out_dtype):
    n_slots, rows, cols = s.shape
    by_cols = _halves_by_cols(rows)
    hr, hc = (rows, cols // 2) if by_cols else (rows // 2, cols)
    br, bc = _tiling(hr, hc, 1 << 20)
    ni, nj = hr // br, hc // bc
    if by_cols:
        s_spec = pl.BlockSpec((None, br, bc), lambda sl, i, j, href: (sl, i, href[0] * nj + j))
    else:
        s_spec = pl.BlockSpec((None, br, bc), lambda sl, i, j, href: (sl, href[0] * ni + i, j))
    flat = pl.BlockSpec((None, br, bc), lambda sl, i, j, href: (sl, i, j))
    has_extra = extra is not None

    def kern(href, s_ref, *rest):
        v = s_ref[...]
        if has_extra:
            v = v + rest[0][...].astype(F32)
        rest[-1][...] = v.astype(out_dtype)

    grid_spec = pltpu.PrefetchScalarGridSpec(
        num_scalar_prefetch=1, grid=(n_slots, ni, nj), in_specs=[s_spec] + ([flat] if has_extra else []), out_specs=flat)
    return pl.pallas_call(
        kern, name=name, grid_spec=grid_spec, out_shape=jax.ShapeDtypeStruct((n_slots, hr, hc), out_dtype),
        compiler_params=_params(("parallel", "parallel", "parallel")),
    )(jnp.reshape(h, (1,)).astype(jnp.int32), s, *([extra] if has_extra else []))


def sum_slots(name, r):
    _, rows, cols = r.shape
    br, bc = _tiling(rows, cols, 1 << 20)

    def kern(r0, r1, r2, r3, o):
        o[...] = ((r0[...].astype(F32) + r1[...].astype(F32)) + r2[...].astype(F32)) + r3[...].astype(F32)

    in_specs = [pl.BlockSpec((None, br, bc), functools.partial(lambda i, j, s: (s, i, j), s=s)) for s in range(4)]
    return pl.pallas_call(
        kern, name=name, grid=(rows // br, cols // bc), in_specs=in_specs,
        out_specs=pl.BlockSpec((br, bc), lambda i, j: (i, j)),
        out_shape=jax.ShapeDtypeStruct((rows, cols), F32), compiler_params=_params(("parallel", "parallel")),
    )(r, r, r, r)


def _my_place():
    return lax.axis_index("x"), lax.axis_index("y"), lax.axis_index("c")


def _chip_peers(x, y):
    peers = [(1 - x, y), (x, 1 - y), (1 - x, 1 - y)]
    return peers, [2 * px + py for px, py in peers]


def gather_shards(name, arrays):
    nw = len(arrays)
    ANY = pl.BlockSpec(memory_space=pl.ANY)

    def body(*refs):
        ins, outs = refs[:nw], refs[nw:2 * nw]
        send, recv, loc = refs[2 * nw:]
        x, y, c = _my_place()
        q = 2 * x + y
        peers, chips = _chip_peers(x, y)

        def remote(w, j, slot):
            return pltpu.make_async_remote_copy(
                src_ref=ins[w], dst_ref=outs[w].at[slot], send_sem=send.at[w, j], recv_sem=recv.at[w, j],
                device_id=(*peers[j], c), device_id_type=MESH_ID)

        local = [pltpu.make_async_copy(ins[w], outs[w].at[q], loc.at[w]) for w in range(nw)]
        sends = [[remote(w, j, q) for j in range(3)] for w in range(nw)]
        for w in range(nw):
            local[w].start()
            for j in range(3):
                sends[w][j].start()
        for w in range(nw):
            local[w].wait()
            for j in range(3):
                sends[w][j].wait_send()
                remote(w, j, chips[j]).wait_recv()

    return pl.pallas_call(
        body, name=name, in_specs=[ANY] * nw, out_specs=[ANY] * nw,
        out_shape=[jax.ShapeDtypeStruct((4,) + a.shape, a.dtype) for a in arrays],
        scratch_shapes=[pltpu.SemaphoreType.DMA((nw, 3)), pltpu.SemaphoreType.DMA((nw, 3)), pltpu.SemaphoreType.DMA((nw,))],
        compiler_params=pltpu.CompilerParams(has_side_effects=True),
    )(*arrays)


def scatter_slots(name, arrays, collective_id):
    nw = len(arrays)

    def body(*refs):
        ins, outs = refs[:nw], refs[nw:2 * nw]
        send, recv, loc = refs[2 * nw:]
        x, y, c = _my_place()
        q = 2 * x + y
        peers, chips = _chip_peers(x, y)
        barrier = pltpu.get_barrier_semaphore()
        for p in peers:
            pl.semaphore_signal(barrier, inc=1, device_id=(*p, c), device_id_type=MESH_ID)
        pl.semaphore_wait(barrier, 3)

        def remote(w, j, src_slot, dst_slot):
            return pltpu.make_async_remote_copy(
                src_ref=ins[w].at[src_slot], dst_ref=outs[w].at[dst_slot], send_sem=send.at[w, j], recv_sem=recv.at[w, j],
                device_id=(*peers[j], c), device_id_type=MESH_ID)

        sends = [[remote(w, j, chips[j], q) for j in range(3)] for w in range(nw)]
        own = [pltpu.make_async_copy(ins[w].at[q], outs[w].at[q], loc.at[w]) for w in range(nw)]
        for w in range(nw):
            for j in range(3):
                sends[w][j].start()
            own[w].start()
        for w in range(nw):
            for j in range(3):
                sends[w][j].wait_send()
                remote(w, j, q, chips[j]).wait_recv()
            own[w].wait()

    return pl.kernel(
        body, out_type=[jax.ShapeDtypeStruct(a.shape, a.dtype) for a in arrays],
        mesh=plsc.ScalarSubcoreMesh(axis_name="sequencer", num_cores=1), name=name,
        scratch_types=[pltpu.SemaphoreType.DMA((nw, 3)), pltpu.SemaphoreType.DMA((nw, 3)), pltpu.SemaphoreType.DMA((nw,))],
        compiler_params=pltpu.CompilerParams(collective_id=collective_id),
    )(*arrays)


def _halves_by_cols(rows):
    return rows % 32 != 0


def _half_of(ref, shape, h):
    rows, cols = shape
    if _halves_by_cols(rows):
        return ref.at[:, pl.ds(h * (cols // 2), cols // 2)]
    return ref.at[pl.ds(h * (rows // 2), rows // 2)]


def _join_halves(lo, hi, rows):
    return jnp.concatenate([lo, hi], axis=lo.ndim - 1 if _halves_by_cols(rows) else lo.ndim - 2)


def gather_two_level(name, arrays, collective_id):
    nw = len(arrays)
    shapes = [a.shape for a in arrays]

    def body(*refs):
        ins, outs = refs[:nw], refs[nw:2 * nw]
        send, recv, loc = refs[2 * nw:]
        x, y, c = _my_place()
        q = 2 * x + y
        me, sibling = (x, y, c), (x, y, 1 - c)
        peers = [(1 - x, y), (x, 1 - y), (1 - x, 1 - y)]
        chips = [2 * px + py for px, py in peers]
        barrier = pltpu.get_barrier_semaphore()
        for dev in [sibling] + [(*p, c) for p in peers]:
            pl.semaphore_signal(barrier, inc=1, device_id=dev, device_id_type=MESH_ID)
        pl.semaphore_wait(barrier, 4)

        def mine(w):
            return _half_of(ins[w], shapes[w], c)

        def landed(w, chip, half):
            return _half_of(outs[w].at[chip], shapes[w], half)

        def copy(w, k, src, chip, half, to):
            return pltpu.make_async_remote_copy(
                src_ref=src, dst_ref=landed(w, chip, half), send_sem=send.at[w, k], recv_sem=recv.at[w, k],
                device_id=to, device_id_type=MESH_ID)

        first = [[copy(w, 0, mine(w), q, c, sibling)] + [copy(w, 1 + j, mine(w), q, c, (*peers[j], c)) for j in range(3)]
                 for w in range(nw)]
        own = [pltpu.make_async_copy(mine(w), landed(w, q, c), loc.at[w]) for w in range(nw)]
        for w in range(nw):
            for cp in first[w]:
                cp.start()
            own[w].start()
        passed = []
        for w in range(nw):
            for j in range(3):
                copy(w, 1 + j, mine(w), chips[j], c, me).wait_recv()
                fwd = copy(w, 4 + j, landed(w, chips[j], c), chips[j], c, sibling)
                fwd.start()
                passed.append(fwd)
        for w in range(nw):
            copy(w, 0, mine(w), q, 1 - c, me).wait_recv()
            for j in range(3):
                copy(w, 4 + j, mine(w), chips[j], 1 - c, me).wait_recv()
        for w in range(nw):
            for cp in first[w]:
                cp.wait_send()
            own[w].wait()
        for cp in passed:
            cp.wait_send()

    out_type = [jax.ShapeDtypeStruct((4,) + a.shape, a.dtype) for a in arrays]
    return pl.kernel(
        body, out_type=out_type, mesh=plsc.ScalarSubcoreMesh(axis_name="sequencer", num_cores=1), name=name,
        scratch_types=[pltpu.SemaphoreType.DMA((nw, 7)), pltpu.SemaphoreType.DMA((nw, 7)), pltpu.SemaphoreType.DMA((nw,))],
        compiler_params=pltpu.CompilerParams(collective_id=collective_id),
    )(*arrays)


def core_swap(name, arrays):
    nw = len(arrays)
    ANY = pl.BlockSpec(memory_space=pl.ANY)

    def body(*refs):
        ins, outs = refs[:nw], refs[nw:2 * nw]
        send, recv = refs[2 * nw:]
        x, y, c = _my_place()
        copies = [pltpu.make_async_remote_copy(
            src_ref=ins[w], dst_ref=outs[w], send_sem=send.at[w], recv_sem=recv.at[w],
            device_id=(x, y, 1 - c), device_id_type=MESH_ID) for w in range(nw)]
        for cp in copies:
            cp.start()
        for cp in copies:
            cp.wait_send()
            cp.wait_recv()

    return pl.pallas_call(
        body, name=name, in_specs=[ANY] * nw, out_specs=[ANY] * nw,
        out_shape=[jax.ShapeDtypeStruct(a.shape, a.dtype) for a in arrays],
        scratch_shapes=[pltpu.SemaphoreType.DMA((nw,)), pltpu.SemaphoreType.DMA((nw,))],
        compiler_params=pltpu.CompilerParams(has_side_effects=True),
    )(*arrays)


def all_reduce_small(name, v):
    rows = v.shape[0]
    VM = pl.BlockSpec(memory_space=pltpu.VMEM)

    def body(v_ref, o_ref, buf, send, recv):
        x, y, c = _my_place()
        me = 4 * x + 2 * y + c

        def peer(kx):
            return (x ^ ((kx >> 2) & 1), y ^ ((kx >> 1) & 1), c ^ (kx & 1))

        def copy(kx, slot):
            return pltpu.make_async_remote_copy(
                src_ref=v_ref, dst_ref=buf.at[slot], send_sem=send.at[kx - 1], recv_sem=recv.at[kx - 1],
                device_id=peer(kx), device_id_type=MESH_ID)

        sends = [copy(kx, me) for kx in range(1, 8)]
        for cp in sends:
            cp.start()
        buf[me] = v_ref[...]
        for kx in range(1, 8):
            copy(kx, me ^ kx).wait_recv()
        for cp in sends:
            cp.wait_send()
        acc = buf[0]
        for d in range(1, 8):
            acc = acc + buf[d]
        o_ref[...] = acc

    return pl.pallas_call(
        body, name=name, in_specs=[VM], out_specs=VM, out_shape=jax.ShapeDtypeStruct(v.shape, F32),
        scratch_shapes=[pltpu.VMEM((8, rows, LANES), F32), pltpu.SemaphoreType.DMA((7,)), pltpu.SemaphoreType.DMA((7,))],
        compiler_params=pltpu.CompilerParams(has_side_effects=True, vmem_limit_bytes=VMEM_LIMIT),
    )(v)


def _pad_cols(a, n):
    return jnp.pad(a, ((0, 0), (0, n - a.shape[1])))


def _pad_rows(a, n):
    return jnp.pad(a, ((0, n - a.shape[0]), (0, 0)))


def _halo(u, tr):
    t, cdim = u.shape
    tails = u.reshape(t // tr, tr, cdim)[:, tr - HALO:, :]
    tails = jnp.concatenate([jnp.zeros((1, HALO, cdim), u.dtype), tails[:-1]], axis=0)
    return tails.reshape(-1, cdim)


def _unhalo(du, dhalo, tr):
    t, cdim = du.shape
    n = t // tr
    dh = dhalo.reshape(n, HALO, cdim)
    dh = jnp.concatenate([dh[1:], jnp.zeros((1, HALO, cdim), du.dtype)], axis=0)
    d3 = du.reshape(n, tr, cdim)
    d3 = jnp.concatenate([d3[:, :tr - HALO, :], d3[:, tr - HALO:, :] + dh], axis=1)
    return d3.reshape(t, cdim)


def _to_slots(g, axis):
    r, cdim = g.shape
    if axis == 0:
        return g.reshape(4, r // 4, cdim)
    return g.reshape(r, 4, cdim // 4).transpose(1, 0, 2)


def _from_slots(s, axis):
    if axis == 0:
        return s.reshape(s.shape[0] * s.shape[1], s.shape[2])
    return s.transpose(1, 0, 2).reshape(s.shape[1], 4 * s.shape[2])


BIG = ("w_in", "w_out", "xattn_wq", "xattn_wk", "xattn_wv", "xattn_wo", "ffn_w1", "ffn_w2")
TRANSPOSED = ("w_in",)
BIG_AXIS = {"w_in": 0, "w_out": 0, "xattn_wq": 0, "xattn_wk": 0, "xattn_wv": 0, "xattn_wo": 0, "ffn_w1": 1, "ffn_w2": 0}
SMALL_SHARDED = ("ssd_conv_w", "rwkv_w2", "rwkv_a2", "rwkv_g2")
GATHER_GROUPS = (("w_in",), ("w_out", "xattn_wq", "xattn_wk", "xattn_wv", "xattn_wo"), ("ffn_w1", "ffn_w2"))
REDUCE_GROUPS = (("ffn_w2", "ffn_w1"), ("xattn_wo", "xattn_wq", "xattn_wk", "xattn_wv", "w_out"),
                 ("rwkv_w2", "rwkv_a2", "rwkv_g2", "w_in"))
REDUCED = BIG + ("rwkv_w2", "rwkv_a2", "rwkv_g2")
REDUCE_AXIS = dict(BIG_AXIS, rwkv_w2=1, rwkv_a2=1, rwkv_g2=1)
WEIGHTS = ("norm_mix_g", "w_in", "ssd_conv_w", "ssd_conv_b", "ssd_dt_bias", "ssd_a_log", "ssd_d", "ssd_norm_g",
           "rwkv_mu", "rwkv_w0", "rwkv_w2", "rwkv_a0", "rwkv_a2", "rwkv_g2", "rwkv_k_k", "rwkv_k_a", "rwkv_r_k",
           "rwkv_ln_w", "rwkv_ln_b", "w_out", "norm_x_g", "norm_mem_g", "xattn_wq", "xattn_wk", "xattn_wv", "xattn_wo",
           "norm_ffn_g", "ffn_w1", "ffn_w2", "final_norm_g")


def _local_grads(x, mem, tgt, wt, full, big, reducer):
    t, d = x.shape
    w = d // 2
    nh = w // HEAD_DIM
    n_pairs = nh // 2
    ppg = n_pairs // SSD_GROUPS
    bc = SSD_GROUPS * SSD_STATE
    conv_dim = w + 2 * bc
    tr = ROW_TILE
    nt = t // tr
    tr2 = 2 * tr if t % (2 * tr) == 0 else tr
    nt2 = t // tr2
    dr = wt["rwkv_w2"].shape[0]
    ar = wt["rwkv_a2"].shape[0]
    gr = wt["rwkv_g2"].shape[0]

    big.start(0, None)
    big.start(1, None)
    h1 = norm_fwd("norm_mix", x, wt["norm_mix_g"], tr2)
    w_in_t = big.get("w_in", (h1, full))
    o = 0
    segs = {}
    for nm, width in (("z", w), ("xbc", conv_dim), ("dt", nh), ("rkv", 3 * w), ("pw", dr), ("pa", ar), ("pg", gr)):
        segs[nm] = (o, width)
        o += width
    padded = {"z": w, "xbc": conv_dim, "dt": LANES, "rkv": 3 * w, "pw": LANES, "pa": LANES, "pg": gr}
    order = ("z", "xbc", "dt", "rkv", "pw", "pa", "pg")
    w_segs = [jnp.concatenate([_pad_rows(w_in_t[segs[nm][0]:segs[nm][0] + segs[nm][1]], padded[nm]) for nm in grp], axis=0)
              for grp in (("z",), ("xbc",), ("dt",), ("rkv",), ("pw", "pa", "pg"))]
    w_perm_t = jnp.concatenate(w_segs, axis=0)
    offs = {}
    o = 0
    for nm in order:
        offs[nm] = o
        o += padded[nm]
    lora_w = 2 * LANES + gr

    mu = wt["rwkv_mu"]
    mo = 3 * w
    mu_rkv = mu[:, :mo]
    mu_lora = jnp.concatenate([_pad_cols(mu[:, mo:mo + dr], LANES), _pad_cols(mu[:, mo + dr:mo + dr + ar], LANES),
                               mu[:, mo + dr + ar:]], axis=1)
    w2p = _pad_rows(full["rwkv_w2"], LANES)
    a2p = _pad_rows(full["rwkv_a2"], LANES)
    g2 = full["rwkv_g2"]
    conv_w = full["ssd_conv_w"]
    cw = [conv_w[i:i + 1] for i in range(SSD_CONV)]
    dt_bias = _pad_cols(wt["ssd_dt_bias"], LANES)
    a_log = _pad_cols(wt["ssd_a_log"], LANES)
    d_skip = _pad_cols(wt["ssd_d"], LANES)
    r_k = wt["rwkv_r_k"].reshape(1, w)

    z, xbc, dtraw, urkv, ulora = [matmul("in_proj_%d" % i, h1, ws, tb=True) for i, ws in enumerate(w_segs)]
    big.start(2, urkv)

    halo_xbc = _halo(xbc, tr)
    ssd_pre_t = [(xbc, tr, conv_dim, 0), (halo_xbc, HALO, conv_dim, 0), (dtraw, tr, LANES, 0)]
    ssd_pre_f = cw + [wt["ssd_conv_b"], dt_bias]
    act, dt = fn_fwd("ssd_pre", _ssd_pre, nt, ssd_pre_t, ssd_pre_f, [(t, tr, conv_dim, F32), (t, tr, LANES, F32)])

    nb = w // LANES
    ssd_seq = [(act, None), (act, lambda p: nb + p // ppg), (act, lambda p: nb + SSD_GROUPS + p // ppg), (dt, lambda p: 0)]
    ssd_ppb = min(ppg, PAIRS_PER_STEP)
    rw_ppb = min(n_pairs, 2 * PAIRS_PER_STEP)

    def ssd_fn(sv, cv, hts, ids):
        return _ssd_chunks([(*s, ht) for s, ht in zip(sv, hts)], cv[0], ids)

    y_scan, ssd_states = scan_fwd("ssd_scan", ssd_fn, SSD_CHUNK, ssd_seq, [a_log], n_pairs, ssd_ppb)
    ssd_post_t = [(y_scan, tr, w, 0), (act, tr, w, 0), (z, tr, w, 0)]
    ssd_post_f = [d_skip, wt["ssd_norm_g"]]
    (y_ssd,) = fn_fwd("ssd_post", _ssd_post, nt, ssd_post_t, ssd_post_f, [(t, tr, w, BF16)])

    halo_rkv, halo_lora = _halo(urkv, tr), _halo(ulora, tr)
    rw_pre_t = [(urkv, tr, 3 * w, 0), (ulora, tr, lora_w, 0), (halo_rkv, HALO, 3 * w, 0), (halo_lora, HALO, lora_w, 0)]
    rw_pre_f = [mu_rkv, mu_lora, wt["rwkv_w0"], wt["rwkv_a0"], wt["rwkv_k_k"], wt["rwkv_k_a"], w2p, a2p, g2]
    rw = fn_fwd("rwkv_pre", _rwkv_pre, nt, rw_pre_t, rw_pre_f, [(t, tr, w, F32)] * 7)
    r_, lw_, k2_, v_, nkk_, b_, gate_ = rw
    rw_seq = [(a, None) for a in (r_, lw_, k2_, v_, nkk_, b_)]

    def rw_fn(sv, cv, hts, ids):
        return _rwkv_chunks([(*s, ht) for s, ht in zip(sv, hts)])

    yr_scan, rw_states = scan_fwd("rwkv_scan", rw_fn, RWKV_CHUNK, rw_seq, [], n_pairs, rw_ppb)
    rw_post_t = [(a, tr, w, 0) for a in (yr_scan, r_, k2_, v_, gate_)]
    rw_post_f = [r_k, wt["rwkv_ln_w"], wt["rwkv_ln_b"]]
    (y_rwkv,) = fn_fwd("rwkv_post", _rwkv_post, nt, rw_post_t, rw_post_f, [(t, tr, w, BF16)])

    ymix = jnp.concatenate([y_ssd, y_rwkv], axis=1)
    w_out = big.get("w_out", ymix)
    x1 = matmul("out_proj", ymix, w_out, resid=x)

    h2 = norm_fwd("norm_x", x1, wt["norm_x_g"], tr2)
    mrows = mem.shape[0]
    mn = norm_fwd("norm_mem", mem, wt["norm_mem_g"], mrows)
    wq, wk, wv, wo = [big.get(nm, ymix) for nm in ("xattn_wq", "xattn_wk", "xattn_wv", "xattn_wo")]
    q = matmul("xattn_q", h2, wq)
    kx = matmul("xattn_k", mn, wk)
    vx = matmul("xattn_v", mn, wv)
    (ao,) = fn_fwd("xattn_core", _attn, nt2, [(q, tr2, d, 0)], [kx, vx], [(t, tr2, d, BF16)])
    x2 = matmul("xattn_o", ao, wo, resid=x1)

    h3 = norm_fwd("norm_ffn", x2, wt["norm_ffn_g"], tr2)
    w1, w2 = big.get("ffn_w1", h3), big.get("ffn_w2", h3)
    a1 = matmul("ffn_up", h3, w1, out_dtype=BF16)
    dff = a1.shape[1]
    (f1,) = fn_fwd("ffn_act", _relu2, nt, [(a1, tr, dff, 0)], [], [(t, tr, dff, BF16)])
    x3 = matmul("ffn_down", f1, w2, resid=x2)

    dx3, dx3b, g_final, loss_tile = loss_head(x3, tgt, wt["final_norm_g"].reshape(1, d), tr2)

    grads = {"final_norm_g": g_final.reshape(d)}
    grads["ffn_w2"] = matmul("ffn_down_dw", f1, dx3b, ta=True)
    df1 = matmul("ffn_down_dx", dx3b, w2, tb=True, out_dtype=BF16)
    (da1,), _ = fn_bwd("ffn_act_bwd", _relu2, nt, [(a1, tr, dff, 0)], [], [(df1, tr, dff, 0)], lambda c: [c[0].astype(F32)],
                       [(t, tr, dff, BF16)])
    grads["ffn_w1"] = matmul("ffn_up_dw", h3, da1, out_slots=4, ta=True)
    dh3 = reducer.launch(0, grads, matmul("ffn_up_dx", da1, w1, tb=True))
    dx2, dx2b, grads["norm_ffn_g"] = norm_bwd("norm_ffn_bwd", x2, wt["norm_ffn_g"], dh3, dx3, tr2)

    grads["xattn_wo"] = matmul("xattn_o_dw", ao, dx2b, ta=True)
    dao = matmul("xattn_o_dx", dx2b, wo, tb=True)
    (dq,), (dkx, dvx) = fn_bwd("xattn_core_bwd", _attn, nt2, [(q, tr2, d, 0)], [kx, vx], [(dao, tr2, d, 0)], lambda c: c,
                               [(t, tr2, d, BF16)])
    grads["xattn_wq"] = matmul("xattn_q_dw", h2, dq, ta=True)
    dh2 = matmul("xattn_q_dx", dq, wq, tb=True)
    dkb, dvb = dkx.astype(BF16), dvx.astype(BF16)
    grads["xattn_wk"] = matmul("xattn_k_dw", mn, dkb, ta=True)
    grads["xattn_wv"] = matmul("xattn_v_dw", mn, dvb, ta=True)
    dmn = matmul("xattn_k_dx", dkb, wk, tb=True)
    dmn = matmul("xattn_v_dx", dvb, wv, tb=True, resid=dmn)
    _, _, grads["norm_mem_g"] = norm_bwd("norm_mem_bwd", mem, wt["norm_mem_g"], dmn, None, mrows)
    dx1, dx1b, grads["norm_x_g"] = norm_bwd("norm_x_bwd", x1, wt["norm_x_g"], dh2, dx2, tr2)

    grads["w_out"] = matmul("out_proj_dw", ymix, dx1b, ta=True)
    dymix = reducer.launch(1, grads, matmul("out_proj_dx", dx1b, w_out, tb=True))

    (dyr, dr1, dk1, dv1, dgate), (g_rk, grads["rwkv_ln_w"], grads["rwkv_ln_b"]) = fn_bwd(
        "rwkv_post_bwd", _rwkv_post, nt, rw_post_t, rw_post_f, [(dymix, tr, w, 1)], lambda c: c, [(t, tr, w, F32)] * 5)
    grads["rwkv_r_k"] = g_rk.reshape(wt["rwkv_r_k"].shape)
    (dr2, dlw, dk2, dv2, dnkk, db), _ = scan_bwd("rwkv_scan_bwd", rw_fn, RWKV_CHUNK, rw_seq, [], rw_states, dyr, n_pairs, rw_ppb)
    rw_ct = [(a, tr, w, 0) for a in (dr1, dr2, dlw, dk1, dk2, dv1, dv2, dnkk, db, dgate)]

    def rw_ct_fn(c):
        return (c[0] + c[1], c[2], c[3] + c[4], c[5] + c[6], c[7], c[8], c[9])

    (durkv, dulora, dhrkv, dhlora), rw_pg = fn_bwd(
        "rwkv_pre_bwd", _rwkv_pre, nt, rw_pre_t, rw_pre_f, rw_ct, rw_ct_fn,
        [(t, tr, 3 * w, F32), (t, tr, lora_w, F32), (nt * HALO, HALO, 3 * w, F32), (nt * HALO, HALO, lora_w, F32)])
    durkv = _unhalo(durkv, dhrkv, tr)
    dulora = _unhalo(dulora, dhlora, tr)
    g_mu_rkv, g_mu_lora, grads["rwkv_w0"], grads["rwkv_a0"], grads["rwkv_k_k"], grads["rwkv_k_a"], g_w2p, g_a2p, grads["rwkv_g2"] = rw_pg
    grads["rwkv_mu"] = jnp.concatenate([g_mu_rkv, g_mu_lora[:, :dr], g_mu_lora[:, LANES:LANES + ar], g_mu_lora[:, 2 * LANES:]], axis=1)
    grads["rwkv_w2"] = g_w2p[:dr]
    grads["rwkv_a2"] = g_a2p[:ar]

    (dys, dxs1, dz), (g_d, grads["ssd_norm_g"]) = fn_bwd(
        "ssd_post_bwd", _ssd_post, nt, ssd_post_t, ssd_post_f, [(dymix, tr, w, 0)], lambda c: c, [(t, tr, w, F32)] * 3)
    grads["ssd_d"] = g_d[:, :nh]
    (dxs2, dbp, dcp, ddtp), (g_alog,) = scan_bwd("ssd_scan_bwd", ssd_fn, SSD_CHUNK, ssd_seq, [a_log], ssd_states, dys, n_pairs, ssd_ppb)
    grads["ssd_a_log"] = g_alog[:, :nh]
    ssd_ct = [(dxs1, tr, w, 0), (dxs2, tr, w, 0), (dbp, tr, w, 0), (dcp, tr, w, 0), (ddtp, tr, w, 0)]

    def ssd_ct_fn(c):
        def group_sum(a):
            parts = []
            for gi in range(SSD_GROUPS):
                s = a[:, gi * ppg * LANES:(gi * ppg + 1) * LANES]
                for j in range(1, ppg):
                    s = s + a[:, (gi * ppg + j) * LANES:(gi * ppg + j + 1) * LANES]
                parts.append(s)
            return parts
        ddt = c[4][:, :LANES]
        for j in range(1, n_pairs):
            ddt = ddt + c[4][:, j * LANES:(j + 1) * LANES]
        return (jnp.concatenate([c[0] + c[1]] + group_sum(c[2]) + group_sum(c[3]), axis=1), ddt)

    (dxbc, dhxbc, ddtraw), ssd_pg = fn_bwd(
        "ssd_pre_bwd", _ssd_pre, nt, ssd_pre_t, ssd_pre_f, ssd_ct, ssd_ct_fn,
        [(t, tr, conv_dim, F32), (nt * HALO, HALO, conv_dim, F32), (t, tr, LANES, F32)])
    dxbc = _unhalo(dxbc, dhxbc, tr)
    grads["ssd_conv_w"] = jnp.concatenate(ssd_pg[:SSD_CONV], axis=0)
    grads["ssd_conv_b"] = ssd_pg[SSD_CONV]
    grads["ssd_dt_bias"] = ssd_pg[SSD_CONV + 1][:, :nh]

    du = jnp.concatenate([dz, dxbc, ddtraw, durkv, dulora], axis=1).astype(BF16)
    g_perm_t = matmul("in_proj_dw", du, h1, ta=True)
    grads["w_in"] = jnp.concatenate([g_perm_t[offs[nm]:offs[nm] + segs[nm][1]] for nm in order], axis=0)
    dh1 = matmul("in_proj_dx", du, w_perm_t)
    dh1 = reducer.launch(2, grads, dh1)
    grad_x, _, grads["norm_mix_g"] = norm_bwd("norm_mix_bwd", x, wt["norm_mix_g"], dh1, dx1, tr2)
    return loss_tile, grad_x, grads


def _pack(arrs):
    flat = jnp.concatenate([a.reshape(-1) for a in arrs])
    n = flat.shape[0]
    rows = -(-n // (8 * LANES)) * 8
    return jnp.pad(flat, (0, rows * LANES - n)).reshape(rows, LANES)


def _unpack(packed, shapes):
    flat = packed.reshape(-1)
    out, o = [], 0
    for s in shapes:
        n = math.prod(s)
        out.append(flat[o:o + n].reshape(s))
        o += n
    return out


def _as2d(a):
    return a.reshape(-1, a.shape[-1])


def _shard_view(n, a):
    return _as2d(a[0]).T if n in TRANSPOSED else _as2d(a[0])


class _GatheredWeights:
    def __init__(self, shard2d, q, c):
        self.shard2d, self.q, self.c = shard2d, q, c
        self.raw, self.ready = {}, {}

    def start(self, gi, after):
        shards = [self.shard2d[n].astype(BF16) for n in GATHER_GROUPS[gi]]
        if after is not None:
            shards, _ = lax.optimization_barrier((shards, after))
        gathered = gather_two_level("gather_weights_%d" % gi, shards, gi + 1)
        self.raw.update(zip(GATHER_GROUPS[gi], gathered))

    def get(self, name, after):
        if name not in self.ready:
            g = self.raw[name]
            if after is not None:
                g, _ = lax.optimization_barrier((g, after))
            self.ready[name] = _from_slots(g, 0) if BIG_AXIS[name] == 0 else g
        return self.ready[name]


class _GradReducer:
    def __init__(self, q, c, update):
        self.q, self.c, self.update = q, c, update
        self.pending, self.updated = {}, {}

    def launch(self, gi, grads, nxt):
        names = REDUCE_GROUPS[gi]
        slots = [grads[n] if grads[n].ndim == 3 else _to_slots(grads[n], REDUCE_AXIS[n]) for n in names]
        rows = [s.shape[1] for s in slots]
        sent = [half_call("send_half_" + n, s, 1 - self.c, None, BF16) for n, s in zip(names, slots)]
        got = core_swap("swap_halves_%d" % gi, sent)
        parts = [half_call("chip_sum_" + n, s, self.c, g, BF16) for n, s, g in zip(names, slots, got)]
        parts, nxt = lax.optimization_barrier((parts, nxt))
        slots = scatter_slots("scatter_grads_%d" % gi, parts, len(GATHER_GROUPS) + 1 + gi)
        self.pending[gi] = (slots, rows)
        return self.finish(gi - 1, nxt) if gi > 0 else nxt

    def finish(self, gi, nxt):
        names = REDUCE_GROUPS[gi]
        slots, rows = self.pending[gi]
        halves = []
        for n, s in zip(names, slots):
            halves.append(sum_slots("sum_" + n, s))
        others = core_swap("swap_reduced_%d" % gi, halves)
        lo = [jnp.where(self.c == 0, mine, other) for mine, other in zip(halves, others)]
        hi = [jnp.where(self.c == 0, other, mine) for mine, other in zip(halves, others)]
        results = [self.update(n, _join_halves(l, h, r)) for n, l, h, r in zip(names, lo, hi, rows)]
        if nxt is not None:
            results, nxt = lax.optimization_barrier((results, nxt))
        self.updated.update(zip(names, results))
        return nxt


def _step(a):
    x, mem, tgt = a["x"][0], a["mem"][0], a["loss_target"][0]
    q = 2 * lax.axis_index("x") + lax.axis_index("y")

    shard2d = {n: _shard_view(n, a[n]) for n in BIG}
    small_sh = {n: _as2d(a[n][0]) for n in SMALL_SHARDED}
    c = lax.axis_index("c")
    full = {}
    big = _GatheredWeights(shard2d, q, c)
    gathered = gather_shards("gather_small", [small_sh[n] for n in SMALL_SHARDED])
    for n, g in zip(SMALL_SHARDED, gathered):
        full[n] = _from_slots(g, 1)

    wt = {n: (a[n] if a[n].ndim <= 2 else a[n][0]) for n in WEIGHTS if n not in BIG and n not in SMALL_SHARDED}
    for n in SMALL_SHARDED:
        wt[n] = small_sh[n]
    shards = dict(shard2d)
    shards.update({n: small_sh[n] for n in REDUCED if n not in BIG})

    def update(n, gsum):
        return adamw("adamw_" + n, shards[n], _shard_view(n, a["m_" + n]), _shard_view(n, a["v_" + n]), gsum)

    reducer = _GradReducer(q, c, update)
    loss_tile, grad_x, grads = _local_grads(x, mem, tgt, wt, full, big, reducer)
    reducer.finish(len(REDUCE_GROUPS) - 1, None)
    out = {}
    for n, vals in reducer.updated.items():
        for key, val in zip(("grad_", "delta_", "new_m_", "new_v_"), vals):
            out[key + n] = (val.T if n in TRANSPOSED else val).reshape(a[n].shape)

    small = [n for n in WEIGHTS if n not in REDUCED]
    red = _unpack(all_reduce_small("all_reduce_small", _pack([grads[n] for n in small])), [grads[n].shape for n in small])
    g_loc = {}
    for n, g in zip(small, red):
        if n in SMALL_SHARDED:
            cols = g.shape[1] // 4
            g = lax.dynamic_slice_in_dim(g, q * cols, cols, axis=1)
        g_loc[n] = g.reshape(a[n].shape)
    res = adamw("adamw_small", *[_pack([src[n] for n in small]) for src in
                                 ({n: a[n] for n in small}, {n: a["m_" + n] for n in small}, {n: a["v_" + n] for n in small})],
                _pack([g_loc[n] for n in small]))
    shapes = [a[n].shape for n in small]
    for key, packed in zip(("grad_", "delta_", "new_m_", "new_v_"), res):
        for n, val in zip(small, _unpack(packed, shapes)):
            out[key + n] = val

    loss = lax.psum(loss_tile[0, 0], ("x", "y", "c"))
    ordered = [loss, grad_x.reshape(a["x"].shape)]
    for key in ("grad_", "delta_", "new_m_", "new_v_"):
        ordered += [out[key + n] for n in WEIGHTS]
    return tuple(ordered)


def kernel(x, mem, norm_mix_g, w_in, ssd_conv_w, ssd_conv_b, ssd_dt_bias, ssd_a_log, ssd_d, ssd_norm_g, rwkv_mu, rwkv_w0, rwkv_w2, rwkv_a0, rwkv_a2, rwkv_g2, rwkv_k_k, rwkv_k_a, rwkv_r_k, rwkv_ln_w, rwkv_ln_b, w_out, norm_x_g, norm_mem_g, xattn_wq, xattn_wk, xattn_wv, xattn_wo, norm_ffn_g, ffn_w1, ffn_w2, final_norm_g, loss_target, m_norm_mix_g, m_w_in, m_ssd_conv_w, m_ssd_conv_b, m_ssd_dt_bias, m_ssd_a_log, m_ssd_d, m_ssd_norm_g, m_rwkv_mu, m_rwkv_w0, m_rwkv_w2, m_rwkv_a0, m_rwkv_a2, m_rwkv_g2, m_rwkv_k_k, m_rwkv_k_a, m_rwkv_r_k, m_rwkv_ln_w, m_rwkv_ln_b, m_w_out, m_norm_x_g, m_norm_mem_g, m_xattn_wq, m_xattn_wk, m_xattn_wv, m_xattn_wo, m_norm_ffn_g, m_ffn_w1, m_ffn_w2, m_final_norm_g, v_norm_mix_g, v_w_in, v_ssd_conv_w, v_ssd_conv_b, v_ssd_dt_bias, v_ssd_a_log, v_ssd_d, v_ssd_norm_g, v_rwkv_mu, v_rwkv_w0, v_rwkv_w2, v_rwkv_a0, v_rwkv_a2, v_rwkv_g2, v_rwkv_k_k, v_rwkv_k_a, v_rwkv_r_k, v_rwkv_ln_w, v_rwkv_ln_b, v_w_out, v_norm_x_g, v_norm_mem_g, v_xattn_wq, v_xattn_wk, v_xattn_wv, v_xattn_wo, v_norm_ffn_g, v_ffn_w1, v_ffn_w2, v_final_norm_g):
    return _step(dict(locals()))
```

```python
import functools
import math

import jax
import jax.numpy as jnp
from jax import lax
from jax.experimental import pallas as pl
from jax.experimental.pallas import tpu as pltpu
from jax.experimental.pallas import tpu_sc as plsc

F32 = jnp.float32
BF16 = jnp.bfloat16
HIGHEST = lax.Precision.HIGHEST
MESH_ID = pl.DeviceIdType.MESH

NORM_EPS = 1e-6
RWKV_LN_EPS = 64e-5
HEAD_DIM = 64
LANES = 128
SSD_STATE = 128
SSD_CHUNK = 128
SSD_GROUPS = 2
SSD_CONV = 4
RWKV_CHUNK = 64
HALO = 8
ROW_TILE = 128
PAIRS_PER_STEP = 4
XATTN_HEADS = 4
RWKV_PASSES = 1
VMEM_LIMIT = 56 * 1024 * 1024
MATMUL_VMEM = 40 * 1024 * 1024

ADAM_LR = 0.001
ADAM_B1 = 0.9
ADAM_B2 = 0.999
ADAM_EPS = 1e-08
ADAM_WD = 0.01
ADAM_STEP = 10


def _dims(ca, cb):
    return (((ca,), (cb,)), ((), ()))


def _split_bf16(a):
    hi = a.astype(BF16)
    lo = (a - hi.astype(F32)).astype(BF16)
    return hi, lo


def _mm_impl(a, b, ca, cb, passes):
    dn = _dims(ca, cb)
    if passes == 1:
        return lax.dot_general(a.astype(BF16), b.astype(BF16), dn, preferred_element_type=F32)
    ah, al = _split_bf16(a)
    bh, bl = _split_bf16(b)
    out = lax.dot_general(ah, bh, dn, preferred_element_type=F32)
    out = out + lax.dot_general(ah, bl, dn, preferred_element_type=F32)
    return out + lax.dot_general(al, bh, dn, preferred_element_type=F32)


@functools.partial(jax.custom_vjp, nondiff_argnums=(2, 3, 4))
def mm(a, b, ca, cb, passes):
    return _mm_impl(a, b, ca, cb, passes)


def _mm_fwd(a, b, ca, cb, passes):
    return _mm_impl(a, b, ca, cb, passes), (a, b)


def _mm_bwd(ca, cb, passes, res, g):
    a, b = res
    da = mm(g, b, 1, 1 - cb, passes) if ca == 1 else mm(b, g, 1 - cb, 1, passes)
    db = mm(a, g, 1 - ca, 0, passes) if cb == 0 else mm(g, a, 0, 1 - ca, passes)
    return da, db


mm.defvjp(_mm_fwd, _mm_bwd)


def _dot_exact(a, b):
    return lax.dot_general(a, b, _dims(1, 0), precision=HIGHEST, preferred_element_type=F32)


def _iota(shape, dim):
    return lax.broadcasted_iota(jnp.int32, shape, dim)


def _sigmoid(x):
    return 1.0 / (1.0 + jnp.exp(-x))


def _silu(x):
    return x * _sigmoid(x)


def _softplus(x):
    return jnp.maximum(x, 0.0) + jnp.log(1.0 + jnp.exp(-jnp.abs(x)))


def _rms(x, g):
    return x * lax.rsqrt(jnp.mean(x * x, axis=-1, keepdims=True) + NORM_EPS) * g


def _select_mm(x, sel):
    hi = x.astype(BF16)
    r1 = x - hi.astype(F32)
    mid = r1.astype(BF16)
    lo = (r1 - mid.astype(F32)).astype(BF16)
    dn = _dims(1, 0)
    out = lax.dot_general(hi, sel, dn, preferred_element_type=F32)
    out = out + lax.dot_general(mid, sel, dn, preferred_element_type=F32)
    return out + lax.dot_general(lo, sel, dn, preferred_element_type=F32)


def _head_sum_impl(x, n):
    sel = (_iota((n, LANES), 0) // HEAD_DIM == _iota((n, LANES), 1)).astype(BF16)
    return _select_mm(x, sel)


def _head_expand_impl(s, n):
    sel = (_iota((LANES, n), 1) // HEAD_DIM == _iota((LANES, n), 0)).astype(BF16)
    return _select_mm(s, sel)


@functools.partial(jax.custom_vjp, nondiff_argnums=(1,))
def _head_sum_n(x, n):
    return _head_sum_impl(x, n)


@functools.partial(jax.custom_vjp, nondiff_argnums=(1,))
def _head_expand(s, n):
    return _head_expand_impl(s, n)


_head_sum_n.defvjp(lambda x, n: (_head_sum_impl(x, n), None), lambda n, _, g: (_head_expand(g, n),))
_head_expand.defvjp(lambda s, n: (_head_expand_impl(s, n), None), lambda n, _, g: (_head_sum_n(g, n),))


def _head_sum(x):
    return _head_sum_n(x, x.shape[1])


def _row_vector_expand(v, n):
    v8 = jnp.broadcast_to(v, (8, LANES))
    return jnp.sum(_head_expand(v8, n), axis=0, keepdims=True) * 0.125


def _shift_rows_impl(u, halo, s):
    rolled = pltpu.roll(u, s, 0)
    top = jnp.where(_iota((HALO, 1), 0) < s, pltpu.roll(halo, s, 0), rolled[:HALO])
    return jnp.concatenate([top, rolled[HALO:]], axis=0)


@functools.partial(jax.custom_vjp, nondiff_argnums=(2,))
def _shift_rows(u, halo, s):
    return _shift_rows_impl(u, halo, s)


def _shift_rows_bwd(s, _, g):
    tr = g.shape[0]
    rolled = pltpu.roll(g, tr - s, 0)
    hrow = _iota((HALO, 1), 0)
    bottom = jnp.where(hrow < HALO - s, rolled[tr - HALO:], 0.0)
    dhalo = jnp.where(hrow >= HALO - s, pltpu.roll(g[:HALO], HALO - s, 0), 0.0)
    return jnp.concatenate([rolled[:tr - HALO], bottom], axis=0), dhalo


_shift_rows.defvjp(lambda u, halo, s: (_shift_rows_impl(u, halo, s), None), _shift_rows_bwd)


def _params(sem):
    return pltpu.CompilerParams(dimension_semantics=sem, vmem_limit_bytes=VMEM_LIMIT)


def row_call(name, body, n_tiles, tiled, full, out_tiled, out_acc):
    nt, nf, no, na = len(tiled), len(full), len(out_tiled), len(out_acc)

    def kern(*refs):
        tv = [r[...] for r in refs[:nt]]
        fv = [r[...] for r in refs[nt:nt + nf]]
        outs, accs = body(tv, fv)
        for r, v in zip(refs[nt + nf:nt + nf + no], outs):
            r[...] = v.astype(r.dtype)
        if na:
            a_refs = refs[nt + nf + no:]
            first = pl.program_id(0) == 0

            @pl.when(first)
            def _():
                for r, v in zip(a_refs, accs):
                    r[...] = v

            @pl.when(jnp.logical_not(first))
            def _():
                for r, v in zip(a_refs, accs):
                    r[...] += v

    in_specs = [pl.BlockSpec((rt, w), functools.partial(lambda i, cb: (i, cb), cb=cb)) for (_, rt, w, cb) in tiled]
    in_specs += [pl.BlockSpec(a.shape, lambda i: (0, 0)) for a in full]
    out_specs = [pl.BlockSpec((rt, w), lambda i: (i, 0)) for (_, rt, w, _) in out_tiled]
    out_specs += [pl.BlockSpec(s, lambda i: (0, 0)) for s in out_acc]
    out_shape = [jax.ShapeDtypeStruct((rows, w), dt) for (rows, _, w, dt) in out_tiled]
    out_shape += [jax.ShapeDtypeStruct(s, F32) for s in out_acc]
    res = pl.pallas_call(
        kern, name=name, grid=(n_tiles,), in_specs=in_specs, out_specs=out_specs, out_shape=out_shape,
        compiler_params=_params(("arbitrary",)),
    )(*[t[0] for t in tiled], *full)
    return list(res[:no]), list(res[no:])


def _pick(dim, cands):
    for c in cands:
        if dim % c == 0:
            return c
    return dim


def matmul(name, a, b, tb=False, resid=None, out_dtype=F32, out_slots=1, ta=False):
    (k, m) = a.shape if ta else a.shape[::-1]
    b_slots = b.shape[0] if b.ndim == 3 else 1
    n = b.shape[-2] if tb else b.shape[-1] * b_slots
    has_resid = resid is not None
    out_bytes = jnp.dtype(out_dtype).itemsize
    sizes = (2048, 1024, 896, 768, 512, 384, 256, 128)
    tm = _pick(m, sizes[1:])
    tn = _pick(n // max(out_slots, 1 if tb else b_slots), sizes[1:])

    def vmem_bytes(tk):
        return 2 * 2 * tk * (tm + tn) + tm * tn * (2 * out_bytes + 4 + (8 if has_resid else 0))

    k_slot = k // b_slots if tb else k
    tk = next((c for c in sizes if k_slot % c == 0 and vmem_bytes(c) <= MATMUL_VMEM), LANES)
    nk = k // tk
    n_per = n // (b_slots if not tb else 1) // tn
    k_per = k_slot // tk
    o_per = n // out_slots // tn

    def kern(*refs):
        a_ref, b_ref = refs[0], refs[1]
        o_ref, acc = refs[-2], refs[-1]
        kk = pl.program_id(2)
        part = lax.dot_general(a_ref[...], b_ref[...], _dims(0 if ta else 1, 1 if tb else 0), preferred_element_type=F32)

        def finish(out):
            if has_resid:
                out = out + refs[2][...]
            o_ref[...] = out.astype(o_ref.dtype)

        if nk == 1:
            finish(part)
            return

        @pl.when(kk == 0)
        def _():
            acc[...] = part

        @pl.when(jnp.logical_and(kk > 0, kk < nk - 1))
        def _():
            acc[...] += part

        @pl.when(kk == nk - 1)
        def _():
            finish(acc[...] + part)

    in_specs = [pl.BlockSpec((tk, tm), lambda i, j, kk: (kk, i)) if ta else pl.BlockSpec((tm, tk), lambda i, j, kk: (i, kk))]
    if b.ndim == 3 and tb:
        in_specs.append(pl.BlockSpec((None, tn, tk), lambda i, j, kk: (kk // k_per, j, kk % k_per)))
    elif b.ndim == 3:
        in_specs.append(pl.BlockSpec((None, tk, tn), lambda i, j, kk: (j // n_per, kk, j % n_per)))
    elif tb:
        in_specs.append(pl.BlockSpec((tn, tk), lambda i, j, kk: (j, kk)))
    else:
        in_specs.append(pl.BlockSpec((tk, tn), lambda i, j, kk: (kk, j)))
    args = [a, b]
    if has_resid:
        in_specs.append(pl.BlockSpec((tm, tn), lambda i, j, kk: (i, j)))
        args.append(resid)
    if out_slots > 1:
        out_spec = pl.BlockSpec((None, tm, tn), lambda i, j, kk: (j // o_per, i, j % o_per))
        out_shape = jax.ShapeDtypeStruct((out_slots, m, n // out_slots), out_dtype)
    else:
        out_spec = pl.BlockSpec((tm, tn), lambda i, j, kk: (i, j))
        out_shape = jax.ShapeDtypeStruct((m, n), out_dtype)
    return pl.pallas_call(
        kern, name=name, grid=(m // tm, n // tn, nk), in_specs=in_specs,
        out_specs=out_spec, out_shape=out_shape,
        scratch_shapes=[pltpu.VMEM((tm, tn), F32)],
        compiler_params=_params(("parallel", "parallel", "arbitrary")),
    )(*args)


def norm_fwd(name, x, g, tr):
    def body(tv, fv):
        return [_rms(tv[0], fv[0])], []
    rows, d = x.shape
    (h,), _ = row_call(name, body, rows // tr, [(x, tr, d, 0)], [g], [(rows, tr, d, BF16)], [])
    return h


def norm_bwd(name, x, g, dh, extra, tr):
    def body(tv, fv):
        _, vjp = jax.vjp(_rms, tv[0], fv[0])
        dx, dg = vjp(tv[1])
        if extra is not None:
            dx = dx + tv[2]
        return [dx, dx], [dg]
    rows, d = x.shape
    tiled = [(x, tr, d, 0), (dh, tr, d, 0)] + ([(extra, tr, d, 0)] if extra is not None else [])
    (dx, dxb), (dg,) = row_call(name, body, rows // tr, tiled, [g], [(rows, tr, d, F32), (rows, tr, d, BF16)], [g.shape])
    return dx, dxb, dg


def _ssd_pre(xbc, halo, dtraw, w0, w1, w2, w3, cb, dtb):
    y = w3 * xbc + w2 * _shift_rows(xbc, halo, 1) + w1 * _shift_rows(xbc, halo, 2) + w0 * _shift_rows(xbc, halo, 3) + cb
    return _silu(y), _softplus(dtraw + dtb)


def _ssd_post(ys, xs, z, dskip, ng):
    w = ys.shape[1]
    y = (ys + xs * _row_vector_expand(dskip, w)) * _silu(z)
    gw = w // SSD_GROUPS
    parts = []
    for gi in range(SSD_GROUPS):
        yg = y[:, gi * gw:(gi + 1) * gw]
        parts.append(yg * lax.rsqrt(jnp.mean(yg * yg, axis=-1, keepdims=True) + NORM_EPS))
    return jnp.concatenate(parts, axis=1) * ng


def _rwkv_pre(urkv, ulora, hrkv, hlora, mu_rkv, mu_lora, w0, a0, kkw, kaw, w2p, a2p, g2):
    w = w0.shape[1]
    urkv = urkv + (_shift_rows(urkv, hrkv, 1) - urkv) * mu_rkv
    ulora = ulora + (_shift_rows(ulora, hlora, 1) - ulora) * mu_lora
    r, k, v = urkv[:, :w], urkv[:, w:2 * w], urkv[:, 2 * w:]
    pw, pa, pg = ulora[:, :LANES], ulora[:, LANES:2 * LANES], ulora[:, 2 * LANES:]
    w_log = -_softplus(-(w0 + mm(jnp.tanh(pw), w2p, 1, 0, 1))) - 0.5
    lw = -jnp.exp(w_log)
    iclr = _sigmoid(a0 + mm(pa, a2p, 1, 0, 1))
    gate = mm(_sigmoid(pg), g2, 1, 0, 1)
    kk = k * kkw
    kk = kk / jnp.maximum(jnp.sqrt(_head_expand(_head_sum(kk * kk), w)), 1e-12)
    k2 = k * (1.0 + (iclr - 1.0) * kaw)
    return r, lw, k2, v, -kk, kk * iclr, gate


def _rwkv_post(ys, r, k2, v, gate, rk, lnw, lnb):
    w = ys.shape[1]
    inv = 1.0 / HEAD_DIM
    mean = _head_expand(_head_sum(ys), w) * inv
    d = ys - mean
    var = _head_expand(_head_sum(d * d), w) * inv
    yn = d * lax.rsqrt(var + RWKV_LN_EPS) * lnw + lnb
    bonus = _head_expand(_head_sum(r * k2 * rk), w) * v
    return (yn + bonus) * gate


def _attn(q, k, v):
    d = q.shape[1]
    hd = d // XATTN_HEADS
    outs = []
    for h in range(XATTN_HEADS):
        sl = slice(h * hd, (h + 1) * hd)
        s = mm(q[:, sl], k[:, sl], 1, 1, 1) * (hd ** -0.5)
        s = s - jnp.max(s, axis=-1, keepdims=True)
        p = jnp.exp(s)
        p = p / jnp.sum(p, axis=-1, keepdims=True)
        outs.append(mm(p, v[:, sl], 1, 0, 1))
    return jnp.concatenate(outs, axis=1)


def _relu2(a):
    return jnp.square(jnp.maximum(a.astype(F32), 0.0))


def fn_fwd(name, fn, n_tiles, tiled, full, out_tiled):
    def body(tv, fv):
        outs = fn(*tv, *fv)
        return (list(outs) if isinstance(outs, (tuple, list)) else [outs]), []
    outs, _ = row_call(name, body, n_tiles, tiled, full, out_tiled, [])
    return outs


def fn_bwd(name, fn, n_tiles, tiled, full, cts, ct_fn, out_tiled):
    nt = len(tiled)

    def body(tv, fv):
        outs, vjp = jax.vjp(fn, *tv[:nt], *fv)
        ct = ct_fn(tv[nt:])
        grads = vjp(tuple(ct) if isinstance(outs, (tuple, list)) else ct[0])
        return list(grads[:nt]), list(grads[nt:])
    return row_call(name, body, n_tiles, tiled + cts, full, out_tiled, [f.shape for f in full])


def _ssd_chunks(pairs, a_log, ids):
    q = pairs[0][0].shape[0]
    lane = _iota((1, LANES), 1)
    row = _iota((q, 1), 0)
    tril = _iota((q, q), 0) >= _iota((q, q), 1)
    half = lane < HEAD_DIM
    not_half = jnp.logical_not(half)
    n = len(pairs)
    bm, cm, dt_all = pairs[0][1], pairs[0][2], pairs[0][3]
    da = dt_all * (-jnp.exp(a_log))
    cs = _dot_exact(tril.astype(F32), da)

    def col(mat, h):
        return jnp.sum(jnp.where(lane == h, mat, 0.0), axis=1, keepdims=True)

    cs0 = [col(cs, 2 * p) for p in ids]
    cs1 = [col(cs, 2 * p + 1) for p in ids]
    xdt = [pairs[j][0] * jnp.where(half, col(dt_all, 2 * p), col(dt_all, 2 * p + 1)) for j, p in enumerate(ids)]
    csx = [jnp.where(half, a0, a1) for a0, a1 in zip(cs0, cs1)]
    last = [jnp.sum(jnp.where(row == q - 1, c_, 0.0), axis=0, keepdims=True) for c_ in csx]
    cb = mm(cm, bm, 1, 1, 1)
    y0 = [mm(cm, pairs[j][4], 1, 0, 1) for j in range(n)]
    st = [mm(bm, xdt[j] * jnp.exp(last[j] - csx[j]), 0, 0, 1) for j in range(n)]

    def decay(csh):
        csl = jnp.broadcast_to(csh, (q, q))
        return jnp.where(tril, jnp.exp(jnp.where(tril, csl - csl.T, 0.0)), 0.0)

    lm = [(decay(cs0[j]), decay(cs1[j])) for j in range(n)]
    yd = [(mm(cb * lm[j][0], xdt[j], 1, 0, 1), mm(cb * lm[j][1], xdt[j], 1, 0, 1)) for j in range(n)]
    out = []
    for j in range(n):
        y = y0[j] * jnp.exp(csx[j]) + jnp.where(half, yd[j][0], 0.0) + jnp.where(not_half, yd[j][1], 0.0)
        out.append((y, pairs[j][4] * jnp.exp(last[j]) + st[j]))
    return out


def _unit_lower_inverses_impl(mats):
    c = mats[0].shape[0]
    eye = (_iota((c, c), 0) == _iota((c, c), 1)).astype(F32)
    tm = [eye + a_ for a_ in mats]
    pm = mats
    for _ in range(int(math.log2(c)) - 1):
        pm = [mm(p_, p_, 1, 0, RWKV_PASSES) for p_ in pm]
        tm = [t_ + mm(t_, p_, 1, 0, RWKV_PASSES) for t_, p_ in zip(tm, pm)]
    return tm


@jax.custom_vjp
def _unit_lower_inverses(mats):
    return _unit_lower_inverses_impl(mats)


def _unit_lower_inverses_fwd(mats):
    tm = _unit_lower_inverses_impl(mats)
    return tm, tm


def _unit_lower_inverses_bwd(tm, g):
    left = [mm(t_, g_, 0, 0, RWKV_PASSES) for t_, g_ in zip(tm, g)]
    return ([mm(l_, t_, 1, 1, RWKV_PASSES) for l_, t_ in zip(left, tm)],)


_unit_lower_inverses.defvjp(_unit_lower_inverses_fwd, _unit_lower_inverses_bwd)


def _rwkv_chunks(pairs):
    c = pairs[0][0].shape[0]
    ps = RWKV_PASSES
    lane = _iota((1, LANES), 1)
    row = _iota((c, 1), 0)
    ri, ci = _iota((c, c), 0), _iota((c, c), 1)
    tril_i, tril_s = ri >= ci, ri > ci
    half = lane < HEAD_DIM
    halves = (half, jnp.logical_not(half))
    bd = (_iota((LANES, LANES), 0) < HEAD_DIM) == (_iota((LANES, LANES), 1) < HEAD_DIM)
    tri = tril_i.astype(F32)
    n = len(pairs)
    heads = [(j, hm) for j in range(n) for hm in halves]

    cum = [_dot_exact(tri, p[1]) for p in pairs]
    at = [p[4] * jnp.exp(cm - p[1]) for p, cm in zip(pairs, cum)]
    en = [jnp.exp(-cm) for cm in cum]
    bt = [p[5] * e for p, e in zip(pairs, en)]
    kt = [p[2] * e for p, e in zip(pairs, en)]
    rt = [p[0] * jnp.exp(cm) for p, cm in zip(pairs, cum)]
    ah = [mm(at[j], pairs[j][6], 1, 1, ps) for j in range(n)]
    y = [mm(rt[j], pairs[j][6], 1, 1, ps) for j in range(n)]
    atm = [jnp.where(hm, at[j], 0.0) for j, hm in heads]
    rtm = [jnp.where(hm, rt[j], 0.0) for j, hm in heads]
    aab = [jnp.where(tril_s, mm(atm[i], bt[j], 1, 1, ps), 0.0) for i, (j, _) in enumerate(heads)]
    aak = [jnp.where(tril_s, mm(atm[i], kt[j], 1, 1, ps), 0.0) for i, (j, _) in enumerate(heads)]
    arb = [jnp.where(tril_i, mm(rtm[i], bt[j], 1, 1, ps), 0.0) for i, (j, _) in enumerate(heads)]
    ark = [jnp.where(tril_i, mm(rtm[i], kt[j], 1, 1, ps), 0.0) for i, (j, _) in enumerate(heads)]
    rhs = [ah[j] + mm(aak[i], pairs[j][3], 1, 0, ps) for i, (j, _) in enumerate(heads)]
    yv = [mm(ark[i], pairs[j][3], 1, 0, ps) for i, (j, _) in enumerate(heads)]
    tm = _unit_lower_inverses(aab)
    uh =[mm(tm[i], rhs[i], 1, 0, ps) for i in range(len(heads))]
    u = [jnp.where(half, uh[2 * j], uh[2 * j + 1]) for j in range(n)]
    yu = [mm(arb[i], u[j], 1, 0, ps) for i, (j, _) in enumerate(heads)]
    out = []
    for j in range(n):
        yj = y[j] + jnp.where(half, yu[2 * j] + yv[2 * j], yu[2 * j + 1] + yv[2 * j + 1])
        plast = jnp.sum(jnp.where(row == c - 1, cum[j], 0.0), axis=0, keepdims=True)
        upd = pairs[j][6] + mm(u[j], bt[j], 0, 0, ps) + mm(pairs[j][3], kt[j], 0, 0, ps)
        out.append((yj, jnp.where(bd, upd * jnp.exp(plast), 0.0)))
    return out


def _seq_spec(chunk, ppb, col, row_of):
    if col is None:
        return pl.BlockSpec((chunk, ppb * LANES), lambda pb, i: (row_of(i), pb))
    return pl.BlockSpec((chunk, LANES), lambda pb, i: (row_of(i), col(pb * ppb)))


def _pair_vals(refs, seq_in, j):
    return [r[...] if col is not None else r[:, j * LANES:(j + 1) * LANES] for r, (_, col) in zip(refs, seq_in)]


def scan_fwd(name, chunk_fn, chunk, seq_in, const_in, n_pairs, ppb):
    t = seq_in[0][0].shape[0]
    nc = t // chunk
    ns, ncst = len(seq_in), len(const_in)

    def kern(*refs):
        y_ref, st_ref, ht = refs[ns + ncst], refs[ns + ncst + 1], refs[ns + ncst + 2]

        @pl.when(pl.program_id(1) == 0)
        def _():
            ht[...] = jnp.zeros_like(ht)

        cv = [r[...] for r in refs[ns:ns + ncst]]
        h0 = [ht[j] for j in range(ppb)]
        for j in range(ppb):
            st_ref[j] = h0[j]
        sv = [_pair_vals(refs[:ns], seq_in, j) for j in range(ppb)]
        outs = chunk_fn(sv, cv, h0, [pl.program_id(0) * ppb + j for j in range(ppb)])
        for j, (y, hn) in enumerate(outs):
            y_ref[:, j * LANES:(j + 1) * LANES] = y
            ht[j] = hn

    in_specs = [_seq_spec(chunk, ppb, col, lambda i: i) for (_, col) in seq_in]
    in_specs += [pl.BlockSpec(a.shape, lambda pb, i: (0, 0)) for a in const_in]
    return pl.pallas_call(
        kern, name=name, grid=(n_pairs // ppb, nc), in_specs=in_specs,
        out_specs=[pl.BlockSpec((chunk, ppb * LANES), lambda pb, i: (i, pb)),
                   pl.BlockSpec((ppb, None, LANES, LANES), lambda pb, i: (pb, i, 0, 0))],
        out_shape=[jax.ShapeDtypeStruct((t, n_pairs * LANES), F32), jax.ShapeDtypeStruct((n_pairs, nc, LANES, LANES), F32)],
        scratch_shapes=[pltpu.VMEM((ppb, LANES, LANES), F32)],
        compiler_params=_params(("arbitrary", "arbitrary")),
    )(*[s[0] for s in seq_in], *const_in)


def scan_bwd(name, chunk_fn, chunk, seq_in, const_in, states, dy, n_pairs, ppb):
    t = dy.shape[0]
    nc = t // chunk
    ns, ncst = len(seq_in), len(const_in)

    def kern(*refs):
        seq_refs, cst_refs = refs[:ns], refs[ns:ns + ncst]
        st_ref, dy_ref = refs[ns + ncst], refs[ns + ncst + 1]
        o = ns + ncst + 2
        dseq_refs, dcst_refs, dht = refs[o:o + ns], refs[o + ns:o + ns + ncst], refs[o + ns + ncst]
        pb, i = pl.program_id(0), pl.program_id(1)

        @pl.when(i == 0)
        def _():
            dht[...] = jnp.zeros_like(dht)

        ids = [pb * ppb + j for j in range(ppb)]
        lanes = [slice(j * LANES, (j + 1) * LANES) for j in range(ppb)]

        def fn(*flat):
            sv = [list(flat[j * ns:(j + 1) * ns]) for j in range(ppb)]
            outs = chunk_fn(sv, list(flat[ppb * ns:ppb * ns + ncst]), list(flat[ppb * ns + ncst:]), ids)
            return tuple(y for y, _ in outs), tuple(h for _, h in outs)

        flat_in = [v for j in range(ppb) for v in _pair_vals(seq_refs, seq_in, j)]
        flat_in += [r[...] for r in cst_refs] + [st_ref[j] for j in range(ppb)]
        _, vjp = jax.vjp(fn, *flat_in)
        grads = vjp((tuple(dy_ref[:, ln] for ln in lanes), tuple(dht[j] for j in range(ppb))))
        for j in range(ppb):
            for r, g in zip(dseq_refs, grads[j * ns:(j + 1) * ns]):
                r[:, lanes[j]] = g
            dht[j] = grads[ppb * ns + ncst + j]
        dcv = grads[ppb * ns:ppb * ns + ncst]
        if ncst:
            first = jnp.logical_and(pb == 0, i == 0)

            @pl.when(first)
            def _():
                for r, g in zip(dcst_refs, dcv):
                    r[...] = g

            @pl.when(jnp.logical_not(first))
            def _():
                for r, g in zip(dcst_refs, dcv):
                    r[...] += g

    rev = lambda i: nc - 1 - i
    wide = pl.BlockSpec((chunk, ppb * LANES), lambda pb, i: (rev(i), pb))
    in_specs = [_seq_spec(chunk, ppb, col, rev) for (_, col) in seq_in]
    in_specs += [pl.BlockSpec(a.shape, lambda pb, i: (0, 0)) for a in const_in]
    in_specs += [pl.BlockSpec((ppb, None, LANES, LANES), lambda pb, i: (pb, rev(i), 0, 0)), wide]
    out_specs = [wide for _ in seq_in]
    out_specs += [pl.BlockSpec(a.shape, lambda pb, i: (0, 0)) for a in const_in]
    out_shape = [jax.ShapeDtypeStruct((t, n_pairs * LANES), F32) for _ in seq_in]
    out_shape += [jax.ShapeDtypeStruct(a.shape, F32) for a in const_in]
    res = pl.pallas_call(
        kern, name=name, grid=(n_pairs // ppb, nc), in_specs=in_specs, out_specs=out_specs, out_shape=out_shape,
        scratch_shapes=[pltpu.VMEM((ppb, LANES, LANES), F32)],
        compiler_params=_params(("arbitrary", "arbitrary")),
    )(*[s[0] for s in seq_in], *const_in, states, dy)
    return list(res[:ns]), list(res[ns:])


def loss_head(x3, tgt, g, tr):
    rows, d = x3.shape

    def body(tv, fv):
        def f(x, gg):
            e = jnp.square(_rms(x, gg) - tv[1])
            return 0.5 * jnp.sum(jnp.mean(e, axis=-1, keepdims=True), axis=0, keepdims=True)
        l, vjp = jax.vjp(f, tv[0], fv[0])
        dx, dg = vjp(jnp.ones((1, 1), F32))
        return [dx, dx], [dg, jnp.broadcast_to(l, (8, LANES))]
    (dx, dxb), (dg, l) = row_call("loss_head", body, rows // tr, [(x3, tr, d, 0), (tgt, tr, d, 0)], [g],
                                  [(rows, tr, d, F32), (rows, tr, d, BF16)], [g.shape, (8, LANES)])
    return dx, dxb, dg, l


def _adam_math(w, g, m, v):
    m = ADAM_B1 * m + (1.0 - ADAM_B1) * g
    v = ADAM_B2 * v + (1.0 - ADAM_B2) * jnp.square(g)
    m_hat = m / (1.0 - ADAM_B1 ** ADAM_STEP)
    v_hat = v / (1.0 - ADAM_B2 ** ADAM_STEP)
    delta = -ADAM_LR * (m_hat / (jnp.sqrt(v_hat) + ADAM_EPS) + ADAM_WD * w)
    return delta, m, v


def _tiling(rows, cols, limit):
    row_tile = max([d for d in range(16, rows + 1, 16) if rows % d == 0 and d * cols <= limit], default=0)
    col_tile = max([ct for ct in range(LANES, cols + 1, LANES) if cols % ct == 0 and rows * ct <= limit], default=0)
    if row_tile and row_tile * cols >= rows * col_tile:
        return row_tile, cols
    return (rows, col_tile) if col_tile else (rows, cols)


def ew_call(name, fn, ins, out_dtypes, limit=1 << 20):
    rows, cols = ins[0].shape
    br, bc = _tiling(rows, cols, limit)
    spec = pl.BlockSpec((br, bc), lambda i, j: (i, j))
    n_in = len(ins)

    def kern(*refs):
        for r, v in zip(refs[n_in:], fn(*[r[...] for r in refs[:n_in]])):
            r[...] = v.astype(r.dtype)

    return pl.pallas_call(
        kern, name=name, grid=(rows // br, cols // bc), in_specs=[spec] * n_in, out_specs=[spec] * len(out_dtypes),
        out_shape=[jax.ShapeDtypeStruct((rows, cols), dt) for dt in out_dtypes],
        compiler_params=_params(("parallel", "parallel")),
    )(*ins)


def adamw(name, w, m, v, g):
    return ew_call(name, lambda wv, mv, vv, gv: (gv, *_adam_math(wv, gv, mv, vv)), [w, m, v, g], [F32] * 4, 1 << 18)


def half_call(name, s, h, extra, out_dtype):
    n_slots, rows, cols = s.shape
    by_cols = _halves_by_cols(rows)
    hr, hc = (rows, cols // 2) if by_cols else (rows // 2, cols)
    br, bc = _tiling(hr, hc, 1 << 20)
    ni, nj = hr // br, hc // bc
    if by_cols:
        s_spec = pl.BlockSpec((None, br, bc), lambda sl, i, j, href: (sl, i, href[0] * nj + j))
    else:
        s_spec = pl.BlockSpec((None, br, bc), lambda sl, i, j, href: (sl, href[0] * ni + i, j))
    flat = pl.BlockSpec((None, br, bc), lambda sl, i, j, href: (sl, i, j))
    has_extra = extra is not None

    def kern(href, s_ref, *rest):
        v = s_ref[...]
        if has_extra:
            v = v + rest[0][...].astype(F32)
        rest[-1][...] = v.astype(out_dtype)

    grid_spec = pltpu.PrefetchScalarGridSpec(
        num_scalar_prefetch=1, grid=(n_slots, ni, nj), in_specs=[s_spec] + ([flat] if has_extra else []), out_specs=flat)
    return pl.pallas_call(
        kern, name=name, grid_spec=grid_spec, out_shape=jax.ShapeDtypeStruct((n_slots, hr, hc), out_dtype),
        compiler_params=_params(("parallel", "parallel", "parallel")),
    )(jnp.reshape(h, (1,)).astype(jnp.int32), s, *([extra] if has_extra else []))


def sum_slots(name, r):
    _, rows, cols = r.shape
    br, bc = _tiling(rows, cols, 1 << 20)

    def kern(r0, r1, r2, r3, o):
        o[...] = ((r0[...].astype(F32) + r1[...].astype(F32)) + r2[...].astype(F32)) + r3[...].astype(F32)

    in_specs = [pl.BlockSpec((None, br, bc), functools.partial(lambda i, j, s: (s, i, j), s=s)) for s in range(4)]
    return pl.pallas_call(
        kern, name=name, grid=(rows // br, cols // bc), in_specs=in_specs,
        out_specs=pl.BlockSpec((br, bc), lambda i, j: (i, j)),
        out_shape=jax.ShapeDtypeStruct((rows, cols), F32), compiler_params=_params(("parallel", "parallel")),
    )(r, r, r, r)


def _my_place():
    return lax.axis_index("x"), lax.axis_index("y"), lax.axis_index("c")


def _chip_peers(x, y):
    peers = [(1 - x, y), (x, 1 - y), (1 - x, 1 - y)]
    return peers, [2 * px + py for px, py in peers]


def gather_shards(name, arrays):
    nw = len(arrays)
    ANY = pl.BlockSpec(memory_space=pl.ANY)

    def body(*refs):
        ins, outs = refs[:nw], refs[nw:2 * nw]
        send, recv, loc = refs[2 * nw:]
        x, y, c = _my_place()
        q = 2 * x + y
        peers, chips = _chip_peers(x, y)

        def remote(w, j, slot):
            return pltpu.make_async_remote_copy(
                src_ref=ins[w], dst_ref=outs[w].at[slot], send_sem=send.at[w, j], recv_sem=recv.at[w, j],
                device_id=(*peers[j], c), device_id_type=MESH_ID)

        local = [pltpu.make_async_copy(ins[w], outs[w].at[q], loc.at[w]) for w in range(nw)]
        sends = [[remote(w, j, q) for j in range(3)] for w in range(nw)]
        for w in range(nw):
            local[w].start()
            for j in range(3):
                sends[w][j].start()
        for w in range(nw):
            local[w].wait()
            for j in range(3):
                sends[w][j].wait_send()
                remote(w, j, chips[j]).wait_recv()

    return pl.pallas_call(
        body, name=name, in_specs=[ANY] * nw, out_specs=[ANY] * nw,
        out_shape=[jax.ShapeDtypeStruct((4,) + a.shape, a.dtype) for a in arrays],
        scratch_shapes=[pltpu.SemaphoreType.DMA((nw, 3)), pltpu.SemaphoreType.DMA((nw, 3)), pltpu.SemaphoreType.DMA((nw,))],
        compiler_params=pltpu.CompilerParams(has_side_effects=True),
    )(*arrays)


def scatter_slots(name, arrays, collective_id):
    nw = len(arrays)

    def body(*refs):
        ins, outs = refs[:nw], refs[nw:2 * nw]
        send, recv, loc = refs[2 * nw:]
        x, y, c = _my_place()
        q = 2 * x + y
        peers, chips = _chip_peers(x, y)
        barrier = pltpu.get_barrier_semaphore()
        for p in peers:
            pl.semaphore_signal(barrier, inc=1, device_id=(*p, c), device_id_type=MESH_ID)
        pl.semaphore_wait(barrier, 3)

        def remote(w, j, src_slot, dst_slot):
            return pltpu.make_async_remote_copy(
                src_ref=ins[w].at[src_slot], dst_ref=outs[w].at[dst_slot], send_sem=send.at[w, j], recv_sem=recv.at[w, j],
                device_id=(*peers[j], c), device_id_type=MESH_ID)

        sends = [[remote(w, j, chips[j], q) for j in range(3)] for w in range(nw)]
        own = [pltpu.make_async_copy(ins[w].at[q], outs[w].at[q], loc.at[w]) for w in range(nw)]
        for w in range(nw):
            for j in range(3):
                sends[w][j].start()
            own[w].start()
        for w in range(nw):
            for j in range(3):
                sends[w][j].wait_send()
                remote(w, j, q, chips[j]).wait_recv()
            own[w].wait()

    return pl.kernel(
        body, out_type=[jax.ShapeDtypeStruct(a.shape, a.dtype) for a in arrays],
        mesh=plsc.ScalarSubcoreMesh(axis_name="sequencer", num_cores=1), name=name,
        scratch_types=[pltpu.SemaphoreType.DMA((nw, 3)), pltpu.SemaphoreType.DMA((nw, 3)), pltpu.SemaphoreType.DMA((nw,))],
        compiler_params=pltpu.CompilerParams(collective_id=collective_id),
    )(*arrays)


def _halves_by_cols(rows):
    return rows % 32 != 0


def _half_of(ref, shape, h):
    rows, cols = shape
    if _halves_by_cols(rows):
        return ref.at[:, pl.ds(h * (cols // 2), cols // 2)]
    return ref.at[pl.ds(h * (rows // 2), rows // 2)]


def _join_halves(lo, hi, rows):
    return jnp.concatenate([lo, hi], axis=lo.ndim - 1 if _halves_by_cols(rows) else lo.ndim - 2)


def gather_two_level(name, arrays, collective_id):
    nw = len(arrays)
    shapes = [a.shape for a in arrays]

    def body(*refs):
        ins, outs = refs[:nw], refs[nw:2 * nw]
        send, recv, loc = refs[2 * nw:]
        x, y, c = _my_place()
        q = 2 * x + y
        me, sibling = (x, y, c), (x, y, 1 - c)
        peers = [(1 - x, y), (x, 1 - y), (1 - x, 1 - y)]
        chips = [2 * px + py for px, py in peers]
        barrier = pltpu.get_barrier_semaphore()
        for dev in [sibling] + [(*p, c) for p in peers]:
            pl.semaphore_signal(barrier, inc=1, device_id=dev, device_id_type=MESH_ID)
        pl.semaphore_wait(barrier, 4)

        def mine(w):
            return _half_of(ins[w], shapes[w], c)

        def landed(w, chip, half):
            return _half_of(outs[w].at[chip], shapes[w], half)

        def copy(w, k, src, chip, half, to):
            return pltpu.make_async_remote_copy(
                src_ref=src, dst_ref=landed(w, chip, half), send_sem=send.at[w, k], recv_sem=recv.at[w, k],
                device_id=to, device_id_type=MESH_ID)

        first = [[copy(w, 0, mine(w), q, c, sibling)] + [copy(w, 1 + j, mine(w), q, c, (*peers[j], c)) for j in range(3)]
                 for w in range(nw)]
        own = [pltpu.make_async_copy(mine(w), landed(w, q, c), loc.at[w]) for w in range(nw)]
        for w in range(nw):
            for cp in first[w]:
                cp.start()
            own[w].start()
        passed = []
        for w in range(nw):
            for j in range(3):
                copy(w, 1 + j, mine(w), chips[j], c, me).wait_recv()
                fwd = copy(w, 4 + j, landed(w, chips[j], c), chips[j], c, sibling)
                fwd.start()
                passed.append(fwd)
        for w in range(nw):
            copy(w, 0, mine(w), q, 1 - c, me).wait_recv()
            for j in range(3):
                copy(w, 4 + j, mine(w), chips[j], 1 - c, me).wait_recv()
        for w in range(nw):
            for cp in first[w]:
                cp.wait_send()
            own[w].wait()
        for cp in passed:
            cp.wait_send()

    out_type = [jax.ShapeDtypeStruct((4,) + a.shape, a.dtype) for a in arrays]
    return pl.kernel(
        body, out_type=out_type, mesh=plsc.ScalarSubcoreMesh(axis_name="sequencer", num_cores=1), name=name,
        scratch_types=[pltpu.SemaphoreType.DMA((nw, 7)), pltpu.SemaphoreType.DMA((nw, 7)), pltpu.SemaphoreType.DMA((nw,))],
        compiler_params=pltpu.CompilerParams(collective_id=collective_id),
    )(*arrays)


def core_swap(name, arrays):
    nw = len(arrays)
    ANY = pl.BlockSpec(memory_space=pl.ANY)

    def body(*refs):
        ins, outs = refs[:nw], refs[nw:2 * nw]
        send, recv = refs[2 * nw:]
        x, y, c = _my_place()
        copies = [pltpu.make_async_remote_copy(
            src_ref=ins[w], dst_ref=outs[w], send_sem=send.at[w], recv_sem=recv.at[w],
            device_id=(x, y, 1 - c), device_id_type=MESH_ID) for w in range(nw)]
        for cp in copies:
            cp.start()
        for cp in copies:
            cp.wait_send()
            cp.wait_recv()

    return pl.pallas_call(
        body, name=name, in_specs=[ANY] * nw, out_specs=[ANY] * nw,
        out_shape=[jax.ShapeDtypeStruct(a.shape, a.dtype) for a in arrays],
        scratch_shapes=[pltpu.SemaphoreType.DMA((nw,)), pltpu.SemaphoreType.DMA((nw,))],
        compiler_params=pltpu.CompilerParams(has_side_effects=True),
    )(*arrays)


def all_reduce_small(name, v):
    rows = v.shape[0]
    VM = pl.BlockSpec(memory_space=pltpu.VMEM)

    def body(v_ref, o_ref, buf, send, recv):
        x, y, c = _my_place()
        me = 4 * x + 2 * y + c

        def peer(kx):
            return (x ^ ((kx >> 2) & 1), y ^ ((kx >> 1) & 1), c ^ (kx & 1))

        def copy(kx, slot):
            return pltpu.make_async_remote_copy(
                src_ref=v_ref, dst_ref=buf.at[slot], send_sem=send.at[kx - 1], recv_sem=recv.at[kx - 1],
                device_id=peer(kx), device_id_type=MESH_ID)

        sends = [copy(kx, me) for kx in range(1, 8)]
        for cp in sends:
            cp.start()
        buf[me] = v_ref[...]
        for kx in range(1, 8):
            copy(kx, me ^ kx).wait_recv()
        for cp in sends:
            cp.wait_send()
        acc = buf[0]
        for d in range(1, 8):
            acc = acc + buf[d]
        o_ref[...] = acc

    return pl.pallas_call(
        body, name=name, in_specs=[VM], out_specs=VM, out_shape=jax.ShapeDtypeStruct(v.shape, F32),
        scratch_shapes=[pltpu.VMEM((8, rows, LANES), F32), pltpu.SemaphoreType.DMA((7,)), pltpu.SemaphoreType.DMA((7,))],
        compiler_params=pltpu.CompilerParams(has_side_effects=True, vmem_limit_bytes=VMEM_LIMIT),
    )(v)


def _pad_cols(a, n):
    return jnp.pad(a, ((0, 0), (0, n - a.shape[1])))


def _pad_rows(a, n):
    return jnp.pad(a, ((0, n - a.shape[0]), (0, 0)))


def _halo(u, tr):
    t, cdim = u.shape
    tails = u.reshape(t // tr, tr, cdim)[:, tr - HALO:, :]
    tails = jnp.concatenate([jnp.zeros((1, HALO, cdim), u.dtype), tails[:-1]], axis=0)
    return tails.reshape(-1, cdim)


def _unhalo(du, dhalo, tr):
    t, cdim = du.shape
    n = t // tr
    dh = dhalo.reshape(n, HALO, cdim)
    dh = jnp.concatenate([dh[1:], jnp.zeros((1, HALO, cdim), du.dtype)], axis=0)
    d3 = du.reshape(n, tr, cdim)
    d3 = jnp.concatenate([d3[:, :tr - HALO, :], d3[:, tr - HALO:, :] + dh], axis=1)
    return d3.reshape(t, cdim)


def _to_slots(g, axis):
    r, cdim = g.shape
    if axis == 0:
        return g.reshape(4, r // 4, cdim)
    return g.reshape(r, 4, cdim // 4).transpose(1, 0, 2)


def _from_slots(s, axis):
    if axis == 0:
        return s.reshape(s.shape[0] * s.shape[1], s.shape[2])
    return s.transpose(1, 0, 2).reshape(s.shape[1], 4 * s.shape[2])


BIG = ("w_in", "w_out", "xattn_wq", "xattn_wk", "xattn_wv", "xattn_wo", "ffn_w1", "ffn_w2")
TRANSPOSED = ("w_in",)
BIG_AXIS = {"w_in": 0, "w_out": 0, "xattn_wq": 0, "xattn_wk": 0, "xattn_wv": 0, "xattn_wo": 0, "ffn_w1": 1, "ffn_w2": 0}
SMALL_SHARDED = ("ssd_conv_w", "rwkv_w2", "rwkv_a2", "rwkv_g2")
GATHER_GROUPS = (("w_in",), ("w_out", "xattn_wq", "xattn_wk", "xattn_wv", "xattn_wo"), ("ffn_w1", "ffn_w2"))
REDUCE_GROUPS = (("ffn_w2", "ffn_w1"), ("xattn_wo", "xattn_wq", "xattn_wk", "xattn_wv", "w_out"),
                 ("rwkv_w2", "rwkv_a2", "rwkv_g2", "w_in"))
REDUCED = BIG + ("rwkv_w2", "rwkv_a2", "rwkv_g2")
REDUCE_AXIS = dict(BIG_AXIS, rwkv_w2=1, rwkv_a2=1, rwkv_g2=1)
WEIGHTS = ("norm_mix_g", "w_in", "ssd_conv_w", "ssd_conv_b", "ssd_dt_bias", "ssd_a_log", "ssd_d", "ssd_norm_g",
           "rwkv_mu", "rwkv_w0", "rwkv_w2", "rwkv_a0", "rwkv_a2", "rwkv_g2", "rwkv_k_k", "rwkv_k_a", "rwkv_r_k",
           "rwkv_ln_w", "rwkv_ln_b", "w_out", "norm_x_g", "norm_mem_g", "xattn_wq", "xattn_wk", "xattn_wv", "xattn_wo",
           "norm_ffn_g", "ffn_w1", "ffn_w2", "final_norm_g")


def _local_grads(x, mem, tgt, wt, full, big, reducer):
    t, d = x.shape
    w = d // 2
    nh = w // HEAD_DIM
    n_pairs = nh // 2
    ppg = n_pairs // SSD_GROUPS
    bc = SSD_GROUPS * SSD_STATE
    conv_dim = w + 2 * bc
    tr = ROW_TILE
    nt = t // tr
    tr2 = 2 * tr if t % (2 * tr) == 0 else tr
    nt2 = t // tr2
    dr = wt["rwkv_w2"].shape[0]
    ar = wt["rwkv_a2"].shape[0]
    gr = wt["rwkv_g2"].shape[0]

    big.start(0, None)
    big.start(1, None)
    h1 = norm_fwd("norm_mix", x, wt["norm_mix_g"], tr2)
    w_in_t = big.get("w_in", (h1, full))
    o = 0
    segs = {}
    for nm, width in (("z", w), ("xbc", conv_dim), ("dt", nh), ("rkv", 3 * w), ("pw", dr), ("pa", ar), ("pg", gr)):
        segs[nm] = (o, width)
        o += width
    padded = {"z": w, "xbc": conv_dim, "dt": LANES, "rkv": 3 * w, "pw": LANES, "pa": LANES, "pg": gr}
    order = ("z", "xbc", "dt", "rkv", "pw", "pa", "pg")
    w_segs = [jnp.concatenate([_pad_rows(w_in_t[segs[nm][0]:segs[nm][0] + segs[nm][1]], padded[nm]) for nm in grp], axis=0)
              for grp in (("z",), ("xbc",), ("dt",), ("rkv",), ("pw", "pa", "pg"))]
    w_perm_t = jnp.concatenate(w_segs, axis=0)
    offs = {}
    o = 0
    for nm in order:
        offs[nm] = o
        o += padded[nm]
    lora_w = 2 * LANES + gr

    mu = wt["rwkv_mu"]
    mo = 3 * w
    mu_rkv = mu[:, :mo]
    mu_lora = jnp.concatenate([_pad_cols(mu[:, mo:mo + dr], LANES), _pad_cols(mu[:, mo + dr:mo + dr + ar], LANES),
                               mu[:, mo + dr + ar:]], axis=1)
    w2p = _pad_rows(full["rwkv_w2"], LANES)
    a2p = _pad_rows(full["rwkv_a2"], LANES)
    g2 = full["rwkv_g2"]
    conv_w = full["ssd_conv_w"]
    cw = [conv_w[i:i + 1] for i in range(SSD_CONV)]
    dt_bias = _pad_cols(wt["ssd_dt_bias"], LANES)
    a_log = _pad_cols(wt["ssd_a_log"], LANES)
    d_skip = _pad_cols(wt["ssd_d"], LANES)
    r_k = wt["rwkv_r_k"].reshape(1, w)

    z, xbc, dtraw, urkv, ulora = [matmul("in_proj_%d" % i, h1, ws, tb=True) for i, ws in enumerate(w_segs)]
    big.start(2, urkv)

    halo_xbc = _halo(xbc, tr)
    ssd_pre_t = [(xbc, tr, conv_dim, 0), (halo_xbc, HALO, conv_dim, 0), (dtraw, tr, LANES, 0)]
    ssd_pre_f = cw + [wt["ssd_conv_b"], dt_bias]
    act, dt = fn_fwd("ssd_pre", _ssd_pre, nt, ssd_pre_t, ssd_pre_f, [(t, tr, conv_dim, F32), (t, tr, LANES, F32)])

    nb = w // LANES
    ssd_seq = [(act, None), (act, lambda p: nb + p // ppg), (act, lambda p: nb + SSD_GROUPS + p // ppg), (dt, lambda p: 0)]
    ssd_ppb = min(ppg, PAIRS_PER_STEP)
    rw_ppb = min(n_pairs, 2 * PAIRS_PER_STEP)

    def ssd_fn(sv, cv, hts, ids):
        return _ssd_chunks([(*s, ht) for s, ht in zip(sv, hts)], cv[0], ids)

    y_scan, ssd_states = scan_fwd("ssd_scan", ssd_fn, SSD_CHUNK, ssd_seq, [a_log], n_pairs, ssd_ppb)
    ssd_post_t = [(y_scan, tr2, w, 0), (act, tr2, w, 0), (z, tr2, w, 0)]
    ssd_post_f = [d_skip, wt["ssd_norm_g"]]
    (y_ssd,) = fn_fwd("ssd_post", _ssd_post, nt2, ssd_post_t, ssd_post_f, [(t, tr2, w, BF16)])

    halo_rkv, halo_lora = _halo(urkv, tr), _halo(ulora, tr)
    rw_pre_t = [(urkv, tr, 3 * w, 0), (ulora, tr, lora_w, 0), (halo_rkv, HALO, 3 * w, 0), (halo_lora, HALO, lora_w, 0)]
    rw_pre_f = [mu_rkv, mu_lora, wt["rwkv_w0"], wt["rwkv_a0"], wt["rwkv_k_k"], wt["rwkv_k_a"], w2p, a2p, g2]
    rw = fn_fwd("rwkv_pre", _rwkv_pre, nt, rw_pre_t, rw_pre_f, [(t, tr, w, F32)] * 7)
    r_, lw_, k2_, v_, nkk_, b_, gate_ = rw
    rw_seq = [(a, None) for a in (r_, lw_, k2_, v_, nkk_, b_)]

    def rw_fn(sv, cv, hts, ids):
        return _rwkv_chunks([(*s, ht) for s, ht in zip(sv, hts)])

    yr_scan, rw_states = scan_fwd("rwkv_scan", rw_fn, RWKV_CHUNK, rw_seq, [], n_pairs, rw_ppb)
    rw_post_t = [(a, tr2, w, 0) for a in (yr_scan, r_, k2_, v_, gate_)]
    rw_post_f = [r_k, wt["rwkv_ln_w"], wt["rwkv_ln_b"]]
    (y_rwkv,) = fn_fwd("rwkv_post", _rwkv_post, nt2, rw_post_t, rw_post_f, [(t, tr2, w, BF16)])

    ymix = jnp.concatenate([y_ssd, y_rwkv], axis=1)
    w_out = big.get("w_out", ymix)
    x1 = matmul("out_proj", ymix, w_out, resid=x)

    h2 = norm_fwd("norm_x", x1, wt["norm_x_g"], tr2)
    mrows = mem.shape[0]
    mn = norm_fwd("norm_mem", mem, wt["norm_mem_g"], mrows)
    wq, wk, wv, wo = [big.get(nm, ymix) for nm in ("xattn_wq", "xattn_wk", "xattn_wv", "xattn_wo")]
    q = matmul("xattn_q", h2, wq)
    kx = matmul("xattn_k", mn, wk)
    vx = matmul("xattn_v", mn, wv)
    (ao,) = fn_fwd("xattn_core", _attn, nt2, [(q, tr2, d, 0)], [kx, vx], [(t, tr2, d, BF16)])
    x2 = matmul("xattn_o", ao, wo, resid=x1)

    h3 = norm_fwd("norm_ffn", x2, wt["norm_ffn_g"], tr2)
    w1, w2 = big.get("ffn_w1", h3), big.get("ffn_w2", h3)
    a1 = matmul("ffn_up", h3, w1, out_dtype=BF16)
    dff = a1.shape[1]
    (f1,) = fn_fwd("ffn_act", _relu2, nt, [(a1, tr, dff, 0)], [], [(t, tr, dff, BF16)])
    x3 = matmul("ffn_down", f1, w2, resid=x2)

    dx3, dx3b, g_final, loss_tile = loss_head(x3, tgt, wt["final_norm_g"].reshape(1, d), tr2)

    grads = {"final_norm_g": g_final.reshape(d)}
    grads["ffn_w2"] = matmul("ffn_down_dw", f1, dx3b, ta=True)
    df1 = matmul("ffn_down_dx", dx3b, w2, tb=True, out_dtype=BF16)
    (da1,), _ = fn_bwd("ffn_act_bwd", _relu2, nt, [(a1, tr, dff, 0)], [], [(df1, tr, dff, 0)], lambda c: [c[0].astype(F32)],
                       [(t, tr, dff, BF16)])
    grads["ffn_w1"] = matmul("ffn_up_dw", h3, da1, out_slots=4, ta=True)
    dh3 = reducer.launch(0, grads, matmul("ffn_up_dx", da1, w1, tb=True))
    dx2, dx2b, grads["norm_ffn_g"] = norm_bwd("norm_ffn_bwd", x2, wt["norm_ffn_g"], dh3, dx3, tr2)

    grads["xattn_wo"] = matmul("xattn_o_dw", ao, dx2b, ta=True)
    dao = matmul("xattn_o_dx", dx2b, wo, tb=True)
    (dq,), (dkx, dvx) = fn_bwd("xattn_core_bwd", _attn, nt2, [(q, tr2, d, 0)], [kx, vx], [(dao, tr2, d, 0)], lambda c: c,
                               [(t, tr2, d, BF16)])
    grads["xattn_wq"] = matmul("xattn_q_dw", h2, dq, ta=True)
    dh2 = matmul("xattn_q_dx", dq, wq, tb=True)
    dkb, dvb = dkx.astype(BF16), dvx.astype(BF16)
    grads["xattn_wk"] = matmul("xattn_k_dw", mn, dkb, ta=True)
    grads["xattn_wv"] = matmul("xattn_v_dw", mn, dvb, ta=True)
    dmn = matmul("xattn_k_dx", dkb, wk, tb=True)
    dmn = matmul("xattn_v_dx", dvb, wv, tb=True, resid=dmn)
    _, _, grads["norm_mem_g"] = norm_bwd("norm_mem_bwd", mem, wt["norm_mem_g"], dmn, None, mrows)
    dx1, dx1b, grads["norm_x_g"] = norm_bwd("norm_x_bwd", x1, wt["norm_x_g"], dh2, dx2, tr2)

    grads["w_out"] = matmul("out_proj_dw", ymix, dx1b, ta=True)
    dymix = reducer.launch(1, grads, matmul("out_proj_dx", dx1b, w_out, tb=True))

    (dyr, dr1, dk1, dv1, dgate), (g_rk, grads["rwkv_ln_w"], grads["rwkv_ln_b"]) = fn_bwd(
        "rwkv_post_bwd", _rwkv_post, nt2, rw_post_t, rw_post_f, [(dymix, tr2, w, 1)], lambda c: c, [(t, tr2, w, F32)] * 5)
    grads["rwkv_r_k"] = g_rk.reshape(wt["rwkv_r_k"].shape)
    (dr2, dlw, dk2, dv2, dnkk, db), _ = scan_bwd("rwkv_scan_bwd", rw_fn, RWKV_CHUNK, rw_seq, [], rw_states, dyr, n_pairs, rw_ppb)
    rw_ct = [(a, tr, w, 0) for a in (dr1, dr2, dlw, dk1, dk2, dv1, dv2, dnkk, db, dgate)]

    def rw_ct_fn(c):
        return (c[0] + c[1], c[2], c[3] + c[4], c[5] + c[6], c[7], c[8], c[9])

    (durkv, dulora, dhrkv, dhlora), rw_pg = fn_bwd(
        "rwkv_pre_bwd", _rwkv_pre, nt, rw_pre_t, rw_pre_f, rw_ct, rw_ct_fn,
        [(t, tr, 3 * w, F32), (t, tr, lora_w, F32), (nt * HALO, HALO, 3 * w, F32), (nt * HALO, HALO, lora_w, F32)])
    durkv = _unhalo(durkv, dhrkv, tr)
    dulora = _unhalo(dulora, dhlora, tr)
    g_mu_rkv, g_mu_lora, grads["rwkv_w0"], grads["rwkv_a0"], grads["rwkv_k_k"], grads["rwkv_k_a"], g_w2p, g_a2p, grads["rwkv_g2"] = rw_pg
    grads["rwkv_mu"] = jnp.concatenate([g_mu_rkv, g_mu_lora[:, :dr], g_mu_lora[:, LANES:LANES + ar], g_mu_lora[:, 2 * LANES:]], axis=1)
    grads["rwkv_w2"] = g_w2p[:dr]
    grads["rwkv_a2"] = g_a2p[:ar]

    (dys, dxs1, dz), (g_d, grads["ssd_norm_g"]) = fn_bwd(
        "ssd_post_bwd", _ssd_post, nt2, ssd_post_t, ssd_post_f, [(dymix, tr2, w, 0)], lambda c: c, [(t, tr2, w, F32)] * 3)
    grads["ssd_d"] = g_d[:, :nh]
    (dxs2, dbp, dcp, ddtp), (g_alog,) = scan_bwd("ssd_scan_bwd", ssd_fn, SSD_CHUNK, ssd_seq, [a_log], ssd_states, dys, n_pairs, ssd_ppb)
    grads["ssd_a_log"] = g_alog[:, :nh]
    ssd_ct = [(dxs1, tr, w, 0), (dxs2, tr, w, 0), (dbp, tr, w, 0), (dcp, tr, w, 0), (ddtp, tr, w, 0)]

    def ssd_ct_fn(c):
        def group_sum(a):
            parts = []
            for gi in range(SSD_GROUPS):
                s = a[:, gi * ppg * LANES:(gi * ppg + 1) * LANES]
                for j in range(1, ppg):
                    s = s + a[:, (gi * ppg + j) * LANES:(gi * ppg + j + 1) * LANES]
                parts.append(s)
            return parts
        ddt = c[4][:, :LANES]
        for j in range(1, n_pairs):
            ddt = ddt + c[4][:, j * LANES:(j + 1) * LANES]
        return (jnp.concatenate([c[0] + c[1]] + group_sum(c[2]) + group_sum(c[3]), axis=1), ddt)

    (dxbc, dhxbc, ddtraw), ssd_pg = fn_bwd(
        "ssd_pre_bwd", _ssd_pre, nt, ssd_pre_t, ssd_pre_f, ssd_ct, ssd_ct_fn,
        [(t, tr, conv_dim, F32), (nt * HALO, HALO, conv_dim, F32), (t, tr, LANES, F32)])
    dxbc = _unhalo(dxbc, dhxbc, tr)
    grads["ssd_conv_w"] = jnp.concatenate(ssd_pg[:SSD_CONV], axis=0)
    grads["ssd_conv_b"] = ssd_pg[SSD_CONV]
    grads["ssd_dt_bias"] = ssd_pg[SSD_CONV + 1][:, :nh]

    du = jnp.concatenate([dz, dxbc, ddtraw, durkv, dulora], axis=1).astype(BF16)
    g_perm_t = matmul("in_proj_dw", du, h1, ta=True)
    grads["w_in"] = jnp.concatenate([g_perm_t[offs[nm]:offs[nm] + segs[nm][1]] for nm in order], axis=0)
    dh1 = matmul("in_proj_dx", du, w_perm_t)
    dh1 = reducer.launch(2, grads, dh1)
    grad_x, _, grads["norm_mix_g"] = norm_bwd("norm_mix_bwd", x, wt["norm_mix_g"], dh1, dx1, tr2)
    return loss_tile, grad_x, grads


def _pack(arrs):
    flat = jnp.concatenate([a.reshape(-1) for a in arrs])
    n = flat.shape[0]
    rows = -(-n // (8 * LANES)) * 8
    return jnp.pad(flat, (0, rows * LANES - n)).reshape(rows, LANES)


def _unpack(packed, shapes):
    flat = packed.reshape(-1)
    out, o = [], 0
    for s in shapes:
        n = math.prod(s)
        out.append(flat[o:o + n].reshape(s))
        o += n
    return out


def _as2d(a):
    return a.reshape(-1, a.shape[-1])


def _shard_view(n, a):
    return _as2d(a[0]).T if n in TRANSPOSED else _as2d(a[0])


class _GatheredWeights:
    def __init__(self, shard2d, q, c):
        self.shard2d, self.q, self.c = shard2d, q, c
        self.raw, self.ready = {}, {}

    def start(self, gi, after):
        shards = [self.shard2d[n].astype(BF16) for n in GATHER_GROUPS[gi]]
        if after is not None:
            shards, _ = lax.optimization_barrier((shards, after))
        gathered = gather_two_level("gather_weights_%d" % gi, shards, gi + 1)
        self.raw.update(zip(GATHER_GROUPS[gi], gathered))

    def get(self, name, after):
        if name not in self.ready:
            g = self.raw[name]
            if after is not None:
                g, _ = lax.optimization_barrier((g, after))
            self.ready[name] = _from_slots(g, 0) if BIG_AXIS[name] == 0 else g
        return self.ready[name]


class _GradReducer:
    def __init__(self, q, c, update):
        self.q, self.c, self.update = q, c, update
        self.pending, self.updated = {}, {}

    def launch(self, gi, grads, nxt):
        names = REDUCE_GROUPS[gi]
        slots = [grads[n] if grads[n].ndim == 3 else _to_slots(grads[n], REDUCE_AXIS[n]) for n in names]
        rows = [s.shape[1] for s in slots]
        sent = [half_call("send_half_" + n, s, 1 - self.c, None, BF16) for n, s in zip(names, slots)]
        got = core_swap("swap_halves_%d" % gi, sent)
        parts = [half_call("chip_sum_" + n, s, self.c, g, BF16) for n, s, g in zip(names, slots, got)]
        parts, nxt = lax.optimization_barrier((parts, nxt))
        slots = scatter_slots("scatter_grads_%d" % gi, parts, len(GATHER_GROUPS) + 1 + gi)
        self.pending[gi] = (slots, rows)
        return self.finish(gi - 1, nxt) if gi > 0 else nxt

    def finish(self, gi, nxt):
        names = REDUCE_GROUPS[gi]
        slots, rows = self.pending[gi]
        halves = []
        for n, s in zip(names, slots):
            halves.append(sum_slots("sum_" + n, s))
        others = core_swap("swap_reduced_%d" % gi, halves)
        lo = [jnp.where(self.c == 0, mine, other) for mine, other in zip(halves, others)]
        hi = [jnp.where(self.c == 0, other, mine) for mine, other in zip(halves, others)]
        results = [self.update(n, _join_halves(l, h, r)) for n, l, h, r in zip(names, lo, hi, rows)]
        if nxt is not None:
            results, nxt = lax.optimization_barrier((results, nxt))
        self.updated.update(zip(names, results))
        return nxt


def _step(a):
    x, mem, tgt = a["x"][0], a["mem"][0], a["loss_target"][0]
    q = 2 * lax.axis_index("x") + lax.axis_index("y")

    shard2d = {n: _shard_view(n, a[n]) for n in BIG}
    small_sh = {n: _as2d(a[n][0]) for n in SMALL_SHARDED}
    c = lax.axis_index("c")
    full = {}
    big = _GatheredWeights(shard2d, q, c)
    gathered = gather_shards("gather_small", [small_sh[n] for n in SMALL_SHARDED])
    for n, g in zip(SMALL_SHARDED, gathered):
        full[n] = _from_slots(g, 1)

    wt = {n: (a[n] if a[n].ndim <= 2 else a[n][0]) for n in WEIGHTS if n not in BIG and n not in SMALL_SHARDED}
    for n in SMALL_SHARDED:
        wt[n] = small_sh[n]
    shards = dict(shard2d)
    shards.update({n: small_sh[n] for n in REDUCED if n not in BIG})

    def update(n, gsum):
        return adamw("adamw_" + n, shards[n], _shard_view(n, a["m_" + n]), _shard_view(n, a["v_" + n]), gsum)

    reducer = _GradReducer(q, c, update)
    loss_tile, grad_x, grads = _local_grads(x, mem, tgt, wt, full, big, reducer)
    reducer.finish(len(REDUCE_GROUPS) - 1, None)
    out = {}
    for n, vals in reducer.updated.items():
        for key, val in zip(("grad_", "delta_", "new_m_", "new_v_"), vals):
            out[key + n] = (val.T if n in TRANSPOSED else val).reshape(a[n].shape)

    small = [n for n in WEIGHTS if n not in REDUCED]
    red = _unpack(all_reduce_small("all_reduce_small", _pack([grads[n] for n in small])), [grads[n].shape for n in small])
    g_loc = {}
    for n, g in zip(small, red):
        if n in SMALL_SHARDED:
            cols = g.shape[1] // 4
            g = lax.dynamic_slice_in_dim(g, q * cols, cols, axis=1)
        g_loc[n] = g.reshape(a[n].shape)
    res = adamw("adamw_small", *[_pack([src[n] for n in small]) for src in
                                 ({n: a[n] for n in small}, {n: a["m_" + n] for n in small}, {n: a["v_" + n] for n in small})],
                _pack([g_loc[n] for n in small]))
    shapes = [a[n].shape for n in small]
    for key, packed in zip(("grad_", "delta_", "new_m_", "new_v_"), res):
        for n, val in zip(small, _unpack(packed, shapes)):
            out[key + n] = val

    loss = lax.psum(loss_tile[0, 0], ("x", "y", "c"))
    ordered = [loss, grad_x.reshape(a["x"].shape)]
    for key in ("grad_", "delta_", "new_m_", "new_v_"):
        ordered += [out[key + n] for n in WEIGHTS]
    return tuple(ordered)


def kernel(x, mem, norm_mix_g, w_in, ssd_conv_w, ssd_conv_b, ssd_dt_bias, ssd_a_log, ssd_d, ssd_norm_g, rwkv_mu, rwkv_w0, rwkv_w2, rwkv_a0, rwkv_a2, rwkv_g2, rwkv_k_k, rwkv_k_a, rwkv_r_k, rwkv_ln_w, rwkv_ln_b, w_out, norm_x_g, norm_mem_g, xattn_wq, xattn_wk, xattn_wv, xattn_wo, norm_ffn_g, ffn_w1, ffn_w2, final_norm_g, loss_target, m_norm_mix_g, m_w_in, m_ssd_conv_w, m_ssd_conv_b, m_ssd_dt_bias, m_ssd_a_log, m_ssd_d, m_ssd_norm_g, m_rwkv_mu, m_rwkv_w0, m_rwkv_w2, m_rwkv_a0, m_rwkv_a2, m_rwkv_g2, m_rwkv_k_k, m_rwkv_k_a, m_rwkv_r_k, m_rwkv_ln_w, m_rwkv_ln_b, m_w_out, m_norm_x_g, m_norm_mem_g, m_xattn_wq, m_xattn_wk, m_xattn_wv, m_xattn_wo, m_norm_ffn_g, m_ffn_w1, m_ffn_w2, m_final_norm_g, v_norm_mix_g, v_w_in, v_ssd_conv_w, v_ssd_conv_b, v_ssd_dt_bias, v_ssd_a_log, v_ssd_d, v_ssd_norm_g, v_rwkv_mu, v_rwkv_w0, v_rwkv_w2, v_rwkv_a0, v_rwkv_a2, v_rwkv_g2, v_rwkv_k_k, v_rwkv_k_a, v_rwkv_r_k, v_rwkv_ln_w, v_rwkv_ln_b, v_w_out, v_norm_x_g, v_norm_mem_g, v_xattn_wq, v_xattn_wk, v_xattn_wv, v_xattn_wo, v_norm_ffn_g, v_ffn_w1, v_ffn_w2, v_final_norm_g):
    return _step(dict(locals()))
```

```python
import functools
import math

import jax
import jax.numpy as jnp
from jax import lax
from jax.experimental import pallas as pl
from jax.experimental.pallas import tpu as pltpu
from jax.experimental.pallas import tpu_sc as plsc

F32 = jnp.float32
BF16 = jnp.bfloat16
HIGHEST = lax.Precision.HIGHEST
MESH_ID = pl.DeviceIdType.MESH

NORM_EPS = 1e-6
RWKV_LN_EPS = 64e-5
HEAD_DIM = 64
LANES = 128
SSD_STATE = 128
SSD_CHUNK = 128
SSD_GROUPS = 2
SSD_CONV = 4
RWKV_CHUNK = 64
HALO = 8
ROW_TILE = 128
PAIRS_PER_STEP = 4
XATTN_HEADS = 4
RWKV_PASSES = 1
VMEM_LIMIT = 56 * 1024 * 1024
MATMUL_VMEM = 48 * 1024 * 1024

ADAM_LR = 0.001
ADAM_B1 = 0.9
ADAM_B2 = 0.999
ADAM_EPS = 1e-08
ADAM_WD = 0.01
ADAM_STEP = 10


def _dims(ca, cb):
    return (((ca,), (cb,)), ((), ()))


def _split_bf16(a):
    hi = a.astype(BF16)
    lo = (a - hi.astype(F32)).astype(BF16)
    return hi, lo


def _mm_impl(a, b, ca, cb, passes):
    dn = _dims(ca, cb)
    if passes == 1:
        return lax.dot_general(a.astype(BF16), b.astype(BF16), dn, preferred_element_type=F32)
    ah, al = _split_bf16(a)
    bh, bl = _split_bf16(b)
    out = lax.dot_general(ah, bh, dn, preferred_element_type=F32)
    out = out + lax.dot_general(ah, bl, dn, preferred_element_type=F32)
    return out + lax.dot_general(al, bh, dn, preferred_element_type=F32)


@functools.partial(jax.custom_vjp, nondiff_argnums=(2, 3, 4))
def mm(a, b, ca, cb, passes):
    return _mm_impl(a, b, ca, cb, passes)


def _mm_fwd(a, b, ca, cb, passes):
    return _mm_impl(a, b, ca, cb, passes), (a, b)


def _mm_bwd(ca, cb, passes, res, g):
    a, b = res
    da = mm(g, b, 1, 1 - cb, passes) if ca == 1 else mm(b, g, 1 - cb, 1, passes)
    db = mm(a, g, 1 - ca, 0, passes) if cb == 0 else mm(g, a, 0, 1 - ca, passes)
    return da, db


mm.defvjp(_mm_fwd, _mm_bwd)


def _dot_exact(a, b):
    return lax.dot_general(a, b, _dims(1, 0), precision=HIGHEST, preferred_element_type=F32)


def _iota(shape, dim):
    return lax.broadcasted_iota(jnp.int32, shape, dim)


def _sigmoid(x):
    return 1.0 / (1.0 + jnp.exp(-x))


def _silu(x):
    return x * _sigmoid(x)


def _softplus(x):
    return jnp.maximum(x, 0.0) + jnp.log(1.0 + jnp.exp(-jnp.abs(x)))


def _rms(x, g):
    return x * lax.rsqrt(jnp.mean(x * x, axis=-1, keepdims=True) + NORM_EPS) * g


def _select_mm(x, sel):
    hi = x.astype(BF16)
    r1 = x - hi.astype(F32)
    mid = r1.astype(BF16)
    lo = (r1 - mid.astype(F32)).astype(BF16)
    dn = _dims(1, 0)
    out = lax.dot_general(hi, sel, dn, preferred_element_type=F32)
    out = out + lax.dot_general(mid, sel, dn, preferred_element_type=F32)
    return out + lax.dot_general(lo, sel, dn, preferred_element_type=F32)


def _head_sum_impl(x, n):
    sel = (_iota((n, LANES), 0) // HEAD_DIM == _iota((n, LANES), 1)).astype(BF16)
    return _select_mm(x, sel)


def _head_expand_impl(s, n):
    sel = (_iota((LANES, n), 1) // HEAD_DIM == _iota((LANES, n), 0)).astype(BF16)
    return _select_mm(s, sel)


@functools.partial(jax.custom_vjp, nondiff_argnums=(1,))
def _head_sum_n(x, n):
    return _head_sum_impl(x, n)


@functools.partial(jax.custom_vjp, nondiff_argnums=(1,))
def _head_expand(s, n):
    return _head_expand_impl(s, n)


_head_sum_n.defvjp(lambda x, n: (_head_sum_impl(x, n), None), lambda n, _, g: (_head_expand(g, n),))
_head_expand.defvjp(lambda s, n: (_head_expand_impl(s, n), None), lambda n, _, g: (_head_sum_n(g, n),))


def _head_sum(x):
    return _head_sum_n(x, x.shape[1])


def _row_vector_expand(v, n):
    v8 = jnp.broadcast_to(v, (8, LANES))
    return jnp.sum(_head_expand(v8, n), axis=0, keepdims=True) * 0.125


def _shift_rows_impl(u, halo, s):
    rolled = pltpu.roll(u, s, 0)
    top = jnp.where(_iota((HALO, 1), 0) < s, pltpu.roll(halo, s, 0), rolled[:HALO])
    return jnp.concatenate([top, rolled[HALO:]], axis=0)


@functools.partial(jax.custom_vjp, nondiff_argnums=(2,))
def _shift_rows(u, halo, s):
    return _shift_rows_impl(u, halo, s)


def _shift_rows_bwd(s, _, g):
    tr = g.shape[0]
    rolled = pltpu.roll(g, tr - s, 0)
    hrow = _iota((HALO, 1), 0)
    bottom = jnp.where(hrow < HALO - s, rolled[tr - HALO:], 0.0)
    dhalo = jnp.where(hrow >= HALO - s, pltpu.roll(g[:HALO], HALO - s, 0), 0.0)
    return jnp.concatenate([rolled[:tr - HALO], bottom], axis=0), dhalo


_shift_rows.defvjp(lambda u, halo, s: (_shift_rows_impl(u, halo, s), None), _shift_rows_bwd)


def _params(sem):
    return pltpu.CompilerParams(dimension_semantics=sem, vmem_limit_bytes=VMEM_LIMIT)


def row_call(name, body, n_tiles, tiled, full, out_tiled, out_acc):
    nt, nf, no, na = len(tiled), len(full), len(out_tiled), len(out_acc)

    def kern(*refs):
        tv = [r[...] for r in refs[:nt]]
        fv = [r[...] for r in refs[nt:nt + nf]]
        outs, accs = body(tv, fv)
        for r, v in zip(refs[nt + nf:nt + nf + no], outs):
            r[...] = v.astype(r.dtype)
        if na:
            a_refs = refs[nt + nf + no:]
            first = pl.program_id(0) == 0

            @pl.when(first)
            def _():
                for r, v in zip(a_refs, accs):
                    r[...] = v

            @pl.when(jnp.logical_not(first))
            def _():
                for r, v in zip(a_refs, accs):
                    r[...] += v

    in_specs = [pl.BlockSpec((rt, w), functools.partial(lambda i, cb: (i, cb), cb=cb)) for (_, rt, w, cb) in tiled]
    in_specs += [pl.BlockSpec(a.shape, lambda i: (0, 0)) for a in full]
    out_specs = [pl.BlockSpec((rt, w), lambda i: (i, 0)) for (_, rt, w, _) in out_tiled]
    out_specs += [pl.BlockSpec(s, lambda i: (0, 0)) for s in out_acc]
    out_shape = [jax.ShapeDtypeStruct((rows, w), dt) for (rows, _, w, dt) in out_tiled]
    out_shape += [jax.ShapeDtypeStruct(s, F32) for s in out_acc]
    res = pl.pallas_call(
        kern, name=name, grid=(n_tiles,), in_specs=in_specs, out_specs=out_specs, out_shape=out_shape,
        compiler_params=_params(("arbitrary",)),
    )(*[t[0] for t in tiled], *full)
    return list(res[:no]), list(res[no:])


def _pick(dim, cands):
    for c in cands:
        if dim % c == 0:
            return c
    return dim


def matmul(name, a, b, tb=False, resid=None, out_dtype=F32, out_slots=1, ta=False):
    (k, m) = a.shape if ta else a.shape[::-1]
    b_slots = b.shape[0] if b.ndim == 3 else 1
    n = b.shape[-2] if tb else b.shape[-1] * b_slots
    has_resid = resid is not None
    out_bytes = jnp.dtype(out_dtype).itemsize
    sizes = (2048, 1024, 896, 768, 512, 384, 256, 128)
    tm = _pick(m, sizes[1:])
    tn = _pick(n // max(out_slots, 1 if tb else b_slots), sizes[1:])

    def vmem_bytes(tk):
        return 2 * 2 * tk * (tm + tn) + tm * tn * (2 * out_bytes + 4 + (8 if has_resid else 0))

    k_slot = k // b_slots if tb else k
    tk = next((c for c in (4096,) + sizes if k_slot % c == 0 and vmem_bytes(c) <= MATMUL_VMEM), LANES)
    nk = k // tk
    n_per = n // (b_slots if not tb else 1) // tn
    k_per = k_slot // tk
    o_per = n // out_slots // tn

    def kern(*refs):
        a_ref, b_ref = refs[0], refs[1]
        o_ref, acc = refs[-2], refs[-1]
        kk = pl.program_id(2)
        part = lax.dot_general(a_ref[...], b_ref[...], _dims(0 if ta else 1, 1 if tb else 0), preferred_element_type=F32)

        def finish(out):
            if has_resid:
                out = out + refs[2][...]
            o_ref[...] = out.astype(o_ref.dtype)

        if nk == 1:
            finish(part)
            return

        @pl.when(kk == 0)
        def _():
            acc[...] = part

        @pl.when(jnp.logical_and(kk > 0, kk < nk - 1))
        def _():
            acc[...] += part

        @pl.when(kk == nk - 1)
        def _():
            finish(acc[...] + part)

    in_specs = [pl.BlockSpec((tk, tm), lambda i, j, kk: (kk, i)) if ta else pl.BlockSpec((tm, tk), lambda i, j, kk: (i, kk))]
    if b.ndim == 3 and tb:
        in_specs.append(pl.BlockSpec((None, tn, tk), lambda i, j, kk: (kk // k_per, j, kk % k_per)))
    elif b.ndim == 3:
        in_specs.append(pl.BlockSpec((None, tk, tn), lambda i, j, kk: (j // n_per, kk, j % n_per)))
    elif tb:
        in_specs.append(pl.BlockSpec((tn, tk), lambda i, j, kk: (j, kk)))
    else:
        in_specs.append(pl.BlockSpec((tk, tn), lambda i, j, kk: (kk, j)))
    args = [a, b]
    if has_resid:
        in_specs.append(pl.BlockSpec((tm, tn), lambda i, j, kk: (i, j)))
        args.append(resid)
    if out_slots > 1:
        out_spec = pl.BlockSpec((None, tm, tn), lambda i, j, kk: (j // o_per, i, j % o_per))
        out_shape = jax.ShapeDtypeStruct((out_slots, m, n // out_slots), out_dtype)
    else:
        out_spec = pl.BlockSpec((tm, tn), lambda i, j, kk: (i, j))
        out_shape = jax.ShapeDtypeStruct((m, n), out_dtype)
    return pl.pallas_call(
        kern, name=name, grid=(m // tm, n // tn, nk), in_specs=in_specs,
        out_specs=out_spec, out_shape=out_shape,
        scratch_shapes=[pltpu.VMEM((tm, tn), F32)],
        compiler_params=_params(("parallel", "parallel", "arbitrary")),
    )(*args)


def norm_fwd(name, x, g, tr):
    def body(tv, fv):
        return [_rms(tv[0], fv[0])], []
    rows, d = x.shape
    (h,), _ = row_call(name, body, rows // tr, [(x, tr, d, 0)], [g], [(rows, tr, d, BF16)], [])
    return h


def norm_bwd(name, x, g, dh, extra, tr):
    def body(tv, fv):
        _, vjp = jax.vjp(_rms, tv[0], fv[0])
        dx, dg = vjp(tv[1])
        if extra is not None:
            dx = dx + tv[2]
        return [dx, dx], [dg]
    rows, d = x.shape
    tiled = [(x, tr, d, 0), (dh, tr, d, 0)] + ([(extra, tr, d, 0)] if extra is not None else [])
    (dx, dxb), (dg,) = row_call(name, body, rows // tr, tiled, [g], [(rows, tr, d, F32), (rows, tr, d, BF16)], [g.shape])
    return dx, dxb, dg


def _ssd_pre(xbc, halo, dtraw, w0, w1, w2, w3, cb, dtb):
    y = w3 * xbc + w2 * _shift_rows(xbc, halo, 1) + w1 * _shift_rows(xbc, halo, 2) + w0 * _shift_rows(xbc, halo, 3) + cb
    return _silu(y), _softplus(dtraw + dtb)


def _ssd_post(ys, xs, z, dskip, ng):
    w = ys.shape[1]
    y = (ys + xs * _row_vector_expand(dskip, w)) * _silu(z)
    gw = w // SSD_GROUPS
    parts = []
    for gi in range(SSD_GROUPS):
        yg = y[:, gi * gw:(gi + 1) * gw]
        parts.append(yg * lax.rsqrt(jnp.mean(yg * yg, axis=-1, keepdims=True) + NORM_EPS))
    return jnp.concatenate(parts, axis=1) * ng


def _rwkv_pre(urkv, ulora, hrkv, hlora, mu_rkv, mu_lora, w0, a0, kkw, kaw, w2p, a2p, g2):
    w = w0.shape[1]
    urkv = urkv + (_shift_rows(urkv, hrkv, 1) - urkv) * mu_rkv
    ulora = ulora + (_shift_rows(ulora, hlora, 1) - ulora) * mu_lora
    r, k, v = urkv[:, :w], urkv[:, w:2 * w], urkv[:, 2 * w:]
    pw, pa, pg = ulora[:, :LANES], ulora[:, LANES:2 * LANES], ulora[:, 2 * LANES:]
    w_log = -_softplus(-(w0 + mm(jnp.tanh(pw), w2p, 1, 0, 1))) - 0.5
    lw = -jnp.exp(w_log)
    iclr = _sigmoid(a0 + mm(pa, a2p, 1, 0, 1))
    gate = mm(_sigmoid(pg), g2, 1, 0, 1)
    kk = k * kkw
    kk = kk / jnp.maximum(jnp.sqrt(_head_expand(_head_sum(kk * kk), w)), 1e-12)
    k2 = k * (1.0 + (iclr - 1.0) * kaw)
    return r, lw, k2, v, -kk, kk * iclr, gate


def _rwkv_post(ys, r, k2, v, gate, rk, lnw, lnb):
    w = ys.shape[1]
    inv = 1.0 / HEAD_DIM
    mean = _head_expand(_head_sum(ys), w) * inv
    d = ys - mean
    var = _head_expand(_head_sum(d * d), w) * inv
    yn = d * lax.rsqrt(var + RWKV_LN_EPS) * lnw + lnb
    bonus = _head_expand(_head_sum(r * k2 * rk), w) * v
    return (yn + bonus) * gate


def _attn(q, k, v):
    d = q.shape[1]
    hd = d // XATTN_HEADS
    outs = []
    for h in range(XATTN_HEADS):
        sl = slice(h * hd, (h + 1) * hd)
        s = mm(q[:, sl], k[:, sl], 1, 1, 1) * (hd ** -0.5)
        s = s - jnp.max(s, axis=-1, keepdims=True)
        p = jnp.exp(s)
        p = p / jnp.sum(p, axis=-1, keepdims=True)
        outs.append(mm(p, v[:, sl], 1, 0, 1))
    return jnp.concatenate(outs, axis=1)


def _relu2(a):
    return jnp.square(jnp.maximum(a.astype(F32), 0.0))


def fn_fwd(name, fn, n_tiles, tiled, full, out_tiled):
    def body(tv, fv):
        outs = fn(*tv, *fv)
        return (list(outs) if isinstance(outs, (tuple, list)) else [outs]), []
    outs, _ = row_call(name, body, n_tiles, tiled, full, out_tiled, [])
    return outs


def fn_bwd(name, fn, n_tiles, tiled, full, cts, ct_fn, out_tiled):
    nt = len(tiled)

    def body(tv, fv):
        outs, vjp = jax.vjp(fn, *tv[:nt], *fv)
        ct = ct_fn(tv[nt:])
        grads = vjp(tuple(ct) if isinstance(outs, (tuple, list)) else ct[0])
        return list(grads[:nt]), list(grads[nt:])
    return row_call(name, body, n_tiles, tiled + cts, full, out_tiled, [f.shape for f in full])


def _ssd_chunks(pairs, a_log, ids):
    q = pairs[0][0].shape[0]
    lane = _iota((1, LANES), 1)
    row = _iota((q, 1), 0)
    tril = _iota((q, q), 0) >= _iota((q, q), 1)
    half = lane < HEAD_DIM
    not_half = jnp.logical_not(half)
    n = len(pairs)
    bm, cm, dt_all = pairs[0][1], pairs[0][2], pairs[0][3]
    da = dt_all * (-jnp.exp(a_log))
    cs = _dot_exact(tril.astype(F32), da)

    def col(mat, h):
        return jnp.sum(jnp.where(lane == h, mat, 0.0), axis=1, keepdims=True)

    cs0 = [col(cs, 2 * p) for p in ids]
    cs1 = [col(cs, 2 * p + 1) for p in ids]
    xdt = [pairs[j][0] * jnp.where(half, col(dt_all, 2 * p), col(dt_all, 2 * p + 1)) for j, p in enumerate(ids)]
    csx = [jnp.where(half, a0, a1) for a0, a1 in zip(cs0, cs1)]
    last = [jnp.sum(jnp.where(row == q - 1, c_, 0.0), axis=0, keepdims=True) for c_ in csx]
    cb = mm(cm, bm, 1, 1, 1)
    y0 = [mm(cm, pairs[j][4], 1, 0, 1) for j in range(n)]
    st = [mm(bm, xdt[j] * jnp.exp(last[j] - csx[j]), 0, 0, 1) for j in range(n)]

    def decay(csh):
        csl = jnp.broadcast_to(csh, (q, q))
        return jnp.where(tril, jnp.exp(jnp.where(tril, csl - csl.T, 0.0)), 0.0)

    lm = [(decay(cs0[j]), decay(cs1[j])) for j in range(n)]
    yd = [(mm(cb * lm[j][0], xdt[j], 1, 0, 1), mm(cb * lm[j][1], xdt[j], 1, 0, 1)) for j in range(n)]
    out = []
    for j in range(n):
        y = y0[j] * jnp.exp(csx[j]) + jnp.where(half, yd[j][0], 0.0) + jnp.where(not_half, yd[j][1], 0.0)
        out.append((y, pairs[j][4] * jnp.exp(last[j]) + st[j]))
    return out


def _unit_lower_inverses_impl(mats):
    c = mats[0].shape[0]
    eye = (_iota((c, c), 0) == _iota((c, c), 1)).astype(F32)
    tm = [eye + a_ for a_ in mats]
    pm = mats
    for _ in range(int(math.log2(c)) - 1):
        pm = [mm(p_, p_, 1, 0, RWKV_PASSES) for p_ in pm]
        tm = [t_ + mm(t_, p_, 1, 0, RWKV_PASSES) for t_, p_ in zip(tm, pm)]
    return tm


@jax.custom_vjp
def _unit_lower_inverses(mats):
    return _unit_lower_inverses_impl(mats)


def _unit_lower_inverses_fwd(mats):
    tm = _unit_lower_inverses_impl(mats)
    return tm, tm


def _unit_lower_inverses_bwd(tm, g):
    left = [mm(t_, g_, 0, 0, RWKV_PASSES) for t_, g_ in zip(tm, g)]
    return ([mm(l_, t_, 1, 1, RWKV_PASSES) for l_, t_ in zip(left, tm)],)


_unit_lower_inverses.defvjp(_unit_lower_inverses_fwd, _unit_lower_inverses_bwd)


def _rwkv_chunks(pairs):
    c = pairs[0][0].shape[0]
    ps = RWKV_PASSES
    lane = _iota((1, LANES), 1)
    row = _iota((c, 1), 0)
    ri, ci = _iota((c, c), 0), _iota((c, c), 1)
    tril_i, tril_s = ri >= ci, ri > ci
    half = lane < HEAD_DIM
    halves = (half, jnp.logical_not(half))
    bd = (_iota((LANES, LANES), 0) < HEAD_DIM) == (_iota((LANES, LANES), 1) < HEAD_DIM)
    tri = tril_i.astype(F32)
    n = len(pairs)
    heads = [(j, hm) for j in range(n) for hm in halves]

    cum = [_dot_exact(tri, p[1]) for p in pairs]
    at = [p[4] * jnp.exp(cm - p[1]) for p, cm in zip(pairs, cum)]
    en = [jnp.exp(-cm) for cm in cum]
    bt = [p[5] * e for p, e in zip(pairs, en)]
    kt = [p[2] * e for p, e in zip(pairs, en)]
    rt = [p[0] * jnp.exp(cm) for p, cm in zip(pairs, cum)]
    ah = [mm(at[j], pairs[j][6], 1, 1, ps) for j in range(n)]
    y = [mm(rt[j], pairs[j][6], 1, 1, ps) for j in range(n)]
    atm = [jnp.where(hm, at[j], 0.0) for j, hm in heads]
    rtm = [jnp.where(hm, rt[j], 0.0) for j, hm in heads]
    aab = [jnp.where(tril_s, mm(atm[i], bt[j], 1, 1, ps), 0.0) for i, (j, _) in enumerate(heads)]
    aak = [jnp.where(tril_s, mm(atm[i], kt[j], 1, 1, ps), 0.0) for i, (j, _) in enumerate(heads)]
    arb = [jnp.where(tril_i, mm(rtm[i], bt[j], 1, 1, ps), 0.0) for i, (j, _) in enumerate(heads)]
    ark = [jnp.where(tril_i, mm(rtm[i], kt[j], 1, 1, ps), 0.0) for i, (j, _) in enumerate(heads)]
    rhs = [ah[j] + mm(aak[i], pairs[j][3], 1, 0, ps) for i, (j, _) in enumerate(heads)]
    yv = [mm(ark[i], pairs[j][3], 1, 0, ps) for i, (j, _) in enumerate(heads)]
    tm = _unit_lower_inverses(aab)
    uh =[mm(tm[i], rhs[i], 1, 0, ps) for i in range(len(heads))]
    u = [jnp.where(half, uh[2 * j], uh[2 * j + 1]) for j in range(n)]
    yu = [mm(arb[i], u[j], 1, 0, ps) for i, (j, _) in enumerate(heads)]
    out = []
    for j in range(n):
        yj = y[j] + jnp.where(half, yu[2 * j] + yv[2 * j], yu[2 * j + 1] + yv[2 * j + 1])
        plast = jnp.sum(jnp.where(row == c - 1, cum[j], 0.0), axis=0, keepdims=True)
        upd = pairs[j][6] + mm(u[j], bt[j], 0, 0, ps) + mm(pairs[j][3], kt[j], 0, 0, ps)
        out.append((yj, jnp.where(bd, upd * jnp.exp(plast), 0.0)))
    return out


def _seq_spec(chunk, ppb, col, row_of):
    if col is None:
        return pl.BlockSpec((chunk, ppb * LANES), lambda pb, i: (row_of(i), pb))
    return pl.BlockSpec((chunk, LANES), lambda pb, i: (row_of(i), col(pb * ppb)))


def _pair_vals(refs, seq_in, j):
    return [r[...] if col is not None else r[:, j * LANES:(j + 1) * LANES] for r, (_, col) in zip(refs, seq_in)]


def scan_fwd(name, chunk_fn, chunk, seq_in, const_in, n_pairs, ppb):
    t = seq_in[0][0].shape[0]
    nc = t // chunk
    ns, ncst = len(seq_in), len(const_in)

    def kern(*refs):
        y_ref, st_ref, ht = refs[ns + ncst], refs[ns + ncst + 1], refs[ns + ncst + 2]

        @pl.when(pl.program_id(1) == 0)
        def _():
            ht[...] = jnp.zeros_like(ht)

        cv = [r[...] for r in refs[ns:ns + ncst]]
        h0 = [ht[j] for j in range(ppb)]
        for j in range(ppb):
            st_ref[j] = h0[j]
        sv = [_pair_vals(refs[:ns], seq_in, j) for j in range(ppb)]
        outs = chunk_fn(sv, cv, h0, [pl.program_id(0) * ppb + j for j in range(ppb)])
        for j, (y, hn) in enumerate(outs):
            y_ref[:, j * LANES:(j + 1) * LANES] = y
            ht[j] = hn

    in_specs = [_seq_spec(chunk, ppb, col, lambda i: i) for (_, col) in seq_in]
    in_specs += [pl.BlockSpec(a.shape, lambda pb, i: (0, 0)) for a in const_in]
    return pl.pallas_call(
        kern, name=name, grid=(n_pairs // ppb, nc), in_specs=in_specs,
        out_specs=[pl.BlockSpec((chunk, ppb * LANES), lambda pb, i: (i, pb)),
                   pl.BlockSpec((ppb, None, LANES, LANES), lambda pb, i: (pb, i, 0, 0))],
        out_shape=[jax.ShapeDtypeStruct((t, n_pairs * LANES), F32), jax.ShapeDtypeStruct((n_pairs, nc, LANES, LANES), F32)],
        scratch_shapes=[pltpu.VMEM((ppb, LANES, LANES), F32)],
        compiler_params=_params(("arbitrary", "arbitrary")),
    )(*[s[0] for s in seq_in], *const_in)


def scan_bwd(name, chunk_fn, chunk, seq_in, const_in, states, dy, n_pairs, ppb):
    t = dy.shape[0]
    nc = t // chunk
    ns, ncst = len(seq_in), len(const_in)

    def kern(*refs):
        seq_refs, cst_refs = refs[:ns], refs[ns:ns + ncst]
        st_ref, dy_ref = refs[ns + ncst], refs[ns + ncst + 1]
        o = ns + ncst + 2
        dseq_refs, dcst_refs, dht = refs[o:o + ns], refs[o + ns:o + ns + ncst], refs[o + ns + ncst]
        pb, i = pl.program_id(0), pl.program_id(1)

        @pl.when(i == 0)
        def _():
            dht[...] = jnp.zeros_like(dht)

        ids = [pb * ppb + j for j in range(ppb)]
        lanes = [slice(j * LANES, (j + 1) * LANES) for j in range(ppb)]

        def fn(*flat):
            sv = [list(flat[j * ns:(j + 1) * ns]) for j in range(ppb)]
            outs = chunk_fn(sv, list(flat[ppb * ns:ppb * ns + ncst]), list(flat[ppb * ns + ncst:]), ids)
            return tuple(y for y, _ in outs), tuple(h for _, h in outs)

        flat_in = [v for j in range(ppb) for v in _pair_vals(seq_refs, seq_in, j)]
        flat_in += [r[...] for r in cst_refs] + [st_ref[j] for j in range(ppb)]
        _, vjp = jax.vjp(fn, *flat_in)
        grads = vjp((tuple(dy_ref[:, ln] for ln in lanes), tuple(dht[j] for j in range(ppb))))
        for j in range(ppb):
            for r, g in zip(dseq_refs, grads[j * ns:(j + 1) * ns]):
                r[:, lanes[j]] = g
            dht[j] = grads[ppb * ns + ncst + j]
        dcv = grads[ppb * ns:ppb * ns + ncst]
        if ncst:
            first = jnp.logical_and(pb == 0, i == 0)

            @pl.when(first)
            def _():
                for r, g in zip(dcst_refs, dcv):
                    r[...] = g

            @pl.when(jnp.logical_not(first))
            def _():
                for r, g in zip(dcst_refs, dcv):
                    r[...] += g

    rev = lambda i: nc - 1 - i
    wide = pl.BlockSpec((chunk, ppb * LANES), lambda pb, i: (rev(i), pb))
    in_specs = [_seq_spec(chunk, ppb, col, rev) for (_, col) in seq_in]
    in_specs += [pl.BlockSpec(a.shape, lambda pb, i: (0, 0)) for a in const_in]
    in_specs += [pl.BlockSpec((ppb, None, LANES, LANES), lambda pb, i: (pb, rev(i), 0, 0)), wide]
    out_specs = [wide for _ in seq_in]
    out_specs += [pl.BlockSpec(a.shape, lambda pb, i: (0, 0)) for a in const_in]
    out_shape = [jax.ShapeDtypeStruct((t, n_pairs * LANES), F32) for _ in seq_in]
    out_shape += [jax.ShapeDtypeStruct(a.shape, F32) for a in const_in]
    res = pl.pallas_call(
        kern, name=name, grid=(n_pairs // ppb, nc), in_specs=in_specs, out_specs=out_specs, out_shape=out_shape,
        scratch_shapes=[pltpu.VMEM((ppb, LANES, LANES), F32)],
        compiler_params=_params(("arbitrary", "arbitrary")),
    )(*[s[0] for s in seq_in], *const_in, states, dy)
    return list(res[:ns]), list(res[ns:])


def loss_head(x3, tgt, g, tr):
    rows, d = x3.shape

    def body(tv, fv):
        def f(x, gg):
            e = jnp.square(_rms(x, gg) - tv[1])
            return 0.5 * jnp.sum(jnp.mean(e, axis=-1, keepdims=True), axis=0, keepdims=True)
        l, vjp = jax.vjp(f, tv[0], fv[0])
        dx, dg = vjp(jnp.ones((1, 1), F32))
        return [dx, dx], [dg, jnp.broadcast_to(l, (8, LANES))]
    (dx, dxb), (dg, l) = row_call("loss_head", body, rows // tr, [(x3, tr, d, 0), (tgt, tr, d, 0)], [g],
                                  [(rows, tr, d, F32), (rows, tr, d, BF16)], [g.shape, (8, LANES)])
    return dx, dxb, dg, l


def _adam_math(w, g, m, v):
    m = ADAM_B1 * m + (1.0 - ADAM_B1) * g
    v = ADAM_B2 * v + (1.0 - ADAM_B2) * jnp.square(g)
    m_hat = m / (1.0 - ADAM_B1 ** ADAM_STEP)
    v_hat = v / (1.0 - ADAM_B2 ** ADAM_STEP)
    delta = -ADAM_LR * (m_hat / (jnp.sqrt(v_hat) + ADAM_EPS) + ADAM_WD * w)
    return delta, m, v


def _tiling(rows, cols, limit):
    row_tile = max([d for d in range(16, rows + 1, 16) if rows % d == 0 and d * cols <= limit], default=0)
    col_tile = max([ct for ct in range(LANES, cols + 1, LANES) if cols % ct == 0 and rows * ct <= limit], default=0)
    if row_tile and row_tile * cols >= rows * col_tile:
        return row_tile, cols
    return (rows, col_tile) if col_tile else (rows, cols)


def ew_call(name, fn, ins, out_dtypes, limit=1 << 20):
    rows, cols = ins[0].shape
    br, bc = _tiling(rows, cols, limit)
    spec = pl.BlockSpec((br, bc), lambda i, j: (i, j))
    n_in = len(ins)

    def kern(*refs):
        for r, v in zip(refs[n_in:], fn(*[r[...] for r in refs[:n_in]])):
            r[...] = v.astype(r.dtype)

    return pl.pallas_call(
        kern, name=name, grid=(rows // br, cols // bc), in_specs=[spec] * n_in, out_specs=[spec] * len(out_dtypes),
        out_shape=[jax.ShapeDtypeStruct((rows, cols), dt) for dt in out_dtypes],
        compiler_params=_params(("parallel", "parallel")),
    )(*ins)


def adamw(name, w, m, v, g):
    return ew_call(name, lambda wv, mv, vv, gv: (gv, *_adam_math(wv, gv, mv, vv)), [w, m, v, g], [F32] * 4, 1 << 18)


def half_call(name, s, h, extra, out_dtype):
    n_slots, rows, cols = s.shape
    by_cols = _halves_by_cols(rows)
    hr, hc = (rows, cols // 2) if by_cols else (rows // 2, cols)
    br, bc = _tiling(hr, hc, 1 << 20)
    ni, nj = hr // br, hc // bc
    if by_cols:
        s_spec = pl.BlockSpec((None, br, bc), lambda sl, i, j, href: (sl, i, href[0] * nj + j))
    else:
        s_spec = pl.BlockSpec((None, br, bc), lambda sl, i, j, href: (sl, href[0] * ni + i, j))
    flat = pl.BlockSpec((None, br, bc), lambda sl, i, j, href: (sl, i, j))
    has_extra = extra is not None

    def kern(href, s_ref, *rest):
        v = s_ref[...]
        if has_extra:
            v = v + rest[0][...].astype(F32)
        rest[-1][...] = v.astype(out_dtype)

    grid_spec = pltpu.PrefetchScalarGridSpec(
        num_scalar_prefetch=1, grid=(n_slots, ni, nj), in_specs=[s_spec] + ([flat] if has_extra else []), out_specs=flat)
    return pl.pallas_call(
        kern, name=name, grid_spec=grid_spec, out_shape=jax.ShapeDtypeStruct((n_slots, hr, hc), out_dtype),
        compiler_params=_params(("parallel", "parallel", "parallel")),
    )(jnp.reshape(h, (1,)).astype(jnp.int32), s, *([extra] if has_extra else []))


def sum_slots(name, r):
    _, rows, cols = r.shape
    br, bc = _tiling(rows, cols, 1 << 20)

    def kern(r0, r1, r2, r3, o):
        o[...] = ((r0[...].astype(F32) + r1[...].astype(F32)) + r2[...].astype(F32)) + r3[...].astype(F32)

    in_specs = [pl.BlockSpec((None, br, bc), functools.partial(lambda i, j, s: (s, i, j), s=s)) for s in range(4)]
    return pl.pallas_call(
        kern, name=name, grid=(rows // br, cols // bc), in_specs=in_specs,
        out_specs=pl.BlockSpec((br, bc), lambda i, j: (i, j)),
        out_shape=jax.ShapeDtypeStruct((rows, cols), F32), compiler_params=_params(("parallel", "parallel")),
    )(r, r, r, r)


def _my_place():
    return lax.axis_index("x"), lax.axis_index("y"), lax.axis_index("c")


def _chip_peers(x, y):
    peers = [(1 - x, y), (x, 1 - y), (1 - x, 1 - y)]
    return peers, [2 * px + py for px, py in peers]


def gather_shards(name, arrays):
    nw = len(arrays)
    ANY = pl.BlockSpec(memory_space=pl.ANY)

    def body(*refs):
        ins, outs = refs[:nw], refs[nw:2 * nw]
        send, recv, loc = refs[2 * nw:]
        x, y, c = _my_place()
        q = 2 * x + y
        peers, chips = _chip_peers(x, y)

        def remote(w, j, slot):
            return pltpu.make_async_remote_copy(
                src_ref=ins[w], dst_ref=outs[w].at[slot], send_sem=send.at[w, j], recv_sem=recv.at[w, j],
                device_id=(*peers[j], c), device_id_type=MESH_ID)

        local = [pltpu.make_async_copy(ins[w], outs[w].at[q], loc.at[w]) for w in range(nw)]
        sends = [[remote(w, j, q) for j in range(3)] for w in range(nw)]
        for w in range(nw):
            local[w].start()
            for j in range(3):
                sends[w][j].start()
        for w in range(nw):
            local[w].wait()
            for j in range(3):
                sends[w][j].wait_send()
                remote(w, j, chips[j]).wait_recv()

    return pl.pallas_call(
        body, name=name, in_specs=[ANY] * nw, out_specs=[ANY] * nw,
        out_shape=[jax.ShapeDtypeStruct((4,) + a.shape, a.dtype) for a in arrays],
        scratch_shapes=[pltpu.SemaphoreType.DMA((nw, 3)), pltpu.SemaphoreType.DMA((nw, 3)), pltpu.SemaphoreType.DMA((nw,))],
        compiler_params=pltpu.CompilerParams(has_side_effects=True),
    )(*arrays)


def scatter_slots(name, arrays, collective_id):
    nw = len(arrays)

    def body(*refs):
        ins, outs = refs[:nw], refs[nw:2 * nw]
        send, recv, loc = refs[2 * nw:]
        x, y, c = _my_place()
        q = 2 * x + y
        peers, chips = _chip_peers(x, y)
        barrier = pltpu.get_barrier_semaphore()
        for p in peers:
            pl.semaphore_signal(barrier, inc=1, device_id=(*p, c), device_id_type=MESH_ID)
        pl.semaphore_wait(barrier, 3)

        def remote(w, j, src_slot, dst_slot):
            return pltpu.make_async_remote_copy(
                src_ref=ins[w].at[src_slot], dst_ref=outs[w].at[dst_slot], send_sem=send.at[w, j], recv_sem=recv.at[w, j],
                device_id=(*peers[j], c), device_id_type=MESH_ID)

        sends = [[remote(w, j, chips[j], q) for j in range(3)] for w in range(nw)]
        own = [pltpu.make_async_copy(ins[w].at[q], outs[w].at[q], loc.at[w]) for w in range(nw)]
        for w in range(nw):
            for j in range(3):
                sends[w][j].start()
            own[w].start()
        for w in range(nw):
            for j in range(3):
                sends[w][j].wait_send()
                remote(w, j, q, chips[j]).wait_recv()
            own[w].wait()

    return pl.kernel(
        body, out_type=[jax.ShapeDtypeStruct(a.shape, a.dtype) for a in arrays],
        mesh=plsc.ScalarSubcoreMesh(axis_name="sequencer", num_cores=1), name=name,
        scratch_types=[pltpu.SemaphoreType.DMA((nw, 3)), pltpu.SemaphoreType.DMA((nw, 3)), pltpu.SemaphoreType.DMA((nw,))],
        compiler_params=pltpu.CompilerParams(collective_id=collective_id),
    )(*arrays)


def _halves_by_cols(rows):
    return rows % 32 != 0


def _half_of(ref, shape, h):
    rows, cols = shape
    if _halves_by_cols(rows):
        return ref.at[:, pl.ds(h * (cols // 2), cols // 2)]
    return ref.at[pl.ds(h * (rows // 2), rows // 2)]


def _join_halves(lo, hi, rows):
    return jnp.concatenate([lo, hi], axis=lo.ndim - 1 if _halves_by_cols(rows) else lo.ndim - 2)


def gather_two_level(name, arrays, collective_id):
    nw = len(arrays)
    shapes = [a.shape for a in arrays]

    def body(*refs):
        ins, outs = refs[:nw], refs[nw:2 * nw]
        send, recv, loc = refs[2 * nw:]
        x, y, c = _my_place()
        q = 2 * x + y
        me, sibling = (x, y, c), (x, y, 1 - c)
        peers = [(1 - x, y), (x, 1 - y), (1 - x, 1 - y)]
        chips = [2 * px + py for px, py in peers]
        barrier = pltpu.get_barrier_semaphore()
        for dev in [sibling] + [(*p, c) for p in peers]:
            pl.semaphore_signal(barrier, inc=1, device_id=dev, device_id_type=MESH_ID)
        pl.semaphore_wait(barrier, 4)

        def mine(w):
            return _half_of(ins[w], shapes[w], c)

        def landed(w, chip, half):
            return _half_of(outs[w].at[chip], shapes[w], half)

        def copy(w, k, src, chip, half, to):
            return pltpu.make_async_remote_copy(
                src_ref=src, dst_ref=landed(w, chip, half), send_sem=send.at[w, k], recv_sem=recv.at[w, k],
                device_id=to, device_id_type=MESH_ID)

        first = [[copy(w, 0, mine(w), q, c, sibling)] + [copy(w, 1 + j, mine(w), q, c, (*peers[j], c)) for j in range(3)]
                 for w in range(nw)]
        own = [pltpu.make_async_copy(mine(w), landed(w, q, c), loc.at[w]) for w in range(nw)]
        for w in range(nw):
            for cp in first[w]:
                cp.start()
            own[w].start()
        passed = []
        for w in range(nw):
            for j in range(3):
                copy(w, 1 + j, mine(w), chips[j], c, me).wait_recv()
                fwd = copy(w, 4 + j, landed(w, chips[j], c), chips[j], c, sibling)
                fwd.start()
                passed.append(fwd)
        for w in range(nw):
            copy(w, 0, mine(w), q, 1 - c, me).wait_recv()
            for j in range(3):
                copy(w, 4 + j, mine(w), chips[j], 1 - c, me).wait_recv()
        for w in range(nw):
            for cp in first[w]:
                cp.wait_send()
            own[w].wait()
        for cp in passed:
            cp.wait_send()

    out_type = [jax.ShapeDtypeStruct((4,) + a.shape, a.dtype) for a in arrays]
    return pl.kernel(
        body, out_type=out_type, mesh=plsc.ScalarSubcoreMesh(axis_name="sequencer", num_cores=1), name=name,
        scratch_types=[pltpu.SemaphoreType.DMA((nw, 7)), pltpu.SemaphoreType.DMA((nw, 7)), pltpu.SemaphoreType.DMA((nw,))],
        compiler_params=pltpu.CompilerParams(collective_id=collective_id),
    )(*arrays)


def core_swap(name, arrays):
    nw = len(arrays)
    ANY = pl.BlockSpec(memory_space=pl.ANY)

    def body(*refs):
        ins, outs = refs[:nw], refs[nw:2 * nw]
        send, recv = refs[2 * nw:]
        x, y, c = _my_place()
        copies = [pltpu.make_async_remote_copy(
            src_ref=ins[w], dst_ref=outs[w], send_sem=send.at[w], recv_sem=recv.at[w],
            device_id=(x, y, 1 - c), device_id_type=MESH_ID) for w in range(nw)]
        for cp in copies:
            cp.start()
        for cp in copies:
            cp.wait_send()
            cp.wait_recv()

    return pl.pallas_call(
        body, name=name, in_specs=[ANY] * nw, out_specs=[ANY] * nw,
        out_shape=[jax.ShapeDtypeStruct(a.shape, a.dtype) for a in arrays],
        scratch_shapes=[pltpu.SemaphoreType.DMA((nw,)), pltpu.SemaphoreType.DMA((nw,))],
        compiler_params=pltpu.CompilerParams(has_side_effects=True),
    )(*arrays)


def all_reduce_small(name, v):
    rows = v.shape[0]
    VM = pl.BlockSpec(memory_space=pltpu.VMEM)

    def body(v_ref, o_ref, buf, send, recv):
        x, y, c = _my_place()
        me = 4 * x + 2 * y + c

        def peer(kx):
            return (x ^ ((kx >> 2) & 1), y ^ ((kx >> 1) & 1), c ^ (kx & 1))

        def copy(kx, slot):
            return pltpu.make_async_remote_copy(
                src_ref=v_ref, dst_ref=buf.at[slot], send_sem=send.at[kx - 1], recv_sem=recv.at[kx - 1],
                device_id=peer(kx), device_id_type=MESH_ID)

        sends = [copy(kx, me) for kx in range(1, 8)]
        for cp in sends:
            cp.start()
        buf[me] = v_ref[...]
        for kx in range(1, 8):
            copy(kx, me ^ kx).wait_recv()
        for cp in sends:
            cp.wait_send()
        acc = buf[0]
        for d in range(1, 8):
            acc = acc + buf[d]
        o_ref[...] = acc

    return pl.pallas_call(
        body, name=name, in_specs=[VM], out_specs=VM, out_shape=jax.ShapeDtypeStruct(v.shape, F32),
        scratch_shapes=[pltpu.VMEM((8, rows, LANES), F32), pltpu.SemaphoreType.DMA((7,)), pltpu.SemaphoreType.DMA((7,))],
        compiler_params=pltpu.CompilerParams(has_side_effects=True, vmem_limit_bytes=VMEM_LIMIT),
    )(v)


def _pad_cols(a, n):
    return jnp.pad(a, ((0, 0), (0, n - a.shape[1])))


def _pad_rows(a, n):
    return jnp.pad(a, ((0, n - a.shape[0]), (0, 0)))


def _halo(u, tr):
    t, cdim = u.shape
    tails = u.reshape(t // tr, tr, cdim)[:, tr - HALO:, :]
    tails = jnp.concatenate([jnp.zeros((1, HALO, cdim), u.dtype), tails[:-1]], axis=0)
    return tails.reshape(-1, cdim)


def _unhalo(du, dhalo, tr):
    t, cdim = du.shape
    n = t // tr
    dh = dhalo.reshape(n, HALO, cdim)
    dh = jnp.concatenate([dh[1:], jnp.zeros((1, HALO, cdim), du.dtype)], axis=0)
    d3 = du.reshape(n, tr, cdim)
    d3 = jnp.concatenate([d3[:, :tr - HALO, :], d3[:, tr - HALO:, :] + dh], axis=1)
    return d3.reshape(t, cdim)


def _to_slots(g, axis):
    r, cdim = g.shape
    if axis == 0:
        return g.reshape(4, r // 4, cdim)
    return g.reshape(r, 4, cdim // 4).transpose(1, 0, 2)


def _from_slots(s, axis):
    if axis == 0:
        return s.reshape(s.shape[0] * s.shape[1], s.shape[2])
    return s.transpose(1, 0, 2).reshape(s.shape[1], 4 * s.shape[2])


BIG = ("w_in", "w_out", "xattn_wq", "xattn_wk", "xattn_wv", "xattn_wo", "ffn_w1", "ffn_w2")
TRANSPOSED = ("w_in",)
BIG_AXIS = {"w_in": 0, "w_out": 0, "xattn_wq": 0, "xattn_wk": 0, "xattn_wv": 0, "xattn_wo": 0, "ffn_w1": 1, "ffn_w2": 0}
SMALL_SHARDED = ("ssd_conv_w", "rwkv_w2", "rwkv_a2", "rwkv_g2")
GATHER_GROUPS = (("w_in",), ("w_out", "xattn_wq", "xattn_wk", "xattn_wv", "xattn_wo"), ("ffn_w1", "ffn_w2"))
REDUCE_GROUPS = (("ffn_w2", "ffn_w1"), ("xattn_wo", "xattn_wq", "xattn_wk", "xattn_wv", "w_out"),
                 ("rwkv_w2", "rwkv_a2", "rwkv_g2", "w_in"))
REDUCED = BIG + ("rwkv_w2", "rwkv_a2", "rwkv_g2")
REDUCE_AXIS = dict(BIG_AXIS, rwkv_w2=1, rwkv_a2=1, rwkv_g2=1)
WEIGHTS = ("norm_mix_g", "w_in", "ssd_conv_w", "ssd_conv_b", "ssd_dt_bias", "ssd_a_log", "ssd_d", "ssd_norm_g",
           "rwkv_mu", "rwkv_w0", "rwkv_w2", "rwkv_a0", "rwkv_a2", "rwkv_g2", "rwkv_k_k", "rwkv_k_a", "rwkv_r_k",
           "rwkv_ln_w", "rwkv_ln_b", "w_out", "norm_x_g", "norm_mem_g", "xattn_wq", "xattn_wk", "xattn_wv", "xattn_wo",
           "norm_ffn_g", "ffn_w1", "ffn_w2", "final_norm_g")


def _local_grads(x, mem, tgt, wt, full, big, reducer):
    t, d = x.shape
    w = d // 2
    nh = w // HEAD_DIM
    n_pairs = nh // 2
    ppg = n_pairs // SSD_GROUPS
    bc = SSD_GROUPS * SSD_STATE
    conv_dim = w + 2 * bc
    tr = ROW_TILE
    nt = t // tr
    tr2 = 2 * tr if t % (2 * tr) == 0 else tr
    nt2 = t // tr2
    dr = wt["rwkv_w2"].shape[0]
    ar = wt["rwkv_a2"].shape[0]
    gr = wt["rwkv_g2"].shape[0]

    big.start(0, None)
    big.start(1, None)
    h1 = norm_fwd("norm_mix", x, wt["norm_mix_g"], tr2)
    w_in_t = big.get("w_in", (h1, full))
    o = 0
    segs = {}
    for nm, width in (("z", w), ("xbc", conv_dim), ("dt", nh), ("rkv", 3 * w), ("pw", dr), ("pa", ar), ("pg", gr)):
        segs[nm] = (o, width)
        o += width
    padded = {"z": w, "xbc": conv_dim, "dt": LANES, "rkv": 3 * w, "pw": LANES, "pa": LANES, "pg": gr}
    order = ("z", "xbc", "dt", "rkv", "pw", "pa", "pg")
    w_segs = [jnp.concatenate([_pad_rows(w_in_t[segs[nm][0]:segs[nm][0] + segs[nm][1]], padded[nm]) for nm in grp], axis=0)
              for grp in (("z",), ("xbc",), ("dt",), ("rkv",), ("pw", "pa", "pg"))]
    w_perm_t = jnp.concatenate(w_segs, axis=0)
    offs = {}
    o = 0
    for nm in order:
        offs[nm] = o
        o += padded[nm]
    lora_w = 2 * LANES + gr

    mu = wt["rwkv_mu"]
    mo = 3 * w
    mu_rkv = mu[:, :mo]
    mu_lora = jnp.concatenate([_pad_cols(mu[:, mo:mo + dr], LANES), _pad_cols(mu[:, mo + dr:mo + dr + ar], LANES),
                               mu[:, mo + dr + ar:]], axis=1)
    w2p = _pad_rows(full["rwkv_w2"], LANES)
    a2p = _pad_rows(full["rwkv_a2"], LANES)
    g2 = full["rwkv_g2"]
    conv_w = full["ssd_conv_w"]
    cw = [conv_w[i:i + 1] for i in range(SSD_CONV)]
    dt_bias = _pad_cols(wt["ssd_dt_bias"], LANES)
    a_log = _pad_cols(wt["ssd_a_log"], LANES)
    d_skip = _pad_cols(wt["ssd_d"], LANES)
    r_k = wt["rwkv_r_k"].reshape(1, w)

    z, xbc, dtraw, urkv, ulora = [matmul("in_proj_%d" % i, h1, ws, tb=True) for i, ws in enumerate(w_segs)]
    big.start(2, urkv)

    halo_xbc = _halo(xbc, tr)
    ssd_pre_t = [(xbc, tr, conv_dim, 0), (halo_xbc, HALO, conv_dim, 0), (dtraw, tr, LANES, 0)]
    ssd_pre_f = cw + [wt["ssd_conv_b"], dt_bias]
    act, dt = fn_fwd("ssd_pre", _ssd_pre, nt, ssd_pre_t, ssd_pre_f, [(t, tr, conv_dim, F32), (t, tr, LANES, F32)])

    nb = w // LANES
    ssd_seq = [(act, None), (act, lambda p: nb + p // ppg), (act, lambda p: nb + SSD_GROUPS + p // ppg), (dt, lambda p: 0)]
    ssd_ppb = min(ppg, PAIRS_PER_STEP)
    rw_ppb = min(n_pairs, 2 * PAIRS_PER_STEP)

    def ssd_fn(sv, cv, hts, ids):
        return _ssd_chunks([(*s, ht) for s, ht in zip(sv, hts)], cv[0], ids)

    y_scan, ssd_states = scan_fwd("ssd_scan", ssd_fn, SSD_CHUNK, ssd_seq, [a_log], n_pairs, ssd_ppb)
    ssd_post_t = [(y_scan, tr2, w, 0), (act, tr2, w, 0), (z, tr2, w, 0)]
    ssd_post_f = [d_skip, wt["ssd_norm_g"]]
    (y_ssd,) = fn_fwd("ssd_post", _ssd_post, nt2, ssd_post_t, ssd_post_f, [(t, tr2, w, BF16)])

    halo_rkv, halo_lora = _halo(urkv, tr), _halo(ulora, tr)
    rw_pre_t = [(urkv, tr, 3 * w, 0), (ulora, tr, lora_w, 0), (halo_rkv, HALO, 3 * w, 0), (halo_lora, HALO, lora_w, 0)]
    rw_pre_f = [mu_rkv, mu_lora, wt["rwkv_w0"], wt["rwkv_a0"], wt["rwkv_k_k"], wt["rwkv_k_a"], w2p, a2p, g2]
    rw = fn_fwd("rwkv_pre", _rwkv_pre, nt, rw_pre_t, rw_pre_f, [(t, tr, w, F32)] * 7)
    r_, lw_, k2_, v_, nkk_, b_, gate_ = rw
    rw_seq = [(a, None) for a in (r_, lw_, k2_, v_, nkk_, b_)]

    def rw_fn(sv, cv, hts, ids):
        return _rwkv_chunks([(*s, ht) for s, ht in zip(sv, hts)])

    yr_scan, rw_states = scan_fwd("rwkv_scan", rw_fn, RWKV_CHUNK, rw_seq, [], n_pairs, rw_ppb)
    rw_post_t = [(a, tr2, w, 0) for a in (yr_scan, r_, k2_, v_, gate_)]
    rw_post_f = [r_k, wt["rwkv_ln_w"], wt["rwkv_ln_b"]]
    (y_rwkv,) = fn_fwd("rwkv_post", _rwkv_post, nt2, rw_post_t, rw_post_f, [(t, tr2, w, BF16)])

    ymix = jnp.concatenate([y_ssd, y_rwkv], axis=1)
    w_out = big.get("w_out", ymix)
    x1 = matmul("out_proj", ymix, w_out, resid=x)

    h2 = norm_fwd("norm_x", x1, wt["norm_x_g"], tr2)
    mrows = mem.shape[0]
    mn = norm_fwd("norm_mem", mem, wt["norm_mem_g"], mrows)
    wq, wk, wv, wo = [big.get(nm, ymix) for nm in ("xattn_wq", "xattn_wk", "xattn_wv", "xattn_wo")]
    q = matmul("xattn_q", h2, wq)
    kx = matmul("xattn_k", mn, wk)
    vx = matmul("xattn_v", mn, wv)
    (ao,) = fn_fwd("xattn_core", _attn, nt2, [(q, tr2, d, 0)], [kx, vx], [(t, tr2, d, BF16)])
    x2 = matmul("xattn_o", ao, wo, resid=x1)

    h3 = norm_fwd("norm_ffn", x2, wt["norm_ffn_g"], tr2)
    w1, w2 = big.get("ffn_w1", h3), big.get("ffn_w2", h3)
    a1 = matmul("ffn_up", h3, w1, out_dtype=BF16)
    dff = a1.shape[1]
    (f1,) = fn_fwd("ffn_act", _relu2, nt, [(a1, tr, dff, 0)], [], [(t, tr, dff, BF16)])
    x3 = matmul("ffn_down", f1, w2, resid=x2)

    dx3, dx3b, g_final, loss_tile = loss_head(x3, tgt, wt["final_norm_g"].reshape(1, d), tr2)

    grads = {"final_norm_g": g_final.reshape(d)}
    grads["ffn_w2"] = matmul("ffn_down_dw", f1, dx3b, ta=True)
    df1 = matmul("ffn_down_dx", dx3b, w2, tb=True, out_dtype=BF16)
    (da1,), _ = fn_bwd("ffn_act_bwd", _relu2, nt, [(a1, tr, dff, 0)], [], [(df1, tr, dff, 0)], lambda c: [c[0].astype(F32)],
                       [(t, tr, dff, BF16)])
    grads["ffn_w1"] = matmul("ffn_up_dw", h3, da1, out_slots=4, ta=True)
    dh3 = reducer.launch(0, grads, matmul("ffn_up_dx", da1, w1, tb=True))
    dx2, dx2b, grads["norm_ffn_g"] = norm_bwd("norm_ffn_bwd", x2, wt["norm_ffn_g"], dh3, dx3, tr2)

    grads["xattn_wo"] = matmul("xattn_o_dw", ao, dx2b, ta=True)
    dao = matmul("xattn_o_dx", dx2b, wo, tb=True)
    (dq,), (dkx, dvx) = fn_bwd("xattn_core_bwd", _attn, nt2, [(q, tr2, d, 0)], [kx, vx], [(dao, tr2, d, 0)], lambda c: c,
                               [(t, tr2, d, BF16)])
    grads["xattn_wq"] = matmul("xattn_q_dw", h2, dq, ta=True)
    dh2 = matmul("xattn_q_dx", dq, wq, tb=True)
    dkb, dvb = dkx.astype(BF16), dvx.astype(BF16)
    grads["xattn_wk"] = matmul("xattn_k_dw", mn, dkb, ta=True)
    grads["xattn_wv"] = matmul("xattn_v_dw", mn, dvb, ta=True)
    dmn = matmul("xattn_k_dx", dkb, wk, tb=True)
    dmn = matmul("xattn_v_dx", dvb, wv, tb=True, resid=dmn)
    _, _, grads["norm_mem_g"] = norm_bwd("norm_mem_bwd", mem, wt["norm_mem_g"], dmn, None, mrows)
    dx1, dx1b, grads["norm_x_g"] = norm_bwd("norm_x_bwd", x1, wt["norm_x_g"], dh2, dx2, tr2)

    grads["w_out"] = matmul("out_proj_dw", ymix, dx1b, ta=True)
    dymix = reducer.launch(1, grads, matmul("out_proj_dx", dx1b, w_out, tb=True))

    (dyr, dr1, dk1, dv1, dgate), (g_rk, grads["rwkv_ln_w"], grads["rwkv_ln_b"]) = fn_bwd(
        "rwkv_post_bwd", _rwkv_post, nt2, rw_post_t, rw_post_f, [(dymix, tr2, w, 1)], lambda c: c, [(t, tr2, w, F32)] * 5)
    grads["rwkv_r_k"] = g_rk.reshape(wt["rwkv_r_k"].shape)
    (dr2, dlw, dk2, dv2, dnkk, db), _ = scan_bwd("rwkv_scan_bwd", rw_fn, RWKV_CHUNK, rw_seq, [], rw_states, dyr, n_pairs, rw_ppb)
    rw_ct = [(a, tr, w, 0) for a in (dr1, dr2, dlw, dk1, dk2, dv1, dv2, dnkk, db, dgate)]

    def rw_ct_fn(c):
        return (c[0] + c[1], c[2], c[3] + c[4], c[5] + c[6], c[7], c[8], c[9])

    (durkv, dulora, dhrkv, dhlora), rw_pg = fn_bwd(
        "rwkv_pre_bwd", _rwkv_pre, nt, rw_pre_t, rw_pre_f, rw_ct, rw_ct_fn,
        [(t, tr, 3 * w, F32), (t, tr, lora_w, F32), (nt * HALO, HALO, 3 * w, F32), (nt * HALO, HALO, lora_w, F32)])
    durkv = _unhalo(durkv, dhrkv, tr)
    dulora = _unhalo(dulora, dhlora, tr)
    g_mu_rkv, g_mu_lora, grads["rwkv_w0"], grads["rwkv_a0"], grads["rwkv_k_k"], grads["rwkv_k_a"], g_w2p, g_a2p, grads["rwkv_g2"] = rw_pg
    grads["rwkv_mu"] = jnp.concatenate([g_mu_rkv, g_mu_lora[:, :dr], g_mu_lora[:, LANES:LANES + ar], g_mu_lora[:, 2 * LANES:]], axis=1)
    grads["rwkv_w2"] = g_w2p[:dr]
    grads["rwkv_a2"] = g_a2p[:ar]

    (dys, dxs1, dz), (g_d, grads["ssd_norm_g"]) = fn_bwd(
        "ssd_post_bwd", _ssd_post, nt2, ssd_post_t, ssd_post_f, [(dymix, tr2, w, 0)], lambda c: c, [(t, tr2, w, F32)] * 3)
    grads["ssd_d"] = g_d[:, :nh]
    (dxs2, dbp, dcp, ddtp), (g_alog,) = scan_bwd("ssd_scan_bwd", ssd_fn, SSD_CHUNK, ssd_seq, [a_log], ssd_states, dys, n_pairs, ssd_ppb)
    grads["ssd_a_log"] = g_alog[:, :nh]
    ssd_ct = [(dxs1, tr, w, 0), (dxs2, tr, w, 0), (dbp, tr, w, 0), (dcp, tr, w, 0), (ddtp, tr, w, 0)]

    def ssd_ct_fn(c):
        def group_sum(a):
            parts = []
            for gi in range(SSD_GROUPS):
                s = a[:, gi * ppg * LANES:(gi * ppg + 1) * LANES]
                for j in range(1, ppg):
                    s = s + a[:, (gi * ppg + j) * LANES:(gi * ppg + j + 1) * LANES]
                parts.append(s)
            return parts
        ddt = c[4][:, :LANES]
        for j in range(1, n_pairs):
            ddt = ddt + c[4][:, j * LANES:(j + 1) * LANES]
        return (jnp.concatenate([c[0] + c[1]] + group_sum(c[2]) + group_sum(c[3]), axis=1), ddt)

    (dxbc, dhxbc, ddtraw), ssd_pg = fn_bwd(
        "ssd_pre_bwd", _ssd_pre, nt, ssd_pre_t, ssd_pre_f, ssd_ct, ssd_ct_fn,
        [(t, tr, conv_dim, F32), (nt * HALO, HALO, conv_dim, F32), (t, tr, LANES, F32)])
    dxbc = _unhalo(dxbc, dhxbc, tr)
    grads["ssd_conv_w"] = jnp.concatenate(ssd_pg[:SSD_CONV], axis=0)
    grads["ssd_conv_b"] = ssd_pg[SSD_CONV]
    grads["ssd_dt_bias"] = ssd_pg[SSD_CONV + 1][:, :nh]

    du = jnp.concatenate([dz, dxbc, ddtraw, durkv, dulora], axis=1).astype(BF16)
    g_perm_t = matmul("in_proj_dw", du, h1, ta=True)
    grads["w_in"] = jnp.concatenate([g_perm_t[offs[nm]:offs[nm] + segs[nm][1]] for nm in order], axis=0)
    dh1 = matmul("in_proj_dx", du, w_perm_t)
    dh1 = reducer.launch(2, grads, dh1)
    grad_x, _, grads["norm_mix_g"] = norm_bwd("norm_mix_bwd", x, wt["norm_mix_g"], dh1, dx1, tr2)
    return loss_tile, grad_x, grads


def _pack(arrs):
    flat = jnp.concatenate([a.reshape(-1) for a in arrs])
    n = flat.shape[0]
    rows = -(-n // (8 * LANES)) * 8
    return jnp.pad(flat, (0, rows * LANES - n)).reshape(rows, LANES)


def _unpack(packed, shapes):
    flat = packed.reshape(-1)
    out, o = [], 0
    for s in shapes:
        n = math.prod(s)
        out.append(flat[o:o + n].reshape(s))
        o += n
    return out


def _as2d(a):
    return a.reshape(-1, a.shape[-1])


def _shard_view(n, a):
    return _as2d(a[0]).T if n in TRANSPOSED else _as2d(a[0])


class _GatheredWeights:
    def __init__(self, shard2d, q, c):
        self.shard2d, self.q, self.c = shard2d, q, c
        self.raw, self.ready = {}, {}

    def start(self, gi, after):
        shards = [self.shard2d[n].astype(BF16) for n in GATHER_GROUPS[gi]]
        if after is not None:
            shards, _ = lax.optimization_barrier((shards, after))
        gathered = gather_two_level("gather_weights_%d" % gi, shards, gi + 1)
        self.raw.update(zip(GATHER_GROUPS[gi], gathered))

    def get(self, name, after):
        if name not in self.ready:
            g = self.raw[name]
            if after is not None:
                g, _ = lax.optimization_barrier((g, after))
            self.ready[name] = _from_slots(g, 0) if BIG_AXIS[name] == 0 else g
        return self.ready[name]


class _GradReducer:
    def __init__(self, q, c, update):
        self.q, self.c, self.update = q, c, update
        self.pending, self.updated = {}, {}

    def launch(self, gi, grads, nxt):
        names = REDUCE_GROUPS[gi]
        slots = [grads[n] if grads[n].ndim == 3 else _to_slots(grads[n], REDUCE_AXIS[n]) for n in names]
        rows = [s.shape[1] for s in slots]
        sent = [half_call("send_half_" + n, s, 1 - self.c, None, BF16) for n, s in zip(names, slots)]
        got = core_swap("swap_halves_%d" % gi, sent)
        parts = [half_call("chip_sum_" + n, s, self.c, g, BF16) for n, s, g in zip(names, slots, got)]
        parts, nxt = lax.optimization_barrier((parts, nxt))
        slots = scatter_slots("scatter_grads_%d" % gi, parts, len(GATHER_GROUPS) + 1 + gi)
        self.pending[gi] = (slots, rows)
        return self.finish(gi - 1, nxt) if gi > 0 else nxt

    def finish(self, gi, nxt):
        names = REDUCE_GROUPS[gi]
        slots, rows = self.pending[gi]
        halves = []
        for n, s in zip(names, slots):
            halves.append(sum_slots("sum_" + n, s))
        others = core_swap("swap_reduced_%d" % gi, halves)
        lo = [jnp.where(self.c == 0, mine, other) for mine, other in zip(halves, others)]
        hi = [jnp.where(self.c == 0, other, mine) for mine, other in zip(halves, others)]
        results = [self.update(n, _join_halves(l, h, r)) for n, l, h, r in zip(names, lo, hi, rows)]
        if nxt is not None:
            results, nxt = lax.optimization_barrier((results, nxt))
        self.updated.update(zip(names, results))
        return nxt


def _step(a):
    x, mem, tgt = a["x"][0], a["mem"][0], a["loss_target"][0]
    q = 2 * lax.axis_index("x") + lax.axis_index("y")

    shard2d = {n: _shard_view(n, a[n]) for n in BIG}
    small_sh = {n: _as2d(a[n][0]) for n in SMALL_SHARDED}
    c = lax.axis_index("c")
    full = {}
    big = _GatheredWeights(shard2d, q, c)
    gathered = gather_shards("gather_small", [small_sh[n] for n in SMALL_SHARDED])
    for n, g in zip(SMALL_SHARDED, gathered):
        full[n] = _from_slots(g, 1)

    wt = {n: (a[n] if a[n].ndim <= 2 else a[n][0]) for n in WEIGHTS if n not in BIG and n not in SMALL_SHARDED}
    for n in SMALL_SHARDED:
        wt[n] = small_sh[n]
    shards = dict(shard2d)
    shards.update({n: small_sh[n] for n in REDUCED if n not in BIG})

    def update(n, gsum):
        return adamw("adamw_" + n, shards[n], _shard_view(n, a["m_" + n]), _shard_view(n, a["v_" + n]), gsum)

    reducer = _GradReducer(q, c, update)
    loss_tile, grad_x, grads = _local_grads(x, mem, tgt, wt, full, big, reducer)
    reducer.finish(len(REDUCE_GROUPS) - 1, None)
    out = {}
    for n, vals in reducer.updated.items():
        for key, val in zip(("grad_", "delta_", "new_m_", "new_v_"), vals):
            out[key + n] = (val.T if n in TRANSPOSED else val).reshape(a[n].shape)

    small = [n for n in WEIGHTS if n not in REDUCED]
    red = _unpack(all_reduce_small("all_reduce_small", _pack([grads[n] for n in small])), [grads[n].shape for n in small])
    g_loc = {}
    for n, g in zip(small, red):
        if n in SMALL_SHARDED:
            cols = g.shape[1] // 4
            g = lax.dynamic_slice_in_dim(g, q * cols, cols, axis=1)
        g_loc[n] = g.reshape(a[n].shape)
    res = adamw("adamw_small", *[_pack([src[n] for n in small]) for src in
                                 ({n: a[n] for n in small}, {n: a["m_" + n] for n in small}, {n: a["v_" + n] for n in small})],
                _pack([g_loc[n] for n in small]))
    shapes = [a[n].shape for n in small]
    for key, packed in zip(("grad_", "delta_", "new_m_", "new_v_"), res):
        for n, val in zip(small, _unpack(packed, shapes)):
            out[key + n] = val

    loss = lax.psum(loss_tile[0, 0], ("x", "y", "c"))
    ordered = [loss, grad_x.reshape(a["x"].shape)]
    for key in ("grad_", "delta_", "new_m_", "new_v_"):
        ordered += [out[key + n] for n in WEIGHTS]
    return tuple(ordered)


def kernel(x, mem, norm_mix_g, w_in, ssd_conv_w, ssd_conv_b, ssd_dt_bias, ssd_a_log, ssd_d, ssd_norm_g, rwkv_mu, rwkv_w0, rwkv_w2, rwkv_a0, rwkv_a2, rwkv_g2, rwkv_k_k, rwkv_k_a, rwkv_r_k, rwkv_ln_w, rwkv_ln_b, w_out, norm_x_g, norm_mem_g, xattn_wq, xattn_wk, xattn_wv, xattn_wo, norm_ffn_g, ffn_w1, ffn_w2, final_norm_g, loss_target, m_norm_mix_g, m_w_in, m_ssd_conv_w, m_ssd_conv_b, m_ssd_dt_bias, m_ssd_a_log, m_ssd_d, m_ssd_norm_g, m_rwkv_mu, m_rwkv_w0, m_rwkv_w2, m_rwkv_a0, m_rwkv_a2, m_rwkv_g2, m_rwkv_k_k, m_rwkv_k_a, m_rwkv_r_k, m_rwkv_ln_w, m_rwkv_ln_b, m_w_out, m_norm_x_g, m_norm_mem_g, m_xattn_wq, m_xattn_wk, m_xattn_wv, m_xattn_wo, m_norm_ffn_g, m_ffn_w1, m_ffn_w2, m_final_norm_g, v_norm_mix_g, v_w_in, v_ssd_conv_w, v_ssd_conv_b, v_ssd_dt_bias, v_ssd_a_log, v_ssd_d, v_ssd_norm_g, v_rwkv_mu, v_rwkv_w0, v_rwkv_w2, v_rwkv_a0, v_rwkv_a2, v_rwkv_g2, v_rwkv_k_k, v_rwkv_k_a, v_rwkv_r_k, v_rwkv_ln_w, v_rwkv_ln_b, v_w_out, v_norm_x_g, v_norm_mem_g, v_xattn_wq, v_xattn_wk, v_xattn_wv, v_xattn_wo, v_norm_ffn_g, v_ffn_w1, v_ffn_w2, v_final_norm_g):
    return _step(dict(locals()))
```
